```python
import jax, jax.numpy as jnp
from jax import lax
import numpy as np

D_MODEL = 1024
BATCH = 8
SEQ = 4096
DEPTH = 1

HEAD_DIM = 64
RET_HEADS = 8
NSA_HEADS = 8
NSA_KV_HEADS = 2
NSA_GROUP = NSA_HEADS // NSA_KV_HEADS
D_RET = RET_HEADS * HEAD_DIM
D_NSA = NSA_HEADS * HEAD_DIM
D_MIX = D_RET + D_NSA
D_KV = NSA_KV_HEADS * HEAD_DIM
N_BRANCH = 3
RET_CHUNK = 128
ROPE_THETA = 10000.0
CMP_BLOCK = 32
CMP_STRIDE = 16
CMP_HIDDEN = 256
SLC_BLOCK = 64
SLC_TOPK = 16
SLC_QBLOCK = 64
WIN_SIZE = 512
WIN_QBLOCK = 128
EPS = 1e-6
NEG = -1e30
FORCE_BONUS = 1e4

IN_WIDTHS = (D_RET, D_RET, D_RET, D_RET,
             D_NSA, D_NSA,
             D_KV, D_KV, D_KV, D_KV, D_KV, D_KV,
             N_BRANCH * NSA_HEADS)
D_IN = sum(IN_WIDTHS)
SPLIT_POINTS = [int(s) for s in np.cumsum(IN_WIDTHS)[:-1]]

kernel_name = "hymba_retnet_nsa_layer"


def rms_norm(x, w):
    xf = x.astype(jnp.float32)
    y = xf * lax.rsqrt(jnp.mean(xf * xf, axis=-1, keepdims=True) + EPS)
    return (y * w.astype(jnp.float32)).astype(x.dtype)


def rope(x):
    T = x.shape[2]
    half = HEAD_DIM // 2
    inv = ROPE_THETA ** (-jnp.arange(half, dtype=jnp.float32) / half)
    ang = jnp.arange(T, dtype=jnp.float32)[:, None] * inv[None, :]
    cos, sin = jnp.cos(ang), jnp.sin(ang)
    x1, x2 = x[..., :half], x[..., half:]
    return jnp.concatenate([x1 * cos - x2 * sin, x1 * sin + x2 * cos], axis=-1)


def retention(q, k, v):
    B, H, T, d = q.shape
    C = RET_CHUNK
    N = T // C
    f32 = jnp.float32
    log_g = jnp.log1p(-jnp.exp2(-5.0 - jnp.arange(H, dtype=f32)))
    qc = rope(q.astype(f32)).reshape(B, H, N, C, d)
    kc = (rope(k.astype(f32)) * d ** -0.5).reshape(B, H, N, C, d)
    vc = v.astype(f32).reshape(B, H, N, C, d)
    pos = jnp.arange(C, dtype=f32)
    diff = pos[:, None] - pos[None, :]
    decay = jnp.where(diff >= 0, jnp.exp(log_g[:, None, None] * jnp.maximum(diff, 0.0)), 0.0)
    s = jnp.einsum('bhncd,bhnsd->bhncs', qc, kc) * decay[None, :, None]
    o_inner = jnp.einsum('bhncs,bhnse->bhnce', s, vc)
    zeta = jnp.exp(log_g[:, None] * (C - 1.0 - pos))
    kv = jnp.einsum('bhnsd,bhnse->nbhde', kc, vc * zeta[None, :, None, :, None])
    g_chunk = jnp.exp(log_g * C)[None, :, None, None]

    def step(R, kv_i):
        return R * g_chunk + kv_i, R

    _, R_prev = lax.scan(step, jnp.zeros((B, H, d, d), f32), kv)
    xi = jnp.exp(log_g[:, None] * (pos + 1.0))
    o_cross = jnp.einsum('bhncd,nbhde->bhnce', qc, R_prev) * xi[None, :, None, :, None]
    return (o_inner + o_cross).reshape(B, H, T, d)


def compress(kv, pos_emb, w1, w2):
    B, G, T, d = kv.shape
    n_cmp = (T - CMP_BLOCK) // CMP_STRIDE + 1
    idx = np.arange(n_cmp)[:, None] * CMP_STRIDE + np.arange(CMP_BLOCK)[None, :]
    blocks = kv[:, :, idx] + pos_emb
    flat = blocks.reshape(B, G, n_cmp, CMP_BLOCK * d)
    return jax.nn.silu(flat @ w1) @ w2


def compressed_branch(q, kc_raw, vc_raw, k_norm, pos_k, w1_k, w2_k, pos_v, w1_v, w2_v):
    B, G, R, T, d = q.shape
    kc = rms_norm(compress(kc_raw, pos_k, w1_k, w2_k), k_norm)
    vc = compress(vc_raw, pos_v, w1_v, w2_v)
    n_cmp = kc.shape[2]
    block_end = np.arange(n_cmp) * CMP_STRIDE + CMP_BLOCK - 1
    valid = jnp.asarray(block_end[None, :] <= np.arange(T)[:, None])
    s = jnp.einsum('bgrtd,bgnd->bgrtn', q, kc).astype(jnp.float32) * d ** -0.5
    p = jax.nn.softmax(jnp.where(valid, s, NEG), axis=-1)
    p = jnp.where(valid, p, 0.0)
    o = jnp.einsum('bgrtn,bgnd->bgrtd', p.astype(vc.dtype), vc)
    return o, p


def overlap_matrix(T):
    n_cmp = (T - CMP_BLOCK) // CMP_STRIDE + 1
    pos = np.arange(n_cmp)[:, None] * CMP_STRIDE + np.arange(CMP_BLOCK)[None, :]
    blk = pos // SLC_BLOCK
    return (blk[:, :, None] == np.arange(T // SLC_BLOCK)[None, None, :]).mean(axis=1).astype(np.float32)


def selected_branch(q, ks, vs, p_cmp):
    B, G, R, T, d = q.shape
    n_slc = T // SLC_BLOCK
    top = min(SLC_TOPK, n_slc)
    M = jnp.asarray(overlap_matrix(T))
    imp = jnp.einsum('bgtn,ns->bgts', p_cmp.sum(axis=2), M)
    t = jnp.arange(T)
    blk = jnp.arange(n_slc)
    valid = blk[None, :] * SLC_BLOCK <= t[:, None]
    force = (blk[None, :] == (t // SLC_BLOCK)[:, None]) | (blk[None, :] == 0)
    score = jnp.where(valid, jnp.where(force, imp + FORCE_BONUS, imp), NEG)
    _, idx = lax.top_k(score, top)
    ks_blk = ks.reshape(B, G, n_slc, SLC_BLOCK, d)
    vs_blk = vs.reshape(B, G, n_slc, SLC_BLOCK, d)
    nb = T // SLC_QBLOCK
    q_b = q.reshape(B, G, R, nb, SLC_QBLOCK, d).transpose(3, 0, 1, 2, 4, 5)
    idx_b = idx.reshape(B, G, nb, SLC_QBLOCK, top).transpose(2, 0, 1, 3, 4)
    t_b = t.reshape(nb, SLC_QBLOCK)
    gather = jax.vmap(jax.vmap(lambda arr, ib: arr[ib]))

    def block(args):
        qb, ib, tb = args
        flat = ib.reshape(B, G, SLC_QBLOCK * top)
        kg = gather(ks_blk, flat).reshape(B, G, SLC_QBLOCK, top, SLC_BLOCK, d)
        vg = gather(vs_blk, flat).reshape(B, G, SLC_QBLOCK, top, SLC_BLOCK, d)
        s = jnp.einsum('bgrqd,bgqkld->bgrqkl', qb, kg).astype(jnp.float32) * d ** -0.5
        key_pos = ib[..., None] * SLC_BLOCK + jnp.arange(SLC_BLOCK)
        mask = key_pos <= tb[None, None, :, None, None]
        s = jnp.where(mask[:, :, None], s, NEG).reshape(B, G, R, SLC_QBLOCK, top * SLC_BLOCK)
        p = jax.nn.softmax(s, axis=-1).reshape(B, G, R, SLC_QBLOCK, top, SLC_BLOCK)
        return jnp.einsum('bgrqkl,bgqkld->bgrqd', p.astype(vg.dtype), vg)

    o = lax.map(block, (q_b, idx_b, t_b))
    return o.transpose(1, 2, 3, 0, 4, 5).reshape(B, G, R, T, d)


def window_branch(q, kw, vw):
    B, G, R, T, d = q.shape
    nb = T // WIN_QBLOCK
    span = WIN_QBLOCK + WIN_SIZE
    idx = np.arange(nb)[:, None] * WIN_QBLOCK + np.arange(span)[None, :]
    pad = ((0, 0), (0, 0), (WIN_SIZE, 0), (0, 0))
    kp = jnp.pad(kw, pad)[:, :, idx]
    vp = jnp.pad(vw, pad)[:, :, idx]
    qb = q.reshape(B, G, R, nb, WIN_QBLOCK, d)
    tq = np.arange(nb)[:, None] * WIN_QBLOCK + np.arange(WIN_QBLOCK)[None, :]
    s_pos = idx - WIN_SIZE
    delta = tq[:, :, None] - s_pos[:, None, :]
    mask = jnp.asarray((delta >= 0) & (delta < WIN_SIZE) & (s_pos[:, None, :] >= 0))
    s = jnp.einsum('bgrnqd,bgnsd->bgrnqs', qb, kp).astype(jnp.float32) * d ** -0.5
    p = jax.nn.softmax(jnp.where(mask, s, NEG), axis=-1)
    o = jnp.einsum('bgrnqs,bgnsd->bgrnqd', p.astype(vp.dtype), vp)
    return o.reshape(B, G, R, T, d)


def hybrid_layer(x, norm_w, w_in, ret_norm_w, q_norm_w, k_norm_cmp, k_norm_slc, k_norm_win,
                 cmp_pos_k, cmp_w1_k, cmp_w2_k, cmp_pos_v, cmp_w1_v, cmp_w2_v, b_gate, w_out):
    B, T, _ = x.shape
    h = rms_norm(x, norm_w)
    proj = h @ w_in
    rq, rk, rv, rg, nq, ng, ck, cv, sk, sv, wk, wv, gl = jnp.split(proj, SPLIT_POINTS, axis=-1)

    def heads(t, n):
        return t.reshape(B, T, n, HEAD_DIM).transpose(0, 2, 1, 3)

    o_ret = retention(heads(rq, RET_HEADS), heads(rk, RET_HEADS), heads(rv, RET_HEADS))
    o_ret = rms_norm(o_ret.transpose(0, 2, 1, 3), ret_norm_w).astype(x.dtype)
    y_ret = o_ret.reshape(B, T, D_RET) * jax.nn.silu(rg)

    q = rms_norm(heads(nq, NSA_HEADS), q_norm_w).reshape(B, NSA_KV_HEADS, NSA_GROUP, T, HEAD_DIM)
    kc_raw, vc_raw = heads(ck, NSA_KV_HEADS), heads(cv, NSA_KV_HEADS)
    ks, vs = rms_norm(heads(sk, NSA_KV_HEADS), k_norm_slc), heads(sv, NSA_KV_HEADS)
    kw, vw = rms_norm(heads(wk, NSA_KV_HEADS), k_norm_win), heads(wv, NSA_KV_HEADS)
    o_cmp, p_cmp = compressed_branch(q, kc_raw, vc_raw, k_norm_cmp,
                                     cmp_pos_k, cmp_w1_k, cmp_w2_k, cmp_pos_v, cmp_w1_v, cmp_w2_v)
    o_slc = selected_branch(q, ks, vs, p_cmp)
    o_win = window_branch(q, kw, vw)
    gates = jax.nn.sigmoid((gl + b_gate).astype(jnp.float32))
    gates = gates.reshape(B, T, N_BRANCH, NSA_HEADS).transpose(2, 0, 3, 1)[..., None]
    shp = (B, NSA_HEADS, T, HEAD_DIM)
    o_nsa = (gates[0] * o_cmp.reshape(shp) + gates[1] * o_slc.reshape(shp)
             + gates[2] * o_win.reshape(shp)).astype(x.dtype)
    y_nsa = o_nsa.transpose(0, 2, 1, 3).reshape(B, T, D_NSA) * jax.nn.silu(ng)

    y = jnp.concatenate([y_ret, y_nsa], axis=-1) @ w_out
    return x + y


def setup_inputs(seed: int = 0) -> dict:
    key = jax.random.key(seed)
    ks = jax.random.split(key, 17)

    def nrm(k, shape, scale):
        return jax.random.normal(k, shape, jnp.float32) * scale

    L = DEPTH
    return {
        "x": nrm(ks[0], (BATCH, SEQ, D_MODEL), 1.0),
        "norm_w": 1.0 + nrm(ks[1], (L, D_MODEL), 0.01),
        "w_in": nrm(ks[2], (L, D_MODEL, D_IN), D_MODEL ** -0.5),
        "ret_norm_w": 1.0 + nrm(ks[3], (L, RET_HEADS, HEAD_DIM), 0.01),
        "q_norm_w": 1.0 + nrm(ks[4], (L, HEAD_DIM), 0.01),
        "k_norm_cmp": 1.0 + nrm(ks[5], (L, HEAD_DIM), 0.01),
        "k_norm_slc": 1.0 + nrm(ks[6], (L, HEAD_DIM), 0.01),
        "k_norm_win": 1.0 + nrm(ks[7], (L, HEAD_DIM), 0.01),
        "cmp_pos_k": nrm(ks[8], (L, CMP_BLOCK, HEAD_DIM), 0.02),
        "cmp_w1_k": nrm(ks[9], (L, CMP_BLOCK * HEAD_DIM, CMP_HIDDEN), (CMP_BLOCK * HEAD_DIM) ** -0.5),
        "cmp_w2_k": nrm(ks[10], (L, CMP_HIDDEN, HEAD_DIM), CMP_HIDDEN ** -0.5),
        "cmp_pos_v": nrm(ks[11], (L, CMP_BLOCK, HEAD_DIM), 0.02),
        "cmp_w1_v": nrm(ks[12], (L, CMP_BLOCK * HEAD_DIM, CMP_HIDDEN), (CMP_BLOCK * HEAD_DIM) ** -0.5),
        "cmp_w2_v": nrm(ks[13], (L, CMP_HIDDEN, HEAD_DIM), CMP_HIDDEN ** -0.5),
        "b_gate": nrm(ks[14], (L, N_BRANCH * NSA_HEADS), 0.01),
        "w_out": nrm(ks[15], (L, D_MIX, D_MODEL), D_MIX ** -0.5),
    }


def reference(x, norm_w, w_in, ret_norm_w, q_norm_w, k_norm_cmp, k_norm_slc, k_norm_win,
              cmp_pos_k, cmp_w1_k, cmp_w2_k, cmp_pos_v, cmp_w1_v, cmp_w2_v, b_gate, w_out):
    for layer in range(DEPTH):
        x = hybrid_layer(x, norm_w[layer], w_in[layer], ret_norm_w[layer], q_norm_w[layer],
                         k_norm_cmp[layer], k_norm_slc[layer], k_norm_win[layer],
                         cmp_pos_k[layer], cmp_w1_k[layer], cmp_w2_k[layer],
                         cmp_pos_v[layer], cmp_w1_v[layer], cmp_w2_v[layer],
                         b_gate[layer], w_out[layer])
    return x
```

```python
import functools

import numpy as np
import jax
import jax.numpy as jnp
from jax import lax
from jax.experimental import pallas as pl
from jax.experimental.pallas import tpu as pltpu

F32 = jnp.float32
BF16 = jnp.bfloat16

D_MODEL = 1024
HEAD_DIM = 64
HEAD_SHIFT = 6
RET_HEADS = 8
NSA_HEADS = 8
NSA_KV_HEADS = 2
NSA_GROUP = NSA_HEADS // NSA_KV_HEADS
D_RET = RET_HEADS * HEAD_DIM
D_NSA = NSA_HEADS * HEAD_DIM
D_KV = NSA_KV_HEADS * HEAD_DIM
N_BRANCH = 3
RET_CHUNK = 128
ROPE_THETA = 10000.0
CMP_BLOCK = 32
CMP_STRIDE = 16
CMP_HIDDEN = 256
SLC_BLOCK = 64
SLC_SHIFT = 6
SLC_TOPK = 16
WIN_SIZE = 512
EPS = 1e-6
NEG = -1e30
FORCE_BONUS = 1e4
QK_SCALE = HEAD_DIM ** -0.5

LANES = 128
PAIR = 2 * HEAD_DIM
VMEM_LIMIT = 48 * 1024 * 1024

PROJ_TM = 256
RET_TC = 512
KPREP_TM = 512
NSA_TQ = 128
SLC_TK = 512
WIN_KEYS = WIN_SIZE + NSA_TQ
OUT_TM = 512

NT_DIMS = (((1,), (1,)), ((), ()))
TN_DIMS = (((0,), (0,)), ((), ()))


def _sigmoid(x):
    return 1.0 / (1.0 + jnp.exp(-x))


def _tile4(a):
    return jnp.concatenate([a, a, a, a], axis=1)


def _proj_kernel(x_ref, nw_ref, w_ret_ref, w_ng_ref, w_ckv_ref, w_skwk_ref, wt_q_ref, wt_v_ref, wt_g_ref,
                 ret_ref, ng_ref, ckv_ref, skwk_ref, nqt_ref, vt_ref, glt_ref):
    x = x_ref[...]
    ms = jnp.mean(x * x, axis=-1, keepdims=True)
    h = (x * lax.rsqrt(ms + EPS) * nw_ref[...]).astype(BF16)
    ret_ref[...] = jnp.dot(h, w_ret_ref[...], preferred_element_type=F32)
    ng_ref[...] = jnp.dot(h, w_ng_ref[...], preferred_element_type=F32)
    ckv_ref[...] = jnp.dot(h, w_ckv_ref[...], preferred_element_type=F32)
    skwk_ref[...] = jnp.dot(h, w_skwk_ref[...], preferred_element_type=F32)
    qt = lax.dot_general(wt_q_ref[...], h, NT_DIMS, preferred_element_type=F32)
    vt = lax.dot_general(wt_v_ref[...], h, NT_DIMS, preferred_element_type=F32)
    gt = lax.dot_general(wt_g_ref[...], h, NT_DIMS, preferred_element_type=F32)
    for j in range(PROJ_TM // LANES):
        sl = slice(j * LANES, (j + 1) * LANES)
        nqt_ref[0, j] = qt[:, sl]
        vt_ref[0, j] = vt[:, sl].astype(BF16)
        glt_ref[0, j] = gt[:, sl]


def _proj(x2, nw, w_ret, w_ng, w_ckv, w_skwk, wt_q, wt_v, wt_g, B, T):
    N = B * T
    tpb = T // PROJ_TM
    sub = PROJ_TM // LANES
    nt = T // LANES
    const = lambda i: (0, 0)
    row = lambda i: (i, 0)
    trn = lambda i: (i // tpb, i % tpb, 0, 0)
    return pl.pallas_call(
        _proj_kernel,
        grid=(N // PROJ_TM,),
        in_specs=[
            pl.BlockSpec((PROJ_TM, D_MODEL), row),
            pl.BlockSpec((1, D_MODEL), const),
            pl.BlockSpec(w_ret.shape, const),
            pl.BlockSpec(w_ng.shape, const),
            pl.BlockSpec(w_ckv.shape, const),
            pl.BlockSpec(w_skwk.shape, const),
            pl.BlockSpec(wt_q.shape, const),
            pl.BlockSpec(wt_v.shape, const),
            pl.BlockSpec(wt_g.shape, const),
        ],
        out_specs=[
            pl.BlockSpec((PROJ_TM, 4 * D_RET), row),
            pl.BlockSpec((PROJ_TM, D_NSA), row),
            pl.BlockSpec((PROJ_TM, 2 * D_KV), row),
            pl.BlockSpec((PROJ_TM, 2 * D_KV), row),
            pl.BlockSpec((1, sub, D_NSA, LANES), trn),
            pl.BlockSpec((1, sub, 2 * D_KV, LANES), trn),
            pl.BlockSpec((1, sub, 32, LANES), trn),
        ],
        out_shape=[
            jax.ShapeDtypeStruct((N, 4 * D_RET), F32),
            jax.ShapeDtypeStruct((N, D_NSA), F32),
            jax.ShapeDtypeStruct((N, 2 * D_KV), F32),
            jax.ShapeDtypeStruct((N, 2 * D_KV), F32),
            jax.ShapeDtypeStruct((B, nt, D_NSA, LANES), F32),
            jax.ShapeDtypeStruct((B, nt, 2 * D_KV, LANES), BF16),
            jax.ShapeDtypeStruct((B, nt, 32, LANES), F32),
        ],
        compiler_params=pltpu.CompilerParams(dimension_semantics=("parallel",), vmem_limit_bytes=VMEM_LIMIT),
        name="proj",
    )(x2, nw, w_ret, w_ng, w_ckv, w_skwk, wt_q, wt_v, wt_g)


def _ret_kernel(q_ref, k_ref, v_ref, g_ref, cos_ref, sin_ref, dec_ref, zeta_ref, xi_ref, gch_ref, nw_ref,
                o_ref, state_ref):
    @pl.when(pl.program_id(2) == 0)
    def _():
        state_ref[...] = jnp.zeros_like(state_ref)

    lane = lax.broadcasted_iota(jnp.int32, (1, PAIR), 1)
    head0 = lane < HEAD_DIM
    m0 = jnp.where(head0, 1.0, 0.0)
    m1 = 1.0 - m0
    first_half = (lane & (HEAD_DIM - 1)) < (HEAD_DIM // 2)
    rr = lax.broadcasted_iota(jnp.int32, (PAIR, PAIR), 0) >> HEAD_SHIFT
    cc = lax.broadcasted_iota(jnp.int32, (PAIR, PAIR), 1) >> HEAD_SHIFT
    blockdiag = jnp.where(rr == cc, 1.0, 0.0)
    zeta = zeta_ref[0]
    xi = xi_ref[0]
    gch = gch_ref[0]
    nw = nw_ref[0]

    for c in range(RET_TC // RET_CHUNK):
        sl = pl.ds(c * RET_CHUNK, RET_CHUNK)
        q = q_ref[0, sl, :]
        k = k_ref[0, sl, :]
        v = v_ref[0, sl, :]
        g = g_ref[0, sl, :]
        cos = cos_ref[sl, :]
        sin = sin_ref[sl, :]

        def rope(t):
            rot = jnp.where(first_half, pltpu.roll(t, PAIR - HEAD_DIM // 2, 1), pltpu.roll(t, HEAD_DIM // 2, 1))
            return t * cos + rot * sin

        qr = rope(q)
        kb = (rope(k) * QK_SCALE).astype(BF16)
        state = state_ref[...]
        o = jnp.dot(qr.astype(BF16), state.astype(BF16), preferred_element_type=F32) * xi
        for h, mh in ((0, m0), (1, m1)):
            s = lax.dot_general((qr * mh).astype(BF16), kb, NT_DIMS, preferred_element_type=F32) * dec_ref[0, h]
            o = o + jnp.dot(s.astype(BF16), (v * mh).astype(BF16), preferred_element_type=F32)
        kv = lax.dot_general(kb, (v * zeta).astype(BF16), TN_DIMS, preferred_element_type=F32)
        state_ref[...] = state * gch + kv * blockdiag

        o2 = o * o
        ms0 = jnp.sum(o2 * m0, axis=-1, keepdims=True)
        ms1 = jnp.sum(o2 * m1, axis=-1, keepdims=True)
        ms = jnp.where(head0, ms0, ms1) * (1.0 / HEAD_DIM)
        y = o * lax.rsqrt(ms + EPS) * nw
        o_ref[0, sl, :] = y * (g * _sigmoid(g))


def _retention(ret3, cos, sin, dec, zeta, xi, gch, nw, B, T):
    npair = RET_HEADS // 2
    blk = lambda col0: pl.BlockSpec((1, RET_TC, PAIR), lambda b, p, i: (b, i, col0 + p))
    tab = pl.BlockSpec((RET_TC, PAIR), lambda b, p, i: (i, 0))
    per_pair3 = lambda shape: pl.BlockSpec((1,) + shape, lambda b, p, i: (p,) + (0,) * len(shape))
    return pl.pallas_call(
        _ret_kernel,
        grid=(B, npair, T // RET_TC),
        in_specs=[blk(0), blk(npair), blk(2 * npair), blk(3 * npair), tab, tab,
                  per_pair3((2, RET_CHUNK, RET_CHUNK)), per_pair3((RET_CHUNK, PAIR)), per_pair3((RET_CHUNK, PAIR)),
                  per_pair3((1, PAIR)), per_pair3((1, PAIR))],
        out_specs=pl.BlockSpec((1, RET_TC, PAIR), lambda b, p, i: (b, i, p)),
        out_shape=jax.ShapeDtypeStruct((B, T, D_RET), F32),
        scratch_shapes=[pltpu.VMEM((PAIR, PAIR), F32)],
        compiler_params=pltpu.CompilerParams(dimension_semantics=("parallel", "parallel", "arbitrary"),
                                             vmem_limit_bytes=VMEM_LIMIT),
        name="retention",
    )(ret3, ret3, ret3, ret3, cos, sin, dec, zeta, xi, gch, nw)


def _cmp_kernel(x_ref, pos_ref, w1_ref, w2_ref, w2t_ref, knw_ref, o_ref, ot_ref):
    is_key = pl.program_id(1) == 0
    half = CMP_STRIDE * HEAD_DIM
    for g in range(NSA_KV_HEADS):
        x = x_ref[0, g]
        a = jnp.dot((x + pos_ref[0, 0:1, :]).astype(BF16), w1_ref[0, :half, :], preferred_element_type=F32)
        b = jnp.dot((x + pos_ref[0, 1:2, :]).astype(BF16), w1_ref[0, half:, :], preferred_element_type=F32)
        hid = a + pltpu.roll(b, b.shape[0] - 1, 0)
        hid = (hid * _sigmoid(hid)).astype(BF16)
        out = jnp.dot(hid, w2_ref[0], preferred_element_type=F32)
        ms = jnp.mean(out * out, axis=-1, keepdims=True)
        normed = out * lax.rsqrt(ms + EPS) * knw_ref[...]
        o_ref[0, 0, g] = jnp.where(is_key, normed, out)
        ot_ref[0, 0, g] = lax.dot_general(w2t_ref[0], hid, NT_DIMS, preferred_element_type=F32)


def _compress(xc, pos, w1, w2, w2t, knw, B, ncb):
    return pl.pallas_call(
        _cmp_kernel,
        grid=(B, 2),
        in_specs=[
            pl.BlockSpec((1, NSA_KV_HEADS, ncb, CMP_STRIDE * HEAD_DIM), lambda b, s: (b, s, 0, 0)),
            pl.BlockSpec((1, 2, CMP_STRIDE * HEAD_DIM), lambda b, s: (s, 0, 0)),
            pl.BlockSpec((1, CMP_BLOCK * HEAD_DIM, CMP_HIDDEN), lambda b, s: (s, 0, 0)),
            pl.BlockSpec((1, CMP_HIDDEN, HEAD_DIM), lambda b, s: (s, 0, 0)),
            pl.BlockSpec((1, HEAD_DIM, CMP_HIDDEN), lambda b, s: (s, 0, 0)),
            pl.BlockSpec((1, HEAD_DIM), lambda b, s: (0, 0)),
        ],
        out_specs=[
            pl.BlockSpec((1, 1, NSA_KV_HEADS, ncb, HEAD_DIM), lambda b, s: (b, s, 0, 0, 0)),
            pl.BlockSpec((1, 1, NSA_KV_HEADS, HEAD_DIM, ncb), lambda b, s: (b, s, 0, 0, 0)),
        ],
        out_shape=[
            jax.ShapeDtypeStruct((B, 2, NSA_KV_HEADS, ncb, HEAD_DIM), F32),
            jax.ShapeDtypeStruct((B, 2, NSA_KV_HEADS, HEAD_DIM, ncb), F32),
        ],
        compiler_params=pltpu.CompilerParams(dimension_semantics=("parallel", "parallel"),
                                             vmem_limit_bytes=VMEM_LIMIT),
        name="compress",
    )(xc, pos, w1, w2, w2t, knw)


def _kprep_kernel(x_ref, w_ref, ks_ref, kw_ref):
    lane = lax.broadcasted_iota(jnp.int32, (1, PAIR), 1)
    head0 = lane < HEAD_DIM
    m0 = jnp.where(head0, 1.0, 0.0)
    m1 = 1.0 - m0
    t0 = pl.program_id(1) * KPREP_TM
    tok = t0 + lax.broadcasted_iota(jnp.int32, (KPREP_TM, PAIR), 0)
    col = lax.broadcasted_iota(jnp.int32, (KPREP_TM, PAIR), 1)
    indicator = jnp.where((tok >> SLC_SHIFT) == col - HEAD_DIM, 1.0, 0.0)

    def normed(x, w):
        x2 = x * x
        ms0 = jnp.sum(x2 * m0, axis=-1, keepdims=True)
        ms1 = jnp.sum(x2 * m1, axis=-1, keepdims=True)
        ms = jnp.where(head0, ms0, ms1) * (1.0 / HEAD_DIM)
        return x * lax.rsqrt(ms + EPS) * w

    ns = normed(x_ref[0, :, :PAIR], w_ref[0:1, :])
    nw = normed(x_ref[0, :, PAIR:], w_ref[1:2, :])
    for g in range(NSA_KV_HEADS):
        s_g = ns if g == 0 else pltpu.roll(ns, HEAD_DIM, 1)
        w_g = nw if g == 0 else pltpu.roll(nw, HEAD_DIM, 1)
        ks_ref[0, g] = jnp.where(head0, s_g, indicator).astype(BF16)
        kw_ref[0, g] = jnp.where(head0, w_g, 0.0).astype(BF16)


def _kprep(skwk3, w, B, T):
    return pl.pallas_call(
        _kprep_kernel,
        grid=(B, T // KPREP_TM),
        in_specs=[pl.BlockSpec((1, KPREP_TM, 2 * PAIR), lambda b, i: (b, i, 0)),
                  pl.BlockSpec((2, PAIR), lambda b, i: (0, 0))],
        out_specs=[pl.BlockSpec((1, NSA_KV_HEADS, KPREP_TM, PAIR), lambda b, i: (b, 0, i, 0)),
                   pl.BlockSpec((1, NSA_KV_HEADS, KPREP_TM, PAIR), lambda b, i: (b, 0, i, 0))],
        out_shape=[jax.ShapeDtypeStruct((B, NSA_KV_HEADS, T, PAIR), BF16),
                   jax.ShapeDtypeStruct((B, NSA_KV_HEADS, T, PAIR), BF16)],
        compiler_params=pltpu.CompilerParams(dimension_semantics=("parallel", "parallel"),
                                             vmem_limit_bytes=VMEM_LIMIT),
        name="kprep",
    )(skwk3, w)


def _nsa_kernel(qt_ref, glt_ref, ng_ref, ks_ref, kw_ref, vst_ref, vwt_ref, kc_ref, vct_ref, mt_ref, qnw_ref, bg_ref,
                o_ref, qp_ref, m_ref, l_ref, acc_ref):
    qi = pl.program_id(2)
    t0 = qi * NSA_TQ
    ncols = NSA_GROUP * NSA_TQ

    cols = []
    for r in range(NSA_GROUP):
        q = qt_ref[0, 0, r * HEAD_DIM:(r + 1) * HEAD_DIM, :]
        ms = jnp.mean(q * q, axis=0, keepdims=True)
        cols.append(q * lax.rsqrt(ms + EPS) * qnw_ref[...] * QK_SCALE)
    qs = jnp.concatenate(cols, axis=1).astype(BF16)
    qp_ref[0:HEAD_DIM, :] = qs

    ncb = kc_ref.shape[3]
    sc = jnp.dot(kc_ref[0, 0, 0].astype(BF16), qs, preferred_element_type=F32)
    n_idx = lax.broadcasted_iota(jnp.int32, (ncb, ncols), 0)
    tok_c = t0 + (lax.broadcasted_iota(jnp.int32, (ncb, ncols), 1) & (NSA_TQ - 1))
    valid_c = (n_idx * CMP_STRIDE + (CMP_BLOCK - 1)) <= tok_c
    sc = jnp.where(valid_c, sc, NEG)
    mc = jnp.max(sc, axis=0, keepdims=True)
    ec = jnp.exp(sc - mc)
    lc = jnp.sum(ec, axis=0, keepdims=True)
    p = jnp.where(valid_c, ec * (1.0 / lc), 0.0)
    oc_t = jnp.dot(vct_ref[0, 0, 0].astype(BF16), p.astype(BF16), preferred_element_type=F32)

    ps = p[:, 0:NSA_TQ]
    for r in range(1, NSA_GROUP):
        ps = ps + p[:, r * NSA_TQ:(r + 1) * NSA_TQ]
    ps_hi = ps.astype(BF16)
    ps_lo = (ps - ps_hi.astype(F32)).astype(BF16)
    imp = (jnp.dot(mt_ref[...], ps_hi, preferred_element_type=F32)
           + jnp.dot(mt_ref[...], ps_lo, preferred_element_type=F32))
    n_slc = mt_ref.shape[0]
    jb = lax.broadcasted_iota(jnp.int32, (n_slc, NSA_TQ), 0)
    tok_s = t0 + lax.broadcasted_iota(jnp.int32, (n_slc, NSA_TQ), 1)
    valid_s = jb * SLC_BLOCK <= tok_s
    force = (jb == (tok_s >> SLC_SHIFT)) | (jb == 0)
    score = jnp.where(valid_s, jnp.where(force, imp + FORCE_BONUS, imp), NEG)
    rank = jnp.zeros((n_slc, NSA_TQ), F32)
    for i in range(n_slc):
        row = score[i:i + 1, :]
        beats = (row > score) | ((row >= score) & (jb > i))
        rank = rank + jnp.where(beats, 1.0, 0.0)
    sel = (rank < float(SLC_TOPK)) & valid_s
    bias = jnp.where(sel, 0.0, NEG).astype(BF16)
    qp_ref[HEAD_DIM:2 * HEAD_DIM, :] = _tile4(bias)
    qp = qp_ref[...]

    kt0 = jnp.maximum(qi - WIN_SIZE // NSA_TQ, 0)
    ks0 = pl.multiple_of(kt0 * NSA_TQ, NSA_TQ)
    sw = jnp.dot(kw_ref[0, 0, pl.ds(ks0, WIN_KEYS), :], qp, preferred_element_type=F32)
    r_w = lax.broadcasted_iota(jnp.int32, (WIN_KEYS, ncols), 0)
    c_w = lax.broadcasted_iota(jnp.int32, (WIN_KEYS, ncols), 1) & (NSA_TQ - 1)
    delta = (t0 - ks0) + c_w - r_w
    sw = sw + jnp.where((delta >= 0) & (delta < WIN_SIZE), 0.0, NEG)
    mw = jnp.max(sw, axis=0, keepdims=True)
    ew = jnp.exp(sw - mw)
    lw = jnp.sum(ew, axis=0, keepdims=True)
    ewb = ew.astype(BF16)
    ow_t = jnp.zeros((HEAD_DIM, ncols), F32)
    for j in range(WIN_KEYS // LANES):
        ow_t = ow_t + jnp.dot(vwt_ref[0, kt0 + j], ewb[j * LANES:(j + 1) * LANES, :], preferred_element_type=F32)
    ow_t = ow_t * (1.0 / lw)

    m_ref[...] = jnp.full(m_ref.shape, NEG, F32)
    l_ref[...] = jnp.zeros(l_ref.shape, F32)
    acc_ref[...] = jnp.zeros(acc_ref.shape, F32)
    vt_per_tile = SLC_TK // LANES

    def slc_tile(j, causal):
        kst = pl.multiple_of(j * SLC_TK, SLC_TK)
        s = jnp.dot(ks_ref[0, 0, pl.ds(kst, SLC_TK), :], qp, preferred_element_type=F32)
        if causal:
            r_s = lax.broadcasted_iota(jnp.int32, (SLC_TK, ncols), 0)
            c_s = lax.broadcasted_iota(jnp.int32, (SLC_TK, ncols), 1) & (NSA_TQ - 1)
            s = s + jnp.where(kst + r_s <= t0 + c_s, 0.0, NEG)
        m_old = m_ref[...]
        m_new = jnp.maximum(m_old, jnp.max(s, axis=0, keepdims=True))
        alpha = jnp.exp(m_old - m_new)
        e = jnp.exp(s - m_new)
        l_ref[...] = alpha * l_ref[...] + jnp.sum(e, axis=0, keepdims=True)
        eb = e.astype(BF16)
        pv = jnp.zeros((HEAD_DIM, ncols), F32)
        for jj in range(vt_per_tile):
            pv = pv + jnp.dot(vst_ref[0, j * vt_per_tile + jj], eb[jj * LANES:(jj + 1) * LANES, :],
                              preferred_element_type=F32)
        acc_ref[...] = alpha * acc_ref[...] + pv
        m_ref[...] = m_new

    n_full = t0 // SLC_TK

    def body(j, carry):
        slc_tile(j, False)
        return carry

    lax.fori_loop(0, n_full, body, 0)
    slc_tile(n_full, True)
    os_t = acc_ref[...] * (1.0 / l_ref[...])

    gates = _sigmoid(glt_ref[0, 0] + bg_ref[...])
    outs = []
    for r in range(NSA_GROUP):
        sl = slice(r * NSA_TQ, (r + 1) * NSA_TQ)
        outs.append(gates[r:r + 1, :] * oc_t[:, sl]
                    + gates[NSA_GROUP + r:NSA_GROUP + r + 1, :] * os_t[:, sl]
                    + gates[2 * NSA_GROUP + r:2 * NSA_GROUP + r + 1, :] * ow_t[:, sl])
    o_tok = jnp.concatenate(outs, axis=0).T
    ng = ng_ref[0]
    o_ref[0] = o_tok * (ng * _sigmoid(ng))


def _nsa(nqt, glt, ng3, ks, kw, vt, kc, vct, mt, qnw, bg, B, T):
    nt = T // LANES
    ncb = kc.shape[3]
    gw = NSA_GROUP * HEAD_DIM
    return pl.pallas_call(
        _nsa_kernel,
        grid=(B, NSA_KV_HEADS, T // NSA_TQ),
        in_specs=[
            pl.BlockSpec((1, 1, gw, LANES), lambda b, g, i: (b, i, g, 0)),
            pl.BlockSpec((1, 1, 16, LANES), lambda b, g, i: (b, i, g, 0)),
            pl.BlockSpec((1, NSA_TQ, gw), lambda b, g, i: (b, i, g)),
            pl.BlockSpec((1, 1, T, PAIR), lambda b, g, i: (b, g, 0, 0)),
            pl.BlockSpec((1, 1, T, PAIR), lambda b, g, i: (b, g, 0, 0)),
            pl.BlockSpec((1, nt, HEAD_DIM, LANES), lambda b, g, i: (b, 0, g, 0)),
            pl.BlockSpec((1, nt, HEAD_DIM, LANES), lambda b, g, i: (b, 0, NSA_KV_HEADS + g, 0)),
            pl.BlockSpec((1, 1, 1, ncb, HEAD_DIM), lambda b, g, i: (b, 0, g, 0, 0)),
            pl.BlockSpec((1, 1, 1, HEAD_DIM, ncb), lambda b, g, i: (b, 1, g, 0, 0)),
            pl.BlockSpec(mt.shape, lambda b, g, i: (0, 0)),
            pl.BlockSpec((HEAD_DIM, 1), lambda b, g, i: (0, 0)),
            pl.BlockSpec((16, 1), lambda b, g, i: (g, 0)),
        ],
        out_specs=pl.BlockSpec((1, NSA_TQ, gw), lambda b, g, i: (b, i, g)),
        out_shape=jax.ShapeDtypeStruct((B, T, D_NSA), F32),
        scratch_shapes=[
            pltpu.VMEM((2 * HEAD_DIM, NSA_GROUP * NSA_TQ), BF16),
            pltpu.VMEM((1, NSA_GROUP * NSA_TQ), F32),
            pltpu.VMEM((1, NSA_GROUP * NSA_TQ), F32),
            pltpu.VMEM((HEAD_DIM, NSA_GROUP * NSA_TQ), F32),
        ],
        compiler_params=pltpu.CompilerParams(dimension_semantics=("parallel", "parallel", "arbitrary"),
                                             vmem_limit_bytes=VMEM_LIMIT),
        name="nsa",
    )(nqt, glt, ng3, ks, kw, vt, vt, kc, vct, mt, qnw, bg)


def _out_kernel(x_ref, yr_ref, yn_ref, wr_ref, wn_ref, o_ref):
    o_ref[...] = (x_ref[...]
                  + jnp.dot(yr_ref[...].astype(BF16), wr_ref[...], preferred_element_type=F32)
                  + jnp.dot(yn_ref[...].astype(BF16), wn_ref[...], preferred_element_type=F32))


def _outproj(x2, yr, yn, wr, wn):
    N = x2.shape[0]
    row = lambda i: (i, 0)
    const = lambda i: (0, 0)
    return pl.pallas_call(
        _out_kernel,
        grid=(N // OUT_TM,),
        in_specs=[pl.BlockSpec((OUT_TM, D_MODEL), row), pl.BlockSpec((OUT_TM, D_RET), row),
                  pl.BlockSpec((OUT_TM, D_NSA), row), pl.BlockSpec(wr.shape, const), pl.BlockSpec(wn.shape, const)],
        out_specs=pl.BlockSpec((OUT_TM, D_MODEL), row),
        out_shape=jax.ShapeDtypeStruct((N, D_MODEL), F32),
        compiler_params=pltpu.CompilerParams(dimension_semantics=("parallel",), vmem_limit_bytes=VMEM_LIMIT),
        name="outproj",
    )(x2, yr, yn, wr, wn)


@functools.lru_cache(maxsize=None)
def _tables(T):
    half = HEAD_DIM // 2
    inv = ROPE_THETA ** (-np.arange(half, dtype=np.float64) / half)
    ang = np.arange(T, dtype=np.float64)[:, None] * inv[None, :]
    cos64 = np.concatenate([np.cos(ang), np.cos(ang)], axis=1)
    sin64 = np.concatenate([-np.sin(ang), np.sin(ang)], axis=1)
    cos = np.concatenate([cos64, cos64], axis=1).astype(np.float32)
    sin = np.concatenate([sin64, sin64], axis=1).astype(np.float32)

    C = RET_CHUNK
    log_g = np.log1p(-np.exp2(-5.0 - np.arange(RET_HEADS, dtype=np.float64)))
    pos = np.arange(C, dtype=np.float64)
    diff = pos[:, None] - pos[None, :]
    decay = np.where(diff >= 0, np.exp(log_g[:, None, None] * np.maximum(diff, 0.0)), 0.0)
    zeta = np.exp(log_g[:, None] * (C - 1.0 - pos))
    xi = np.exp(log_g[:, None] * (pos + 1.0))
    g_chunk = np.exp(log_g * C)
    npair = RET_HEADS // 2

    def pair_lanes(a):
        return np.repeat(a.reshape(npair, 2, C).transpose(0, 2, 1), HEAD_DIM, axis=2).astype(np.float32)

    dec = decay.reshape(npair, 2, C, C).astype(np.float32)
    gch = np.repeat(g_chunk.reshape(npair, 1, 2), HEAD_DIM, axis=2).astype(np.float32)

    n_cmp = (T - CMP_BLOCK) // CMP_STRIDE + 1
    ncb = T // CMP_STRIDE
    p = np.arange(n_cmp)[:, None] * CMP_STRIDE + np.arange(CMP_BLOCK)[None, :]
    blk = p // SLC_BLOCK
    M = (blk[:, :, None] == np.arange(T // SLC_BLOCK)[None, None, :]).mean(axis=1)
    mt = np.zeros((T // SLC_BLOCK, ncb), np.float32)
    mt[:, :n_cmp] = M.T
    return cos, sin, dec, pair_lanes(zeta), pair_lanes(xi), gch, mt


def kernel(x, norm_w, w_in, ret_norm_w, q_norm_w, k_norm_cmp, k_norm_slc, k_norm_win, cmp_pos_k, cmp_w1_k, cmp_w2_k,
           cmp_pos_v, cmp_w1_v, cmp_w2_v, b_gate, w_out):
    B, T, D = x.shape
    depth = norm_w.shape[0]
    cos, sin, dec, zeta, xi, gch, mt = _tables(T)
    ncb = T // CMP_STRIDE
    half = CMP_STRIDE * HEAD_DIM
    gate_src = np.zeros((NSA_KV_HEADS, 16), np.int32)
    gate_ok = np.zeros((NSA_KV_HEADS, 16), bool)
    for g in range(NSA_KV_HEADS):
        for br in range(N_BRANCH):
            for r in range(NSA_GROUP):
                gate_src[g, br * NSA_GROUP + r] = br * NSA_HEADS + g * NSA_GROUP + r
                gate_ok[g, br * NSA_GROUP + r] = True
    gate_src = gate_src.reshape(-1)
    gate_ok = gate_ok.reshape(-1)

    x2 = x.reshape(B * T, D)
    for layer in range(depth):
        w = w_in[layer].astype(BF16)
        o_ng = 4 * D_RET + D_NSA
        o_kv = o_ng + D_NSA
        w_ret = w[:, :4 * D_RET]
        wt_q = w[:, 4 * D_RET:o_ng].T
        w_ng = w[:, o_ng:o_kv]
        w_ckv = w[:, o_kv:o_kv + 2 * D_KV]
        w_skwk = jnp.concatenate([w[:, o_kv + 2 * D_KV:o_kv + 3 * D_KV], w[:, o_kv + 4 * D_KV:o_kv + 5 * D_KV]], axis=1)
        wt_v = jnp.concatenate([w[:, o_kv + 3 * D_KV:o_kv + 4 * D_KV], w[:, o_kv + 5 * D_KV:o_kv + 6 * D_KV]], axis=1).T
        w_gl = w[:, o_kv + 6 * D_KV:]
        wt_g = jnp.where(gate_ok[:, None], w_gl.T[gate_src], jnp.zeros((), BF16))
        bg = jnp.where(gate_ok, b_gate[layer][gate_src], 0.0).reshape(-1, 1)

        ret, ng, ckv, skwk, nqt, vt, glt = _proj(x2, norm_w[layer].reshape(1, D), w_ret, w_ng, w_ckv, w_skwk,
                                                 wt_q, wt_v, wt_g, B, T)

        nw_pair = ret_norm_w[layer].reshape(RET_HEADS // 2, 1, PAIR)
        y_ret = _retention(ret.reshape(B, T, 4 * D_RET), jnp.asarray(cos), jnp.asarray(sin), jnp.asarray(dec),
                           jnp.asarray(zeta), jnp.asarray(xi), jnp.asarray(gch), nw_pair, B, T)

        xc = ckv.reshape(B, T, 2 * NSA_KV_HEADS, HEAD_DIM).transpose(0, 2, 1, 3).reshape(B, 2 * NSA_KV_HEADS, ncb, half)
        pos = jnp.stack([cmp_pos_k[layer], cmp_pos_v[layer]]).reshape(2, 2, half)
        w1 = jnp.stack([cmp_w1_k[layer], cmp_w1_v[layer]]).astype(BF16)
        w2 = jnp.stack([cmp_w2_k[layer], cmp_w2_v[layer]]).astype(BF16)
        w2t = jnp.swapaxes(w2, 1, 2)
        kc, vct = _compress(xc, pos, w1, w2, w2t, k_norm_cmp[layer].reshape(1, HEAD_DIM), B, ncb)

        knw = jnp.stack([jnp.tile(k_norm_slc[layer], 2), jnp.tile(k_norm_win[layer], 2)])
        ks, kw = _kprep(skwk.reshape(B, T, 2 * PAIR), knw, B, T)

        y_nsa = _nsa(nqt, glt, ng.reshape(B, T, D_NSA), ks, kw, vt, kc, vct, jnp.asarray(mt).astype(BF16),
                     q_norm_w[layer].reshape(HEAD_DIM, 1), bg, B, T)

        wo = w_out[layer].astype(BF16)
        x2 = _outproj(x2, y_ret.reshape(B * T, D_RET), y_nsa.reshape(B * T, D_NSA), wo[:D_RET], wo[D_RET:])
    return x2.reshape(B, T, D)
```

```python
import functools

import numpy as np
import jax
import jax.numpy as jnp
from jax import lax
from jax.experimental import pallas as pl
from jax.experimental.pallas import tpu as pltpu

F32 = jnp.float32
BF16 = jnp.bfloat16

D_MODEL = 1024
HEAD_DIM = 64
HEAD_SHIFT = 6
RET_HEADS = 8
NSA_HEADS = 8
NSA_KV_HEADS = 2
NSA_GROUP = NSA_HEADS // NSA_KV_HEADS
D_RET = RET_HEADS * HEAD_DIM
D_NSA = NSA_HEADS * HEAD_DIM
D_KV = NSA_KV_HEADS * HEAD_DIM
N_BRANCH = 3
RET_CHUNK = 128
ROPE_THETA = 10000.0
CMP_BLOCK = 32
CMP_STRIDE = 16
CMP_HIDDEN = 256
SLC_BLOCK = 64
SLC_SHIFT = 6
SLC_TOPK = 16
WIN_SIZE = 512
EPS = 1e-6
NEG = -1e30
FORCE_BONUS = 1e4
QK_SCALE = HEAD_DIM ** -0.5

LANES = 128
SUBLANES = 8
PAIR = 2 * HEAD_DIM
VMEM_LIMIT = 48 * 1024 * 1024

PROJ_TM = 256
RET_TC = 512
KPREP_TM = 512
NSA_TQ = 128
SLC_TK = 512
WIN_KEYS = WIN_SIZE + NSA_TQ
OUT_TM = 512

NT_DIMS = (((1,), (1,)), ((), ()))
TN_DIMS = (((0,), (0,)), ((), ()))


def _sigmoid(x):
    return 1.0 / (1.0 + jnp.exp(-x))


def _tile4(a):
    return jnp.concatenate([a, a, a, a], axis=1)


def _proj_kernel(x_ref, nw_ref, w_ret_ref, w_ng_ref, w_ckv_ref, w_skwk_ref, wt_q_ref, wt_v_ref, wt_g_ref,
                 ret_ref, ng_ref, ckv_ref, skwk_ref, nqt_ref, vt_ref, glt_ref):
    x = x_ref[...]
    ms = jnp.mean(x * x, axis=-1, keepdims=True)
    h = (x * lax.rsqrt(ms + EPS) * nw_ref[...]).astype(BF16)
    ret_ref[...] = jnp.dot(h, w_ret_ref[...], preferred_element_type=F32)
    ng_ref[...] = jnp.dot(h, w_ng_ref[...], preferred_element_type=F32)
    ckv_ref[...] = jnp.dot(h, w_ckv_ref[...], preferred_element_type=F32)
    skwk_ref[...] = jnp.dot(h, w_skwk_ref[...], preferred_element_type=F32)
    qt = lax.dot_general(wt_q_ref[...], h, NT_DIMS, preferred_element_type=F32)
    vt = lax.dot_general(wt_v_ref[...], h, NT_DIMS, preferred_element_type=F32)
    gt = lax.dot_general(wt_g_ref[...], h, NT_DIMS, preferred_element_type=F32)
    for j in range(PROJ_TM // LANES):
        sl = slice(j * LANES, (j + 1) * LANES)
        nqt_ref[0, j] = qt[:, sl]
        vt_ref[0, j] = vt[:, sl].astype(BF16)
        glt_ref[0, j] = gt[:, sl]


def _proj(x2, nw, w_ret, w_ng, w_ckv, w_skwk, wt_q, wt_v, wt_g, B, T):
    N = B * T
    tpb = T // PROJ_TM
    sub = PROJ_TM // LANES
    nt = T // LANES
    const = lambda i: (0, 0)
    row = lambda i: (i, 0)
    trn = lambda i: (i // tpb, i % tpb, 0, 0)
    return pl.pallas_call(
        _proj_kernel,
        grid=(N // PROJ_TM,),
        in_specs=[
            pl.BlockSpec((PROJ_TM, D_MODEL), row),
            pl.BlockSpec((1, D_MODEL), const),
            pl.BlockSpec(w_ret.shape, const),
            pl.BlockSpec(w_ng.shape, const),
            pl.BlockSpec(w_ckv.shape, const),
            pl.BlockSpec(w_skwk.shape, const),
            pl.BlockSpec(wt_q.shape, const),
            pl.BlockSpec(wt_v.shape, const),
            pl.BlockSpec(wt_g.shape, const),
        ],
        out_specs=[
            pl.BlockSpec((PROJ_TM, 4 * D_RET), row),
            pl.BlockSpec((PROJ_TM, D_NSA), row),
            pl.BlockSpec((PROJ_TM, 2 * D_KV), row),
            pl.BlockSpec((PROJ_TM, 2 * D_KV), row),
            pl.BlockSpec((1, sub, D_NSA, LANES), trn),
            pl.BlockSpec((1, sub, 2 * D_KV, LANES), trn),
            pl.BlockSpec((1, sub, 32, LANES), trn),
        ],
        out_shape=[
            jax.ShapeDtypeStruct((N, 4 * D_RET), F32),
            jax.ShapeDtypeStruct((N, D_NSA), F32),
            jax.ShapeDtypeStruct((N, 2 * D_KV), F32),
            jax.ShapeDtypeStruct((N, 2 * D_KV), F32),
            jax.ShapeDtypeStruct((B, nt, D_NSA, LANES), F32),
            jax.ShapeDtypeStruct((B, nt, 2 * D_KV, LANES), BF16),
            jax.ShapeDtypeStruct((B, nt, 32, LANES), F32),
        ],
        compiler_params=pltpu.CompilerParams(dimension_semantics=("parallel",), vmem_limit_bytes=VMEM_LIMIT),
        name="proj",
    )(x2, nw, w_ret, w_ng, w_ckv, w_skwk, wt_q, wt_v, wt_g)


def _ret_kernel(q_ref, k_ref, v_ref, g_ref, cos_ref, sin_ref, dec_ref, zeta_ref, xi_ref, gch_ref, nw_ref,
                o_ref, state_ref):
    @pl.when(pl.program_id(2) == 0)
    def _():
        state_ref[...] = jnp.zeros_like(state_ref)

    lane = lax.broadcasted_iota(jnp.int32, (1, PAIR), 1)
    head0 = lane < HEAD_DIM
    m0 = jnp.where(head0, 1.0, 0.0)
    m1 = 1.0 - m0
    first_half = (lane & (HEAD_DIM - 1)) < (HEAD_DIM // 2)
    rr = lax.broadcasted_iota(jnp.int32, (PAIR, PAIR), 0) >> HEAD_SHIFT
    cc = lax.broadcasted_iota(jnp.int32, (PAIR, PAIR), 1) >> HEAD_SHIFT
    blockdiag = jnp.where(rr == cc, 1.0, 0.0)
    zeta = zeta_ref[0]
    xi = xi_ref[0]
    gch = gch_ref[0]
    nw = nw_ref[0]

    for c in range(RET_TC // RET_CHUNK):
        sl = pl.ds(c * RET_CHUNK, RET_CHUNK)
        q = q_ref[0, sl, :]
        k = k_ref[0, sl, :]
        v = v_ref[0, sl, :]
        g = g_ref[0, sl, :]
        cos = cos_ref[sl, :]
        sin = sin_ref[sl, :]

        def rope(t):
            rot = jnp.where(first_half, pltpu.roll(t, PAIR - HEAD_DIM // 2, 1), pltpu.roll(t, HEAD_DIM // 2, 1))
            return t * cos + rot * sin

        qr = rope(q)
        kb = (rope(k) * QK_SCALE).astype(BF16)
        state = state_ref[...]
        o = jnp.dot(qr.astype(BF16), state.astype(BF16), preferred_element_type=F32) * xi
        for h, mh in ((0, m0), (1, m1)):
            s = lax.dot_general((qr * mh).astype(BF16), kb, NT_DIMS, preferred_element_type=F32) * dec_ref[0, h]
            o = o + jnp.dot(s.astype(BF16), (v * mh).astype(BF16), preferred_element_type=F32)
        kv = lax.dot_general(kb, (v * zeta).astype(BF16), TN_DIMS, preferred_element_type=F32)
        state_ref[...] = state * gch + kv * blockdiag

        o2 = o * o
        ms0 = jnp.sum(o2 * m0, axis=-1, keepdims=True)
        ms1 = jnp.sum(o2 * m1, axis=-1, keepdims=True)
        ms = jnp.where(head0, ms0, ms1) * (1.0 / HEAD_DIM)
        y = o * lax.rsqrt(ms + EPS) * nw
        o_ref[0, sl, :] = y * (g * _sigmoid(g))


def _retention(ret3, cos, sin, dec, zeta, xi, gch, nw, B, T):
    npair = RET_HEADS // 2
    blk = lambda col0: pl.BlockSpec((1, RET_TC, PAIR), lambda b, p, i: (b, i, col0 + p))
    tab = pl.BlockSpec((RET_TC, PAIR), lambda b, p, i: (i, 0))
    per_pair3 = lambda shape: pl.BlockSpec((1,) + shape, lambda b, p, i: (p,) + (0,) * len(shape))
    return pl.pallas_call(
        _ret_kernel,
        grid=(B, npair, T // RET_TC),
        in_specs=[blk(0), blk(npair), blk(2 * npair), blk(3 * npair), tab, tab,
                  per_pair3((2, RET_CHUNK, RET_CHUNK)), per_pair3((RET_CHUNK, PAIR)), per_pair3((RET_CHUNK, PAIR)),
                  per_pair3((1, PAIR)), per_pair3((1, PAIR))],
        out_specs=pl.BlockSpec((1, RET_TC, PAIR), lambda b, p, i: (b, i, p)),
        out_shape=jax.ShapeDtypeStruct((B, T, D_RET), F32),
        scratch_shapes=[pltpu.VMEM((PAIR, PAIR), F32)],
        compiler_params=pltpu.CompilerParams(dimension_semantics=("parallel", "parallel", "arbitrary"),
                                             vmem_limit_bytes=VMEM_LIMIT),
        name="retention",
    )(ret3, ret3, ret3, ret3, cos, sin, dec, zeta, xi, gch, nw)


def _cmp_kernel(x_ref, pos_ref, w1_ref, w2_ref, w2t_ref, knw_ref, o_ref, ot_ref):
    is_key = pl.program_id(1) == 0
    half = CMP_STRIDE * HEAD_DIM
    for g in range(NSA_KV_HEADS):
        x = x_ref[0, g]
        a = jnp.dot((x + pos_ref[0, 0:1, :]).astype(BF16), w1_ref[0, :half, :], preferred_element_type=F32)
        b = jnp.dot((x + pos_ref[0, 1:2, :]).astype(BF16), w1_ref[0, half:, :], preferred_element_type=F32)
        hid = a + pltpu.roll(b, b.shape[0] - 1, 0)
        hid = (hid * _sigmoid(hid)).astype(BF16)
        out = jnp.dot(hid, w2_ref[0], preferred_element_type=F32)
        ms = jnp.mean(out * out, axis=-1, keepdims=True)
        normed = out * lax.rsqrt(ms + EPS) * knw_ref[...]
        o_ref[0, 0, g] = jnp.where(is_key, normed, out)
        ot_ref[0, 0, g] = lax.dot_general(w2t_ref[0], hid, NT_DIMS, preferred_element_type=F32)


def _compress(xc, pos, w1, w2, w2t, knw, B, ncb):
    return pl.pallas_call(
        _cmp_kernel,
        grid=(B, 2),
        in_specs=[
            pl.BlockSpec((1, NSA_KV_HEADS, ncb, CMP_STRIDE * HEAD_DIM), lambda b, s: (b, s, 0, 0)),
            pl.BlockSpec((1, 2, CMP_STRIDE * HEAD_DIM), lambda b, s: (s, 0, 0)),
            pl.BlockSpec((1, CMP_BLOCK * HEAD_DIM, CMP_HIDDEN), lambda b, s: (s, 0, 0)),
            pl.BlockSpec((1, CMP_HIDDEN, HEAD_DIM), lambda b, s: (s, 0, 0)),
            pl.BlockSpec((1, HEAD_DIM, CMP_HIDDEN), lambda b, s: (s, 0, 0)),
            pl.BlockSpec((1, HEAD_DIM), lambda b, s: (0, 0)),
        ],
        out_specs=[
            pl.BlockSpec((1, 1, NSA_KV_HEADS, ncb, HEAD_DIM), lambda b, s: (b, s, 0, 0, 0)),
            pl.BlockSpec((1, 1, NSA_KV_HEADS, HEAD_DIM, ncb), lambda b, s: (b, s, 0, 0, 0)),
        ],
        out_shape=[
            jax.ShapeDtypeStruct((B, 2, NSA_KV_HEADS, ncb, HEAD_DIM), F32),
            jax.ShapeDtypeStruct((B, 2, NSA_KV_HEADS, HEAD_DIM, ncb), F32),
        ],
        compiler_params=pltpu.CompilerParams(dimension_semantics=("parallel", "parallel"),
                                             vmem_limit_bytes=VMEM_LIMIT),
        name="compress",
    )(xc, pos, w1, w2, w2t, knw)


def _kprep_kernel(x_ref, w_ref, ks_ref, kw_ref):
    lane = lax.broadcasted_iota(jnp.int32, (1, PAIR), 1)
    head0 = lane < HEAD_DIM
    m0 = jnp.where(head0, 1.0, 0.0)
    m1 = 1.0 - m0
    t0 = pl.program_id(1) * KPREP_TM
    tok = t0 + lax.broadcasted_iota(jnp.int32, (KPREP_TM, PAIR), 0)
    col = lax.broadcasted_iota(jnp.int32, (KPREP_TM, PAIR), 1)
    indicator = jnp.where((tok >> SLC_SHIFT) == col - HEAD_DIM, 1.0, 0.0)

    def normed(x, w):
        x2 = x * x
        ms0 = jnp.sum(x2 * m0, axis=-1, keepdims=True)
        ms1 = jnp.sum(x2 * m1, axis=-1, keepdims=True)
        ms = jnp.where(head0, ms0, ms1) * (1.0 / HEAD_DIM)
        return x * lax.rsqrt(ms + EPS) * w

    ns = normed(x_ref[0, :, :PAIR], w_ref[0:1, :])
    nw = normed(x_ref[0, :, PAIR:], w_ref[1:2, :])
    for g in range(NSA_KV_HEADS):
        s_g = ns if g == 0 else pltpu.roll(ns, HEAD_DIM, 1)
        w_g = nw if g == 0 else pltpu.roll(nw, HEAD_DIM, 1)
        ks_ref[0, g] = jnp.where(head0, s_g, indicator).astype(BF16)
        kw_ref[0, g] = jnp.where(head0, w_g, 0.0).astype(BF16)


def _kprep(skwk3, w, B, T):
    return pl.pallas_call(
        _kprep_kernel,
        grid=(B, T // KPREP_TM),
        in_specs=[pl.BlockSpec((1, KPREP_TM, 2 * PAIR), lambda b, i: (b, i, 0)),
                  pl.BlockSpec((2, PAIR), lambda b, i: (0, 0))],
        out_specs=[pl.BlockSpec((1, NSA_KV_HEADS, KPREP_TM, PAIR), lambda b, i: (b, 0, i, 0)),
                   pl.BlockSpec((1, NSA_KV_HEADS, KPREP_TM, PAIR), lambda b, i: (b, 0, i, 0))],
        out_shape=[jax.ShapeDtypeStruct((B, NSA_KV_HEADS, T, PAIR), BF16),
                   jax.ShapeDtypeStruct((B, NSA_KV_HEADS, T, PAIR), BF16)],
        compiler_params=pltpu.CompilerParams(dimension_semantics=("parallel", "parallel"),
                                             vmem_limit_bytes=VMEM_LIMIT),
        name="kprep",
    )(skwk3, w)


def _nsa_kernel(qt_ref, glt_ref, ng_ref, ks_ref, kw_ref, vst_ref, vwt_ref, kc_ref, vct_ref, mt_ref, qnw_ref, bg_ref,
                o_ref, qp_ref, sa_ref, sb_ref, sw_ref, m_ref, l_ref, acc_ref):
    qi = pl.program_id(2)
    t0 = qi * NSA_TQ
    ncols = NSA_GROUP * NSA_TQ

    cols = []
    for r in range(NSA_GROUP):
        q = qt_ref[0, 0, r * HEAD_DIM:(r + 1) * HEAD_DIM, :]
        ms = jnp.mean(q * q, axis=0, keepdims=True)
        cols.append(q * lax.rsqrt(ms + EPS) * qnw_ref[...] * QK_SCALE)
    qs = jnp.concatenate(cols, axis=1).astype(BF16)
    qp_ref[0:HEAD_DIM, :] = qs

    kt0 = jnp.maximum(qi - WIN_SIZE // NSA_TQ, 0)
    ks0 = pl.multiple_of(kt0 * NSA_TQ, NSA_TQ)
    sw_ref[...] = jnp.dot(kw_ref[0, 0, pl.ds(ks0, WIN_KEYS), 0:HEAD_DIM], qs, preferred_element_type=F32)

    ncb = kc_ref.shape[3]
    sc = jnp.dot(kc_ref[0, 0, 0].astype(BF16), qs, preferred_element_type=F32)
    n_idx = lax.broadcasted_iota(jnp.int32, (ncb, NSA_TQ), 0)
    tok_c = t0 + lax.broadcasted_iota(jnp.int32, (ncb, NSA_TQ), 1)
    cbias = jnp.where((n_idx * CMP_STRIDE + (CMP_BLOCK - 1)) <= tok_c, 0.0, NEG)
    sc = sc + _tile4(cbias)
    mc = jnp.max(sc, axis=0, keepdims=True)
    ec = jnp.exp(sc - mc)
    lc = jnp.sum(ec, axis=0, keepdims=True)
    tok_row = t0 + lax.broadcasted_iota(jnp.int32, (1, NSA_TQ), 1)
    has_block = _tile4(jnp.where(tok_row >= CMP_BLOCK - 1, 1.0, 0.0))
    p = ec * (has_block / lc)
    oc_t = jnp.dot(vct_ref[0, 0, 0].astype(BF16), p.astype(BF16), preferred_element_type=F32)

    ps = p[:, 0:NSA_TQ]
    for r in range(1, NSA_GROUP):
        ps = ps + p[:, r * NSA_TQ:(r + 1) * NSA_TQ]
    ps_hi = ps.astype(BF16)
    ps_lo = (ps - ps_hi.astype(F32)).astype(BF16)
    imp = (jnp.dot(mt_ref[...], ps_hi, preferred_element_type=F32)
           + jnp.dot(mt_ref[...], ps_lo, preferred_element_type=F32))
    n_slc = mt_ref.shape[0]
    jb = lax.broadcasted_iota(jnp.int32, (n_slc, NSA_TQ), 0)
    tok_s = t0 + lax.broadcasted_iota(jnp.int32, (n_slc, NSA_TQ), 1)
    valid_s = jb * SLC_BLOCK <= tok_s
    force = (jb == (tok_s >> SLC_SHIFT)) | (jb == 0)
    score = jnp.where(valid_s, jnp.where(force, imp + FORCE_BONUS, imp), NEG)
    sub = lax.broadcasted_iota(jnp.int32, (SUBLANES, NSA_TQ), 0)
    groups = [score[v * SUBLANES:(v + 1) * SUBLANES, :] for v in range(n_slc // SUBLANES)]
    ranks = [jnp.zeros((SUBLANES, NSA_TQ), F32) for _ in groups]
    for i in range(n_slc):
        row = score[i:i + 1, :]
        for v, blk in enumerate(groups):
            if v * SUBLANES > i:
                beats = row >= blk
            elif (v + 1) * SUBLANES <= i:
                beats = row > blk
            else:
                beats = (row > blk) | ((row >= blk) & (sub > i - v * SUBLANES))
            ranks[v] = ranks[v] + jnp.where(beats, 1.0, 0.0)
    rank = jnp.concatenate(ranks, axis=0)
    sel = (rank < float(SLC_TOPK)) & valid_s
    bias = jnp.where(sel, 0.0, NEG).astype(BF16)
    qp_ref[HEAD_DIM:2 * HEAD_DIM, :] = _tile4(bias)

    vt_per_tile = SLC_TK // LANES

    def slc_scores(j, dst_ref):
        kst = pl.multiple_of(j * SLC_TK, SLC_TK)
        dst_ref[...] = jnp.dot(ks_ref[0, 0, pl.ds(kst, SLC_TK), :], qp_ref[...], preferred_element_type=F32)

    def slc_update(j, src_ref, causal):
        s = src_ref[...]
        if causal:
            r_s = lax.broadcasted_iota(jnp.int32, (SLC_TK, NSA_TQ), 0)
            c_s = lax.broadcasted_iota(jnp.int32, (SLC_TK, NSA_TQ), 1)
            s = s + _tile4(jnp.where(j * SLC_TK + r_s <= t0 + c_s, 0.0, NEG))
        m_old = m_ref[...]
        m_new = jnp.maximum(m_old, jnp.max(s, axis=0, keepdims=True))
        alpha = jnp.exp(m_old - m_new)
        e = jnp.exp(s - m_new)
        l_ref[...] = alpha * l_ref[...] + jnp.sum(e, axis=0, keepdims=True)
        eb = e.astype(BF16)
        pv = jnp.zeros((HEAD_DIM, ncols), F32)
        for jj in range(vt_per_tile):
            pv = pv + jnp.dot(vst_ref[0, j * vt_per_tile + jj], eb[jj * LANES:(jj + 1) * LANES, :],
                              preferred_element_type=F32)
        acc_ref[...] = alpha * acc_ref[...] + pv
        m_ref[...] = m_new

    slc_scores(0, sa_ref)

    r_w = lax.broadcasted_iota(jnp.int32, (WIN_KEYS, NSA_TQ), 0)
    c_w = lax.broadcasted_iota(jnp.int32, (WIN_KEYS, NSA_TQ), 1)
    delta = (t0 - ks0) + c_w - r_w
    sw = sw_ref[...] + _tile4(jnp.where((delta >= 0) & (delta < WIN_SIZE), 0.0, NEG))
    mw = jnp.max(sw, axis=0, keepdims=True)
    ew = jnp.exp(sw - mw)
    lw = jnp.sum(ew, axis=0, keepdims=True)
    ewb = ew.astype(BF16)
    ow_t = jnp.zeros((HEAD_DIM, ncols), F32)
    for j in range(WIN_KEYS // LANES):
        ow_t = ow_t + jnp.dot(vwt_ref[0, kt0 + j], ewb[j * LANES:(j + 1) * LANES, :], preferred_element_type=F32)
    ow_t = ow_t * (1.0 / lw)

    m_ref[...] = jnp.full(m_ref.shape, NEG, F32)
    l_ref[...] = jnp.zeros(l_ref.shape, F32)
    acc_ref[...] = jnp.zeros(acc_ref.shape, F32)
    n_full = t0 // SLC_TK

    def pair(jj, carry):
        j = 2 * jj
        slc_scores(j + 1, sb_ref)
        slc_update(j, sa_ref, False)
        slc_scores(j + 2, sa_ref)
        slc_update(j + 1, sb_ref, False)
        return carry

    lax.fori_loop(0, n_full // 2, pair, 0)

    @pl.when(n_full % 2 == 1)
    def _():
        slc_scores(n_full, sb_ref)
        slc_update(n_full - 1, sa_ref, False)
        slc_update(n_full, sb_ref, True)

    @pl.when(n_full % 2 == 0)
    def _():
        slc_update(n_full, sa_ref, True)

    os_t = acc_ref[...] * (1.0 / l_ref[...])

    gates = _sigmoid(glt_ref[0, 0] + bg_ref[...])
    outs = []
    for r in range(NSA_GROUP):
        sl = slice(r * NSA_TQ, (r + 1) * NSA_TQ)
        outs.append(gates[r:r + 1, :] * oc_t[:, sl]
                    + gates[NSA_GROUP + r:NSA_GROUP + r + 1, :] * os_t[:, sl]
                    + gates[2 * NSA_GROUP + r:2 * NSA_GROUP + r + 1, :] * ow_t[:, sl])
    o_tok = jnp.concatenate(outs, axis=0).T
    ng = ng_ref[0]
    o_ref[0] = o_tok * (ng * _sigmoid(ng))


def _nsa(nqt, glt, ng3, ks, kw, vt, kc, vct, mt, qnw, bg, B, T):
    nt = T // LANES
    ncb = kc.shape[3]
    gw = NSA_GROUP * HEAD_DIM
    return pl.pallas_call(
        _nsa_kernel,
        grid=(B, NSA_KV_HEADS, T // NSA_TQ),
        in_specs=[
            pl.BlockSpec((1, 1, gw, LANES), lambda b, g, i: (b, i, g, 0)),
            pl.BlockSpec((1, 1, 16, LANES), lambda b, g, i: (b, i, g, 0)),
            pl.BlockSpec((1, NSA_TQ, gw), lambda b, g, i: (b, i, g)),
            pl.BlockSpec((1, 1, T, PAIR), lambda b, g, i: (b, g, 0, 0)),
            pl.BlockSpec((1, 1, T, PAIR), lambda b, g, i: (b, g, 0, 0)),
            pl.BlockSpec((1, nt, HEAD_DIM, LANES), lambda b, g, i: (b, 0, g, 0)),
            pl.BlockSpec((1, nt, HEAD_DIM, LANES), lambda b, g, i: (b, 0, NSA_KV_HEADS + g, 0)),
            pl.BlockSpec((1, 1, 1, ncb, HEAD_DIM), lambda b, g, i: (b, 0, g, 0, 0)),
            pl.BlockSpec((1, 1, 1, HEAD_DIM, ncb), lambda b, g, i: (b, 1, g, 0, 0)),
            pl.BlockSpec(mt.shape, lambda b, g, i: (0, 0)),
            pl.BlockSpec((HEAD_DIM, 1), lambda b, g, i: (0, 0)),
            pl.BlockSpec((16, 1), lambda b, g, i: (g, 0)),
        ],
        out_specs=pl.BlockSpec((1, NSA_TQ, gw), lambda b, g, i: (b, i, g)),
        out_shape=jax.ShapeDtypeStruct((B, T, D_NSA), F32),
        scratch_shapes=[
            pltpu.VMEM((2 * HEAD_DIM, NSA_GROUP * NSA_TQ), BF16),
            pltpu.VMEM((SLC_TK, NSA_GROUP * NSA_TQ), F32),
            pltpu.VMEM((SLC_TK, NSA_GROUP * NSA_TQ), F32),
            pltpu.VMEM((WIN_KEYS, NSA_GROUP * NSA_TQ), F32),
            pltpu.VMEM((1, NSA_GROUP * NSA_TQ), F32),
            pltpu.VMEM((1, NSA_GROUP * NSA_TQ), F32),
            pltpu.VMEM((HEAD_DIM, NSA_GROUP * NSA_TQ), F32),
        ],
        compiler_params=pltpu.CompilerParams(dimension_semantics=("parallel", "parallel", "arbitrary"),
                                             vmem_limit_bytes=VMEM_LIMIT),
        name="nsa",
    )(nqt, glt, ng3, ks, kw, vt, vt, kc, vct, mt, qnw, bg)


def _out_kernel(x_ref, yr_ref, yn_ref, wr_ref, wn_ref, o_ref):
    o_ref[...] = (x_ref[...]
                  + jnp.dot(yr_ref[...].astype(BF16), wr_ref[...], preferred_element_type=F32)
                  + jnp.dot(yn_ref[...].astype(BF16), wn_ref[...], preferred_element_type=F32))


def _outproj(x2, yr, yn, wr, wn):
    N = x2.shape[0]
    row = lambda i: (i, 0)
    const = lambda i: (0, 0)
    return pl.pallas_call(
        _out_kernel,
        grid=(N // OUT_TM,),
        in_specs=[pl.BlockSpec((OUT_TM, D_MODEL), row), pl.BlockSpec((OUT_TM, D_RET), row),
                  pl.BlockSpec((OUT_TM, D_NSA), row), pl.BlockSpec(wr.shape, const), pl.BlockSpec(wn.shape, const)],
        out_specs=pl.BlockSpec((OUT_TM, D_MODEL), row),
        out_shape=jax.ShapeDtypeStruct((N, D_MODEL), F32),
        compiler_params=pltpu.CompilerParams(dimension_semantics=("parallel",), vmem_limit_bytes=VMEM_LIMIT),
        name="outproj",
    )(x2, yr, yn, wr, wn)


@functools.lru_cache(maxsize=None)
def _tables(T):
    half = HEAD_DIM // 2
    inv = ROPE_THETA ** (-np.arange(half, dtype=np.float64) / half)
    ang = np.arange(T, dtype=np.float64)[:, None] * inv[None, :]
    cos64 = np.concatenate([np.cos(ang), np.cos(ang)], axis=1)
    sin64 = np.concatenate([-np.sin(ang), np.sin(ang)], axis=1)
    cos = np.concatenate([cos64, cos64], axis=1).astype(np.float32)
    sin = np.concatenate([sin64, sin64], axis=1).astype(np.float32)

    C = RET_CHUNK
    log_g = np.log1p(-np.exp2(-5.0 - np.arange(RET_HEADS, dtype=np.float64)))
    pos = np.arange(C, dtype=np.float64)
    diff = pos[:, None] - pos[None, :]
    decay = np.where(diff >= 0, np.exp(log_g[:, None, None] * np.maximum(diff, 0.0)), 0.0)
    zeta = np.exp(log_g[:, None] * (C - 1.0 - pos))
    xi = np.exp(log_g[:, None] * (pos + 1.0))
    g_chunk = np.exp(log_g * C)
    npair = RET_HEADS // 2

    def pair_lanes(a):
        return np.repeat(a.reshape(npair, 2, C).transpose(0, 2, 1), HEAD_DIM, axis=2).astype(np.float32)

    dec = decay.reshape(npair, 2, C, C).astype(np.float32)
    gch = np.repeat(g_chunk.reshape(npair, 1, 2), HEAD_DIM, axis=2).astype(np.float32)

    n_cmp = (T - CMP_BLOCK) // CMP_STRIDE + 1
    ncb = T // CMP_STRIDE
    p = np.arange(n_cmp)[:, None] * CMP_STRIDE + np.arange(CMP_BLOCK)[None, :]
    blk = p // SLC_BLOCK
    M = (blk[:, :, None] == np.arange(T // SLC_BLOCK)[None, None, :]).mean(axis=1)
    mt = np.zeros((T // SLC_BLOCK, ncb), np.float32)
    mt[:, :n_cmp] = M.T
    return cos, sin, dec, pair_lanes(zeta), pair_lanes(xi), gch, mt


def kernel(x, norm_w, w_in, ret_norm_w, q_norm_w, k_norm_cmp, k_norm_slc, k_norm_win, cmp_pos_k, cmp_w1_k, cmp_w2_k,
           cmp_pos_v, cmp_w1_v, cmp_w2_v, b_gate, w_out):
    B, T, D = x.shape
    depth = norm_w.shape[0]
    cos, sin, dec, zeta, xi, gch, mt = _tables(T)
    ncb = T // CMP_STRIDE
    half = CMP_STRIDE * HEAD_DIM
    gate_src = np.zeros((NSA_KV_HEADS, 16), np.int32)
    gate_ok = np.zeros((NSA_KV_HEADS, 16), bool)
    for g in range(NSA_KV_HEADS):
        for br in range(N_BRANCH):
            for r in range(NSA_GROUP):
                gate_src[g, br * NSA_GROUP + r] = br * NSA_HEADS + g * NSA_GROUP + r
                gate_ok[g, br * NSA_GROUP + r] = True
    gate_src = gate_src.reshape(-1)
    gate_ok = gate_ok.reshape(-1)

    x2 = x.reshape(B * T, D)
    for layer in range(depth):
        w = w_in[layer].astype(BF16)
        o_ng = 4 * D_RET + D_NSA
        o_kv = o_ng + D_NSA
        w_ret = w[:, :4 * D_RET]
        wt_q = w[:, 4 * D_RET:o_ng].T
        w_ng = w[:, o_ng:o_kv]
        w_ckv = w[:, o_kv:o_kv + 2 * D_KV]
        w_skwk = jnp.concatenate([w[:, o_kv + 2 * D_KV:o_kv + 3 * D_KV], w[:, o_kv + 4 * D_KV:o_kv + 5 * D_KV]], axis=1)
        wt_v = jnp.concatenate([w[:, o_kv + 3 * D_KV:o_kv + 4 * D_KV], w[:, o_kv + 5 * D_KV:o_kv + 6 * D_KV]], axis=1).T
        w_gl = w[:, o_kv + 6 * D_KV:]
        wt_g = jnp.where(gate_ok[:, None], w_gl.T[gate_src], jnp.zeros((), BF16))
        bg = jnp.where(gate_ok, b_gate[layer][gate_src], 0.0).reshape(-1, 1)

        ret, ng, ckv, skwk, nqt, vt, glt = _proj(x2, norm_w[layer].reshape(1, D), w_ret, w_ng, w_ckv, w_skwk,
                                                 wt_q, wt_v, wt_g, B, T)

        nw_pair = ret_norm_w[layer].reshape(RET_HEADS // 2, 1, PAIR)
        y_ret = _retention(ret.reshape(B, T, 4 * D_RET), jnp.asarray(cos), jnp.asarray(sin), jnp.asarray(dec),
                           jnp.asarray(zeta), jnp.asarray(xi), jnp.asarray(gch), nw_pair, B, T)

        xc = ckv.reshape(B, T, 2 * NSA_KV_HEADS, HEAD_DIM).transpose(0, 2, 1, 3).reshape(B, 2 * NSA_KV_HEADS, ncb, half)
        pos = jnp.stack([cmp_pos_k[layer], cmp_pos_v[layer]]).reshape(2, 2, half)
        w1 = jnp.stack([cmp_w1_k[layer], cmp_w1_v[layer]]).astype(BF16)
        w2 = jnp.stack([cmp_w2_k[layer], cmp_w2_v[layer]]).astype(BF16)
        w2t = jnp.swapaxes(w2, 1, 2)
        kc, vct = _compress(xc, pos, w1, w2, w2t, k_norm_cmp[layer].reshape(1, HEAD_DIM), B, ncb)

        knw = jnp.stack([jnp.tile(k_norm_slc[layer], 2), jnp.tile(k_norm_win[layer], 2)])
        ks, kw = _kprep(skwk.reshape(B, T, 2 * PAIR), knw, B, T)

        y_nsa = _nsa(nqt, glt, ng.reshape(B, T, D_NSA), ks, kw, vt, kc, vct, jnp.asarray(mt).astype(BF16),
                     q_norm_w[layer].reshape(HEAD_DIM, 1), bg, B, T)

        wo = w_out[layer].astype(BF16)
        x2 = _outproj(x2, y_ret.reshape(B * T, D_RET), y_nsa.reshape(B * T, D_NSA), wo[:D_RET], wo[D_RET:])
    return x2.reshape(B, T, D)
```

```python
import functools

import numpy as np
import jax
import jax.numpy as jnp
from jax import lax
from jax.experimental import pallas as pl
from jax.experimental.pallas import tpu as pltpu

F32 = jnp.float32
BF16 = jnp.bfloat16

D_MODEL = 1024
HEAD_DIM = 64
HEAD_SHIFT = 6
RET_HEADS = 8
NSA_HEADS = 8
NSA_KV_HEADS = 2
NSA_GROUP = NSA_HEADS // NSA_KV_HEADS
D_RET = RET_HEADS * HEAD_DIM
D_NSA = NSA_HEADS * HEAD_DIM
D_KV = NSA_KV_HEADS * HEAD_DIM
N_BRANCH = 3
RET_CHUNK = 128
ROPE_THETA = 10000.0
CMP_BLOCK = 32
CMP_STRIDE = 16
CMP_HIDDEN = 256
SLC_BLOCK = 64
SLC_SHIFT = 6
SLC_TOPK = 16
WIN_SIZE = 512
EPS = 1e-6
NEG = -1e30
FORCE_BONUS = 1e4
QK_SCALE = HEAD_DIM ** -0.5
LOG2E = 1.4426950408889634
V_ROWS = HEAD_DIM + 16

LANES = 128
SUBLANES = 8
PAIR = 2 * HEAD_DIM
VMEM_LIMIT = 48 * 1024 * 1024

PROJ_TM = 256
RET_TC = 1024
KPREP_TM = 512
NSA_TQ = 128
SLC_TK = 512
WIN_KEYS = WIN_SIZE + NSA_TQ
OUT_TM = 512

NT_DIMS = (((1,), (1,)), ((), ()))
TN_DIMS = (((0,), (0,)), ((), ()))


def _sigmoid(x):
    return 1.0 / (1.0 + jnp.exp(-x))


def _tile4(a):
    return jnp.concatenate([a, a, a, a], axis=1)


def _proj_kernel(x_ref, nw_ref, w_ret_ref, w_ng_ref, w_ckv_ref, w_skwk_ref, wt_q_ref, wt_v_ref, wt_g_ref,
                 ret_ref, ng_ref, ckv_ref, skwk_ref, nqt_ref, vt_ref, glt_ref):
    x = x_ref[...]
    ms = jnp.mean(x * x, axis=-1, keepdims=True)
    h = (x * lax.rsqrt(ms + EPS) * nw_ref[...]).astype(BF16)
    ret_ref[...] = jnp.dot(h, w_ret_ref[...], preferred_element_type=F32)
    ng_ref[...] = jnp.dot(h, w_ng_ref[...], preferred_element_type=F32)
    ckv_ref[...] = jnp.dot(h, w_ckv_ref[...], preferred_element_type=F32)
    skwk_ref[...] = jnp.dot(h, w_skwk_ref[...], preferred_element_type=F32)
    qt = lax.dot_general(wt_q_ref[...], h, NT_DIMS, preferred_element_type=F32)
    vt = lax.dot_general(wt_v_ref[...], h, NT_DIMS, preferred_element_type=F32)
    gt = lax.dot_general(wt_g_ref[...], h, NT_DIMS, preferred_element_type=F32)
    pad_row = lax.broadcasted_iota(jnp.int32, (V_ROWS - HEAD_DIM, LANES), 0)
    ones_pad = jnp.where(pad_row == 0, 1.0, 0.0).astype(BF16)
    for j in range(PROJ_TM // LANES):
        sl = slice(j * LANES, (j + 1) * LANES)
        nqt_ref[0, j] = qt[:, sl]
        for blk in range(2 * NSA_KV_HEADS):
            vt_ref[0, j, blk * V_ROWS:blk * V_ROWS + HEAD_DIM, :] = (
                vt[blk * HEAD_DIM:(blk + 1) * HEAD_DIM, sl].astype(BF16))
            vt_ref[0, j, blk * V_ROWS + HEAD_DIM:(blk + 1) * V_ROWS, :] = ones_pad
        glt_ref[0, j] = gt[:, sl]


def _proj(x2, nw, w_ret, w_ng, w_ckv, w_skwk, wt_q, wt_v, wt_g, B, T):
    N = B * T
    tpb = T // PROJ_TM
    sub = PROJ_TM // LANES
    nt = T // LANES
    const = lambda i: (0, 0)
    row = lambda i: (i, 0)
    trn = lambda i: (i // tpb, i % tpb, 0, 0)
    return pl.pallas_call(
        _proj_kernel,
        grid=(N // PROJ_TM,),
        in_specs=[
            pl.BlockSpec((PROJ_TM, D_MODEL), row),
            pl.BlockSpec((1, D_MODEL), const),
            pl.BlockSpec(w_ret.shape, const),
            pl.BlockSpec(w_ng.shape, const),
            pl.BlockSpec(w_ckv.shape, const),
            pl.BlockSpec(w_skwk.shape, const),
            pl.BlockSpec(wt_q.shape, const),
            pl.BlockSpec(wt_v.shape, const),
            pl.BlockSpec(wt_g.shape, const),
        ],
        out_specs=[
            pl.BlockSpec((PROJ_TM, 4 * D_RET), row),
            pl.BlockSpec((PROJ_TM, D_NSA), row),
            pl.BlockSpec((PROJ_TM, 2 * D_KV), row),
            pl.BlockSpec((PROJ_TM, 2 * D_KV), row),
            pl.BlockSpec((1, sub, D_NSA, LANES), trn),
            pl.BlockSpec((1, sub, 2 * NSA_KV_HEADS * V_ROWS, LANES), trn),
            pl.BlockSpec((1, sub, 32, LANES), trn),
        ],
        out_shape=[
            jax.ShapeDtypeStruct((N, 4 * D_RET), F32),
            jax.ShapeDtypeStruct((N, D_NSA), F32),
            jax.ShapeDtypeStruct((N, 2 * D_KV), F32),
            jax.ShapeDtypeStruct((N, 2 * D_KV), F32),
            jax.ShapeDtypeStruct((B, nt, D_NSA, LANES), F32),
            jax.ShapeDtypeStruct((B, nt, 2 * NSA_KV_HEADS * V_ROWS, LANES), BF16),
            jax.ShapeDtypeStruct((B, nt, 32, LANES), F32),
        ],
        compiler_params=pltpu.CompilerParams(dimension_semantics=("parallel",), vmem_limit_bytes=VMEM_LIMIT),
        name="proj",
    )(x2, nw, w_ret, w_ng, w_ckv, w_skwk, wt_q, wt_v, wt_g)


def _ret_kernel(q_ref, k_ref, v_ref, g_ref, cos_ref, sin_ref, dec_ref, zeta_ref, xi_ref, gch_ref, nw_ref,
                o_ref, state_ref):
    @pl.when(pl.program_id(2) == 0)
    def _():
        state_ref[...] = jnp.zeros_like(state_ref)

    lane = lax.broadcasted_iota(jnp.int32, (1, PAIR), 1)
    head0 = lane < HEAD_DIM
    m0 = jnp.where(head0, 1.0, 0.0)
    m1 = 1.0 - m0
    first_half = (lane & (HEAD_DIM - 1)) < (HEAD_DIM // 2)
    rr = lax.broadcasted_iota(jnp.int32, (PAIR, PAIR), 0) >> HEAD_SHIFT
    cc = lax.broadcasted_iota(jnp.int32, (PAIR, PAIR), 1) >> HEAD_SHIFT
    blockdiag = jnp.where(rr == cc, 1.0, 0.0)
    zeta = zeta_ref[0]
    xi = xi_ref[0]
    gch = gch_ref[0]
    nw = nw_ref[0]

    for c in range(RET_TC // RET_CHUNK):
        sl = pl.ds(c * RET_CHUNK, RET_CHUNK)
        q = q_ref[0, sl, :]
        k = k_ref[0, sl, :]
        v = v_ref[0, sl, :]
        g = g_ref[0, sl, :]
        cos = cos_ref[sl, :]
        sin = sin_ref[sl, :]

        def rope(t):
            rot = jnp.where(first_half, pltpu.roll(t, PAIR - HEAD_DIM // 2, 1), pltpu.roll(t, HEAD_DIM // 2, 1))
            return t * cos + rot * sin

        qr = rope(q)
        kb = (rope(k) * QK_SCALE).astype(BF16)
        state = state_ref[...]
        o = jnp.dot(qr.astype(BF16), state.astype(BF16), preferred_element_type=F32) * xi
        for h, mh in ((0, m0), (1, m1)):
            s = lax.dot_general((qr * mh).astype(BF16), kb, NT_DIMS, preferred_element_type=F32) * dec_ref[0, h]
            o = o + jnp.dot(s.astype(BF16), (v * mh).astype(BF16), preferred_element_type=F32)
        kv = lax.dot_general(kb, (v * zeta).astype(BF16), TN_DIMS, preferred_element_type=F32)
        state_ref[...] = state * gch + kv * blockdiag

        o2 = o * o
        ms0 = jnp.sum(o2 * m0, axis=-1, keepdims=True)
        ms1 = jnp.sum(o2 * m1, axis=-1, keepdims=True)
        ms = jnp.where(head0, ms0, ms1) * (1.0 / HEAD_DIM)
        y = o * lax.rsqrt(ms + EPS) * nw
        o_ref[0, sl, :] = (y * (g * _sigmoid(g))).astype(BF16)


def _retention(ret3, cos, sin, dec, zeta, xi, gch, nw, B, T):
    npair = RET_HEADS // 2
    blk = lambda col0: pl.BlockSpec((1, RET_TC, PAIR), lambda b, p, i: (b, i, col0 + p))
    tab = pl.BlockSpec((RET_TC, PAIR), lambda b, p, i: (i, 0))
    per_pair3 = lambda shape: pl.BlockSpec((1,) + shape, lambda b, p, i: (p,) + (0,) * len(shape))
    return pl.pallas_call(
        _ret_kernel,
        grid=(B, npair, T // RET_TC),
        in_specs=[blk(0), blk(npair), blk(2 * npair), blk(3 * npair), tab, tab,
                  per_pair3((2, RET_CHUNK, RET_CHUNK)), per_pair3((RET_CHUNK, PAIR)), per_pair3((RET_CHUNK, PAIR)),
                  per_pair3((1, PAIR)), per_pair3((1, PAIR))],
        out_specs=pl.BlockSpec((1, RET_TC, PAIR), lambda b, p, i: (b, i, p)),
        out_shape=jax.ShapeDtypeStruct((B, T, D_RET), BF16),
        scratch_shapes=[pltpu.VMEM((PAIR, PAIR), F32)],
        compiler_params=pltpu.CompilerParams(dimension_semantics=("parallel", "parallel", "arbitrary"),
                                             vmem_limit_bytes=VMEM_LIMIT),
        name="retention",
    )(ret3, ret3, ret3, ret3, cos, sin, dec, zeta, xi, gch, nw)


def _cmp_kernel(x_ref, pos_ref, w1_ref, w2_ref, w2t_ref, knw_ref, o_ref, ot_ref):
    is_key = pl.program_id(1) == 0
    half = CMP_STRIDE * HEAD_DIM
    for g in range(NSA_KV_HEADS):
        x = x_ref[0, g]
        a = jnp.dot((x + pos_ref[0, 0:1, :]).astype(BF16), w1_ref[0, :half, :], preferred_element_type=F32)
        b = jnp.dot((x + pos_ref[0, 1:2, :]).astype(BF16), w1_ref[0, half:, :], preferred_element_type=F32)
        hid = a + pltpu.roll(b, b.shape[0] - 1, 0)
        hid = (hid * _sigmoid(hid)).astype(BF16)
        out = jnp.dot(hid, w2_ref[0], preferred_element_type=F32)
        ms = jnp.mean(out * out, axis=-1, keepdims=True)
        normed = out * lax.rsqrt(ms + EPS) * knw_ref[...]
        o_ref[0, 0, g] = jnp.where(is_key, normed, out)
        ot_ref[0, 0, g] = lax.dot_general(w2t_ref[0], hid, NT_DIMS, preferred_element_type=F32)


def _compress(xc, pos, w1, w2, w2t, knw, B, ncb):
    return pl.pallas_call(
        _cmp_kernel,
        grid=(B, 2),
        in_specs=[
            pl.BlockSpec((1, NSA_KV_HEADS, ncb, CMP_STRIDE * HEAD_DIM), lambda b, s: (b, s, 0, 0)),
            pl.BlockSpec((1, 2, CMP_STRIDE * HEAD_DIM), lambda b, s: (s, 0, 0)),
            pl.BlockSpec((1, CMP_BLOCK * HEAD_DIM, CMP_HIDDEN), lambda b, s: (s, 0, 0)),
            pl.BlockSpec((1, CMP_HIDDEN, HEAD_DIM), lambda b, s: (s, 0, 0)),
            pl.BlockSpec((1, HEAD_DIM, CMP_HIDDEN), lambda b, s: (s, 0, 0)),
            pl.BlockSpec((1, HEAD_DIM), lambda b, s: (0, 0)),
        ],
        out_specs=[
            pl.BlockSpec((1, 1, NSA_KV_HEADS, ncb, HEAD_DIM), lambda b, s: (b, s, 0, 0, 0)),
            pl.BlockSpec((1, 1, NSA_KV_HEADS, HEAD_DIM, ncb), lambda b, s: (b, s, 0, 0, 0)),
        ],
        out_shape=[
            jax.ShapeDtypeStruct((B, 2, NSA_KV_HEADS, ncb, HEAD_DIM), F32),
            jax.ShapeDtypeStruct((B, 2, NSA_KV_HEADS, HEAD_DIM, ncb), F32),
        ],
        compiler_params=pltpu.CompilerParams(dimension_semantics=("parallel", "parallel"),
                                             vmem_limit_bytes=VMEM_LIMIT),
        name="compress",
    )(xc, pos, w1, w2, w2t, knw)


def _kprep_kernel(x_ref, w_ref, ks_ref, kw_ref):
    lane = lax.broadcasted_iota(jnp.int32, (1, PAIR), 1)
    head0 = lane < HEAD_DIM
    m0 = jnp.where(head0, 1.0, 0.0)
    m1 = 1.0 - m0
    t0 = pl.program_id(1) * KPREP_TM
    tok = t0 + lax.broadcasted_iota(jnp.int32, (KPREP_TM, PAIR), 0)
    col = lax.broadcasted_iota(jnp.int32, (KPREP_TM, PAIR), 1)
    indicator = jnp.where((tok >> SLC_SHIFT) == col - HEAD_DIM, 1.0, 0.0)

    def normed(x, w):
        x2 = x * x
        ms0 = jnp.sum(x2 * m0, axis=-1, keepdims=True)
        ms1 = jnp.sum(x2 * m1, axis=-1, keepdims=True)
        ms = jnp.where(head0, ms0, ms1) * (1.0 / HEAD_DIM)
        return x * lax.rsqrt(ms + EPS) * w

    ns = normed(x_ref[0, :, :PAIR], w_ref[0:1, :])
    nw = normed(x_ref[0, :, PAIR:], w_ref[1:2, :])
    for g in range(NSA_KV_HEADS):
        s_g = ns if g == 0 else pltpu.roll(ns, HEAD_DIM, 1)
        w_g = nw if g == 0 else pltpu.roll(nw, HEAD_DIM, 1)
        ks_ref[0, g] = jnp.where(head0, s_g, indicator).astype(BF16)
        kw_ref[0, g] = jnp.where(head0, w_g, 0.0).astype(BF16)


def _kprep(skwk3, w, B, T):
    return pl.pallas_call(
        _kprep_kernel,
        grid=(B, T // KPREP_TM),
        in_specs=[pl.BlockSpec((1, KPREP_TM, 2 * PAIR), lambda b, i: (b, i, 0)),
                  pl.BlockSpec((2, PAIR), lambda b, i: (0, 0))],
        out_specs=[pl.BlockSpec((1, NSA_KV_HEADS, KPREP_TM, PAIR), lambda b, i: (b, 0, i, 0)),
                   pl.BlockSpec((1, NSA_KV_HEADS, KPREP_TM, PAIR), lambda b, i: (b, 0, i, 0))],
        out_shape=[jax.ShapeDtypeStruct((B, NSA_KV_HEADS, T, PAIR), BF16),
                   jax.ShapeDtypeStruct((B, NSA_KV_HEADS, T, PAIR), BF16)],
        compiler_params=pltpu.CompilerParams(dimension_semantics=("parallel", "parallel"),
                                             vmem_limit_bytes=VMEM_LIMIT),
        name="kprep",
    )(skwk3, w)


def _nsa_kernel(qt_ref, glt_ref, ng_ref, ks_ref, kw_ref, vst_ref, vwt_ref, kc_ref, vct_ref, mt_ref, qnw_ref, bg_ref,
                o_ref, qp_ref, sa_ref, sb_ref, sw_ref, m_ref, acc_ref):
    qi = pl.program_id(2)
    t0 = qi * NSA_TQ
    ncols = NSA_GROUP * NSA_TQ

    cols = []
    for r in range(NSA_GROUP):
        q = qt_ref[0, 0, r * HEAD_DIM:(r + 1) * HEAD_DIM, :]
        ms = jnp.mean(q * q, axis=0, keepdims=True)
        cols.append(q * lax.rsqrt(ms + EPS) * qnw_ref[...] * (QK_SCALE * LOG2E))
    qs = jnp.concatenate(cols, axis=1).astype(BF16)
    qp_ref[0:HEAD_DIM, :] = qs

    kt0 = jnp.maximum(qi - WIN_SIZE // NSA_TQ, 0)
    ks0 = pl.multiple_of(kt0 * NSA_TQ, NSA_TQ)
    sw_ref[...] = jnp.dot(kw_ref[0, 0, pl.ds(ks0, WIN_KEYS), 0:HEAD_DIM], qs, preferred_element_type=F32)

    ncb = kc_ref.shape[3]
    sc = jnp.dot(kc_ref[0, 0, 0].astype(BF16), qs, preferred_element_type=F32)
    n_idx = lax.broadcasted_iota(jnp.int32, (ncb, NSA_TQ), 0)
    tok_c = t0 + lax.broadcasted_iota(jnp.int32, (ncb, NSA_TQ), 1)
    cbias = jnp.where((n_idx * CMP_STRIDE + (CMP_BLOCK - 1)) <= tok_c, 0.0, NEG)
    sc = sc + _tile4(cbias)
    mc = jnp.max(sc, axis=0, keepdims=True)
    ec = jnp.exp2(sc - mc)
    lc = jnp.sum(ec, axis=0, keepdims=True)
    tok_row = t0 + lax.broadcasted_iota(jnp.int32, (1, NSA_TQ), 1)
    has_block = _tile4(jnp.where(tok_row >= CMP_BLOCK - 1, 1.0, 0.0))
    p = ec * (has_block / lc)
    oc_t = jnp.dot(vct_ref[0, 0, 0].astype(BF16), p.astype(BF16), preferred_element_type=F32)

    ps = p[:, 0:NSA_TQ]
    for r in range(1, NSA_GROUP):
        ps = ps + p[:, r * NSA_TQ:(r + 1) * NSA_TQ]
    ps_hi = ps.astype(BF16)
    ps_lo = (ps - ps_hi.astype(F32)).astype(BF16)
    imp = (jnp.dot(mt_ref[...], ps_hi, preferred_element_type=F32)
           + jnp.dot(mt_ref[...], ps_lo, preferred_element_type=F32))
    n_slc = mt_ref.shape[0]
    jb = lax.broadcasted_iota(jnp.int32, (n_slc, NSA_TQ), 0)
    tok_s = t0 + lax.broadcasted_iota(jnp.int32, (n_slc, NSA_TQ), 1)
    valid_s = jb * SLC_BLOCK <= tok_s
    force = (jb == (tok_s >> SLC_SHIFT)) | (jb == 0)
    score = jnp.where(valid_s, jnp.where(force, imp + FORCE_BONUS, imp), NEG)
    sub = lax.broadcasted_iota(jnp.int32, (SUBLANES, NSA_TQ), 0)
    groups = [score[v * SUBLANES:(v + 1) * SUBLANES, :] for v in range(n_slc // SUBLANES)]
    ranks = [jnp.zeros((SUBLANES, NSA_TQ), F32) for _ in groups]
    for i in range(n_slc):
        row = score[i:i + 1, :]
        for v, blk in enumerate(groups):
            if v * SUBLANES > i:
                beats = row >= blk
            elif (v + 1) * SUBLANES <= i:
                beats = row > blk
            else:
                beats = (row > blk) | ((row >= blk) & (sub > i - v * SUBLANES))
            ranks[v] = ranks[v] + jnp.where(beats, 1.0, 0.0)
    rank = jnp.concatenate(ranks, axis=0)
    sel = (rank < float(SLC_TOPK)) & valid_s
    bias = jnp.where(sel, 0.0, NEG).astype(BF16)
    qp_ref[HEAD_DIM:2 * HEAD_DIM, :] = _tile4(bias)

    vt_per_tile = SLC_TK // LANES

    def slc_scores(j, dst_ref):
        kst = pl.multiple_of(j * SLC_TK, SLC_TK)
        dst_ref[...] = jnp.dot(ks_ref[0, 0, pl.ds(kst, SLC_TK), :], qp_ref[...], preferred_element_type=F32)

    def slc_update(j, src_ref, causal):
        s = src_ref[...]
        if causal:
            r_s = lax.broadcasted_iota(jnp.int32, (SLC_TK, NSA_TQ), 0)
            c_s = lax.broadcasted_iota(jnp.int32, (SLC_TK, NSA_TQ), 1)
            s = s + _tile4(jnp.where(j * SLC_TK + r_s <= t0 + c_s, 0.0, NEG))
        m_old = m_ref[...]
        m_new = jnp.maximum(m_old, jnp.max(s, axis=0, keepdims=True))
        alpha = jnp.exp2(m_old - m_new)
        eb = jnp.exp2(s - m_new).astype(BF16)
        vt = jnp.concatenate([vst_ref[0, j * vt_per_tile + jj] for jj in range(vt_per_tile)], axis=1)
        acc_ref[...] = alpha * acc_ref[...] + jnp.dot(vt, eb, preferred_element_type=F32)
        m_ref[...] = m_new

    slc_scores(0, sa_ref)

    r_w = lax.broadcasted_iota(jnp.int32, (WIN_KEYS, NSA_TQ), 0)
    c_w = lax.broadcasted_iota(jnp.int32, (WIN_KEYS, NSA_TQ), 1)
    delta = (t0 - ks0) + c_w - r_w
    sw = sw_ref[...] + _tile4(jnp.where((delta >= 0) & (delta < WIN_SIZE), 0.0, NEG))
    mw = jnp.max(sw, axis=0, keepdims=True)
    ewb = jnp.exp2(sw - mw).astype(BF16)
    vwt = jnp.concatenate([vwt_ref[0, kt0 + j] for j in range(WIN_KEYS // LANES)], axis=1)
    ow_aug = jnp.dot(vwt, ewb, preferred_element_type=F32)
    ow_t = ow_aug[0:HEAD_DIM, :] * (1.0 / ow_aug[HEAD_DIM:HEAD_DIM + 1, :])

    m_ref[...] = jnp.full(m_ref.shape, NEG, F32)
    acc_ref[...] = jnp.zeros(acc_ref.shape, F32)
    n_full = t0 // SLC_TK

    def pair(jj, carry):
        j = 2 * jj
        slc_scores(j + 1, sb_ref)
        slc_update(j, sa_ref, False)
        slc_scores(j + 2, sa_ref)
        slc_update(j + 1, sb_ref, False)
        return carry

    lax.fori_loop(0, n_full // 2, pair, 0)

    @pl.when(n_full % 2 == 1)
    def _():
        slc_scores(n_full, sb_ref)
        slc_update(n_full - 1, sa_ref, False)
        slc_update(n_full, sb_ref, True)

    @pl.when(n_full % 2 == 0)
    def _():
        slc_update(n_full, sa_ref, True)

    os_t = acc_ref[0:HEAD_DIM, :] * (1.0 / acc_ref[HEAD_DIM:HEAD_DIM + 1, :])

    gates = _sigmoid(glt_ref[0, 0] + bg_ref[...])
    outs = []
    for r in range(NSA_GROUP):
        sl = slice(r * NSA_TQ, (r + 1) * NSA_TQ)
        outs.append(gates[r:r + 1, :] * oc_t[:, sl]
                    + gates[NSA_GROUP + r:NSA_GROUP + r + 1, :] * os_t[:, sl]
                    + gates[2 * NSA_GROUP + r:2 * NSA_GROUP + r + 1, :] * ow_t[:, sl])
    o_tok = jnp.concatenate(outs, axis=0).T
    ng = ng_ref[0]
    o_ref[0] = (o_tok * (ng * _sigmoid(ng))).astype(BF16)


def _nsa(nqt, glt, ng3, ks, kw, vt, kc, vct, mt, qnw, bg, B, T):
    nt = T // LANES
    ncb = kc.shape[3]
    gw = NSA_GROUP * HEAD_DIM
    return pl.pallas_call(
        _nsa_kernel,
        grid=(B, NSA_KV_HEADS, T // NSA_TQ),
        in_specs=[
            pl.BlockSpec((1, 1, gw, LANES), lambda b, g, i: (b, i, g, 0)),
            pl.BlockSpec((1, 1, 16, LANES), lambda b, g, i: (b, i, g, 0)),
            pl.BlockSpec((1, NSA_TQ, gw), lambda b, g, i: (b, i, g)),
            pl.BlockSpec((1, 1, T, PAIR), lambda b, g, i: (b, g, 0, 0)),
            pl.BlockSpec((1, 1, T, PAIR), lambda b, g, i: (b, g, 0, 0)),
            pl.BlockSpec((1, nt, V_ROWS, LANES), lambda b, g, i: (b, 0, g, 0)),
            pl.BlockSpec((1, nt, V_ROWS, LANES), lambda b, g, i: (b, 0, NSA_KV_HEADS + g, 0)),
            pl.BlockSpec((1, 1, 1, ncb, HEAD_DIM), lambda b, g, i: (b, 0, g, 0, 0)),
            pl.BlockSpec((1, 1, 1, HEAD_DIM, ncb), lambda b, g, i: (b, 1, g, 0, 0)),
            pl.BlockSpec(mt.shape, lambda b, g, i: (0, 0)),
            pl.BlockSpec((HEAD_DIM, 1), lambda b, g, i: (0, 0)),
            pl.BlockSpec((16, 1), lambda b, g, i: (g, 0)),
        ],
        out_specs=pl.BlockSpec((1, NSA_TQ, gw), lambda b, g, i: (b, i, g)),
        out_shape=jax.ShapeDtypeStruct((B, T, D_NSA), BF16),
        scratch_shapes=[
            pltpu.VMEM((2 * HEAD_DIM, NSA_GROUP * NSA_TQ), BF16),
            pltpu.VMEM((SLC_TK, NSA_GROUP * NSA_TQ), F32),
            pltpu.VMEM((SLC_TK, NSA_GROUP * NSA_TQ), F32),
            pltpu.VMEM((WIN_KEYS, NSA_GROUP * NSA_TQ), F32),
            pltpu.VMEM((1, NSA_GROUP * NSA_TQ), F32),
            pltpu.VMEM((V_ROWS, NSA_GROUP * NSA_TQ), F32),
        ],
        compiler_params=pltpu.CompilerParams(dimension_semantics=("parallel", "parallel", "arbitrary"),
                                             vmem_limit_bytes=VMEM_LIMIT),
        name="nsa",
    )(nqt, glt, ng3, ks, kw, vt, vt, kc, vct, mt, qnw, bg)


def _out_kernel(x_ref, yr_ref, yn_ref, wr_ref, wn_ref, o_ref):
    o_ref[...] = (x_ref[...]
                  + jnp.dot(yr_ref[...], wr_ref[...], preferred_element_type=F32)
                  + jnp.dot(yn_ref[...], wn_ref[...], preferred_element_type=F32))


def _outproj(x2, yr, yn, wr, wn):
    N = x2.shape[0]
    row = lambda i: (i, 0)
    const = lambda i: (0, 0)
    return pl.pallas_call(
        _out_kernel,
        grid=(N // OUT_TM,),
        in_specs=[pl.BlockSpec((OUT_TM, D_MODEL), row), pl.BlockSpec((OUT_TM, D_RET), row),
                  pl.BlockSpec((OUT_TM, D_NSA), row), pl.BlockSpec(wr.shape, const), pl.BlockSpec(wn.shape, const)],
        out_specs=pl.BlockSpec((OUT_TM, D_MODEL), row),
        out_shape=jax.ShapeDtypeStruct((N, D_MODEL), F32),
        compiler_params=pltpu.CompilerParams(dimension_semantics=("parallel",), vmem_limit_bytes=VMEM_LIMIT),
        name="outproj",
    )(x2, yr, yn, wr, wn)


@functools.lru_cache(maxsize=None)
def _tables(T):
    half = HEAD_DIM // 2
    inv = ROPE_THETA ** (-np.arange(half, dtype=np.float64) / half)
    ang = np.arange(T, dtype=np.float64)[:, None] * inv[None, :]
    cos64 = np.concatenate([np.cos(ang), np.cos(ang)], axis=1)
    sin64 = np.concatenate([-np.sin(ang), np.sin(ang)], axis=1)
    cos = np.concatenate([cos64, cos64], axis=1).astype(np.float32)
    sin = np.concatenate([sin64, sin64], axis=1).astype(np.float32)

    C = RET_CHUNK
    log_g = np.log1p(-np.exp2(-5.0 - np.arange(RET_HEADS, dtype=np.float64)))
    pos = np.arange(C, dtype=np.float64)
    diff = pos[:, None] - pos[None, :]
    decay = np.where(diff >= 0, np.exp(log_g[:, None, None] * np.maximum(diff, 0.0)), 0.0)
    zeta = np.exp(log_g[:, None] * (C - 1.0 - pos))
    xi = np.exp(log_g[:, None] * (pos + 1.0))
    g_chunk = np.exp(log_g * C)
    npair = RET_HEADS // 2

    def pair_lanes(a):
        return np.repeat(a.reshape(npair, 2, C).transpose(0, 2, 1), HEAD_DIM, axis=2).astype(np.float32)

    dec = decay.reshape(npair, 2, C, C).astype(np.float32)
    gch = np.repeat(g_chunk.reshape(npair, 1, 2), HEAD_DIM, axis=2).astype(np.float32)

    n_cmp = (T - CMP_BLOCK) // CMP_STRIDE + 1
    ncb = T // CMP_STRIDE
    p = np.arange(n_cmp)[:, None] * CMP_STRIDE + np.arange(CMP_BLOCK)[None, :]
    blk = p // SLC_BLOCK
    M = (blk[:, :, None] == np.arange(T // SLC_BLOCK)[None, None, :]).mean(axis=1)
    mt = np.zeros((T // SLC_BLOCK, ncb), np.float32)
    mt[:, :n_cmp] = M.T
    return cos, sin, dec, pair_lanes(zeta), pair_lanes(xi), gch, mt


def kernel(x, norm_w, w_in, ret_norm_w, q_norm_w, k_norm_cmp, k_norm_slc, k_norm_win, cmp_pos_k, cmp_w1_k, cmp_w2_k,
           cmp_pos_v, cmp_w1_v, cmp_w2_v, b_gate, w_out):
    B, T, D = x.shape
    depth = norm_w.shape[0]
    cos, sin, dec, zeta, xi, gch, mt = _tables(T)
    ncb = T // CMP_STRIDE
    half = CMP_STRIDE * HEAD_DIM
    gate_src = np.zeros((NSA_KV_HEADS, 16), np.int32)
    gate_ok = np.zeros((NSA_KV_HEADS, 16), bool)
    for g in range(NSA_KV_HEADS):
        for br in range(N_BRANCH):
            for r in range(NSA_GROUP):
                gate_src[g, br * NSA_GROUP + r] = br * NSA_HEADS + g * NSA_GROUP + r
                gate_ok[g, br * NSA_GROUP + r] = True
    gate_src = gate_src.reshape(-1)
    gate_ok = gate_ok.reshape(-1)

    x2 = x.reshape(B * T, D)
    for layer in range(depth):
        w = w_in[layer].astype(BF16)
        o_ng = 4 * D_RET + D_NSA
        o_kv = o_ng + D_NSA
        w_ret = w[:, :4 * D_RET]
        wt_q = w[:, 4 * D_RET:o_ng].T
        w_ng = w[:, o_ng:o_kv]
        w_ckv = w[:, o_kv:o_kv + 2 * D_KV]
        w_skwk = jnp.concatenate([w[:, o_kv + 2 * D_KV:o_kv + 3 * D_KV], w[:, o_kv + 4 * D_KV:o_kv + 5 * D_KV]], axis=1)
        wt_v = jnp.concatenate([w[:, o_kv + 3 * D_KV:o_kv + 4 * D_KV], w[:, o_kv + 5 * D_KV:o_kv + 6 * D_KV]], axis=1).T
        w_gl = w[:, o_kv + 6 * D_KV:]
        wt_g = jnp.where(gate_ok[:, None], w_gl.T[gate_src], jnp.zeros((), BF16))
        bg = jnp.where(gate_ok, b_gate[layer][gate_src], 0.0).reshape(-1, 1)

        ret, ng, ckv, skwk, nqt, vt, glt = _proj(x2, norm_w[layer].reshape(1, D), w_ret, w_ng, w_ckv, w_skwk,
                                                 wt_q, wt_v, wt_g, B, T)

        nw_pair = ret_norm_w[layer].reshape(RET_HEADS // 2, 1, PAIR)
        y_ret = _retention(ret.reshape(B, T, 4 * D_RET), jnp.asarray(cos), jnp.asarray(sin), jnp.asarray(dec),
                           jnp.asarray(zeta), jnp.asarray(xi), jnp.asarray(gch), nw_pair, B, T)

        xc = ckv.reshape(B, T, 2 * NSA_KV_HEADS, HEAD_DIM).transpose(0, 2, 1, 3).reshape(B, 2 * NSA_KV_HEADS, ncb, half)
        pos = jnp.stack([cmp_pos_k[layer], cmp_pos_v[layer]]).reshape(2, 2, half)
        w1 = jnp.stack([cmp_w1_k[layer], cmp_w1_v[layer]]).astype(BF16)
        w2 = jnp.stack([cmp_w2_k[layer], cmp_w2_v[layer]]).astype(BF16)
        w2t = jnp.swapaxes(w2, 1, 2)
        kc, vct = _compress(xc, pos, w1, w2, w2t, k_norm_cmp[layer].reshape(1, HEAD_DIM), B, ncb)

        knw = jnp.stack([jnp.tile(k_norm_slc[layer], 2), jnp.tile(k_norm_win[layer], 2)])
        ks, kw = _kprep(skwk.reshape(B, T, 2 * PAIR), knw, B, T)

        y_nsa = _nsa(nqt, glt, ng.reshape(B, T, D_NSA), ks, kw, vt, kc, vct, jnp.asarray(mt).astype(BF16),
                     q_norm_w[layer].reshape(HEAD_DIM, 1), bg, B, T)

        wo = w_out[layer].astype(BF16)
        x2 = _outproj(x2, y_ret.reshape(B * T, D_RET), y_nsa.reshape(B * T, D_NSA), wo[:D_RET], wo[D_RET:])
    return x2.reshape(B, T, D)
```

```python
import functools

import numpy as np
import jax
import jax.numpy as jnp
from jax import lax
from jax.experimental import pallas as pl
from jax.experimental.pallas import tpu as pltpu

F32 = jnp.float32
BF16 = jnp.bfloat16

D_MODEL = 1024
HEAD_DIM = 64
HEAD_SHIFT = 6
RET_HEADS = 8
NSA_HEADS = 8
NSA_KV_HEADS = 2
NSA_GROUP = NSA_HEADS // NSA_KV_HEADS
D_RET = RET_HEADS * HEAD_DIM
D_NSA = NSA_HEADS * HEAD_DIM
D_KV = NSA_KV_HEADS * HEAD_DIM
N_BRANCH = 3
RET_CHUNK = 128
ROPE_THETA = 10000.0
CMP_BLOCK = 32
CMP_STRIDE = 16
CMP_HIDDEN = 256
SLC_BLOCK = 64
SLC_SHIFT = 6
SLC_TOPK = 16
WIN_SIZE = 512
EPS = 1e-6
NEG = -1e30
FORCE_BONUS = 1e4
QK_SCALE = HEAD_DIM ** -0.5
LOG2E = 1.4426950408889634
V_ROWS = HEAD_DIM + 16

LANES = 128
SUBLANES = 8
PAIR = 2 * HEAD_DIM
VMEM_LIMIT = 48 * 1024 * 1024

PROJ_TM = 256
RET_TC = 1024
KPREP_TM = 512
NSA_TQ = 128
SLC_TK = 512
WIN_KEYS = WIN_SIZE + NSA_TQ
OUT_TM = 512

NT_DIMS = (((1,), (1,)), ((), ()))
TN_DIMS = (((0,), (0,)), ((), ()))


def _sigmoid(x):
    return 1.0 / (1.0 + jnp.exp(-x))


def _tile4(a):
    return jnp.concatenate([a, a, a, a], axis=1)


def _proj_kernel(x_ref, nw_ref, w_ret_ref, w_ng_ref, w_ckv_ref, w_skwk_ref, wt_q_ref, wt_v_ref, wt_g_ref,
                 ret_ref, ng_ref, ckv_ref, skwk_ref, nqt_ref, vt_ref, glt_ref):
    x = x_ref[...]
    ms = jnp.mean(x * x, axis=-1, keepdims=True)
    h = (x * lax.rsqrt(ms + EPS) * nw_ref[...]).astype(BF16)
    ret_ref[...] = jnp.dot(h, w_ret_ref[...], preferred_element_type=F32)
    ng_ref[...] = jnp.dot(h, w_ng_ref[...], preferred_element_type=F32)
    ckv_ref[...] = jnp.dot(h, w_ckv_ref[...], preferred_element_type=F32)
    skwk_ref[...] = jnp.dot(h, w_skwk_ref[...], preferred_element_type=F32)
    qt = lax.dot_general(wt_q_ref[...], h, NT_DIMS, preferred_element_type=F32)
    vt = lax.dot_general(wt_v_ref[...], h, NT_DIMS, preferred_element_type=F32)
    gt = lax.dot_general(wt_g_ref[...], h, NT_DIMS, preferred_element_type=F32)
    pad_row = lax.broadcasted_iota(jnp.int32, (V_ROWS - HEAD_DIM, LANES), 0)
    ones_pad = jnp.where(pad_row == 0, 1.0, 0.0).astype(BF16)
    for j in range(PROJ_TM // LANES):
        sl = slice(j * LANES, (j + 1) * LANES)
        nqt_ref[0, j] = qt[:, sl]
        for blk in range(2 * NSA_KV_HEADS):
            vt_ref[0, j, blk * V_ROWS:blk * V_ROWS + HEAD_DIM, :] = (
                vt[blk * HEAD_DIM:(blk + 1) * HEAD_DIM, sl].astype(BF16))
            vt_ref[0, j, blk * V_ROWS + HEAD_DIM:(blk + 1) * V_ROWS, :] = ones_pad
        glt_ref[0, j] = gt[:, sl]


def _proj(x2, nw, w_ret, w_ng, w_ckv, w_skwk, wt_q, wt_v, wt_g, B, T):
    N = B * T
    tpb = T // PROJ_TM
    sub = PROJ_TM // LANES
    nt = T // LANES
    const = lambda i: (0, 0)
    row = lambda i: (i, 0)
    trn = lambda i: (i // tpb, i % tpb, 0, 0)
    return pl.pallas_call(
        _proj_kernel,
        grid=(N // PROJ_TM,),
        in_specs=[
            pl.BlockSpec((PROJ_TM, D_MODEL), row),
            pl.BlockSpec((1, D_MODEL), const),
            pl.BlockSpec(w_ret.shape, const),
            pl.BlockSpec(w_ng.shape, const),
            pl.BlockSpec(w_ckv.shape, const),
            pl.BlockSpec(w_skwk.shape, const),
            pl.BlockSpec(wt_q.shape, const),
            pl.BlockSpec(wt_v.shape, const),
            pl.BlockSpec(wt_g.shape, const),
        ],
        out_specs=[
            pl.BlockSpec((PROJ_TM, 4 * D_RET), row),
            pl.BlockSpec((PROJ_TM, D_NSA), row),
            pl.BlockSpec((PROJ_TM, 2 * D_KV), row),
            pl.BlockSpec((PROJ_TM, 2 * D_KV), row),
            pl.BlockSpec((1, sub, D_NSA, LANES), trn),
            pl.BlockSpec((1, sub, 2 * NSA_KV_HEADS * V_ROWS, LANES), trn),
            pl.BlockSpec((1, sub, 32, LANES), trn),
        ],
        out_shape=[
            jax.ShapeDtypeStruct((N, 4 * D_RET), F32),
            jax.ShapeDtypeStruct((N, D_NSA), F32),
            jax.ShapeDtypeStruct((N, 2 * D_KV), F32),
            jax.ShapeDtypeStruct((N, 2 * D_KV), F32),
            jax.ShapeDtypeStruct((B, nt, D_NSA, LANES), F32),
            jax.ShapeDtypeStruct((B, nt, 2 * NSA_KV_HEADS * V_ROWS, LANES), BF16),
            jax.ShapeDtypeStruct((B, nt, 32, LANES), F32),
        ],
        compiler_params=pltpu.CompilerParams(dimension_semantics=("parallel",), vmem_limit_bytes=VMEM_LIMIT),
        name="proj",
    )(x2, nw, w_ret, w_ng, w_ckv, w_skwk, wt_q, wt_v, wt_g)


def _ret_kernel(q_ref, k_ref, v_ref, g_ref, cos_ref, sin_ref, dec_ref, zeta_ref, xi_ref, gch_ref, nw_ref,
                o_ref, state_ref):
    @pl.when(pl.program_id(2) == 0)
    def _():
        state_ref[...] = jnp.zeros_like(state_ref)

    lane = lax.broadcasted_iota(jnp.int32, (1, PAIR), 1)
    head0 = lane < HEAD_DIM
    m0 = jnp.where(head0, 1.0, 0.0)
    m1 = 1.0 - m0
    first_half = (lane & (HEAD_DIM - 1)) < (HEAD_DIM // 2)
    rr = lax.broadcasted_iota(jnp.int32, (PAIR, PAIR), 0) >> HEAD_SHIFT
    cc = lax.broadcasted_iota(jnp.int32, (PAIR, PAIR), 1) >> HEAD_SHIFT
    blockdiag = jnp.where(rr == cc, 1.0, 0.0)
    zeta = zeta_ref[0]
    xi = xi_ref[0]
    gch = gch_ref[0]
    nw = nw_ref[0]

    for c in range(RET_TC // RET_CHUNK):
        sl = pl.ds(c * RET_CHUNK, RET_CHUNK)
        q = q_ref[0, sl, :]
        k = k_ref[0, sl, :]
        v = v_ref[0, sl, :]
        g = g_ref[0, sl, :]
        cos = cos_ref[sl, :]
        sin = sin_ref[sl, :]

        def rope(t):
            rot = jnp.where(first_half, pltpu.roll(t, PAIR - HEAD_DIM // 2, 1), pltpu.roll(t, HEAD_DIM // 2, 1))
            return t * cos + rot * sin

        qr = rope(q)
        kb = (rope(k) * QK_SCALE).astype(BF16)
        state = state_ref[...]
        o = jnp.dot(qr.astype(BF16), state.astype(BF16), preferred_element_type=F32) * xi
        for h, mh in ((0, m0), (1, m1)):
            s = lax.dot_general((qr * mh).astype(BF16), kb, NT_DIMS, preferred_element_type=F32) * dec_ref[0, h]
            o = o + jnp.dot(s.astype(BF16), (v * mh).astype(BF16), preferred_element_type=F32)
        kv = lax.dot_general(kb, (v * zeta).astype(BF16), TN_DIMS, preferred_element_type=F32)
        state_ref[...] = state * gch + kv * blockdiag

        o2 = o * o
        ms0 = jnp.sum(o2 * m0, axis=-1, keepdims=True)
        ms1 = jnp.sum(o2 * m1, axis=-1, keepdims=True)
        ms = jnp.where(head0, ms0, ms1) * (1.0 / HEAD_DIM)
        y = o * lax.rsqrt(ms + EPS) * nw
        o_ref[0, sl, :] = (y * (g * _sigmoid(g))).astype(BF16)


def _retention(ret3, cos, sin, dec, zeta, xi, gch, nw, B, T):
    npair = RET_HEADS // 2
    blk = lambda col0: pl.BlockSpec((1, RET_TC, PAIR), lambda b, p, i: (b, i, col0 + p))
    tab = pl.BlockSpec((RET_TC, PAIR), lambda b, p, i: (i, 0))
    per_pair3 = lambda shape: pl.BlockSpec((1,) + shape, lambda b, p, i: (p,) + (0,) * len(shape))
    return pl.pallas_call(
        _ret_kernel,
        grid=(B, npair, T // RET_TC),
        in_specs=[blk(0), blk(npair), blk(2 * npair), blk(3 * npair), tab, tab,
                  per_pair3((2, RET_CHUNK, RET_CHUNK)), per_pair3((RET_CHUNK, PAIR)), per_pair3((RET_CHUNK, PAIR)),
                  per_pair3((1, PAIR)), per_pair3((1, PAIR))],
        out_specs=pl.BlockSpec((1, RET_TC, PAIR), lambda b, p, i: (b, i, p)),
        out_shape=jax.ShapeDtypeStruct((B, T, D_RET), BF16),
        scratch_shapes=[pltpu.VMEM((PAIR, PAIR), F32)],
        compiler_params=pltpu.CompilerParams(dimension_semantics=("parallel", "parallel", "arbitrary"),
                                             vmem_limit_bytes=VMEM_LIMIT),
        name="retention",
    )(ret3, ret3, ret3, ret3, cos, sin, dec, zeta, xi, gch, nw)


def _cmp_kernel(x_ref, pos_ref, w1_ref, w2_ref, w2t_ref, knw_ref, o_ref, ot_ref):
    is_key = pl.program_id(1) == 0
    half = CMP_STRIDE * HEAD_DIM
    for g in range(NSA_KV_HEADS):
        x = x_ref[0, g]
        a = jnp.dot((x + pos_ref[0, 0:1, :]).astype(BF16), w1_ref[0, :half, :], preferred_element_type=F32)
        b = jnp.dot((x + pos_ref[0, 1:2, :]).astype(BF16), w1_ref[0, half:, :], preferred_element_type=F32)
        hid = a + pltpu.roll(b, b.shape[0] - 1, 0)
        hid = (hid * _sigmoid(hid)).astype(BF16)
        out = jnp.dot(hid, w2_ref[0], preferred_element_type=F32)
        ms = jnp.mean(out * out, axis=-1, keepdims=True)
        normed = out * lax.rsqrt(ms + EPS) * knw_ref[...]
        o_ref[0, 0, g] = jnp.where(is_key, normed, out)
        ot_ref[0, 0, g] = lax.dot_general(w2t_ref[0], hid, NT_DIMS, preferred_element_type=F32)


def _compress(xc, pos, w1, w2, w2t, knw, B, ncb):
    return pl.pallas_call(
        _cmp_kernel,
        grid=(B, 2),
        in_specs=[
            pl.BlockSpec((1, NSA_KV_HEADS, ncb, CMP_STRIDE * HEAD_DIM), lambda b, s: (b, s, 0, 0)),
            pl.BlockSpec((1, 2, CMP_STRIDE * HEAD_DIM), lambda b, s: (s, 0, 0)),
            pl.BlockSpec((1, CMP_BLOCK * HEAD_DIM, CMP_HIDDEN), lambda b, s: (s, 0, 0)),
            pl.BlockSpec((1, CMP_HIDDEN, HEAD_DIM), lambda b, s: (s, 0, 0)),
            pl.BlockSpec((1, HEAD_DIM, CMP_HIDDEN), lambda b, s: (s, 0, 0)),
            pl.BlockSpec((1, HEAD_DIM), lambda b, s: (0, 0)),
        ],
        out_specs=[
            pl.BlockSpec((1, 1, NSA_KV_HEADS, ncb, HEAD_DIM), lambda b, s: (b, s, 0, 0, 0)),
            pl.BlockSpec((1, 1, NSA_KV_HEADS, HEAD_DIM, ncb), lambda b, s: (b, s, 0, 0, 0)),
        ],
        out_shape=[
            jax.ShapeDtypeStruct((B, 2, NSA_KV_HEADS, ncb, HEAD_DIM), F32),
            jax.ShapeDtypeStruct((B, 2, NSA_KV_HEADS, HEAD_DIM, ncb), F32),
        ],
        compiler_params=pltpu.CompilerParams(dimension_semantics=("parallel", "parallel"),
                                             vmem_limit_bytes=VMEM_LIMIT),
        name="compress",
    )(xc, pos, w1, w2, w2t, knw)


def _kprep_kernel(x_ref, w_ref, ks_ref, kw_ref):
    lane = lax.broadcasted_iota(jnp.int32, (1, PAIR), 1)
    head0 = lane < HEAD_DIM
    m0 = jnp.where(head0, 1.0, 0.0)
    m1 = 1.0 - m0
    t0 = pl.program_id(1) * KPREP_TM
    tok = t0 + lax.broadcasted_iota(jnp.int32, (KPREP_TM, PAIR), 0)
    col = lax.broadcasted_iota(jnp.int32, (KPREP_TM, PAIR), 1)
    indicator = jnp.where((tok >> SLC_SHIFT) == col - HEAD_DIM, 1.0, 0.0)

    def normed(x, w):
        x2 = x * x
        ms0 = jnp.sum(x2 * m0, axis=-1, keepdims=True)
        ms1 = jnp.sum(x2 * m1, axis=-1, keepdims=True)
        ms = jnp.where(head0, ms0, ms1) * (1.0 / HEAD_DIM)
        return x * lax.rsqrt(ms + EPS) * w

    ns = normed(x_ref[0, :, :PAIR], w_ref[0:1, :])
    nw = normed(x_ref[0, :, PAIR:], w_ref[1:2, :])
    for g in range(NSA_KV_HEADS):
        s_g = ns if g == 0 else pltpu.roll(ns, HEAD_DIM, 1)
        w_g = nw if g == 0 else pltpu.roll(nw, HEAD_DIM, 1)
        ks_ref[0, g] = jnp.where(head0, s_g, indicator).astype(BF16)
        kw_ref[0, g] = jnp.where(head0, w_g, 0.0).astype(BF16)


def _kprep(skwk3, w, B, T):
    return pl.pallas_call(
        _kprep_kernel,
        grid=(B, T // KPREP_TM),
        in_specs=[pl.BlockSpec((1, KPREP_TM, 2 * PAIR), lambda b, i: (b, i, 0)),
                  pl.BlockSpec((2, PAIR), lambda b, i: (0, 0))],
        out_specs=[pl.BlockSpec((1, NSA_KV_HEADS, KPREP_TM, PAIR), lambda b, i: (b, 0, i, 0)),
                   pl.BlockSpec((1, NSA_KV_HEADS, KPREP_TM, PAIR), lambda b, i: (b, 0, i, 0))],
        out_shape=[jax.ShapeDtypeStruct((B, NSA_KV_HEADS, T, PAIR), BF16),
                   jax.ShapeDtypeStruct((B, NSA_KV_HEADS, T, PAIR), BF16)],
        compiler_params=pltpu.CompilerParams(dimension_semantics=("parallel", "parallel"),
                                             vmem_limit_bytes=VMEM_LIMIT),
        name="kprep",
    )(skwk3, w)


def _nsa_kernel(qt_ref, glt_ref, ng_ref, ks_ref, kw_ref, vst_ref, vwt_ref, kc_ref, vct_ref, mt_ref, qnw_ref, bg_ref,
                o_ref, qp_ref, sa_ref, sb_ref, sw_ref, m_ref, acc_ref):
    qi = pl.program_id(1)
    t0 = qi * NSA_TQ
    groups = range(NSA_KV_HEADS)
    gq = NSA_GROUP * HEAD_DIM
    gg = 16

    qs = []
    for g in groups:
        cols = []
        for r in range(NSA_GROUP):
            q = qt_ref[0, 0, g * gq + r * HEAD_DIM:g * gq + (r + 1) * HEAD_DIM, :]
            ms = jnp.mean(q * q, axis=0, keepdims=True)
            cols.append(q * lax.rsqrt(ms + EPS) * qnw_ref[...] * (QK_SCALE * LOG2E))
        qs.append(jnp.concatenate(cols, axis=1).astype(BF16))
        qp_ref[g, 0:HEAD_DIM, :] = qs[g]

    kt0 = jnp.maximum(qi - WIN_SIZE // NSA_TQ, 0)
    ks0 = pl.multiple_of(kt0 * NSA_TQ, NSA_TQ)
    for g in groups:
        sw_ref[g] = jnp.dot(kw_ref[0, g, pl.ds(ks0, WIN_KEYS), 0:HEAD_DIM], qs[g], preferred_element_type=F32)

    ncb = kc_ref.shape[3]
    n_idx = lax.broadcasted_iota(jnp.int32, (ncb, NSA_TQ), 0)
    tok_c = t0 + lax.broadcasted_iota(jnp.int32, (ncb, NSA_TQ), 1)
    cbias = _tile4(jnp.where((n_idx * CMP_STRIDE + (CMP_BLOCK - 1)) <= tok_c, 0.0, NEG))
    tok_row = t0 + lax.broadcasted_iota(jnp.int32, (1, NSA_TQ), 1)
    has_block = _tile4(jnp.where(tok_row >= CMP_BLOCK - 1, 1.0, 0.0))
    p, oc_t = [], []
    for g in groups:
        sc = jnp.dot(kc_ref[0, 0, g].astype(BF16), qs[g], preferred_element_type=F32) + cbias
        mc = jnp.max(sc, axis=0, keepdims=True)
        ec = jnp.exp2(sc - mc)
        lc = jnp.sum(ec, axis=0, keepdims=True)
        p.append(ec * (has_block / lc))
        oc_t.append(jnp.dot(vct_ref[0, 0, g].astype(BF16), p[g].astype(BF16), preferred_element_type=F32))

    n_slc = mt_ref.shape[0]
    jb = lax.broadcasted_iota(jnp.int32, (n_slc, NSA_TQ), 0)
    tok_s = t0 + lax.broadcasted_iota(jnp.int32, (n_slc, NSA_TQ), 1)
    valid_s = jb * SLC_BLOCK <= tok_s
    force = (jb == (tok_s >> SLC_SHIFT)) | (jb == 0)
    sub = lax.broadcasted_iota(jnp.int32, (SUBLANES, NSA_TQ), 0)
    for g in groups:
        ps = p[g][:, 0:NSA_TQ]
        for r in range(1, NSA_GROUP):
            ps = ps + p[g][:, r * NSA_TQ:(r + 1) * NSA_TQ]
        ps_hi = ps.astype(BF16)
        ps_lo = (ps - ps_hi.astype(F32)).astype(BF16)
        imp = (jnp.dot(mt_ref[...], ps_hi, preferred_element_type=F32)
               + jnp.dot(mt_ref[...], ps_lo, preferred_element_type=F32))
        score = jnp.where(valid_s, jnp.where(force, imp + FORCE_BONUS, imp), NEG)
        blocks = [score[v * SUBLANES:(v + 1) * SUBLANES, :] for v in range(n_slc // SUBLANES)]
        ranks = [jnp.zeros((SUBLANES, NSA_TQ), F32) for _ in blocks]
        for i in range(n_slc):
            row = score[i:i + 1, :]
            for v, blk in enumerate(blocks):
                if v * SUBLANES > i:
                    beats = row >= blk
                elif (v + 1) * SUBLANES <= i:
                    beats = row > blk
                else:
                    beats = (row > blk) | ((row >= blk) & (sub > i - v * SUBLANES))
                ranks[v] = ranks[v] + jnp.where(beats, 1.0, 0.0)
        rank = jnp.concatenate(ranks, axis=0)
        sel = (rank < float(SLC_TOPK)) & valid_s
        qp_ref[g, HEAD_DIM:2 * HEAD_DIM, :] = _tile4(jnp.where(sel, 0.0, NEG).astype(BF16))

    vt_per_tile = SLC_TK // LANES

    def slc_scores(j, dst_ref):
        kst = pl.multiple_of(j * SLC_TK, SLC_TK)
        for g in groups:
            dst_ref[g] = jnp.dot(ks_ref[0, g, pl.ds(kst, SLC_TK), :], qp_ref[g], preferred_element_type=F32)

    def slc_update(j, src_ref, causal):
        if causal:
            r_s = lax.broadcasted_iota(jnp.int32, (SLC_TK, NSA_TQ), 0)
            c_s = lax.broadcasted_iota(jnp.int32, (SLC_TK, NSA_TQ), 1)
            causal_bias = _tile4(jnp.where(j * SLC_TK + r_s <= t0 + c_s, 0.0, NEG))
        for g in groups:
            s = src_ref[g]
            if causal:
                s = s + causal_bias
            m_old = m_ref[g]
            m_new = jnp.maximum(m_old, jnp.max(s, axis=0, keepdims=True))
            alpha = jnp.exp2(m_old - m_new)
            eb = jnp.exp2(s - m_new).astype(BF16)
            vt = jnp.concatenate([vst_ref[0, j * vt_per_tile + jj, g * V_ROWS:(g + 1) * V_ROWS, :]
                                  for jj in range(vt_per_tile)], axis=1)
            acc_ref[g] = alpha * acc_ref[g] + jnp.dot(vt, eb, preferred_element_type=F32)
            m_ref[g] = m_new

    slc_scores(0, sa_ref)

    r_w = lax.broadcasted_iota(jnp.int32, (WIN_KEYS, NSA_TQ), 0)
    c_w = lax.broadcasted_iota(jnp.int32, (WIN_KEYS, NSA_TQ), 1)
    delta = (t0 - ks0) + c_w - r_w
    wbias = _tile4(jnp.where((delta >= 0) & (delta < WIN_SIZE), 0.0, NEG))
    ow_t = []
    for g in groups:
        sw = sw_ref[g] + wbias
        mw = jnp.max(sw, axis=0, keepdims=True)
        ewb = jnp.exp2(sw - mw).astype(BF16)
        vwt = jnp.concatenate([vwt_ref[0, kt0 + j, g * V_ROWS:(g + 1) * V_ROWS, :]
                               for j in range(WIN_KEYS // LANES)], axis=1)
        ow_aug = jnp.dot(vwt, ewb, preferred_element_type=F32)
        ow_t.append(ow_aug[0:HEAD_DIM, :] * (1.0 / ow_aug[HEAD_DIM:HEAD_DIM + 1, :]))

    m_ref[...] = jnp.full(m_ref.shape, NEG, F32)
    acc_ref[...] = jnp.zeros(acc_ref.shape, F32)
    n_full = t0 // SLC_TK

    def pair(jj, carry):
        j = 2 * jj
        slc_scores(j + 1, sb_ref)
        slc_update(j, sa_ref, False)
        slc_scores(j + 2, sa_ref)
        slc_update(j + 1, sb_ref, False)
        return carry

    lax.fori_loop(0, n_full // 2, pair, 0)

    @pl.when(n_full % 2 == 1)
    def _():
        slc_scores(n_full, sb_ref)
        slc_update(n_full - 1, sa_ref, False)
        slc_update(n_full, sb_ref, True)

    @pl.when(n_full % 2 == 0)
    def _():
        slc_update(n_full, sa_ref, True)

    gates = _sigmoid(glt_ref[0, 0] + bg_ref[...])
    outs = []
    for g in groups:
        os_t = acc_ref[g, 0:HEAD_DIM, :] * (1.0 / acc_ref[g, HEAD_DIM:HEAD_DIM + 1, :])
        for r in range(NSA_GROUP):
            sl = slice(r * NSA_TQ, (r + 1) * NSA_TQ)
            row = g * gg + r
            outs.append(gates[row:row + 1, :] * oc_t[g][:, sl]
                        + gates[row + NSA_GROUP:row + NSA_GROUP + 1, :] * os_t[:, sl]
                        + gates[row + 2 * NSA_GROUP:row + 2 * NSA_GROUP + 1, :] * ow_t[g][:, sl])
    o_tok = jnp.concatenate(outs, axis=0).T
    ng = ng_ref[0]
    o_ref[0] = (o_tok * (ng * _sigmoid(ng))).astype(BF16)


def _nsa(nqt, glt, ng3, ks, kw, vt, kc, vct, mt, qnw, bg, B, T):
    nt = T // LANES
    ncb = kc.shape[3]
    G = NSA_KV_HEADS
    ncols = NSA_GROUP * NSA_TQ
    return pl.pallas_call(
        _nsa_kernel,
        grid=(B, T // NSA_TQ),
        in_specs=[
            pl.BlockSpec((1, 1, D_NSA, LANES), lambda b, i: (b, i, 0, 0)),
            pl.BlockSpec((1, 1, 16 * G, LANES), lambda b, i: (b, i, 0, 0)),
            pl.BlockSpec((1, NSA_TQ, D_NSA), lambda b, i: (b, i, 0)),
            pl.BlockSpec((1, G, T, PAIR), lambda b, i: (b, 0, 0, 0)),
            pl.BlockSpec((1, G, T, PAIR), lambda b, i: (b, 0, 0, 0)),
            pl.BlockSpec((1, nt, G * V_ROWS, LANES), lambda b, i: (b, 0, 0, 0)),
            pl.BlockSpec((1, nt, G * V_ROWS, LANES), lambda b, i: (b, 0, 1, 0)),
            pl.BlockSpec((1, 1, G, ncb, HEAD_DIM), lambda b, i: (b, 0, 0, 0, 0)),
            pl.BlockSpec((1, 1, G, HEAD_DIM, ncb), lambda b, i: (b, 1, 0, 0, 0)),
            pl.BlockSpec(mt.shape, lambda b, i: (0, 0)),
            pl.BlockSpec((HEAD_DIM, 1), lambda b, i: (0, 0)),
            pl.BlockSpec((16 * G, 1), lambda b, i: (0, 0)),
        ],
        out_specs=pl.BlockSpec((1, NSA_TQ, D_NSA), lambda b, i: (b, i, 0)),
        out_shape=jax.ShapeDtypeStruct((B, T, D_NSA), BF16),
        scratch_shapes=[
            pltpu.VMEM((G, 2 * HEAD_DIM, ncols), BF16),
            pltpu.VMEM((G, SLC_TK, ncols), F32),
            pltpu.VMEM((G, SLC_TK, ncols), F32),
            pltpu.VMEM((G, WIN_KEYS, ncols), F32),
            pltpu.VMEM((G, 1, ncols), F32),
            pltpu.VMEM((G, V_ROWS, ncols), F32),
        ],
        compiler_params=pltpu.CompilerParams(dimension_semantics=("parallel", "arbitrary"),
                                             vmem_limit_bytes=VMEM_LIMIT),
        name="nsa",
    )(nqt, glt, ng3, ks, kw, vt, vt, kc, vct, mt, qnw, bg)


def _out_kernel(x_ref, yr_ref, yn_ref, wr_ref, wn_ref, o_ref):
    o_ref[...] = (x_ref[...]
                  + jnp.dot(yr_ref[...], wr_ref[...], preferred_element_type=F32)
                  + jnp.dot(yn_ref[...], wn_ref[...], preferred_element_type=F32))


def _outproj(x2, yr, yn, wr, wn):
    N = x2.shape[0]
    row = lambda i: (i, 0)
    const = lambda i: (0, 0)
    return pl.pallas_call(
        _out_kernel,
        grid=(N // OUT_TM,),
        in_specs=[pl.BlockSpec((OUT_TM, D_MODEL), row), pl.BlockSpec((OUT_TM, D_RET), row),
                  pl.BlockSpec((OUT_TM, D_NSA), row), pl.BlockSpec(wr.shape, const), pl.BlockSpec(wn.shape, const)],
        out_specs=pl.BlockSpec((OUT_TM, D_MODEL), row),
        out_shape=jax.ShapeDtypeStruct((N, D_MODEL), F32),
        compiler_params=pltpu.CompilerParams(dimension_semantics=("parallel",), vmem_limit_bytes=VMEM_LIMIT),
        name="outproj",
    )(x2, yr, yn, wr, wn)


@functools.lru_cache(maxsize=None)
def _tables(T):
    half = HEAD_DIM // 2
    inv = ROPE_THETA ** (-np.arange(half, dtype=np.float64) / half)
    ang = np.arange(T, dtype=np.float64)[:, None] * inv[None, :]
    cos64 = np.concatenate([np.cos(ang), np.cos(ang)], axis=1)
    sin64 = np.concatenate([-np.sin(ang), np.sin(ang)], axis=1)
    cos = np.concatenate([cos64, cos64], axis=1).astype(np.float32)
    sin = np.concatenate([sin64, sin64], axis=1).astype(np.float32)

    C = RET_CHUNK
    log_g = np.log1p(-np.exp2(-5.0 - np.arange(RET_HEADS, dtype=np.float64)))
    pos = np.arange(C, dtype=np.float64)
    diff = pos[:, None] - pos[None, :]
    decay = np.where(diff >= 0, np.exp(log_g[:, None, None] * np.maximum(diff, 0.0)), 0.0)
    zeta = np.exp(log_g[:, None] * (C - 1.0 - pos))
    xi = np.exp(log_g[:, None] * (pos + 1.0))
    g_chunk = np.exp(log_g * C)
    npair = RET_HEADS // 2

    def pair_lanes(a):
        return np.repeat(a.reshape(npair, 2, C).transpose(0, 2, 1), HEAD_DIM, axis=2).astype(np.float32)

    dec = decay.reshape(npair, 2, C, C).astype(np.float32)
    gch = np.repeat(g_chunk.reshape(npair, 1, 2), HEAD_DIM, axis=2).astype(np.float32)

    n_cmp = (T - CMP_BLOCK) // CMP_STRIDE + 1
    ncb = T // CMP_STRIDE
    p = np.arange(n_cmp)[:, None] * CMP_STRIDE + np.arange(CMP_BLOCK)[None, :]
    blk = p // SLC_BLOCK
    M = (blk[:, :, None] == np.arange(T // SLC_BLOCK)[None, None, :]).mean(axis=1)
    mt = np.zeros((T // SLC_BLOCK, ncb), np.float32)
    mt[:, :n_cmp] = M.T
    return cos, sin, dec, pair_lanes(zeta), pair_lanes(xi), gch, mt


def kernel(x, norm_w, w_in, ret_norm_w, q_norm_w, k_norm_cmp, k_norm_slc, k_norm_win, cmp_pos_k, cmp_w1_k, cmp_w2_k,
           cmp_pos_v, cmp_w1_v, cmp_w2_v, b_gate, w_out):
    B, T, D = x.shape
    depth = norm_w.shape[0]
    cos, sin, dec, zeta, xi, gch, mt = _tables(T)
    ncb = T // CMP_STRIDE
    half = CMP_STRIDE * HEAD_DIM
    gate_src = np.zeros((NSA_KV_HEADS, 16), np.int32)
    gate_ok = np.zeros((NSA_KV_HEADS, 16), bool)
    for g in range(NSA_KV_HEADS):
        for br in range(N_BRANCH):
            for r in range(NSA_GROUP):
                gate_src[g, br * NSA_GROUP + r] = br * NSA_HEADS + g * NSA_GROUP + r
                gate_ok[g, br * NSA_GROUP + r] = True
    gate_src = gate_src.reshape(-1)
    gate_ok = gate_ok.reshape(-1)

    x2 = x.reshape(B * T, D)
    for layer in range(depth):
        w = w_in[layer].astype(BF16)
        o_ng = 4 * D_RET + D_NSA
        o_kv = o_ng + D_NSA
        w_ret = w[:, :4 * D_RET]
        wt_q = w[:, 4 * D_RET:o_ng].T
        w_ng = w[:, o_ng:o_kv]
        w_ckv = w[:, o_kv:o_kv + 2 * D_KV]
        w_skwk = jnp.concatenate([w[:, o_kv + 2 * D_KV:o_kv + 3 * D_KV], w[:, o_kv + 4 * D_KV:o_kv + 5 * D_KV]], axis=1)
        wt_v = jnp.concatenate([w[:, o_kv + 3 * D_KV:o_kv + 4 * D_KV], w[:, o_kv + 5 * D_KV:o_kv + 6 * D_KV]], axis=1).T
        w_gl = w[:, o_kv + 6 * D_KV:]
        wt_g = jnp.where(gate_ok[:, None], w_gl.T[gate_src], jnp.zeros((), BF16))
        bg = jnp.where(gate_ok, b_gate[layer][gate_src], 0.0).reshape(-1, 1)

        ret, ng, ckv, skwk, nqt, vt, glt = _proj(x2, norm_w[layer].reshape(1, D), w_ret, w_ng, w_ckv, w_skwk,
                                                 wt_q, wt_v, wt_g, B, T)

        nw_pair = ret_norm_w[layer].reshape(RET_HEADS // 2, 1, PAIR)
        y_ret = _retention(ret.reshape(B, T, 4 * D_RET), jnp.asarray(cos), jnp.asarray(sin), jnp.asarray(dec),
                           jnp.asarray(zeta), jnp.asarray(xi), jnp.asarray(gch), nw_pair, B, T)

        xc = ckv.reshape(B, T, 2 * NSA_KV_HEADS, HEAD_DIM).transpose(0, 2, 1, 3).reshape(B, 2 * NSA_KV_HEADS, ncb, half)
        pos = jnp.stack([cmp_pos_k[layer], cmp_pos_v[layer]]).reshape(2, 2, half)
        w1 = jnp.stack([cmp_w1_k[layer], cmp_w1_v[layer]]).astype(BF16)
        w2 = jnp.stack([cmp_w2_k[layer], cmp_w2_v[layer]]).astype(BF16)
        w2t = jnp.swapaxes(w2, 1, 2)
        kc, vct = _compress(xc, pos, w1, w2, w2t, k_norm_cmp[layer].reshape(1, HEAD_DIM), B, ncb)

        knw = jnp.stack([jnp.tile(k_norm_slc[layer], 2), jnp.tile(k_norm_win[layer], 2)])
        ks, kw = _kprep(skwk.reshape(B, T, 2 * PAIR), knw, B, T)

        y_nsa = _nsa(nqt, glt, ng.reshape(B, T, D_NSA), ks, kw, vt, kc, vct, jnp.asarray(mt).astype(BF16),
                     q_norm_w[layer].reshape(HEAD_DIM, 1), bg, B, T)

        wo = w_out[layer].astype(BF16)
        x2 = _outproj(x2, y_ret.reshape(B * T, D_RET), y_nsa.reshape(B * T, D_NSA), wo[:D_RET], wo[D_RET:])
    return x2.reshape(B, T, D)
```

```python
import functools

import numpy as np
import jax
import jax.numpy as jnp
from jax import lax
from jax.experimental import pallas as pl
from jax.experimental.pallas import tpu as pltpu

F32 = jnp.float32
BF16 = jnp.bfloat16

D_MODEL = 1024
HEAD_DIM = 64
HEAD_SHIFT = 6
RET_HEADS = 8
NSA_HEADS = 8
NSA_KV_HEADS = 2
NSA_GROUP = NSA_HEADS // NSA_KV_HEADS
D_RET = RET_HEADS * HEAD_DIM
D_NSA = NSA_HEADS * HEAD_DIM
D_KV = NSA_KV_HEADS * HEAD_DIM
N_BRANCH = 3
RET_CHUNK = 128
ROPE_THETA = 10000.0
CMP_BLOCK = 32
CMP_STRIDE = 16
CMP_HIDDEN = 256
SLC_BLOCK = 64
SLC_SHIFT = 6
SLC_TOPK = 16
WIN_SIZE = 512
EPS = 1e-6
NEG = -1e30
FORCE_BONUS = 1e4
QK_SCALE = HEAD_DIM ** -0.5
LOG2E = 1.4426950408889634
V_ROWS = HEAD_DIM + 16

LANES = 128
SUBLANES = 8
PAIR = 2 * HEAD_DIM
VMEM_LIMIT = 48 * 1024 * 1024

PROJ_TM = 256
RET_TC = 1024
NSA_TQ = 128
SLC_TK = 512
WIN_KEYS = WIN_SIZE + NSA_TQ
OUT_TM = 512

NT_DIMS = (((1,), (1,)), ((), ()))
TN_DIMS = (((0,), (0,)), ((), ()))


def _sigmoid(x):
    return 1.0 / (1.0 + jnp.exp(-x))


def _tile4(a):
    return jnp.concatenate([a, a, a, a], axis=1)


def _proj_kernel(steps_per_batch, x_ref, nw_ref, w_ret_ref, w_ng_ref, w_ckv_ref, w_skwk_ref, wt_q_ref, wt_v_ref,
                 wt_g_ref, knw_ref, ret_ref, ng_ref, xc_ref, ks_ref, kw_ref, nqt_ref, vt_ref, glt_ref, ckv_scr):
    x = x_ref[...]
    ms = jnp.mean(x * x, axis=-1, keepdims=True)
    h = (x * lax.rsqrt(ms + EPS) * nw_ref[...]).astype(BF16)
    ret_ref[...] = jnp.dot(h, w_ret_ref[...], preferred_element_type=F32)
    ng_ref[...] = jnp.dot(h, w_ng_ref[...], preferred_element_type=F32)

    ckv = jnp.dot(h, w_ckv_ref[...], preferred_element_type=F32)
    for half in range(2 * D_KV // LANES):
        ckv_scr[half] = ckv[:, half * LANES:(half + 1) * LANES]
    for l in range(CMP_STRIDE):
        for half in range(2 * D_KV // LANES):
            rows = ckv_scr[half, pl.ds(l, PROJ_TM // CMP_STRIDE, stride=CMP_STRIDE), :]
            for s in range(LANES // HEAD_DIM):
                xc_ref[0, half * (LANES // HEAD_DIM) + s, :, l * HEAD_DIM:(l + 1) * HEAD_DIM] = (
                    rows[:, s * HEAD_DIM:(s + 1) * HEAD_DIM])

    skwk = jnp.dot(h, w_skwk_ref[...], preferred_element_type=F32)
    lane = lax.broadcasted_iota(jnp.int32, (1, PAIR), 1)
    head0 = lane < HEAD_DIM
    m0 = jnp.where(head0, 1.0, 0.0)
    m1 = 1.0 - m0
    t_start = (pl.program_id(0) % steps_per_batch) * PROJ_TM
    tok = t_start + lax.broadcasted_iota(jnp.int32, (PROJ_TM, PAIR), 0)
    col = lax.broadcasted_iota(jnp.int32, (PROJ_TM, PAIR), 1)
    indicator = jnp.where((tok >> SLC_SHIFT) == col - HEAD_DIM, 1.0, 0.0)

    def pair_normed(t, w):
        t2 = t * t
        ms0 = jnp.sum(t2 * m0, axis=-1, keepdims=True)
        ms1 = jnp.sum(t2 * m1, axis=-1, keepdims=True)
        return t * lax.rsqrt(jnp.where(head0, ms0, ms1) * (1.0 / HEAD_DIM) + EPS) * w

    ns = pair_normed(skwk[:, :PAIR], knw_ref[0:1, :])
    nwin = pair_normed(skwk[:, PAIR:], knw_ref[1:2, :])
    for g in range(NSA_KV_HEADS):
        s_g = ns if g == 0 else pltpu.roll(ns, HEAD_DIM, 1)
        w_g = nwin if g == 0 else pltpu.roll(nwin, HEAD_DIM, 1)
        ks_ref[0, g] = jnp.where(head0, s_g, indicator).astype(BF16)
        kw_ref[0, g] = jnp.where(head0, w_g, 0.0).astype(BF16)

    qt =lax.dot_general(wt_q_ref[...], h, NT_DIMS, preferred_element_type=F32)
    vt = lax.dot_general(wt_v_ref[...], h, NT_DIMS, preferred_element_type=F32)
    gt = lax.dot_general(wt_g_ref[...], h, NT_DIMS, preferred_element_type=F32)
    pad_row = lax.broadcasted_iota(jnp.int32, (V_ROWS - HEAD_DIM, LANES), 0)
    ones_pad = jnp.where(pad_row == 0, 1.0, 0.0).astype(BF16)
    for j in range(PROJ_TM // LANES):
        sl = slice(j * LANES, (j + 1) * LANES)
        nqt_ref[0, j] = qt[:, sl]
        for blk in range(2 * NSA_KV_HEADS):
            vt_ref[0, j, blk * V_ROWS:blk * V_ROWS + HEAD_DIM, :] = (
                vt[blk * HEAD_DIM:(blk + 1) * HEAD_DIM, sl].astype(BF16))
            vt_ref[0, j, blk * V_ROWS + HEAD_DIM:(blk + 1) * V_ROWS, :] = ones_pad
        glt_ref[0, j] = gt[:, sl]


def _proj(x2, nw, w_ret, w_ng, w_ckv, w_skwk, wt_q, wt_v, wt_g, knw, B, T):
    N = B * T
    tpb = T // PROJ_TM
    sub = PROJ_TM // LANES
    nt = T // LANES
    const = lambda i: (0, 0)
    row = lambda i: (i, 0)
    trn = lambda i: (i // tpb, i % tpb, 0, 0)
    tokm = lambda i: (i // tpb, 0, i % tpb, 0)
    return pl.pallas_call(
        functools.partial(_proj_kernel, tpb),
        grid=(N // PROJ_TM,),
        in_specs=[
            pl.BlockSpec((PROJ_TM, D_MODEL), row),
            pl.BlockSpec((1, D_MODEL), const),
            pl.BlockSpec(w_ret.shape, const),
            pl.BlockSpec(w_ng.shape, const),
            pl.BlockSpec(w_ckv.shape, const),
            pl.BlockSpec(w_skwk.shape, const),
            pl.BlockSpec(wt_q.shape, const),
            pl.BlockSpec(wt_v.shape, const),
            pl.BlockSpec(wt_g.shape, const),
            pl.BlockSpec(knw.shape, const),
        ],
        out_specs=[
            pl.BlockSpec((PROJ_TM, 4 * D_RET), row),
            pl.BlockSpec((PROJ_TM, D_NSA), row),
            pl.BlockSpec((1, 2 * NSA_KV_HEADS, PROJ_TM // CMP_STRIDE, CMP_STRIDE * HEAD_DIM), tokm),
            pl.BlockSpec((1, NSA_KV_HEADS, PROJ_TM, PAIR), tokm),
            pl.BlockSpec((1, NSA_KV_HEADS, PROJ_TM, PAIR), tokm),
            pl.BlockSpec((1, sub, D_NSA, LANES), trn),
            pl.BlockSpec((1, sub, 2 * NSA_KV_HEADS * V_ROWS, LANES), trn),
            pl.BlockSpec((1, sub, 32, LANES), trn),
        ],
        out_shape=[
            jax.ShapeDtypeStruct((N, 4 * D_RET), F32),
            jax.ShapeDtypeStruct((N, D_NSA), F32),
            jax.ShapeDtypeStruct((B, 2 * NSA_KV_HEADS, T // CMP_STRIDE, CMP_STRIDE * HEAD_DIM), F32),
            jax.ShapeDtypeStruct((B, NSA_KV_HEADS, T, PAIR), BF16),
            jax.ShapeDtypeStruct((B, NSA_KV_HEADS, T, PAIR), BF16),
            jax.ShapeDtypeStruct((B, nt, D_NSA, LANES), F32),
            jax.ShapeDtypeStruct((B, nt, 2 * NSA_KV_HEADS * V_ROWS, LANES), BF16),
            jax.ShapeDtypeStruct((B, nt, 32, LANES), F32),
        ],
        scratch_shapes=[pltpu.VMEM((2 * D_KV // LANES, PROJ_TM, LANES), F32)],
        compiler_params=pltpu.CompilerParams(dimension_semantics=("parallel",), vmem_limit_bytes=VMEM_LIMIT),
        name="proj",
    )(x2, nw, w_ret, w_ng, w_ckv, w_skwk, wt_q, wt_v, wt_g, knw)


def _ret_kernel(q_ref, k_ref, v_ref, g_ref, cos_ref, sin_ref, dec_ref, zeta_ref, xi_ref, gch_ref, nw_ref,
                o_ref, state_ref):
    @pl.when(pl.program_id(2) == 0)
    def _():
        state_ref[...] = jnp.zeros_like(state_ref)

    lane = lax.broadcasted_iota(jnp.int32, (1, PAIR), 1)
    head0 = lane < HEAD_DIM
    m0 = jnp.where(head0, 1.0, 0.0)
    m1 = 1.0 - m0
    first_half = (lane & (HEAD_DIM - 1)) < (HEAD_DIM // 2)
    rr = lax.broadcasted_iota(jnp.int32, (PAIR, PAIR), 0) >> HEAD_SHIFT
    cc = lax.broadcasted_iota(jnp.int32, (PAIR, PAIR), 1) >> HEAD_SHIFT
    blockdiag = jnp.where(rr == cc, 1.0, 0.0)
    zeta = zeta_ref[0]
    xi = xi_ref[0]
    gch = gch_ref[0]
    nw = nw_ref[0]

    for c in range(RET_TC // RET_CHUNK):
        sl = pl.ds(c * RET_CHUNK, RET_CHUNK)
        q = q_ref[0, sl, :]
        k = k_ref[0, sl, :]
        v = v_ref[0, sl, :]
        g = g_ref[0, sl, :]
        cos = cos_ref[sl, :]
        sin = sin_ref[sl, :]

        def rope(t):
            rot = jnp.where(first_half, pltpu.roll(t, PAIR - HEAD_DIM // 2, 1), pltpu.roll(t, HEAD_DIM // 2, 1))
            return t * cos + rot * sin

        qr = rope(q)
        kb = (rope(k) * QK_SCALE).astype(BF16)
        state = state_ref[...]
        o = jnp.dot(qr.astype(BF16), state.astype(BF16), preferred_element_type=F32) * xi
        for h, mh in ((0, m0), (1, m1)):
            s = lax.dot_general((qr * mh).astype(BF16), kb, NT_DIMS, preferred_element_type=F32) * dec_ref[0, h]
            o = o + jnp.dot(s.astype(BF16), (v * mh).astype(BF16), preferred_element_type=F32)
        kv = lax.dot_general(kb, (v * zeta).astype(BF16), TN_DIMS, preferred_element_type=F32)
        state_ref[...] = state * gch + kv * blockdiag

        o2 = o * o
        ms0 = jnp.sum(o2 * m0, axis=-1, keepdims=True)
        ms1 = jnp.sum(o2 * m1, axis=-1, keepdims=True)
        ms = jnp.where(head0, ms0, ms1) * (1.0 / HEAD_DIM)
        y = o * lax.rsqrt(ms + EPS) * nw
        o_ref[0, sl, :] = (y * (g * _sigmoid(g))).astype(BF16)


def _retention(ret3, cos, sin, dec, zeta, xi, gch, nw, B, T):
    npair = RET_HEADS // 2
    blk = lambda col0: pl.BlockSpec((1, RET_TC, PAIR), lambda b, p, i: (b, i, col0 + p))
    tab = pl.BlockSpec((RET_TC, PAIR), lambda b, p, i: (i, 0))
    per_pair3 = lambda shape: pl.BlockSpec((1,) + shape, lambda b, p, i: (p,) + (0,) * len(shape))
    return pl.pallas_call(
        _ret_kernel,
        grid=(B, npair, T // RET_TC),
        in_specs=[blk(0), blk(npair), blk(2 * npair), blk(3 * npair), tab, tab,
                  per_pair3((2, RET_CHUNK, RET_CHUNK)), per_pair3((RET_CHUNK, PAIR)), per_pair3((RET_CHUNK, PAIR)),
                  per_pair3((1, PAIR)), per_pair3((1, PAIR))],
        out_specs=pl.BlockSpec((1, RET_TC, PAIR), lambda b, p, i: (b, i, p)),
        out_shape=jax.ShapeDtypeStruct((B, T, D_RET), BF16),
        scratch_shapes=[pltpu.VMEM((PAIR, PAIR), F32)],
        compiler_params=pltpu.CompilerParams(dimension_semantics=("parallel", "parallel", "arbitrary"),
                                             vmem_limit_bytes=VMEM_LIMIT),
        name="retention",
    )(ret3, ret3, ret3, ret3, cos, sin, dec, zeta, xi, gch, nw)


def _cmp_kernel(x_ref, pos_ref, w1_ref, w2_ref, w2t_ref, knw_ref, o_ref, ot_ref):
    is_key = pl.program_id(1) == 0
    half = CMP_STRIDE * HEAD_DIM
    for g in range(NSA_KV_HEADS):
        x = x_ref[0, g]
        a = jnp.dot((x + pos_ref[0, 0:1, :]).astype(BF16), w1_ref[0, :half, :], preferred_element_type=F32)
        b = jnp.dot((x + pos_ref[0, 1:2, :]).astype(BF16), w1_ref[0, half:, :], preferred_element_type=F32)
        hid = a + pltpu.roll(b, b.shape[0] - 1, 0)
        hid = (hid * _sigmoid(hid)).astype(BF16)
        out = jnp.dot(hid, w2_ref[0], preferred_element_type=F32)
        ms = jnp.mean(out * out, axis=-1, keepdims=True)
        normed = out * lax.rsqrt(ms + EPS) * knw_ref[...]
        o_ref[0, 0, g] = jnp.where(is_key, normed, out)
        ot_ref[0, 0, g] = lax.dot_general(w2t_ref[0], hid, NT_DIMS, preferred_element_type=F32)


def _compress(xc, pos, w1, w2, w2t, knw, B, ncb):
    return pl.pallas_call(
        _cmp_kernel,
        grid=(B, 2),
        in_specs=[
            pl.BlockSpec((1, NSA_KV_HEADS, ncb, CMP_STRIDE * HEAD_DIM), lambda b, s: (b, s, 0, 0)),
            pl.BlockSpec((1, 2, CMP_STRIDE * HEAD_DIM), lambda b, s: (s, 0, 0)),
            pl.BlockSpec((1, CMP_BLOCK * HEAD_DIM, CMP_HIDDEN), lambda b, s: (s, 0, 0)),
            pl.BlockSpec((1, CMP_HIDDEN, HEAD_DIM), lambda b, s: (s, 0, 0)),
            pl.BlockSpec((1, HEAD_DIM, CMP_HIDDEN), lambda b, s: (s, 0, 0)),
            pl.BlockSpec((1, HEAD_DIM), lambda b, s: (0, 0)),
        ],
        out_specs=[
            pl.BlockSpec((1, 1, NSA_KV_HEADS, ncb, HEAD_DIM), lambda b, s: (b, s, 0, 0, 0)),
            pl.BlockSpec((1, 1, NSA_KV_HEADS, HEAD_DIM, ncb), lambda b, s: (b, s, 0, 0, 0)),
        ],
        out_shape=[
            jax.ShapeDtypeStruct((B, 2, NSA_KV_HEADS, ncb, HEAD_DIM), F32),
            jax.ShapeDtypeStruct((B, 2, NSA_KV_HEADS, HEAD_DIM, ncb), F32),
        ],
        compiler_params=pltpu.CompilerParams(dimension_semantics=("parallel", "parallel"),
                                             vmem_limit_bytes=VMEM_LIMIT),
        name="compress",
    )(xc, pos, w1, w2, w2t, knw)


def _nsa_kernel(qt_ref, glt_ref, ng_ref, ks_ref, kw_ref, vst_ref, vwt_ref, kc_ref, vct_ref, mt_ref, qnw_ref, bg_ref,
                o_ref, qp_ref, sa_ref, sb_ref, sw_ref, m_ref, acc_ref):
    qi = pl.program_id(1)
    t0 = qi * NSA_TQ
    groups = range(NSA_KV_HEADS)
    gq = NSA_GROUP * HEAD_DIM
    gg = 16

    qs = []
    for g in groups:
        cols = []
        for r in range(NSA_GROUP):
            q = qt_ref[0, 0, g * gq + r * HEAD_DIM:g * gq + (r + 1) * HEAD_DIM, :]
            ms = jnp.mean(q * q, axis=0, keepdims=True)
            cols.append(q * lax.rsqrt(ms + EPS) * qnw_ref[...] * (QK_SCALE * LOG2E))
        qs.append(jnp.concatenate(cols, axis=1).astype(BF16))
        qp_ref[g, 0:HEAD_DIM, :] = qs[g]

    kt0 = jnp.maximum(qi - WIN_SIZE // NSA_TQ, 0)
    ks0 = pl.multiple_of(kt0 * NSA_TQ, NSA_TQ)
    for g in groups:
        sw_ref[g] = jnp.dot(kw_ref[0, g, pl.ds(ks0, WIN_KEYS), 0:HEAD_DIM], qs[g], preferred_element_type=F32)

    ncb = kc_ref.shape[3]
    n_idx = lax.broadcasted_iota(jnp.int32, (ncb, NSA_TQ), 0)
    tok_c = t0 + lax.broadcasted_iota(jnp.int32, (ncb, NSA_TQ), 1)
    cbias = _tile4(jnp.where((n_idx * CMP_STRIDE + (CMP_BLOCK - 1)) <= tok_c, 0.0, NEG))
    tok_row = t0 + lax.broadcasted_iota(jnp.int32, (1, NSA_TQ), 1)
    has_block = _tile4(jnp.where(tok_row >= CMP_BLOCK - 1, 1.0, 0.0))
    p, oc_t = [], []
    for g in groups:
        sc = jnp.dot(kc_ref[0, 0, g].astype(BF16), qs[g], preferred_element_type=F32) + cbias
        mc = jnp.max(sc, axis=0, keepdims=True)
        ec = jnp.exp2(sc - mc)
        lc = jnp.sum(ec, axis=0, keepdims=True)
        p.append(ec * (has_block / lc))
        oc_t.append(jnp.dot(vct_ref[0, 0, g].astype(BF16), p[g].astype(BF16), preferred_element_type=F32))

    n_slc = mt_ref.shape[0]
    jb = lax.broadcasted_iota(jnp.int32, (n_slc, NSA_TQ), 0)
    tok_s = t0 + lax.broadcasted_iota(jnp.int32, (n_slc, NSA_TQ), 1)
    valid_s = jb * SLC_BLOCK <= tok_s
    force = (jb == (tok_s >> SLC_SHIFT)) | (jb == 0)
    sub = lax.broadcasted_iota(jnp.int32, (SUBLANES, NSA_TQ), 0)
    for g in groups:
        ps = p[g][:, 0:NSA_TQ]
        for r in range(1, NSA_GROUP):
            ps = ps + p[g][:, r * NSA_TQ:(r + 1) * NSA_TQ]
        ps_hi = ps.astype(BF16)
        ps_lo = (ps - ps_hi.astype(F32)).astype(BF16)
        imp = (jnp.dot(mt_ref[...], ps_hi, preferred_element_type=F32)
               + jnp.dot(mt_ref[...], ps_lo, preferred_element_type=F32))
        score = jnp.where(valid_s, jnp.where(force, imp + FORCE_BONUS, imp), NEG)
        blocks = [score[v * SUBLANES:(v + 1) * SUBLANES, :] for v in range(n_slc // SUBLANES)]
        ranks = [jnp.zeros((SUBLANES, NSA_TQ), F32) for _ in blocks]
        for i in range(n_slc):
            row = score[i:i + 1, :]
            for v, blk in enumerate(blocks):
                if v * SUBLANES > i:
                    beats = row >= blk
                elif (v + 1) * SUBLANES <= i:
                    beats = row > blk
                else:
                    beats = (row > blk) | ((row >= blk) & (sub > i - v * SUBLANES))
                ranks[v] = ranks[v] + jnp.where(beats, 1.0, 0.0)
        rank = jnp.concatenate(ranks, axis=0)
        sel = (rank < float(SLC_TOPK)) & valid_s
        qp_ref[g, HEAD_DIM:2 * HEAD_DIM, :] = _tile4(jnp.where(sel, 0.0, NEG).astype(BF16))

    vt_per_tile = SLC_TK // LANES

    def slc_scores(j, dst_ref):
        kst = pl.multiple_of(j * SLC_TK, SLC_TK)
        for g in groups:
            dst_ref[g] = jnp.dot(ks_ref[0, g, pl.ds(kst, SLC_TK), :], qp_ref[g], preferred_element_type=F32)

    def slc_update(j, src_ref, causal):
        if causal:
            r_s = lax.broadcasted_iota(jnp.int32, (SLC_TK, NSA_TQ), 0)
            c_s = lax.broadcasted_iota(jnp.int32, (SLC_TK, NSA_TQ), 1)
            causal_bias = _tile4(jnp.where(j * SLC_TK + r_s <= t0 + c_s, 0.0, NEG))
        for g in groups:
            s = src_ref[g]
            if causal:
                s = s + causal_bias
            m_old = m_ref[g]
            m_new = jnp.maximum(m_old, jnp.max(s, axis=0, keepdims=True))
            alpha = jnp.exp2(m_old - m_new)
            eb = jnp.exp2(s - m_new).astype(BF16)
            vt = jnp.concatenate([vst_ref[0, j * vt_per_tile + jj, g * V_ROWS:(g + 1) * V_ROWS, :]
                                  for jj in range(vt_per_tile)], axis=1)
            acc_ref[g] = alpha * acc_ref[g] + jnp.dot(vt, eb, preferred_element_type=F32)
            m_ref[g] = m_new

    slc_scores(0, sa_ref)

    r_w = lax.broadcasted_iota(jnp.int32, (WIN_KEYS, NSA_TQ), 0)
    c_w = lax.broadcasted_iota(jnp.int32, (WIN_KEYS, NSA_TQ), 1)
    delta = (t0 - ks0) + c_w - r_w
    wbias = _tile4(jnp.where((delta >= 0) & (delta < WIN_SIZE), 0.0, NEG))
    ow_t = []
    for g in groups:
        sw = sw_ref[g] + wbias
        mw = jnp.max(sw, axis=0, keepdims=True)
        ewb = jnp.exp2(sw - mw).astype(BF16)
        vwt = jnp.concatenate([vwt_ref[0, kt0 + j, g * V_ROWS:(g + 1) * V_ROWS, :]
                               for j in range(WIN_KEYS // LANES)], axis=1)
        ow_aug = jnp.dot(vwt, ewb, preferred_element_type=F32)
        ow_t.append(ow_aug[0:HEAD_DIM, :] * (1.0 / ow_aug[HEAD_DIM:HEAD_DIM + 1, :]))

    m_ref[...] = jnp.full(m_ref.shape, NEG, F32)
    acc_ref[...] = jnp.zeros(acc_ref.shape, F32)
    n_full = t0 // SLC_TK

    def pair(jj, carry):
        j = 2 * jj
        slc_scores(j + 1, sb_ref)
        slc_update(j, sa_ref, False)
        slc_scores(j + 2, sa_ref)
        slc_update(j + 1, sb_ref, False)
        return carry

    lax.fori_loop(0, n_full // 2, pair, 0)

    @pl.when(n_full % 2 == 1)
    def _():
        slc_scores(n_full, sb_ref)
        slc_update(n_full - 1, sa_ref, False)
        slc_update(n_full, sb_ref, True)

    @pl.when(n_full % 2 == 0)
    def _():
        slc_update(n_full, sa_ref, True)

    gates = _sigmoid(glt_ref[0, 0] + bg_ref[...])
    outs = []
    for g in groups:
        os_t = acc_ref[g, 0:HEAD_DIM, :] * (1.0 / acc_ref[g, HEAD_DIM:HEAD_DIM + 1, :])
        for r in range(NSA_GROUP):
            sl = slice(r * NSA_TQ, (r + 1) * NSA_TQ)
            row = g * gg + r
            outs.append(gates[row:row + 1, :] * oc_t[g][:, sl]
                        + gates[row + NSA_GROUP:row + NSA_GROUP + 1, :] * os_t[:, sl]
                        + gates[row + 2 * NSA_GROUP:row + 2 * NSA_GROUP + 1, :] * ow_t[g][:, sl])
    o_tok = jnp.concatenate(outs, axis=0).T
    ng = ng_ref[0]
    o_ref[0] = (o_tok * (ng * _sigmoid(ng))).astype(BF16)


def _nsa(nqt, glt, ng3, ks, kw, vt, kc, vct, mt, qnw, bg, B, T):
    nt = T // LANES
    ncb = kc.shape[3]
    G = NSA_KV_HEADS
    ncols = NSA_GROUP * NSA_TQ
    return pl.pallas_call(
        _nsa_kernel,
        grid=(B, T // NSA_TQ),
        in_specs=[
            pl.BlockSpec((1, 1, D_NSA, LANES), lambda b, i: (b, i, 0, 0)),
            pl.BlockSpec((1, 1, 16 * G, LANES), lambda b, i: (b, i, 0, 0)),
            pl.BlockSpec((1, NSA_TQ, D_NSA), lambda b, i: (b, i, 0)),
            pl.BlockSpec((1, G, T, PAIR), lambda b, i: (b, 0, 0, 0)),
            pl.BlockSpec((1, G, T, PAIR), lambda b, i: (b, 0, 0, 0)),
            pl.BlockSpec((1, nt, G * V_ROWS, LANES), lambda b, i: (b, 0, 0, 0)),
            pl.BlockSpec((1, nt, G * V_ROWS, LANES), lambda b, i: (b, 0, 1, 0)),
            pl.BlockSpec((1, 1, G, ncb, HEAD_DIM), lambda b, i: (b, 0, 0, 0, 0)),
            pl.BlockSpec((1, 1, G, HEAD_DIM, ncb), lambda b, i: (b, 1, 0, 0, 0)),
            pl.BlockSpec(mt.shape, lambda b, i: (0, 0)),
            pl.BlockSpec((HEAD_DIM, 1), lambda b, i: (0, 0)),
            pl.BlockSpec((16 * G, 1), lambda b, i: (0, 0)),
        ],
        out_specs=pl.BlockSpec((1, NSA_TQ, D_NSA), lambda b, i: (b, i, 0)),
        out_shape=jax.ShapeDtypeStruct((B, T, D_NSA), BF16),
        scratch_shapes=[
            pltpu.VMEM((G, 2 * HEAD_DIM, ncols), BF16),
            pltpu.VMEM((G, SLC_TK, ncols), F32),
            pltpu.VMEM((G, SLC_TK, ncols), F32),
            pltpu.VMEM((G, WIN_KEYS, ncols), F32),
            pltpu.VMEM((G, 1, ncols), F32),
            pltpu.VMEM((G, V_ROWS, ncols), F32),
        ],
        compiler_params=pltpu.CompilerParams(dimension_semantics=("parallel", "arbitrary"),
                                             vmem_limit_bytes=VMEM_LIMIT),
        name="nsa",
    )(nqt, glt, ng3, ks, kw, vt, vt, kc, vct, mt, qnw, bg)


def _out_kernel(x_ref, yr_ref, yn_ref, wr_ref, wn_ref, o_ref):
    o_ref[...] = (x_ref[...]
                  + jnp.dot(yr_ref[...], wr_ref[...], preferred_element_type=F32)
                  + jnp.dot(yn_ref[...], wn_ref[...], preferred_element_type=F32))


def _outproj(x2, yr, yn, wr, wn):
    N = x2.shape[0]
    row = lambda i: (i, 0)
    const = lambda i: (0, 0)
    return pl.pallas_call(
        _out_kernel,
        grid=(N // OUT_TM,),
        in_specs=[pl.BlockSpec((OUT_TM, D_MODEL), row), pl.BlockSpec((OUT_TM, D_RET), row),
                  pl.BlockSpec((OUT_TM, D_NSA), row), pl.BlockSpec(wr.shape, const), pl.BlockSpec(wn.shape, const)],
        out_specs=pl.BlockSpec((OUT_TM, D_MODEL), row),
        out_shape=jax.ShapeDtypeStruct((N, D_MODEL), F32),
        compiler_params=pltpu.CompilerParams(dimension_semantics=("parallel",), vmem_limit_bytes=VMEM_LIMIT),
        name="outproj",
    )(x2, yr, yn, wr, wn)


@functools.lru_cache(maxsize=None)
def _tables(T):
    half = HEAD_DIM // 2
    inv = ROPE_THETA ** (-np.arange(half, dtype=np.float64) / half)
    ang = np.arange(T, dtype=np.float64)[:, None] * inv[None, :]
    cos64 = np.concatenate([np.cos(ang), np.cos(ang)], axis=1)
    sin64 = np.concatenate([-np.sin(ang), np.sin(ang)], axis=1)
    cos = np.concatenate([cos64, cos64], axis=1).astype(np.float32)
    sin = np.concatenate([sin64, sin64], axis=1).astype(np.float32)

    C = RET_CHUNK
    log_g = np.log1p(-np.exp2(-5.0 - np.arange(RET_HEADS, dtype=np.float64)))
    pos = np.arange(C, dtype=np.float64)
    diff = pos[:, None] - pos[None, :]
    decay = np.where(diff >= 0, np.exp(log_g[:, None, None] * np.maximum(diff, 0.0)), 0.0)
    zeta = np.exp(log_g[:, None] * (C - 1.0 - pos))
    xi = np.exp(log_g[:, None] * (pos + 1.0))
    g_chunk = np.exp(log_g * C)
    npair = RET_HEADS // 2

    def pair_lanes(a):
        return np.repeat(a.reshape(npair, 2, C).transpose(0, 2, 1), HEAD_DIM, axis=2).astype(np.float32)

    dec = decay.reshape(npair, 2, C, C).astype(np.float32)
    gch = np.repeat(g_chunk.reshape(npair, 1, 2), HEAD_DIM, axis=2).astype(np.float32)

    n_cmp = (T - CMP_BLOCK) // CMP_STRIDE + 1
    ncb = T // CMP_STRIDE
    p = np.arange(n_cmp)[:, None] * CMP_STRIDE + np.arange(CMP_BLOCK)[None, :]
    blk = p // SLC_BLOCK
    M = (blk[:, :, None] == np.arange(T // SLC_BLOCK)[None, None, :]).mean(axis=1)
    mt = np.zeros((T // SLC_BLOCK, ncb), np.float32)
    mt[:, :n_cmp] = M.T
    return cos, sin, dec, pair_lanes(zeta), pair_lanes(xi), gch, mt


def kernel(x, norm_w, w_in, ret_norm_w, q_norm_w, k_norm_cmp, k_norm_slc, k_norm_win, cmp_pos_k, cmp_w1_k, cmp_w2_k,
           cmp_pos_v, cmp_w1_v, cmp_w2_v, b_gate, w_out):
    B, T, D = x.shape
    depth = norm_w.shape[0]
    cos, sin, dec, zeta, xi, gch, mt = _tables(T)
    ncb = T // CMP_STRIDE
    half = CMP_STRIDE * HEAD_DIM
    gate_src = np.zeros((NSA_KV_HEADS, 16), np.int32)
    gate_ok = np.zeros((NSA_KV_HEADS, 16), bool)
    for g in range(NSA_KV_HEADS):
        for br in range(N_BRANCH):
            for r in range(NSA_GROUP):
                gate_src[g, br * NSA_GROUP + r] = br * NSA_HEADS + g * NSA_GROUP + r
                gate_ok[g, br * NSA_GROUP + r] = True
    gate_src = gate_src.reshape(-1)
    gate_ok = gate_ok.reshape(-1)

    x2 = x.reshape(B * T, D)
    for layer in range(depth):
        w = w_in[layer].astype(BF16)
        o_ng = 4 * D_RET + D_NSA
        o_kv = o_ng + D_NSA
        w_ret = w[:, :4 * D_RET]
        wt_q = w[:, 4 * D_RET:o_ng].T
        w_ng = w[:, o_ng:o_kv]
        w_ckv = w[:, o_kv:o_kv + 2 * D_KV]
        w_skwk = jnp.concatenate([w[:, o_kv + 2 * D_KV:o_kv + 3 * D_KV], w[:, o_kv + 4 * D_KV:o_kv + 5 * D_KV]], axis=1)
        wt_v = jnp.concatenate([w[:, o_kv + 3 * D_KV:o_kv + 4 * D_KV], w[:, o_kv + 5 * D_KV:o_kv + 6 * D_KV]], axis=1).T
        w_gl = w[:, o_kv + 6 * D_KV:]
        wt_g = jnp.where(gate_ok[:, None], w_gl.T[gate_src], jnp.zeros((), BF16))
        bg = jnp.where(gate_ok, b_gate[layer][gate_src], 0.0).reshape(-1, 1)

        knw = jnp.stack([jnp.tile(k_norm_slc[layer], 2), jnp.tile(k_norm_win[layer], 2)])
        ret, ng, xc, ks, kw, nqt, vt, glt = _proj(x2, norm_w[layer].reshape(1, D), w_ret, w_ng, w_ckv, w_skwk,
                                                  wt_q, wt_v, wt_g, knw, B, T)

        nw_pair = ret_norm_w[layer].reshape(RET_HEADS // 2, 1, PAIR)
        y_ret = _retention(ret.reshape(B, T, 4 * D_RET), jnp.asarray(cos), jnp.asarray(sin), jnp.asarray(dec),
                           jnp.asarray(zeta), jnp.asarray(xi), jnp.asarray(gch), nw_pair, B, T)

        pos =jnp.stack([cmp_pos_k[layer], cmp_pos_v[layer]]).reshape(2, 2, half)
        w1 = jnp.stack([cmp_w1_k[layer], cmp_w1_v[layer]]).astype(BF16)
        w2 = jnp.stack([cmp_w2_k[layer], cmp_w2_v[layer]]).astype(BF16)
        w2t = jnp.swapaxes(w2, 1, 2)
        kc, vct = _compress(xc, pos, w1, w2, w2t, k_norm_cmp[layer].reshape(1, HEAD_DIM), B, ncb)

        y_nsa = _nsa(nqt, glt, ng.reshape(B, T, D_NSA), ks, kw, vt, kc, vct, jnp.asarray(mt).astype(BF16),
                     q_norm_w[layer].reshape(HEAD_DIM, 1), bg, B, T)

        wo = w_out[layer].astype(BF16)
        x2 = _outproj(x2, y_ret.reshape(B * T, D_RET), y_nsa.reshape(B * T, D_NSA), wo[:D_RET], wo[D_RET:])
    return x2.reshape(B, T, D)
```

```python
import functools

import numpy as np
import jax
import jax.numpy as jnp
from jax import lax
from jax.experimental import pallas as pl
from jax.experimental.pallas import tpu as pltpu

F32 = jnp.float32
BF16 = jnp.bfloat16

D_MODEL = 1024
HEAD_DIM = 64
HEAD_SHIFT = 6
RET_HEADS = 8
NSA_HEADS = 8
NSA_KV_HEADS = 2
NSA_GROUP = NSA_HEADS // NSA_KV_HEADS
D_RET = RET_HEADS * HEAD_DIM
D_NSA = NSA_HEADS * HEAD_DIM
D_KV = NSA_KV_HEADS * HEAD_DIM
N_BRANCH = 3
RET_CHUNK = 128
ROPE_THETA = 10000.0
CMP_BLOCK = 32
CMP_STRIDE = 16
CMP_HIDDEN = 256
SLC_BLOCK = 64
SLC_SHIFT = 6
SLC_TOPK = 16
WIN_SIZE = 512
EPS = 1e-6
NEG = -1e30
FORCE_BONUS = 1e4
QK_SCALE = HEAD_DIM ** -0.5
LOG2E = 1.4426950408889634
V_ROWS = HEAD_DIM + 16

LANES = 128
SUBLANES = 8
PAIR = 2 * HEAD_DIM
VMEM_LIMIT = 48 * 1024 * 1024

PROJ_TM = 256
RET_TC = 1024
NSA_TQ = 128
SLC_TK = 512
WIN_KEYS = WIN_SIZE + NSA_TQ
OUT_TM = 512

NT_DIMS = (((1,), (1,)), ((), ()))
TN_DIMS = (((0,), (0,)), ((), ()))


def _sigmoid(x):
    return 1.0 / (1.0 + jnp.exp(-x))


def _tile4(a):
    return jnp.concatenate([a, a, a, a], axis=1)


def _proj_kernel(steps_per_batch, x_ref, nw_ref, w_ret_ref, w_ng_ref, w_ckv_ref, w_skwk_ref, wt_q_ref, wt_v_ref,
                 wt_g_ref, knw_ref, ret_ref, ng_ref, xc_ref, ks_ref, kw_ref, nqt_ref, vt_ref, glt_ref, ckv_scr):
    x = x_ref[...]
    ms = jnp.mean(x * x, axis=-1, keepdims=True)
    h = (x * lax.rsqrt(ms + EPS) * nw_ref[...]).astype(BF16)
    ret_ref[...] = jnp.dot(h, w_ret_ref[...], preferred_element_type=F32)
    ng_ref[...] = jnp.dot(h, w_ng_ref[...], preferred_element_type=F32)

    ckv = jnp.dot(h, w_ckv_ref[...], preferred_element_type=F32)
    for half in range(2 * D_KV // LANES):
        ckv_scr[half] = ckv[:, half * LANES:(half + 1) * LANES]
    for l in range(CMP_STRIDE):
        for half in range(2 * D_KV // LANES):
            rows = ckv_scr[half, pl.ds(l, PROJ_TM // CMP_STRIDE, stride=CMP_STRIDE), :]
            for s in range(LANES // HEAD_DIM):
                xc_ref[0, half * (LANES // HEAD_DIM) + s, :, l * HEAD_DIM:(l + 1) * HEAD_DIM] = (
                    rows[:, s * HEAD_DIM:(s + 1) * HEAD_DIM])

    skwk = jnp.dot(h, w_skwk_ref[...], preferred_element_type=F32)
    lane = lax.broadcasted_iota(jnp.int32, (1, PAIR), 1)
    head0 = lane < HEAD_DIM
    m0 = jnp.where(head0, 1.0, 0.0)
    m1 = 1.0 - m0
    t_start = (pl.program_id(0) % steps_per_batch) * PROJ_TM
    tok = t_start + lax.broadcasted_iota(jnp.int32, (PROJ_TM, PAIR), 0)
    col = lax.broadcasted_iota(jnp.int32, (PROJ_TM, PAIR), 1)
    indicator = jnp.where((tok >> SLC_SHIFT) == col - HEAD_DIM, 1.0, 0.0)

    def pair_normed(t, w):
        t2 = t * t
        ms0 = jnp.sum(t2 * m0, axis=-1, keepdims=True)
        ms1 = jnp.sum(t2 * m1, axis=-1, keepdims=True)
        return t * lax.rsqrt(jnp.where(head0, ms0, ms1) * (1.0 / HEAD_DIM) + EPS) * w

    ns = pair_normed(skwk[:, :PAIR], knw_ref[0:1, :])
    nwin = pair_normed(skwk[:, PAIR:], knw_ref[1:2, :])
    for g in range(NSA_KV_HEADS):
        s_g = ns if g == 0 else pltpu.roll(ns, HEAD_DIM, 1)
        w_g = nwin if g == 0 else pltpu.roll(nwin, HEAD_DIM, 1)
        ks_ref[0, g] = jnp.where(head0, s_g, indicator).astype(BF16)
        kw_ref[0, g] = jnp.where(head0, w_g, 0.0).astype(BF16)

    qt =lax.dot_general(wt_q_ref[...], h, NT_DIMS, preferred_element_type=F32)
    vt = lax.dot_general(wt_v_ref[...], h, NT_DIMS, preferred_element_type=F32)
    gt = lax.dot_general(wt_g_ref[...], h, NT_DIMS, preferred_element_type=F32)
    pad_row = lax.broadcasted_iota(jnp.int32, (V_ROWS - HEAD_DIM, LANES), 0)
    ones_pad = jnp.where(pad_row == 0, 1.0, 0.0).astype(BF16)
    for j in range(PROJ_TM // LANES):
        sl = slice(j * LANES, (j + 1) * LANES)
        nqt_ref[0, j] = qt[:, sl]
        for blk in range(2 * NSA_KV_HEADS):
            vt_ref[0, j, blk * V_ROWS:blk * V_ROWS + HEAD_DIM, :] = (
                vt[blk * HEAD_DIM:(blk + 1) * HEAD_DIM, sl].astype(BF16))
            vt_ref[0, j, blk * V_ROWS + HEAD_DIM:(blk + 1) * V_ROWS, :] = ones_pad
        glt_ref[0, j] = gt[:, sl]


def _proj(x2, nw, w_ret, w_ng, w_ckv, w_skwk, wt_q, wt_v, wt_g, knw, B, T):
    N = B * T
    tpb = T // PROJ_TM
    sub = PROJ_TM // LANES
    nt = T // LANES
    const = lambda i: (0, 0)
    row = lambda i: (i, 0)
    trn = lambda i: (i // tpb, i % tpb, 0, 0)
    tokm = lambda i: (i // tpb, 0, i % tpb, 0)
    return pl.pallas_call(
        functools.partial(_proj_kernel, tpb),
        grid=(N // PROJ_TM,),
        in_specs=[
            pl.BlockSpec((PROJ_TM, D_MODEL), row),
            pl.BlockSpec((1, D_MODEL), const),
            pl.BlockSpec(w_ret.shape, const),
            pl.BlockSpec(w_ng.shape, const),
            pl.BlockSpec(w_ckv.shape, const),
            pl.BlockSpec(w_skwk.shape, const),
            pl.BlockSpec(wt_q.shape, const),
            pl.BlockSpec(wt_v.shape, const),
            pl.BlockSpec(wt_g.shape, const),
            pl.BlockSpec(knw.shape, const),
        ],
        out_specs=[
            pl.BlockSpec((PROJ_TM, 4 * D_RET), row),
            pl.BlockSpec((PROJ_TM, D_NSA), row),
            pl.BlockSpec((1, 2 * NSA_KV_HEADS, PROJ_TM // CMP_STRIDE, CMP_STRIDE * HEAD_DIM), tokm),
            pl.BlockSpec((1, NSA_KV_HEADS, PROJ_TM, PAIR), tokm),
            pl.BlockSpec((1, NSA_KV_HEADS, PROJ_TM, PAIR), tokm),
            pl.BlockSpec((1, sub, D_NSA, LANES), trn),
            pl.BlockSpec((1, sub, 2 * NSA_KV_HEADS * V_ROWS, LANES), trn),
            pl.BlockSpec((1, sub, 32, LANES), trn),
        ],
        out_shape=[
            jax.ShapeDtypeStruct((N, 4 * D_RET), F32),
            jax.ShapeDtypeStruct((N, D_NSA), F32),
            jax.ShapeDtypeStruct((B, 2 * NSA_KV_HEADS, T // CMP_STRIDE, CMP_STRIDE * HEAD_DIM), F32),
            jax.ShapeDtypeStruct((B, NSA_KV_HEADS, T, PAIR), BF16),
            jax.ShapeDtypeStruct((B, NSA_KV_HEADS, T, PAIR), BF16),
            jax.ShapeDtypeStruct((B, nt, D_NSA, LANES), F32),
            jax.ShapeDtypeStruct((B, nt, 2 * NSA_KV_HEADS * V_ROWS, LANES), BF16),
            jax.ShapeDtypeStruct((B, nt, 32, LANES), F32),
        ],
        scratch_shapes=[pltpu.VMEM((2 * D_KV // LANES, PROJ_TM, LANES), F32)],
        compiler_params=pltpu.CompilerParams(dimension_semantics=("parallel",), vmem_limit_bytes=VMEM_LIMIT),
        name="proj",
    )(x2, nw, w_ret, w_ng, w_ckv, w_skwk, wt_q, wt_v, wt_g, knw)


def _ret_kernel(q_ref, k_ref, v_ref, g_ref, cos_ref, sin_ref, dec_ref, zeta_ref, xi_ref, gch_ref, nw_ref,
                o_ref, state_ref):
    @pl.when(pl.program_id(2) == 0)
    def _():
        state_ref[...] = jnp.zeros_like(state_ref)

    lane = lax.broadcasted_iota(jnp.int32, (1, PAIR), 1)
    head0 = lane < HEAD_DIM
    m0 = jnp.where(head0, 1.0, 0.0)
    m1 = 1.0 - m0
    first_half = (lane & (HEAD_DIM - 1)) < (HEAD_DIM // 2)
    rr = lax.broadcasted_iota(jnp.int32, (PAIR, PAIR), 0) >> HEAD_SHIFT
    cc = lax.broadcasted_iota(jnp.int32, (PAIR, PAIR), 1) >> HEAD_SHIFT
    blockdiag = jnp.where(rr == cc, 1.0, 0.0)
    zeta = zeta_ref[0]
    xi = xi_ref[0]
    gch = gch_ref[0]
    nw = nw_ref[0]

    for c in range(RET_TC // RET_CHUNK):
        sl = pl.ds(c * RET_CHUNK, RET_CHUNK)
        q = q_ref[0, sl, :]
        k = k_ref[0, sl, :]
        v = v_ref[0, sl, :]
        g = g_ref[0, sl, :]
        cos = cos_ref[sl, :]
        sin = sin_ref[sl, :]

        def rope(t):
            rot = jnp.where(first_half, pltpu.roll(t, PAIR - HEAD_DIM // 2, 1), pltpu.roll(t, HEAD_DIM // 2, 1))
            return t * cos + rot * sin

        qr = rope(q)
        kb = (rope(k) * QK_SCALE).astype(BF16)
        state = state_ref[...]
        o = jnp.dot(qr.astype(BF16), state.astype(BF16), preferred_element_type=F32) * xi
        for h, mh in ((0, m0), (1, m1)):
            s = lax.dot_general((qr * mh).astype(BF16), kb, NT_DIMS, preferred_element_type=F32) * dec_ref[0, h]
            o = o + jnp.dot(s.astype(BF16), (v * mh).astype(BF16), preferred_element_type=F32)
        kv = lax.dot_general(kb, (v * zeta).astype(BF16), TN_DIMS, preferred_element_type=F32)
        state_ref[...] = state * gch + kv * blockdiag

        o2 = o * o
        ms0 = jnp.sum(o2 * m0, axis=-1, keepdims=True)
        ms1 = jnp.sum(o2 * m1, axis=-1, keepdims=True)
        ms = jnp.where(head0, ms0, ms1) * (1.0 / HEAD_DIM)
        y = o * lax.rsqrt(ms + EPS) * nw
        o_ref[0, sl, :] = (y * (g * _sigmoid(g))).astype(BF16)


def _retention(ret3, cos, sin, dec, zeta, xi, gch, nw, B, T):
    npair = RET_HEADS // 2
    blk = lambda col0: pl.BlockSpec((1, RET_TC, PAIR), lambda b, p, i: (b, i, col0 + p))
    tab = pl.BlockSpec((RET_TC, PAIR), lambda b, p, i: (i, 0))
    per_pair3 = lambda shape: pl.BlockSpec((1,) + shape, lambda b, p, i: (p,) + (0,) * len(shape))
    return pl.pallas_call(
        _ret_kernel,
        grid=(B, npair, T // RET_TC),
        in_specs=[blk(0), blk(npair), blk(2 * npair), blk(3 * npair), tab, tab,
                  per_pair3((2, RET_CHUNK, RET_CHUNK)), per_pair3((RET_CHUNK, PAIR)), per_pair3((RET_CHUNK, PAIR)),
                  per_pair3((1, PAIR)), per_pair3((1, PAIR))],
        out_specs=pl.BlockSpec((1, RET_TC, PAIR), lambda b, p, i: (b, i, p)),
        out_shape=jax.ShapeDtypeStruct((B, T, D_RET), BF16),
        scratch_shapes=[pltpu.VMEM((PAIR, PAIR), F32)],
        compiler_params=pltpu.CompilerParams(dimension_semantics=("parallel", "parallel", "arbitrary"),
                                             vmem_limit_bytes=VMEM_LIMIT),
        name="retention",
    )(ret3, ret3, ret3, ret3, cos, sin, dec, zeta, xi, gch, nw)


def _cmp_kernel(x_ref, pos_ref, w1_ref, w2_ref, w2t_ref, knw_ref, o_ref, ot_ref):
    is_key = pl.program_id(1) == 0
    half = CMP_STRIDE * HEAD_DIM
    for g in range(NSA_KV_HEADS):
        x = x_ref[0, g]
        a = jnp.dot((x + pos_ref[0, 0:1, :]).astype(BF16), w1_ref[0, :half, :], preferred_element_type=F32)
        b = jnp.dot((x + pos_ref[0, 1:2, :]).astype(BF16), w1_ref[0, half:, :], preferred_element_type=F32)
        hid = a + pltpu.roll(b, b.shape[0] - 1, 0)
        hid = (hid * _sigmoid(hid)).astype(BF16)
        out = jnp.dot(hid, w2_ref[0], preferred_element_type=F32)
        ms = jnp.mean(out * out, axis=-1, keepdims=True)
        normed = out * lax.rsqrt(ms + EPS) * knw_ref[...]
        o_ref[0, 0, g] = jnp.where(is_key, normed, out)
        ot_ref[0, 0, g] = lax.dot_general(w2t_ref[0], hid, NT_DIMS, preferred_element_type=F32)


def _compress(xc, pos, w1, w2, w2t, knw, B, ncb):
    return pl.pallas_call(
        _cmp_kernel,
        grid=(B, 2),
        in_specs=[
            pl.BlockSpec((1, NSA_KV_HEADS, ncb, CMP_STRIDE * HEAD_DIM), lambda b, s: (b, s, 0, 0)),
            pl.BlockSpec((1, 2, CMP_STRIDE * HEAD_DIM), lambda b, s: (s, 0, 0)),
            pl.BlockSpec((1, CMP_BLOCK * HEAD_DIM, CMP_HIDDEN), lambda b, s: (s, 0, 0)),
            pl.BlockSpec((1, CMP_HIDDEN, HEAD_DIM), lambda b, s: (s, 0, 0)),
            pl.BlockSpec((1, HEAD_DIM, CMP_HIDDEN), lambda b, s: (s, 0, 0)),
            pl.BlockSpec((1, HEAD_DIM), lambda b, s: (0, 0)),
        ],
        out_specs=[
            pl.BlockSpec((1, 1, NSA_KV_HEADS, ncb, HEAD_DIM), lambda b, s: (b, s, 0, 0, 0)),
            pl.BlockSpec((1, 1, NSA_KV_HEADS, HEAD_DIM, ncb), lambda b, s: (b, s, 0, 0, 0)),
        ],
        out_shape=[
            jax.ShapeDtypeStruct((B, 2, NSA_KV_HEADS, ncb, HEAD_DIM), F32),
            jax.ShapeDtypeStruct((B, 2, NSA_KV_HEADS, HEAD_DIM, ncb), F32),
        ],
        compiler_params=pltpu.CompilerParams(dimension_semantics=("parallel", "parallel"),
                                             vmem_limit_bytes=VMEM_LIMIT),
        name="compress",
    )(xc, pos, w1, w2, w2t, knw)


def _nsa_kernel(qt_ref, qtn_ref, glt_ref, ng_ref, ks_ref, kw_ref, vst_ref, vwt_ref, kc_ref, vct_ref, mt_ref, qnw_ref,
                bg_ref, o_ref, qp_ref, oc_ref, sa_ref, sb_ref, sw_ref, m_ref, acc_ref):
    qi = pl.program_id(1)
    t0 = qi * NSA_TQ
    groups = range(NSA_KV_HEADS)
    gq = NSA_GROUP * HEAD_DIM
    gg = 16
    ncb = kc_ref.shape[3]
    n_slc = mt_ref.shape[0]

    def select_scores(src_ref, tile_t0, slot):
        qs = []
        for g in groups:
            cols = []
            for r in range(NSA_GROUP):
                q = src_ref[0, 0, g * gq + r * HEAD_DIM:g * gq + (r + 1) * HEAD_DIM, :]
                ms = jnp.mean(q * q, axis=0, keepdims=True)
                cols.append(q * lax.rsqrt(ms + EPS) * qnw_ref[...] * (QK_SCALE * LOG2E))
            qs.append(jnp.concatenate(cols, axis=1).astype(BF16))
            qp_ref[slot, g, 0:HEAD_DIM, :] = qs[g]

        n_idx = lax.broadcasted_iota(jnp.int32, (ncb, NSA_TQ), 0)
        tok_c = tile_t0 + lax.broadcasted_iota(jnp.int32, (ncb, NSA_TQ), 1)
        cbias = _tile4(jnp.where((n_idx * CMP_STRIDE + (CMP_BLOCK - 1)) <= tok_c, 0.0, NEG))
        return [jnp.dot(kc_ref[0, 0, g].astype(BF16), qs[g], preferred_element_type=F32) + cbias
                for g in groups]

    def select_probs(scs, tile_t0, slot):
        tok_row = tile_t0 + lax.broadcasted_iota(jnp.int32, (1, NSA_TQ), 1)
        has_block = _tile4(jnp.where(tok_row >= CMP_BLOCK - 1, 1.0, 0.0))
        p = []
        for g in groups:
            sc = scs[g]
            mc = jnp.max(sc, axis=0, keepdims=True)
            ec = jnp.exp2(sc - mc)
            lc = jnp.sum(ec, axis=0, keepdims=True)
            p.append(ec * (has_block / lc))
            oc_ref[slot, g] = jnp.dot(vct_ref[0, 0, g].astype(BF16), p[g].astype(BF16),
                                      preferred_element_type=F32)

        jb = lax.broadcasted_iota(jnp.int32, (n_slc, NSA_TQ), 0)
        tok_s = tile_t0 + lax.broadcasted_iota(jnp.int32, (n_slc, NSA_TQ), 1)
        valid_s = jb * SLC_BLOCK <= tok_s
        force = (jb == (tok_s >> SLC_SHIFT)) | (jb == 0)
        scores = []
        for g in groups:
            ps = p[g][:, 0:NSA_TQ]
            for r in range(1, NSA_GROUP):
                ps = ps + p[g][:, r * NSA_TQ:(r + 1) * NSA_TQ]
            ps_hi = ps.astype(BF16)
            ps_lo = (ps - ps_hi.astype(F32)).astype(BF16)
            imp = (jnp.dot(mt_ref[...], ps_hi, preferred_element_type=F32)
                   + jnp.dot(mt_ref[...], ps_lo, preferred_element_type=F32))
            scores.append(jnp.where(valid_s, jnp.where(force, imp + FORCE_BONUS, imp), NEG))
        return scores, valid_s

    def select_rank(scores, valid_s, slot):
        sub = lax.broadcasted_iota(jnp.int32, (SUBLANES, NSA_TQ), 0)
        for g in groups:
            score = scores[g]
            blocks = [score[v * SUBLANES:(v + 1) * SUBLANES, :] for v in range(n_slc // SUBLANES)]
            ranks = [jnp.zeros((SUBLANES, NSA_TQ), F32) for _ in blocks]
            for i in range(n_slc):
                row = score[i:i + 1, :]
                for v, blk in enumerate(blocks):
                    if v * SUBLANES > i:
                        beats = row >= blk
                    elif (v + 1) * SUBLANES <= i:
                        beats = row > blk
                    else:
                        beats = (row > blk) | ((row >= blk) & (sub > i - v * SUBLANES))
                    ranks[v] = ranks[v] + jnp.where(beats, 1.0, 0.0)
            rank = jnp.concatenate(ranks, axis=0)
            sel = (rank < float(SLC_TOPK)) & valid_s
            qp_ref[slot, g, HEAD_DIM:2 * HEAD_DIM, :] = _tile4(jnp.where(sel, 0.0, NEG).astype(BF16))

    @pl.when(qi == 0)
    def _():
        scores0, valid0 = select_probs(select_scores(qt_ref, 0, 0), 0, 0)
        select_rank(scores0, valid0, 0)

    cur = qi % 2
    nxt = 1 - cur

    cmp_scores = select_scores(qtn_ref, t0 + NSA_TQ, nxt)

    kt0 = jnp.maximum(qi - WIN_SIZE // NSA_TQ, 0)
    ks0 = pl.multiple_of(kt0 * NSA_TQ, NSA_TQ)
    for g in groups:
        sw_ref[g] = jnp.dot(kw_ref[0, g, pl.ds(ks0, WIN_KEYS), 0:HEAD_DIM], qp_ref[cur, g, 0:HEAD_DIM, :],
                            preferred_element_type=F32)

    vt_per_tile = SLC_TK // LANES

    def slc_scores(j, dst_ref):
        kst = pl.multiple_of(j * SLC_TK, SLC_TK)
        for g in groups:
            dst_ref[g] = jnp.dot(ks_ref[0, g, pl.ds(kst, SLC_TK), :], qp_ref[cur, g], preferred_element_type=F32)

    def slc_update(j, src_ref, causal):
        if causal:
            r_s = lax.broadcasted_iota(jnp.int32, (SLC_TK, NSA_TQ), 0)
            c_s = lax.broadcasted_iota(jnp.int32, (SLC_TK, NSA_TQ), 1)
            causal_bias = _tile4(jnp.where(j * SLC_TK + r_s <= t0 + c_s, 0.0, NEG))
        for g in groups:
            s = src_ref[g]
            if causal:
                s = s + causal_bias
            m_old = m_ref[g]
            m_new = jnp.maximum(m_old, jnp.max(s, axis=0, keepdims=True))
            alpha = jnp.exp2(m_old - m_new)
            eb = jnp.exp2(s - m_new).astype(BF16)
            vt = jnp.concatenate([vst_ref[0, j * vt_per_tile + jj, g * V_ROWS:(g + 1) * V_ROWS, :]
                                  for jj in range(vt_per_tile)], axis=1)
            acc_ref[g] = alpha * acc_ref[g] + jnp.dot(vt, eb, preferred_element_type=F32)
            m_ref[g] = m_new

    sel_scores, sel_valid = select_probs(cmp_scores, t0 + NSA_TQ, nxt)
    slc_scores(0, sa_ref)

    r_w = lax.broadcasted_iota(jnp.int32, (WIN_KEYS, NSA_TQ), 0)
    c_w = lax.broadcasted_iota(jnp.int32, (WIN_KEYS, NSA_TQ), 1)
    delta = (t0 - ks0) + c_w - r_w
    wbias = _tile4(jnp.where((delta >= 0) & (delta < WIN_SIZE), 0.0, NEG))
    ow_t = []
    for g in groups:
        sw = sw_ref[g] + wbias
        mw = jnp.max(sw, axis=0, keepdims=True)
        ewb = jnp.exp2(sw - mw).astype(BF16)
        vwt = jnp.concatenate([vwt_ref[0, kt0 + j, g * V_ROWS:(g + 1) * V_ROWS, :]
                               for j in range(WIN_KEYS // LANES)], axis=1)
        ow_aug = jnp.dot(vwt, ewb, preferred_element_type=F32)
        ow_t.append(ow_aug[0:HEAD_DIM, :] * (1.0 / ow_aug[HEAD_DIM:HEAD_DIM + 1, :]))

    select_rank(sel_scores, sel_valid, nxt)

    m_ref[...] = jnp.full(m_ref.shape, NEG, F32)
    acc_ref[...] = jnp.zeros(acc_ref.shape, F32)
    n_full = t0 // SLC_TK

    def pair(jj, carry):
        j = 2 * jj
        slc_scores(j + 1, sb_ref)
        slc_update(j, sa_ref, False)
        slc_scores(j + 2, sa_ref)
        slc_update(j + 1, sb_ref, False)
        return carry

    lax.fori_loop(0, n_full // 2, pair, 0)

    @pl.when(n_full % 2 == 1)
    def _():
        slc_scores(n_full, sb_ref)
        slc_update(n_full - 1, sa_ref, False)
        slc_update(n_full, sb_ref, True)

    @pl.when(n_full % 2 == 0)
    def _():
        slc_update(n_full, sa_ref, True)

    gates = _sigmoid(glt_ref[0, 0] + bg_ref[...])
    outs = []
    for g in groups:
        os_t = acc_ref[g, 0:HEAD_DIM, :] * (1.0 / acc_ref[g, HEAD_DIM:HEAD_DIM + 1, :])
        oc_t = oc_ref[cur, g]
        for r in range(NSA_GROUP):
            sl = slice(r * NSA_TQ, (r + 1) * NSA_TQ)
            row = g * gg + r
            outs.append(gates[row:row + 1, :] * oc_t[:, sl]
                        + gates[row + NSA_GROUP:row + NSA_GROUP + 1, :] * os_t[:, sl]
                        + gates[row + 2 * NSA_GROUP:row + 2 * NSA_GROUP + 1, :] * ow_t[g][:, sl])
    o_tok = jnp.concatenate(outs, axis=0).T
    ng = ng_ref[0]
    o_ref[0] = (o_tok * (ng * _sigmoid(ng))).astype(BF16)


def _nsa(nqt, glt, ng3, ks, kw, vt, kc, vct, mt, qnw, bg, B, T):
    nt = T // LANES
    ncb = kc.shape[3]
    G = NSA_KV_HEADS
    ncols = NSA_GROUP * NSA_TQ
    return pl.pallas_call(
        _nsa_kernel,
        grid=(B, T // NSA_TQ),
        in_specs=[
            pl.BlockSpec((1, 1, D_NSA, LANES), lambda b, i: (b, i, 0, 0)),
            pl.BlockSpec((1, 1, D_NSA, LANES), lambda b, i: (b, jnp.minimum(i + 1, nt - 1), 0, 0)),
            pl.BlockSpec((1, 1, 16 * G, LANES), lambda b, i: (b, i, 0, 0)),
            pl.BlockSpec((1, NSA_TQ, D_NSA), lambda b, i: (b, i, 0)),
            pl.BlockSpec((1, G, T, PAIR), lambda b, i: (b, 0, 0, 0)),
            pl.BlockSpec((1, G, T, PAIR), lambda b, i: (b, 0, 0, 0)),
            pl.BlockSpec((1, nt, G * V_ROWS, LANES), lambda b, i: (b, 0, 0, 0)),
            pl.BlockSpec((1, nt, G * V_ROWS, LANES), lambda b, i: (b, 0, 1, 0)),
            pl.BlockSpec((1, 1, G, ncb, HEAD_DIM), lambda b, i: (b, 0, 0, 0, 0)),
            pl.BlockSpec((1, 1, G, HEAD_DIM, ncb), lambda b, i: (b, 1, 0, 0, 0)),
            pl.BlockSpec(mt.shape, lambda b, i: (0, 0)),
            pl.BlockSpec((HEAD_DIM, 1), lambda b, i: (0, 0)),
            pl.BlockSpec((16 * G, 1), lambda b, i: (0, 0)),
        ],
        out_specs=pl.BlockSpec((1, NSA_TQ, D_NSA), lambda b, i: (b, i, 0)),
        out_shape=jax.ShapeDtypeStruct((B, T, D_NSA), BF16),
        scratch_shapes=[
            pltpu.VMEM((2, G, 2 * HEAD_DIM, ncols), BF16),
            pltpu.VMEM((2, G, HEAD_DIM, ncols), F32),
            pltpu.VMEM((G, SLC_TK, ncols), F32),
            pltpu.VMEM((G, SLC_TK, ncols), F32),
            pltpu.VMEM((G, WIN_KEYS, ncols), F32),
            pltpu.VMEM((G, 1, ncols), F32),
            pltpu.VMEM((G, V_ROWS, ncols), F32),
        ],
        compiler_params=pltpu.CompilerParams(dimension_semantics=("parallel", "arbitrary"),
                                             vmem_limit_bytes=VMEM_LIMIT),
        name="nsa",
    )(nqt, nqt, glt, ng3, ks, kw, vt, vt, kc, vct, mt, qnw, bg)


def _out_kernel(x_ref, yr_ref, yn_ref, wr_ref, wn_ref, o_ref):
    o_ref[...] = (x_ref[...]
                  + jnp.dot(yr_ref[...], wr_ref[...], preferred_element_type=F32)
                  + jnp.dot(yn_ref[...], wn_ref[...], preferred_element_type=F32))


def _outproj(x2, yr, yn, wr, wn):
    N = x2.shape[0]
    row = lambda i: (i, 0)
    const = lambda i: (0, 0)
    return pl.pallas_call(
        _out_kernel,
        grid=(N // OUT_TM,),
        in_specs=[pl.BlockSpec((OUT_TM, D_MODEL), row), pl.BlockSpec((OUT_TM, D_RET), row),
                  pl.BlockSpec((OUT_TM, D_NSA), row), pl.BlockSpec(wr.shape, const), pl.BlockSpec(wn.shape, const)],
        out_specs=pl.BlockSpec((OUT_TM, D_MODEL), row),
        out_shape=jax.ShapeDtypeStruct((N, D_MODEL), F32),
        compiler_params=pltpu.CompilerParams(dimension_semantics=("parallel",), vmem_limit_bytes=VMEM_LIMIT),
        name="outproj",
    )(x2, yr, yn, wr, wn)


@functools.lru_cache(maxsize=None)
def _tables(T):
    half = HEAD_DIM // 2
    inv = ROPE_THETA ** (-np.arange(half, dtype=np.float64) / half)
    ang = np.arange(T, dtype=np.float64)[:, None] * inv[None, :]
    cos64 = np.concatenate([np.cos(ang), np.cos(ang)], axis=1)
    sin64 = np.concatenate([-np.sin(ang), np.sin(ang)], axis=1)
    cos = np.concatenate([cos64, cos64], axis=1).astype(np.float32)
    sin = np.concatenate([sin64, sin64], axis=1).astype(np.float32)

    C = RET_CHUNK
    log_g = np.log1p(-np.exp2(-5.0 - np.arange(RET_HEADS, dtype=np.float64)))
    pos = np.arange(C, dtype=np.float64)
    diff = pos[:, None] - pos[None, :]
    decay = np.where(diff >= 0, np.exp(log_g[:, None, None] * np.maximum(diff, 0.0)), 0.0)
    zeta = np.exp(log_g[:, None] * (C - 1.0 - pos))
    xi = np.exp(log_g[:, None] * (pos + 1.0))
    g_chunk = np.exp(log_g * C)
    npair = RET_HEADS // 2

    def pair_lanes(a):
        return np.repeat(a.reshape(npair, 2, C).transpose(0, 2, 1), HEAD_DIM, axis=2).astype(np.float32)

    dec = decay.reshape(npair, 2, C, C).astype(np.float32)
    gch = np.repeat(g_chunk.reshape(npair, 1, 2), HEAD_DIM, axis=2).astype(np.float32)

    n_cmp = (T - CMP_BLOCK) // CMP_STRIDE + 1
    ncb = T // CMP_STRIDE
    p = np.arange(n_cmp)[:, None] * CMP_STRIDE + np.arange(CMP_BLOCK)[None, :]
    blk = p // SLC_BLOCK
    M = (blk[:, :, None] == np.arange(T // SLC_BLOCK)[None, None, :]).mean(axis=1)
    mt = np.zeros((T // SLC_BLOCK, ncb), np.float32)
    mt[:, :n_cmp] = M.T
    return cos, sin, dec, pair_lanes(zeta), pair_lanes(xi), gch, mt


def kernel(x, norm_w, w_in, ret_norm_w, q_norm_w, k_norm_cmp, k_norm_slc, k_norm_win, cmp_pos_k, cmp_w1_k, cmp_w2_k,
           cmp_pos_v, cmp_w1_v, cmp_w2_v, b_gate, w_out):
    B, T, D = x.shape
    depth = norm_w.shape[0]
    cos, sin, dec, zeta, xi, gch, mt = _tables(T)
    ncb = T // CMP_STRIDE
    half = CMP_STRIDE * HEAD_DIM
    gate_src = np.zeros((NSA_KV_HEADS, 16), np.int32)
    gate_ok = np.zeros((NSA_KV_HEADS, 16), bool)
    for g in range(NSA_KV_HEADS):
        for br in range(N_BRANCH):
            for r in range(NSA_GROUP):
                gate_src[g, br * NSA_GROUP + r] = br * NSA_HEADS + g * NSA_GROUP + r
                gate_ok[g, br * NSA_GROUP + r] = True
    gate_src = gate_src.reshape(-1)
    gate_ok = gate_ok.reshape(-1)

    x2 = x.reshape(B * T, D)
    for layer in range(depth):
        w = w_in[layer].astype(BF16)
        o_ng = 4 * D_RET + D_NSA
        o_kv = o_ng + D_NSA
        w_ret = w[:, :4 * D_RET]
        wt_q = w[:, 4 * D_RET:o_ng].T
        w_ng = w[:, o_ng:o_kv]
        w_ckv = w[:, o_kv:o_kv + 2 * D_KV]
        w_skwk = jnp.concatenate([w[:, o_kv + 2 * D_KV:o_kv + 3 * D_KV], w[:, o_kv + 4 * D_KV:o_kv + 5 * D_KV]], axis=1)
        wt_v = jnp.concatenate([w[:, o_kv + 3 * D_KV:o_kv + 4 * D_KV], w[:, o_kv + 5 * D_KV:o_kv + 6 * D_KV]], axis=1).T
        w_gl = w[:, o_kv + 6 * D_KV:]
        wt_g = jnp.where(gate_ok[:, None], w_gl.T[gate_src], jnp.zeros((), BF16))
        bg = jnp.where(gate_ok, b_gate[layer][gate_src], 0.0).reshape(-1, 1)

        knw = jnp.stack([jnp.tile(k_norm_slc[layer], 2), jnp.tile(k_norm_win[layer], 2)])
        ret, ng, xc, ks, kw, nqt, vt, glt = _proj(x2, norm_w[layer].reshape(1, D), w_ret, w_ng, w_ckv, w_skwk,
                                                  wt_q, wt_v, wt_g, knw, B, T)

        nw_pair = ret_norm_w[layer].reshape(RET_HEADS // 2, 1, PAIR)
        y_ret = _retention(ret.reshape(B, T, 4 * D_RET), jnp.asarray(cos), jnp.asarray(sin), jnp.asarray(dec),
                           jnp.asarray(zeta), jnp.asarray(xi), jnp.asarray(gch), nw_pair, B, T)

        pos =jnp.stack([cmp_pos_k[layer], cmp_pos_v[layer]]).reshape(2, 2, half)
        w1 = jnp.stack([cmp_w1_k[layer], cmp_w1_v[layer]]).astype(BF16)
        w2 = jnp.stack([cmp_w2_k[layer], cmp_w2_v[layer]]).astype(BF16)
        w2t = jnp.swapaxes(w2, 1, 2)
        kc, vct = _compress(xc, pos, w1, w2, w2t, k_norm_cmp[layer].reshape(1, HEAD_DIM), B, ncb)

        y_nsa = _nsa(nqt, glt, ng.reshape(B, T, D_NSA), ks, kw, vt, kc, vct, jnp.asarray(mt).astype(BF16),
                     q_norm_w[layer].reshape(HEAD_DIM, 1), bg, B, T)

        wo = w_out[layer].astype(BF16)
        x2 = _outproj(x2, y_ret.reshape(B * T, D_RET), y_nsa.reshape(B * T, D_NSA), wo[:D_RET], wo[D_RET:])
    return x2.reshape(B, T, D)
```

```python
import functools

import numpy as np
import jax
import jax.numpy as jnp
from jax import lax
from jax.experimental import pallas as pl
from jax.experimental.pallas import tpu as pltpu

F32 = jnp.float32
BF16 = jnp.bfloat16

D_MODEL = 1024
HEAD_DIM = 64
HEAD_SHIFT = 6
RET_HEADS = 8
NSA_HEADS = 8
NSA_KV_HEADS = 2
NSA_GROUP = NSA_HEADS // NSA_KV_HEADS
D_RET = RET_HEADS * HEAD_DIM
D_NSA = NSA_HEADS * HEAD_DIM
D_KV = NSA_KV_HEADS * HEAD_DIM
N_BRANCH = 3
RET_CHUNK = 128
ROPE_THETA = 10000.0
CMP_BLOCK = 32
CMP_STRIDE = 16
CMP_HIDDEN = 256
SLC_BLOCK = 64
SLC_SHIFT = 6
SLC_TOPK = 16
WIN_SIZE = 512
EPS = 1e-6
NEG = -1e30
FORCE_BONUS = 1e4
QK_SCALE = HEAD_DIM ** -0.5
LOG2E = 1.4426950408889634
V_ROWS = HEAD_DIM + 16

LANES = 128
SUBLANES = 8
PAIR = 2 * HEAD_DIM
VMEM_LIMIT = 48 * 1024 * 1024

PROJ_TM = 256
RET_TC = 512
NSA_TQ = 128
SLC_TK = 512
WIN_KEYS = WIN_SIZE + NSA_TQ
OUT_TM = 512

NT_DIMS = (((1,), (1,)), ((), ()))
TN_DIMS = (((0,), (0,)), ((), ()))


def _sigmoid(x):
    return 1.0 / (1.0 + jnp.exp(-x))


def _tile4(a):
    return jnp.concatenate([a, a, a, a], axis=1)


def _proj_kernel(steps_per_batch, x_ref, nw_ref, w_ret_ref, w_ng_ref, w_ckv_ref, w_skwk_ref, wt_q_ref, wt_v_ref,
                 wt_g_ref, knw_ref, ret_ref, ng_ref, xc_ref, ks_ref, kw_ref, nqt_ref, vt_ref, glt_ref, ckv_scr):
    x = x_ref[...]
    ms = jnp.mean(x * x, axis=-1, keepdims=True)
    h = (x * lax.rsqrt(ms + EPS) * nw_ref[...]).astype(BF16)
    ret_ref[...] = jnp.dot(h, w_ret_ref[...], preferred_element_type=F32)
    ng_ref[...] = jnp.dot(h, w_ng_ref[...], preferred_element_type=F32)

    ckv = jnp.dot(h, w_ckv_ref[...], preferred_element_type=F32)
    for half in range(2 * D_KV // LANES):
        ckv_scr[half] = ckv[:, half * LANES:(half + 1) * LANES]
    for l in range(CMP_STRIDE):
        for half in range(2 * D_KV // LANES):
            rows = ckv_scr[half, pl.ds(l, PROJ_TM // CMP_STRIDE, stride=CMP_STRIDE), :]
            for s in range(LANES // HEAD_DIM):
                xc_ref[0, half * (LANES // HEAD_DIM) + s, :, l * HEAD_DIM:(l + 1) * HEAD_DIM] = (
                    rows[:, s * HEAD_DIM:(s + 1) * HEAD_DIM])

    skwk = jnp.dot(h, w_skwk_ref[...], preferred_element_type=F32)
    lane = lax.broadcasted_iota(jnp.int32, (1, PAIR), 1)
    head0 = lane < HEAD_DIM
    m0 = jnp.where(head0, 1.0, 0.0)
    m1 = 1.0 - m0
    t_start = (pl.program_id(0) % steps_per_batch) * PROJ_TM
    tok = t_start + lax.broadcasted_iota(jnp.int32, (PROJ_TM, PAIR), 0)
    col = lax.broadcasted_iota(jnp.int32, (PROJ_TM, PAIR), 1)
    indicator = jnp.where((tok >> SLC_SHIFT) == col - HEAD_DIM, 1.0, 0.0)

    def pair_normed(t, w):
        t2 = t * t
        ms0 = jnp.sum(t2 * m0, axis=-1, keepdims=True)
        ms1 = jnp.sum(t2 * m1, axis=-1, keepdims=True)
        return t * lax.rsqrt(jnp.where(head0, ms0, ms1) * (1.0 / HEAD_DIM) + EPS) * w

    ns = pair_normed(skwk[:, :PAIR], knw_ref[0:1, :])
    nwin = pair_normed(skwk[:, PAIR:], knw_ref[1:2, :])
    for g in range(NSA_KV_HEADS):
        s_g = ns if g == 0 else pltpu.roll(ns, HEAD_DIM, 1)
        w_g = nwin if g == 0 else pltpu.roll(nwin, HEAD_DIM, 1)
        ks_ref[0, g] = jnp.where(head0, s_g, indicator).astype(BF16)
        kw_ref[0, g] = jnp.where(head0, w_g, 0.0).astype(BF16)

    qt =lax.dot_general(wt_q_ref[...], h, NT_DIMS, preferred_element_type=F32)
    vt = lax.dot_general(wt_v_ref[...], h, NT_DIMS, preferred_element_type=F32)
    gt = lax.dot_general(wt_g_ref[...], h, NT_DIMS, preferred_element_type=F32)
    pad_row = lax.broadcasted_iota(jnp.int32, (V_ROWS - HEAD_DIM, LANES), 0)
    ones_pad = jnp.where(pad_row == 0, 1.0, 0.0).astype(BF16)
    for j in range(PROJ_TM // LANES):
        sl = slice(j * LANES, (j + 1) * LANES)
        nqt_ref[0, j] = qt[:, sl]
        for blk in range(2 * NSA_KV_HEADS):
            vt_ref[0, j, blk * V_ROWS:blk * V_ROWS + HEAD_DIM, :] = (
                vt[blk * HEAD_DIM:(blk + 1) * HEAD_DIM, sl].astype(BF16))
            vt_ref[0, j, blk * V_ROWS + HEAD_DIM:(blk + 1) * V_ROWS, :] = ones_pad
        glt_ref[0, j] = gt[:, sl]


def _proj(x2, nw, w_ret, w_ng, w_ckv, w_skwk, wt_q, wt_v, wt_g, knw, B, T):
    N = B * T
    tpb = T // PROJ_TM
    sub = PROJ_TM // LANES
    nt = T // LANES
    const = lambda i: (0, 0)
    row = lambda i: (i, 0)
    trn = lambda i: (i // tpb, i % tpb, 0, 0)
    tokm = lambda i: (i // tpb, 0, i % tpb, 0)
    return pl.pallas_call(
        functools.partial(_proj_kernel, tpb),
        grid=(N // PROJ_TM,),
        in_specs=[
            pl.BlockSpec((PROJ_TM, D_MODEL), row),
            pl.BlockSpec((1, D_MODEL), const),
            pl.BlockSpec(w_ret.shape, const),
            pl.BlockSpec(w_ng.shape, const),
            pl.BlockSpec(w_ckv.shape, const),
            pl.BlockSpec(w_skwk.shape, const),
            pl.BlockSpec(wt_q.shape, const),
            pl.BlockSpec(wt_v.shape, const),
            pl.BlockSpec(wt_g.shape, const),
            pl.BlockSpec(knw.shape, const),
        ],
        out_specs=[
            pl.BlockSpec((PROJ_TM, 4 * D_RET), row),
            pl.BlockSpec((PROJ_TM, D_NSA), row),
            pl.BlockSpec((1, 2 * NSA_KV_HEADS, PROJ_TM // CMP_STRIDE, CMP_STRIDE * HEAD_DIM), tokm),
            pl.BlockSpec((1, NSA_KV_HEADS, PROJ_TM, PAIR), tokm),
            pl.BlockSpec((1, NSA_KV_HEADS, PROJ_TM, PAIR), tokm),
            pl.BlockSpec((1, sub, D_NSA, LANES), trn),
            pl.BlockSpec((1, sub, 2 * NSA_KV_HEADS * V_ROWS, LANES), trn),
            pl.BlockSpec((1, sub, 32, LANES), trn),
        ],
        out_shape=[
            jax.ShapeDtypeStruct((N, 4 * D_RET), F32),
            jax.ShapeDtypeStruct((N, D_NSA), F32),
            jax.ShapeDtypeStruct((B, 2 * NSA_KV_HEADS, T // CMP_STRIDE, CMP_STRIDE * HEAD_DIM), F32),
            jax.ShapeDtypeStruct((B, NSA_KV_HEADS, T, PAIR), BF16),
            jax.ShapeDtypeStruct((B, NSA_KV_HEADS, T, PAIR), BF16),
            jax.ShapeDtypeStruct((B, nt, D_NSA, LANES), F32),
            jax.ShapeDtypeStruct((B, nt, 2 * NSA_KV_HEADS * V_ROWS, LANES), BF16),
            jax.ShapeDtypeStruct((B, nt, 32, LANES), F32),
        ],
        scratch_shapes=[pltpu.VMEM((2 * D_KV // LANES, PROJ_TM, LANES), F32)],
        compiler_params=pltpu.CompilerParams(dimension_semantics=("parallel",), vmem_limit_bytes=VMEM_LIMIT),
        name="proj",
    )(x2, nw, w_ret, w_ng, w_ckv, w_skwk, wt_q, wt_v, wt_g, knw)


def _ret_kernel(q_ref, k_ref, v_ref, g_ref, cos_ref, sin_ref, dec_ref, zeta_ref, xi_ref, gch_ref, nw_ref,
                o_ref, state_ref):
    @pl.when(pl.program_id(1) == 0)
    def _():
        state_ref[...] = jnp.zeros_like(state_ref)

    lane = lax.broadcasted_iota(jnp.int32, (1, PAIR), 1)
    q_head = (lane >> (HEAD_SHIFT - 1)) & 1
    v_head = lane >> HEAD_SHIFT
    q_mask = [jnp.where(q_head == h, 1.0, 0.0).astype(BF16) for h in (0, 1)]
    v_mask = [jnp.where(v_head == h, 1.0, 0.0).astype(BF16) for h in (0, 1)]
    row_qh = (lax.broadcasted_iota(jnp.int32, (PAIR, PAIR), 0) >> (HEAD_SHIFT - 1)) & 1
    row_vh = lax.broadcasted_iota(jnp.int32, (PAIR, PAIR), 0) >> HEAD_SHIFT
    col_vh = lax.broadcasted_iota(jnp.int32, (PAIR, PAIR), 1) >> HEAD_SHIFT
    same_head_kv = jnp.where(row_qh == col_vh, 1.0, 0.0)
    head_mean = jnp.where(row_vh == col_vh, 1.0 / HEAD_DIM, 0.0).astype(BF16)

    pairs = range(RET_HEADS // 2)
    mean2 = jnp.concatenate([head_mean, head_mean], axis=0)
    for c in range(RET_TC // RET_CHUNK):
        sl = pl.ds(c * RET_CHUNK, RET_CHUNK)
        cos = cos_ref[sl, :]
        sin = sin_ref[sl, :]
        cols = [slice(p * PAIR, (p + 1) * PAIR) for p in pairs]
        qb, kb, vb, vzb = [], [], [], []
        for p in pairs:
            q = q_ref[0, sl, cols[p]]
            k = k_ref[0, sl, cols[p]]
            v = v_ref[0, sl, cols[p]]
            qb.append((q * cos + pltpu.roll(q, HEAD_DIM, 1) * sin).astype(BF16))
            kb.append(((k * cos + pltpu.roll(k, HEAD_DIM, 1) * sin) * QK_SCALE).astype(BF16))
            vb.append(v.astype(BF16))
            vzb.append((v * zeta_ref[p]).astype(BF16))
        states = [state_ref[p] for p in pairs]
        s = [[lax.dot_general(qb[p] * q_mask[h], kb[p], NT_DIMS, preferred_element_type=F32) for h in (0, 1)]
             for p in pairs]
        o_cross = [jnp.dot(qb[p], states[p].astype(BF16), preferred_element_type=F32) for p in pairs]
        kv = [lax.dot_general(kb[p], vzb[p], TN_DIMS, preferred_element_type=F32) for p in pairs]
        sb = [jnp.concatenate([(s[p][h] * dec_ref[p, h]).astype(BF16) for h in (0, 1)], axis=1) for p in pairs]
        vv = [jnp.concatenate([vb[p] * v_mask[h] for h in (0, 1)], axis=0) for p in pairs]
        o = [jnp.dot(sb[p], vv[p], preferred_element_type=F32) + o_cross[p] * xi_ref[p] for p in pairs]
        for p in pairs:
            state_ref[p] = states[p] * gch_ref[p] + kv[p] * same_head_kv
        o2 = [o[p] * o[p] for p in pairs]
        o2_hi = [o2[p].astype(BF16) for p in pairs]
        o2_hl = [jnp.concatenate([o2_hi[p], (o2[p] - o2_hi[p].astype(F32)).astype(BF16)], axis=1) for p in pairs]
        ms = [jnp.dot(o2_hl[p], mean2, preferred_element_type=F32) for p in pairs]
        for p in pairs:
            g = g_ref[0, sl, cols[p]]
            y = o[p] * lax.rsqrt(ms[p] + EPS) * nw_ref[p]
            o_ref[0, sl, cols[p]] = (y * (g * _sigmoid(g))).astype(BF16)


def _retention(ret3, cos, sin, dec, zeta, xi, gch, nw, B, T):
    npair = RET_HEADS // 2
    blk = lambda col: pl.BlockSpec((1, RET_TC, D_RET), lambda b, i: (b, i, col))
    tab = pl.BlockSpec((RET_TC, PAIR), lambda b, i: (i, 0))
    whole = lambda a: pl.BlockSpec(a.shape, lambda b, i: (0,) * a.ndim)
    return pl.pallas_call(
        _ret_kernel,
        grid=(B, T // RET_TC),
        in_specs=[blk(0), blk(1), blk(2), blk(3), tab, tab, whole(dec), whole(zeta), whole(xi), whole(gch), whole(nw)],
        out_specs=pl.BlockSpec((1, RET_TC, D_RET), lambda b, i: (b, i, 0)),
        out_shape=jax.ShapeDtypeStruct((B, T, D_RET), BF16),
        scratch_shapes=[pltpu.VMEM((npair, PAIR, PAIR), F32)],
        compiler_params=pltpu.CompilerParams(dimension_semantics=("parallel", "arbitrary"),
                                             vmem_limit_bytes=VMEM_LIMIT),
        name="retention",
    )(ret3, ret3, ret3, ret3, cos, sin, dec, zeta, xi, gch, nw)


def _cmp_kernel(x_ref, pos_ref, w1_ref, w2_ref, w2t_ref, knw_ref, o_ref, ot_ref):
    is_key = pl.program_id(1) == 0
    half = CMP_STRIDE * HEAD_DIM
    for g in range(NSA_KV_HEADS):
        x = x_ref[0, g]
        a = jnp.dot((x + pos_ref[0, 0:1, :]).astype(BF16), w1_ref[0, :half, :], preferred_element_type=F32)
        b = jnp.dot((x + pos_ref[0, 1:2, :]).astype(BF16), w1_ref[0, half:, :], preferred_element_type=F32)
        hid = a + pltpu.roll(b, b.shape[0] - 1, 0)
        hid = (hid * _sigmoid(hid)).astype(BF16)
        out = jnp.dot(hid, w2_ref[0], preferred_element_type=F32)
        ms = jnp.mean(out * out, axis=-1, keepdims=True)
        normed = out * lax.rsqrt(ms + EPS) * knw_ref[...]
        o_ref[0, 0, g] = jnp.where(is_key, normed, out)
        ot_ref[0, 0, g] = lax.dot_general(w2t_ref[0], hid, NT_DIMS, preferred_element_type=F32)


def _compress(xc, pos, w1, w2, w2t, knw, B, ncb):
    return pl.pallas_call(
        _cmp_kernel,
        grid=(B, 2),
        in_specs=[
            pl.BlockSpec((1, NSA_KV_HEADS, ncb, CMP_STRIDE * HEAD_DIM), lambda b, s: (b, s, 0, 0)),
            pl.BlockSpec((1, 2, CMP_STRIDE * HEAD_DIM), lambda b, s: (s, 0, 0)),
            pl.BlockSpec((1, CMP_BLOCK * HEAD_DIM, CMP_HIDDEN), lambda b, s: (s, 0, 0)),
            pl.BlockSpec((1, CMP_HIDDEN, HEAD_DIM), lambda b, s: (s, 0, 0)),
            pl.BlockSpec((1, HEAD_DIM, CMP_HIDDEN), lambda b, s: (s, 0, 0)),
            pl.BlockSpec((1, HEAD_DIM), lambda b, s: (0, 0)),
        ],
        out_specs=[
            pl.BlockSpec((1, 1, NSA_KV_HEADS, ncb, HEAD_DIM), lambda b, s: (b, s, 0, 0, 0)),
            pl.BlockSpec((1, 1, NSA_KV_HEADS, HEAD_DIM, ncb), lambda b, s: (b, s, 0, 0, 0)),
        ],
        out_shape=[
            jax.ShapeDtypeStruct((B, 2, NSA_KV_HEADS, ncb, HEAD_DIM), F32),
            jax.ShapeDtypeStruct((B, 2, NSA_KV_HEADS, HEAD_DIM, ncb), F32),
        ],
        compiler_params=pltpu.CompilerParams(dimension_semantics=("parallel", "parallel"),
                                             vmem_limit_bytes=VMEM_LIMIT),
        name="compress",
    )(xc, pos, w1, w2, w2t, knw)


def _nsa_kernel(qt_ref, qtn_ref, glt_ref, ng_ref, ks_ref, kw_ref, vst_ref, vwt_ref, kc_ref, vct_ref, mt_ref, qnw_ref,
                bg_ref, o_ref, qp_ref, oc_ref, sa_ref, sb_ref, sw_ref, m_ref, acc_ref):
    qi = pl.program_id(1)
    t0 = qi * NSA_TQ
    groups = range(NSA_KV_HEADS)
    gq = NSA_GROUP * HEAD_DIM
    gg = 16
    ncb = kc_ref.shape[3]
    n_slc = mt_ref.shape[0]

    def select_scores(src_ref, tile_t0, slot):
        qs = []
        for g in groups:
            cols = []
            for r in range(NSA_GROUP):
                q = src_ref[0, 0, g * gq + r * HEAD_DIM:g * gq + (r + 1) * HEAD_DIM, :]
                ms = jnp.mean(q * q, axis=0, keepdims=True)
                cols.append(q * lax.rsqrt(ms + EPS) * qnw_ref[...] * (QK_SCALE * LOG2E))
            qs.append(jnp.concatenate(cols, axis=1).astype(BF16))
            qp_ref[slot, g, 0:HEAD_DIM, :] = qs[g]

        n_idx = lax.broadcasted_iota(jnp.int32, (ncb, NSA_TQ), 0)
        tok_c = tile_t0 + lax.broadcasted_iota(jnp.int32, (ncb, NSA_TQ), 1)
        cbias = _tile4(jnp.where((n_idx * CMP_STRIDE + (CMP_BLOCK - 1)) <= tok_c, 0.0, NEG))
        return [jnp.dot(kc_ref[0, 0, g].astype(BF16), qs[g], preferred_element_type=F32) + cbias
                for g in groups]

    def select_probs(scs, tile_t0, slot):
        tok_row = tile_t0 + lax.broadcasted_iota(jnp.int32, (1, NSA_TQ), 1)
        has_block = _tile4(jnp.where(tok_row >= CMP_BLOCK - 1, 1.0, 0.0))
        p = []
        for g in groups:
            sc = scs[g]
            mc = jnp.max(sc, axis=0, keepdims=True)
            ec = jnp.exp2(sc - mc)
            lc = jnp.sum(ec, axis=0, keepdims=True)
            p.append(ec * (has_block / lc))
            oc_ref[slot, g] = jnp.dot(vct_ref[0, 0, g].astype(BF16), p[g].astype(BF16),
                                      preferred_element_type=F32)

        jb = lax.broadcasted_iota(jnp.int32, (n_slc, NSA_TQ), 0)
        tok_s = tile_t0 + lax.broadcasted_iota(jnp.int32, (n_slc, NSA_TQ), 1)
        valid_s = jb * SLC_BLOCK <= tok_s
        force = (jb == (tok_s >> SLC_SHIFT)) | (jb == 0)
        scores = []
        for g in groups:
            ps = p[g][:, 0:NSA_TQ]
            for r in range(1, NSA_GROUP):
                ps = ps + p[g][:, r * NSA_TQ:(r + 1) * NSA_TQ]
            ps_hi = ps.astype(BF16)
            ps_lo = (ps - ps_hi.astype(F32)).astype(BF16)
            imp = (jnp.dot(mt_ref[...], ps_hi, preferred_element_type=F32)
                   + jnp.dot(mt_ref[...], ps_lo, preferred_element_type=F32))
            scores.append(jnp.where(valid_s, jnp.where(force, imp + FORCE_BONUS, imp), NEG))
        return scores, valid_s

    def select_rank(scores, valid_s, slot):
        sub = lax.broadcasted_iota(jnp.int32, (SUBLANES, NSA_TQ), 0)
        for g in groups:
            score = scores[g]
            blocks = [score[v * SUBLANES:(v + 1) * SUBLANES, :] for v in range(n_slc // SUBLANES)]
            ranks = [jnp.zeros((SUBLANES, NSA_TQ), F32) for _ in blocks]
            for i in range(n_slc):
                row = score[i:i + 1, :]
                for v, blk in enumerate(blocks):
                    if v * SUBLANES > i:
                        beats = row >= blk
                    elif (v + 1) * SUBLANES <= i:
                        beats = row > blk
                    else:
                        beats = (row > blk) | ((row >= blk) & (sub > i - v * SUBLANES))
                    ranks[v] = ranks[v] + jnp.where(beats, 1.0, 0.0)
            rank = jnp.concatenate(ranks, axis=0)
            sel = (rank < float(SLC_TOPK)) & valid_s
            qp_ref[slot, g, HEAD_DIM:2 * HEAD_DIM, :] = _tile4(jnp.where(sel, 0.0, NEG).astype(BF16))

    @pl.when(qi == 0)
    def _():
        scores0, valid0 = select_probs(select_scores(qt_ref, 0, 0), 0, 0)
        select_rank(scores0, valid0, 0)

    cur = qi % 2
    nxt = 1 - cur

    kt0 = jnp.maximum(qi - WIN_SIZE // NSA_TQ, 0)
    ks0 = pl.multiple_of(kt0 * NSA_TQ, NSA_TQ)
    for g in groups:
        sw_ref[g] = jnp.dot(kw_ref[0, g, pl.ds(ks0, WIN_KEYS), 0:HEAD_DIM], qp_ref[cur, g, 0:HEAD_DIM, :],
                            preferred_element_type=F32)

    cmp_scores = select_scores(qtn_ref, t0 + NSA_TQ, nxt)

    vt_per_tile = SLC_TK // LANES

    def slc_scores(j, dst_ref):
        kst = pl.multiple_of(j * SLC_TK, SLC_TK)
        for g in groups:
            dst_ref[g] = jnp.dot(ks_ref[0, g, pl.ds(kst, SLC_TK), :], qp_ref[cur, g], preferred_element_type=F32)

    def slc_update(j, src_ref, causal):
        if causal:
            r_s = lax.broadcasted_iota(jnp.int32, (SLC_TK, NSA_TQ), 0)
            c_s = lax.broadcasted_iota(jnp.int32, (SLC_TK, NSA_TQ), 1)
            causal_bias = _tile4(jnp.where(j * SLC_TK + r_s <= t0 + c_s, 0.0, NEG))
        for g in groups:
            s = src_ref[g]
            if causal:
                s = s + causal_bias
            m_old = m_ref[g]
            m_new = jnp.maximum(m_old, jnp.max(s, axis=0, keepdims=True))
            alpha = jnp.exp2(m_old - m_new)
            eb = jnp.exp2(s - m_new).astype(BF16)
            vt = jnp.concatenate([vst_ref[0, j * vt_per_tile + jj, g * V_ROWS:(g + 1) * V_ROWS, :]
                                  for jj in range(vt_per_tile)], axis=1)
            acc_ref[g] = alpha * acc_ref[g] + jnp.dot(vt, eb, preferred_element_type=F32)
            m_ref[g] = m_new

    slc_scores(0, sa_ref)
    sel_scores, sel_valid = select_probs(cmp_scores, t0 + NSA_TQ, nxt)

    r_w = lax.broadcasted_iota(jnp.int32, (WIN_KEYS, NSA_TQ), 0)
    c_w = lax.broadcasted_iota(jnp.int32, (WIN_KEYS, NSA_TQ), 1)
    delta = (t0 - ks0) + c_w - r_w
    wbias = _tile4(jnp.where((delta >= 0) & (delta < WIN_SIZE), 0.0, NEG))
    ow_t = []
    for g in groups:
        sw = sw_ref[g] + wbias
        mw = jnp.max(sw, axis=0, keepdims=True)
        ewb = jnp.exp2(sw - mw).astype(BF16)
        vwt = jnp.concatenate([vwt_ref[0, kt0 + j, g * V_ROWS:(g + 1) * V_ROWS, :]
                               for j in range(WIN_KEYS // LANES)], axis=1)
        ow_aug = jnp.dot(vwt, ewb, preferred_element_type=F32)
        ow_t.append(ow_aug[0:HEAD_DIM, :] * (1.0 / ow_aug[HEAD_DIM:HEAD_DIM + 1, :]))

    select_rank(sel_scores, sel_valid, nxt)

    m_ref[...] = jnp.full(m_ref.shape, NEG, F32)
    acc_ref[...] = jnp.zeros(acc_ref.shape, F32)
    n_full = t0 // SLC_TK

    def pair(jj, carry):
        j = 2 * jj
        slc_scores(j + 1, sb_ref)
        slc_update(j, sa_ref, False)
        slc_scores(j + 2, sa_ref)
        slc_update(j + 1, sb_ref, False)
        return carry

    lax.fori_loop(0, n_full // 2, pair, 0)

    @pl.when(n_full % 2 == 1)
    def _():
        slc_scores(n_full, sb_ref)
        slc_update(n_full - 1, sa_ref, False)
        slc_update(n_full, sb_ref, True)

    @pl.when(n_full % 2 == 0)
    def _():
        slc_update(n_full, sa_ref, True)

    gates = _sigmoid(glt_ref[0, 0] + bg_ref[...])
    outs = []
    for g in groups:
        os_t = acc_ref[g, 0:HEAD_DIM, :] * (1.0 / acc_ref[g, HEAD_DIM:HEAD_DIM + 1, :])
        oc_t = oc_ref[cur, g]
        for r in range(NSA_GROUP):
            sl = slice(r * NSA_TQ, (r + 1) * NSA_TQ)
            row = g * gg + r
            outs.append(gates[row:row + 1, :] * oc_t[:, sl]
                        + gates[row + NSA_GROUP:row + NSA_GROUP + 1, :] * os_t[:, sl]
                        + gates[row + 2 * NSA_GROUP:row + 2 * NSA_GROUP + 1, :] * ow_t[g][:, sl])
    o_tok = jnp.concatenate(outs, axis=0).T
    ng = ng_ref[0]
    o_ref[0] = (o_tok * (ng * _sigmoid(ng))).astype(BF16)


def _nsa(nqt, glt, ng3, ks, kw, vt, kc, vct, mt, qnw, bg, B, T):
    nt = T // LANES
    ncb = kc.shape[3]
    G = NSA_KV_HEADS
    ncols = NSA_GROUP * NSA_TQ
    return pl.pallas_call(
        _nsa_kernel,
        grid=(B, T // NSA_TQ),
        in_specs=[
            pl.BlockSpec((1, 1, D_NSA, LANES), lambda b, i: (b, i, 0, 0)),
            pl.BlockSpec((1, 1, D_NSA, LANES), lambda b, i: (b, jnp.minimum(i + 1, nt - 1), 0, 0)),
            pl.BlockSpec((1, 1, 16 * G, LANES), lambda b, i: (b, i, 0, 0)),
            pl.BlockSpec((1, NSA_TQ, D_NSA), lambda b, i: (b, i, 0)),
            pl.BlockSpec((1, G, T, PAIR), lambda b, i: (b, 0, 0, 0)),
            pl.BlockSpec((1, G, T, PAIR), lambda b, i: (b, 0, 0, 0)),
            pl.BlockSpec((1, nt, G * V_ROWS, LANES), lambda b, i: (b, 0, 0, 0)),
            pl.BlockSpec((1, nt, G * V_ROWS, LANES), lambda b, i: (b, 0, 1, 0)),
            pl.BlockSpec((1, 1, G, ncb, HEAD_DIM), lambda b, i: (b, 0, 0, 0, 0)),
            pl.BlockSpec((1, 1, G, HEAD_DIM, ncb), lambda b, i: (b, 1, 0, 0, 0)),
            pl.BlockSpec(mt.shape, lambda b, i: (0, 0)),
            pl.BlockSpec((HEAD_DIM, 1), lambda b, i: (0, 0)),
            pl.BlockSpec((16 * G, 1), lambda b, i: (0, 0)),
        ],
        out_specs=pl.BlockSpec((1, NSA_TQ, D_NSA), lambda b, i: (b, i, 0)),
        out_shape=jax.ShapeDtypeStruct((B, T, D_NSA), BF16),
        scratch_shapes=[
            pltpu.VMEM((2, G, 2 * HEAD_DIM, ncols), BF16),
            pltpu.VMEM((2, G, HEAD_DIM, ncols), F32),
            pltpu.VMEM((G, SLC_TK, ncols), F32),
            pltpu.VMEM((G, SLC_TK, ncols), F32),
            pltpu.VMEM((G, WIN_KEYS, ncols), F32),
            pltpu.VMEM((G, 1, ncols), F32),
            pltpu.VMEM((G, V_ROWS, ncols), F32),
        ],
        compiler_params=pltpu.CompilerParams(dimension_semantics=("parallel", "arbitrary"),
                                             vmem_limit_bytes=VMEM_LIMIT),
        name="nsa",
    )(nqt, nqt, glt, ng3, ks, kw, vt, vt, kc, vct, mt, qnw, bg)


def _out_kernel(x_ref, yr_ref, yn_ref, wr_ref, wn_ref, o_ref):
    o_ref[...] = (x_ref[...]
                  + jnp.dot(yr_ref[...], wr_ref[...], preferred_element_type=F32)
                  + jnp.dot(yn_ref[...], wn_ref[...], preferred_element_type=F32))


def _outproj(x2, yr, yn, wr, wn):
    N = x2.shape[0]
    row = lambda i: (i, 0)
    const = lambda i: (0, 0)
    return pl.pallas_call(
        _out_kernel,
        grid=(N // OUT_TM,),
        in_specs=[pl.BlockSpec((OUT_TM, D_MODEL), row), pl.BlockSpec((OUT_TM, D_RET), row),
                  pl.BlockSpec((OUT_TM, D_NSA), row), pl.BlockSpec(wr.shape, const), pl.BlockSpec(wn.shape, const)],
        out_specs=pl.BlockSpec((OUT_TM, D_MODEL), row),
        out_shape=jax.ShapeDtypeStruct((N, D_MODEL), F32),
        compiler_params=pltpu.CompilerParams(dimension_semantics=("parallel",), vmem_limit_bytes=VMEM_LIMIT),
        name="outproj",
    )(x2, yr, yn, wr, wn)


@functools.lru_cache(maxsize=None)
def _tables(T):
    half = HEAD_DIM // 2
    inv = ROPE_THETA ** (-np.arange(half, dtype=np.float64) / half)
    ang = np.arange(T, dtype=np.float64)[:, None] * inv[None, :]
    cos = np.concatenate([np.cos(ang)] * 4, axis=1).astype(np.float32)
    sin = np.concatenate([-np.sin(ang), -np.sin(ang), np.sin(ang), np.sin(ang)], axis=1).astype(np.float32)

    C = RET_CHUNK
    log_g = np.log1p(-np.exp2(-5.0 - np.arange(RET_HEADS, dtype=np.float64)))
    pos = np.arange(C, dtype=np.float64)
    diff = pos[:, None] - pos[None, :]
    decay = np.where(diff >= 0, np.exp(log_g[:, None, None] * np.maximum(diff, 0.0)), 0.0)
    zeta = np.exp(log_g[:, None] * (C - 1.0 - pos))
    xi = np.exp(log_g[:, None] * (pos + 1.0))
    g_chunk = np.exp(log_g * C)
    npair = RET_HEADS // 2

    def pair_lanes(a):
        return np.repeat(a.reshape(npair, 2, C).transpose(0, 2, 1), HEAD_DIM, axis=2).astype(np.float32)

    dec = decay.reshape(npair, 2, C, C).astype(np.float32)
    gch = np.repeat(g_chunk.reshape(npair, 1, 2), HEAD_DIM, axis=2).astype(np.float32)

    n_cmp = (T - CMP_BLOCK) // CMP_STRIDE + 1
    ncb = T // CMP_STRIDE
    p = np.arange(n_cmp)[:, None] * CMP_STRIDE + np.arange(CMP_BLOCK)[None, :]
    blk = p // SLC_BLOCK
    M = (blk[:, :, None] == np.arange(T // SLC_BLOCK)[None, None, :]).mean(axis=1)
    mt = np.zeros((T // SLC_BLOCK, ncb), np.float32)
    mt[:, :n_cmp] = M.T
    return cos, sin, dec, pair_lanes(zeta), pair_lanes(xi), gch, mt


def kernel(x, norm_w, w_in, ret_norm_w, q_norm_w, k_norm_cmp, k_norm_slc, k_norm_win, cmp_pos_k, cmp_w1_k, cmp_w2_k,
           cmp_pos_v, cmp_w1_v, cmp_w2_v, b_gate, w_out):
    B, T, D = x.shape
    depth = norm_w.shape[0]
    cos, sin, dec, zeta, xi, gch, mt = _tables(T)
    ncb = T // CMP_STRIDE
    half = CMP_STRIDE * HEAD_DIM
    gate_src = np.zeros((NSA_KV_HEADS, 16), np.int32)
    gate_ok = np.zeros((NSA_KV_HEADS, 16), bool)
    for g in range(NSA_KV_HEADS):
        for br in range(N_BRANCH):
            for r in range(NSA_GROUP):
                gate_src[g, br * NSA_GROUP + r] = br * NSA_HEADS + g * NSA_GROUP + r
                gate_ok[g, br * NSA_GROUP + r] = True
    gate_src = gate_src.reshape(-1)
    gate_ok = gate_ok.reshape(-1)

    x2 = x.reshape(B * T, D)
    for layer in range(depth):
        w = w_in[layer].astype(BF16)
        o_ng = 4 * D_RET + D_NSA
        o_kv = o_ng + D_NSA
        quarter = HEAD_DIM // 2
        pair_perm = np.concatenate([np.arange(quarter), HEAD_DIM + np.arange(quarter),
                                    quarter + np.arange(quarter), HEAD_DIM + quarter + np.arange(quarter)])
        qk_perm = np.concatenate([p * PAIR + pair_perm for p in range(RET_HEADS // 2)])
        w_ret = jnp.concatenate([w[:, :D_RET][:, qk_perm], w[:, D_RET:2 * D_RET][:, qk_perm],
                                 w[:, 2 * D_RET:4 * D_RET]], axis=1)
        wt_q = w[:, 4 * D_RET:o_ng].T
        w_ng = w[:, o_ng:o_kv]
        w_ckv = w[:, o_kv:o_kv + 2 * D_KV]
        w_skwk = jnp.concatenate([w[:, o_kv + 2 * D_KV:o_kv + 3 * D_KV], w[:, o_kv + 4 * D_KV:o_kv + 5 * D_KV]], axis=1)
        wt_v = jnp.concatenate([w[:, o_kv + 3 * D_KV:o_kv + 4 * D_KV], w[:, o_kv + 5 * D_KV:o_kv + 6 * D_KV]], axis=1).T
        w_gl = w[:, o_kv + 6 * D_KV:]
        wt_g = jnp.where(gate_ok[:, None], w_gl.T[gate_src], jnp.zeros((), BF16))
        bg = jnp.where(gate_ok, b_gate[layer][gate_src], 0.0).reshape(-1, 1)

        knw = jnp.stack([jnp.tile(k_norm_slc[layer], 2), jnp.tile(k_norm_win[layer], 2)])
        ret, ng, xc, ks, kw, nqt, vt, glt = _proj(x2, norm_w[layer].reshape(1, D), w_ret, w_ng, w_ckv, w_skwk,
                                                  wt_q, wt_v, wt_g, knw, B, T)

        nw_pair = ret_norm_w[layer].reshape(RET_HEADS // 2, 1, PAIR)
        y_ret = _retention(ret.reshape(B, T, 4 * D_RET), jnp.asarray(cos), jnp.asarray(sin), jnp.asarray(dec),
                           jnp.asarray(zeta), jnp.asarray(xi), jnp.asarray(gch), nw_pair, B, T)

        pos =jnp.stack([cmp_pos_k[layer], cmp_pos_v[layer]]).reshape(2, 2, half)
        w1 = jnp.stack([cmp_w1_k[layer], cmp_w1_v[layer]]).astype(BF16)
        w2 = jnp.stack([cmp_w2_k[layer], cmp_w2_v[layer]]).astype(BF16)
        w2t = jnp.swapaxes(w2, 1, 2)
        kc, vct = _compress(xc, pos, w1, w2, w2t, k_norm_cmp[layer].reshape(1, HEAD_DIM), B, ncb)

        y_nsa = _nsa(nqt, glt, ng.reshape(B, T, D_NSA), ks, kw, vt, kc, vct, jnp.asarray(mt).astype(BF16),
                     q_norm_w[layer].reshape(HEAD_DIM, 1), bg, B, T)

        wo = w_out[layer].astype(BF16)
        x2 = _outproj(x2, y_ret.reshape(B * T, D_RET), y_nsa.reshape(B * T, D_NSA), wo[:D_RET], wo[D_RET:])
    return x2.reshape(B, T, D)
```

```python
import functools

import numpy as np
import jax
import jax.numpy as jnp
from jax import lax
from jax.experimental import pallas as pl
from jax.experimental.pallas import tpu as pltpu

F32 = jnp.float32
BF16 = jnp.bfloat16

D_MODEL = 1024
HEAD_DIM = 64
HEAD_SHIFT = 6
RET_HEADS = 8
NSA_HEADS = 8
NSA_KV_HEADS = 2
NSA_GROUP = NSA_HEADS // NSA_KV_HEADS
D_RET = RET_HEADS * HEAD_DIM
D_NSA = NSA_HEADS * HEAD_DIM
D_KV = NSA_KV_HEADS * HEAD_DIM
N_BRANCH = 3
RET_CHUNK = 128
ROPE_THETA = 10000.0
CMP_BLOCK = 32
CMP_STRIDE = 16
CMP_HIDDEN = 256
SLC_BLOCK = 64
SLC_SHIFT = 6
SLC_TOPK = 16
WIN_SIZE = 512
EPS = 1e-6
NEG = -1e30
FORCE_BONUS = 1e4
QK_SCALE = HEAD_DIM ** -0.5
LOG2E = 1.4426950408889634
V_ROWS = HEAD_DIM + 16

LANES = 128
SUBLANES = 8
PAIR = 2 * HEAD_DIM
VMEM_LIMIT = 48 * 1024 * 1024

PROJ_TM = 256
RET_TC = 512
NSA_TQ = 128
NSA_QT = 2
SLC_TK = 512
WIN_KEYS = WIN_SIZE + NSA_TQ
OUT_TM = 512

NT_DIMS = (((1,), (1,)), ((), ()))
TN_DIMS = (((0,), (0,)), ((), ()))


def _sigmoid(x):
    return 1.0 / (1.0 + jnp.exp(-x))


def _tile4(a):
    return jnp.concatenate([a, a, a, a], axis=1)


def _proj_kernel(steps_per_batch, x_ref, nw_ref, w_ret_ref, w_ng_ref, w_ckv_ref, w_skwk_ref, wt_q_ref, wt_v_ref,
                 wt_g_ref, knw_ref, ret_ref, ng_ref, xc_ref, ks_ref, kw_ref, nqt_ref, vt_ref, glt_ref, ckv_scr):
    x = x_ref[...]
    ms = jnp.mean(x * x, axis=-1, keepdims=True)
    h = (x * lax.rsqrt(ms + EPS) * nw_ref[...]).astype(BF16)
    ret_ref[...] = jnp.dot(h, w_ret_ref[...], preferred_element_type=F32)
    ng_ref[...] = jnp.dot(h, w_ng_ref[...], preferred_element_type=F32)

    ckv = jnp.dot(h, w_ckv_ref[...], preferred_element_type=F32)
    for half in range(2 * D_KV // LANES):
        ckv_scr[half] = ckv[:, half * LANES:(half + 1) * LANES]
    for l in range(CMP_STRIDE):
        for half in range(2 * D_KV // LANES):
            rows = ckv_scr[half, pl.ds(l, PROJ_TM // CMP_STRIDE, stride=CMP_STRIDE), :]
            for s in range(LANES // HEAD_DIM):
                xc_ref[0, half * (LANES // HEAD_DIM) + s, :, l * HEAD_DIM:(l + 1) * HEAD_DIM] = (
                    rows[:, s * HEAD_DIM:(s + 1) * HEAD_DIM])

    skwk = jnp.dot(h, w_skwk_ref[...], preferred_element_type=F32)
    lane = lax.broadcasted_iota(jnp.int32, (1, PAIR), 1)
    head0 = lane < HEAD_DIM
    m0 = jnp.where(head0, 1.0, 0.0)
    m1 = 1.0 - m0
    t_start = (pl.program_id(0) % steps_per_batch) * PROJ_TM
    tok = t_start + lax.broadcasted_iota(jnp.int32, (PROJ_TM, PAIR), 0)
    col = lax.broadcasted_iota(jnp.int32, (PROJ_TM, PAIR), 1)
    indicator = jnp.where((tok >> SLC_SHIFT) == col - HEAD_DIM, 1.0, 0.0)

    def pair_normed(t, w):
        t2 = t * t
        ms0 = jnp.sum(t2 * m0, axis=-1, keepdims=True)
        ms1 = jnp.sum(t2 * m1, axis=-1, keepdims=True)
        return t * lax.rsqrt(jnp.where(head0, ms0, ms1) * (1.0 / HEAD_DIM) + EPS) * w

    ns = pair_normed(skwk[:, :PAIR], knw_ref[0:1, :])
    nwin = pair_normed(skwk[:, PAIR:], knw_ref[1:2, :])
    for g in range(NSA_KV_HEADS):
        s_g = ns if g == 0 else pltpu.roll(ns, HEAD_DIM, 1)
        w_g = nwin if g == 0 else pltpu.roll(nwin, HEAD_DIM, 1)
        ks_ref[0, g] = jnp.where(head0, s_g, indicator).astype(BF16)
        kw_ref[0, g] = jnp.where(head0, w_g, 0.0).astype(BF16)

    qt =lax.dot_general(wt_q_ref[...], h, NT_DIMS, preferred_element_type=F32)
    vt = lax.dot_general(wt_v_ref[...], h, NT_DIMS, preferred_element_type=F32)
    gt = lax.dot_general(wt_g_ref[...], h, NT_DIMS, preferred_element_type=F32)
    pad_row = lax.broadcasted_iota(jnp.int32, (V_ROWS - HEAD_DIM, LANES), 0)
    ones_pad = jnp.where(pad_row == 0, 1.0, 0.0).astype(BF16)
    for j in range(PROJ_TM // LANES):
        sl = slice(j * LANES, (j + 1) * LANES)
        nqt_ref[0, j] = qt[:, sl]
        for blk in range(2 * NSA_KV_HEADS):
            vt_ref[0, j, blk * V_ROWS:blk * V_ROWS + HEAD_DIM, :] = (
                vt[blk * HEAD_DIM:(blk + 1) * HEAD_DIM, sl].astype(BF16))
            vt_ref[0, j, blk * V_ROWS + HEAD_DIM:(blk + 1) * V_ROWS, :] = ones_pad
        glt_ref[0, j] = gt[:, sl]


def _proj(x2, nw, w_ret, w_ng, w_ckv, w_skwk, wt_q, wt_v, wt_g, knw, B, T):
    N = B * T
    tpb = T // PROJ_TM
    sub = PROJ_TM // LANES
    nt = T // LANES
    const = lambda i: (0, 0)
    row = lambda i: (i, 0)
    trn = lambda i: (i // tpb, i % tpb, 0, 0)
    tokm = lambda i: (i // tpb, 0, i % tpb, 0)
    return pl.pallas_call(
        functools.partial(_proj_kernel, tpb),
        grid=(N // PROJ_TM,),
        in_specs=[
            pl.BlockSpec((PROJ_TM, D_MODEL), row),
            pl.BlockSpec((1, D_MODEL), const),
            pl.BlockSpec(w_ret.shape, const),
            pl.BlockSpec(w_ng.shape, const),
            pl.BlockSpec(w_ckv.shape, const),
            pl.BlockSpec(w_skwk.shape, const),
            pl.BlockSpec(wt_q.shape, const),
            pl.BlockSpec(wt_v.shape, const),
            pl.BlockSpec(wt_g.shape, const),
            pl.BlockSpec(knw.shape, const),
        ],
        out_specs=[
            pl.BlockSpec((PROJ_TM, 4 * D_RET), row),
            pl.BlockSpec((PROJ_TM, D_NSA), row),
            pl.BlockSpec((1, 2 * NSA_KV_HEADS, PROJ_TM // CMP_STRIDE, CMP_STRIDE * HEAD_DIM), tokm),
            pl.BlockSpec((1, NSA_KV_HEADS, PROJ_TM, PAIR), tokm),
            pl.BlockSpec((1, NSA_KV_HEADS, PROJ_TM, PAIR), tokm),
            pl.BlockSpec((1, sub, D_NSA, LANES), trn),
            pl.BlockSpec((1, sub, 2 * NSA_KV_HEADS * V_ROWS, LANES), trn),
            pl.BlockSpec((1, sub, 32, LANES), trn),
        ],
        out_shape=[
            jax.ShapeDtypeStruct((N, 4 * D_RET), F32),
            jax.ShapeDtypeStruct((N, D_NSA), F32),
            jax.ShapeDtypeStruct((B, 2 * NSA_KV_HEADS, T // CMP_STRIDE, CMP_STRIDE * HEAD_DIM), F32),
            jax.ShapeDtypeStruct((B, NSA_KV_HEADS, T, PAIR), BF16),
            jax.ShapeDtypeStruct((B, NSA_KV_HEADS, T, PAIR), BF16),
            jax.ShapeDtypeStruct((B, nt, D_NSA, LANES), F32),
            jax.ShapeDtypeStruct((B, nt, 2 * NSA_KV_HEADS * V_ROWS, LANES), BF16),
            jax.ShapeDtypeStruct((B, nt, 32, LANES), F32),
        ],
        scratch_shapes=[pltpu.VMEM((2 * D_KV // LANES, PROJ_TM, LANES), F32)],
        compiler_params=pltpu.CompilerParams(dimension_semantics=("parallel",), vmem_limit_bytes=VMEM_LIMIT),
        name="proj",
    )(x2, nw, w_ret, w_ng, w_ckv, w_skwk, wt_q, wt_v, wt_g, knw)


def _ret_kernel(q_ref, k_ref, v_ref, g_ref, cos_ref, sin_ref, dec_ref, zeta_ref, xi_ref, gch_ref, nw_ref,
                o_ref, state_ref):
    @pl.when(pl.program_id(1) == 0)
    def _():
        state_ref[...] = jnp.zeros_like(state_ref)

    lane = lax.broadcasted_iota(jnp.int32, (1, PAIR), 1)
    q_head = (lane >> (HEAD_SHIFT - 1)) & 1
    v_head = lane >> HEAD_SHIFT
    q_mask = [jnp.where(q_head == h, 1.0, 0.0).astype(BF16) for h in (0, 1)]
    v_mask = [jnp.where(v_head == h, 1.0, 0.0).astype(BF16) for h in (0, 1)]
    row_qh = (lax.broadcasted_iota(jnp.int32, (PAIR, PAIR), 0) >> (HEAD_SHIFT - 1)) & 1
    row_vh = lax.broadcasted_iota(jnp.int32, (PAIR, PAIR), 0) >> HEAD_SHIFT
    col_vh = lax.broadcasted_iota(jnp.int32, (PAIR, PAIR), 1) >> HEAD_SHIFT
    same_head_kv = jnp.where(row_qh == col_vh, 1.0, 0.0)
    head_mean = jnp.where(row_vh == col_vh, 1.0 / HEAD_DIM, 0.0).astype(BF16)

    pairs = range(RET_HEADS // 2)
    mean2 = jnp.concatenate([head_mean, head_mean], axis=0)
    for c in range(RET_TC // RET_CHUNK):
        sl = pl.ds(c * RET_CHUNK, RET_CHUNK)
        cos = cos_ref[sl, :]
        sin = sin_ref[sl, :]
        cols = [slice(p * PAIR, (p + 1) * PAIR) for p in pairs]
        qb, kb, vb, vzb = [], [], [], []
        for p in pairs:
            q = q_ref[0, sl, cols[p]]
            k = k_ref[0, sl, cols[p]]
            v = v_ref[0, sl, cols[p]]
            qb.append((q * cos + pltpu.roll(q, HEAD_DIM, 1) * sin).astype(BF16))
            kb.append(((k * cos + pltpu.roll(k, HEAD_DIM, 1) * sin) * QK_SCALE).astype(BF16))
            vb.append(v.astype(BF16))
            vzb.append((v * zeta_ref[p]).astype(BF16))
        states = [state_ref[p] for p in pairs]
        s = [[lax.dot_general(qb[p] * q_mask[h], kb[p], NT_DIMS, preferred_element_type=F32) for h in (0, 1)]
             for p in pairs]
        o_cross = [jnp.dot(qb[p], states[p].astype(BF16), preferred_element_type=F32) for p in pairs]
        kv = [lax.dot_general(kb[p], vzb[p], TN_DIMS, preferred_element_type=F32) for p in pairs]
        sb = [jnp.concatenate([(s[p][h] * dec_ref[p, h]).astype(BF16) for h in (0, 1)], axis=1) for p in pairs]
        vv = [jnp.concatenate([vb[p] * v_mask[h] for h in (0, 1)], axis=0) for p in pairs]
        o = [jnp.dot(sb[p], vv[p], preferred_element_type=F32) + o_cross[p] * xi_ref[p] for p in pairs]
        for p in pairs:
            state_ref[p] = states[p] * gch_ref[p] + kv[p] * same_head_kv
        o2 = [o[p] * o[p] for p in pairs]
        o2_hi = [o2[p].astype(BF16) for p in pairs]
        o2_hl = [jnp.concatenate([o2_hi[p], (o2[p] - o2_hi[p].astype(F32)).astype(BF16)], axis=1) for p in pairs]
        ms = [jnp.dot(o2_hl[p], mean2, preferred_element_type=F32) for p in pairs]
        for p in pairs:
            g = g_ref[0, sl, cols[p]]
            y = o[p] * lax.rsqrt(ms[p] + EPS) * nw_ref[p]
            o_ref[0, sl, cols[p]] = (y * (g * _sigmoid(g))).astype(BF16)


def _retention(ret3, cos, sin, dec, zeta, xi, gch, nw, B, T):
    npair = RET_HEADS // 2
    blk = lambda col: pl.BlockSpec((1, RET_TC, D_RET), lambda b, i: (b, i, col))
    tab = pl.BlockSpec((RET_TC, PAIR), lambda b, i: (i, 0))
    whole = lambda a: pl.BlockSpec(a.shape, lambda b, i: (0,) * a.ndim)
    return pl.pallas_call(
        _ret_kernel,
        grid=(B, T // RET_TC),
        in_specs=[blk(0), blk(1), blk(2), blk(3), tab, tab, whole(dec), whole(zeta), whole(xi), whole(gch), whole(nw)],
        out_specs=pl.BlockSpec((1, RET_TC, D_RET), lambda b, i: (b, i, 0)),
        out_shape=jax.ShapeDtypeStruct((B, T, D_RET), BF16),
        scratch_shapes=[pltpu.VMEM((npair, PAIR, PAIR), F32)],
        compiler_params=pltpu.CompilerParams(dimension_semantics=("parallel", "arbitrary"),
                                             vmem_limit_bytes=VMEM_LIMIT),
        name="retention",
    )(ret3, ret3, ret3, ret3, cos, sin, dec, zeta, xi, gch, nw)


def _cmp_kernel(x_ref, pos_ref, w1_ref, w2_ref, w2t_ref, knw_ref, o_ref, ot_ref):
    is_key = pl.program_id(1) == 0
    half = CMP_STRIDE * HEAD_DIM
    for g in range(NSA_KV_HEADS):
        x = x_ref[0, g]
        a = jnp.dot((x + pos_ref[0, 0:1, :]).astype(BF16), w1_ref[0, :half, :], preferred_element_type=F32)
        b = jnp.dot((x + pos_ref[0, 1:2, :]).astype(BF16), w1_ref[0, half:, :], preferred_element_type=F32)
        hid = a + pltpu.roll(b, b.shape[0] - 1, 0)
        hid = (hid * _sigmoid(hid)).astype(BF16)
        out = jnp.dot(hid, w2_ref[0], preferred_element_type=F32)
        ms = jnp.mean(out * out, axis=-1, keepdims=True)
        normed = out * lax.rsqrt(ms + EPS) * knw_ref[...]
        o_ref[0, 0, g] = jnp.where(is_key, normed, out)
        ot_ref[0, 0, g] = lax.dot_general(w2t_ref[0], hid, NT_DIMS, preferred_element_type=F32)


def _compress(xc, pos, w1, w2, w2t, knw, B, ncb):
    return pl.pallas_call(
        _cmp_kernel,
        grid=(B, 2),
        in_specs=[
            pl.BlockSpec((1, NSA_KV_HEADS, ncb, CMP_STRIDE * HEAD_DIM), lambda b, s: (b, s, 0, 0)),
            pl.BlockSpec((1, 2, CMP_STRIDE * HEAD_DIM), lambda b, s: (s, 0, 0)),
            pl.BlockSpec((1, CMP_BLOCK * HEAD_DIM, CMP_HIDDEN), lambda b, s: (s, 0, 0)),
            pl.BlockSpec((1, CMP_HIDDEN, HEAD_DIM), lambda b, s: (s, 0, 0)),
            pl.BlockSpec((1, HEAD_DIM, CMP_HIDDEN), lambda b, s: (s, 0, 0)),
            pl.BlockSpec((1, HEAD_DIM), lambda b, s: (0, 0)),
        ],
        out_specs=[
            pl.BlockSpec((1, 1, NSA_KV_HEADS, ncb, HEAD_DIM), lambda b, s: (b, s, 0, 0, 0)),
            pl.BlockSpec((1, 1, NSA_KV_HEADS, HEAD_DIM, ncb), lambda b, s: (b, s, 0, 0, 0)),
        ],
        out_shape=[
            jax.ShapeDtypeStruct((B, 2, NSA_KV_HEADS, ncb, HEAD_DIM), F32),
            jax.ShapeDtypeStruct((B, 2, NSA_KV_HEADS, HEAD_DIM, ncb), F32),
        ],
        compiler_params=pltpu.CompilerParams(dimension_semantics=("parallel", "parallel"),
                                             vmem_limit_bytes=VMEM_LIMIT),
        name="compress",
    )(xc, pos, w1, w2, w2t, knw)


def _nsa_kernel(qt_ref, qtn_ref, glt_ref, ng_ref, ks_ref, kw_ref, vst_ref, vwt_ref, kc_ref, vct_ref, mt_ref, qnw_ref,
                bg_ref, o_ref, qp_ref, oc_ref, sa_ref, sb_ref, sw_ref, m_ref, acc_ref):
    qi = pl.program_id(1)
    t0 = qi * (NSA_QT * NSA_TQ)
    tiles = range(NSA_QT)
    chains = [(qt, g) for qt in tiles for g in range(NSA_KV_HEADS)]
    gq = NSA_GROUP * HEAD_DIM
    gg = 16
    ncb = kc_ref.shape[3]
    n_slc = mt_ref.shape[0]

    def select_scores(src_ref, base_t0, slot):
        qs = []
        for c, (qt, g) in enumerate(chains):
            cols = []
            for r in range(NSA_GROUP):
                q = src_ref[0, qt, g * gq + r * HEAD_DIM:g * gq + (r + 1) * HEAD_DIM, :]
                ms = jnp.mean(q * q, axis=0, keepdims=True)
                cols.append(q * lax.rsqrt(ms + EPS) * qnw_ref[...] * (QK_SCALE * LOG2E))
            qs.append(jnp.concatenate(cols, axis=1).astype(BF16))
            qp_ref[slot, c, 0:HEAD_DIM, :] = qs[c]

        n_idx = lax.broadcasted_iota(jnp.int32, (ncb, NSA_TQ), 0)
        cbias = []
        for qt in tiles:
            tok_c = base_t0 + qt * NSA_TQ + lax.broadcasted_iota(jnp.int32, (ncb, NSA_TQ), 1)
            cbias.append(_tile4(jnp.where((n_idx * CMP_STRIDE + (CMP_BLOCK - 1)) <= tok_c, 0.0, NEG)))
        return [jnp.dot(kc_ref[0, 0, g].astype(BF16), qs[c], preferred_element_type=F32) + cbias[qt]
                for c, (qt, g) in enumerate(chains)]

    def select_probs(scs, base_t0, slot):
        has_block = []
        for qt in tiles:
            tok_row = base_t0 + qt * NSA_TQ + lax.broadcasted_iota(jnp.int32, (1, NSA_TQ), 1)
            has_block.append(_tile4(jnp.where(tok_row >= CMP_BLOCK - 1, 1.0, 0.0)))
        p = []
        for c, (qt, g) in enumerate(chains):
            sc = scs[c]
            mc = jnp.max(sc, axis=0, keepdims=True)
            ec = jnp.exp2(sc - mc)
            lc = jnp.sum(ec, axis=0, keepdims=True)
            p.append(ec * (has_block[qt] / lc))
            oc_ref[slot, c] = jnp.dot(vct_ref[0, 0, g].astype(BF16), p[c].astype(BF16),
                                      preferred_element_type=F32)

        jb = lax.broadcasted_iota(jnp.int32, (n_slc, NSA_TQ), 0)
        valid_s, force = [], []
        for qt in tiles:
            tok_s = base_t0 + qt * NSA_TQ + lax.broadcasted_iota(jnp.int32, (n_slc, NSA_TQ), 1)
            valid_s.append(jb * SLC_BLOCK <= tok_s)
            force.append((jb == (tok_s >> SLC_SHIFT)) | (jb == 0))
        scores = []
        for c, (qt, g) in enumerate(chains):
            ps = p[c][:, 0:NSA_TQ]
            for r in range(1, NSA_GROUP):
                ps = ps + p[c][:, r * NSA_TQ:(r + 1) * NSA_TQ]
            ps_hi = ps.astype(BF16)
            ps_lo = (ps - ps_hi.astype(F32)).astype(BF16)
            imp = (jnp.dot(mt_ref[...], ps_hi, preferred_element_type=F32)
                   + jnp.dot(mt_ref[...], ps_lo, preferred_element_type=F32))
            scores.append(jnp.where(valid_s[qt], jnp.where(force[qt], imp + FORCE_BONUS, imp), NEG))
        return scores, valid_s

    def select_rank(scores, valid_s, slot):
        sub = lax.broadcasted_iota(jnp.int32, (SUBLANES, NSA_TQ), 0)
        for c, (qt, g) in enumerate(chains):
            score = scores[c]
            blocks = [score[v * SUBLANES:(v + 1) * SUBLANES, :] for v in range(n_slc // SUBLANES)]
            ranks = [jnp.zeros((SUBLANES, NSA_TQ), F32) for _ in blocks]
            for i in range(n_slc):
                row = score[i:i + 1, :]
                for v, blk in enumerate(blocks):
                    if v * SUBLANES > i:
                        beats = row >= blk
                    elif (v + 1) * SUBLANES <= i:
                        beats = row > blk
                    else:
                        beats = (row > blk) | ((row >= blk) & (sub > i - v * SUBLANES))
                    ranks[v] = ranks[v] + jnp.where(beats, 1.0, 0.0)
            rank = jnp.concatenate(ranks, axis=0)
            sel = (rank < float(SLC_TOPK)) & valid_s[qt]
            qp_ref[slot, c, HEAD_DIM:2 * HEAD_DIM, :] = _tile4(jnp.where(sel, 0.0, NEG).astype(BF16))

    @pl.when(qi == 0)
    def _():
        scores0, valid0 = select_probs(select_scores(qt_ref, 0, 0), 0, 0)
        select_rank(scores0, valid0, 0)

    cur = qi % 2
    nxt = 1 - cur
    t_next = t0 + NSA_QT * NSA_TQ

    kt0, ks0 = [], []
    for qt in tiles:
        kt0.append(jnp.maximum(qi * NSA_QT + qt - WIN_SIZE // NSA_TQ, 0))
        ks0.append(pl.multiple_of(kt0[qt] * NSA_TQ, NSA_TQ))
    for c, (qt, g) in enumerate(chains):
        sw_ref[c] = jnp.dot(kw_ref[0, g, pl.ds(ks0[qt], WIN_KEYS), 0:HEAD_DIM], qp_ref[cur, c, 0:HEAD_DIM, :],
                            preferred_element_type=F32)

    cmp_scores = select_scores(qtn_ref, t_next, nxt)

    vt_per_tile = SLC_TK // LANES

    def slc_scores(j, dst_ref):
        kst = pl.multiple_of(j * SLC_TK, SLC_TK)
        for c, (qt, g) in enumerate(chains):
            dst_ref[c] = jnp.dot(ks_ref[0, g, pl.ds(kst, SLC_TK), :], qp_ref[cur, c], preferred_element_type=F32)

    def slc_update(j, src_ref, causal):
        if causal:
            r_s = lax.broadcasted_iota(jnp.int32, (SLC_TK, NSA_TQ), 0)
            c_s = lax.broadcasted_iota(jnp.int32, (SLC_TK, NSA_TQ), 1)
            causal_bias = [_tile4(jnp.where(j * SLC_TK + r_s <= t0 + qt * NSA_TQ + c_s, 0.0, NEG)) for qt in tiles]
        for c, (qt, g) in enumerate(chains):
            s = src_ref[c]
            if causal:
                s = s + causal_bias[qt]
            m_old = m_ref[c]
            m_new = jnp.maximum(m_old, jnp.max(s, axis=0, keepdims=True))
            alpha = jnp.exp2(m_old - m_new)
            eb = jnp.exp2(s - m_new).astype(BF16)
            vt = jnp.concatenate([vst_ref[0, j * vt_per_tile + jj, g * V_ROWS:(g + 1) * V_ROWS, :]
                                  for jj in range(vt_per_tile)], axis=1)
            acc_ref[c] = alpha * acc_ref[c] + jnp.dot(vt, eb, preferred_element_type=F32)
            m_ref[c] = m_new

    slc_scores(0, sa_ref)
    sel_scores, sel_valid = select_probs(cmp_scores, t_next, nxt)

    r_w = lax.broadcasted_iota(jnp.int32, (WIN_KEYS, NSA_TQ), 0)
    c_w = lax.broadcasted_iota(jnp.int32, (WIN_KEYS, NSA_TQ), 1)
    wbias = []
    for qt in tiles:
        delta = (t0 + qt * NSA_TQ - ks0[qt]) + c_w - r_w
        wbias.append(_tile4(jnp.where((delta >= 0) & (delta < WIN_SIZE), 0.0, NEG)))
    ow_t = []
    for c, (qt, g) in enumerate(chains):
        sw = sw_ref[c] + wbias[qt]
        mw = jnp.max(sw, axis=0, keepdims=True)
        ewb = jnp.exp2(sw - mw).astype(BF16)
        vwt = jnp.concatenate([vwt_ref[0, kt0[qt] + j, g * V_ROWS:(g + 1) * V_ROWS, :]
                               for j in range(WIN_KEYS // LANES)], axis=1)
        ow_aug = jnp.dot(vwt, ewb, preferred_element_type=F32)
        ow_t.append(ow_aug[0:HEAD_DIM, :] * (1.0 / ow_aug[HEAD_DIM:HEAD_DIM + 1, :]))

    select_rank(sel_scores, sel_valid, nxt)

    m_ref[...] = jnp.full(m_ref.shape, NEG, F32)
    acc_ref[...] = jnp.zeros(acc_ref.shape, F32)
    n_full = t0 // SLC_TK

    def pair(jj, carry):
        j = 2 * jj
        slc_scores(j + 1, sb_ref)
        slc_update(j, sa_ref, False)
        slc_scores(j + 2, sa_ref)
        slc_update(j + 1, sb_ref, False)
        return carry

    lax.fori_loop(0, n_full // 2, pair, 0)

    @pl.when(n_full % 2 == 1)
    def _():
        slc_scores(n_full, sb_ref)
        slc_update(n_full - 1, sa_ref, False)
        slc_update(n_full, sb_ref, True)

    @pl.when(n_full % 2 == 0)
    def _():
        slc_update(n_full, sa_ref, True)

    for qt in tiles:
        gates = _sigmoid(glt_ref[0, qt] + bg_ref[...])
        outs = []
        for g in range(NSA_KV_HEADS):
            c = qt * NSA_KV_HEADS + g
            os_t = acc_ref[c, 0:HEAD_DIM, :] * (1.0 / acc_ref[c, HEAD_DIM:HEAD_DIM + 1, :])
            oc_t = oc_ref[cur, c]
            for r in range(NSA_GROUP):
                sl = slice(r * NSA_TQ, (r + 1) * NSA_TQ)
                row = g * gg + r
                outs.append(gates[row:row + 1, :] * oc_t[:, sl]
                            + gates[row + NSA_GROUP:row + NSA_GROUP + 1, :] * os_t[:, sl]
                            + gates[row + 2 * NSA_GROUP:row + 2 * NSA_GROUP + 1, :] * ow_t[c][:, sl])
        o_tok = jnp.concatenate(outs, axis=0).T
        rows = pl.ds(qt * NSA_TQ, NSA_TQ)
        ng = ng_ref[0, rows, :]
        o_ref[0, rows, :] = (o_tok * (ng * _sigmoid(ng))).astype(BF16)


def _nsa(nqt, glt, ng3, ks, kw, vt, kc, vct, mt, qnw, bg, B, T):
    nt = T // LANES
    ncb = kc.shape[3]
    G = NSA_KV_HEADS
    NC = NSA_QT * G
    ncols = NSA_GROUP * NSA_TQ
    steps = T // (NSA_QT * NSA_TQ)
    return pl.pallas_call(
        _nsa_kernel,
        grid=(B, steps),
        in_specs=[
            pl.BlockSpec((1, NSA_QT, D_NSA, LANES), lambda b, i: (b, i, 0, 0)),
            pl.BlockSpec((1, NSA_QT, D_NSA, LANES), lambda b, i: (b, jnp.minimum(i + 1, steps - 1), 0, 0)),
            pl.BlockSpec((1, NSA_QT, 16 * G, LANES), lambda b, i: (b, i, 0, 0)),
            pl.BlockSpec((1, NSA_QT * NSA_TQ, D_NSA), lambda b, i: (b, i, 0)),
            pl.BlockSpec((1, G, T, PAIR), lambda b, i: (b, 0, 0, 0)),
            pl.BlockSpec((1, G, T, PAIR), lambda b, i: (b, 0, 0, 0)),
            pl.BlockSpec((1, nt, G * V_ROWS, LANES), lambda b, i: (b, 0, 0, 0)),
            pl.BlockSpec((1, nt, G * V_ROWS, LANES), lambda b, i: (b, 0, 1, 0)),
            pl.BlockSpec((1, 1, G, ncb, HEAD_DIM), lambda b, i: (b, 0, 0, 0, 0)),
            pl.BlockSpec((1, 1, G, HEAD_DIM, ncb), lambda b, i: (b, 1, 0, 0, 0)),
            pl.BlockSpec(mt.shape, lambda b, i: (0, 0)),
            pl.BlockSpec((HEAD_DIM, 1), lambda b, i: (0, 0)),
            pl.BlockSpec((16 * G, 1), lambda b, i: (0, 0)),
        ],
        out_specs=pl.BlockSpec((1, NSA_QT * NSA_TQ, D_NSA), lambda b, i: (b, i, 0)),
        out_shape=jax.ShapeDtypeStruct((B, T, D_NSA), BF16),
        scratch_shapes=[
            pltpu.VMEM((2, NC, 2 * HEAD_DIM, ncols), BF16),
            pltpu.VMEM((2, NC, HEAD_DIM, ncols), F32),
            pltpu.VMEM((NC, SLC_TK, ncols), F32),
            pltpu.VMEM((NC, SLC_TK, ncols), F32),
            pltpu.VMEM((NC, WIN_KEYS, ncols), F32),
            pltpu.VMEM((NC, 1, ncols), F32),
            pltpu.VMEM((NC, V_ROWS, ncols), F32),
        ],
        compiler_params=pltpu.CompilerParams(dimension_semantics=("parallel", "arbitrary"),
                                             vmem_limit_bytes=VMEM_LIMIT),
        name="nsa",
    )(nqt, nqt, glt, ng3, ks, kw, vt, vt, kc, vct, mt, qnw, bg)


def _out_kernel(x_ref, yr_ref, yn_ref, wr_ref, wn_ref, o_ref):
    o_ref[...] = (x_ref[...]
                  + jnp.dot(yr_ref[...], wr_ref[...], preferred_element_type=F32)
                  + jnp.dot(yn_ref[...], wn_ref[...], preferred_element_type=F32))


def _outproj(x2, yr, yn, wr, wn):
    N = x2.shape[0]
    row = lambda i: (i, 0)
    const = lambda i: (0, 0)
    return pl.pallas_call(
        _out_kernel,
        grid=(N // OUT_TM,),
        in_specs=[pl.BlockSpec((OUT_TM, D_MODEL), row), pl.BlockSpec((OUT_TM, D_RET), row),
                  pl.BlockSpec((OUT_TM, D_NSA), row), pl.BlockSpec(wr.shape, const), pl.BlockSpec(wn.shape, const)],
        out_specs=pl.BlockSpec((OUT_TM, D_MODEL), row),
        out_shape=jax.ShapeDtypeStruct((N, D_MODEL), F32),
        compiler_params=pltpu.CompilerParams(dimension_semantics=("parallel",), vmem_limit_bytes=VMEM_LIMIT),
        name="outproj",
    )(x2, yr, yn, wr, wn)


@functools.lru_cache(maxsize=None)
def _tables(T):
    half = HEAD_DIM // 2
    inv = ROPE_THETA ** (-np.arange(half, dtype=np.float64) / half)
    ang = np.arange(T, dtype=np.float64)[:, None] * inv[None, :]
    cos = np.concatenate([np.cos(ang)] * 4, axis=1).astype(np.float32)
    sin = np.concatenate([-np.sin(ang), -np.sin(ang), np.sin(ang), np.sin(ang)], axis=1).astype(np.float32)

    C = RET_CHUNK
    log_g = np.log1p(-np.exp2(-5.0 - np.arange(RET_HEADS, dtype=np.float64)))
    pos = np.arange(C, dtype=np.float64)
    diff = pos[:, None] - pos[None, :]
    decay = np.where(diff >= 0, np.exp(log_g[:, None, None] * np.maximum(diff, 0.0)), 0.0)
    zeta = np.exp(log_g[:, None] * (C - 1.0 - pos))
    xi = np.exp(log_g[:, None] * (pos + 1.0))
    g_chunk = np.exp(log_g * C)
    npair = RET_HEADS // 2

    def pair_lanes(a):
        return np.repeat(a.reshape(npair, 2, C).transpose(0, 2, 1), HEAD_DIM, axis=2).astype(np.float32)

    dec = decay.reshape(npair, 2, C, C).astype(np.float32)
    gch = np.repeat(g_chunk.reshape(npair, 1, 2), HEAD_DIM, axis=2).astype(np.float32)

    n_cmp = (T - CMP_BLOCK) // CMP_STRIDE + 1
    ncb = T // CMP_STRIDE
    p = np.arange(n_cmp)[:, None] * CMP_STRIDE + np.arange(CMP_BLOCK)[None, :]
    blk = p // SLC_BLOCK
    M = (blk[:, :, None] == np.arange(T // SLC_BLOCK)[None, None, :]).mean(axis=1)
    mt = np.zeros((T // SLC_BLOCK, ncb), np.float32)
    mt[:, :n_cmp] = M.T
    return cos, sin, dec, pair_lanes(zeta), pair_lanes(xi), gch, mt


def kernel(x, norm_w, w_in, ret_norm_w, q_norm_w, k_norm_cmp, k_norm_slc, k_norm_win, cmp_pos_k, cmp_w1_k, cmp_w2_k,
           cmp_pos_v, cmp_w1_v, cmp_w2_v, b_gate, w_out):
    B, T, D = x.shape
    depth = norm_w.shape[0]
    cos, sin, dec, zeta, xi, gch, mt = _tables(T)
    ncb = T // CMP_STRIDE
    half = CMP_STRIDE * HEAD_DIM
    gate_src = np.zeros((NSA_KV_HEADS, 16), np.int32)
    gate_ok = np.zeros((NSA_KV_HEADS, 16), bool)
    for g in range(NSA_KV_HEADS):
        for br in range(N_BRANCH):
            for r in range(NSA_GROUP):
                gate_src[g, br * NSA_GROUP + r] = br * NSA_HEADS + g * NSA_GROUP + r
                gate_ok[g, br * NSA_GROUP + r] = True
    gate_src = gate_src.reshape(-1)
    gate_ok = gate_ok.reshape(-1)

    x2 = x.reshape(B * T, D)
    for layer in range(depth):
        w = w_in[layer].astype(BF16)
        o_ng = 4 * D_RET + D_NSA
        o_kv = o_ng + D_NSA
        quarter = HEAD_DIM // 2
        pair_perm = np.concatenate([np.arange(quarter), HEAD_DIM + np.arange(quarter),
                                    quarter + np.arange(quarter), HEAD_DIM + quarter + np.arange(quarter)])
        qk_perm = np.concatenate([p * PAIR + pair_perm for p in range(RET_HEADS // 2)])
        w_ret = jnp.concatenate([w[:, :D_RET][:, qk_perm], w[:, D_RET:2 * D_RET][:, qk_perm],
                                 w[:, 2 * D_RET:4 * D_RET]], axis=1)
        wt_q = w[:, 4 * D_RET:o_ng].T
        w_ng = w[:, o_ng:o_kv]
        w_ckv = w[:, o_kv:o_kv + 2 * D_KV]
        w_skwk = jnp.concatenate([w[:, o_kv + 2 * D_KV:o_kv + 3 * D_KV], w[:, o_kv + 4 * D_KV:o_kv + 5 * D_KV]], axis=1)
        wt_v = jnp.concatenate([w[:, o_kv + 3 * D_KV:o_kv + 4 * D_KV], w[:, o_kv + 5 * D_KV:o_kv + 6 * D_KV]], axis=1).T
        w_gl = w[:, o_kv + 6 * D_KV:]
        wt_g = jnp.where(gate_ok[:, None], w_gl.T[gate_src], jnp.zeros((), BF16))
        bg = jnp.where(gate_ok, b_gate[layer][gate_src], 0.0).reshape(-1, 1)

        knw = jnp.stack([jnp.tile(k_norm_slc[layer], 2), jnp.tile(k_norm_win[layer], 2)])
        ret, ng, xc, ks, kw, nqt, vt, glt = _proj(x2, norm_w[layer].reshape(1, D), w_ret, w_ng, w_ckv, w_skwk,
                                                  wt_q, wt_v, wt_g, knw, B, T)

        nw_pair = ret_norm_w[layer].reshape(RET_HEADS // 2, 1, PAIR)
        y_ret = _retention(ret.reshape(B, T, 4 * D_RET), jnp.asarray(cos), jnp.asarray(sin), jnp.asarray(dec),
                           jnp.asarray(zeta), jnp.asarray(xi), jnp.asarray(gch), nw_pair, B, T)

        pos =jnp.stack([cmp_pos_k[layer], cmp_pos_v[layer]]).reshape(2, 2, half)
        w1 = jnp.stack([cmp_w1_k[layer], cmp_w1_v[layer]]).astype(BF16)
        w2 = jnp.stack([cmp_w2_k[layer], cmp_w2_v[layer]]).astype(BF16)
        w2t = jnp.swapaxes(w2, 1, 2)
        kc, vct = _compress(xc, pos, w1, w2, w2t, k_norm_cmp[layer].reshape(1, HEAD_DIM), B, ncb)

        y_nsa = _nsa(nqt, glt, ng.reshape(B, T, D_NSA), ks, kw, vt, kc, vct, jnp.asarray(mt).astype(BF16),
                     q_norm_w[layer].reshape(HEAD_DIM, 1), bg, B, T)

        wo = w_out[layer].astype(BF16)
        x2 = _outproj(x2, y_ret.reshape(B * T, D_RET), y_nsa.reshape(B * T, D_NSA), wo[:D_RET], wo[D_RET:])
    return x2.reshape(B, T, D)
```

```python
import functools

import numpy as np
import jax
import jax.numpy as jnp
from jax import lax
from jax.experimental import pallas as pl
from jax.experimental.pallas import tpu as pltpu

F32 = jnp.float32
BF16 = jnp.bfloat16

D_MODEL = 1024
HEAD_DIM = 64
HEAD_SHIFT = 6
RET_HEADS = 8
NSA_HEADS = 8
NSA_KV_HEADS = 2
NSA_GROUP = NSA_HEADS // NSA_KV_HEADS
D_RET = RET_HEADS * HEAD_DIM
D_NSA = NSA_HEADS * HEAD_DIM
D_KV = NSA_KV_HEADS * HEAD_DIM
N_BRANCH = 3
RET_CHUNK = 128
ROPE_THETA = 10000.0
CMP_BLOCK = 32
CMP_STRIDE = 16
CMP_HIDDEN = 256
SLC_BLOCK = 64
SLC_SHIFT = 6
SLC_TOPK = 16
WIN_SIZE = 512
EPS = 1e-6
NEG = -1e30
FORCE_BONUS = 1e4
QK_SCALE = HEAD_DIM ** -0.5
LOG2E = 1.4426950408889634
V_ROWS = HEAD_DIM + 16

LANES = 128
SUBLANES = 8
PAIR = 2 * HEAD_DIM
VMEM_LIMIT = 48 * 1024 * 1024

PROJ_TM = 512
RET_TC = 512
NSA_TQ = 128
NSA_QT = 2
SLC_TK = 512
WIN_KEYS = WIN_SIZE + NSA_TQ
OUT_TM = 512

NT_DIMS = (((1,), (1,)), ((), ()))
TN_DIMS = (((0,), (0,)), ((), ()))


def _sigmoid(x):
    return 1.0 / (1.0 + jnp.exp(-x))


def _tile4(a):
    return jnp.concatenate([a, a, a, a], axis=1)


def _proj_kernel(steps_per_batch, x_ref, nw_ref, w_ret_ref, w_ng_ref, w_ckv_ref, w_skwk_ref, wt_q_ref, wt_v_ref,
                 wt_g_ref, knw_ref, ret_ref, ng_ref, xc_ref, ks_ref, kw_ref, nqt_ref, vt_ref, glt_ref, ckv_scr):
    x = x_ref[...]
    ms = jnp.mean(x * x, axis=-1, keepdims=True)
    h = (x * lax.rsqrt(ms + EPS) * nw_ref[...]).astype(BF16)
    ret_ref[...] = jnp.dot(h, w_ret_ref[...], preferred_element_type=F32)
    ng_ref[...] = jnp.dot(h, w_ng_ref[...], preferred_element_type=F32)

    ckv = jnp.dot(h, w_ckv_ref[...], preferred_element_type=F32)
    for half in range(2 * D_KV // LANES):
        ckv_scr[half] = ckv[:, half * LANES:(half + 1) * LANES]
    for l in range(CMP_STRIDE):
        for half in range(2 * D_KV // LANES):
            rows = ckv_scr[half, pl.ds(l, PROJ_TM // CMP_STRIDE, stride=CMP_STRIDE), :]
            for s in range(LANES // HEAD_DIM):
                xc_ref[0, half * (LANES // HEAD_DIM) + s, :, l * HEAD_DIM:(l + 1) * HEAD_DIM] = (
                    rows[:, s * HEAD_DIM:(s + 1) * HEAD_DIM])

    skwk = jnp.dot(h, w_skwk_ref[...], preferred_element_type=F32)
    lane = lax.broadcasted_iota(jnp.int32, (1, PAIR), 1)
    head0 = lane < HEAD_DIM
    m0 = jnp.where(head0, 1.0, 0.0)
    m1 = 1.0 - m0
    t_start = (pl.program_id(0) % steps_per_batch) * PROJ_TM
    tok = t_start + lax.broadcasted_iota(jnp.int32, (PROJ_TM, PAIR), 0)
    col = lax.broadcasted_iota(jnp.int32, (PROJ_TM, PAIR), 1)
    indicator = jnp.where((tok >> SLC_SHIFT) == col - HEAD_DIM, 1.0, 0.0)

    def pair_normed(t, w):
        t2 = t * t
        ms0 = jnp.sum(t2 * m0, axis=-1, keepdims=True)
        ms1 = jnp.sum(t2 * m1, axis=-1, keepdims=True)
        return t * lax.rsqrt(jnp.where(head0, ms0, ms1) * (1.0 / HEAD_DIM) + EPS) * w

    ns = pair_normed(skwk[:, :PAIR], knw_ref[0:1, :])
    nwin = pair_normed(skwk[:, PAIR:], knw_ref[1:2, :])
    for g in range(NSA_KV_HEADS):
        s_g = ns if g == 0 else pltpu.roll(ns, HEAD_DIM, 1)
        w_g = nwin if g == 0 else pltpu.roll(nwin, HEAD_DIM, 1)
        ks_ref[0, g] = jnp.where(head0, s_g, indicator).astype(BF16)
        kw_ref[0, g] = jnp.where(head0, w_g, 0.0).astype(BF16)

    qt =lax.dot_general(wt_q_ref[...], h, NT_DIMS, preferred_element_type=F32)
    vt = lax.dot_general(wt_v_ref[...], h, NT_DIMS, preferred_element_type=F32)
    gt = lax.dot_general(wt_g_ref[...], h, NT_DIMS, preferred_element_type=F32)
    pad_row = lax.broadcasted_iota(jnp.int32, (V_ROWS - HEAD_DIM, LANES), 0)
    ones_pad = jnp.where(pad_row == 0, 1.0, 0.0).astype(BF16)
    for j in range(PROJ_TM // LANES):
        sl = slice(j * LANES, (j + 1) * LANES)
        nqt_ref[0, j] = qt[:, sl]
        for blk in range(2 * NSA_KV_HEADS):
            vt_ref[0, j, blk * V_ROWS:blk * V_ROWS + HEAD_DIM, :] = (
                vt[blk * HEAD_DIM:(blk + 1) * HEAD_DIM, sl].astype(BF16))
            vt_ref[0, j, blk * V_ROWS + HEAD_DIM:(blk + 1) * V_ROWS, :] = ones_pad
        glt_ref[0, j] = gt[:, sl]


def _proj(x2, nw, w_ret, w_ng, w_ckv, w_skwk, wt_q, wt_v, wt_g, knw, B, T):
    N = B * T
    tpb = T // PROJ_TM
    sub = PROJ_TM // LANES
    nt = T // LANES
    const = lambda i: (0, 0)
    row = lambda i: (i, 0)
    trn = lambda i: (i // tpb, i % tpb, 0, 0)
    tokm = lambda i: (i // tpb, 0, i % tpb, 0)
    return pl.pallas_call(
        functools.partial(_proj_kernel, tpb),
        grid=(N // PROJ_TM,),
        in_specs=[
            pl.BlockSpec((PROJ_TM, D_MODEL), row),
            pl.BlockSpec((1, D_MODEL), const),
            pl.BlockSpec(w_ret.shape, const),
            pl.BlockSpec(w_ng.shape, const),
            pl.BlockSpec(w_ckv.shape, const),
            pl.BlockSpec(w_skwk.shape, const),
            pl.BlockSpec(wt_q.shape, const),
            pl.BlockSpec(wt_v.shape, const),
            pl.BlockSpec(wt_g.shape, const),
            pl.BlockSpec(knw.shape, const),
        ],
        out_specs=[
            pl.BlockSpec((PROJ_TM, 4 * D_RET), row),
            pl.BlockSpec((PROJ_TM, D_NSA), row),
            pl.BlockSpec((1, 2 * NSA_KV_HEADS, PROJ_TM // CMP_STRIDE, CMP_STRIDE * HEAD_DIM), tokm),
            pl.BlockSpec((1, NSA_KV_HEADS, PROJ_TM, PAIR), tokm),
            pl.BlockSpec((1, NSA_KV_HEADS, PROJ_TM, PAIR), tokm),
            pl.BlockSpec((1, sub, D_NSA, LANES), trn),
            pl.BlockSpec((1, sub, 2 * NSA_KV_HEADS * V_ROWS, LANES), trn),
            pl.BlockSpec((1, sub, 32, LANES), trn),
        ],
        out_shape=[
            jax.ShapeDtypeStruct((N, 4 * D_RET), F32),
            jax.ShapeDtypeStruct((N, D_NSA), F32),
            jax.ShapeDtypeStruct((B, 2 * NSA_KV_HEADS, T // CMP_STRIDE, CMP_STRIDE * HEAD_DIM), F32),
            jax.ShapeDtypeStruct((B, NSA_KV_HEADS, T, PAIR), BF16),
            jax.ShapeDtypeStruct((B, NSA_KV_HEADS, T, PAIR), BF16),
            jax.ShapeDtypeStruct((B, nt, D_NSA, LANES), F32),
            jax.ShapeDtypeStruct((B, nt, 2 * NSA_KV_HEADS * V_ROWS, LANES), BF16),
            jax.ShapeDtypeStruct((B, nt, 32, LANES), F32),
        ],
        scratch_shapes=[pltpu.VMEM((2 * D_KV // LANES, PROJ_TM, LANES), F32)],
        compiler_params=pltpu.CompilerParams(dimension_semantics=("parallel",), vmem_limit_bytes=VMEM_LIMIT),
        name="proj",
    )(x2, nw, w_ret, w_ng, w_ckv, w_skwk, wt_q, wt_v, wt_g, knw)


def _ret_kernel(q_ref, k_ref, v_ref, g_ref, cos_ref, sin_ref, dec_ref, zeta_ref, xi_ref, gch_ref, nw_ref,
                o_ref, state_ref):
    @pl.when(pl.program_id(1) == 0)
    def _():
        state_ref[...] = jnp.zeros_like(state_ref)

    lane = lax.broadcasted_iota(jnp.int32, (1, PAIR), 1)
    q_head = (lane >> (HEAD_SHIFT - 1)) & 1
    v_head = lane >> HEAD_SHIFT
    q_mask = [jnp.where(q_head == h, 1.0, 0.0).astype(BF16) for h in (0, 1)]
    v_mask = [jnp.where(v_head == h, 1.0, 0.0).astype(BF16) for h in (0, 1)]
    row_qh = (lax.broadcasted_iota(jnp.int32, (PAIR, PAIR), 0) >> (HEAD_SHIFT - 1)) & 1
    row_vh = lax.broadcasted_iota(jnp.int32, (PAIR, PAIR), 0) >> HEAD_SHIFT
    col_vh = lax.broadcasted_iota(jnp.int32, (PAIR, PAIR), 1) >> HEAD_SHIFT
    same_head_kv = jnp.where(row_qh == col_vh, 1.0, 0.0)
    head_mean = jnp.where(row_vh == col_vh, 1.0 / HEAD_DIM, 0.0).astype(BF16)

    pairs = range(RET_HEADS // 2)
    mean2 = jnp.concatenate([head_mean, head_mean], axis=0)
    for c in range(RET_TC // RET_CHUNK):
        sl = pl.ds(c * RET_CHUNK, RET_CHUNK)
        cos = cos_ref[sl, :]
        sin = sin_ref[sl, :]
        cols = [slice(p * PAIR, (p + 1) * PAIR) for p in pairs]
        qb, kb, vb, vzb = [], [], [], []
        for p in pairs:
            q = q_ref[0, sl, cols[p]]
            k = k_ref[0, sl, cols[p]]
            v = v_ref[0, sl, cols[p]]
            qb.append((q * cos + pltpu.roll(q, HEAD_DIM, 1) * sin).astype(BF16))
            kb.append(((k * cos + pltpu.roll(k, HEAD_DIM, 1) * sin) * QK_SCALE).astype(BF16))
            vb.append(v.astype(BF16))
            vzb.append((v * zeta_ref[p]).astype(BF16))
        states = [state_ref[p] for p in pairs]
        s = [[lax.dot_general(qb[p] * q_mask[h], kb[p], NT_DIMS, preferred_element_type=F32) for h in (0, 1)]
             for p in pairs]
        o_cross = [jnp.dot(qb[p], states[p].astype(BF16), preferred_element_type=F32) for p in pairs]
        kv = [lax.dot_general(kb[p], vzb[p], TN_DIMS, preferred_element_type=F32) for p in pairs]
        sb = [jnp.concatenate([(s[p][h] * dec_ref[p, h]).astype(BF16) for h in (0, 1)], axis=1) for p in pairs]
        vv = [jnp.concatenate([vb[p] * v_mask[h] for h in (0, 1)], axis=0) for p in pairs]
        o = [jnp.dot(sb[p], vv[p], preferred_element_type=F32) + o_cross[p] * xi_ref[p] for p in pairs]
        for p in pairs:
            state_ref[p] = states[p] * gch_ref[p] + kv[p] * same_head_kv
        o2 = [o[p] * o[p] for p in pairs]
        o2_hi = [o2[p].astype(BF16) for p in pairs]
        o2_hl = [jnp.concatenate([o2_hi[p], (o2[p] - o2_hi[p].astype(F32)).astype(BF16)], axis=1) for p in pairs]
        ms = [jnp.dot(o2_hl[p], mean2, preferred_element_type=F32) for p in pairs]
        for p in pairs:
            g = g_ref[0, sl, cols[p]]
            y = o[p] * lax.rsqrt(ms[p] + EPS) * nw_ref[p]
            o_ref[0, sl, cols[p]] = (y * (g * _sigmoid(g))).astype(BF16)


def _retention(ret3, cos, sin, dec, zeta, xi, gch, nw, B, T):
    npair = RET_HEADS // 2
    blk = lambda col: pl.BlockSpec((1, RET_TC, D_RET), lambda b, i: (b, i, col))
    tab = pl.BlockSpec((RET_TC, PAIR), lambda b, i: (i, 0))
    whole = lambda a: pl.BlockSpec(a.shape, lambda b, i: (0,) * a.ndim)
    return pl.pallas_call(
        _ret_kernel,
        grid=(B, T // RET_TC),
        in_specs=[blk(0), blk(1), blk(2), blk(3), tab, tab, whole(dec), whole(zeta), whole(xi), whole(gch), whole(nw)],
        out_specs=pl.BlockSpec((1, RET_TC, D_RET), lambda b, i: (b, i, 0)),
        out_shape=jax.ShapeDtypeStruct((B, T, D_RET), BF16),
        scratch_shapes=[pltpu.VMEM((npair, PAIR, PAIR), F32)],
        compiler_params=pltpu.CompilerParams(dimension_semantics=("parallel", "arbitrary"),
                                             vmem_limit_bytes=VMEM_LIMIT),
        name="retention",
    )(ret3, ret3, ret3, ret3, cos, sin, dec, zeta, xi, gch, nw)


def _cmp_kernel(x_ref, pos_ref, w1_ref, w2_ref, w2t_ref, knw_ref, o_ref, ot_ref):
    is_key = pl.program_id(1) == 0
    half = CMP_STRIDE * HEAD_DIM
    for g in range(NSA_KV_HEADS):
        x = x_ref[0, g]
        a = jnp.dot((x + pos_ref[0, 0:1, :]).astype(BF16), w1_ref[0, :half, :], preferred_element_type=F32)
        b = jnp.dot((x + pos_ref[0, 1:2, :]).astype(BF16), w1_ref[0, half:, :], preferred_element_type=F32)
        hid = a + pltpu.roll(b, b.shape[0] - 1, 0)
        hid = (hid * _sigmoid(hid)).astype(BF16)
        out = jnp.dot(hid, w2_ref[0], preferred_element_type=F32)
        ms = jnp.mean(out * out, axis=-1, keepdims=True)
        normed = out * lax.rsqrt(ms + EPS) * knw_ref[...]
        o_ref[0, 0, g] = jnp.where(is_key, normed, out)
        ot_ref[0, 0, g] = lax.dot_general(w2t_ref[0], hid, NT_DIMS, preferred_element_type=F32)


def _compress(xc, pos, w1, w2, w2t, knw, B, ncb):
    return pl.pallas_call(
        _cmp_kernel,
        grid=(B, 2),
        in_specs=[
            pl.BlockSpec((1, NSA_KV_HEADS, ncb, CMP_STRIDE * HEAD_DIM), lambda b, s: (b, s, 0, 0)),
            pl.BlockSpec((1, 2, CMP_STRIDE * HEAD_DIM), lambda b, s: (s, 0, 0)),
            pl.BlockSpec((1, CMP_BLOCK * HEAD_DIM, CMP_HIDDEN), lambda b, s: (s, 0, 0)),
            pl.BlockSpec((1, CMP_HIDDEN, HEAD_DIM), lambda b, s: (s, 0, 0)),
            pl.BlockSpec((1, HEAD_DIM, CMP_HIDDEN), lambda b, s: (s, 0, 0)),
            pl.BlockSpec((1, HEAD_DIM), lambda b, s: (0, 0)),
        ],
        out_specs=[
            pl.BlockSpec((1, 1, NSA_KV_HEADS, ncb, HEAD_DIM), lambda b, s: (b, s, 0, 0, 0)),
            pl.BlockSpec((1, 1, NSA_KV_HEADS, HEAD_DIM, ncb), lambda b, s: (b, s, 0, 0, 0)),
        ],
        out_shape=[
            jax.ShapeDtypeStruct((B, 2, NSA_KV_HEADS, ncb, HEAD_DIM), F32),
            jax.ShapeDtypeStruct((B, 2, NSA_KV_HEADS, HEAD_DIM, ncb), F32),
        ],
        compiler_params=pltpu.CompilerParams(dimension_semantics=("parallel", "parallel"),
                                             vmem_limit_bytes=VMEM_LIMIT),
        name="compress",
    )(xc, pos, w1, w2, w2t, knw)


def _nsa_kernel(qt_ref, qtn_ref, glt_ref, ng_ref, ks_ref, kw_ref, vst_ref, vwt_ref, kc_ref, vct_ref, mt_ref, qnw_ref,
                bg_ref, o_ref, qp_ref, oc_ref, sa_ref, sb_ref, sw_ref, m_ref, acc_ref):
    qi = pl.program_id(1)
    t0 = qi * (NSA_QT * NSA_TQ)
    tiles = range(NSA_QT)
    chains = [(qt, g) for qt in tiles for g in range(NSA_KV_HEADS)]
    gq = NSA_GROUP * HEAD_DIM
    gg = 16
    ncb = kc_ref.shape[3]
    n_slc = mt_ref.shape[0]

    def select_scores(src_ref, base_t0, slot):
        qs = []
        for c, (qt, g) in enumerate(chains):
            cols = []
            for r in range(NSA_GROUP):
                q = src_ref[0, qt, g * gq + r * HEAD_DIM:g * gq + (r + 1) * HEAD_DIM, :]
                ms = jnp.mean(q * q, axis=0, keepdims=True)
                cols.append(q * lax.rsqrt(ms + EPS) * qnw_ref[...] * (QK_SCALE * LOG2E))
            qs.append(jnp.concatenate(cols, axis=1).astype(BF16))
            qp_ref[slot, c, 0:HEAD_DIM, :] = qs[c]

        n_idx = lax.broadcasted_iota(jnp.int32, (ncb, NSA_TQ), 0)
        cbias = []
        for qt in tiles:
            tok_c = base_t0 + qt * NSA_TQ + lax.broadcasted_iota(jnp.int32, (ncb, NSA_TQ), 1)
            cbias.append(_tile4(jnp.where((n_idx * CMP_STRIDE + (CMP_BLOCK - 1)) <= tok_c, 0.0, NEG)))
        return [jnp.dot(kc_ref[0, 0, g].astype(BF16), qs[c], preferred_element_type=F32) + cbias[qt]
                for c, (qt, g) in enumerate(chains)]

    def select_probs(scs, base_t0, slot):
        has_block = []
        for qt in tiles:
            tok_row = base_t0 + qt * NSA_TQ + lax.broadcasted_iota(jnp.int32, (1, NSA_TQ), 1)
            has_block.append(_tile4(jnp.where(tok_row >= CMP_BLOCK - 1, 1.0, 0.0)))
        p = []
        for c, (qt, g) in enumerate(chains):
            sc = scs[c]
            mc = jnp.max(sc, axis=0, keepdims=True)
            ec = jnp.exp2(sc - mc)
            lc = jnp.sum(ec, axis=0, keepdims=True)
            p.append(ec * (has_block[qt] / lc))
            oc_ref[slot, c] = jnp.dot(vct_ref[0, 0, g].astype(BF16), p[c].astype(BF16),
                                      preferred_element_type=F32)

        jb = lax.broadcasted_iota(jnp.int32, (n_slc, NSA_TQ), 0)
        valid_s, force = [], []
        for qt in tiles:
            tok_s = base_t0 + qt * NSA_TQ + lax.broadcasted_iota(jnp.int32, (n_slc, NSA_TQ), 1)
            valid_s.append(jb * SLC_BLOCK <= tok_s)
            force.append((jb == (tok_s >> SLC_SHIFT)) | (jb == 0))
        scores = []
        for c, (qt, g) in enumerate(chains):
            ps = p[c][:, 0:NSA_TQ]
            for r in range(1, NSA_GROUP):
                ps = ps + p[c][:, r * NSA_TQ:(r + 1) * NSA_TQ]
            ps_hi = ps.astype(BF16)
            ps_lo = (ps - ps_hi.astype(F32)).astype(BF16)
            imp = (jnp.dot(mt_ref[...], ps_hi, preferred_element_type=F32)
                   + jnp.dot(mt_ref[...], ps_lo, preferred_element_type=F32))
            scores.append(jnp.where(valid_s[qt], jnp.where(force[qt], imp + FORCE_BONUS, imp), NEG))
        return scores, valid_s

    def select_rank(scores, valid_s, slot, n_live):
        if n_live <= SLC_TOPK:
            for c, (qt, g) in enumerate(chains):
                qp_ref[slot, c, HEAD_DIM:2 * HEAD_DIM, :] = _tile4(jnp.where(valid_s[qt], 0.0, NEG).astype(BF16))
            return
        sub = lax.broadcasted_iota(jnp.int32, (SUBLANES, NSA_TQ), 0)
        for c, (qt, g) in enumerate(chains):
            score = scores[c]
            blocks = [score[v * SUBLANES:(v + 1) * SUBLANES, :] for v in range(n_live // SUBLANES)]
            ranks = [jnp.zeros((SUBLANES, NSA_TQ), F32) for _ in blocks]
            for i in range(n_live):
                row = score[i:i + 1, :]
                for v, blk in enumerate(blocks):
                    if v * SUBLANES > i:
                        beats = row >= blk
                    elif (v + 1) * SUBLANES <= i:
                        beats = row > blk
                    else:
                        beats = (row > blk) | ((row >= blk) & (sub > i - v * SUBLANES))
                    ranks[v] = ranks[v] + jnp.where(beats, 1.0, 0.0)
            dead = [jnp.full((SUBLANES, NSA_TQ), float(n_slc), F32)] * ((n_slc - n_live) // SUBLANES)
            rank = jnp.concatenate(ranks + dead, axis=0)
            sel = (rank < float(SLC_TOPK)) & valid_s[qt]
            qp_ref[slot, c, HEAD_DIM:2 * HEAD_DIM, :] = _tile4(jnp.where(sel, 0.0, NEG).astype(BF16))

    step_tokens = NSA_QT * NSA_TQ

    def live_blocks(step):
        return ((step + 1) * step_tokens - 1) // SLC_BLOCK + 1

    @pl.when(qi == 0)
    def _():
        scores0, valid0 = select_probs(select_scores(qt_ref, 0, 0), 0, 0)
        select_rank(scores0, valid0, 0, live_blocks(0))

    cur = qi % 2
    nxt = 1 - cur
    t_next = t0 + NSA_QT * NSA_TQ

    kt0, ks0 = [], []
    for qt in tiles:
        kt0.append(jnp.maximum(qi * NSA_QT + qt - WIN_SIZE // NSA_TQ, 0))
        ks0.append(pl.multiple_of(kt0[qt] * NSA_TQ, NSA_TQ))
    for c, (qt, g) in enumerate(chains):
        sw_ref[c] = jnp.dot(kw_ref[0, g, pl.ds(ks0[qt], WIN_KEYS), 0:HEAD_DIM], qp_ref[cur, c, 0:HEAD_DIM, :],
                            preferred_element_type=F32)

    cmp_scores = select_scores(qtn_ref, t_next, nxt)

    vt_per_tile = SLC_TK // LANES

    def slc_scores(j, dst_ref):
        kst = pl.multiple_of(j * SLC_TK, SLC_TK)
        for c, (qt, g) in enumerate(chains):
            dst_ref[c] = jnp.dot(ks_ref[0, g, pl.ds(kst, SLC_TK), :], qp_ref[cur, c], preferred_element_type=F32)

    def slc_update(j, src_ref, causal):
        if causal:
            r_s = lax.broadcasted_iota(jnp.int32, (SLC_TK, NSA_TQ), 0)
            c_s = lax.broadcasted_iota(jnp.int32, (SLC_TK, NSA_TQ), 1)
            causal_bias = [_tile4(jnp.where(j * SLC_TK + r_s <= t0 + qt * NSA_TQ + c_s, 0.0, NEG)) for qt in tiles]
        for c, (qt, g) in enumerate(chains):
            s = src_ref[c]
            if causal:
                s = s + causal_bias[qt]
            m_old = m_ref[c]
            m_new = jnp.maximum(m_old, jnp.max(s, axis=0, keepdims=True))
            alpha = jnp.exp2(m_old - m_new)
            eb = jnp.exp2(s - m_new).astype(BF16)
            vt = jnp.concatenate([vst_ref[0, j * vt_per_tile + jj, g * V_ROWS:(g + 1) * V_ROWS, :]
                                  for jj in range(vt_per_tile)], axis=1)
            acc_ref[c] = alpha * acc_ref[c] + jnp.dot(vt, eb, preferred_element_type=F32)
            m_ref[c] = m_new

    slc_scores(0, sa_ref)
    sel_scores, sel_valid = select_probs(cmp_scores, t_next, nxt)

    c_minus_r = (lax.broadcasted_iota(jnp.int32, (WIN_KEYS, NSA_TQ), 1)
                 - lax.broadcasted_iota(jnp.int32, (WIN_KEYS, NSA_TQ), 0))
    wbias = []
    for qt in tiles:
        delta = (t0 + qt * NSA_TQ - ks0[qt]) + c_minus_r
        in_window = lax.bitcast_convert_type(delta, jnp.uint32) < WIN_SIZE
        wbias.append(_tile4(jnp.where(in_window, 0.0, NEG)))
    ow_t = []
    for c, (qt, g) in enumerate(chains):
        sw = sw_ref[c] + wbias[qt]
        mw = jnp.max(sw, axis=0, keepdims=True)
        ewb = jnp.exp2(sw - mw).astype(BF16)
        vwt = jnp.concatenate([vwt_ref[0, kt0[qt] + j, g * V_ROWS:(g + 1) * V_ROWS, :]
                               for j in range(WIN_KEYS // LANES)], axis=1)
        ow_aug = jnp.dot(vwt, ewb, preferred_element_type=F32)
        ow_t.append(ow_aug[0:HEAD_DIM, :] * (1.0 / ow_aug[HEAD_DIM:HEAD_DIM + 1, :]))

    need = live_blocks(qi + 1)
    bounds = list(range(SLC_TOPK, n_slc + 1, SLC_TOPK))
    for lo, hi in zip([0] + bounds[:-1], bounds):
        in_range = (need > lo) if hi == bounds[-1] else ((need > lo) & (need <= hi))
        pl.when(in_range)(functools.partial(select_rank, sel_scores, sel_valid, nxt, hi))

    m_ref[...] = jnp.full(m_ref.shape, NEG, F32)
    acc_ref[...] = jnp.zeros(acc_ref.shape, F32)
    n_full = t0 // SLC_TK

    def pair(jj, carry):
        j = 2 * jj
        slc_scores(j + 1, sb_ref)
        slc_update(j, sa_ref, False)
        slc_scores(j + 2, sa_ref)
        slc_update(j + 1, sb_ref, False)
        return carry

    lax.fori_loop(0, n_full // 2, pair, 0)

    @pl.when(n_full % 2 == 1)
    def _():
        slc_scores(n_full, sb_ref)
        slc_update(n_full - 1, sa_ref, False)
        slc_update(n_full, sb_ref, True)

    @pl.when(n_full % 2 == 0)
    def _():
        slc_update(n_full, sa_ref, True)

    for qt in tiles:
        gates = _sigmoid(glt_ref[0, qt] + bg_ref[...])
        outs = []
        for g in range(NSA_KV_HEADS):
            c = qt * NSA_KV_HEADS + g
            os_t = acc_ref[c, 0:HEAD_DIM, :] * (1.0 / acc_ref[c, HEAD_DIM:HEAD_DIM + 1, :])
            oc_t = oc_ref[cur, c]
            for r in range(NSA_GROUP):
                sl = slice(r * NSA_TQ, (r + 1) * NSA_TQ)
                row = g * gg + r
                outs.append(gates[row:row + 1, :] * oc_t[:, sl]
                            + gates[row + NSA_GROUP:row + NSA_GROUP + 1, :] * os_t[:, sl]
                            + gates[row + 2 * NSA_GROUP:row + 2 * NSA_GROUP + 1, :] * ow_t[c][:, sl])
        o_tok = jnp.concatenate(outs, axis=0).T
        rows = pl.ds(qt * NSA_TQ, NSA_TQ)
        ng = ng_ref[0, rows, :]
        o_ref[0, rows, :] = (o_tok * (ng * _sigmoid(ng))).astype(BF16)


def _nsa(nqt, glt, ng3, ks, kw, vt, kc, vct, mt, qnw, bg, B, T):
    nt = T // LANES
    ncb = kc.shape[3]
    G = NSA_KV_HEADS
    NC = NSA_QT * G
    ncols = NSA_GROUP * NSA_TQ
    steps = T // (NSA_QT * NSA_TQ)
    return pl.pallas_call(
        _nsa_kernel,
        grid=(B, steps),
        in_specs=[
            pl.BlockSpec((1, NSA_QT, D_NSA, LANES), lambda b, i: (b, i, 0, 0)),
            pl.BlockSpec((1, NSA_QT, D_NSA, LANES), lambda b, i: (b, jnp.minimum(i + 1, steps - 1), 0, 0)),
            pl.BlockSpec((1, NSA_QT, 16 * G, LANES), lambda b, i: (b, i, 0, 0)),
            pl.BlockSpec((1, NSA_QT * NSA_TQ, D_NSA), lambda b, i: (b, i, 0)),
            pl.BlockSpec((1, G, T, PAIR), lambda b, i: (b, 0, 0, 0)),
            pl.BlockSpec((1, G, T, PAIR), lambda b, i: (b, 0, 0, 0)),
            pl.BlockSpec((1, nt, G * V_ROWS, LANES), lambda b, i: (b, 0, 0, 0)),
            pl.BlockSpec((1, nt, G * V_ROWS, LANES), lambda b, i: (b, 0, 1, 0)),
            pl.BlockSpec((1, 1, G, ncb, HEAD_DIM), lambda b, i: (b, 0, 0, 0, 0)),
            pl.BlockSpec((1, 1, G, HEAD_DIM, ncb), lambda b, i: (b, 1, 0, 0, 0)),
            pl.BlockSpec(mt.shape, lambda b, i: (0, 0)),
            pl.BlockSpec((HEAD_DIM, 1), lambda b, i: (0, 0)),
            pl.BlockSpec((16 * G, 1), lambda b, i: (0, 0)),
        ],
        out_specs=pl.BlockSpec((1, NSA_QT * NSA_TQ, D_NSA), lambda b, i: (b, i, 0)),
        out_shape=jax.ShapeDtypeStruct((B, T, D_NSA), BF16),
        scratch_shapes=[
            pltpu.VMEM((2, NC, 2 * HEAD_DIM, ncols), BF16),
            pltpu.VMEM((2, NC, HEAD_DIM, ncols), F32),
            pltpu.VMEM((NC, SLC_TK, ncols), F32),
            pltpu.VMEM((NC, SLC_TK, ncols), F32),
            pltpu.VMEM((NC, WIN_KEYS, ncols), F32),
            pltpu.VMEM((NC, 1, ncols), F32),
            pltpu.VMEM((NC, V_ROWS, ncols), F32),
        ],
        compiler_params=pltpu.CompilerParams(dimension_semantics=("parallel", "arbitrary"),
                                             vmem_limit_bytes=VMEM_LIMIT),
        name="nsa",
    )(nqt, nqt, glt, ng3, ks, kw, vt, vt, kc, vct, mt, qnw, bg)


def _out_kernel(x_ref, yr_ref, yn_ref, wr_ref, wn_ref, o_ref):
    o_ref[...] = (x_ref[...]
                  + jnp.dot(yr_ref[...], wr_ref[...], preferred_element_type=F32)
                  + jnp.dot(yn_ref[...], wn_ref[...], preferred_element_type=F32))


def _outproj(x2, yr, yn, wr, wn):
    N = x2.shape[0]
    row = lambda i: (i, 0)
    const = lambda i: (0, 0)
    return pl.pallas_call(
        _out_kernel,
        grid=(N // OUT_TM,),
        in_specs=[pl.BlockSpec((OUT_TM, D_MODEL), row), pl.BlockSpec((OUT_TM, D_RET), row),
                  pl.BlockSpec((OUT_TM, D_NSA), row), pl.BlockSpec(wr.shape, const), pl.BlockSpec(wn.shape, const)],
        out_specs=pl.BlockSpec((OUT_TM, D_MODEL), row),
        out_shape=jax.ShapeDtypeStruct((N, D_MODEL), F32),
        compiler_params=pltpu.CompilerParams(dimension_semantics=("parallel",), vmem_limit_bytes=VMEM_LIMIT),
        name="outproj",
    )(x2, yr, yn, wr, wn)


@functools.lru_cache(maxsize=None)
def _tables(T):
    half = HEAD_DIM // 2
    inv = ROPE_THETA ** (-np.arange(half, dtype=np.float64) / half)
    ang = np.arange(T, dtype=np.float64)[:, None] * inv[None, :]
    cos = np.concatenate([np.cos(ang)] * 4, axis=1).astype(np.float32)
    sin = np.concatenate([-np.sin(ang), -np.sin(ang), np.sin(ang), np.sin(ang)], axis=1).astype(np.float32)

    C = RET_CHUNK
    log_g = np.log1p(-np.exp2(-5.0 - np.arange(RET_HEADS, dtype=np.float64)))
    pos = np.arange(C, dtype=np.float64)
    diff = pos[:, None] - pos[None, :]
    decay = np.where(diff >= 0, np.exp(log_g[:, None, None] * np.maximum(diff, 0.0)), 0.0)
    zeta = np.exp(log_g[:, None] * (C - 1.0 - pos))
    xi = np.exp(log_g[:, None] * (pos + 1.0))
    g_chunk = np.exp(log_g * C)
    npair = RET_HEADS // 2

    def pair_lanes(a):
        return np.repeat(a.reshape(npair, 2, C).transpose(0, 2, 1), HEAD_DIM, axis=2).astype(np.float32)

    dec = decay.reshape(npair, 2, C, C).astype(np.float32)
    gch = np.repeat(g_chunk.reshape(npair, 1, 2), HEAD_DIM, axis=2).astype(np.float32)

    n_cmp = (T - CMP_BLOCK) // CMP_STRIDE + 1
    ncb = T // CMP_STRIDE
    p = np.arange(n_cmp)[:, None] * CMP_STRIDE + np.arange(CMP_BLOCK)[None, :]
    blk = p // SLC_BLOCK
    M = (blk[:, :, None] == np.arange(T // SLC_BLOCK)[None, None, :]).mean(axis=1)
    mt = np.zeros((T // SLC_BLOCK, ncb), np.float32)
    mt[:, :n_cmp] = M.T
    return cos, sin, dec, pair_lanes(zeta), pair_lanes(xi), gch, mt


def kernel(x, norm_w, w_in, ret_norm_w, q_norm_w, k_norm_cmp, k_norm_slc, k_norm_win, cmp_pos_k, cmp_w1_k, cmp_w2_k,
           cmp_pos_v, cmp_w1_v, cmp_w2_v, b_gate, w_out):
    B, T, D = x.shape
    depth = norm_w.shape[0]
    cos, sin, dec, zeta, xi, gch, mt = _tables(T)
    ncb = T // CMP_STRIDE
    half = CMP_STRIDE * HEAD_DIM
    gate_src = np.zeros((NSA_KV_HEADS, 16), np.int32)
    gate_ok = np.zeros((NSA_KV_HEADS, 16), bool)
    for g in range(NSA_KV_HEADS):
        for br in range(N_BRANCH):
            for r in range(NSA_GROUP):
                gate_src[g, br * NSA_GROUP + r] = br * NSA_HEADS + g * NSA_GROUP + r
                gate_ok[g, br * NSA_GROUP + r] = True
    gate_src = gate_src.reshape(-1)
    gate_ok = gate_ok.reshape(-1)

    x2 = x.reshape(B * T, D)
    for layer in range(depth):
        w = w_in[layer].astype(BF16)
        o_ng = 4 * D_RET + D_NSA
        o_kv = o_ng + D_NSA
        quarter = HEAD_DIM // 2
        pair_perm = np.concatenate([np.arange(quarter), HEAD_DIM + np.arange(quarter),
                                    quarter + np.arange(quarter), HEAD_DIM + quarter + np.arange(quarter)])
        qk_perm = np.concatenate([p * PAIR + pair_perm for p in range(RET_HEADS // 2)])
        w_ret = jnp.concatenate([w[:, :D_RET][:, qk_perm], w[:, D_RET:2 * D_RET][:, qk_perm],
                                 w[:, 2 * D_RET:4 * D_RET]], axis=1)
        wt_q = w[:, 4 * D_RET:o_ng].T
        w_ng = w[:, o_ng:o_kv]
        w_ckv = w[:, o_kv:o_kv + 2 * D_KV]
        w_skwk = jnp.concatenate([w[:, o_kv + 2 * D_KV:o_kv + 3 * D_KV], w[:, o_kv + 4 * D_KV:o_kv + 5 * D_KV]], axis=1)
        wt_v = jnp.concatenate([w[:, o_kv + 3 * D_KV:o_kv + 4 * D_KV], w[:, o_kv + 5 * D_KV:o_kv + 6 * D_KV]], axis=1).T
        w_gl = w[:, o_kv + 6 * D_KV:]
        wt_g = jnp.where(gate_ok[:, None], w_gl.T[gate_src], jnp.zeros((), BF16))
        bg = jnp.where(gate_ok, b_gate[layer][gate_src], 0.0).reshape(-1, 1)

        knw = jnp.stack([jnp.tile(k_norm_slc[layer], 2), jnp.tile(k_norm_win[layer], 2)])
        ret, ng, xc, ks, kw, nqt, vt, glt = _proj(x2, norm_w[layer].reshape(1, D), w_ret, w_ng, w_ckv, w_skwk,
                                                  wt_q, wt_v, wt_g, knw, B, T)

        nw_pair = ret_norm_w[layer].reshape(RET_HEADS // 2, 1, PAIR)
        y_ret = _retention(ret.reshape(B, T, 4 * D_RET), jnp.asarray(cos), jnp.asarray(sin), jnp.asarray(dec),
                           jnp.asarray(zeta), jnp.asarray(xi), jnp.asarray(gch), nw_pair, B, T)

        pos =jnp.stack([cmp_pos_k[layer], cmp_pos_v[layer]]).reshape(2, 2, half)
        w1 = jnp.stack([cmp_w1_k[layer], cmp_w1_v[layer]]).astype(BF16)
        w2 = jnp.stack([cmp_w2_k[layer], cmp_w2_v[layer]]).astype(BF16)
        w2t = jnp.swapaxes(w2, 1, 2)
        kc, vct = _compress(xc, pos, w1, w2, w2t, k_norm_cmp[layer].reshape(1, HEAD_DIM), B, ncb)

        y_nsa = _nsa(nqt, glt, ng.reshape(B, T, D_NSA), ks, kw, vt, kc, vct, jnp.asarray(mt).astype(BF16),
                     q_norm_w[layer].reshape(HEAD_DIM, 1), bg, B, T)

        wo = w_out[layer].astype(BF16)
        x2 = _outproj(x2, y_ret.reshape(B * T, D_RET), y_nsa.reshape(B * T, D_NSA), wo[:D_RET], wo[D_RET:])
    return x2.reshape(B, T, D)
```

```python
import functools

import numpy as np
import jax
import jax.numpy as jnp
from jax import lax
from jax.experimental import pallas as pl
from jax.experimental.pallas import tpu as pltpu

F32 = jnp.float32
BF16 = jnp.bfloat16

D_MODEL = 1024
HEAD_DIM = 64
HEAD_SHIFT = 6
RET_HEADS = 8
NSA_HEADS = 8
NSA_KV_HEADS = 2
NSA_GROUP = NSA_HEADS // NSA_KV_HEADS
D_RET = RET_HEADS * HEAD_DIM
D_NSA = NSA_HEADS * HEAD_DIM
D_KV = NSA_KV_HEADS * HEAD_DIM
N_BRANCH = 3
RET_CHUNK = 128
ROPE_THETA = 10000.0
CMP_BLOCK = 32
CMP_STRIDE = 16
CMP_HIDDEN = 256
SLC_BLOCK = 64
SLC_SHIFT = 6
SLC_TOPK = 16
WIN_SIZE = 512
EPS = 1e-6
NEG = -1e30
FORCE_BONUS = 1e4
QK_SCALE = HEAD_DIM ** -0.5
LOG2E = 1.4426950408889634
V_ROWS = HEAD_DIM + 16

LANES = 128
SUBLANES = 8
PAIR = 2 * HEAD_DIM
VMEM_LIMIT = 48 * 1024 * 1024

PROJ_TM = 512
RET_TC = 512
NSA_TQ = 128
NSA_QT = 2
SLC_TK = 512
WIN_KEYS = WIN_SIZE + NSA_TQ
OUT_TM = 512

NT_DIMS = (((1,), (1,)), ((), ()))
TN_DIMS = (((0,), (0,)), ((), ()))


def _sigmoid(x):
    return 1.0 / (1.0 + jnp.exp(-x))


def _tile4(a):
    return jnp.concatenate([a, a, a, a], axis=1)


def _proj_kernel(steps_per_batch, x_ref, nw_ref, w_ret_ref, w_ng_ref, w_ckv_ref, w_skwk_ref, wt_q_ref, wt_v_ref,
                 wt_g_ref, knw_ref, ret_ref, ng_ref, xc_ref, ks_ref, kw_ref, nqt_ref, vt_ref, glt_ref, ckv_scr):
    x = x_ref[...]
    ms = jnp.mean(x * x, axis=-1, keepdims=True)
    h = (x * lax.rsqrt(ms + EPS) * nw_ref[...]).astype(BF16)
    ret_ref[...] = jnp.dot(h, w_ret_ref[...], preferred_element_type=F32)
    ng_ref[...] = jnp.dot(h, w_ng_ref[...], preferred_element_type=F32)

    ckv = jnp.dot(h, w_ckv_ref[...], preferred_element_type=F32)
    for half in range(2 * D_KV // LANES):
        ckv_scr[half] = ckv[:, half * LANES:(half + 1) * LANES]
    for l in range(CMP_STRIDE):
        for half in range(2 * D_KV // LANES):
            rows = ckv_scr[half, pl.ds(l, PROJ_TM // CMP_STRIDE, stride=CMP_STRIDE), :]
            for s in range(LANES // HEAD_DIM):
                xc_ref[0, half * (LANES // HEAD_DIM) + s, :, l * HEAD_DIM:(l + 1) * HEAD_DIM] = (
                    rows[:, s * HEAD_DIM:(s + 1) * HEAD_DIM])

    skwk = jnp.dot(h, w_skwk_ref[...], preferred_element_type=F32)
    lane = lax.broadcasted_iota(jnp.int32, (1, PAIR), 1)
    head0 = lane < HEAD_DIM
    m0 = jnp.where(head0, 1.0, 0.0)
    m1 = 1.0 - m0
    t_start = (pl.program_id(0) % steps_per_batch) * PROJ_TM
    tok = t_start + lax.broadcasted_iota(jnp.int32, (PROJ_TM, PAIR), 0)
    col = lax.broadcasted_iota(jnp.int32, (PROJ_TM, PAIR), 1)
    indicator = jnp.where((tok >> SLC_SHIFT) == col - HEAD_DIM, 1.0, 0.0)

    def pair_normed(t, w):
        t2 = t * t
        ms0 = jnp.sum(t2 * m0, axis=-1, keepdims=True)
        ms1 = jnp.sum(t2 * m1, axis=-1, keepdims=True)
        return t * lax.rsqrt(jnp.where(head0, ms0, ms1) * (1.0 / HEAD_DIM) + EPS) * w

    ns = pair_normed(skwk[:, :PAIR], knw_ref[0:1, :])
    nwin = pair_normed(skwk[:, PAIR:], knw_ref[1:2, :])
    for g in range(NSA_KV_HEADS):
        s_g = ns if g == 0 else pltpu.roll(ns, HEAD_DIM, 1)
        w_g = nwin if g == 0 else pltpu.roll(nwin, HEAD_DIM, 1)
        ks_ref[0, g] = jnp.where(head0, s_g, indicator).astype(BF16)
        kw_ref[0, g] = jnp.where(head0, w_g, 0.0).astype(BF16)

    qt =lax.dot_general(wt_q_ref[...], h, NT_DIMS, preferred_element_type=F32)
    vt = lax.dot_general(wt_v_ref[...], h, NT_DIMS, preferred_element_type=F32)
    gt = lax.dot_general(wt_g_ref[...], h, NT_DIMS, preferred_element_type=F32)
    pad_row = lax.broadcasted_iota(jnp.int32, (V_ROWS - HEAD_DIM, LANES), 0)
    ones_pad = jnp.where(pad_row == 0, 1.0, 0.0).astype(BF16)
    for j in range(PROJ_TM // LANES):
        sl = slice(j * LANES, (j + 1) * LANES)
        nqt_ref[0, j] = qt[:, sl]
        for blk in range(2 * NSA_KV_HEADS):
            vt_ref[0, j, blk * V_ROWS:blk * V_ROWS + HEAD_DIM, :] = (
                vt[blk * HEAD_DIM:(blk + 1) * HEAD_DIM, sl].astype(BF16))
            vt_ref[0, j, blk * V_ROWS + HEAD_DIM:(blk + 1) * V_ROWS, :] = ones_pad
        glt_ref[0, j] = gt[:, sl]


def _proj(x2, nw, w_ret, w_ng, w_ckv, w_skwk, wt_q, wt_v, wt_g, knw, B, T):
    N = B * T
    tpb = T // PROJ_TM
    sub = PROJ_TM // LANES
    nt = T // LANES
    const = lambda i: (0, 0)
    row = lambda i: (i, 0)
    trn = lambda i: (i // tpb, i % tpb, 0, 0)
    tokm = lambda i: (i // tpb, 0, i % tpb, 0)
    return pl.pallas_call(
        functools.partial(_proj_kernel, tpb),
        grid=(N // PROJ_TM,),
        in_specs=[
            pl.BlockSpec((PROJ_TM, D_MODEL), row),
            pl.BlockSpec((1, D_MODEL), const),
            pl.BlockSpec(w_ret.shape, const),
            pl.BlockSpec(w_ng.shape, const),
            pl.BlockSpec(w_ckv.shape, const),
            pl.BlockSpec(w_skwk.shape, const),
            pl.BlockSpec(wt_q.shape, const),
            pl.BlockSpec(wt_v.shape, const),
            pl.BlockSpec(wt_g.shape, const),
            pl.BlockSpec(knw.shape, const),
        ],
        out_specs=[
            pl.BlockSpec((PROJ_TM, 4 * D_RET), row),
            pl.BlockSpec((PROJ_TM, D_NSA), row),
            pl.BlockSpec((1, 2 * NSA_KV_HEADS, PROJ_TM // CMP_STRIDE, CMP_STRIDE * HEAD_DIM), tokm),
            pl.BlockSpec((1, NSA_KV_HEADS, PROJ_TM, PAIR), tokm),
            pl.BlockSpec((1, NSA_KV_HEADS, PROJ_TM, PAIR), tokm),
            pl.BlockSpec((1, sub, D_NSA, LANES), trn),
            pl.BlockSpec((1, sub, 2 * NSA_KV_HEADS * V_ROWS, LANES), trn),
            pl.BlockSpec((1, sub, 32, LANES), trn),
        ],
        out_shape=[
            jax.ShapeDtypeStruct((N, 4 * D_RET), F32),
            jax.ShapeDtypeStruct((N, D_NSA), F32),
            jax.ShapeDtypeStruct((B, 2 * NSA_KV_HEADS, T // CMP_STRIDE, CMP_STRIDE * HEAD_DIM), F32),
            jax.ShapeDtypeStruct((B, NSA_KV_HEADS, T, PAIR), BF16),
            jax.ShapeDtypeStruct((B, NSA_KV_HEADS, T, PAIR), BF16),
            jax.ShapeDtypeStruct((B, nt, D_NSA, LANES), F32),
            jax.ShapeDtypeStruct((B, nt, 2 * NSA_KV_HEADS * V_ROWS, LANES), BF16),
            jax.ShapeDtypeStruct((B, nt, 32, LANES), F32),
        ],
        scratch_shapes=[pltpu.VMEM((2 * D_KV // LANES, PROJ_TM, LANES), F32)],
        compiler_params=pltpu.CompilerParams(dimension_semantics=("parallel",), vmem_limit_bytes=VMEM_LIMIT),
        name="proj",
    )(x2, nw, w_ret, w_ng, w_ckv, w_skwk, wt_q, wt_v, wt_g, knw)


def _ret_kernel(q_ref, k_ref, v_ref, g_ref, cos_ref, sin_ref, dec_ref, zeta_ref, xi_ref, gch_ref, nw_ref,
                o_ref, state_ref):
    @pl.when(pl.program_id(1) == 0)
    def _():
        state_ref[...] = jnp.zeros_like(state_ref)

    lane = lax.broadcasted_iota(jnp.int32, (1, PAIR), 1)
    q_head = (lane >> (HEAD_SHIFT - 1)) & 1
    v_head = lane >> HEAD_SHIFT
    q_mask = [jnp.where(q_head == h, 1.0, 0.0).astype(BF16) for h in (0, 1)]
    v_mask = [jnp.where(v_head == h, 1.0, 0.0).astype(BF16) for h in (0, 1)]
    row_qh = (lax.broadcasted_iota(jnp.int32, (PAIR, PAIR), 0) >> (HEAD_SHIFT - 1)) & 1
    row_vh = lax.broadcasted_iota(jnp.int32, (PAIR, PAIR), 0) >> HEAD_SHIFT
    col_vh = lax.broadcasted_iota(jnp.int32, (PAIR, PAIR), 1) >> HEAD_SHIFT
    same_head_kv = jnp.where(row_qh == col_vh, 1.0, 0.0)
    head_mean = jnp.where(row_vh == col_vh, 1.0 / HEAD_DIM, 0.0).astype(BF16)

    pairs = range(RET_HEADS // 2)
    mean2 = jnp.concatenate([head_mean, head_mean], axis=0)
    for c in range(RET_TC // RET_CHUNK):
        sl = pl.ds(c * RET_CHUNK, RET_CHUNK)
        cos = cos_ref[sl, :]
        sin = sin_ref[sl, :]
        cols = [slice(p * PAIR, (p + 1) * PAIR) for p in pairs]
        qb, kb, vb, vzb = [], [], [], []
        for p in pairs:
            q = q_ref[0, sl, cols[p]]
            k = k_ref[0, sl, cols[p]]
            v = v_ref[0, sl, cols[p]]
            qb.append((q * cos + pltpu.roll(q, HEAD_DIM, 1) * sin).astype(BF16))
            kb.append(((k * cos + pltpu.roll(k, HEAD_DIM, 1) * sin) * QK_SCALE).astype(BF16))
            vb.append(v.astype(BF16))
            vzb.append((v * zeta_ref[p]).astype(BF16))
        states = [state_ref[p] for p in pairs]
        kk = [jnp.concatenate([kb[p] * q_mask[h] for h in (0, 1)], axis=0) for p in pairs]
        s = [lax.dot_general(qb[p], kk[p], NT_DIMS, preferred_element_type=F32) for p in pairs]
        o_cross = [jnp.dot(qb[p], states[p].astype(BF16), preferred_element_type=F32) for p in pairs]
        kv = [lax.dot_general(kb[p], vzb[p], TN_DIMS, preferred_element_type=F32) for p in pairs]
        sb = [(s[p] * dec_ref[p]).astype(BF16) for p in pairs]
        vv = [jnp.concatenate([vb[p] * v_mask[h] for h in (0, 1)], axis=0) for p in pairs]
        o = [jnp.dot(sb[p], vv[p], preferred_element_type=F32) + o_cross[p] * xi_ref[p] for p in pairs]
        for p in pairs:
            state_ref[p] = states[p] * gch_ref[p] + kv[p] * same_head_kv
        o2 = [o[p] * o[p] for p in pairs]
        o2_hi = [o2[p].astype(BF16) for p in pairs]
        o2_hl = [jnp.concatenate([o2_hi[p], (o2[p] - o2_hi[p].astype(F32)).astype(BF16)], axis=1) for p in pairs]
        ms = [jnp.dot(o2_hl[p], mean2, preferred_element_type=F32) for p in pairs]
        for p in pairs:
            g = g_ref[0, sl, cols[p]]
            y = o[p] * lax.rsqrt(ms[p] + EPS) * nw_ref[p]
            o_ref[0, sl, cols[p]] = (y * (g * _sigmoid(g))).astype(BF16)


def _retention(ret3, cos, sin, dec, zeta, xi, gch, nw, B, T):
    npair = RET_HEADS // 2
    blk = lambda col: pl.BlockSpec((1, RET_TC, D_RET), lambda b, i: (b, i, col))
    tab = pl.BlockSpec((RET_TC, PAIR), lambda b, i: (i, 0))
    whole = lambda a: pl.BlockSpec(a.shape, lambda b, i: (0,) * a.ndim)
    return pl.pallas_call(
        _ret_kernel,
        grid=(B, T // RET_TC),
        in_specs=[blk(0), blk(1), blk(2), blk(3), tab, tab, whole(dec), whole(zeta), whole(xi), whole(gch), whole(nw)],
        out_specs=pl.BlockSpec((1, RET_TC, D_RET), lambda b, i: (b, i, 0)),
        out_shape=jax.ShapeDtypeStruct((B, T, D_RET), BF16),
        scratch_shapes=[pltpu.VMEM((npair, PAIR, PAIR), F32)],
        compiler_params=pltpu.CompilerParams(dimension_semantics=("parallel", "arbitrary"),
                                             vmem_limit_bytes=VMEM_LIMIT),
        name="retention",
    )(ret3, ret3, ret3, ret3, cos, sin, dec, zeta, xi, gch, nw)


def _cmp_kernel(x_ref, pos_ref, w1_ref, w2_ref, w2t_ref, knw_ref, o_ref, ot_ref):
    is_key = pl.program_id(1) == 0
    half = CMP_STRIDE * HEAD_DIM
    for g in range(NSA_KV_HEADS):
        x = x_ref[0, g]
        a = jnp.dot((x + pos_ref[0, 0:1, :]).astype(BF16), w1_ref[0, :half, :], preferred_element_type=F32)
        b = jnp.dot((x + pos_ref[0, 1:2, :]).astype(BF16), w1_ref[0, half:, :], preferred_element_type=F32)
        hid = a + pltpu.roll(b, b.shape[0] - 1, 0)
        hid = (hid * _sigmoid(hid)).astype(BF16)
        out = jnp.dot(hid, w2_ref[0], preferred_element_type=F32)
        ms = jnp.mean(out * out, axis=-1, keepdims=True)
        normed = out * lax.rsqrt(ms + EPS) * knw_ref[...]
        o_ref[0, 0, g] = jnp.where(is_key, normed, out)
        ot_ref[0, 0, g] = lax.dot_general(w2t_ref[0], hid, NT_DIMS, preferred_element_type=F32)


def _compress(xc, pos, w1, w2, w2t, knw, B, ncb):
    return pl.pallas_call(
        _cmp_kernel,
        grid=(B, 2),
        in_specs=[
            pl.BlockSpec((1, NSA_KV_HEADS, ncb, CMP_STRIDE * HEAD_DIM), lambda b, s: (b, s, 0, 0)),
            pl.BlockSpec((1, 2, CMP_STRIDE * HEAD_DIM), lambda b, s: (s, 0, 0)),
            pl.BlockSpec((1, CMP_BLOCK * HEAD_DIM, CMP_HIDDEN), lambda b, s: (s, 0, 0)),
            pl.BlockSpec((1, CMP_HIDDEN, HEAD_DIM), lambda b, s: (s, 0, 0)),
            pl.BlockSpec((1, HEAD_DIM, CMP_HIDDEN), lambda b, s: (s, 0, 0)),
            pl.BlockSpec((1, HEAD_DIM), lambda b, s: (0, 0)),
        ],
        out_specs=[
            pl.BlockSpec((1, 1, NSA_KV_HEADS, ncb, HEAD_DIM), lambda b, s: (b, s, 0, 0, 0)),
            pl.BlockSpec((1, 1, NSA_KV_HEADS, HEAD_DIM, ncb), lambda b, s: (b, s, 0, 0, 0)),
        ],
        out_shape=[
            jax.ShapeDtypeStruct((B, 2, NSA_KV_HEADS, ncb, HEAD_DIM), F32),
            jax.ShapeDtypeStruct((B, 2, NSA_KV_HEADS, HEAD_DIM, ncb), F32),
        ],
        compiler_params=pltpu.CompilerParams(dimension_semantics=("parallel", "parallel"),
                                             vmem_limit_bytes=VMEM_LIMIT),
        name="compress",
    )(xc, pos, w1, w2, w2t, knw)


def _nsa_kernel(qt_ref, qtn_ref, glt_ref, ng_ref, ks_ref, kw_ref, vst_ref, vwt_ref, kc_ref, vct_ref, mt_ref, qnw_ref,
                bg_ref, o_ref, qp_ref, oc_ref, sa_ref, sb_ref, mxa_ref, mxb_ref, sw_ref, m_ref, acc_ref):
    qi = pl.program_id(1)
    t0 = qi * (NSA_QT * NSA_TQ)
    tiles = range(NSA_QT)
    chains = [(qt, g) for qt in tiles for g in range(NSA_KV_HEADS)]
    gq = NSA_GROUP * HEAD_DIM
    gg = 16
    ncb = kc_ref.shape[3]
    n_slc = mt_ref.shape[0]

    def select_masks(base_t0):
        n_idx = lax.broadcasted_iota(jnp.int32, (ncb, NSA_TQ), 0)
        jb = lax.broadcasted_iota(jnp.int32, (n_slc, NSA_TQ), 0)
        out = []
        for qt in tiles:
            tok = base_t0 + qt * NSA_TQ
            tok_c = tok + lax.broadcasted_iota(jnp.int32, (ncb, NSA_TQ), 1)
            cbias = _tile4(jnp.where((n_idx * CMP_STRIDE + (CMP_BLOCK - 1)) <= tok_c, 0.0, NEG))
            tok_row = tok + lax.broadcasted_iota(jnp.int32, (1, NSA_TQ), 1)
            has_block = _tile4(jnp.where(tok_row >= CMP_BLOCK - 1, 1.0, 0.0))
            tok_s = tok + lax.broadcasted_iota(jnp.int32, (n_slc, NSA_TQ), 1)
            valid_s = jb * SLC_BLOCK <= tok_s
            force = (jb == (tok_s >> SLC_SHIFT)) | (jb == 0)
            out.append((cbias, has_block, valid_s, force))
        return out

    def select_scores(src_ref, slot, masks, c):
        qt, g = chains[c]
        cols = []
        for r in range(NSA_GROUP):
            q = src_ref[0, qt, g * gq + r * HEAD_DIM:g * gq + (r + 1) * HEAD_DIM, :]
            ms = jnp.mean(q * q, axis=0, keepdims=True)
            cols.append(q * lax.rsqrt(ms + EPS) * qnw_ref[...] * (QK_SCALE * LOG2E))
        qs = jnp.concatenate(cols, axis=1).astype(BF16)
        qp_ref[slot, c, 0:HEAD_DIM, :] = qs
        return jnp.dot(kc_ref[0, 0, g].astype(BF16), qs, preferred_element_type=F32) + masks[qt][0]

    def select_probs(sc, slot, masks, c):
        qt, g = chains[c]
        _, has_block, valid_s, force = masks[qt]
        mc = jnp.max(sc, axis=0, keepdims=True)
        ec = jnp.exp2(sc - mc)
        lc = jnp.sum(ec, axis=0, keepdims=True)
        p = ec * (has_block / lc)
        oc_ref[slot, c] = jnp.dot(vct_ref[0, 0, g].astype(BF16), p.astype(BF16), preferred_element_type=F32)
        ps = p[:, 0:NSA_TQ]
        for r in range(1, NSA_GROUP):
            ps = ps + p[:, r * NSA_TQ:(r + 1) * NSA_TQ]
        ps_hi = ps.astype(BF16)
        ps_lo = (ps - ps_hi.astype(F32)).astype(BF16)
        imp = (jnp.dot(mt_ref[...], ps_hi, preferred_element_type=F32)
               + jnp.dot(mt_ref[...], ps_lo, preferred_element_type=F32))
        return jnp.where(valid_s, jnp.where(force, imp + FORCE_BONUS, imp), NEG)

    def select_rank(scores, valid_s, slot, n_live):
        if n_live <= SLC_TOPK:
            for c, (qt, g) in enumerate(chains):
                qp_ref[slot, c, HEAD_DIM:2 * HEAD_DIM, :] = _tile4(jnp.where(valid_s[qt], 0.0, NEG).astype(BF16))
            return
        sub = lax.broadcasted_iota(jnp.int32, (SUBLANES, NSA_TQ), 0)
        for c, (qt, g) in enumerate(chains):
            score = scores[c]
            blocks = [score[v * SUBLANES:(v + 1) * SUBLANES, :] for v in range(n_live // SUBLANES)]
            ranks = [jnp.zeros((SUBLANES, NSA_TQ), F32) for _ in blocks]
            for i in range(n_live):
                row = score[i:i + 1, :]
                for v, blk in enumerate(blocks):
                    if v * SUBLANES > i:
                        beats = row >= blk
                    elif (v + 1) * SUBLANES <= i:
                        beats = row > blk
                    else:
                        beats = (row > blk) | ((row >= blk) & (sub > i - v * SUBLANES))
                    ranks[v] = ranks[v] + jnp.where(beats, 1.0, 0.0)
            dead = [jnp.full((SUBLANES, NSA_TQ), float(n_slc), F32)] * ((n_slc - n_live) // SUBLANES)
            rank = jnp.concatenate(ranks + dead, axis=0)
            sel = (rank < float(SLC_TOPK)) & valid_s[qt]
            qp_ref[slot, c, HEAD_DIM:2 * HEAD_DIM, :] = _tile4(jnp.where(sel, 0.0, NEG).astype(BF16))

    step_tokens = NSA_QT * NSA_TQ

    def live_blocks(step):
        return ((step + 1) * step_tokens - 1) // SLC_BLOCK + 1

    all_chains = list(range(len(chains)))

    @pl.when(qi == 0)
    def _():
        masks0 = select_masks(0)
        scores0 = [select_probs(select_scores(qt_ref, 0, masks0, c), 0, masks0, c) for c in all_chains]
        select_rank(scores0, [m[2] for m in masks0], 0, live_blocks(0))

    cur = qi % 2
    nxt = 1 - cur
    t_next = t0 + NSA_QT * NSA_TQ

    kt0, ks0, wbias = [], [], []
    c_minus_r = (lax.broadcasted_iota(jnp.int32, (WIN_KEYS, NSA_TQ), 1)
                 - lax.broadcasted_iota(jnp.int32, (WIN_KEYS, NSA_TQ), 0))
    for qt in tiles:
        kt0.append(jnp.maximum(qi * NSA_QT + qt - WIN_SIZE // NSA_TQ, 0))
        ks0.append(pl.multiple_of(kt0[qt] * NSA_TQ, NSA_TQ))
        delta = (t0 + qt * NSA_TQ - ks0[qt]) + c_minus_r
        in_window = lax.bitcast_convert_type(delta, jnp.uint32) < WIN_SIZE
        wbias.append(_tile4(jnp.where(in_window, 0.0, NEG)))

    def win_scores(c):
        qt, g = chains[c]
        sw = jnp.dot(kw_ref[0, g, pl.ds(ks0[qt], WIN_KEYS), 0:HEAD_DIM], qp_ref[cur, c, 0:HEAD_DIM, :],
                     preferred_element_type=F32) + wbias[qt]
        sw_ref[c] = sw
        return jnp.max(sw, axis=0, keepdims=True)

    def win_attend(c, mw):
        qt, g = chains[c]
        ewb = jnp.exp2(sw_ref[c] - mw).astype(BF16)
        vwt = jnp.concatenate([vwt_ref[0, kt0[qt] + j, g * V_ROWS:(g + 1) * V_ROWS, :]
                               for j in range(WIN_KEYS // LANES)], axis=1)
        ow_aug = jnp.dot(vwt, ewb, preferred_element_type=F32)
        return ow_aug[0:HEAD_DIM, :] * (1.0 / ow_aug[HEAD_DIM:HEAD_DIM + 1, :])

    vt_per_tile = SLC_TK // LANES

    def slc_scores(j, dst_ref, mx_ref, which=all_chains):
        kst = pl.multiple_of(j * SLC_TK, SLC_TK)
        for c in which:
            qt, g = chains[c]
            s = jnp.dot(ks_ref[0, g, pl.ds(kst, SLC_TK), :], qp_ref[cur, c], preferred_element_type=F32)
            dst_ref[c] = s
            mx_ref[c] = jnp.max(s, axis=0, keepdims=True)

    def slc_update(j, src_ref, mx_ref, causal, which=all_chains):
        if causal:
            r_s = lax.broadcasted_iota(jnp.int32, (SLC_TK, NSA_TQ), 0)
            c_s = lax.broadcasted_iota(jnp.int32, (SLC_TK, NSA_TQ), 1)
            causal_bias = [_tile4(jnp.where(j * SLC_TK + r_s <= t0 + qt * NSA_TQ + c_s, 0.0, NEG)) for qt in tiles]
        for c in which:
            qt, g = chains[c]
            s = src_ref[c]
            if causal:
                s = s + causal_bias[qt]
                tile_max = jnp.max(s, axis=0, keepdims=True)
            else:
                tile_max = mx_ref[c]
            m_old = m_ref[c]
            m_new = jnp.maximum(m_old, tile_max)
            alpha = jnp.exp2(m_old - m_new)
            eb = jnp.exp2(s - m_new).astype(BF16)
            vt = jnp.concatenate([vst_ref[0, j * vt_per_tile + jj, g * V_ROWS:(g + 1) * V_ROWS, :]
                                  for jj in range(vt_per_tile)], axis=1)
            acc_ref[c] = alpha * acc_ref[c] + jnp.dot(vt, eb, preferred_element_type=F32)
            m_ref[c] = m_new

    masks = select_masks(t_next)

    def matmul_stage(c):
        mw = win_scores(c)
        sc = select_scores(qtn_ref, nxt, masks, c)
        slc_scores(0, sa_ref, mxa_ref, [c])
        return sc, mw

    sel_scores, ow_t = [], []
    staged = matmul_stage(0)
    for c in all_chains:
        staged_next = matmul_stage(c + 1) if c + 1 < len(chains) else None
        sel_scores.append(select_probs(staged[0], nxt, masks, c))
        ow_t.append(win_attend(c, staged[1]))
        staged = staged_next
    sel_valid = [m[2] for m in masks]

    need = live_blocks(qi + 1)
    bounds = list(range(SLC_TOPK, n_slc + 1, SLC_TOPK))
    for lo, hi in zip([0] + bounds[:-1], bounds):
        in_range = (need > lo) if hi == bounds[-1] else ((need > lo) & (need <= hi))
        pl.when(in_range)(functools.partial(select_rank, sel_scores, sel_valid, nxt, hi))

    m_ref[...] = jnp.full(m_ref.shape, NEG, F32)
    acc_ref[...] = jnp.zeros(acc_ref.shape, F32)
    n_full = t0 // SLC_TK

    def pair(jj, carry):
        j = 2 * jj
        for c in all_chains:
            slc_scores(j + 1, sb_ref, mxb_ref, [c])
            slc_update(j, sa_ref, mxa_ref, False, [c])
        for c in all_chains:
            slc_scores(j + 2, sa_ref, mxa_ref, [c])
            slc_update(j + 1, sb_ref, mxb_ref, False, [c])
        return carry

    lax.fori_loop(0, n_full // 2, pair, 0)

    @pl.when(n_full % 2 == 1)
    def _():
        for c in all_chains:
            slc_scores(n_full, sb_ref, mxb_ref, [c])
            slc_update(n_full - 1, sa_ref, mxa_ref, False, [c])
        slc_update(n_full, sb_ref, mxb_ref, True)

    @pl.when(n_full % 2 == 0)
    def _():
        slc_update(n_full, sa_ref, mxa_ref, True)

    for qt in tiles:
        gates = _sigmoid(glt_ref[0, qt] + bg_ref[...])
        outs = []
        for g in range(NSA_KV_HEADS):
            c = qt * NSA_KV_HEADS + g
            os_t = acc_ref[c, 0:HEAD_DIM, :] * (1.0 / acc_ref[c, HEAD_DIM:HEAD_DIM + 1, :])
            oc_t = oc_ref[cur, c]
            for r in range(NSA_GROUP):
                sl = slice(r * NSA_TQ, (r + 1) * NSA_TQ)
                row = g * gg + r
                outs.append(gates[row:row + 1, :] * oc_t[:, sl]
                            + gates[row + NSA_GROUP:row + NSA_GROUP + 1, :] * os_t[:, sl]
                            + gates[row + 2 * NSA_GROUP:row + 2 * NSA_GROUP + 1, :] * ow_t[c][:, sl])
        o_tok = jnp.concatenate(outs, axis=0).T
        rows = pl.ds(qt * NSA_TQ, NSA_TQ)
        ng = ng_ref[0, rows, :]
        o_ref[0, rows, :] = (o_tok * (ng * _sigmoid(ng))).astype(BF16)


def _nsa(nqt, glt, ng3, ks, kw, vt, kc, vct, mt, qnw, bg, B, T):
    nt = T // LANES
    ncb = kc.shape[3]
    G = NSA_KV_HEADS
    NC = NSA_QT * G
    ncols = NSA_GROUP * NSA_TQ
    steps = T // (NSA_QT * NSA_TQ)
    return pl.pallas_call(
        _nsa_kernel,
        grid=(B, steps),
        in_specs=[
            pl.BlockSpec((1, NSA_QT, D_NSA, LANES), lambda b, i: (b, i, 0, 0)),
            pl.BlockSpec((1, NSA_QT, D_NSA, LANES), lambda b, i: (b, jnp.minimum(i + 1, steps - 1), 0, 0)),
            pl.BlockSpec((1, NSA_QT, 16 * G, LANES), lambda b, i: (b, i, 0, 0)),
            pl.BlockSpec((1, NSA_QT * NSA_TQ, D_NSA), lambda b, i: (b, i, 0)),
            pl.BlockSpec((1, G, T, PAIR), lambda b, i: (b, 0, 0, 0)),
            pl.BlockSpec((1, G, T, PAIR), lambda b, i: (b, 0, 0, 0)),
            pl.BlockSpec((1, nt, G * V_ROWS, LANES), lambda b, i: (b, 0, 0, 0)),
            pl.BlockSpec((1, nt, G * V_ROWS, LANES), lambda b, i: (b, 0, 1, 0)),
            pl.BlockSpec((1, 1, G, ncb, HEAD_DIM), lambda b, i: (b, 0, 0, 0, 0)),
            pl.BlockSpec((1, 1, G, HEAD_DIM, ncb), lambda b, i: (b, 1, 0, 0, 0)),
            pl.BlockSpec(mt.shape, lambda b, i: (0, 0)),
            pl.BlockSpec((HEAD_DIM, 1), lambda b, i: (0, 0)),
            pl.BlockSpec((16 * G, 1), lambda b, i: (0, 0)),
        ],
        out_specs=pl.BlockSpec((1, NSA_QT * NSA_TQ, D_NSA), lambda b, i: (b, i, 0)),
        out_shape=jax.ShapeDtypeStruct((B, T, D_NSA), BF16),
        scratch_shapes=[
            pltpu.VMEM((2, NC, 2 * HEAD_DIM, ncols), BF16),
            pltpu.VMEM((2, NC, HEAD_DIM, ncols), F32),
            pltpu.VMEM((NC, SLC_TK, ncols), F32),
            pltpu.VMEM((NC, SLC_TK, ncols), F32),
            pltpu.VMEM((NC, 1, ncols), F32),
            pltpu.VMEM((NC, 1, ncols), F32),
            pltpu.VMEM((NC, WIN_KEYS, ncols), F32),
            pltpu.VMEM((NC, 1, ncols), F32),
            pltpu.VMEM((NC, V_ROWS, ncols), F32),
        ],
        compiler_params=pltpu.CompilerParams(dimension_semantics=("parallel", "arbitrary"),
                                             vmem_limit_bytes=VMEM_LIMIT),
        name="nsa",
    )(nqt, nqt, glt, ng3, ks, kw, vt, vt, kc, vct, mt, qnw, bg)


def _out_kernel(x_ref, yr_ref, yn_ref, wr_ref, wn_ref, o_ref):
    o_ref[...] = (x_ref[...]
                  + jnp.dot(yr_ref[...], wr_ref[...], preferred_element_type=F32)
                  + jnp.dot(yn_ref[...], wn_ref[...], preferred_element_type=F32))


def _outproj(x2, yr, yn, wr, wn):
    N = x2.shape[0]
    row = lambda i: (i, 0)
    const = lambda i: (0, 0)
    return pl.pallas_call(
        _out_kernel,
        grid=(N // OUT_TM,),
        in_specs=[pl.BlockSpec((OUT_TM, D_MODEL), row), pl.BlockSpec((OUT_TM, D_RET), row),
                  pl.BlockSpec((OUT_TM, D_NSA), row), pl.BlockSpec(wr.shape, const), pl.BlockSpec(wn.shape, const)],
        out_specs=pl.BlockSpec((OUT_TM, D_MODEL), row),
        out_shape=jax.ShapeDtypeStruct((N, D_MODEL), F32),
        compiler_params=pltpu.CompilerParams(dimension_semantics=("parallel",), vmem_limit_bytes=VMEM_LIMIT),
        name="outproj",
    )(x2, yr, yn, wr, wn)


@functools.lru_cache(maxsize=None)
def _tables(T):
    half = HEAD_DIM // 2
    inv = ROPE_THETA ** (-np.arange(half, dtype=np.float64) / half)
    ang = np.arange(T, dtype=np.float64)[:, None] * inv[None, :]
    cos = np.concatenate([np.cos(ang)] * 4, axis=1).astype(np.float32)
    sin = np.concatenate([-np.sin(ang), -np.sin(ang), np.sin(ang), np.sin(ang)], axis=1).astype(np.float32)

    C = RET_CHUNK
    log_g = np.log1p(-np.exp2(-5.0 - np.arange(RET_HEADS, dtype=np.float64)))
    pos = np.arange(C, dtype=np.float64)
    diff = pos[:, None] - pos[None, :]
    decay = np.where(diff >= 0, np.exp(log_g[:, None, None] * np.maximum(diff, 0.0)), 0.0)
    zeta = np.exp(log_g[:, None] * (C - 1.0 - pos))
    xi = np.exp(log_g[:, None] * (pos + 1.0))
    g_chunk = np.exp(log_g * C)
    npair = RET_HEADS // 2

    def pair_lanes(a):
        return np.repeat(a.reshape(npair, 2, C).transpose(0, 2, 1), HEAD_DIM, axis=2).astype(np.float32)

    dec = decay.reshape(npair, 2, C, C).transpose(0, 2, 1, 3).reshape(npair, C, 2 * C).astype(np.float32)
    gch = np.repeat(g_chunk.reshape(npair, 1, 2), HEAD_DIM, axis=2).astype(np.float32)

    n_cmp = (T - CMP_BLOCK) // CMP_STRIDE + 1
    ncb = T // CMP_STRIDE
    p = np.arange(n_cmp)[:, None] * CMP_STRIDE + np.arange(CMP_BLOCK)[None, :]
    blk = p // SLC_BLOCK
    M = (blk[:, :, None] == np.arange(T // SLC_BLOCK)[None, None, :]).mean(axis=1)
    mt = np.zeros((T // SLC_BLOCK, ncb), np.float32)
    mt[:, :n_cmp] = M.T
    return cos, sin, dec, pair_lanes(zeta), pair_lanes(xi), gch, mt


def kernel(x, norm_w, w_in, ret_norm_w, q_norm_w, k_norm_cmp, k_norm_slc, k_norm_win, cmp_pos_k, cmp_w1_k, cmp_w2_k,
           cmp_pos_v, cmp_w1_v, cmp_w2_v, b_gate, w_out):
    B, T, D = x.shape
    depth = norm_w.shape[0]
    cos, sin, dec, zeta, xi, gch, mt = _tables(T)
    ncb = T // CMP_STRIDE
    half = CMP_STRIDE * HEAD_DIM
    gate_src = np.zeros((NSA_KV_HEADS, 16), np.int32)
    gate_ok = np.zeros((NSA_KV_HEADS, 16), bool)
    for g in range(NSA_KV_HEADS):
        for br in range(N_BRANCH):
            for r in range(NSA_GROUP):
                gate_src[g, br * NSA_GROUP + r] = br * NSA_HEADS + g * NSA_GROUP + r
                gate_ok[g, br * NSA_GROUP + r] = True
    gate_src = gate_src.reshape(-1)
    gate_ok = gate_ok.reshape(-1)

    x2 = x.reshape(B * T, D)
    for layer in range(depth):
        w = w_in[layer].astype(BF16)
        o_ng = 4 * D_RET + D_NSA
        o_kv = o_ng + D_NSA
        quarter = HEAD_DIM // 2
        pair_perm = np.concatenate([np.arange(quarter), HEAD_DIM + np.arange(quarter),
                                    quarter + np.arange(quarter), HEAD_DIM + quarter + np.arange(quarter)])
        qk_perm = np.concatenate([p * PAIR + pair_perm for p in range(RET_HEADS // 2)])
        w_ret = jnp.concatenate([w[:, :D_RET][:, qk_perm], w[:, D_RET:2 * D_RET][:, qk_perm],
                                 w[:, 2 * D_RET:4 * D_RET]], axis=1)
        wt_q = w[:, 4 * D_RET:o_ng].T
        w_ng = w[:, o_ng:o_kv]
        w_ckv = w[:, o_kv:o_kv + 2 * D_KV]
        w_skwk = jnp.concatenate([w[:, o_kv + 2 * D_KV:o_kv + 3 * D_KV], w[:, o_kv + 4 * D_KV:o_kv + 5 * D_KV]], axis=1)
        wt_v = jnp.concatenate([w[:, o_kv + 3 * D_KV:o_kv + 4 * D_KV], w[:, o_kv + 5 * D_KV:o_kv + 6 * D_KV]], axis=1).T
        w_gl = w[:, o_kv + 6 * D_KV:]
        wt_g = jnp.where(gate_ok[:, None], w_gl.T[gate_src], jnp.zeros((), BF16))
        bg = jnp.where(gate_ok, b_gate[layer][gate_src], 0.0).reshape(-1, 1)

        knw = jnp.stack([jnp.tile(k_norm_slc[layer], 2), jnp.tile(k_norm_win[layer], 2)])
        ret, ng, xc, ks, kw, nqt, vt, glt = _proj(x2, norm_w[layer].reshape(1, D), w_ret, w_ng, w_ckv, w_skwk,
                                                  wt_q, wt_v, wt_g, knw, B, T)

        nw_pair = ret_norm_w[layer].reshape(RET_HEADS // 2, 1, PAIR)
        y_ret = _retention(ret.reshape(B, T, 4 * D_RET), jnp.asarray(cos), jnp.asarray(sin), jnp.asarray(dec),
                           jnp.asarray(zeta), jnp.asarray(xi), jnp.asarray(gch), nw_pair, B, T)

        pos =jnp.stack([cmp_pos_k[layer], cmp_pos_v[layer]]).reshape(2, 2, half)
        w1 = jnp.stack([cmp_w1_k[layer], cmp_w1_v[layer]]).astype(BF16)
        w2 = jnp.stack([cmp_w2_k[layer], cmp_w2_v[layer]]).astype(BF16)
        w2t = jnp.swapaxes(w2, 1, 2)
        kc, vct = _compress(xc, pos, w1, w2, w2t, k_norm_cmp[layer].reshape(1, HEAD_DIM), B, ncb)

        y_nsa = _nsa(nqt, glt, ng.reshape(B, T, D_NSA), ks, kw, vt, kc, vct, jnp.asarray(mt).astype(BF16),
                     q_norm_w[layer].reshape(HEAD_DIM, 1), bg, B, T)

        wo = w_out[layer].astype(BF16)
        x2 = _outproj(x2, y_ret.reshape(B * T, D_RET), y_nsa.reshape(B * T, D_NSA), wo[:D_RET], wo[D_RET:])
    return x2.reshape(B, T, D)
```

```python
import functools

import numpy as np
import jax
import jax.numpy as jnp
from jax import lax
from jax.experimental import pallas as pl
from jax.experimental.pallas import tpu as pltpu

F32 = jnp.float32
BF16 = jnp.bfloat16

D_MODEL = 1024
HEAD_DIM = 64
HEAD_SHIFT = 6
RET_HEADS = 8
NSA_HEADS = 8
NSA_KV_HEADS = 2
NSA_GROUP = NSA_HEADS // NSA_KV_HEADS
D_RET = RET_HEADS * HEAD_DIM
D_NSA = NSA_HEADS * HEAD_DIM
D_KV = NSA_KV_HEADS * HEAD_DIM
N_BRANCH = 3
RET_CHUNK = 128
ROPE_THETA = 10000.0
CMP_BLOCK = 32
CMP_STRIDE = 16
CMP_HIDDEN = 256
SLC_BLOCK = 64
SLC_SHIFT = 6
SLC_TOPK = 16
WIN_SIZE = 512
EPS = 1e-6
NEG = -1e30
FORCE_BONUS = 1e4
QK_SCALE = HEAD_DIM ** -0.5
LOG2E = 1.4426950408889634
V_ROWS = HEAD_DIM + 16

LANES = 128
SUBLANES = 8
PAIR = 2 * HEAD_DIM
VMEM_LIMIT = 48 * 1024 * 1024

PROJ_TM = 512
RET_TC = 512
NSA_TQ = 128
NSA_QT = 2
SLC_TK = 512
WIN_KEYS = WIN_SIZE + NSA_TQ
OUT_TM = 1024

NT_DIMS = (((1,), (1,)), ((), ()))
TN_DIMS = (((0,), (0,)), ((), ()))


def _sigmoid(x):
    return 1.0 / (1.0 + jnp.exp(-x))


def _tile4(a):
    return jnp.concatenate([a, a, a, a], axis=1)


def _proj_kernel(steps_per_batch, x_ref, nw_ref, w_ret_ref, w_ng_ref, w_ckv_ref, w_skwk_ref, wt_q_ref, wt_v_ref,
                 wt_g_ref, knw_ref, ret_ref, ng_ref, xc_ref, ks_ref, kw_ref, nqt_ref, vt_ref, glt_ref, ckv_scr):
    x = x_ref[...]
    ms = jnp.mean(x * x, axis=-1, keepdims=True)
    h = (x * lax.rsqrt(ms + EPS) * nw_ref[...]).astype(BF16)
    ret_ref[...] = jnp.dot(h, w_ret_ref[...], preferred_element_type=F32)
    ng_ref[...] = jnp.dot(h, w_ng_ref[...], preferred_element_type=F32)

    ckv = jnp.dot(h, w_ckv_ref[...], preferred_element_type=F32)
    for half in range(2 * D_KV // LANES):
        ckv_scr[half] = ckv[:, half * LANES:(half + 1) * LANES]
    for l in range(CMP_STRIDE):
        for half in range(2 * D_KV // LANES):
            rows = ckv_scr[half, pl.ds(l, PROJ_TM // CMP_STRIDE, stride=CMP_STRIDE), :]
            for s in range(LANES // HEAD_DIM):
                xc_ref[0, half * (LANES // HEAD_DIM) + s, :, l * HEAD_DIM:(l + 1) * HEAD_DIM] = (
                    rows[:, s * HEAD_DIM:(s + 1) * HEAD_DIM])

    skwk = jnp.dot(h, w_skwk_ref[...], preferred_element_type=F32)
    lane = lax.broadcasted_iota(jnp.int32, (1, PAIR), 1)
    head0 = lane < HEAD_DIM
    m0 = jnp.where(head0, 1.0, 0.0)
    m1 = 1.0 - m0
    t_start = (pl.program_id(0) % steps_per_batch) * PROJ_TM
    tok = t_start + lax.broadcasted_iota(jnp.int32, (PROJ_TM, PAIR), 0)
    col = lax.broadcasted_iota(jnp.int32, (PROJ_TM, PAIR), 1)
    indicator = jnp.where((tok >> SLC_SHIFT) == col - HEAD_DIM, 1.0, 0.0)

    def pair_normed(t, w):
        t2 = t * t
        ms0 = jnp.sum(t2 * m0, axis=-1, keepdims=True)
        ms1 = jnp.sum(t2 * m1, axis=-1, keepdims=True)
        return t * lax.rsqrt(jnp.where(head0, ms0, ms1) * (1.0 / HEAD_DIM) + EPS) * w

    ns = pair_normed(skwk[:, :PAIR], knw_ref[0:1, :])
    nwin = pair_normed(skwk[:, PAIR:], knw_ref[1:2, :])
    for g in range(NSA_KV_HEADS):
        s_g = ns if g == 0 else pltpu.roll(ns, HEAD_DIM, 1)
        w_g = nwin if g == 0 else pltpu.roll(nwin, HEAD_DIM, 1)
        ks_ref[0, g] = jnp.where(head0, s_g, indicator).astype(BF16)
        kw_ref[0, g] = jnp.where(head0, w_g, 0.0).astype(BF16)

    qt =lax.dot_general(wt_q_ref[...], h, NT_DIMS, preferred_element_type=F32)
    vt = lax.dot_general(wt_v_ref[...], h, NT_DIMS, preferred_element_type=F32)
    gt = lax.dot_general(wt_g_ref[...], h, NT_DIMS, preferred_element_type=F32)
    pad_row = lax.broadcasted_iota(jnp.int32, (V_ROWS - HEAD_DIM, LANES), 0)
    ones_pad = jnp.where(pad_row == 0, 1.0, 0.0).astype(BF16)
    for j in range(PROJ_TM // LANES):
        sl = slice(j * LANES, (j + 1) * LANES)
        nqt_ref[0, j] = qt[:, sl]
        for blk in range(2 * NSA_KV_HEADS):
            vt_ref[0, j, blk * V_ROWS:blk * V_ROWS + HEAD_DIM, :] = (
                vt[blk * HEAD_DIM:(blk + 1) * HEAD_DIM, sl].astype(BF16))
            vt_ref[0, j, blk * V_ROWS + HEAD_DIM:(blk + 1) * V_ROWS, :] = ones_pad
        glt_ref[0, j] = gt[:, sl]


def _proj(x2, nw, w_ret, w_ng, w_ckv, w_skwk, wt_q, wt_v, wt_g, knw, B, T):
    N = B * T
    tpb = T // PROJ_TM
    sub = PROJ_TM // LANES
    nt = T // LANES
    const = lambda i: (0, 0)
    row = lambda i: (i, 0)
    trn = lambda i: (i // tpb, i % tpb, 0, 0)
    tokm = lambda i: (i // tpb, 0, i % tpb, 0)
    return pl.pallas_call(
        functools.partial(_proj_kernel, tpb),
        grid=(N // PROJ_TM,),
        in_specs=[
            pl.BlockSpec((PROJ_TM, D_MODEL), row),
            pl.BlockSpec((1, D_MODEL), const),
            pl.BlockSpec(w_ret.shape, const),
            pl.BlockSpec(w_ng.shape, const),
            pl.BlockSpec(w_ckv.shape, const),
            pl.BlockSpec(w_skwk.shape, const),
            pl.BlockSpec(wt_q.shape, const),
            pl.BlockSpec(wt_v.shape, const),
            pl.BlockSpec(wt_g.shape, const),
            pl.BlockSpec(knw.shape, const),
        ],
        out_specs=[
            pl.BlockSpec((PROJ_TM, 4 * D_RET), row),
            pl.BlockSpec((PROJ_TM, D_NSA), row),
            pl.BlockSpec((1, 2 * NSA_KV_HEADS, PROJ_TM // CMP_STRIDE, CMP_STRIDE * HEAD_DIM), tokm),
            pl.BlockSpec((1, NSA_KV_HEADS, PROJ_TM, PAIR), tokm),
            pl.BlockSpec((1, NSA_KV_HEADS, PROJ_TM, PAIR), tokm),
            pl.BlockSpec((1, sub, D_NSA, LANES), trn),
            pl.BlockSpec((1, sub, 2 * NSA_KV_HEADS * V_ROWS, LANES), trn),
            pl.BlockSpec((1, sub, 32, LANES), trn),
        ],
        out_shape=[
            jax.ShapeDtypeStruct((N, 4 * D_RET), F32),
            jax.ShapeDtypeStruct((N, D_NSA), F32),
            jax.ShapeDtypeStruct((B, 2 * NSA_KV_HEADS, T // CMP_STRIDE, CMP_STRIDE * HEAD_DIM), F32),
            jax.ShapeDtypeStruct((B, NSA_KV_HEADS, T, PAIR), BF16),
            jax.ShapeDtypeStruct((B, NSA_KV_HEADS, T, PAIR), BF16),
            jax.ShapeDtypeStruct((B, nt, D_NSA, LANES), F32),
            jax.ShapeDtypeStruct((B, nt, 2 * NSA_KV_HEADS * V_ROWS, LANES), BF16),
            jax.ShapeDtypeStruct((B, nt, 32, LANES), F32),
        ],
        scratch_shapes=[pltpu.VMEM((2 * D_KV // LANES, PROJ_TM, LANES), F32)],
        compiler_params=pltpu.CompilerParams(dimension_semantics=("parallel",), vmem_limit_bytes=VMEM_LIMIT),
        name="proj",
    )(x2, nw, w_ret, w_ng, w_ckv, w_skwk, wt_q, wt_v, wt_g, knw)


def _ret_kernel(q_ref, k_ref, v_ref, g_ref, cos_ref, sin_ref, dec_ref, zeta_ref, xi_ref, gch_ref, nw_ref,
                o_ref, state_ref):
    @pl.when(pl.program_id(1) == 0)
    def _():
        state_ref[...] = jnp.zeros_like(state_ref)

    lane = lax.broadcasted_iota(jnp.int32, (1, PAIR), 1)
    q_head = (lane >> (HEAD_SHIFT - 1)) & 1
    v_head = lane >> HEAD_SHIFT
    q_mask = [jnp.where(q_head == h, 1.0, 0.0).astype(BF16) for h in (0, 1)]
    v_mask = [jnp.where(v_head == h, 1.0, 0.0).astype(BF16) for h in (0, 1)]
    row_qh = (lax.broadcasted_iota(jnp.int32, (PAIR, PAIR), 0) >> (HEAD_SHIFT - 1)) & 1
    row_vh = lax.broadcasted_iota(jnp.int32, (PAIR, PAIR), 0) >> HEAD_SHIFT
    col_vh = lax.broadcasted_iota(jnp.int32, (PAIR, PAIR), 1) >> HEAD_SHIFT
    same_head_kv = jnp.where(row_qh == col_vh, 1.0, 0.0)
    head_mean = jnp.where(row_vh == col_vh, 1.0 / HEAD_DIM, 0.0).astype(BF16)

    pairs = range(RET_HEADS // 2)
    mean2 = jnp.concatenate([head_mean, head_mean], axis=0)
    for c in range(RET_TC // RET_CHUNK):
        sl = pl.ds(c * RET_CHUNK, RET_CHUNK)
        cos = cos_ref[sl, :]
        sin = sin_ref[sl, :]
        cols = [slice(p * PAIR, (p + 1) * PAIR) for p in pairs]
        qb, kb, vb, vzb = [], [], [], []
        for p in pairs:
            q = q_ref[0, sl, cols[p]]
            k = k_ref[0, sl, cols[p]]
            v = v_ref[0, sl, cols[p]]
            qb.append((q * cos + pltpu.roll(q, HEAD_DIM, 1) * sin).astype(BF16))
            kb.append(((k * cos + pltpu.roll(k, HEAD_DIM, 1) * sin) * QK_SCALE).astype(BF16))
            vb.append(v.astype(BF16))
            vzb.append((v * zeta_ref[p]).astype(BF16))
        states = [state_ref[p] for p in pairs]
        kk = [jnp.concatenate([kb[p] * q_mask[h] for h in (0, 1)], axis=0) for p in pairs]
        s = [lax.dot_general(qb[p], kk[p], NT_DIMS, preferred_element_type=F32) for p in pairs]
        o_cross = [jnp.dot(qb[p], states[p].astype(BF16), preferred_element_type=F32) for p in pairs]
        kv = [lax.dot_general(kb[p], vzb[p], TN_DIMS, preferred_element_type=F32) for p in pairs]
        sb = [(s[p] * dec_ref[p]).astype(BF16) for p in pairs]
        vv = [jnp.concatenate([vb[p] * v_mask[h] for h in (0, 1)], axis=0) for p in pairs]
        o = [jnp.dot(sb[p], vv[p], preferred_element_type=F32) + o_cross[p] * xi_ref[p] for p in pairs]
        for p in pairs:
            state_ref[p] = states[p] * gch_ref[p] + kv[p] * same_head_kv
        o2 = [o[p] * o[p] for p in pairs]
        o2_hi = [o2[p].astype(BF16) for p in pairs]
        o2_hl = [jnp.concatenate([o2_hi[p], (o2[p] - o2_hi[p].astype(F32)).astype(BF16)], axis=1) for p in pairs]
        ms = [jnp.dot(o2_hl[p], mean2, preferred_element_type=F32) for p in pairs]
        for p in pairs:
            g = g_ref[0, sl, cols[p]]
            y = o[p] * lax.rsqrt(ms[p] + EPS) * nw_ref[p]
            o_ref[0, sl, cols[p]] = (y * (g * _sigmoid(g))).astype(BF16)


def _retention(ret3, cos, sin, dec, zeta, xi, gch, nw, B, T):
    npair = RET_HEADS // 2
    blk = lambda col: pl.BlockSpec((1, RET_TC, D_RET), lambda b, i: (b, i, col))
    tab = pl.BlockSpec((RET_TC, PAIR), lambda b, i: (i, 0))
    whole = lambda a: pl.BlockSpec(a.shape, lambda b, i: (0,) * a.ndim)
    return pl.pallas_call(
        _ret_kernel,
        grid=(B, T // RET_TC),
        in_specs=[blk(0), blk(1), blk(2), blk(3), tab, tab, whole(dec), whole(zeta), whole(xi), whole(gch), whole(nw)],
        out_specs=pl.BlockSpec((1, RET_TC, D_RET), lambda b, i: (b, i, 0)),
        out_shape=jax.ShapeDtypeStruct((B, T, D_RET), BF16),
        scratch_shapes=[pltpu.VMEM((npair, PAIR, PAIR), F32)],
        compiler_params=pltpu.CompilerParams(dimension_semantics=("parallel", "arbitrary"),
                                             vmem_limit_bytes=VMEM_LIMIT),
        name="retention",
    )(ret3, ret3, ret3, ret3, cos, sin, dec, zeta, xi, gch, nw)


def _cmp_kernel(x_ref, pos_ref, w1_ref, w2_ref, w2t_ref, knw_ref, o_ref, ot_ref):
    is_key = pl.program_id(1) == 0
    half = CMP_STRIDE * HEAD_DIM
    for g in range(NSA_KV_HEADS):
        x = x_ref[0, g]
        a = jnp.dot((x + pos_ref[0, 0:1, :]).astype(BF16), w1_ref[0, :half, :], preferred_element_type=F32)
        b = jnp.dot((x + pos_ref[0, 1:2, :]).astype(BF16), w1_ref[0, half:, :], preferred_element_type=F32)
        hid = a + pltpu.roll(b, b.shape[0] - 1, 0)
        hid = (hid * _sigmoid(hid)).astype(BF16)
        out = jnp.dot(hid, w2_ref[0], preferred_element_type=F32)
        ms = jnp.mean(out * out, axis=-1, keepdims=True)
        normed = out * lax.rsqrt(ms + EPS) * knw_ref[...]
        o_ref[0, 0, g] = jnp.where(is_key, normed, out)
        ot_ref[0, 0, g] = lax.dot_general(w2t_ref[0], hid, NT_DIMS, preferred_element_type=F32)


def _compress(xc, pos, w1, w2, w2t, knw, B, ncb):
    return pl.pallas_call(
        _cmp_kernel,
        grid=(B, 2),
        in_specs=[
            pl.BlockSpec((1, NSA_KV_HEADS, ncb, CMP_STRIDE * HEAD_DIM), lambda b, s: (b, s, 0, 0)),
            pl.BlockSpec((1, 2, CMP_STRIDE * HEAD_DIM), lambda b, s: (s, 0, 0)),
            pl.BlockSpec((1, CMP_BLOCK * HEAD_DIM, CMP_HIDDEN), lambda b, s: (s, 0, 0)),
            pl.BlockSpec((1, CMP_HIDDEN, HEAD_DIM), lambda b, s: (s, 0, 0)),
            pl.BlockSpec((1, HEAD_DIM, CMP_HIDDEN), lambda b, s: (s, 0, 0)),
            pl.BlockSpec((1, HEAD_DIM), lambda b, s: (0, 0)),
        ],
        out_specs=[
            pl.BlockSpec((1, 1, NSA_KV_HEADS, ncb, HEAD_DIM), lambda b, s: (b, s, 0, 0, 0)),
            pl.BlockSpec((1, 1, NSA_KV_HEADS, HEAD_DIM, ncb), lambda b, s: (b, s, 0, 0, 0)),
        ],
        out_shape=[
            jax.ShapeDtypeStruct((B, 2, NSA_KV_HEADS, ncb, HEAD_DIM), F32),
            jax.ShapeDtypeStruct((B, 2, NSA_KV_HEADS, HEAD_DIM, ncb), F32),
        ],
        compiler_params=pltpu.CompilerParams(dimension_semantics=("parallel", "parallel"),
                                             vmem_limit_bytes=VMEM_LIMIT),
        name="compress",
    )(xc, pos, w1, w2, w2t, knw)


def _nsa_kernel(qt_ref, qtn_ref, glt_ref, ng_ref, ks_ref, kw_ref, vst_ref, vwt_ref, kc_ref, vct_ref, mt_ref, qnw_ref,
                bg_ref, o_ref, qp_ref, oc_ref, sa_ref, sb_ref, mxa_ref, mxb_ref, sw_ref, m_ref, acc_ref):
    qi = pl.program_id(1)
    t0 = qi * (NSA_QT * NSA_TQ)
    tiles = range(NSA_QT)
    chains = [(qt, g) for qt in tiles for g in range(NSA_KV_HEADS)]
    gq = NSA_GROUP * HEAD_DIM
    gg = 16
    ncb = kc_ref.shape[3]
    n_slc = mt_ref.shape[0]

    def select_masks(base_t0):
        n_idx = lax.broadcasted_iota(jnp.int32, (ncb, NSA_TQ), 0)
        jb = lax.broadcasted_iota(jnp.int32, (n_slc, NSA_TQ), 0)
        out = []
        for qt in tiles:
            tok = base_t0 + qt * NSA_TQ
            tok_c = tok + lax.broadcasted_iota(jnp.int32, (ncb, NSA_TQ), 1)
            cbias = _tile4(jnp.where((n_idx * CMP_STRIDE + (CMP_BLOCK - 1)) <= tok_c, 0.0, NEG))
            tok_row = tok + lax.broadcasted_iota(jnp.int32, (1, NSA_TQ), 1)
            has_block = _tile4(jnp.where(tok_row >= CMP_BLOCK - 1, 1.0, 0.0))
            tok_s = tok + lax.broadcasted_iota(jnp.int32, (n_slc, NSA_TQ), 1)
            valid_s = jb * SLC_BLOCK <= tok_s
            force = (jb == (tok_s >> SLC_SHIFT)) | (jb == 0)
            out.append((cbias, has_block, valid_s, force))
        return out

    def select_scores(src_ref, slot, masks, c):
        qt, g = chains[c]
        cols = []
        for r in range(NSA_GROUP):
            q = src_ref[0, qt, g * gq + r * HEAD_DIM:g * gq + (r + 1) * HEAD_DIM, :]
            ms = jnp.mean(q * q, axis=0, keepdims=True)
            cols.append(q * lax.rsqrt(ms + EPS) * qnw_ref[...] * (QK_SCALE * LOG2E))
        qs = jnp.concatenate(cols, axis=1).astype(BF16)
        qp_ref[slot, c, 0:HEAD_DIM, :] = qs
        return jnp.dot(kc_ref[0, 0, g].astype(BF16), qs, preferred_element_type=F32) + masks[qt][0]

    def select_probs(sc, slot, masks, c):
        qt, g = chains[c]
        _, has_block, valid_s, force = masks[qt]
        mc = jnp.max(sc, axis=0, keepdims=True)
        ec = jnp.exp2(sc - mc)
        lc = jnp.sum(ec, axis=0, keepdims=True)
        p = ec * (has_block / lc)
        oc_ref[slot, c] = jnp.dot(vct_ref[0, 0, g].astype(BF16), p.astype(BF16), preferred_element_type=F32)
        ps = p[:, 0:NSA_TQ]
        for r in range(1, NSA_GROUP):
            ps = ps + p[:, r * NSA_TQ:(r + 1) * NSA_TQ]
        ps_hi = ps.astype(BF16)
        ps_lo = (ps - ps_hi.astype(F32)).astype(BF16)
        imp = (jnp.dot(mt_ref[...], ps_hi, preferred_element_type=F32)
               + jnp.dot(mt_ref[...], ps_lo, preferred_element_type=F32))
        return jnp.where(valid_s, jnp.where(force, imp + FORCE_BONUS, imp), NEG)

    def select_rank(scores, valid_s, slot, n_live):
        if n_live <= SLC_TOPK:
            for c, (qt, g) in enumerate(chains):
                qp_ref[slot, c, HEAD_DIM:2 * HEAD_DIM, :] = _tile4(jnp.where(valid_s[qt], 0.0, NEG).astype(BF16))
            return
        sub = lax.broadcasted_iota(jnp.int32, (SUBLANES, NSA_TQ), 0)
        for c, (qt, g) in enumerate(chains):
            score = scores[c]
            blocks = [score[v * SUBLANES:(v + 1) * SUBLANES, :] for v in range(n_live // SUBLANES)]
            ranks = [jnp.zeros((SUBLANES, NSA_TQ), F32) for _ in blocks]
            for i in range(n_live):
                row = score[i:i + 1, :]
                for v, blk in enumerate(blocks):
                    if v * SUBLANES > i:
                        beats = row >= blk
                    elif (v + 1) * SUBLANES <= i:
                        beats = row > blk
                    else:
                        beats = (row > blk) | ((row >= blk) & (sub > i - v * SUBLANES))
                    ranks[v] = ranks[v] + jnp.where(beats, 1.0, 0.0)
            dead = [jnp.full((SUBLANES, NSA_TQ), float(n_slc), F32)] * ((n_slc - n_live) // SUBLANES)
            rank = jnp.concatenate(ranks + dead, axis=0)
            sel = (rank < float(SLC_TOPK)) & valid_s[qt]
            qp_ref[slot, c, HEAD_DIM:2 * HEAD_DIM, :] = _tile4(jnp.where(sel, 0.0, NEG).astype(BF16))

    step_tokens = NSA_QT * NSA_TQ

    def live_blocks(step):
        return ((step + 1) * step_tokens - 1) // SLC_BLOCK + 1

    all_chains = list(range(len(chains)))

    @pl.when(qi == 0)
    def _():
        masks0 = select_masks(0)
        scores0 = [select_probs(select_scores(qt_ref, 0, masks0, c), 0, masks0, c) for c in all_chains]
        select_rank(scores0, [m[2] for m in masks0], 0, live_blocks(0))

    cur = qi % 2
    nxt = 1 - cur
    t_next = t0 + NSA_QT * NSA_TQ

    kt0, ks0, wbias = [], [], []
    c_minus_r = (lax.broadcasted_iota(jnp.int32, (WIN_KEYS, NSA_TQ), 1)
                 - lax.broadcasted_iota(jnp.int32, (WIN_KEYS, NSA_TQ), 0))
    for qt in tiles:
        kt0.append(jnp.maximum(qi * NSA_QT + qt - WIN_SIZE // NSA_TQ, 0))
        ks0.append(pl.multiple_of(kt0[qt] * NSA_TQ, NSA_TQ))
        delta = (t0 + qt * NSA_TQ - ks0[qt]) + c_minus_r
        in_window = lax.bitcast_convert_type(delta, jnp.uint32) < WIN_SIZE
        wbias.append(_tile4(jnp.where(in_window, 0.0, NEG)))

    def win_scores(c):
        qt, g = chains[c]
        sw = jnp.dot(kw_ref[0, g, pl.ds(ks0[qt], WIN_KEYS), 0:HEAD_DIM], qp_ref[cur, c, 0:HEAD_DIM, :],
                     preferred_element_type=F32) + wbias[qt]
        sw_ref[c] = sw
        return jnp.max(sw, axis=0, keepdims=True)

    def win_attend(c, mw):
        qt, g = chains[c]
        ewb = jnp.exp2(sw_ref[c] - mw).astype(BF16)
        vwt = jnp.concatenate([vwt_ref[0, kt0[qt] + j, g * V_ROWS:(g + 1) * V_ROWS, :]
                               for j in range(WIN_KEYS // LANES)], axis=1)
        ow_aug = jnp.dot(vwt, ewb, preferred_element_type=F32)
        return ow_aug[0:HEAD_DIM, :] * (1.0 / ow_aug[HEAD_DIM:HEAD_DIM + 1, :])

    vt_per_tile = SLC_TK // LANES

    def slc_scores(j, dst_ref, mx_ref, which=all_chains):
        kst = pl.multiple_of(j * SLC_TK, SLC_TK)
        for c in which:
            qt, g = chains[c]
            s = jnp.dot(ks_ref[0, g, pl.ds(kst, SLC_TK), :], qp_ref[cur, c], preferred_element_type=F32)
            dst_ref[c] = s
            mx_ref[c] = jnp.max(s, axis=0, keepdims=True)

    def slc_update(j, src_ref, mx_ref, causal, which=all_chains, rows=SLC_TK):
        if causal:
            r_s = lax.broadcasted_iota(jnp.int32, (rows, NSA_TQ), 0)
            c_s = lax.broadcasted_iota(jnp.int32, (rows, NSA_TQ), 1)
            causal_bias = [_tile4(jnp.where(j * SLC_TK + r_s <= t0 + qt * NSA_TQ + c_s, 0.0, NEG)) for qt in tiles]
        for c in which:
            qt, g = chains[c]
            s = src_ref[c, 0:rows, :]
            if causal:
                s = s + causal_bias[qt]
                tile_max = jnp.max(s, axis=0, keepdims=True)
            else:
                tile_max = mx_ref[c]
            m_old = m_ref[c]
            m_new = jnp.maximum(m_old, tile_max)
            alpha = jnp.exp2(m_old - m_new)
            eb = jnp.exp2(s - m_new).astype(BF16)
            vt = jnp.concatenate([vst_ref[0, j * vt_per_tile + jj, g * V_ROWS:(g + 1) * V_ROWS, :]
                                  for jj in range(rows // LANES)], axis=1)
            acc_ref[c] = alpha * acc_ref[c] + jnp.dot(vt, eb, preferred_element_type=F32)
            m_ref[c] = m_new

    masks = select_masks(t_next)

    def matmul_stage(c):
        mw = win_scores(c)
        sc = select_scores(qtn_ref, nxt, masks, c)
        slc_scores(0, sa_ref, mxa_ref, [c])
        return sc, mw

    sel_scores, ow_t = [], []
    staged = matmul_stage(0)
    for c in all_chains:
        staged_next = matmul_stage(c + 1) if c + 1 < len(chains) else None
        sel_scores.append(select_probs(staged[0], nxt, masks, c))
        ow_t.append(win_attend(c, staged[1]))
        staged = staged_next
    sel_valid = [m[2] for m in masks]

    need = live_blocks(qi + 1)
    bounds = list(range(SLC_TOPK, n_slc + 1, SLC_TOPK))
    for lo, hi in zip([0] + bounds[:-1], bounds):
        in_range = (need > lo) if hi == bounds[-1] else ((need > lo) & (need <= hi))
        pl.when(in_range)(functools.partial(select_rank, sel_scores, sel_valid, nxt, hi))

    m_ref[...] = jnp.full(m_ref.shape, NEG, F32)
    acc_ref[...] = jnp.zeros(acc_ref.shape, F32)
    n_full = t0 // SLC_TK

    def pair(jj, carry):
        j = 2 * jj
        for c in all_chains:
            slc_scores(j + 1, sb_ref, mxb_ref, [c])
            slc_update(j, sa_ref, mxa_ref, False, [c])
        for c in all_chains:
            slc_scores(j + 2, sa_ref, mxa_ref, [c])
            slc_update(j + 1, sb_ref, mxb_ref, False, [c])
        return carry

    lax.fori_loop(0, n_full // 2, pair, 0)

    visible = t0 - n_full * SLC_TK + step_tokens

    def tail(odd, rows):
        if odd:
            for c in all_chains:
                slc_scores(n_full, sb_ref, mxb_ref, [c])
                slc_update(n_full - 1, sa_ref, mxa_ref, False, [c])
            slc_update(n_full, sb_ref, mxb_ref, True, rows=rows)
        else:
            slc_update(n_full, sa_ref, mxa_ref, True, rows=rows)

    for rows in range(step_tokens, SLC_TK + 1, step_tokens):
        for odd in (False, True):
            parity = (n_full % 2 == 1) if odd else (n_full % 2 == 0)
            pl.when((visible == rows) & parity)(functools.partial(tail, odd, rows))

    for qt in tiles:
        gates = _sigmoid(glt_ref[0, qt] + bg_ref[...])
        outs = []
        for g in range(NSA_KV_HEADS):
            c = qt * NSA_KV_HEADS + g
            os_t = acc_ref[c, 0:HEAD_DIM, :] * (1.0 / acc_ref[c, HEAD_DIM:HEAD_DIM + 1, :])
            oc_t = oc_ref[cur, c]
            for r in range(NSA_GROUP):
                sl = slice(r * NSA_TQ, (r + 1) * NSA_TQ)
                row = g * gg + r
                outs.append(gates[row:row + 1, :] * oc_t[:, sl]
                            + gates[row + NSA_GROUP:row + NSA_GROUP + 1, :] * os_t[:, sl]
                            + gates[row + 2 * NSA_GROUP:row + 2 * NSA_GROUP + 1, :] * ow_t[c][:, sl])
        o_tok = jnp.concatenate(outs, axis=0).T
        rows = pl.ds(qt * NSA_TQ, NSA_TQ)
        ng = ng_ref[0, rows, :]
        o_ref[0, rows, :] = (o_tok * (ng * _sigmoid(ng))).astype(BF16)


def _nsa(nqt, glt, ng3, ks, kw, vt, kc, vct, mt, qnw, bg, B, T):
    nt = T // LANES
    ncb = kc.shape[3]
    G = NSA_KV_HEADS
    NC = NSA_QT * G
    ncols = NSA_GROUP * NSA_TQ
    steps = T // (NSA_QT * NSA_TQ)
    return pl.pallas_call(
        _nsa_kernel,
        grid=(B, steps),
        in_specs=[
            pl.BlockSpec((1, NSA_QT, D_NSA, LANES), lambda b, i: (b, 0, 0, 0)),
            pl.BlockSpec((1, NSA_QT, D_NSA, LANES), lambda b, i: (b, jnp.minimum(i + 1, steps - 1), 0, 0)),
            pl.BlockSpec((1, NSA_QT, 16 * G, LANES), lambda b, i: (b, i, 0, 0)),
            pl.BlockSpec((1, NSA_QT * NSA_TQ, D_NSA), lambda b, i: (b, i, 0)),
            pl.BlockSpec((1, G, T, PAIR), lambda b, i: (b, 0, 0, 0)),
            pl.BlockSpec((1, G, T, PAIR), lambda b, i: (b, 0, 0, 0)),
            pl.BlockSpec((1, nt, G * V_ROWS, LANES), lambda b, i: (b, 0, 0, 0)),
            pl.BlockSpec((1, nt, G * V_ROWS, LANES), lambda b, i: (b, 0, 1, 0)),
            pl.BlockSpec((1, 1, G, ncb, HEAD_DIM), lambda b, i: (b, 0, 0, 0, 0)),
            pl.BlockSpec((1, 1, G, HEAD_DIM, ncb), lambda b, i: (b, 1, 0, 0, 0)),
            pl.BlockSpec(mt.shape, lambda b, i: (0, 0)),
            pl.BlockSpec((HEAD_DIM, 1), lambda b, i: (0, 0)),
            pl.BlockSpec((16 * G, 1), lambda b, i: (0, 0)),
        ],
        out_specs=pl.BlockSpec((1, NSA_QT * NSA_TQ, D_NSA), lambda b, i: (b, i, 0)),
        out_shape=jax.ShapeDtypeStruct((B, T, D_NSA), BF16),
        scratch_shapes=[
            pltpu.VMEM((2, NC, 2 * HEAD_DIM, ncols), BF16),
            pltpu.VMEM((2, NC, HEAD_DIM, ncols), F32),
            pltpu.VMEM((NC, SLC_TK, ncols), F32),
            pltpu.VMEM((NC, SLC_TK, ncols), F32),
            pltpu.VMEM((NC, 1, ncols), F32),
            pltpu.VMEM((NC, 1, ncols), F32),
            pltpu.VMEM((NC, WIN_KEYS, ncols), F32),
            pltpu.VMEM((NC, 1, ncols), F32),
            pltpu.VMEM((NC, V_ROWS, ncols), F32),
        ],
        compiler_params=pltpu.CompilerParams(dimension_semantics=("parallel", "arbitrary"),
                                             vmem_limit_bytes=VMEM_LIMIT),
        name="nsa",
    )(nqt, nqt, glt, ng3, ks, kw, vt, vt, kc, vct, mt, qnw, bg)


def _out_kernel(x_ref, yr_ref, yn_ref, wr_ref, wn_ref, o_ref):
    o_ref[...] = (x_ref[...]
                  + jnp.dot(yr_ref[...], wr_ref[...], preferred_element_type=F32)
                  + jnp.dot(yn_ref[...], wn_ref[...], preferred_element_type=F32))


def _outproj(x2, yr, yn, wr, wn):
    N = x2.shape[0]
    row = lambda i: (i, 0)
    const = lambda i: (0, 0)
    return pl.pallas_call(
        _out_kernel,
        grid=(N // OUT_TM,),
        in_specs=[pl.BlockSpec((OUT_TM, D_MODEL), row), pl.BlockSpec((OUT_TM, D_RET), row),
                  pl.BlockSpec((OUT_TM, D_NSA), row), pl.BlockSpec(wr.shape, const), pl.BlockSpec(wn.shape, const)],
        out_specs=pl.BlockSpec((OUT_TM, D_MODEL), row),
        out_shape=jax.ShapeDtypeStruct((N, D_MODEL), F32),
        compiler_params=pltpu.CompilerParams(dimension_semantics=("parallel",), vmem_limit_bytes=VMEM_LIMIT),
        name="outproj",
    )(x2, yr, yn, wr, wn)


@functools.lru_cache(maxsize=None)
def _tables(T):
    half = HEAD_DIM // 2
    inv = ROPE_THETA ** (-np.arange(half, dtype=np.float64) / half)
    ang = np.arange(T, dtype=np.float64)[:, None] * inv[None, :]
    cos = np.concatenate([np.cos(ang)] * 4, axis=1).astype(np.float32)
    sin = np.concatenate([-np.sin(ang), -np.sin(ang), np.sin(ang), np.sin(ang)], axis=1).astype(np.float32)

    C = RET_CHUNK
    log_g = np.log1p(-np.exp2(-5.0 - np.arange(RET_HEADS, dtype=np.float64)))
    pos = np.arange(C, dtype=np.float64)
    diff = pos[:, None] - pos[None, :]
    decay = np.where(diff >= 0, np.exp(log_g[:, None, None] * np.maximum(diff, 0.0)), 0.0)
    zeta = np.exp(log_g[:, None] * (C - 1.0 - pos))
    xi = np.exp(log_g[:, None] * (pos + 1.0))
    g_chunk = np.exp(log_g * C)
    npair = RET_HEADS // 2

    def pair_lanes(a):
        return np.repeat(a.reshape(npair, 2, C).transpose(0, 2, 1), HEAD_DIM, axis=2).astype(np.float32)

    dec = decay.reshape(npair, 2, C, C).transpose(0, 2, 1, 3).reshape(npair, C, 2 * C).astype(np.float32)
    gch = np.repeat(g_chunk.reshape(npair, 1, 2), HEAD_DIM, axis=2).astype(np.float32)

    n_cmp = (T - CMP_BLOCK) // CMP_STRIDE + 1
    ncb = T // CMP_STRIDE
    p = np.arange(n_cmp)[:, None] * CMP_STRIDE + np.arange(CMP_BLOCK)[None, :]
    blk = p // SLC_BLOCK
    M = (blk[:, :, None] == np.arange(T // SLC_BLOCK)[None, None, :]).mean(axis=1)
    mt = np.zeros((T // SLC_BLOCK, ncb), np.float32)
    mt[:, :n_cmp] = M.T
    return cos, sin, dec, pair_lanes(zeta), pair_lanes(xi), gch, mt


def kernel(x, norm_w, w_in, ret_norm_w, q_norm_w, k_norm_cmp, k_norm_slc, k_norm_win, cmp_pos_k, cmp_w1_k, cmp_w2_k,
           cmp_pos_v, cmp_w1_v, cmp_w2_v, b_gate, w_out):
    B, T, D = x.shape
    depth = norm_w.shape[0]
    cos, sin, dec, zeta, xi, gch, mt = _tables(T)
    ncb = T // CMP_STRIDE
    half = CMP_STRIDE * HEAD_DIM
    gate_src = np.zeros((NSA_KV_HEADS, 16), np.int32)
    gate_ok = np.zeros((NSA_KV_HEADS, 16), bool)
    for g in range(NSA_KV_HEADS):
        for br in range(N_BRANCH):
            for r in range(NSA_GROUP):
                gate_src[g, br * NSA_GROUP + r] = br * NSA_HEADS + g * NSA_GROUP + r
                gate_ok[g, br * NSA_GROUP + r] = True
    gate_src = gate_src.reshape(-1)
    gate_ok = gate_ok.reshape(-1)

    x2 = x.reshape(B * T, D)
    for layer in range(depth):
        w = w_in[layer].astype(BF16)
        o_ng = 4 * D_RET + D_NSA
        o_kv = o_ng + D_NSA
        quarter = HEAD_DIM // 2
        pair_perm = np.concatenate([np.arange(quarter), HEAD_DIM + np.arange(quarter),
                                    quarter + np.arange(quarter), HEAD_DIM + quarter + np.arange(quarter)])
        qk_perm = np.concatenate([p * PAIR + pair_perm for p in range(RET_HEADS // 2)])
        w_ret = jnp.concatenate([w[:, :D_RET][:, qk_perm], w[:, D_RET:2 * D_RET][:, qk_perm],
                                 w[:, 2 * D_RET:4 * D_RET]], axis=1)
        wt_q = w[:, 4 * D_RET:o_ng].T
        w_ng = w[:, o_ng:o_kv]
        w_ckv = w[:, o_kv:o_kv + 2 * D_KV]
        w_skwk = jnp.concatenate([w[:, o_kv + 2 * D_KV:o_kv + 3 * D_KV], w[:, o_kv + 4 * D_KV:o_kv + 5 * D_KV]], axis=1)
        wt_v = jnp.concatenate([w[:, o_kv + 3 * D_KV:o_kv + 4 * D_KV], w[:, o_kv + 5 * D_KV:o_kv + 6 * D_KV]], axis=1).T
        w_gl = w[:, o_kv + 6 * D_KV:]
        wt_g = jnp.where(gate_ok[:, None], w_gl.T[gate_src], jnp.zeros((), BF16))
        bg = jnp.where(gate_ok, b_gate[layer][gate_src], 0.0).reshape(-1, 1)

        knw = jnp.stack([jnp.tile(k_norm_slc[layer], 2), jnp.tile(k_norm_win[layer], 2)])
        ret, ng, xc, ks, kw, nqt, vt, glt = _proj(x2, norm_w[layer].reshape(1, D), w_ret, w_ng, w_ckv, w_skwk,
                                                  wt_q, wt_v, wt_g, knw, B, T)

        nw_pair = ret_norm_w[layer].reshape(RET_HEADS // 2, 1, PAIR)
        y_ret = _retention(ret.reshape(B, T, 4 * D_RET), jnp.asarray(cos), jnp.asarray(sin), jnp.asarray(dec),
                           jnp.asarray(zeta), jnp.asarray(xi), jnp.asarray(gch), nw_pair, B, T)

        pos =jnp.stack([cmp_pos_k[layer], cmp_pos_v[layer]]).reshape(2, 2, half)
        w1 = jnp.stack([cmp_w1_k[layer], cmp_w1_v[layer]]).astype(BF16)
        w2 = jnp.stack([cmp_w2_k[layer], cmp_w2_v[layer]]).astype(BF16)
        w2t = jnp.swapaxes(w2, 1, 2)
        kc, vct = _compress(xc, pos, w1, w2, w2t, k_norm_cmp[layer].reshape(1, HEAD_DIM), B, ncb)

        y_nsa = _nsa(nqt, glt, ng.reshape(B, T, D_NSA), ks, kw, vt, kc, vct, jnp.asarray(mt).astype(BF16),
                     q_norm_w[layer].reshape(HEAD_DIM, 1), bg, B, T)

        wo = w_out[layer].astype(BF16)
        x2 = _outproj(x2, y_ret.reshape(B * T, D_RET), y_nsa.reshape(B * T, D_NSA), wo[:D_RET], wo[D_RET:])
    return x2.reshape(B, T, D)
```

```python
import functools

import numpy as np
import jax
import jax.numpy as jnp
from jax import lax
from jax.experimental import pallas as pl
from jax.experimental.pallas import tpu as pltpu

F32 = jnp.float32
BF16 = jnp.bfloat16

D_MODEL = 1024
HEAD_DIM = 64
HEAD_SHIFT = 6
RET_HEADS = 8
NSA_HEADS = 8
NSA_KV_HEADS = 2
NSA_GROUP = NSA_HEADS // NSA_KV_HEADS
D_RET = RET_HEADS * HEAD_DIM
D_NSA = NSA_HEADS * HEAD_DIM
D_KV = NSA_KV_HEADS * HEAD_DIM
N_BRANCH = 3
RET_CHUNK = 128
ROPE_THETA = 10000.0
CMP_BLOCK = 32
CMP_STRIDE = 16
CMP_HIDDEN = 256
SLC_BLOCK = 64
SLC_SHIFT = 6
SLC_TOPK = 16
WIN_SIZE = 512
EPS = 1e-6
NEG = -1e30
FORCE_BONUS = 1e4
QK_SCALE = HEAD_DIM ** -0.5
LOG2E = 1.4426950408889634
V_ROWS = HEAD_DIM + 16

LANES = 128
SUBLANES = 8
PAIR = 2 * HEAD_DIM
VMEM_LIMIT = 48 * 1024 * 1024

PROJ_TM = 512
RET_TC = 1024
NSA_TQ = 128
NSA_QT = 2
SLC_TK = 512
WIN_KEYS = WIN_SIZE + NSA_TQ
OUT_TM = 1024

NT_DIMS = (((1,), (1,)), ((), ()))
TN_DIMS = (((0,), (0,)), ((), ()))


def _sigmoid(x):
    return 1.0 / (1.0 + jnp.exp(-x))


def _tile4(a):
    return jnp.concatenate([a, a, a, a], axis=1)


def _proj_kernel(steps_per_batch, x_ref, nw_ref, w_ret_ref, w_ng_ref, w_ckv_ref, w_skwk_ref, wt_q_ref, wt_v_ref,
                 wt_g_ref, knw_ref, ret_ref, ng_ref, xc_ref, ks_ref, kw_ref, nqt_ref, vt_ref, glt_ref, ckv_scr):
    x = x_ref[...]
    ms = jnp.mean(x * x, axis=-1, keepdims=True)
    h = (x * lax.rsqrt(ms + EPS) * nw_ref[...]).astype(BF16)
    ret_ref[...] = jnp.dot(h, w_ret_ref[...], preferred_element_type=F32)
    ng_ref[...] = jnp.dot(h, w_ng_ref[...], preferred_element_type=F32)

    ckv = jnp.dot(h, w_ckv_ref[...], preferred_element_type=F32)
    for half in range(2 * D_KV // LANES):
        ckv_scr[half] = ckv[:, half * LANES:(half + 1) * LANES]
    for l in range(CMP_STRIDE):
        for half in range(2 * D_KV // LANES):
            rows = ckv_scr[half, pl.ds(l, PROJ_TM // CMP_STRIDE, stride=CMP_STRIDE), :]
            for s in range(LANES // HEAD_DIM):
                xc_ref[0, half * (LANES // HEAD_DIM) + s, :, l * HEAD_DIM:(l + 1) * HEAD_DIM] = (
                    rows[:, s * HEAD_DIM:(s + 1) * HEAD_DIM])

    skwk = jnp.dot(h, w_skwk_ref[...], preferred_element_type=F32)
    lane = lax.broadcasted_iota(jnp.int32, (1, PAIR), 1)
    head0 = lane < HEAD_DIM
    m0 = jnp.where(head0, 1.0, 0.0)
    m1 = 1.0 - m0
    t_start = (pl.program_id(0) % steps_per_batch) * PROJ_TM
    tok = t_start + lax.broadcasted_iota(jnp.int32, (PROJ_TM, PAIR), 0)
    col = lax.broadcasted_iota(jnp.int32, (PROJ_TM, PAIR), 1)
    indicator = jnp.where((tok >> SLC_SHIFT) == col - HEAD_DIM, 1.0, 0.0)

    def pair_normed(t, w):
        t2 = t * t
        ms0 = jnp.sum(t2 * m0, axis=-1, keepdims=True)
        ms1 = jnp.sum(t2 * m1, axis=-1, keepdims=True)
        return t * lax.rsqrt(jnp.where(head0, ms0, ms1) * (1.0 / HEAD_DIM) + EPS) * w

    ns = pair_normed(skwk[:, :PAIR], knw_ref[0:1, :])
    nwin = pair_normed(skwk[:, PAIR:], knw_ref[1:2, :])
    for g in range(NSA_KV_HEADS):
        s_g = ns if g == 0 else pltpu.roll(ns, HEAD_DIM, 1)
        w_g = nwin if g == 0 else pltpu.roll(nwin, HEAD_DIM, 1)
        ks_ref[0, g] = jnp.where(head0, s_g, indicator).astype(BF16)
        kw_ref[0, g] = jnp.where(head0, w_g, 0.0).astype(BF16)

    qt =lax.dot_general(wt_q_ref[...], h, NT_DIMS, preferred_element_type=F32)
    vt = lax.dot_general(wt_v_ref[...], h, NT_DIMS, preferred_element_type=F32)
    gt = lax.dot_general(wt_g_ref[...], h, NT_DIMS, preferred_element_type=F32)
    pad_row = lax.broadcasted_iota(jnp.int32, (V_ROWS - HEAD_DIM, LANES), 0)
    ones_pad = jnp.where(pad_row == 0, 1.0, 0.0).astype(BF16)
    for j in range(PROJ_TM // LANES):
        sl = slice(j * LANES, (j + 1) * LANES)
        nqt_ref[0, j] = qt[:, sl]
        for blk in range(2 * NSA_KV_HEADS):
            vt_ref[0, j, blk * V_ROWS:blk * V_ROWS + HEAD_DIM, :] = (
                vt[blk * HEAD_DIM:(blk + 1) * HEAD_DIM, sl].astype(BF16))
            vt_ref[0, j, blk * V_ROWS + HEAD_DIM:(blk + 1) * V_ROWS, :] = ones_pad
        glt_ref[0, j] = gt[:, sl]


def _proj(x2, nw, w_ret, w_ng, w_ckv, w_skwk, wt_q, wt_v, wt_g, knw, B, T):
    N = B * T
    tpb = T // PROJ_TM
    sub = PROJ_TM // LANES
    nt = T // LANES
    const = lambda i: (0, 0)
    row = lambda i: (i, 0)
    trn = lambda i: (i // tpb, i % tpb, 0, 0)
    tokm = lambda i: (i // tpb, 0, i % tpb, 0)
    return pl.pallas_call(
        functools.partial(_proj_kernel, tpb),
        grid=(N // PROJ_TM,),
        in_specs=[
            pl.BlockSpec((PROJ_TM, D_MODEL), row),
            pl.BlockSpec((1, D_MODEL), const),
            pl.BlockSpec(w_ret.shape, const),
            pl.BlockSpec(w_ng.shape, const),
            pl.BlockSpec(w_ckv.shape, const),
            pl.BlockSpec(w_skwk.shape, const),
            pl.BlockSpec(wt_q.shape, const),
            pl.BlockSpec(wt_v.shape, const),
            pl.BlockSpec(wt_g.shape, const),
            pl.BlockSpec(knw.shape, const),
        ],
        out_specs=[
            pl.BlockSpec((PROJ_TM, 4 * D_RET), row),
            pl.BlockSpec((PROJ_TM, D_NSA), row),
            pl.BlockSpec((1, 2 * NSA_KV_HEADS, PROJ_TM // CMP_STRIDE, CMP_STRIDE * HEAD_DIM), tokm),
            pl.BlockSpec((1, NSA_KV_HEADS, PROJ_TM, PAIR), tokm),
            pl.BlockSpec((1, NSA_KV_HEADS, PROJ_TM, PAIR), tokm),
            pl.BlockSpec((1, sub, D_NSA, LANES), trn),
            pl.BlockSpec((1, sub, 2 * NSA_KV_HEADS * V_ROWS, LANES), trn),
            pl.BlockSpec((1, sub, 32, LANES), trn),
        ],
        out_shape=[
            jax.ShapeDtypeStruct((N, 4 * D_RET), F32),
            jax.ShapeDtypeStruct((N, D_NSA), F32),
            jax.ShapeDtypeStruct((B, 2 * NSA_KV_HEADS, T // CMP_STRIDE, CMP_STRIDE * HEAD_DIM), F32),
            jax.ShapeDtypeStruct((B, NSA_KV_HEADS, T, PAIR), BF16),
            jax.ShapeDtypeStruct((B, NSA_KV_HEADS, T, PAIR), BF16),
            jax.ShapeDtypeStruct((B, nt, D_NSA, LANES), F32),
            jax.ShapeDtypeStruct((B, nt, 2 * NSA_KV_HEADS * V_ROWS, LANES), BF16),
            jax.ShapeDtypeStruct((B, nt, 32, LANES), F32),
        ],
        scratch_shapes=[pltpu.VMEM((2 * D_KV // LANES, PROJ_TM, LANES), F32)],
        compiler_params=pltpu.CompilerParams(dimension_semantics=("parallel",), vmem_limit_bytes=VMEM_LIMIT),
        name="proj",
    )(x2, nw, w_ret, w_ng, w_ckv, w_skwk, wt_q, wt_v, wt_g, knw)


def _ret_kernel(q_ref, k_ref, v_ref, g_ref, cos_ref, sin_ref, dec_ref, zeta_ref, xi_ref, gch_ref, nw_ref,
                o_ref, state_ref):
    @pl.when(pl.program_id(1) == 0)
    def _():
        state_ref[...] = jnp.zeros_like(state_ref)

    lane = lax.broadcasted_iota(jnp.int32, (1, PAIR), 1)
    q_head = (lane >> (HEAD_SHIFT - 1)) & 1
    v_head = lane >> HEAD_SHIFT
    q_mask = [jnp.where(q_head == h, 1.0, 0.0).astype(BF16) for h in (0, 1)]
    v_mask = [jnp.where(v_head == h, 1.0, 0.0).astype(BF16) for h in (0, 1)]
    row_qh = (lax.broadcasted_iota(jnp.int32, (PAIR, PAIR), 0) >> (HEAD_SHIFT - 1)) & 1
    row_vh = lax.broadcasted_iota(jnp.int32, (PAIR, PAIR), 0) >> HEAD_SHIFT
    col_vh = lax.broadcasted_iota(jnp.int32, (PAIR, PAIR), 1) >> HEAD_SHIFT
    same_head_kv = jnp.where(row_qh == col_vh, 1.0, 0.0)
    head_mean = jnp.where(row_vh == col_vh, 1.0 / HEAD_DIM, 0.0).astype(BF16)

    pairs = range(RET_HEADS // 2)
    mean2 = jnp.concatenate([head_mean, head_mean], axis=0)
    for c in range(RET_TC // RET_CHUNK):
        sl = pl.ds(c * RET_CHUNK, RET_CHUNK)
        cos = cos_ref[sl, :]
        sin = sin_ref[sl, :]
        cols = [slice(p * PAIR, (p + 1) * PAIR) for p in pairs]
        qb, kb, vb, vzb = [], [], [], []
        for p in pairs:
            q = q_ref[0, sl, cols[p]]
            k = k_ref[0, sl, cols[p]]
            v = v_ref[0, sl, cols[p]]
            qb.append((q * cos + pltpu.roll(q, HEAD_DIM, 1) * sin).astype(BF16))
            kb.append(((k * cos + pltpu.roll(k, HEAD_DIM, 1) * sin) * QK_SCALE).astype(BF16))
            vb.append(v.astype(BF16))
            vzb.append((v * zeta_ref[p]).astype(BF16))
        states = [state_ref[p] for p in pairs]
        kk = [jnp.concatenate([kb[p] * q_mask[h] for h in (0, 1)], axis=0) for p in pairs]
        s = [lax.dot_general(qb[p], kk[p], NT_DIMS, preferred_element_type=F32) for p in pairs]
        o_cross = [jnp.dot(qb[p], states[p].astype(BF16), preferred_element_type=F32) for p in pairs]
        kv = [lax.dot_general(kb[p], vzb[p], TN_DIMS, preferred_element_type=F32) for p in pairs]
        sb = [(s[p] * dec_ref[p]).astype(BF16) for p in pairs]
        vv = [jnp.concatenate([vb[p] * v_mask[h] for h in (0, 1)], axis=0) for p in pairs]
        o = [jnp.dot(sb[p], vv[p], preferred_element_type=F32) + o_cross[p] * xi_ref[p] for p in pairs]
        for p in pairs:
            state_ref[p] = states[p] * gch_ref[p] + kv[p] * same_head_kv
        o2 = [o[p] * o[p] for p in pairs]
        o2_hi = [o2[p].astype(BF16) for p in pairs]
        o2_hl = [jnp.concatenate([o2_hi[p], (o2[p] - o2_hi[p].astype(F32)).astype(BF16)], axis=1) for p in pairs]
        ms = [jnp.dot(o2_hl[p], mean2, preferred_element_type=F32) for p in pairs]
        for p in pairs:
            g = g_ref[0, sl, cols[p]]
            y = o[p] * lax.rsqrt(ms[p] + EPS) * nw_ref[p]
            o_ref[0, sl, cols[p]] = (y * (g * _sigmoid(g))).astype(BF16)


def _retention(ret3, cos, sin, dec, zeta, xi, gch, nw, B, T):
    npair = RET_HEADS // 2
    blk = lambda col: pl.BlockSpec((1, RET_TC, D_RET), lambda b, i: (b, i, col))
    tab = pl.BlockSpec((RET_TC, PAIR), lambda b, i: (i, 0))
    whole = lambda a: pl.BlockSpec(a.shape, lambda b, i: (0,) * a.ndim)
    return pl.pallas_call(
        _ret_kernel,
        grid=(B, T // RET_TC),
        in_specs=[blk(0), blk(1), blk(2), blk(3), tab, tab, whole(dec), whole(zeta), whole(xi), whole(gch), whole(nw)],
        out_specs=pl.BlockSpec((1, RET_TC, D_RET), lambda b, i: (b, i, 0)),
        out_shape=jax.ShapeDtypeStruct((B, T, D_RET), BF16),
        scratch_shapes=[pltpu.VMEM((npair, PAIR, PAIR), F32)],
        compiler_params=pltpu.CompilerParams(dimension_semantics=("parallel", "arbitrary"),
                                             vmem_limit_bytes=VMEM_LIMIT),
        name="retention",
    )(ret3, ret3, ret3, ret3, cos, sin, dec, zeta, xi, gch, nw)


def _cmp_kernel(x_ref, pos_ref, w1_ref, w2_ref, w2t_ref, knw_ref, o_ref, ot_ref):
    is_key = pl.program_id(1) == 0
    half = CMP_STRIDE * HEAD_DIM
    for g in range(NSA_KV_HEADS):
        x = x_ref[0, g]
        a = jnp.dot((x + pos_ref[0, 0:1, :]).astype(BF16), w1_ref[0, :half, :], preferred_element_type=F32)
        b = jnp.dot((x + pos_ref[0, 1:2, :]).astype(BF16), w1_ref[0, half:, :], preferred_element_type=F32)
        hid = a + pltpu.roll(b, b.shape[0] - 1, 0)
        hid = (hid * _sigmoid(hid)).astype(BF16)
        out = jnp.dot(hid, w2_ref[0], preferred_element_type=F32)
        ms = jnp.mean(out * out, axis=-1, keepdims=True)
        normed = out * lax.rsqrt(ms + EPS) * knw_ref[...]
        o_ref[0, 0, g] = jnp.where(is_key, normed, out)
        ot_ref[0, 0, g] = lax.dot_general(w2t_ref[0], hid, NT_DIMS, preferred_element_type=F32)


def _compress(xc, pos, w1, w2, w2t, knw, B, ncb):
    return pl.pallas_call(
        _cmp_kernel,
        grid=(B, 2),
        in_specs=[
            pl.BlockSpec((1, NSA_KV_HEADS, ncb, CMP_STRIDE * HEAD_DIM), lambda b, s: (b, s, 0, 0)),
            pl.BlockSpec((1, 2, CMP_STRIDE * HEAD_DIM), lambda b, s: (s, 0, 0)),
            pl.BlockSpec((1, CMP_BLOCK * HEAD_DIM, CMP_HIDDEN), lambda b, s: (s, 0, 0)),
            pl.BlockSpec((1, CMP_HIDDEN, HEAD_DIM), lambda b, s: (s, 0, 0)),
            pl.BlockSpec((1, HEAD_DIM, CMP_HIDDEN), lambda b, s: (s, 0, 0)),
            pl.BlockSpec((1, HEAD_DIM), lambda b, s: (0, 0)),
        ],
        out_specs=[
            pl.BlockSpec((1, 1, NSA_KV_HEADS, ncb, HEAD_DIM), lambda b, s: (b, s, 0, 0, 0)),
            pl.BlockSpec((1, 1, NSA_KV_HEADS, HEAD_DIM, ncb), lambda b, s: (b, s, 0, 0, 0)),
        ],
        out_shape=[
            jax.ShapeDtypeStruct((B, 2, NSA_KV_HEADS, ncb, HEAD_DIM), F32),
            jax.ShapeDtypeStruct((B, 2, NSA_KV_HEADS, HEAD_DIM, ncb), F32),
        ],
        compiler_params=pltpu.CompilerParams(dimension_semantics=("parallel", "parallel"),
                                             vmem_limit_bytes=VMEM_LIMIT),
        name="compress",
    )(xc, pos, w1, w2, w2t, knw)


def _nsa_kernel(qt_ref, qtn_ref, glt_ref, ng_ref, ks_ref, kw_ref, vst_ref, vwt_ref, kc_ref, vct_ref, mt_ref, qnw_ref,
                bg_ref, o_ref, qp_ref, oc_ref, sa_ref, sb_ref, mxa_ref, mxb_ref, sw_ref, m_ref, acc_ref):
    qi = pl.program_id(1)
    t0 = qi * (NSA_QT * NSA_TQ)
    tiles = range(NSA_QT)
    chains = [(qt, g) for qt in tiles for g in range(NSA_KV_HEADS)]
    gq = NSA_GROUP * HEAD_DIM
    gg = 16
    ncb = kc_ref.shape[3]
    n_slc = mt_ref.shape[0]

    def select_masks(base_t0):
        n_idx = lax.broadcasted_iota(jnp.int32, (ncb, NSA_TQ), 0)
        jb = lax.broadcasted_iota(jnp.int32, (n_slc, NSA_TQ), 0)
        out = []
        for qt in tiles:
            tok = base_t0 + qt * NSA_TQ
            tok_c = tok + lax.broadcasted_iota(jnp.int32, (ncb, NSA_TQ), 1)
            cbias = _tile4(jnp.where((n_idx * CMP_STRIDE + (CMP_BLOCK - 1)) <= tok_c, 0.0, NEG))
            tok_row = tok + lax.broadcasted_iota(jnp.int32, (1, NSA_TQ), 1)
            has_block = _tile4(jnp.where(tok_row >= CMP_BLOCK - 1, 1.0, 0.0))
            tok_s = tok + lax.broadcasted_iota(jnp.int32, (n_slc, NSA_TQ), 1)
            valid_s = jb * SLC_BLOCK <= tok_s
            force = (jb == (tok_s >> SLC_SHIFT)) | (jb == 0)
            out.append((cbias, has_block, valid_s, force))
        return out

    def select_scores(src_ref, slot, masks, c):
        qt, g = chains[c]
        cols = []
        for r in range(NSA_GROUP):
            q = src_ref[0, qt, g * gq + r * HEAD_DIM:g * gq + (r + 1) * HEAD_DIM, :]
            ms = jnp.mean(q * q, axis=0, keepdims=True)
            cols.append(q * lax.rsqrt(ms + EPS) * qnw_ref[...] * (QK_SCALE * LOG2E))
        qs = jnp.concatenate(cols, axis=1).astype(BF16)
        qp_ref[slot, c, 0:HEAD_DIM, :] = qs
        return jnp.dot(kc_ref[0, 0, g].astype(BF16), qs, preferred_element_type=F32) + masks[qt][0]

    def select_probs(sc, slot, masks, c):
        qt, g = chains[c]
        _, has_block, valid_s, force = masks[qt]
        mc = jnp.max(sc, axis=0, keepdims=True)
        ec = jnp.exp2(sc - mc)
        lc = jnp.sum(ec, axis=0, keepdims=True)
        p = ec * (has_block / lc)
        oc_ref[slot, c] = jnp.dot(vct_ref[0, 0, g].astype(BF16), p.astype(BF16), preferred_element_type=F32)
        ps = p[:, 0:NSA_TQ]
        for r in range(1, NSA_GROUP):
            ps = ps + p[:, r * NSA_TQ:(r + 1) * NSA_TQ]
        ps_hi = ps.astype(BF16)
        ps_lo = (ps - ps_hi.astype(F32)).astype(BF16)
        imp = (jnp.dot(mt_ref[...], ps_hi, preferred_element_type=F32)
               + jnp.dot(mt_ref[...], ps_lo, preferred_element_type=F32))
        return jnp.where(valid_s, jnp.where(force, imp + FORCE_BONUS, imp), NEG)

    def select_rank(scores, valid_s, slot, n_live):
        if n_live <= SLC_TOPK:
            for c, (qt, g) in enumerate(chains):
                qp_ref[slot, c, HEAD_DIM:2 * HEAD_DIM, :] = _tile4(jnp.where(valid_s[qt], 0.0, NEG).astype(BF16))
            return
        sub = lax.broadcasted_iota(jnp.int32, (SUBLANES, NSA_TQ), 0)
        for c, (qt, g) in enumerate(chains):
            score = scores[c]
            blocks = [score[v * SUBLANES:(v + 1) * SUBLANES, :] for v in range(n_live // SUBLANES)]
            ranks = [jnp.zeros((SUBLANES, NSA_TQ), F32) for _ in blocks]
            for i in range(n_live):
                row = score[i:i + 1, :]
                for v, blk in enumerate(blocks):
                    if v * SUBLANES > i:
                        beats = row >= blk
                    elif (v + 1) * SUBLANES <= i:
                        beats = row > blk
                    else:
                        beats = (row > blk) | ((row >= blk) & (sub > i - v * SUBLANES))
                    ranks[v] = ranks[v] + jnp.where(beats, 1.0, 0.0)
            dead = [jnp.full((SUBLANES, NSA_TQ), float(n_slc), F32)] * ((n_slc - n_live) // SUBLANES)
            rank = jnp.concatenate(ranks + dead, axis=0)
            sel = (rank < float(SLC_TOPK)) & valid_s[qt]
            qp_ref[slot, c, HEAD_DIM:2 * HEAD_DIM, :] = _tile4(jnp.where(sel, 0.0, NEG).astype(BF16))

    step_tokens = NSA_QT * NSA_TQ

    def live_blocks(step):
        return ((step + 1) * step_tokens - 1) // SLC_BLOCK + 1

    all_chains = list(range(len(chains)))

    @pl.when(qi == 0)
    def _():
        masks0 = select_masks(0)
        scores0 = [select_probs(select_scores(qt_ref, 0, masks0, c), 0, masks0, c) for c in all_chains]
        select_rank(scores0, [m[2] for m in masks0], 0, live_blocks(0))

    cur = qi % 2
    nxt = 1 - cur
    t_next = t0 + NSA_QT * NSA_TQ

    kt0, ks0, wbias = [], [], []
    c_minus_r = (lax.broadcasted_iota(jnp.int32, (WIN_KEYS, NSA_TQ), 1)
                 - lax.broadcasted_iota(jnp.int32, (WIN_KEYS, NSA_TQ), 0))
    for qt in tiles:
        kt0.append(jnp.maximum(qi * NSA_QT + qt - WIN_SIZE // NSA_TQ, 0))
        ks0.append(pl.multiple_of(kt0[qt] * NSA_TQ, NSA_TQ))
        delta = (t0 + qt * NSA_TQ - ks0[qt]) + c_minus_r
        in_window = lax.bitcast_convert_type(delta, jnp.uint32) < WIN_SIZE
        wbias.append(_tile4(jnp.where(in_window, 0.0, NEG)))

    def win_scores(c):
        qt, g = chains[c]
        sw = jnp.dot(kw_ref[0, g, pl.ds(ks0[qt], WIN_KEYS), 0:HEAD_DIM], qp_ref[cur, c, 0:HEAD_DIM, :],
                     preferred_element_type=F32) + wbias[qt]
        sw_ref[c] = sw
        return jnp.max(sw, axis=0, keepdims=True)

    def win_attend(c, mw):
        qt, g = chains[c]
        ewb = jnp.exp2(sw_ref[c] - mw).astype(BF16)
        vwt = jnp.concatenate([vwt_ref[0, kt0[qt] + j, g * V_ROWS:(g + 1) * V_ROWS, :]
                               for j in range(WIN_KEYS // LANES)], axis=1)
        ow_aug = jnp.dot(vwt, ewb, preferred_element_type=F32)
        return ow_aug[0:HEAD_DIM, :] * (1.0 / ow_aug[HEAD_DIM:HEAD_DIM + 1, :])

    vt_per_tile = SLC_TK // LANES

    def slc_scores(j, dst_ref, mx_ref, which=all_chains):
        kst = pl.multiple_of(j * SLC_TK, SLC_TK)
        for c in which:
            qt, g = chains[c]
            s = jnp.dot(ks_ref[0, g, pl.ds(kst, SLC_TK), :], qp_ref[cur, c], preferred_element_type=F32)
            dst_ref[c] = s
            mx_ref[c] = jnp.max(s, axis=0, keepdims=True)

    def slc_update(j, src_ref, mx_ref, causal, which=all_chains, rows=SLC_TK):
        if causal:
            r_s = lax.broadcasted_iota(jnp.int32, (rows, NSA_TQ), 0)
            c_s = lax.broadcasted_iota(jnp.int32, (rows, NSA_TQ), 1)
            causal_bias = [_tile4(jnp.where(j * SLC_TK + r_s <= t0 + qt * NSA_TQ + c_s, 0.0, NEG)) for qt in tiles]
        for c in which:
            qt, g = chains[c]
            s = src_ref[c, 0:rows, :]
            if causal:
                s = s + causal_bias[qt]
                tile_max = jnp.max(s, axis=0, keepdims=True)
            else:
                tile_max = mx_ref[c]
            m_old = m_ref[c]
            m_new = jnp.maximum(m_old, tile_max)
            alpha = jnp.exp2(m_old - m_new)
            eb = jnp.exp2(s - m_new).astype(BF16)
            vt = jnp.concatenate([vst_ref[0, j * vt_per_tile + jj, g * V_ROWS:(g + 1) * V_ROWS, :]
                                  for jj in range(rows // LANES)], axis=1)
            acc_ref[c] = alpha * acc_ref[c] + jnp.dot(vt, eb, preferred_element_type=F32)
            m_ref[c] = m_new

    masks = select_masks(t_next)

    def matmul_stage(c):
        mw = win_scores(c)
        sc = select_scores(qtn_ref, nxt, masks, c)
        slc_scores(0, sa_ref, mxa_ref, [c])
        return sc, mw

    sel_scores, ow_t = [], []
    staged = matmul_stage(0)
    for c in all_chains:
        staged_next = matmul_stage(c + 1) if c + 1 < len(chains) else None
        sel_scores.append(select_probs(staged[0], nxt, masks, c))
        ow_t.append(win_attend(c, staged[1]))
        staged = staged_next
    sel_valid = [m[2] for m in masks]

    need = live_blocks(qi + 1)
    bounds = list(range(SLC_TOPK, n_slc + 1, SLC_TOPK))
    for lo, hi in zip([0] + bounds[:-1], bounds):
        in_range = (need > lo) if hi == bounds[-1] else ((need > lo) & (need <= hi))
        pl.when(in_range)(functools.partial(select_rank, sel_scores, sel_valid, nxt, hi))

    m_ref[...] = jnp.full(m_ref.shape, NEG, F32)
    acc_ref[...] = jnp.zeros(acc_ref.shape, F32)
    n_full = t0 // SLC_TK

    def pair(jj, carry):
        j = 2 * jj
        for c in all_chains:
            slc_scores(j + 1, sb_ref, mxb_ref, [c])
            slc_update(j, sa_ref, mxa_ref, False, [c])
        for c in all_chains:
            slc_scores(j + 2, sa_ref, mxa_ref, [c])
            slc_update(j + 1, sb_ref, mxb_ref, False, [c])
        return carry

    lax.fori_loop(0, n_full // 2, pair, 0)

    visible = t0 - n_full * SLC_TK + step_tokens

    def tail(odd, rows):
        if odd:
            for c in all_chains:
                slc_scores(n_full, sb_ref, mxb_ref, [c])
                slc_update(n_full - 1, sa_ref, mxa_ref, False, [c])
            slc_update(n_full, sb_ref, mxb_ref, True, rows=rows)
        else:
            slc_update(n_full, sa_ref, mxa_ref, True, rows=rows)

    for rows in range(step_tokens, SLC_TK + 1, step_tokens):
        for odd in (False, True):
            parity = (n_full % 2 == 1) if odd else (n_full % 2 == 0)
            pl.when((visible == rows) & parity)(functools.partial(tail, odd, rows))

    for qt in tiles:
        gates = _sigmoid(glt_ref[0, qt] + bg_ref[...])
        outs = []
        for g in range(NSA_KV_HEADS):
            c = qt * NSA_KV_HEADS + g
            os_t = acc_ref[c, 0:HEAD_DIM, :] * (1.0 / acc_ref[c, HEAD_DIM:HEAD_DIM + 1, :])
            oc_t = oc_ref[cur, c]
            for r in range(NSA_GROUP):
                sl = slice(r * NSA_TQ, (r + 1) * NSA_TQ)
                row = g * gg + r
                outs.append(gates[row:row + 1, :] * oc_t[:, sl]
                            + gates[row + NSA_GROUP:row + NSA_GROUP + 1, :] * os_t[:, sl]
                            + gates[row + 2 * NSA_GROUP:row + 2 * NSA_GROUP + 1, :] * ow_t[c][:, sl])
        o_tok = jnp.concatenate(outs, axis=0).T
        rows = pl.ds(qt * NSA_TQ, NSA_TQ)
        ng = ng_ref[0, rows, :]
        o_ref[0, rows, :] = (o_tok * (ng * _sigmoid(ng))).astype(BF16)


def _nsa(nqt, glt, ng3, ks, kw, vt, kc, vct, mt, qnw, bg, B, T):
    nt = T // LANES
    ncb = kc.shape[3]
    G = NSA_KV_HEADS
    NC = NSA_QT * G
    ncols = NSA_GROUP * NSA_TQ
    steps = T // (NSA_QT * NSA_TQ)
    return pl.pallas_call(
        _nsa_kernel,
        grid=(B, steps),
        in_specs=[
            pl.BlockSpec((1, NSA_QT, D_NSA, LANES), lambda b, i: (b, 0, 0, 0)),
            pl.BlockSpec((1, NSA_QT, D_NSA, LANES), lambda b, i: (b, jnp.minimum(i + 1, steps - 1), 0, 0)),
            pl.BlockSpec((1, NSA_QT, 16 * G, LANES), lambda b, i: (b, i, 0, 0)),
            pl.BlockSpec((1, NSA_QT * NSA_TQ, D_NSA), lambda b, i: (b, i, 0)),
            pl.BlockSpec((1, G, T, PAIR), lambda b, i: (b, 0, 0, 0)),
            pl.BlockSpec((1, G, T, PAIR), lambda b, i: (b, 0, 0, 0)),
            pl.BlockSpec((1, nt, G * V_ROWS, LANES), lambda b, i: (b, 0, 0, 0)),
            pl.BlockSpec((1, nt, G * V_ROWS, LANES), lambda b, i: (b, 0, 1, 0)),
            pl.BlockSpec((1, 1, G, ncb, HEAD_DIM), lambda b, i: (b, 0, 0, 0, 0)),
            pl.BlockSpec((1, 1, G, HEAD_DIM, ncb), lambda b, i: (b, 1, 0, 0, 0)),
            pl.BlockSpec(mt.shape, lambda b, i: (0, 0)),
            pl.BlockSpec((HEAD_DIM, 1), lambda b, i: (0, 0)),
            pl.BlockSpec((16 * G, 1), lambda b, i: (0, 0)),
        ],
        out_specs=pl.BlockSpec((1, NSA_QT * NSA_TQ, D_NSA), lambda b, i: (b, i, 0)),
        out_shape=jax.ShapeDtypeStruct((B, T, D_NSA), BF16),
        scratch_shapes=[
            pltpu.VMEM((2, NC, 2 * HEAD_DIM, ncols), BF16),
            pltpu.VMEM((2, NC, HEAD_DIM, ncols), F32),
            pltpu.VMEM((NC, SLC_TK, ncols), F32),
            pltpu.VMEM((NC, SLC_TK, ncols), F32),
            pltpu.VMEM((NC, 1, ncols), F32),
            pltpu.VMEM((NC, 1, ncols), F32),
            pltpu.VMEM((NC, WIN_KEYS, ncols), F32),
            pltpu.VMEM((NC, 1, ncols), F32),
            pltpu.VMEM((NC, V_ROWS, ncols), F32),
        ],
        compiler_params=pltpu.CompilerParams(dimension_semantics=("parallel", "arbitrary"),
                                             vmem_limit_bytes=VMEM_LIMIT),
        name="nsa",
    )(nqt, nqt, glt, ng3, ks, kw, vt, vt, kc, vct, mt, qnw, bg)


def _out_kernel(x_ref, yr_ref, yn_ref, wr_ref, wn_ref, o_ref):
    o_ref[...] = (x_ref[...]
                  + jnp.dot(yr_ref[...], wr_ref[...], preferred_element_type=F32)
                  + jnp.dot(yn_ref[...], wn_ref[...], preferred_element_type=F32))


def _outproj(x2, yr, yn, wr, wn):
    N = x2.shape[0]
    row = lambda i: (i, 0)
    const = lambda i: (0, 0)
    return pl.pallas_call(
        _out_kernel,
        grid=(N // OUT_TM,),
        in_specs=[pl.BlockSpec((OUT_TM, D_MODEL), row), pl.BlockSpec((OUT_TM, D_RET), row),
                  pl.BlockSpec((OUT_TM, D_NSA), row), pl.BlockSpec(wr.shape, const), pl.BlockSpec(wn.shape, const)],
        out_specs=pl.BlockSpec((OUT_TM, D_MODEL), row),
        out_shape=jax.ShapeDtypeStruct((N, D_MODEL), F32),
        compiler_params=pltpu.CompilerParams(dimension_semantics=("parallel",), vmem_limit_bytes=VMEM_LIMIT),
        name="outproj",
    )(x2, yr, yn, wr, wn)


@functools.lru_cache(maxsize=None)
def _tables(T):
    half = HEAD_DIM // 2
    inv = ROPE_THETA ** (-np.arange(half, dtype=np.float64) / half)
    ang = np.arange(T, dtype=np.float64)[:, None] * inv[None, :]
    cos = np.concatenate([np.cos(ang)] * 4, axis=1).astype(np.float32)
    sin = np.concatenate([-np.sin(ang), -np.sin(ang), np.sin(ang), np.sin(ang)], axis=1).astype(np.float32)

    C = RET_CHUNK
    log_g = np.log1p(-np.exp2(-5.0 - np.arange(RET_HEADS, dtype=np.float64)))
    pos = np.arange(C, dtype=np.float64)
    diff = pos[:, None] - pos[None, :]
    decay = np.where(diff >= 0, np.exp(log_g[:, None, None] * np.maximum(diff, 0.0)), 0.0)
    zeta = np.exp(log_g[:, None] * (C - 1.0 - pos))
    xi = np.exp(log_g[:, None] * (pos + 1.0))
    g_chunk = np.exp(log_g * C)
    npair = RET_HEADS // 2

    def pair_lanes(a):
        return np.repeat(a.reshape(npair, 2, C).transpose(0, 2, 1), HEAD_DIM, axis=2).astype(np.float32)

    dec = decay.reshape(npair, 2, C, C).transpose(0, 2, 1, 3).reshape(npair, C, 2 * C).astype(np.float32)
    gch = np.repeat(g_chunk.reshape(npair, 1, 2), HEAD_DIM, axis=2).astype(np.float32)

    n_cmp = (T - CMP_BLOCK) // CMP_STRIDE + 1
    ncb = T // CMP_STRIDE
    p = np.arange(n_cmp)[:, None] * CMP_STRIDE + np.arange(CMP_BLOCK)[None, :]
    blk = p // SLC_BLOCK
    M = (blk[:, :, None] == np.arange(T // SLC_BLOCK)[None, None, :]).mean(axis=1)
    mt = np.zeros((T // SLC_BLOCK, ncb), np.float32)
    mt[:, :n_cmp] = M.T
    return cos, sin, dec, pair_lanes(zeta), pair_lanes(xi), gch, mt


def kernel(x, norm_w, w_in, ret_norm_w, q_norm_w, k_norm_cmp, k_norm_slc, k_norm_win, cmp_pos_k, cmp_w1_k, cmp_w2_k,
           cmp_pos_v, cmp_w1_v, cmp_w2_v, b_gate, w_out):
    B, T, D = x.shape
    depth = norm_w.shape[0]
    cos, sin, dec, zeta, xi, gch, mt = _tables(T)
    ncb = T // CMP_STRIDE
    half = CMP_STRIDE * HEAD_DIM
    gate_src = np.zeros((NSA_KV_HEADS, 16), np.int32)
    gate_ok = np.zeros((NSA_KV_HEADS, 16), bool)
    for g in range(NSA_KV_HEADS):
        for br in range(N_BRANCH):
            for r in range(NSA_GROUP):
                gate_src[g, br * NSA_GROUP + r] = br * NSA_HEADS + g * NSA_GROUP + r
                gate_ok[g, br * NSA_GROUP + r] = True
    gate_src = gate_src.reshape(-1)
    gate_ok = gate_ok.reshape(-1)

    x2 = x.reshape(B * T, D)
    for layer in range(depth):
        w = w_in[layer].astype(BF16)
        o_ng = 4 * D_RET + D_NSA
        o_kv = o_ng + D_NSA
        quarter = HEAD_DIM // 2
        pair_perm = np.concatenate([np.arange(quarter), HEAD_DIM + np.arange(quarter),
                                    quarter + np.arange(quarter), HEAD_DIM + quarter + np.arange(quarter)])
        qk_perm = np.concatenate([p * PAIR + pair_perm for p in range(RET_HEADS // 2)])
        w_ret = jnp.concatenate([w[:, :D_RET][:, qk_perm], w[:, D_RET:2 * D_RET][:, qk_perm],
                                 w[:, 2 * D_RET:4 * D_RET]], axis=1)
        wt_q = w[:, 4 * D_RET:o_ng].T
        w_ng = w[:, o_ng:o_kv]
        w_ckv = w[:, o_kv:o_kv + 2 * D_KV]
        w_skwk = jnp.concatenate([w[:, o_kv + 2 * D_KV:o_kv + 3 * D_KV], w[:, o_kv + 4 * D_KV:o_kv + 5 * D_KV]], axis=1)
        wt_v = jnp.concatenate([w[:, o_kv + 3 * D_KV:o_kv + 4 * D_KV], w[:, o_kv + 5 * D_KV:o_kv + 6 * D_KV]], axis=1).T
        w_gl = w[:, o_kv + 6 * D_KV:]
        wt_g = jnp.where(gate_ok[:, None], w_gl.T[gate_src], jnp.zeros((), BF16))
        bg = jnp.where(gate_ok, b_gate[layer][gate_src], 0.0).reshape(-1, 1)

        knw = jnp.stack([jnp.tile(k_norm_slc[layer], 2), jnp.tile(k_norm_win[layer], 2)])
        ret, ng, xc, ks, kw, nqt, vt, glt = _proj(x2, norm_w[layer].reshape(1, D), w_ret, w_ng, w_ckv, w_skwk,
                                                  wt_q, wt_v, wt_g, knw, B, T)

        nw_pair = ret_norm_w[layer].reshape(RET_HEADS // 2, 1, PAIR)
        y_ret = _retention(ret.reshape(B, T, 4 * D_RET), jnp.asarray(cos), jnp.asarray(sin), jnp.asarray(dec),
                           jnp.asarray(zeta), jnp.asarray(xi), jnp.asarray(gch), nw_pair, B, T)

        pos =jnp.stack([cmp_pos_k[layer], cmp_pos_v[layer]]).reshape(2, 2, half)
        w1 = jnp.stack([cmp_w1_k[layer], cmp_w1_v[layer]]).astype(BF16)
        w2 = jnp.stack([cmp_w2_k[layer], cmp_w2_v[layer]]).astype(BF16)
        w2t = jnp.swapaxes(w2, 1, 2)
        kc, vct = _compress(xc, pos, w1, w2, w2t, k_norm_cmp[layer].reshape(1, HEAD_DIM), B, ncb)

        y_nsa = _nsa(nqt, glt, ng.reshape(B, T, D_NSA), ks, kw, vt, kc, vct, jnp.asarray(mt).astype(BF16),
                     q_norm_w[layer].reshape(HEAD_DIM, 1), bg, B, T)

        wo = w_out[layer].astype(BF16)
        x2 = _outproj(x2, y_ret.reshape(B * T, D_RET), y_nsa.reshape(B * T, D_NSA), wo[:D_RET], wo[D_RET:])
    return x2.reshape(B, T, D)
```

```python
import functools

import numpy as np
import jax
import jax.numpy as jnp
from jax import lax
from jax.experimental import pallas as pl
from jax.experimental.pallas import tpu as pltpu

F32 = jnp.float32
BF16 = jnp.bfloat16

D_MODEL = 1024
HEAD_DIM = 64
HEAD_SHIFT = 6
RET_HEADS = 8
NSA_HEADS = 8
NSA_KV_HEADS = 2
NSA_GROUP = NSA_HEADS // NSA_KV_HEADS
D_RET = RET_HEADS * HEAD_DIM
D_NSA = NSA_HEADS * HEAD_DIM
D_KV = NSA_KV_HEADS * HEAD_DIM
N_BRANCH = 3
RET_CHUNK = 128
ROPE_THETA = 10000.0
CMP_BLOCK = 32
CMP_STRIDE = 16
CMP_HIDDEN = 256
SLC_BLOCK = 64
SLC_SHIFT = 6
SLC_TOPK = 16
WIN_SIZE = 512
EPS = 1e-6
NEG = -1e30
FORCE_BONUS = 1e4
QK_SCALE = HEAD_DIM ** -0.5
LOG2E = 1.4426950408889634
V_ROWS = HEAD_DIM + 16

LANES = 128
SUBLANES = 8
PAIR = 2 * HEAD_DIM
VMEM_LIMIT = 48 * 1024 * 1024

PROJ_TM = 512
RET_TC = 1024
NSA_TQ = 128
NSA_QT = 2
SLC_TK = 512
WIN_KEYS = WIN_SIZE + NSA_TQ
OUT_TM = 1024

NT_DIMS = (((1,), (1,)), ((), ()))
TN_DIMS = (((0,), (0,)), ((), ()))


def _sigmoid(x):
    return 1.0 / (1.0 + jnp.exp(-x))


def _tile4(a):
    return jnp.concatenate([a, a, a, a], axis=1)


def _proj_kernel(steps_per_batch, x_ref, nw_ref, w_ret_ref, w_ng_ref, w_ckv_ref, w_skwk_ref, wt_q_ref, wt_v_ref,
                 wt_g_ref, knw_ref, ret_ref, ng_ref, xc_ref, ks_ref, kw_ref, nqt_ref, vt_ref, glt_ref, ckv_scr):
    x = x_ref[...]
    ms = jnp.mean(x * x, axis=-1, keepdims=True)
    h = (x * lax.rsqrt(ms + EPS) * nw_ref[...]).astype(BF16)
    ret_ref[...] = jnp.dot(h, w_ret_ref[...], preferred_element_type=F32)
    ng_ref[...] = jnp.dot(h, w_ng_ref[...], preferred_element_type=F32)

    ckv = jnp.dot(h, w_ckv_ref[...], preferred_element_type=F32)
    for half in range(2 * D_KV // LANES):
        ckv_scr[half] = ckv[:, half * LANES:(half + 1) * LANES]
    for l in range(CMP_STRIDE):
        for half in range(2 * D_KV // LANES):
            rows = ckv_scr[half, pl.ds(l, PROJ_TM // CMP_STRIDE, stride=CMP_STRIDE), :]
            for s in range(LANES // HEAD_DIM):
                xc_ref[0, half * (LANES // HEAD_DIM) + s, :, l * HEAD_DIM:(l + 1) * HEAD_DIM] = (
                    rows[:, s * HEAD_DIM:(s + 1) * HEAD_DIM])

    skwk = jnp.dot(h, w_skwk_ref[...], preferred_element_type=F32)
    lane = lax.broadcasted_iota(jnp.int32, (1, PAIR), 1)
    head0 = lane < HEAD_DIM
    m0 = jnp.where(head0, 1.0, 0.0)
    m1 = 1.0 - m0
    t_start = (pl.program_id(0) % steps_per_batch) * PROJ_TM
    tok = t_start + lax.broadcasted_iota(jnp.int32, (PROJ_TM, PAIR), 0)
    col = lax.broadcasted_iota(jnp.int32, (PROJ_TM, PAIR), 1)
    indicator = jnp.where((tok >> SLC_SHIFT) == col - HEAD_DIM, 1.0, 0.0)

    def pair_normed(t, w):
        t2 = t * t
        ms0 = jnp.sum(t2 * m0, axis=-1, keepdims=True)
        ms1 = jnp.sum(t2 * m1, axis=-1, keepdims=True)
        return t * lax.rsqrt(jnp.where(head0, ms0, ms1) * (1.0 / HEAD_DIM) + EPS) * w

    ns = pair_normed(skwk[:, :PAIR], knw_ref[0:1, :])
    nwin = pair_normed(skwk[:, PAIR:], knw_ref[1:2, :])
    for g in range(NSA_KV_HEADS):
        s_g = ns if g == 0 else pltpu.roll(ns, HEAD_DIM, 1)
        w_g = nwin if g == 0 else pltpu.roll(nwin, HEAD_DIM, 1)
        ks_ref[0, g] = jnp.where(head0, s_g, indicator).astype(BF16)
        kw_ref[0, g] = jnp.where(head0, w_g, 0.0).astype(BF16)

    qt =lax.dot_general(wt_q_ref[...], h, NT_DIMS, preferred_element_type=F32)
    vt = lax.dot_general(wt_v_ref[...], h, NT_DIMS, preferred_element_type=F32)
    gt = lax.dot_general(wt_g_ref[...], h, NT_DIMS, preferred_element_type=F32)
    pad_row = lax.broadcasted_iota(jnp.int32, (V_ROWS - HEAD_DIM, LANES), 0)
    ones_pad = jnp.where(pad_row == 0, 1.0, 0.0).astype(BF16)
    for j in range(PROJ_TM // LANES):
        sl = slice(j * LANES, (j + 1) * LANES)
        nqt_ref[0, j] = qt[:, sl]
        for blk in range(2 * NSA_KV_HEADS):
            vt_ref[0, j, blk * V_ROWS:blk * V_ROWS + HEAD_DIM, :] = (
                vt[blk * HEAD_DIM:(blk + 1) * HEAD_DIM, sl].astype(BF16))
            vt_ref[0, j, blk * V_ROWS + HEAD_DIM:(blk + 1) * V_ROWS, :] = ones_pad
        glt_ref[0, j] = gt[:, sl]


def _proj(x2, nw, w_ret, w_ng, w_ckv, w_skwk, wt_q, wt_v, wt_g, knw, B, T):
    N = B * T
    tpb = T // PROJ_TM
    sub = PROJ_TM // LANES
    nt = T // LANES
    const = lambda i: (0, 0)
    row = lambda i: (i, 0)
    trn = lambda i: (i // tpb, i % tpb, 0, 0)
    tokm = lambda i: (i // tpb, 0, i % tpb, 0)
    return pl.pallas_call(
        functools.partial(_proj_kernel, tpb),
        grid=(N // PROJ_TM,),
        in_specs=[
            pl.BlockSpec((PROJ_TM, D_MODEL), row),
            pl.BlockSpec((1, D_MODEL), const),
            pl.BlockSpec(w_ret.shape, const),
            pl.BlockSpec(w_ng.shape, const),
            pl.BlockSpec(w_ckv.shape, const),
            pl.BlockSpec(w_skwk.shape, const),
            pl.BlockSpec(wt_q.shape, const),
            pl.BlockSpec(wt_v.shape, const),
            pl.BlockSpec(wt_g.shape, const),
            pl.BlockSpec(knw.shape, const),
        ],
        out_specs=[
            pl.BlockSpec((PROJ_TM, 4 * D_RET), row),
            pl.BlockSpec((PROJ_TM, D_NSA), row),
            pl.BlockSpec((1, 2 * NSA_KV_HEADS, PROJ_TM // CMP_STRIDE, CMP_STRIDE * HEAD_DIM), tokm),
            pl.BlockSpec((1, NSA_KV_HEADS, PROJ_TM, PAIR), tokm),
            pl.BlockSpec((1, NSA_KV_HEADS, PROJ_TM, PAIR), tokm),
            pl.BlockSpec((1, sub, D_NSA, LANES), trn),
            pl.BlockSpec((1, sub, 2 * NSA_KV_HEADS * V_ROWS, LANES), trn),
            pl.BlockSpec((1, sub, 32, LANES), trn),
        ],
        out_shape=[
            jax.ShapeDtypeStruct((N, 4 * D_RET), F32),
            jax.ShapeDtypeStruct((N, D_NSA), F32),
            jax.ShapeDtypeStruct((B, 2 * NSA_KV_HEADS, T // CMP_STRIDE, CMP_STRIDE * HEAD_DIM), F32),
            jax.ShapeDtypeStruct((B, NSA_KV_HEADS, T, PAIR), BF16),
            jax.ShapeDtypeStruct((B, NSA_KV_HEADS, T, PAIR), BF16),
            jax.ShapeDtypeStruct((B, nt, D_NSA, LANES), F32),
            jax.ShapeDtypeStruct((B, nt, 2 * NSA_KV_HEADS * V_ROWS, LANES), BF16),
            jax.ShapeDtypeStruct((B, nt, 32, LANES), F32),
        ],
        scratch_shapes=[pltpu.VMEM((2 * D_KV // LANES, PROJ_TM, LANES), F32)],
        compiler_params=pltpu.CompilerParams(dimension_semantics=("parallel",), vmem_limit_bytes=VMEM_LIMIT),
        name="proj",
    )(x2, nw, w_ret, w_ng, w_ckv, w_skwk, wt_q, wt_v, wt_g, knw)


def _ret_kernel(q_ref, k_ref, v_ref, g_ref, cos_ref, sin_ref, dec_ref, zeta_ref, xi_ref, gch_ref, nw_ref,
                o_ref, state_ref):
    @pl.when(pl.program_id(1) == 0)
    def _():
        state_ref[...] = jnp.zeros_like(state_ref)

    lane = lax.broadcasted_iota(jnp.int32, (1, PAIR), 1)
    q_head = (lane >> (HEAD_SHIFT - 1)) & 1
    v_head = lane >> HEAD_SHIFT
    q_mask = [jnp.where(q_head == h, 1.0, 0.0).astype(BF16) for h in (0, 1)]
    v_mask = [jnp.where(v_head == h, 1.0, 0.0).astype(BF16) for h in (0, 1)]
    row_qh = (lax.broadcasted_iota(jnp.int32, (PAIR, PAIR), 0) >> (HEAD_SHIFT - 1)) & 1
    row_vh = lax.broadcasted_iota(jnp.int32, (PAIR, PAIR), 0) >> HEAD_SHIFT
    col_vh = lax.broadcasted_iota(jnp.int32, (PAIR, PAIR), 1) >> HEAD_SHIFT
    same_head_kv = jnp.where(row_qh == col_vh, 1.0, 0.0)
    head_mean = jnp.where(row_vh == col_vh, 1.0 / HEAD_DIM, 0.0).astype(BF16)

    pairs = range(RET_HEADS // 2)
    mean2 = jnp.concatenate([head_mean, head_mean], axis=0)
    for c in range(RET_TC // RET_CHUNK):
        sl = pl.ds(c * RET_CHUNK, RET_CHUNK)
        cos = cos_ref[sl, :]
        sin = sin_ref[sl, :]
        cols = [slice(p * PAIR, (p + 1) * PAIR) for p in pairs]
        qb, kb, vb, vzb = [], [], [], []
        for p in pairs:
            q = q_ref[0, sl, cols[p]]
            k = k_ref[0, sl, cols[p]]
            v = v_ref[0, sl, cols[p]]
            qb.append((q * cos + pltpu.roll(q, HEAD_DIM, 1) * sin).astype(BF16))
            kb.append(((k * cos + pltpu.roll(k, HEAD_DIM, 1) * sin) * QK_SCALE).astype(BF16))
            vb.append(v.astype(BF16))
            vzb.append((v * zeta_ref[p]).astype(BF16))
        states = [state_ref[p] for p in pairs]
        kk = [jnp.concatenate([kb[p] * q_mask[h] for h in (0, 1)], axis=0) for p in pairs]
        s = [lax.dot_general(qb[p], kk[p], NT_DIMS, preferred_element_type=F32) for p in pairs]
        o_cross = [jnp.dot(qb[p], states[p].astype(BF16), preferred_element_type=F32) for p in pairs]
        kv = [lax.dot_general(kb[p], vzb[p], TN_DIMS, preferred_element_type=F32) for p in pairs]
        sb = [(s[p] * dec_ref[p]).astype(BF16) for p in pairs]
        vv = [jnp.concatenate([vb[p] * v_mask[h] for h in (0, 1)], axis=0) for p in pairs]
        o = [jnp.dot(sb[p], vv[p], preferred_element_type=F32) + o_cross[p] * xi_ref[p] for p in pairs]
        for p in pairs:
            state_ref[p] = states[p] * gch_ref[p] + kv[p] * same_head_kv
        o2 = [o[p] * o[p] for p in pairs]
        o2_hi = [o2[p].astype(BF16) for p in pairs]
        o2_hl = [jnp.concatenate([o2_hi[p], (o2[p] - o2_hi[p].astype(F32)).astype(BF16)], axis=1) for p in pairs]
        ms = [jnp.dot(o2_hl[p], mean2, preferred_element_type=F32) for p in pairs]
        for p in pairs:
            g = g_ref[0, sl, cols[p]]
            y = o[p] * lax.rsqrt(ms[p] + EPS) * nw_ref[p]
            o_ref[0, sl, cols[p]] = (y * (g * _sigmoid(g))).astype(BF16)


def _retention(ret3, cos, sin, dec, zeta, xi, gch, nw, B, T):
    npair = RET_HEADS // 2
    blk = lambda col: pl.BlockSpec((1, RET_TC, D_RET), lambda b, i: (b, i, col))
    tab = pl.BlockSpec((RET_TC, PAIR), lambda b, i: (i, 0))
    whole = lambda a: pl.BlockSpec(a.shape, lambda b, i: (0,) * a.ndim)
    return pl.pallas_call(
        _ret_kernel,
        grid=(B, T // RET_TC),
        in_specs=[blk(0), blk(1), blk(2), blk(3), tab, tab, whole(dec), whole(zeta), whole(xi), whole(gch), whole(nw)],
        out_specs=pl.BlockSpec((1, RET_TC, D_RET), lambda b, i: (b, i, 0)),
        out_shape=jax.ShapeDtypeStruct((B, T, D_RET), BF16),
        scratch_shapes=[pltpu.VMEM((npair, PAIR, PAIR), F32)],
        compiler_params=pltpu.CompilerParams(dimension_semantics=("parallel", "arbitrary"),
                                             vmem_limit_bytes=VMEM_LIMIT),
        name="retention",
    )(ret3, ret3, ret3, ret3, cos, sin, dec, zeta, xi, gch, nw)


def _cmp_kernel(x_ref, pos_ref, w1_ref, w2_ref, w2t_ref, knw_ref, o_ref, ot_ref):
    is_key = pl.program_id(1) == 0
    half = CMP_STRIDE * HEAD_DIM
    for g in range(NSA_KV_HEADS):
        x = x_ref[0, g]
        a = jnp.dot((x + pos_ref[0, 0:1, :]).astype(BF16), w1_ref[0, :half, :], preferred_element_type=F32)
        b = jnp.dot((x + pos_ref[0, 1:2, :]).astype(BF16), w1_ref[0, half:, :], preferred_element_type=F32)
        hid = a + pltpu.roll(b, b.shape[0] - 1, 0)
        hid = (hid * _sigmoid(hid)).astype(BF16)
        out = jnp.dot(hid, w2_ref[0], preferred_element_type=F32)
        ms = jnp.mean(out * out, axis=-1, keepdims=True)
        normed = out * lax.rsqrt(ms + EPS) * knw_ref[...]
        o_ref[0, 0, g] = jnp.where(is_key, normed, out)
        ot_ref[0, 0, g] = lax.dot_general(w2t_ref[0], hid, NT_DIMS, preferred_element_type=F32)


def _compress(xc, pos, w1, w2, w2t, knw, B, ncb):
    return pl.pallas_call(
        _cmp_kernel,
        grid=(B, 2),
        in_specs=[
            pl.BlockSpec((1, NSA_KV_HEADS, ncb, CMP_STRIDE * HEAD_DIM), lambda b, s: (b, s, 0, 0)),
            pl.BlockSpec((1, 2, CMP_STRIDE * HEAD_DIM), lambda b, s: (s, 0, 0)),
            pl.BlockSpec((1, CMP_BLOCK * HEAD_DIM, CMP_HIDDEN), lambda b, s: (s, 0, 0)),
            pl.BlockSpec((1, CMP_HIDDEN, HEAD_DIM), lambda b, s: (s, 0, 0)),
            pl.BlockSpec((1, HEAD_DIM, CMP_HIDDEN), lambda b, s: (s, 0, 0)),
            pl.BlockSpec((1, HEAD_DIM), lambda b, s: (0, 0)),
        ],
        out_specs=[
            pl.BlockSpec((1, 1, NSA_KV_HEADS, ncb, HEAD_DIM), lambda b, s: (b, s, 0, 0, 0)),
            pl.BlockSpec((1, 1, NSA_KV_HEADS, HEAD_DIM, ncb), lambda b, s: (b, s, 0, 0, 0)),
        ],
        out_shape=[
            jax.ShapeDtypeStruct((B, 2, NSA_KV_HEADS, ncb, HEAD_DIM), F32),
            jax.ShapeDtypeStruct((B, 2, NSA_KV_HEADS, HEAD_DIM, ncb), F32),
        ],
        compiler_params=pltpu.CompilerParams(dimension_semantics=("parallel", "parallel"),
                                             vmem_limit_bytes=VMEM_LIMIT),
        name="compress",
    )(xc, pos, w1, w2, w2t, knw)


def _nsa_kernel(qt_ref, qtn_ref, glt_ref, ng_ref, ks_ref, kw_ref, vst_ref, vwt_ref, kc_ref, vct_ref, mt_ref, qnw_ref,
                bg_ref, o_ref, qp_ref, oc_ref, sa_ref, sb_ref, mxa_ref, mxb_ref, sw_ref, m_ref, acc_ref):
    qi = pl.program_id(1)
    t0 = qi * (NSA_QT * NSA_TQ)
    tiles = range(NSA_QT)
    chains = [(qt, g) for qt in tiles for g in range(NSA_KV_HEADS)]
    gq = NSA_GROUP * HEAD_DIM
    gg = 16
    ncb = kc_ref.shape[3]
    n_slc = mt_ref.shape[0]

    def select_masks(base_t0):
        n_idx = lax.broadcasted_iota(jnp.int32, (ncb, NSA_TQ), 0)
        jb = lax.broadcasted_iota(jnp.int32, (n_slc, NSA_TQ), 0)
        out = []
        for qt in tiles:
            tok = base_t0 + qt * NSA_TQ
            tok_c = tok + lax.broadcasted_iota(jnp.int32, (ncb, NSA_TQ), 1)
            cbias = _tile4(jnp.where((n_idx * CMP_STRIDE + (CMP_BLOCK - 1)) <= tok_c, 0.0, NEG))
            tok_row = tok + lax.broadcasted_iota(jnp.int32, (1, NSA_TQ), 1)
            has_block = _tile4(jnp.where(tok_row >= CMP_BLOCK - 1, 1.0, 0.0))
            tok_s = tok + lax.broadcasted_iota(jnp.int32, (n_slc, NSA_TQ), 1)
            valid_s = jb * SLC_BLOCK <= tok_s
            force = (jb == (tok_s >> SLC_SHIFT)) | (jb == 0)
            out.append((cbias, has_block, valid_s, force))
        return out

    def select_scores(src_ref, slot, masks, c):
        qt, g = chains[c]
        cols = []
        for r in range(NSA_GROUP):
            q = src_ref[0, qt, g * gq + r * HEAD_DIM:g * gq + (r + 1) * HEAD_DIM, :]
            ms = jnp.mean(q * q, axis=0, keepdims=True)
            cols.append(q * lax.rsqrt(ms + EPS) * qnw_ref[...] * (QK_SCALE * LOG2E))
        qs = jnp.concatenate(cols, axis=1).astype(BF16)
        qp_ref[slot, c, 0:HEAD_DIM, :] = qs
        return jnp.dot(kc_ref[0, 0, g].astype(BF16), qs, preferred_element_type=F32) + masks[qt][0]

    def select_probs(sc, slot, masks, c):
        qt, g = chains[c]
        _, has_block, valid_s, force = masks[qt]
        mc = jnp.max(sc, axis=0, keepdims=True)
        ec = jnp.exp2(sc - mc)
        lc = jnp.sum(ec, axis=0, keepdims=True)
        p = ec * (has_block / lc)
        oc_ref[slot, c] = jnp.dot(vct_ref[0, 0, g].astype(BF16), p.astype(BF16), preferred_element_type=F32)
        ps = p[:, 0:NSA_TQ]
        for r in range(1, NSA_GROUP):
            ps = ps + p[:, r * NSA_TQ:(r + 1) * NSA_TQ]
        ps_hi = ps.astype(BF16)
        ps_lo = (ps - ps_hi.astype(F32)).astype(BF16)
        imp = (jnp.dot(mt_ref[...], ps_hi, preferred_element_type=F32)
               + jnp.dot(mt_ref[...], ps_lo, preferred_element_type=F32))
        return jnp.where(valid_s, jnp.where(force, imp + FORCE_BONUS, imp), NEG)

    def select_rank(scores, valid_s, slot, n_live):
        if n_live <= SLC_TOPK:
            for c, (qt, g) in enumerate(chains):
                qp_ref[slot, c, HEAD_DIM:2 * HEAD_DIM, :] = _tile4(jnp.where(valid_s[qt], 0.0, NEG).astype(BF16))
            return
        sub = lax.broadcasted_iota(jnp.int32, (SUBLANES, NSA_TQ), 0)
        for c, (qt, g) in enumerate(chains):
            score = scores[c]
            blocks = [score[v * SUBLANES:(v + 1) * SUBLANES, :] for v in range(n_live // SUBLANES)]
            ranks = [jnp.zeros((SUBLANES, NSA_TQ), F32) for _ in blocks]
            for i in range(n_live):
                row = score[i:i + 1, :]
                for v, blk in enumerate(blocks):
                    if v * SUBLANES > i:
                        beats = row >= blk
                    elif (v + 1) * SUBLANES <= i:
                        beats = row > blk
                    else:
                        beats = (row > blk) | ((row >= blk) & (sub > i - v * SUBLANES))
                    ranks[v] = ranks[v] + jnp.where(beats, 1.0, 0.0)
            dead = [jnp.full((SUBLANES, NSA_TQ), float(n_slc), F32)] * ((n_slc - n_live) // SUBLANES)
            rank = jnp.concatenate(ranks + dead, axis=0)
            sel = (rank < float(SLC_TOPK)) & valid_s[qt]
            qp_ref[slot, c, HEAD_DIM:2 * HEAD_DIM, :] = _tile4(jnp.where(sel, 0.0, NEG).astype(BF16))

    step_tokens = NSA_QT * NSA_TQ

    def live_blocks(step):
        return ((step + 1) * step_tokens - 1) // SLC_BLOCK + 1

    all_chains = list(range(len(chains)))

    @pl.when(qi == 0)
    def _():
        masks0 = select_masks(0)
        scores0 = [select_probs(select_scores(qt_ref, 0, masks0, c), 0, masks0, c) for c in all_chains]
        select_rank(scores0, [m[2] for m in masks0], 0, live_blocks(0))

    cur = qi % 2
    nxt = 1 - cur
    t_next = t0 + NSA_QT * NSA_TQ

    kt0, ks0, wbias = [], [], []
    c_minus_r = (lax.broadcasted_iota(jnp.int32, (WIN_KEYS, NSA_TQ), 1)
                 - lax.broadcasted_iota(jnp.int32, (WIN_KEYS, NSA_TQ), 0))
    for qt in tiles:
        kt0.append(jnp.maximum(qi * NSA_QT + qt - WIN_SIZE // NSA_TQ, 0))
        ks0.append(pl.multiple_of(kt0[qt] * NSA_TQ, NSA_TQ))
        delta = (t0 + qt * NSA_TQ - ks0[qt]) + c_minus_r
        in_window = lax.bitcast_convert_type(delta, jnp.uint32) < WIN_SIZE
        wbias.append(_tile4(jnp.where(in_window, 0.0, NEG)))

    def win_scores(c):
        qt, g = chains[c]
        sw = jnp.dot(kw_ref[0, g, pl.ds(ks0[qt], WIN_KEYS), 0:HEAD_DIM], qp_ref[cur, c, 0:HEAD_DIM, :],
                     preferred_element_type=F32) + wbias[qt]
        sw_ref[c] = sw
        return jnp.max(sw, axis=0, keepdims=True)

    def win_attend(c, mw):
        qt, g = chains[c]
        ewb = jnp.exp2(sw_ref[c] - mw).astype(BF16)
        vwt = jnp.concatenate([vwt_ref[0, kt0[qt] + j, g * V_ROWS:(g + 1) * V_ROWS, :]
                               for j in range(WIN_KEYS // LANES)], axis=1)
        ow_aug = jnp.dot(vwt, ewb, preferred_element_type=F32)
        return ow_aug[0:HEAD_DIM, :] * (1.0 / ow_aug[HEAD_DIM:HEAD_DIM + 1, :])

    vt_per_tile = SLC_TK // LANES

    def slc_scores(j, dst_ref, mx_ref, which=all_chains):
        kst = pl.multiple_of(j * SLC_TK, SLC_TK)
        for c in which:
            qt, g = chains[c]
            s = jnp.dot(ks_ref[0, g, pl.ds(kst, SLC_TK), :], qp_ref[cur, c], preferred_element_type=F32)
            dst_ref[c] = s
            mx_ref[c] = jnp.max(s, axis=0, keepdims=True)

    def slc_update(j, src_ref, mx_ref, causal, which=all_chains, visible=SLC_TK):
        if causal:
            band = _tile4(jnp.where(lax.broadcasted_iota(jnp.int32, (NSA_TQ, NSA_TQ), 0)
                                    <= lax.broadcasted_iota(jnp.int32, (NSA_TQ, NSA_TQ), 1), 0.0, NEG))
        for c in which:
            qt, g = chains[c]
            if causal:
                r0 = visible - step_tokens + qt * NSA_TQ
                rows = r0 + NSA_TQ
                s_band = src_ref[c, r0:rows, :] + band
                s = jnp.concatenate([src_ref[c, 0:r0, :], s_band], axis=0) if r0 > 0 else s_band
                tile_max = jnp.max(s, axis=0, keepdims=True)
            else:
                rows = SLC_TK
                s = src_ref[c]
                tile_max = mx_ref[c]
            m_old = m_ref[c]
            m_new = jnp.maximum(m_old, tile_max)
            alpha = jnp.exp2(m_old - m_new)
            eb = jnp.exp2(s - m_new).astype(BF16)
            vt = jnp.concatenate([vst_ref[0, j * vt_per_tile + jj, g * V_ROWS:(g + 1) * V_ROWS, :]
                                  for jj in range(rows // LANES)], axis=1)
            acc_ref[c] = alpha * acc_ref[c] + jnp.dot(vt, eb, preferred_element_type=F32)
            m_ref[c] = m_new

    masks = select_masks(t_next)

    def matmul_stage(c):
        mw = win_scores(c)
        sc = select_scores(qtn_ref, nxt, masks, c)
        slc_scores(0, sa_ref, mxa_ref, [c])
        return sc, mw

    sel_scores, ow_t = [], []
    staged = matmul_stage(0)
    for c in all_chains:
        staged_next = matmul_stage(c + 1) if c + 1 < len(chains) else None
        sel_scores.append(select_probs(staged[0], nxt, masks, c))
        ow_t.append(win_attend(c, staged[1]))
        staged = staged_next
    sel_valid = [m[2] for m in masks]

    need = live_blocks(qi + 1)
    bounds = list(range(SLC_TOPK, n_slc + 1, SUBLANES))
    for lo, hi in zip([0] + bounds[:-1], bounds):
        in_range = (need > lo) if hi == bounds[-1] else ((need > lo) & (need <= hi))
        pl.when(in_range)(functools.partial(select_rank, sel_scores, sel_valid, nxt, hi))

    m_ref[...] = jnp.full(m_ref.shape, NEG, F32)
    acc_ref[...] = jnp.zeros(acc_ref.shape, F32)
    n_full = t0 // SLC_TK

    def pair(jj, carry):
        j = 2 * jj
        for c in all_chains:
            slc_scores(j + 1, sb_ref, mxb_ref, [c])
            slc_update(j, sa_ref, mxa_ref, False, [c])
        for c in all_chains:
            slc_scores(j + 2, sa_ref, mxa_ref, [c])
            slc_update(j + 1, sb_ref, mxb_ref, False, [c])
        return carry

    lax.fori_loop(0, n_full // 2, pair, 0)

    visible = t0 - n_full * SLC_TK + step_tokens

    def tail(odd, rows):
        if odd:
            for c in all_chains:
                slc_scores(n_full, sb_ref, mxb_ref, [c])
                slc_update(n_full - 1, sa_ref, mxa_ref, False, [c])
            slc_update(n_full, sb_ref, mxb_ref, True, visible=rows)
        else:
            slc_update(n_full, sa_ref, mxa_ref, True, visible=rows)

    for rows in range(step_tokens, SLC_TK + 1, step_tokens):
        for odd in (False, True):
            parity = (n_full % 2 == 1) if odd else (n_full % 2 == 0)
            pl.when((visible == rows) & parity)(functools.partial(tail, odd, rows))

    for qt in tiles:
        gates = _sigmoid(glt_ref[0, qt] + bg_ref[...])
        outs = []
        for g in range(NSA_KV_HEADS):
            c = qt * NSA_KV_HEADS + g
            os_t = acc_ref[c, 0:HEAD_DIM, :] * (1.0 / acc_ref[c, HEAD_DIM:HEAD_DIM + 1, :])
            oc_t = oc_ref[cur, c]
            for r in range(NSA_GROUP):
                sl = slice(r * NSA_TQ, (r + 1) * NSA_TQ)
                row = g * gg + r
                outs.append(gates[row:row + 1, :] * oc_t[:, sl]
                            + gates[row + NSA_GROUP:row + NSA_GROUP + 1, :] * os_t[:, sl]
                            + gates[row + 2 * NSA_GROUP:row + 2 * NSA_GROUP + 1, :] * ow_t[c][:, sl])
        o_tok = jnp.concatenate(outs, axis=0).T
        rows = pl.ds(qt * NSA_TQ, NSA_TQ)
        ng = ng_ref[0, rows, :]
        o_ref[0, rows, :] = (o_tok * (ng * _sigmoid(ng))).astype(BF16)


def _nsa(nqt, glt, ng3, ks, kw, vt, kc, vct, mt, qnw, bg, B, T):
    nt = T // LANES
    ncb = kc.shape[3]
    G = NSA_KV_HEADS
    NC = NSA_QT * G
    ncols = NSA_GROUP * NSA_TQ
    steps = T // (NSA_QT * NSA_TQ)
    return pl.pallas_call(
        _nsa_kernel,
        grid=(B, steps),
        in_specs=[
            pl.BlockSpec((1, NSA_QT, D_NSA, LANES), lambda b, i: (b, 0, 0, 0)),
            pl.BlockSpec((1, NSA_QT, D_NSA, LANES), lambda b, i: (b, jnp.minimum(i + 1, steps - 1), 0, 0)),
            pl.BlockSpec((1, NSA_QT, 16 * G, LANES), lambda b, i: (b, i, 0, 0)),
            pl.BlockSpec((1, NSA_QT * NSA_TQ, D_NSA), lambda b, i: (b, i, 0)),
            pl.BlockSpec((1, G, T, PAIR), lambda b, i: (b, 0, 0, 0)),
            pl.BlockSpec((1, G, T, PAIR), lambda b, i: (b, 0, 0, 0)),
            pl.BlockSpec((1, nt, G * V_ROWS, LANES), lambda b, i: (b, 0, 0, 0)),
            pl.BlockSpec((1, nt, G * V_ROWS, LANES), lambda b, i: (b, 0, 1, 0)),
            pl.BlockSpec((1, 1, G, ncb, HEAD_DIM), lambda b, i: (b, 0, 0, 0, 0)),
            pl.BlockSpec((1, 1, G, HEAD_DIM, ncb), lambda b, i: (b, 1, 0, 0, 0)),
            pl.BlockSpec(mt.shape, lambda b, i: (0, 0)),
            pl.BlockSpec((HEAD_DIM, 1), lambda b, i: (0, 0)),
            pl.BlockSpec((16 * G, 1), lambda b, i: (0, 0)),
        ],
        out_specs=pl.BlockSpec((1, NSA_QT * NSA_TQ, D_NSA), lambda b, i: (b, i, 0)),
        out_shape=jax.ShapeDtypeStruct((B, T, D_NSA), BF16),
        scratch_shapes=[
            pltpu.VMEM((2, NC, 2 * HEAD_DIM, ncols), BF16),
            pltpu.VMEM((2, NC, HEAD_DIM, ncols), F32),
            pltpu.VMEM((NC, SLC_TK, ncols), F32),
            pltpu.VMEM((NC, SLC_TK, ncols), F32),
            pltpu.VMEM((NC, 1, ncols), F32),
            pltpu.VMEM((NC, 1, ncols), F32),
            pltpu.VMEM((NC, WIN_KEYS, ncols), F32),
            pltpu.VMEM((NC, 1, ncols), F32),
            pltpu.VMEM((NC, V_ROWS, ncols), F32),
        ],
        compiler_params=pltpu.CompilerParams(dimension_semantics=("parallel", "arbitrary"),
                                             vmem_limit_bytes=VMEM_LIMIT),
        name="nsa",
    )(nqt, nqt, glt, ng3, ks, kw, vt, vt, kc, vct, mt, qnw, bg)


def _out_kernel(x_ref, yr_ref, yn_ref, wr_ref, wn_ref, o_ref):
    o_ref[...] = (x_ref[...]
                  + jnp.dot(yr_ref[...], wr_ref[...], preferred_element_type=F32)
                  + jnp.dot(yn_ref[...], wn_ref[...], preferred_element_type=F32))


def _outproj(x2, yr, yn, wr, wn):
    N = x2.shape[0]
    row = lambda i: (i, 0)
    const = lambda i: (0, 0)
    return pl.pallas_call(
        _out_kernel,
        grid=(N // OUT_TM,),
        in_specs=[pl.BlockSpec((OUT_TM, D_MODEL), row), pl.BlockSpec((OUT_TM, D_RET), row),
                  pl.BlockSpec((OUT_TM, D_NSA), row), pl.BlockSpec(wr.shape, const), pl.BlockSpec(wn.shape, const)],
        out_specs=pl.BlockSpec((OUT_TM, D_MODEL), row),
        out_shape=jax.ShapeDtypeStruct((N, D_MODEL), F32),
        compiler_params=pltpu.CompilerParams(dimension_semantics=("parallel",), vmem_limit_bytes=VMEM_LIMIT),
        name="outproj",
    )(x2, yr, yn, wr, wn)


@functools.lru_cache(maxsize=None)
def _tables(T):
    half = HEAD_DIM // 2
    inv = ROPE_THETA ** (-np.arange(half, dtype=np.float64) / half)
    ang = np.arange(T, dtype=np.float64)[:, None] * inv[None, :]
    cos = np.concatenate([np.cos(ang)] * 4, axis=1).astype(np.float32)
    sin = np.concatenate([-np.sin(ang), -np.sin(ang), np.sin(ang), np.sin(ang)], axis=1).astype(np.float32)

    C = RET_CHUNK
    log_g = np.log1p(-np.exp2(-5.0 - np.arange(RET_HEADS, dtype=np.float64)))
    pos = np.arange(C, dtype=np.float64)
    diff = pos[:, None] - pos[None, :]
    decay = np.where(diff >= 0, np.exp(log_g[:, None, None] * np.maximum(diff, 0.0)), 0.0)
    zeta = np.exp(log_g[:, None] * (C - 1.0 - pos))
    xi = np.exp(log_g[:, None] * (pos + 1.0))
    g_chunk = np.exp(log_g * C)
    npair = RET_HEADS // 2

    def pair_lanes(a):
        return np.repeat(a.reshape(npair, 2, C).transpose(0, 2, 1), HEAD_DIM, axis=2).astype(np.float32)

    dec = decay.reshape(npair, 2, C, C).transpose(0, 2, 1, 3).reshape(npair, C, 2 * C).astype(np.float32)
    gch = np.repeat(g_chunk.reshape(npair, 1, 2), HEAD_DIM, axis=2).astype(np.float32)

    n_cmp = (T - CMP_BLOCK) // CMP_STRIDE + 1
    ncb = T // CMP_STRIDE
    p = np.arange(n_cmp)[:, None] * CMP_STRIDE + np.arange(CMP_BLOCK)[None, :]
    blk = p // SLC_BLOCK
    M = (blk[:, :, None] == np.arange(T // SLC_BLOCK)[None, None, :]).mean(axis=1)
    mt = np.zeros((T // SLC_BLOCK, ncb), np.float32)
    mt[:, :n_cmp] = M.T
    return cos, sin, dec, pair_lanes(zeta), pair_lanes(xi), gch, mt


def kernel(x, norm_w, w_in, ret_norm_w, q_norm_w, k_norm_cmp, k_norm_slc, k_norm_win, cmp_pos_k, cmp_w1_k, cmp_w2_k,
           cmp_pos_v, cmp_w1_v, cmp_w2_v, b_gate, w_out):
    B, T, D = x.shape
    depth = norm_w.shape[0]
    cos, sin, dec, zeta, xi, gch, mt = _tables(T)
    ncb = T // CMP_STRIDE
    half = CMP_STRIDE * HEAD_DIM
    gate_src = np.zeros((NSA_KV_HEADS, 16), np.int32)
    gate_ok = np.zeros((NSA_KV_HEADS, 16), bool)
    for g in range(NSA_KV_HEADS):
        for br in range(N_BRANCH):
            for r in range(NSA_GROUP):
                gate_src[g, br * NSA_GROUP + r] = br * NSA_HEADS + g * NSA_GROUP + r
                gate_ok[g, br * NSA_GROUP + r] = True
    gate_src = gate_src.reshape(-1)
    gate_ok = gate_ok.reshape(-1)

    x2 = x.reshape(B * T, D)
    for layer in range(depth):
        w = w_in[layer].astype(BF16)
        o_ng = 4 * D_RET + D_NSA
        o_kv = o_ng + D_NSA
        quarter = HEAD_DIM // 2
        pair_perm = np.concatenate([np.arange(quarter), HEAD_DIM + np.arange(quarter),
                                    quarter + np.arange(quarter), HEAD_DIM + quarter + np.arange(quarter)])
        qk_perm = np.concatenate([p * PAIR + pair_perm for p in range(RET_HEADS // 2)])
        w_ret = jnp.concatenate([w[:, :D_RET][:, qk_perm], w[:, D_RET:2 * D_RET][:, qk_perm],
                                 w[:, 2 * D_RET:4 * D_RET]], axis=1)
        wt_q = w[:, 4 * D_RET:o_ng].T
        w_ng = w[:, o_ng:o_kv]
        w_ckv = w[:, o_kv:o_kv + 2 * D_KV]
        w_skwk = jnp.concatenate([w[:, o_kv + 2 * D_KV:o_kv + 3 * D_KV], w[:, o_kv + 4 * D_KV:o_kv + 5 * D_KV]], axis=1)
        wt_v = jnp.concatenate([w[:, o_kv + 3 * D_KV:o_kv + 4 * D_KV], w[:, o_kv + 5 * D_KV:o_kv + 6 * D_KV]], axis=1).T
        w_gl = w[:, o_kv + 6 * D_KV:]
        wt_g = jnp.where(gate_ok[:, None], w_gl.T[gate_src], jnp.zeros((), BF16))
        bg = jnp.where(gate_ok, b_gate[layer][gate_src], 0.0).reshape(-1, 1)

        knw = jnp.stack([jnp.tile(k_norm_slc[layer], 2), jnp.tile(k_norm_win[layer], 2)])
        ret, ng, xc, ks, kw, nqt, vt, glt = _proj(x2, norm_w[layer].reshape(1, D), w_ret, w_ng, w_ckv, w_skwk,
                                                  wt_q, wt_v, wt_g, knw, B, T)

        nw_pair = ret_norm_w[layer].reshape(RET_HEADS // 2, 1, PAIR)
        y_ret = _retention(ret.reshape(B, T, 4 * D_RET), jnp.asarray(cos), jnp.asarray(sin), jnp.asarray(dec),
                           jnp.asarray(zeta), jnp.asarray(xi), jnp.asarray(gch), nw_pair, B, T)

        pos =jnp.stack([cmp_pos_k[layer], cmp_pos_v[layer]]).reshape(2, 2, half)
        w1 = jnp.stack([cmp_w1_k[layer], cmp_w1_v[layer]]).astype(BF16)
        w2 = jnp.stack([cmp_w2_k[layer], cmp_w2_v[layer]]).astype(BF16)
        w2t = jnp.swapaxes(w2, 1, 2)
        kc, vct = _compress(xc, pos, w1, w2, w2t, k_norm_cmp[layer].reshape(1, HEAD_DIM), B, ncb)

        y_nsa = _nsa(nqt, glt, ng.reshape(B, T, D_NSA), ks, kw, vt, kc, vct, jnp.asarray(mt).astype(BF16),
                     q_norm_w[layer].reshape(HEAD_DIM, 1), bg, B, T)

        wo = w_out[layer].astype(BF16)
        x2 = _outproj(x2, y_ret.reshape(B * T, D_RET), y_nsa.reshape(B * T, D_NSA), wo[:D_RET], wo[D_RET:])
    return x2.reshape(B, T, D)
```

```python
import functools

import numpy as np
import jax
import jax.numpy as jnp
from jax import lax
from jax.experimental import pallas as pl
from jax.experimental.pallas import tpu as pltpu

F32 = jnp.float32
BF16 = jnp.bfloat16

D_MODEL = 1024
HEAD_DIM = 64
HEAD_SHIFT = 6
RET_HEADS = 8
NSA_HEADS = 8
NSA_KV_HEADS = 2
NSA_GROUP = NSA_HEADS // NSA_KV_HEADS
D_RET = RET_HEADS * HEAD_DIM
D_NSA = NSA_HEADS * HEAD_DIM
D_KV = NSA_KV_HEADS * HEAD_DIM
N_BRANCH = 3
RET_CHUNK = 128
ROPE_THETA = 10000.0
CMP_BLOCK = 32
CMP_STRIDE = 16
CMP_HIDDEN = 256
SLC_BLOCK = 64
SLC_SHIFT = 6
SLC_TOPK = 16
WIN_SIZE = 512
EPS = 1e-6
NEG = -1e30
FORCE_BONUS = 1e4
QK_SCALE = HEAD_DIM ** -0.5
LOG2E = 1.4426950408889634
V_ROWS = HEAD_DIM + 16

LANES = 128
SUBLANES = 8
PAIR = 2 * HEAD_DIM
VMEM_LIMIT = 48 * 1024 * 1024

PROJ_TM = 512
RET_TC = 1024
NSA_TQ = 128
NSA_QT = 2
SLC_TK = 512
WIN_KEYS = WIN_SIZE + NSA_TQ
OUT_TM = 1024

NT_DIMS = (((1,), (1,)), ((), ()))
TN_DIMS = (((0,), (0,)), ((), ()))


def _sigmoid(x):
    return 1.0 / (1.0 + jnp.exp(-x))


def _tile4(a):
    return jnp.concatenate([a, a, a, a], axis=1)


def _proj_kernel(steps_per_batch, x_ref, nw_ref, w_ret_ref, w_ng_ref, w_ckv_ref, w_skwk_ref, wt_q_ref, wt_v_ref,
                 wt_g_ref, knw_ref, ret_ref, ng_ref, xc_ref, ks_ref, kw_ref, nqt_ref, vt_ref, glt_ref, ckv_scr):
    x = x_ref[...]
    ms = jnp.mean(x * x, axis=-1, keepdims=True)
    h = (x * lax.rsqrt(ms + EPS) * nw_ref[...]).astype(BF16)
    ret_ref[...] = jnp.dot(h, w_ret_ref[...], preferred_element_type=F32)
    ng_ref[...] = jnp.dot(h, w_ng_ref[...], preferred_element_type=F32)

    ckv = jnp.dot(h, w_ckv_ref[...], preferred_element_type=F32)
    for half in range(2 * D_KV // LANES):
        ckv_scr[half] = ckv[:, half * LANES:(half + 1) * LANES]
    for l in range(CMP_STRIDE):
        for half in range(2 * D_KV // LANES):
            rows = ckv_scr[half, pl.ds(l, PROJ_TM // CMP_STRIDE, stride=CMP_STRIDE), :]
            for s in range(LANES // HEAD_DIM):
                xc_ref[0, half * (LANES // HEAD_DIM) + s, :, l * HEAD_DIM:(l + 1) * HEAD_DIM] = (
                    rows[:, s * HEAD_DIM:(s + 1) * HEAD_DIM])

    skwk = jnp.dot(h, w_skwk_ref[...], preferred_element_type=F32)
    lane = lax.broadcasted_iota(jnp.int32, (1, PAIR), 1)
    head0 = lane < HEAD_DIM
    m0 = jnp.where(head0, 1.0, 0.0)
    m1 = 1.0 - m0
    t_start = (pl.program_id(0) % steps_per_batch) * PROJ_TM
    tok = t_start + lax.broadcasted_iota(jnp.int32, (PROJ_TM, PAIR), 0)
    col = lax.broadcasted_iota(jnp.int32, (PROJ_TM, PAIR), 1)
    indicator = jnp.where((tok >> SLC_SHIFT) == col - HEAD_DIM, 1.0, 0.0)

    def pair_normed(t, w):
        t2 = t * t
        ms0 = jnp.sum(t2 * m0, axis=-1, keepdims=True)
        ms1 = jnp.sum(t2 * m1, axis=-1, keepdims=True)
        return t * lax.rsqrt(jnp.where(head0, ms0, ms1) * (1.0 / HEAD_DIM) + EPS) * w

    ns = pair_normed(skwk[:, :PAIR], knw_ref[0:1, :])
    nwin = pair_normed(skwk[:, PAIR:], knw_ref[1:2, :])
    for g in range(NSA_KV_HEADS):
        s_g = ns if g == 0 else pltpu.roll(ns, HEAD_DIM, 1)
        w_g = nwin if g == 0 else pltpu.roll(nwin, HEAD_DIM, 1)
        ks_ref[0, g] = jnp.where(head0, s_g, indicator).astype(BF16)
        kw_ref[0, g] = jnp.where(head0, w_g, 0.0).astype(BF16)

    qt =lax.dot_general(wt_q_ref[...], h, NT_DIMS, preferred_element_type=F32)
    vt = lax.dot_general(wt_v_ref[...], h, NT_DIMS, preferred_element_type=F32)
    gt = lax.dot_general(wt_g_ref[...], h, NT_DIMS, preferred_element_type=F32)
    pad_row = lax.broadcasted_iota(jnp.int32, (V_ROWS - HEAD_DIM, LANES), 0)
    ones_pad = jnp.where(pad_row == 0, 1.0, 0.0).astype(BF16)
    for j in range(PROJ_TM // LANES):
        sl = slice(j * LANES, (j + 1) * LANES)
        nqt_ref[0, j] = qt[:, sl]
        for blk in range(2 * NSA_KV_HEADS):
            vt_ref[0, j, blk * V_ROWS:blk * V_ROWS + HEAD_DIM, :] = (
                vt[blk * HEAD_DIM:(blk + 1) * HEAD_DIM, sl].astype(BF16))
            vt_ref[0, j, blk * V_ROWS + HEAD_DIM:(blk + 1) * V_ROWS, :] = ones_pad
        glt_ref[0, j] = gt[:, sl]


def _proj(x2, nw, w_ret, w_ng, w_ckv, w_skwk, wt_q, wt_v, wt_g, knw, B, T):
    N = B * T
    tpb = T // PROJ_TM
    sub = PROJ_TM // LANES
    nt = T // LANES
    const = lambda i: (0, 0)
    row = lambda i: (i, 0)
    trn = lambda i: (i // tpb, i % tpb, 0, 0)
    tokm = lambda i: (i // tpb, 0, i % tpb, 0)
    return pl.pallas_call(
        functools.partial(_proj_kernel, tpb),
        grid=(N // PROJ_TM,),
        in_specs=[
            pl.BlockSpec((PROJ_TM, D_MODEL), row),
            pl.BlockSpec((1, D_MODEL), const),
            pl.BlockSpec(w_ret.shape, const),
            pl.BlockSpec(w_ng.shape, const),
            pl.BlockSpec(w_ckv.shape, const),
            pl.BlockSpec(w_skwk.shape, const),
            pl.BlockSpec(wt_q.shape, const),
            pl.BlockSpec(wt_v.shape, const),
            pl.BlockSpec(wt_g.shape, const),
            pl.BlockSpec(knw.shape, const),
        ],
        out_specs=[
            pl.BlockSpec((PROJ_TM, 4 * D_RET), row),
            pl.BlockSpec((PROJ_TM, D_NSA), row),
            pl.BlockSpec((1, 2 * NSA_KV_HEADS, PROJ_TM // CMP_STRIDE, CMP_STRIDE * HEAD_DIM), tokm),
            pl.BlockSpec((1, NSA_KV_HEADS, PROJ_TM, PAIR), tokm),
            pl.BlockSpec((1, NSA_KV_HEADS, PROJ_TM, PAIR), tokm),
            pl.BlockSpec((1, sub, D_NSA, LANES), trn),
            pl.BlockSpec((1, sub, 2 * NSA_KV_HEADS * V_ROWS, LANES), trn),
            pl.BlockSpec((1, sub, 32, LANES), trn),
        ],
        out_shape=[
            jax.ShapeDtypeStruct((N, 4 * D_RET), F32),
            jax.ShapeDtypeStruct((N, D_NSA), F32),
            jax.ShapeDtypeStruct((B, 2 * NSA_KV_HEADS, T // CMP_STRIDE, CMP_STRIDE * HEAD_DIM), F32),
            jax.ShapeDtypeStruct((B, NSA_KV_HEADS, T, PAIR), BF16),
            jax.ShapeDtypeStruct((B, NSA_KV_HEADS, T, PAIR), BF16),
            jax.ShapeDtypeStruct((B, nt, D_NSA, LANES), F32),
            jax.ShapeDtypeStruct((B, nt, 2 * NSA_KV_HEADS * V_ROWS, LANES), BF16),
            jax.ShapeDtypeStruct((B, nt, 32, LANES), F32),
        ],
        scratch_shapes=[pltpu.VMEM((2 * D_KV // LANES, PROJ_TM, LANES), F32)],
        compiler_params=pltpu.CompilerParams(dimension_semantics=("parallel",), vmem_limit_bytes=VMEM_LIMIT),
        name="proj",
    )(x2, nw, w_ret, w_ng, w_ckv, w_skwk, wt_q, wt_v, wt_g, knw)


def _ret_kernel(q_ref, k_ref, v_ref, g_ref, cos_ref, sin_ref, dec_ref, zeta_ref, xi_ref, gch_ref, nw_ref,
                o_ref, state_ref):
    @pl.when(pl.program_id(1) == 0)
    def _():
        state_ref[...] = jnp.zeros_like(state_ref)

    lane = lax.broadcasted_iota(jnp.int32, (1, PAIR), 1)
    q_head = (lane >> (HEAD_SHIFT - 1)) & 1
    v_head = lane >> HEAD_SHIFT
    q_mask = [jnp.where(q_head == h, 1.0, 0.0).astype(BF16) for h in (0, 1)]
    v_mask = [jnp.where(v_head == h, 1.0, 0.0).astype(BF16) for h in (0, 1)]
    row_qh = (lax.broadcasted_iota(jnp.int32, (PAIR, PAIR), 0) >> (HEAD_SHIFT - 1)) & 1
    row_vh = lax.broadcasted_iota(jnp.int32, (PAIR, PAIR), 0) >> HEAD_SHIFT
    col_vh = lax.broadcasted_iota(jnp.int32, (PAIR, PAIR), 1) >> HEAD_SHIFT
    same_head_kv = jnp.where(row_qh == col_vh, 1.0, 0.0)
    head_mean = jnp.where(row_vh == col_vh, 1.0 / HEAD_DIM, 0.0).astype(BF16)

    pairs = range(RET_HEADS // 2)
    mean2 = jnp.concatenate([head_mean, head_mean], axis=0)
    for c in range(RET_TC // RET_CHUNK):
        sl = pl.ds(c * RET_CHUNK, RET_CHUNK)
        cos = cos_ref[sl, :]
        sin = sin_ref[sl, :]
        cols = [slice(p * PAIR, (p + 1) * PAIR) for p in pairs]
        qb, kb, vb, vzb = [], [], [], []
        for p in pairs:
            q = q_ref[0, sl, cols[p]]
            k = k_ref[0, sl, cols[p]]
            v = v_ref[0, sl, cols[p]]
            qb.append((q * cos + pltpu.roll(q, HEAD_DIM, 1) * sin).astype(BF16))
            kb.append(((k * cos + pltpu.roll(k, HEAD_DIM, 1) * sin) * QK_SCALE).astype(BF16))
            vb.append(v.astype(BF16))
            vzb.append((v * zeta_ref[p]).astype(BF16))
        states = [state_ref[p] for p in pairs]
        kk = [jnp.concatenate([kb[p] * q_mask[h] for h in (0, 1)], axis=0) for p in pairs]
        s = [lax.dot_general(qb[p], kk[p], NT_DIMS, preferred_element_type=F32) for p in pairs]
        o_cross = [jnp.dot(qb[p], states[p].astype(BF16), preferred_element_type=F32) for p in pairs]
        kv = [lax.dot_general(kb[p], vzb[p], TN_DIMS, preferred_element_type=F32) for p in pairs]
        sb = [(s[p] * dec_ref[p]).astype(BF16) for p in pairs]
        vv = [jnp.concatenate([vb[p] * v_mask[h] for h in (0, 1)], axis=0) for p in pairs]
        o = [jnp.dot(sb[p], vv[p], preferred_element_type=F32) + o_cross[p] * xi_ref[p] for p in pairs]
        for p in pairs:
            state_ref[p] = states[p] * gch_ref[p] + kv[p] * same_head_kv
        o2 = [o[p] * o[p] for p in pairs]
        o2_hi = [o2[p].astype(BF16) for p in pairs]
        o2_hl = [jnp.concatenate([o2_hi[p], (o2[p] - o2_hi[p].astype(F32)).astype(BF16)], axis=1) for p in pairs]
        ms = [jnp.dot(o2_hl[p], mean2, preferred_element_type=F32) for p in pairs]
        for p in pairs:
            g = g_ref[0, sl, cols[p]]
            y = o[p] * lax.rsqrt(ms[p] + EPS) * nw_ref[p]
            o_ref[0, sl, cols[p]] = (y * (g * _sigmoid(g))).astype(BF16)


def _retention(ret3, cos, sin, dec, zeta, xi, gch, nw, B, T):
    npair = RET_HEADS // 2
    blk = lambda col: pl.BlockSpec((1, RET_TC, D_RET), lambda b, i: (b, i, col))
    tab = pl.BlockSpec((RET_TC, PAIR), lambda b, i: (i, 0))
    whole = lambda a: pl.BlockSpec(a.shape, lambda b, i: (0,) * a.ndim)
    return pl.pallas_call(
        _ret_kernel,
        grid=(B, T // RET_TC),
        in_specs=[blk(0), blk(1), blk(2), blk(3), tab, tab, whole(dec), whole(zeta), whole(xi), whole(gch), whole(nw)],
        out_specs=pl.BlockSpec((1, RET_TC, D_RET), lambda b, i: (b, i, 0)),
        out_shape=jax.ShapeDtypeStruct((B, T, D_RET), BF16),
        scratch_shapes=[pltpu.VMEM((npair, PAIR, PAIR), F32)],
        compiler_params=pltpu.CompilerParams(dimension_semantics=("parallel", "arbitrary"),
                                             vmem_limit_bytes=VMEM_LIMIT),
        name="retention",
    )(ret3, ret3, ret3, ret3, cos, sin, dec, zeta, xi, gch, nw)


def _cmp_kernel(x_ref, pos_ref, w1_ref, w2_ref, w2t_ref, knw_ref, o_ref, ot_ref):
    is_key = pl.program_id(1) == 0
    half = CMP_STRIDE * HEAD_DIM
    for g in range(NSA_KV_HEADS):
        x = x_ref[0, g]
        a = jnp.dot((x + pos_ref[0, 0:1, :]).astype(BF16), w1_ref[0, :half, :], preferred_element_type=F32)
        b = jnp.dot((x + pos_ref[0, 1:2, :]).astype(BF16), w1_ref[0, half:, :], preferred_element_type=F32)
        hid = a + pltpu.roll(b, b.shape[0] - 1, 0)
        hid = (hid * _sigmoid(hid)).astype(BF16)
        out = jnp.dot(hid, w2_ref[0], preferred_element_type=F32)
        ms = jnp.mean(out * out, axis=-1, keepdims=True)
        normed = out * lax.rsqrt(ms + EPS) * knw_ref[...]
        o_ref[0, 0, g] = jnp.where(is_key, normed, out)
        ot_ref[0, 0, g] = lax.dot_general(w2t_ref[0], hid, NT_DIMS, preferred_element_type=F32)


def _compress(xc, pos, w1, w2, w2t, knw, B, ncb):
    return pl.pallas_call(
        _cmp_kernel,
        grid=(B, 2),
        in_specs=[
            pl.BlockSpec((1, NSA_KV_HEADS, ncb, CMP_STRIDE * HEAD_DIM), lambda b, s: (b, s, 0, 0)),
            pl.BlockSpec((1, 2, CMP_STRIDE * HEAD_DIM), lambda b, s: (s, 0, 0)),
            pl.BlockSpec((1, CMP_BLOCK * HEAD_DIM, CMP_HIDDEN), lambda b, s: (s, 0, 0)),
            pl.BlockSpec((1, CMP_HIDDEN, HEAD_DIM), lambda b, s: (s, 0, 0)),
            pl.BlockSpec((1, HEAD_DIM, CMP_HIDDEN), lambda b, s: (s, 0, 0)),
            pl.BlockSpec((1, HEAD_DIM), lambda b, s: (0, 0)),
        ],
        out_specs=[
            pl.BlockSpec((1, 1, NSA_KV_HEADS, ncb, HEAD_DIM), lambda b, s: (b, s, 0, 0, 0)),
            pl.BlockSpec((1, 1, NSA_KV_HEADS, HEAD_DIM, ncb), lambda b, s: (b, s, 0, 0, 0)),
        ],
        out_shape=[
            jax.ShapeDtypeStruct((B, 2, NSA_KV_HEADS, ncb, HEAD_DIM), F32),
            jax.ShapeDtypeStruct((B, 2, NSA_KV_HEADS, HEAD_DIM, ncb), F32),
        ],
        compiler_params=pltpu.CompilerParams(dimension_semantics=("parallel", "parallel"),
                                             vmem_limit_bytes=VMEM_LIMIT),
        name="compress",
    )(xc, pos, w1, w2, w2t, knw)


def _nsa_kernel(qt_ref, qtn_ref, glt_ref, ng_ref, ks_ref, kw_ref, vst_ref, vwt_ref, kc_ref, vct_ref, mt_ref, qnw_ref,
                bg_ref, o_ref, qp_ref, oc_ref, sa_ref, sb_ref, mxa_ref, mxb_ref, sw_ref, m_ref, acc_ref):
    qi = pl.program_id(1)
    t0 = qi * (NSA_QT * NSA_TQ)
    tiles = range(NSA_QT)
    chains = [(qt, g) for qt in tiles for g in range(NSA_KV_HEADS)]
    gq = NSA_GROUP * HEAD_DIM
    gg = 16
    ncb = kc_ref.shape[3]
    n_slc = mt_ref.shape[0]

    def select_masks(base_t0):
        n_idx = lax.broadcasted_iota(jnp.int32, (ncb, NSA_TQ), 0)
        jb = lax.broadcasted_iota(jnp.int32, (n_slc, NSA_TQ), 0)
        out = []
        for qt in tiles:
            tok = base_t0 + qt * NSA_TQ
            tok_c = tok + lax.broadcasted_iota(jnp.int32, (ncb, NSA_TQ), 1)
            cbias = _tile4(jnp.where((n_idx * CMP_STRIDE + (CMP_BLOCK - 1)) <= tok_c, 0.0, NEG))
            tok_row = tok + lax.broadcasted_iota(jnp.int32, (1, NSA_TQ), 1)
            has_block = _tile4(jnp.where(tok_row >= CMP_BLOCK - 1, 1.0, 0.0))
            tok_s = tok + lax.broadcasted_iota(jnp.int32, (n_slc, NSA_TQ), 1)
            valid_s = jb * SLC_BLOCK <= tok_s
            force = (jb == (tok_s >> SLC_SHIFT)) | (jb == 0)
            out.append((cbias, has_block, valid_s, force))
        return out

    def select_qnorm(src_ref, slot, c):
        qt, g = chains[c]
        cols = []
        for r in range(NSA_GROUP):
            q = src_ref[0, qt, g * gq + r * HEAD_DIM:g * gq + (r + 1) * HEAD_DIM, :]
            ms = jnp.mean(q * q, axis=0, keepdims=True)
            cols.append(q * lax.rsqrt(ms + EPS) * qnw_ref[...] * (QK_SCALE * LOG2E))
        qp_ref[slot, c, 0:HEAD_DIM, :] = jnp.concatenate(cols, axis=1).astype(BF16)

    def select_scores(slot, masks, c):
        qt, g = chains[c]
        return (jnp.dot(kc_ref[0, 0, g].astype(BF16), qp_ref[slot, c, 0:HEAD_DIM, :], preferred_element_type=F32)
                + masks[qt][0])

    def select_softmax(sc, masks, c):
        qt, g = chains[c]
        has_block = masks[qt][1]
        mc = jnp.max(sc, axis=0, keepdims=True)
        ec = jnp.exp2(sc - mc)
        lc = jnp.sum(ec, axis=0, keepdims=True)
        p = ec * (has_block / lc)
        ps = p[:, 0:NSA_TQ]
        for r in range(1, NSA_GROUP):
            ps = ps + p[:, r * NSA_TQ:(r + 1) * NSA_TQ]
        ps_hi = ps.astype(BF16)
        ps_lo = (ps - ps_hi.astype(F32)).astype(BF16)
        return p.astype(BF16), ps_hi, ps_lo

    def select_finish(parts, slot, masks, c):
        qt, g = chains[c]
        _, _, valid_s, force = masks[qt]
        pb, ps_hi, ps_lo = parts
        oc_ref[slot, c] = jnp.dot(vct_ref[0, 0, g].astype(BF16), pb, preferred_element_type=F32)
        imp = (jnp.dot(mt_ref[...], ps_hi, preferred_element_type=F32)
               + jnp.dot(mt_ref[...], ps_lo, preferred_element_type=F32))
        return jnp.where(valid_s, jnp.where(force, imp + FORCE_BONUS, imp), NEG)

    def select_rank(scores, valid_s, slot, n_live):
        if n_live <= SLC_TOPK:
            for c, (qt, g) in enumerate(chains):
                qp_ref[slot, c, HEAD_DIM:2 * HEAD_DIM, :] = _tile4(jnp.where(valid_s[qt], 0.0, NEG).astype(BF16))
            return
        sub = lax.broadcasted_iota(jnp.int32, (SUBLANES, NSA_TQ), 0)
        for c, (qt, g) in enumerate(chains):
            score = scores[c]
            blocks = [score[v * SUBLANES:(v + 1) * SUBLANES, :] for v in range(n_live // SUBLANES)]
            ranks = [jnp.zeros((SUBLANES, NSA_TQ), F32) for _ in blocks]
            for i in range(n_live):
                row = score[i:i + 1, :]
                for v, blk in enumerate(blocks):
                    if v * SUBLANES > i:
                        beats = row >= blk
                    elif (v + 1) * SUBLANES <= i:
                        beats = row > blk
                    else:
                        beats = (row > blk) | ((row >= blk) & (sub > i - v * SUBLANES))
                    ranks[v] = ranks[v] + jnp.where(beats, 1.0, 0.0)
            dead = [jnp.full((SUBLANES, NSA_TQ), float(n_slc), F32)] * ((n_slc - n_live) // SUBLANES)
            rank = jnp.concatenate(ranks + dead, axis=0)
            sel = (rank < float(SLC_TOPK)) & valid_s[qt]
            qp_ref[slot, c, HEAD_DIM:2 * HEAD_DIM, :] = _tile4(jnp.where(sel, 0.0, NEG).astype(BF16))

    step_tokens = NSA_QT * NSA_TQ

    def live_blocks(step):
        return ((step + 1) * step_tokens - 1) // SLC_BLOCK + 1

    all_chains = list(range(len(chains)))

    @pl.when(qi == 0)
    def _():
        masks0 = select_masks(0)
        scores0 = []
        for c in all_chains:
            select_qnorm(qt_ref, 0, c)
            scores0.append(select_finish(select_softmax(select_scores(0, masks0, c), masks0, c), 0, masks0, c))
        select_rank(scores0, [m[2] for m in masks0], 0, live_blocks(0))

    cur = qi % 2
    nxt = 1 - cur
    t_next = t0 + NSA_QT * NSA_TQ

    kt0, ks0, wbias = [], [], []
    c_minus_r = (lax.broadcasted_iota(jnp.int32, (WIN_KEYS, NSA_TQ), 1)
                 - lax.broadcasted_iota(jnp.int32, (WIN_KEYS, NSA_TQ), 0))
    for qt in tiles:
        kt0.append(jnp.maximum(qi * NSA_QT + qt - WIN_SIZE // NSA_TQ, 0))
        ks0.append(pl.multiple_of(kt0[qt] * NSA_TQ, NSA_TQ))
        delta = (t0 + qt * NSA_TQ - ks0[qt]) + c_minus_r
        in_window = lax.bitcast_convert_type(delta, jnp.uint32) < WIN_SIZE
        wbias.append(_tile4(jnp.where(in_window, 0.0, NEG)))

    def win_scores(c):
        qt, g = chains[c]
        sw = jnp.dot(kw_ref[0, g, pl.ds(ks0[qt], WIN_KEYS), 0:HEAD_DIM], qp_ref[cur, c, 0:HEAD_DIM, :],
                     preferred_element_type=F32) + wbias[qt]
        sw_ref[c] = sw
        return jnp.max(sw, axis=0, keepdims=True)

    def win_softmax(c, mw):
        return jnp.exp2(sw_ref[c] - mw).astype(BF16)

    def win_finish(c, ewb):
        qt, g = chains[c]
        vwt = jnp.concatenate([vwt_ref[0, kt0[qt] + j, g * V_ROWS:(g + 1) * V_ROWS, :]
                               for j in range(WIN_KEYS // LANES)], axis=1)
        ow_aug = jnp.dot(vwt, ewb, preferred_element_type=F32)
        return ow_aug[0:HEAD_DIM, :] * (1.0 / ow_aug[HEAD_DIM:HEAD_DIM + 1, :])

    vt_per_tile = SLC_TK // LANES

    def slc_scores(j, dst_ref, mx_ref, which=all_chains):
        kst = pl.multiple_of(j * SLC_TK, SLC_TK)
        for c in which:
            qt, g = chains[c]
            s = jnp.dot(ks_ref[0, g, pl.ds(kst, SLC_TK), :], qp_ref[cur, c], preferred_element_type=F32)
            dst_ref[c] = s
            mx_ref[c] = jnp.max(s, axis=0, keepdims=True)

    def slc_update(j, src_ref, mx_ref, causal, which=all_chains, visible=SLC_TK):
        if causal:
            band = _tile4(jnp.where(lax.broadcasted_iota(jnp.int32, (NSA_TQ, NSA_TQ), 0)
                                    <= lax.broadcasted_iota(jnp.int32, (NSA_TQ, NSA_TQ), 1), 0.0, NEG))
        for c in which:
            qt, g = chains[c]
            if causal:
                r0 = visible - step_tokens + qt * NSA_TQ
                rows = r0 + NSA_TQ
                s_band = src_ref[c, r0:rows, :] + band
                s = jnp.concatenate([src_ref[c, 0:r0, :], s_band], axis=0) if r0 > 0 else s_band
                tile_max = jnp.max(s, axis=0, keepdims=True)
            else:
                rows = SLC_TK
                s = src_ref[c]
                tile_max = mx_ref[c]
            m_old = m_ref[c]
            m_new = jnp.maximum(m_old, tile_max)
            alpha = jnp.exp2(m_old - m_new)
            eb = jnp.exp2(s - m_new).astype(BF16)
            vt = jnp.concatenate([vst_ref[0, j * vt_per_tile + jj, g * V_ROWS:(g + 1) * V_ROWS, :]
                                  for jj in range(rows // LANES)], axis=1)
            acc_ref[c] = alpha * acc_ref[c] + jnp.dot(vt, eb, preferred_element_type=F32)
            m_ref[c] = m_new

    masks = select_masks(t_next)
    n_chains = len(chains)
    stage_a, stage_b = {}, {}
    sel_scores, ow_t = [None] * n_chains, [None] * n_chains

    def run_a(c):
        mw = win_scores(c)
        select_qnorm(qtn_ref, nxt, c)
        sc = select_scores(nxt, masks, c)
        slc_scores(0, sa_ref, mxa_ref, [c])
        stage_a[c] = (sc, mw)

    def run_b(c):
        sc, mw = stage_a.pop(c)
        stage_b[c] = (select_softmax(sc, masks, c), win_softmax(c, mw))

    def run_c(c):
        parts, ewb = stage_b.pop(c)
        sel_scores[c] = select_finish(parts, nxt, masks, c)
        ow_t[c] = win_finish(c, ewb)

    for step in range(n_chains + 2):
        if step < n_chains:
            run_a(step)
        if 1 <= step <= n_chains:
            run_b(step - 1)
        if step >= 2:
            run_c(step - 2)
    sel_valid = [m[2] for m in masks]

    need = live_blocks(qi + 1)
    bounds = list(range(SLC_TOPK, n_slc + 1, SUBLANES))
    for lo, hi in zip([0] + bounds[:-1], bounds):
        in_range = (need > lo) if hi == bounds[-1] else ((need > lo) & (need <= hi))
        pl.when(in_range)(functools.partial(select_rank, sel_scores, sel_valid, nxt, hi))

    m_ref[...] = jnp.full(m_ref.shape, NEG, F32)
    acc_ref[...] = jnp.zeros(acc_ref.shape, F32)
    n_full = t0 // SLC_TK

    def pair(jj, carry):
        j = 2 * jj
        for c in all_chains:
            slc_scores(j + 1, sb_ref, mxb_ref, [c])
            slc_update(j, sa_ref, mxa_ref, False, [c])
        for c in all_chains:
            slc_scores(j + 2, sa_ref, mxa_ref, [c])
            slc_update(j + 1, sb_ref, mxb_ref, False, [c])
        return carry

    lax.fori_loop(0, n_full // 2, pair, 0)

    visible = t0 - n_full * SLC_TK + step_tokens

    def tail(odd, rows):
        if odd:
            for c in all_chains:
                slc_scores(n_full, sb_ref, mxb_ref, [c])
                slc_update(n_full - 1, sa_ref, mxa_ref, False, [c])
            slc_update(n_full, sb_ref, mxb_ref, True, visible=rows)
        else:
            slc_update(n_full, sa_ref, mxa_ref, True, visible=rows)

    for rows in range(step_tokens, SLC_TK + 1, step_tokens):
        for odd in (False, True):
            parity = (n_full % 2 == 1) if odd else (n_full % 2 == 0)
            pl.when((visible == rows) & parity)(functools.partial(tail, odd, rows))

    for qt in tiles:
        gates = _sigmoid(glt_ref[0, qt] + bg_ref[...])
        outs = []
        for g in range(NSA_KV_HEADS):
            c = qt * NSA_KV_HEADS + g
            os_t = acc_ref[c, 0:HEAD_DIM, :] * (1.0 / acc_ref[c, HEAD_DIM:HEAD_DIM + 1, :])
            oc_t = oc_ref[cur, c]
            for r in range(NSA_GROUP):
                sl = slice(r * NSA_TQ, (r + 1) * NSA_TQ)
                row = g * gg + r
                outs.append(gates[row:row + 1, :] * oc_t[:, sl]
                            + gates[row + NSA_GROUP:row + NSA_GROUP + 1, :] * os_t[:, sl]
                            + gates[row + 2 * NSA_GROUP:row + 2 * NSA_GROUP + 1, :] * ow_t[c][:, sl])
        o_tok = jnp.concatenate(outs, axis=0).T
        rows = pl.ds(qt * NSA_TQ, NSA_TQ)
        ng = ng_ref[0, rows, :]
        o_ref[0, rows, :] = (o_tok * (ng * _sigmoid(ng))).astype(BF16)


def _nsa(nqt, glt, ng3, ks, kw, vt, kc, vct, mt, qnw, bg, B, T):
    nt = T // LANES
    ncb = kc.shape[3]
    G = NSA_KV_HEADS
    NC = NSA_QT * G
    ncols = NSA_GROUP * NSA_TQ
    steps = T // (NSA_QT * NSA_TQ)
    return pl.pallas_call(
        _nsa_kernel,
        grid=(B, steps),
        in_specs=[
            pl.BlockSpec((1, NSA_QT, D_NSA, LANES), lambda b, i: (b, 0, 0, 0)),
            pl.BlockSpec((1, NSA_QT, D_NSA, LANES), lambda b, i: (b, jnp.minimum(i + 1, steps - 1), 0, 0)),
            pl.BlockSpec((1, NSA_QT, 16 * G, LANES), lambda b, i: (b, i, 0, 0)),
            pl.BlockSpec((1, NSA_QT * NSA_TQ, D_NSA), lambda b, i: (b, i, 0)),
            pl.BlockSpec((1, G, T, PAIR), lambda b, i: (b, 0, 0, 0)),
            pl.BlockSpec((1, G, T, PAIR), lambda b, i: (b, 0, 0, 0)),
            pl.BlockSpec((1, nt, G * V_ROWS, LANES), lambda b, i: (b, 0, 0, 0)),
            pl.BlockSpec((1, nt, G * V_ROWS, LANES), lambda b, i: (b, 0, 1, 0)),
            pl.BlockSpec((1, 1, G, ncb, HEAD_DIM), lambda b, i: (b, 0, 0, 0, 0)),
            pl.BlockSpec((1, 1, G, HEAD_DIM, ncb), lambda b, i: (b, 1, 0, 0, 0)),
            pl.BlockSpec(mt.shape, lambda b, i: (0, 0)),
            pl.BlockSpec((HEAD_DIM, 1), lambda b, i: (0, 0)),
            pl.BlockSpec((16 * G, 1), lambda b, i: (0, 0)),
        ],
        out_specs=pl.BlockSpec((1, NSA_QT * NSA_TQ, D_NSA), lambda b, i: (b, i, 0)),
        out_shape=jax.ShapeDtypeStruct((B, T, D_NSA), BF16),
        scratch_shapes=[
            pltpu.VMEM((2, NC, 2 * HEAD_DIM, ncols), BF16),
            pltpu.VMEM((2, NC, HEAD_DIM, ncols), F32),
            pltpu.VMEM((NC, SLC_TK, ncols), F32),
            pltpu.VMEM((NC, SLC_TK, ncols), F32),
            pltpu.VMEM((NC, 1, ncols), F32),
            pltpu.VMEM((NC, 1, ncols), F32),
            pltpu.VMEM((NC, WIN_KEYS, ncols), F32),
            pltpu.VMEM((NC, 1, ncols), F32),
            pltpu.VMEM((NC, V_ROWS, ncols), F32),
        ],
        compiler_params=pltpu.CompilerParams(dimension_semantics=("parallel", "arbitrary"),
                                             vmem_limit_bytes=VMEM_LIMIT),
        name="nsa",
    )(nqt, nqt, glt, ng3, ks, kw, vt, vt, kc, vct, mt, qnw, bg)


def _out_kernel(x_ref, yr_ref, yn_ref, wr_ref, wn_ref, o_ref):
    o_ref[...] = (x_ref[...]
                  + jnp.dot(yr_ref[...], wr_ref[...], preferred_element_type=F32)
                  + jnp.dot(yn_ref[...], wn_ref[...], preferred_element_type=F32))


def _outproj(x2, yr, yn, wr, wn):
    N = x2.shape[0]
    row = lambda i: (i, 0)
    const = lambda i: (0, 0)
    return pl.pallas_call(
        _out_kernel,
        grid=(N // OUT_TM,),
        in_specs=[pl.BlockSpec((OUT_TM, D_MODEL), row), pl.BlockSpec((OUT_TM, D_RET), row),
                  pl.BlockSpec((OUT_TM, D_NSA), row), pl.BlockSpec(wr.shape, const), pl.BlockSpec(wn.shape, const)],
        out_specs=pl.BlockSpec((OUT_TM, D_MODEL), row),
        out_shape=jax.ShapeDtypeStruct((N, D_MODEL), F32),
        compiler_params=pltpu.CompilerParams(dimension_semantics=("parallel",), vmem_limit_bytes=VMEM_LIMIT),
        name="outproj",
    )(x2, yr, yn, wr, wn)


@functools.lru_cache(maxsize=None)
def _tables(T):
    half = HEAD_DIM // 2
    inv = ROPE_THETA ** (-np.arange(half, dtype=np.float64) / half)
    ang = np.arange(T, dtype=np.float64)[:, None] * inv[None, :]
    cos = np.concatenate([np.cos(ang)] * 4, axis=1).astype(np.float32)
    sin = np.concatenate([-np.sin(ang), -np.sin(ang), np.sin(ang), np.sin(ang)], axis=1).astype(np.float32)

    C = RET_CHUNK
    log_g = np.log1p(-np.exp2(-5.0 - np.arange(RET_HEADS, dtype=np.float64)))
    pos = np.arange(C, dtype=np.float64)
    diff = pos[:, None] - pos[None, :]
    decay = np.where(diff >= 0, np.exp(log_g[:, None, None] * np.maximum(diff, 0.0)), 0.0)
    zeta = np.exp(log_g[:, None] * (C - 1.0 - pos))
    xi = np.exp(log_g[:, None] * (pos + 1.0))
    g_chunk = np.exp(log_g * C)
    npair = RET_HEADS // 2

    def pair_lanes(a):
        return np.repeat(a.reshape(npair, 2, C).transpose(0, 2, 1), HEAD_DIM, axis=2).astype(np.float32)

    dec = decay.reshape(npair, 2, C, C).transpose(0, 2, 1, 3).reshape(npair, C, 2 * C).astype(np.float32)
    gch = np.repeat(g_chunk.reshape(npair, 1, 2), HEAD_DIM, axis=2).astype(np.float32)

    n_cmp = (T - CMP_BLOCK) // CMP_STRIDE + 1
    ncb = T // CMP_STRIDE
    p = np.arange(n_cmp)[:, None] * CMP_STRIDE + np.arange(CMP_BLOCK)[None, :]
    blk = p // SLC_BLOCK
    M = (blk[:, :, None] == np.arange(T // SLC_BLOCK)[None, None, :]).mean(axis=1)
    mt = np.zeros((T // SLC_BLOCK, ncb), np.float32)
    mt[:, :n_cmp] = M.T
    return cos, sin, dec, pair_lanes(zeta), pair_lanes(xi), gch, mt


def kernel(x, norm_w, w_in, ret_norm_w, q_norm_w, k_norm_cmp, k_norm_slc, k_norm_win, cmp_pos_k, cmp_w1_k, cmp_w2_k,
           cmp_pos_v, cmp_w1_v, cmp_w2_v, b_gate, w_out):
    B, T, D = x.shape
    depth = norm_w.shape[0]
    cos, sin, dec, zeta, xi, gch, mt = _tables(T)
    ncb = T // CMP_STRIDE
    half = CMP_STRIDE * HEAD_DIM
    gate_src = np.zeros((NSA_KV_HEADS, 16), np.int32)
    gate_ok = np.zeros((NSA_KV_HEADS, 16), bool)
    for g in range(NSA_KV_HEADS):
        for br in range(N_BRANCH):
            for r in range(NSA_GROUP):
                gate_src[g, br * NSA_GROUP + r] = br * NSA_HEADS + g * NSA_GROUP + r
                gate_ok[g, br * NSA_GROUP + r] = True
    gate_src = gate_src.reshape(-1)
    gate_ok = gate_ok.reshape(-1)

    x2 = x.reshape(B * T, D)
    for layer in range(depth):
        w = w_in[layer].astype(BF16)
        o_ng = 4 * D_RET + D_NSA
        o_kv = o_ng + D_NSA
        quarter = HEAD_DIM // 2
        pair_perm = np.concatenate([np.arange(quarter), HEAD_DIM + np.arange(quarter),
                                    quarter + np.arange(quarter), HEAD_DIM + quarter + np.arange(quarter)])
        qk_perm = np.concatenate([p * PAIR + pair_perm for p in range(RET_HEADS // 2)])
        w_ret = jnp.concatenate([w[:, :D_RET][:, qk_perm], w[:, D_RET:2 * D_RET][:, qk_perm],
                                 w[:, 2 * D_RET:4 * D_RET]], axis=1)
        wt_q = w[:, 4 * D_RET:o_ng].T
        w_ng = w[:, o_ng:o_kv]
        w_ckv = w[:, o_kv:o_kv + 2 * D_KV]
        w_skwk = jnp.concatenate([w[:, o_kv + 2 * D_KV:o_kv + 3 * D_KV], w[:, o_kv + 4 * D_KV:o_kv + 5 * D_KV]], axis=1)
        wt_v = jnp.concatenate([w[:, o_kv + 3 * D_KV:o_kv + 4 * D_KV], w[:, o_kv + 5 * D_KV:o_kv + 6 * D_KV]], axis=1).T
        w_gl = w[:, o_kv + 6 * D_KV:]
        wt_g = jnp.where(gate_ok[:, None], w_gl.T[gate_src], jnp.zeros((), BF16))
        bg = jnp.where(gate_ok, b_gate[layer][gate_src], 0.0).reshape(-1, 1)

        knw = jnp.stack([jnp.tile(k_norm_slc[layer], 2), jnp.tile(k_norm_win[layer], 2)])
        ret, ng, xc, ks, kw, nqt, vt, glt = _proj(x2, norm_w[layer].reshape(1, D), w_ret, w_ng, w_ckv, w_skwk,
                                                  wt_q, wt_v, wt_g, knw, B, T)

        nw_pair = ret_norm_w[layer].reshape(RET_HEADS // 2, 1, PAIR)
        y_ret = _retention(ret.reshape(B, T, 4 * D_RET), jnp.asarray(cos), jnp.asarray(sin), jnp.asarray(dec),
                           jnp.asarray(zeta), jnp.asarray(xi), jnp.asarray(gch), nw_pair, B, T)

        pos =jnp.stack([cmp_pos_k[layer], cmp_pos_v[layer]]).reshape(2, 2, half)
        w1 = jnp.stack([cmp_w1_k[layer], cmp_w1_v[layer]]).astype(BF16)
        w2 = jnp.stack([cmp_w2_k[layer], cmp_w2_v[layer]]).astype(BF16)
        w2t = jnp.swapaxes(w2, 1, 2)
        kc, vct = _compress(xc, pos, w1, w2, w2t, k_norm_cmp[layer].reshape(1, HEAD_DIM), B, ncb)

        y_nsa = _nsa(nqt, glt, ng.reshape(B, T, D_NSA), ks, kw, vt, kc, vct, jnp.asarray(mt).astype(BF16),
                     q_norm_w[layer].reshape(HEAD_DIM, 1), bg, B, T)

        wo = w_out[layer].astype(BF16)
        x2 = _outproj(x2, y_ret.reshape(B * T, D_RET), y_nsa.reshape(B * T, D_NSA), wo[:D_RET], wo[D_RET:])
    return x2.reshape(B, T, D)
```

```python
import functools

import numpy as np
import jax
import jax.numpy as jnp
from jax import lax
from jax.experimental import pallas as pl
from jax.experimental.pallas import tpu as pltpu

F32 = jnp.float32
BF16 = jnp.bfloat16

D_MODEL = 1024
HEAD_DIM = 64
HEAD_SHIFT = 6
RET_HEADS = 8
NSA_HEADS = 8
NSA_KV_HEADS = 2
NSA_GROUP = NSA_HEADS // NSA_KV_HEADS
D_RET = RET_HEADS * HEAD_DIM
D_NSA = NSA_HEADS * HEAD_DIM
D_KV = NSA_KV_HEADS * HEAD_DIM
N_BRANCH = 3
RET_CHUNK = 128
ROPE_THETA = 10000.0
CMP_BLOCK = 32
CMP_STRIDE = 16
CMP_HIDDEN = 256
SLC_BLOCK = 64
SLC_SHIFT = 6
SLC_TOPK = 16
WIN_SIZE = 512
EPS = 1e-6
NEG = -1e30
FORCE_BONUS = 1e4
QK_SCALE = HEAD_DIM ** -0.5
LOG2E = 1.4426950408889634
V_ROWS = HEAD_DIM + 16

LANES = 128
SUBLANES = 8
PAIR = 2 * HEAD_DIM
VMEM_LIMIT = 48 * 1024 * 1024

PROJ_TM = 512
RET_TC = 1024
NSA_TQ = 128
NSA_QT = 2
SLC_TK = 512
WIN_KEYS = WIN_SIZE + NSA_TQ
OUT_TM = 1024

NT_DIMS = (((1,), (1,)), ((), ()))
TN_DIMS = (((0,), (0,)), ((), ()))


def _sigmoid(x):
    return 1.0 / (1.0 + jnp.exp(-x))


def _tile4(a):
    return jnp.concatenate([a, a, a, a], axis=1)


def _proj_kernel(steps_per_batch, x_ref, nw_ref, w_ret_ref, w_ng_ref, w_ckv_ref, w_skwk_ref, wt_q_ref, wt_v_ref,
                 wt_g_ref, knw_ref, ret_ref, ng_ref, xc_ref, ks_ref, kw_ref, nqt_ref, vt_ref, glt_ref, ckv_scr):
    x = x_ref[...]
    ms = jnp.mean(x * x, axis=-1, keepdims=True)
    h = (x * lax.rsqrt(ms + EPS) * nw_ref[...]).astype(BF16)
    ret_ref[...] = jnp.dot(h, w_ret_ref[...], preferred_element_type=F32)
    ng_ref[...] = jnp.dot(h, w_ng_ref[...], preferred_element_type=F32)

    ckv = jnp.dot(h, w_ckv_ref[...], preferred_element_type=F32)
    for half in range(2 * D_KV // LANES):
        ckv_scr[half] = ckv[:, half * LANES:(half + 1) * LANES]
    for l in range(CMP_STRIDE):
        for half in range(2 * D_KV // LANES):
            rows = ckv_scr[half, pl.ds(l, PROJ_TM // CMP_STRIDE, stride=CMP_STRIDE), :]
            for s in range(LANES // HEAD_DIM):
                xc_ref[0, half * (LANES // HEAD_DIM) + s, :, l * HEAD_DIM:(l + 1) * HEAD_DIM] = (
                    rows[:, s * HEAD_DIM:(s + 1) * HEAD_DIM])

    skwk = jnp.dot(h, w_skwk_ref[...], preferred_element_type=F32)
    lane = lax.broadcasted_iota(jnp.int32, (1, PAIR), 1)
    head0 = lane < HEAD_DIM
    m0 = jnp.where(head0, 1.0, 0.0)
    m1 = 1.0 - m0
    t_start = (pl.program_id(0) % steps_per_batch) * PROJ_TM
    tok = t_start + lax.broadcasted_iota(jnp.int32, (PROJ_TM, PAIR), 0)
    col = lax.broadcasted_iota(jnp.int32, (PROJ_TM, PAIR), 1)
    indicator = jnp.where((tok >> SLC_SHIFT) == col - HEAD_DIM, 1.0, 0.0)

    def pair_normed(t, w):
        t2 = t * t
        ms0 = jnp.sum(t2 * m0, axis=-1, keepdims=True)
        ms1 = jnp.sum(t2 * m1, axis=-1, keepdims=True)
        return t * lax.rsqrt(jnp.where(head0, ms0, ms1) * (1.0 / HEAD_DIM) + EPS) * w

    ns = pair_normed(skwk[:, :PAIR], knw_ref[0:1, :])
    nwin = pair_normed(skwk[:, PAIR:], knw_ref[1:2, :])
    for g in range(NSA_KV_HEADS):
        s_g = ns if g == 0 else pltpu.roll(ns, HEAD_DIM, 1)
        w_g = nwin if g == 0 else pltpu.roll(nwin, HEAD_DIM, 1)
        ks_ref[0, g] = jnp.where(head0, s_g, indicator).astype(BF16)
        kw_ref[0, g] = jnp.where(head0, w_g, 0.0).astype(BF16)

    qt =lax.dot_general(wt_q_ref[...], h, NT_DIMS, preferred_element_type=F32)
    vt = lax.dot_general(wt_v_ref[...], h, NT_DIMS, preferred_element_type=F32)
    gt = lax.dot_general(wt_g_ref[...], h, NT_DIMS, preferred_element_type=F32)
    pad_row = lax.broadcasted_iota(jnp.int32, (V_ROWS - HEAD_DIM, LANES), 0)
    ones_pad = jnp.where(pad_row == 0, 1.0, 0.0).astype(BF16)
    for j in range(PROJ_TM // LANES):
        sl = slice(j * LANES, (j + 1) * LANES)
        nqt_ref[0, j] = qt[:, sl]
        for blk in range(2 * NSA_KV_HEADS):
            vt_ref[0, j, blk * V_ROWS:blk * V_ROWS + HEAD_DIM, :] = (
                vt[blk * HEAD_DIM:(blk + 1) * HEAD_DIM, sl].astype(BF16))
            vt_ref[0, j, blk * V_ROWS + HEAD_DIM:(blk + 1) * V_ROWS, :] = ones_pad
        glt_ref[0, j] = gt[:, sl]


def _proj(x2, nw, w_ret, w_ng, w_ckv, w_skwk, wt_q, wt_v, wt_g, knw, B, T):
    N = B * T
    tpb = T // PROJ_TM
    sub = PROJ_TM // LANES
    nt = T // LANES
    const = lambda i: (0, 0)
    row = lambda i: (i, 0)
    trn = lambda i: (i // tpb, i % tpb, 0, 0)
    tokm = lambda i: (i // tpb, 0, i % tpb, 0)
    return pl.pallas_call(
        functools.partial(_proj_kernel, tpb),
        grid=(N // PROJ_TM,),
        in_specs=[
            pl.BlockSpec((PROJ_TM, D_MODEL), row),
            pl.BlockSpec((1, D_MODEL), const),
            pl.BlockSpec(w_ret.shape, const),
            pl.BlockSpec(w_ng.shape, const),
            pl.BlockSpec(w_ckv.shape, const),
            pl.BlockSpec(w_skwk.shape, const),
            pl.BlockSpec(wt_q.shape, const),
            pl.BlockSpec(wt_v.shape, const),
            pl.BlockSpec(wt_g.shape, const),
            pl.BlockSpec(knw.shape, const),
        ],
        out_specs=[
            pl.BlockSpec((PROJ_TM, 4 * D_RET), row),
            pl.BlockSpec((PROJ_TM, D_NSA), row),
            pl.BlockSpec((1, 2 * NSA_KV_HEADS, PROJ_TM // CMP_STRIDE, CMP_STRIDE * HEAD_DIM), tokm),
            pl.BlockSpec((1, NSA_KV_HEADS, PROJ_TM, PAIR), tokm),
            pl.BlockSpec((1, NSA_KV_HEADS, PROJ_TM, PAIR), tokm),
            pl.BlockSpec((1, sub, D_NSA, LANES), trn),
            pl.BlockSpec((1, sub, 2 * NSA_KV_HEADS * V_ROWS, LANES), trn),
            pl.BlockSpec((1, sub, 32, LANES), trn),
        ],
        out_shape=[
            jax.ShapeDtypeStruct((N, 4 * D_RET), F32),
            jax.ShapeDtypeStruct((N, D_NSA), F32),
            jax.ShapeDtypeStruct((B, 2 * NSA_KV_HEADS, T // CMP_STRIDE, CMP_STRIDE * HEAD_DIM), F32),
            jax.ShapeDtypeStruct((B, NSA_KV_HEADS, T, PAIR), BF16),
            jax.ShapeDtypeStruct((B, NSA_KV_HEADS, T, PAIR), BF16),
            jax.ShapeDtypeStruct((B, nt, D_NSA, LANES), F32),
            jax.ShapeDtypeStruct((B, nt, 2 * NSA_KV_HEADS * V_ROWS, LANES), BF16),
            jax.ShapeDtypeStruct((B, nt, 32, LANES), F32),
        ],
        scratch_shapes=[pltpu.VMEM((2 * D_KV // LANES, PROJ_TM, LANES), F32)],
        compiler_params=pltpu.CompilerParams(dimension_semantics=("parallel",), vmem_limit_bytes=VMEM_LIMIT),
        name="proj",
    )(x2, nw, w_ret, w_ng, w_ckv, w_skwk, wt_q, wt_v, wt_g, knw)


def _ret_kernel(q_ref, k_ref, v_ref, g_ref, cos_ref, sin_ref, dec_ref, zeta_ref, xi_ref, gch_ref, nw_ref,
                o_ref, state_ref):
    @pl.when(pl.program_id(1) == 0)
    def _():
        state_ref[...] = jnp.zeros_like(state_ref)

    lane = lax.broadcasted_iota(jnp.int32, (1, PAIR), 1)
    q_head = (lane >> (HEAD_SHIFT - 1)) & 1
    v_head = lane >> HEAD_SHIFT
    q_mask = [jnp.where(q_head == h, 1.0, 0.0).astype(BF16) for h in (0, 1)]
    v_mask = [jnp.where(v_head == h, 1.0, 0.0).astype(BF16) for h in (0, 1)]
    row_qh = (lax.broadcasted_iota(jnp.int32, (PAIR, PAIR), 0) >> (HEAD_SHIFT - 1)) & 1
    row_vh = lax.broadcasted_iota(jnp.int32, (PAIR, PAIR), 0) >> HEAD_SHIFT
    col_vh = lax.broadcasted_iota(jnp.int32, (PAIR, PAIR), 1) >> HEAD_SHIFT
    same_head_kv = jnp.where(row_qh == col_vh, 1.0, 0.0)
    head_mean = jnp.where(row_vh == col_vh, 1.0 / HEAD_DIM, 0.0).astype(BF16)

    pairs = range(RET_HEADS // 2)
    mean2 = jnp.concatenate([head_mean, head_mean], axis=0)
    for c in range(RET_TC // RET_CHUNK):
        sl = pl.ds(c * RET_CHUNK, RET_CHUNK)
        cos = cos_ref[sl, :]
        sin = sin_ref[sl, :]
        cols = [slice(p * PAIR, (p + 1) * PAIR) for p in pairs]
        qb, kb, vb, vzb = [], [], [], []
        for p in pairs:
            q = q_ref[0, sl, cols[p]]
            k = k_ref[0, sl, cols[p]]
            v = v_ref[0, sl, cols[p]]
            qb.append((q * cos + pltpu.roll(q, HEAD_DIM, 1) * sin).astype(BF16))
            kb.append(((k * cos + pltpu.roll(k, HEAD_DIM, 1) * sin) * QK_SCALE).astype(BF16))
            vb.append(v.astype(BF16))
            vzb.append((v * zeta_ref[p]).astype(BF16))
        states = [state_ref[p] for p in pairs]
        kk = [jnp.concatenate([kb[p] * q_mask[h] for h in (0, 1)], axis=0) for p in pairs]
        s = [lax.dot_general(qb[p], kk[p], NT_DIMS, preferred_element_type=F32) for p in pairs]
        o_cross = [jnp.dot(qb[p], states[p].astype(BF16), preferred_element_type=F32) for p in pairs]
        kv = [lax.dot_general(kb[p], vzb[p], TN_DIMS, preferred_element_type=F32) for p in pairs]
        sb = [(s[p] * dec_ref[p]).astype(BF16) for p in pairs]
        vv = [jnp.concatenate([vb[p] * v_mask[h] for h in (0, 1)], axis=0) for p in pairs]
        o = [jnp.dot(sb[p], vv[p], preferred_element_type=F32) + o_cross[p] * xi_ref[p] for p in pairs]
        for p in pairs:
            state_ref[p] = states[p] * gch_ref[p] + kv[p] * same_head_kv
        o2 = [o[p] * o[p] for p in pairs]
        o2_hi = [o2[p].astype(BF16) for p in pairs]
        o2_hl = [jnp.concatenate([o2_hi[p], (o2[p] - o2_hi[p].astype(F32)).astype(BF16)], axis=1) for p in pairs]
        ms = [jnp.dot(o2_hl[p], mean2, preferred_element_type=F32) for p in pairs]
        for p in pairs:
            g = g_ref[0, sl, cols[p]]
            y = o[p] * lax.rsqrt(ms[p] + EPS) * nw_ref[p]
            o_ref[0, sl, cols[p]] = (y * (g * _sigmoid(g))).astype(BF16)


def _retention(ret3, cos, sin, dec, zeta, xi, gch, nw, B, T):
    npair = RET_HEADS // 2
    blk = lambda col: pl.BlockSpec((1, RET_TC, D_RET), lambda b, i: (b, i, col))
    tab = pl.BlockSpec((RET_TC, PAIR), lambda b, i: (i, 0))
    whole = lambda a: pl.BlockSpec(a.shape, lambda b, i: (0,) * a.ndim)
    return pl.pallas_call(
        _ret_kernel,
        grid=(B, T // RET_TC),
        in_specs=[blk(0), blk(1), blk(2), blk(3), tab, tab, whole(dec), whole(zeta), whole(xi), whole(gch), whole(nw)],
        out_specs=pl.BlockSpec((1, RET_TC, D_RET), lambda b, i: (b, i, 0)),
        out_shape=jax.ShapeDtypeStruct((B, T, D_RET), BF16),
        scratch_shapes=[pltpu.VMEM((npair, PAIR, PAIR), F32)],
        compiler_params=pltpu.CompilerParams(dimension_semantics=("parallel", "arbitrary"),
                                             vmem_limit_bytes=VMEM_LIMIT),
        name="retention",
    )(ret3, ret3, ret3, ret3, cos, sin, dec, zeta, xi, gch, nw)


def _cmp_kernel(x_ref, pos_ref, w1_ref, w2_ref, w2t_ref, knw_ref, o_ref, ot_ref):
    is_key = pl.program_id(1) == 0
    half = CMP_STRIDE * HEAD_DIM
    for g in range(NSA_KV_HEADS):
        x = x_ref[0, g]
        a = jnp.dot((x + pos_ref[0, 0:1, :]).astype(BF16), w1_ref[0, :half, :], preferred_element_type=F32)
        b = jnp.dot((x + pos_ref[0, 1:2, :]).astype(BF16), w1_ref[0, half:, :], preferred_element_type=F32)
        hid = a + pltpu.roll(b, b.shape[0] - 1, 0)
        hid = (hid * _sigmoid(hid)).astype(BF16)
        out = jnp.dot(hid, w2_ref[0], preferred_element_type=F32)
        ms = jnp.mean(out * out, axis=-1, keepdims=True)
        normed = out * lax.rsqrt(ms + EPS) * knw_ref[...]
        o_ref[0, 0, g] = jnp.where(is_key, normed, out)
        ot_ref[0, 0, g] = lax.dot_general(w2t_ref[0], hid, NT_DIMS, preferred_element_type=F32)


def _compress(xc, pos, w1, w2, w2t, knw, B, ncb):
    return pl.pallas_call(
        _cmp_kernel,
        grid=(B, 2),
        in_specs=[
            pl.BlockSpec((1, NSA_KV_HEADS, ncb, CMP_STRIDE * HEAD_DIM), lambda b, s: (b, s, 0, 0)),
            pl.BlockSpec((1, 2, CMP_STRIDE * HEAD_DIM), lambda b, s: (s, 0, 0)),
            pl.BlockSpec((1, CMP_BLOCK * HEAD_DIM, CMP_HIDDEN), lambda b, s: (s, 0, 0)),
            pl.BlockSpec((1, CMP_HIDDEN, HEAD_DIM), lambda b, s: (s, 0, 0)),
            pl.BlockSpec((1, HEAD_DIM, CMP_HIDDEN), lambda b, s: (s, 0, 0)),
            pl.BlockSpec((1, HEAD_DIM), lambda b, s: (0, 0)),
        ],
        out_specs=[
            pl.BlockSpec((1, 1, NSA_KV_HEADS, ncb, HEAD_DIM), lambda b, s: (b, s, 0, 0, 0)),
            pl.BlockSpec((1, 1, NSA_KV_HEADS, HEAD_DIM, ncb), lambda b, s: (b, s, 0, 0, 0)),
        ],
        out_shape=[
            jax.ShapeDtypeStruct((B, 2, NSA_KV_HEADS, ncb, HEAD_DIM), F32),
            jax.ShapeDtypeStruct((B, 2, NSA_KV_HEADS, HEAD_DIM, ncb), F32),
        ],
        compiler_params=pltpu.CompilerParams(dimension_semantics=("parallel", "parallel"),
                                             vmem_limit_bytes=VMEM_LIMIT),
        name="compress",
    )(xc, pos, w1, w2, w2t, knw)


def _nsa_kernel(qt_ref, qtn_ref, glt_ref, ng_ref, ks_ref, kw_ref, vst_ref, vwt_ref, kc_ref, vct_ref, mt_ref, qnw_ref,
                bg_ref, o_ref, qp_ref, oc_ref, sa_ref, sb_ref, mxa_ref, mxb_ref, sw_ref, m_ref, acc_ref):
    qi = pl.program_id(1)
    t0 = qi * (NSA_QT * NSA_TQ)
    tiles = range(NSA_QT)
    chains = [(qt, g) for qt in tiles for g in range(NSA_KV_HEADS)]
    gq = NSA_GROUP * HEAD_DIM
    gg = 16
    ncb = kc_ref.shape[3]
    n_slc = mt_ref.shape[0]

    def select_masks(base_t0):
        n_idx = lax.broadcasted_iota(jnp.int32, (ncb, NSA_TQ), 0)
        jb = lax.broadcasted_iota(jnp.int32, (n_slc, NSA_TQ), 0)
        out = []
        for qt in tiles:
            tok = base_t0 + qt * NSA_TQ
            tok_c = tok + lax.broadcasted_iota(jnp.int32, (ncb, NSA_TQ), 1)
            cbias = _tile4(jnp.where((n_idx * CMP_STRIDE + (CMP_BLOCK - 1)) <= tok_c, 0.0, NEG))
            tok_row = tok + lax.broadcasted_iota(jnp.int32, (1, NSA_TQ), 1)
            has_block = _tile4(jnp.where(tok_row >= CMP_BLOCK - 1, 1.0, 0.0))
            tok_s = tok + lax.broadcasted_iota(jnp.int32, (n_slc, NSA_TQ), 1)
            valid_s = jb * SLC_BLOCK <= tok_s
            force = (jb == (tok_s >> SLC_SHIFT)) | (jb == 0)
            out.append((cbias, has_block, valid_s, force))
        return out

    def select_scores(src_ref, slot, masks, c, rows=ncb):
        qt, g = chains[c]
        cols = []
        for r in range(NSA_GROUP):
            q = src_ref[0, qt, g * gq + r * HEAD_DIM:g * gq + (r + 1) * HEAD_DIM, :]
            ms = jnp.mean(q * q, axis=0, keepdims=True)
            cols.append(q * lax.rsqrt(ms + EPS) * qnw_ref[...] * (QK_SCALE * LOG2E))
        qs = jnp.concatenate(cols, axis=1).astype(BF16)
        qp_ref[slot, c, 0:HEAD_DIM, :] = qs
        return (jnp.dot(kc_ref[0, 0, g, 0:rows, :].astype(BF16), qs, preferred_element_type=F32)
                + masks[qt][0][0:rows, :])

    def select_probs(sc, slot, masks, c, want_scores=True):
        qt, g = chains[c]
        _, has_block, valid_s, force = masks[qt]
        rows = sc.shape[0]
        mc = jnp.max(sc, axis=0, keepdims=True)
        ec = jnp.exp2(sc - mc)
        lc = jnp.sum(ec, axis=0, keepdims=True)
        p = ec * (has_block / lc)
        oc_ref[slot, c] = jnp.dot(vct_ref[0, 0, g, :, 0:rows].astype(BF16), p.astype(BF16),
                                  preferred_element_type=F32)
        if not want_scores:
            return None
        ps = p[:, 0:NSA_TQ]
        for r in range(1, NSA_GROUP):
            ps = ps + p[:, r * NSA_TQ:(r + 1) * NSA_TQ]
        ps_hi = ps.astype(BF16)
        ps_lo = (ps - ps_hi.astype(F32)).astype(BF16)
        imp = (jnp.dot(mt_ref[:, 0:rows], ps_hi, preferred_element_type=F32)
               + jnp.dot(mt_ref[:, 0:rows], ps_lo, preferred_element_type=F32))
        return jnp.where(valid_s, jnp.where(force, imp + FORCE_BONUS, imp), NEG)

    def select_rank(scores, valid_s, slot, n_live):
        if n_live <= SLC_TOPK:
            for c, (qt, g) in enumerate(chains):
                qp_ref[slot, c, HEAD_DIM:2 * HEAD_DIM, :] = _tile4(jnp.where(valid_s[qt], 0.0, NEG).astype(BF16))
            return
        sub = lax.broadcasted_iota(jnp.int32, (SUBLANES, NSA_TQ), 0)
        for c, (qt, g) in enumerate(chains):
            score = scores[c]
            blocks = [score[v * SUBLANES:(v + 1) * SUBLANES, :] for v in range(n_live // SUBLANES)]
            ranks = [jnp.zeros((SUBLANES, NSA_TQ), F32) for _ in blocks]
            for i in range(n_live):
                row = score[i:i + 1, :]
                for v, blk in enumerate(blocks):
                    if v * SUBLANES > i:
                        beats = row >= blk
                    elif (v + 1) * SUBLANES <= i:
                        beats = row > blk
                    else:
                        beats = (row > blk) | ((row >= blk) & (sub > i - v * SUBLANES))
                    ranks[v] = ranks[v] + jnp.where(beats, 1.0, 0.0)
            dead = [jnp.full((SUBLANES, NSA_TQ), float(n_slc), F32)] * ((n_slc - n_live) // SUBLANES)
            rank = jnp.concatenate(ranks + dead, axis=0)
            sel = (rank < float(SLC_TOPK)) & valid_s[qt]
            qp_ref[slot, c, HEAD_DIM:2 * HEAD_DIM, :] = _tile4(jnp.where(sel, 0.0, NEG).astype(BF16))

    step_tokens = NSA_QT * NSA_TQ

    def live_blocks(step):
        return ((step + 1) * step_tokens - 1) // SLC_BLOCK + 1

    all_chains = list(range(len(chains)))

    @pl.when(qi == 0)
    def _():
        masks0 = select_masks(0)
        rows0 = min(ncb, -(-((step_tokens - CMP_BLOCK) // CMP_STRIDE + 1) // 16) * 16)
        assert live_blocks(0) <= SLC_TOPK
        for c in all_chains:
            select_probs(select_scores(qt_ref, 0, masks0, c, rows0), 0, masks0, c, want_scores=False)
        select_rank(None, [m[2] for m in masks0], 0, live_blocks(0))

    cur = qi % 2
    nxt = 1 - cur
    t_next = t0 + NSA_QT * NSA_TQ

    kt0, ks0, wbias = [], [], []
    c_minus_r = (lax.broadcasted_iota(jnp.int32, (WIN_KEYS, NSA_TQ), 1)
                 - lax.broadcasted_iota(jnp.int32, (WIN_KEYS, NSA_TQ), 0))
    for qt in tiles:
        kt0.append(jnp.maximum(qi * NSA_QT + qt - WIN_SIZE // NSA_TQ, 0))
        ks0.append(pl.multiple_of(kt0[qt] * NSA_TQ, NSA_TQ))
        delta = (t0 + qt * NSA_TQ - ks0[qt]) + c_minus_r
        in_window = lax.bitcast_convert_type(delta, jnp.uint32) < WIN_SIZE
        wbias.append(_tile4(jnp.where(in_window, 0.0, NEG)))

    def win_scores(c):
        qt, g = chains[c]
        sw = jnp.dot(kw_ref[0, g, pl.ds(ks0[qt], WIN_KEYS), 0:HEAD_DIM], qp_ref[cur, c, 0:HEAD_DIM, :],
                     preferred_element_type=F32) + wbias[qt]
        sw_ref[c] = sw
        return jnp.max(sw, axis=0, keepdims=True)

    def win_attend(c, mw):
        qt, g = chains[c]
        ewb = jnp.exp2(sw_ref[c] - mw).astype(BF16)
        vwt = jnp.concatenate([vwt_ref[0, kt0[qt] + j, g * V_ROWS:(g + 1) * V_ROWS, :]
                               for j in range(WIN_KEYS // LANES)], axis=1)
        ow_aug = jnp.dot(vwt, ewb, preferred_element_type=F32)
        return ow_aug[0:HEAD_DIM, :] * (1.0 / ow_aug[HEAD_DIM:HEAD_DIM + 1, :])

    vt_per_tile = SLC_TK // LANES

    def slc_scores(j, dst_ref, mx_ref, which=all_chains):
        kst = pl.multiple_of(j * SLC_TK, SLC_TK)
        for c in which:
            qt, g = chains[c]
            s = jnp.dot(ks_ref[0, g, pl.ds(kst, SLC_TK), :], qp_ref[cur, c], preferred_element_type=F32)
            dst_ref[c] = s
            mx_ref[c] = jnp.max(s, axis=0, keepdims=True)

    def slc_update(j, src_ref, mx_ref, causal, which=all_chains, visible=SLC_TK):
        if causal:
            band = _tile4(jnp.where(lax.broadcasted_iota(jnp.int32, (NSA_TQ, NSA_TQ), 0)
                                    <= lax.broadcasted_iota(jnp.int32, (NSA_TQ, NSA_TQ), 1), 0.0, NEG))
        for c in which:
            qt, g = chains[c]
            if causal:
                r0 = visible - step_tokens + qt * NSA_TQ
                rows = r0 + NSA_TQ
                s_band = src_ref[c, r0:rows, :] + band
                s = jnp.concatenate([src_ref[c, 0:r0, :], s_band], axis=0) if r0 > 0 else s_band
                tile_max = jnp.max(s, axis=0, keepdims=True)
            else:
                rows = SLC_TK
                s = src_ref[c]
                tile_max = mx_ref[c]
            m_old = m_ref[c]
            m_new = jnp.maximum(m_old, tile_max)
            alpha = jnp.exp2(m_old - m_new)
            eb = jnp.exp2(s - m_new).astype(BF16)
            vt = jnp.concatenate([vst_ref[0, j * vt_per_tile + jj, g * V_ROWS:(g + 1) * V_ROWS, :]
                                  for jj in range(rows // LANES)], axis=1)
            acc_ref[c] = alpha * acc_ref[c] + jnp.dot(vt, eb, preferred_element_type=F32)
            m_ref[c] = m_new

    masks = select_masks(t_next)

    def matmul_stage(c):
        mw = win_scores(c)
        sc = select_scores(qtn_ref, nxt, masks, c)
        slc_scores(0, sa_ref, mxa_ref, [c])
        return sc, mw

    sel_scores, ow_t = [], []
    staged = matmul_stage(0)
    for c in all_chains:
        staged_next = matmul_stage(c + 1) if c + 1 < len(chains) else None
        sel_scores.append(select_probs(staged[0], nxt, masks, c))
        ow_t.append(win_attend(c, staged[1]))
        staged = staged_next
    sel_valid = [m[2] for m in masks]

    need = live_blocks(qi + 1)
    bounds = list(range(SLC_TOPK, n_slc + 1, SUBLANES))
    for lo, hi in zip([0] + bounds[:-1], bounds):
        in_range = (need > lo) if hi == bounds[-1] else ((need > lo) & (need <= hi))
        pl.when(in_range)(functools.partial(select_rank, sel_scores, sel_valid, nxt, hi))

    m_ref[...] = jnp.full(m_ref.shape, NEG, F32)
    acc_ref[...] = jnp.zeros(acc_ref.shape, F32)
    n_full = t0 // SLC_TK

    def pair(jj, carry):
        j = 2 * jj
        for c in all_chains:
            slc_scores(j + 1, sb_ref, mxb_ref, [c])
            slc_update(j, sa_ref, mxa_ref, False, [c])
        for c in all_chains:
            slc_scores(j + 2, sa_ref, mxa_ref, [c])
            slc_update(j + 1, sb_ref, mxb_ref, False, [c])
        return carry

    lax.fori_loop(0, n_full // 2, pair, 0)

    visible = t0 - n_full * SLC_TK + step_tokens

    def tail(odd, rows):
        if odd:
            for c in all_chains:
                slc_scores(n_full, sb_ref, mxb_ref, [c])
                slc_update(n_full - 1, sa_ref, mxa_ref, False, [c])
            slc_update(n_full, sb_ref, mxb_ref, True, visible=rows)
        else:
            slc_update(n_full, sa_ref, mxa_ref, True, visible=rows)

    for rows in range(step_tokens, SLC_TK + 1, step_tokens):
        for odd in (False, True):
            parity = (n_full % 2 == 1) if odd else (n_full % 2 == 0)
            pl.when((visible == rows) & parity)(functools.partial(tail, odd, rows))

    for qt in tiles:
        gates = _sigmoid(glt_ref[0, qt] + bg_ref[...])
        outs = []
        for g in range(NSA_KV_HEADS):
            c = qt * NSA_KV_HEADS + g
            os_t = acc_ref[c, 0:HEAD_DIM, :] * (1.0 / acc_ref[c, HEAD_DIM:HEAD_DIM + 1, :])
            oc_t = oc_ref[cur, c]
            for r in range(NSA_GROUP):
                sl = slice(r * NSA_TQ, (r + 1) * NSA_TQ)
                row = g * gg + r
                outs.append(gates[row:row + 1, :] * oc_t[:, sl]
                            + gates[row + NSA_GROUP:row + NSA_GROUP + 1, :] * os_t[:, sl]
                            + gates[row + 2 * NSA_GROUP:row + 2 * NSA_GROUP + 1, :] * ow_t[c][:, sl])
        o_tok = jnp.concatenate(outs, axis=0).T
        rows = pl.ds(qt * NSA_TQ, NSA_TQ)
        ng = ng_ref[0, rows, :]
        o_ref[0, rows, :] = (o_tok * (ng * _sigmoid(ng))).astype(BF16)


def _nsa(nqt, glt, ng3, ks, kw, vt, kc, vct, mt, qnw, bg, B, T):
    nt = T // LANES
    ncb = kc.shape[3]
    G = NSA_KV_HEADS
    NC = NSA_QT * G
    ncols = NSA_GROUP * NSA_TQ
    steps = T // (NSA_QT * NSA_TQ)
    return pl.pallas_call(
        _nsa_kernel,
        grid=(B, steps),
        in_specs=[
            pl.BlockSpec((1, NSA_QT, D_NSA, LANES), lambda b, i: (b, 0, 0, 0)),
            pl.BlockSpec((1, NSA_QT, D_NSA, LANES), lambda b, i: (b, jnp.minimum(i + 1, steps - 1), 0, 0)),
            pl.BlockSpec((1, NSA_QT, 16 * G, LANES), lambda b, i: (b, i, 0, 0)),
            pl.BlockSpec((1, NSA_QT * NSA_TQ, D_NSA), lambda b, i: (b, i, 0)),
            pl.BlockSpec((1, G, T, PAIR), lambda b, i: (b, 0, 0, 0)),
            pl.BlockSpec((1, G, T, PAIR), lambda b, i: (b, 0, 0, 0)),
            pl.BlockSpec((1, nt, G * V_ROWS, LANES), lambda b, i: (b, 0, 0, 0)),
            pl.BlockSpec((1, nt, G * V_ROWS, LANES), lambda b, i: (b, 0, 1, 0)),
            pl.BlockSpec((1, 1, G, ncb, HEAD_DIM), lambda b, i: (b, 0, 0, 0, 0)),
            pl.BlockSpec((1, 1, G, HEAD_DIM, ncb), lambda b, i: (b, 1, 0, 0, 0)),
            pl.BlockSpec(mt.shape, lambda b, i: (0, 0)),
            pl.BlockSpec((HEAD_DIM, 1), lambda b, i: (0, 0)),
            pl.BlockSpec((16 * G, 1), lambda b, i: (0, 0)),
        ],
        out_specs=pl.BlockSpec((1, NSA_QT * NSA_TQ, D_NSA), lambda b, i: (b, i, 0)),
        out_shape=jax.ShapeDtypeStruct((B, T, D_NSA), BF16),
        scratch_shapes=[
            pltpu.VMEM((2, NC, 2 * HEAD_DIM, ncols), BF16),
            pltpu.VMEM((2, NC, HEAD_DIM, ncols), F32),
            pltpu.VMEM((NC, SLC_TK, ncols), F32),
            pltpu.VMEM((NC, SLC_TK, ncols), F32),
            pltpu.VMEM((NC, 1, ncols), F32),
            pltpu.VMEM((NC, 1, ncols), F32),
            pltpu.VMEM((NC, WIN_KEYS, ncols), F32),
            pltpu.VMEM((NC, 1, ncols), F32),
            pltpu.VMEM((NC, V_ROWS, ncols), F32),
        ],
        compiler_params=pltpu.CompilerParams(dimension_semantics=("parallel", "arbitrary"),
                                             vmem_limit_bytes=VMEM_LIMIT),
        name="nsa",
    )(nqt, nqt, glt, ng3, ks, kw, vt, vt, kc, vct, mt, qnw, bg)


def _out_kernel(x_ref, yr_ref, yn_ref, wr_ref, wn_ref, o_ref):
    o_ref[...] = (x_ref[...]
                  + jnp.dot(yr_ref[...], wr_ref[...], preferred_element_type=F32)
                  + jnp.dot(yn_ref[...], wn_ref[...], preferred_element_type=F32))


def _outproj(x2, yr, yn, wr, wn):
    N = x2.shape[0]
    row = lambda i: (i, 0)
    const = lambda i: (0, 0)
    return pl.pallas_call(
        _out_kernel,
        grid=(N // OUT_TM,),
        in_specs=[pl.BlockSpec((OUT_TM, D_MODEL), row), pl.BlockSpec((OUT_TM, D_RET), row),
                  pl.BlockSpec((OUT_TM, D_NSA), row), pl.BlockSpec(wr.shape, const), pl.BlockSpec(wn.shape, const)],
        out_specs=pl.BlockSpec((OUT_TM, D_MODEL), row),
        out_shape=jax.ShapeDtypeStruct((N, D_MODEL), F32),
        compiler_params=pltpu.CompilerParams(dimension_semantics=("parallel",), vmem_limit_bytes=VMEM_LIMIT),
        name="outproj",
    )(x2, yr, yn, wr, wn)


@functools.lru_cache(maxsize=None)
def _tables(T):
    half = HEAD_DIM // 2
    inv = ROPE_THETA ** (-np.arange(half, dtype=np.float64) / half)
    ang = np.arange(T, dtype=np.float64)[:, None] * inv[None, :]
    cos = np.concatenate([np.cos(ang)] * 4, axis=1).astype(np.float32)
    sin = np.concatenate([-np.sin(ang), -np.sin(ang), np.sin(ang), np.sin(ang)], axis=1).astype(np.float32)

    C = RET_CHUNK
    log_g = np.log1p(-np.exp2(-5.0 - np.arange(RET_HEADS, dtype=np.float64)))
    pos = np.arange(C, dtype=np.float64)
    diff = pos[:, None] - pos[None, :]
    decay = np.where(diff >= 0, np.exp(log_g[:, None, None] * np.maximum(diff, 0.0)), 0.0)
    zeta = np.exp(log_g[:, None] * (C - 1.0 - pos))
    xi = np.exp(log_g[:, None] * (pos + 1.0))
    g_chunk = np.exp(log_g * C)
    npair = RET_HEADS // 2

    def pair_lanes(a):
        return np.repeat(a.reshape(npair, 2, C).transpose(0, 2, 1), HEAD_DIM, axis=2).astype(np.float32)

    dec = decay.reshape(npair, 2, C, C).transpose(0, 2, 1, 3).reshape(npair, C, 2 * C).astype(np.float32)
    gch = np.repeat(g_chunk.reshape(npair, 1, 2), HEAD_DIM, axis=2).astype(np.float32)

    n_cmp = (T - CMP_BLOCK) // CMP_STRIDE + 1
    ncb = T // CMP_STRIDE
    p = np.arange(n_cmp)[:, None] * CMP_STRIDE + np.arange(CMP_BLOCK)[None, :]
    blk = p // SLC_BLOCK
    M = (blk[:, :, None] == np.arange(T // SLC_BLOCK)[None, None, :]).mean(axis=1)
    mt = np.zeros((T // SLC_BLOCK, ncb), np.float32)
    mt[:, :n_cmp] = M.T
    return cos, sin, dec, pair_lanes(zeta), pair_lanes(xi), gch, mt


def kernel(x, norm_w, w_in, ret_norm_w, q_norm_w, k_norm_cmp, k_norm_slc, k_norm_win, cmp_pos_k, cmp_w1_k, cmp_w2_k,
           cmp_pos_v, cmp_w1_v, cmp_w2_v, b_gate, w_out):
    B, T, D = x.shape
    depth = norm_w.shape[0]
    cos, sin, dec, zeta, xi, gch, mt = _tables(T)
    ncb = T // CMP_STRIDE
    half = CMP_STRIDE * HEAD_DIM
    gate_src = np.zeros((NSA_KV_HEADS, 16), np.int32)
    gate_ok = np.zeros((NSA_KV_HEADS, 16), bool)
    for g in range(NSA_KV_HEADS):
        for br in range(N_BRANCH):
            for r in range(NSA_GROUP):
                gate_src[g, br * NSA_GROUP + r] = br * NSA_HEADS + g * NSA_GROUP + r
                gate_ok[g, br * NSA_GROUP + r] = True
    gate_src = gate_src.reshape(-1)
    gate_ok = gate_ok.reshape(-1)

    x2 = x.reshape(B * T, D)
    for layer in range(depth):
        w = w_in[layer].astype(BF16)
        o_ng = 4 * D_RET + D_NSA
        o_kv = o_ng + D_NSA
        quarter = HEAD_DIM // 2
        pair_perm = np.concatenate([np.arange(quarter), HEAD_DIM + np.arange(quarter),
                                    quarter + np.arange(quarter), HEAD_DIM + quarter + np.arange(quarter)])
        qk_perm = np.concatenate([p * PAIR + pair_perm for p in range(RET_HEADS // 2)])
        w_ret = jnp.concatenate([w[:, :D_RET][:, qk_perm], w[:, D_RET:2 * D_RET][:, qk_perm],
                                 w[:, 2 * D_RET:4 * D_RET]], axis=1)
        wt_q = w[:, 4 * D_RET:o_ng].T
        w_ng = w[:, o_ng:o_kv]
        w_ckv = w[:, o_kv:o_kv + 2 * D_KV]
        w_skwk = jnp.concatenate([w[:, o_kv + 2 * D_KV:o_kv + 3 * D_KV], w[:, o_kv + 4 * D_KV:o_kv + 5 * D_KV]], axis=1)
        wt_v = jnp.concatenate([w[:, o_kv + 3 * D_KV:o_kv + 4 * D_KV], w[:, o_kv + 5 * D_KV:o_kv + 6 * D_KV]], axis=1).T
        w_gl = w[:, o_kv + 6 * D_KV:]
        wt_g = jnp.where(gate_ok[:, None], w_gl.T[gate_src], jnp.zeros((), BF16))
        bg = jnp.where(gate_ok, b_gate[layer][gate_src], 0.0).reshape(-1, 1)

        knw = jnp.stack([jnp.tile(k_norm_slc[layer], 2), jnp.tile(k_norm_win[layer], 2)])
        ret, ng, xc, ks, kw, nqt, vt, glt = _proj(x2, norm_w[layer].reshape(1, D), w_ret, w_ng, w_ckv, w_skwk,
                                                  wt_q, wt_v, wt_g, knw, B, T)

        nw_pair = ret_norm_w[layer].reshape(RET_HEADS // 2, 1, PAIR)
        y_ret = _retention(ret.reshape(B, T, 4 * D_RET), jnp.asarray(cos), jnp.asarray(sin), jnp.asarray(dec),
                           jnp.asarray(zeta), jnp.asarray(xi), jnp.asarray(gch), nw_pair, B, T)

        pos =jnp.stack([cmp_pos_k[layer], cmp_pos_v[layer]]).reshape(2, 2, half)
        w1 = jnp.stack([cmp_w1_k[layer], cmp_w1_v[layer]]).astype(BF16)
        w2 = jnp.stack([cmp_w2_k[layer], cmp_w2_v[layer]]).astype(BF16)
        w2t = jnp.swapaxes(w2, 1, 2)
        kc, vct = _compress(xc, pos, w1, w2, w2t, k_norm_cmp[layer].reshape(1, HEAD_DIM), B, ncb)

        y_nsa = _nsa(nqt, glt, ng.reshape(B, T, D_NSA), ks, kw, vt, kc, vct, jnp.asarray(mt).astype(BF16),
                     q_norm_w[layer].reshape(HEAD_DIM, 1), bg, B, T)

        wo = w_out[layer].astype(BF16)
        x2 = _outproj(x2, y_ret.reshape(B * T, D_RET), y_nsa.reshape(B * T, D_NSA), wo[:D_RET], wo[D_RET:])
    return x2.reshape(B, T, D)
```

```python
import functools

import numpy as np
import jax
import jax.numpy as jnp
from jax import lax
from jax.experimental import pallas as pl
from jax.experimental.pallas import tpu as pltpu

F32 = jnp.float32
BF16 = jnp.bfloat16

D_MODEL = 1024
HEAD_DIM = 64
HEAD_SHIFT = 6
RET_HEADS = 8
NSA_HEADS = 8
NSA_KV_HEADS = 2
NSA_GROUP = NSA_HEADS // NSA_KV_HEADS
D_RET = RET_HEADS * HEAD_DIM
D_NSA = NSA_HEADS * HEAD_DIM
D_KV = NSA_KV_HEADS * HEAD_DIM
N_BRANCH = 3
RET_CHUNK = 128
ROPE_THETA = 10000.0
CMP_BLOCK = 32
CMP_STRIDE = 16
CMP_HIDDEN = 256
SLC_BLOCK = 64
SLC_SHIFT = 6
SLC_TOPK = 16
WIN_SIZE = 512
EPS = 1e-6
NEG = -1e30
FORCE_BONUS = 1e4
QK_SCALE = HEAD_DIM ** -0.5
LOG2E = 1.4426950408889634
V_ROWS = HEAD_DIM + 16

LANES = 128
SUBLANES = 8
PAIR = 2 * HEAD_DIM
VMEM_LIMIT = 48 * 1024 * 1024

PROJ_TM = 512
RET_TC = 1024
NSA_TQ = 128
NSA_QT = 2
SLC_TK = 512
WIN_KEYS = WIN_SIZE + NSA_TQ
OUT_TM = 1024

NT_DIMS = (((1,), (1,)), ((), ()))
TN_DIMS = (((0,), (0,)), ((), ()))


def _sigmoid(x):
    return 1.0 / (1.0 + jnp.exp(-x))


def _tile4(a):
    return jnp.concatenate([a, a, a, a], axis=1)


def _proj_kernel(steps_per_batch, x_ref, nw_ref, w_ret_ref, w_ng_ref, w_ckv_ref, w_skwk_ref, wt_q_ref, wt_v_ref,
                 wt_g_ref, knw_ref, ret_ref, ng_ref, xc_ref, ks_ref, kw_ref, nqt_ref, vt_ref, glt_ref, ckv_scr):
    x = x_ref[...]
    ms = jnp.mean(x * x, axis=-1, keepdims=True)
    h = (x * lax.rsqrt(ms + EPS) * nw_ref[...]).astype(BF16)
    ret_ref[...] = jnp.dot(h, w_ret_ref[...], preferred_element_type=F32)
    ng_ref[...] = jnp.dot(h, w_ng_ref[...], preferred_element_type=F32)

    ckv = jnp.dot(h, w_ckv_ref[...], preferred_element_type=F32)
    for half in range(2 * D_KV // LANES):
        ckv_scr[half] = ckv[:, half * LANES:(half + 1) * LANES]
    for l in range(CMP_STRIDE):
        for half in range(2 * D_KV // LANES):
            rows = ckv_scr[half, pl.ds(l, PROJ_TM // CMP_STRIDE, stride=CMP_STRIDE), :]
            for s in range(LANES // HEAD_DIM):
                xc_ref[0, half * (LANES // HEAD_DIM) + s, :, l * HEAD_DIM:(l + 1) * HEAD_DIM] = (
                    rows[:, s * HEAD_DIM:(s + 1) * HEAD_DIM])

    skwk = jnp.dot(h, w_skwk_ref[...], preferred_element_type=F32)
    lane = lax.broadcasted_iota(jnp.int32, (1, PAIR), 1)
    head0 = lane < HEAD_DIM
    m0 = jnp.where(head0, 1.0, 0.0)
    m1 = 1.0 - m0
    t_start = (pl.program_id(0) % steps_per_batch) * PROJ_TM
    tok = t_start + lax.broadcasted_iota(jnp.int32, (PROJ_TM, PAIR), 0)
    col = lax.broadcasted_iota(jnp.int32, (PROJ_TM, PAIR), 1)
    indicator = jnp.where((tok >> SLC_SHIFT) == col - HEAD_DIM, 1.0, 0.0)

    def pair_normed(t, w):
        t2 = t * t
        ms0 = jnp.sum(t2 * m0, axis=-1, keepdims=True)
        ms1 = jnp.sum(t2 * m1, axis=-1, keepdims=True)
        return t * lax.rsqrt(jnp.where(head0, ms0, ms1) * (1.0 / HEAD_DIM) + EPS) * w

    ns = pair_normed(skwk[:, :PAIR], knw_ref[0:1, :])
    nwin = pair_normed(skwk[:, PAIR:], knw_ref[1:2, :])
    for g in range(NSA_KV_HEADS):
        s_g = ns if g == 0 else pltpu.roll(ns, HEAD_DIM, 1)
        w_g = nwin if g == 0 else pltpu.roll(nwin, HEAD_DIM, 1)
        ks_ref[0, g] = jnp.where(head0, s_g, indicator).astype(BF16)
        kw_ref[0, g] = jnp.where(head0, w_g, 0.0).astype(BF16)

    qt =lax.dot_general(wt_q_ref[...], h, NT_DIMS, preferred_element_type=F32)
    vt = lax.dot_general(wt_v_ref[...], h, NT_DIMS, preferred_element_type=F32)
    gt = lax.dot_general(wt_g_ref[...], h, NT_DIMS, preferred_element_type=F32)
    pad_row = lax.broadcasted_iota(jnp.int32, (V_ROWS - HEAD_DIM, LANES), 0)
    ones_pad = jnp.where(pad_row == 0, 1.0, 0.0).astype(BF16)
    for j in range(PROJ_TM // LANES):
        sl = slice(j * LANES, (j + 1) * LANES)
        nqt_ref[0, j] = qt[:, sl]
        for blk in range(2 * NSA_KV_HEADS):
            vt_ref[0, j, blk * V_ROWS:blk * V_ROWS + HEAD_DIM, :] = (
                vt[blk * HEAD_DIM:(blk + 1) * HEAD_DIM, sl].astype(BF16))
            vt_ref[0, j, blk * V_ROWS + HEAD_DIM:(blk + 1) * V_ROWS, :] = ones_pad
        glt_ref[0, j] = gt[:, sl]


def _proj(x2, nw, w_ret, w_ng, w_ckv, w_skwk, wt_q, wt_v, wt_g, knw, B, T):
    N = B * T
    tpb = T // PROJ_TM
    sub = PROJ_TM // LANES
    nt = T // LANES
    const = lambda i: (0, 0)
    row = lambda i: (i, 0)
    trn = lambda i: (i // tpb, i % tpb, 0, 0)
    tokm = lambda i: (i // tpb, 0, i % tpb, 0)
    return pl.pallas_call(
        functools.partial(_proj_kernel, tpb),
        grid=(N // PROJ_TM,),
        in_specs=[
            pl.BlockSpec((PROJ_TM, D_MODEL), row),
            pl.BlockSpec((1, D_MODEL), const),
            pl.BlockSpec(w_ret.shape, const),
            pl.BlockSpec(w_ng.shape, const),
            pl.BlockSpec(w_ckv.shape, const),
            pl.BlockSpec(w_skwk.shape, const),
            pl.BlockSpec(wt_q.shape, const),
            pl.BlockSpec(wt_v.shape, const),
            pl.BlockSpec(wt_g.shape, const),
            pl.BlockSpec(knw.shape, const),
        ],
        out_specs=[
            pl.BlockSpec((PROJ_TM, 4 * D_RET), row),
            pl.BlockSpec((PROJ_TM, D_NSA), row),
            pl.BlockSpec((1, 2 * NSA_KV_HEADS, PROJ_TM // CMP_STRIDE, CMP_STRIDE * HEAD_DIM), tokm),
            pl.BlockSpec((1, NSA_KV_HEADS, PROJ_TM, PAIR), tokm),
            pl.BlockSpec((1, NSA_KV_HEADS, PROJ_TM, PAIR), tokm),
            pl.BlockSpec((1, sub, D_NSA, LANES), trn),
            pl.BlockSpec((1, sub, 2 * NSA_KV_HEADS * V_ROWS, LANES), trn),
            pl.BlockSpec((1, sub, 32, LANES), trn),
        ],
        out_shape=[
            jax.ShapeDtypeStruct((N, 4 * D_RET), F32),
            jax.ShapeDtypeStruct((N, D_NSA), F32),
            jax.ShapeDtypeStruct((B, 2 * NSA_KV_HEADS, T // CMP_STRIDE, CMP_STRIDE * HEAD_DIM), F32),
            jax.ShapeDtypeStruct((B, NSA_KV_HEADS, T, PAIR), BF16),
            jax.ShapeDtypeStruct((B, NSA_KV_HEADS, T, PAIR), BF16),
            jax.ShapeDtypeStruct((B, nt, D_NSA, LANES), F32),
            jax.ShapeDtypeStruct((B, nt, 2 * NSA_KV_HEADS * V_ROWS, LANES), BF16),
            jax.ShapeDtypeStruct((B, nt, 32, LANES), F32),
        ],
        scratch_shapes=[pltpu.VMEM((2 * D_KV // LANES, PROJ_TM, LANES), F32)],
        compiler_params=pltpu.CompilerParams(dimension_semantics=("parallel",), vmem_limit_bytes=VMEM_LIMIT),
        name="proj",
    )(x2, nw, w_ret, w_ng, w_ckv, w_skwk, wt_q, wt_v, wt_g, knw)


def _ret_kernel(q_ref, k_ref, v_ref, g_ref, cos_ref, sin_ref, dec_ref, zeta_ref, xi_ref, gch_ref, nw_ref,
                o_ref, state_ref):
    @pl.when(pl.program_id(1) == 0)
    def _():
        state_ref[...] = jnp.zeros_like(state_ref)

    lane = lax.broadcasted_iota(jnp.int32, (1, PAIR), 1)
    q_head = (lane >> (HEAD_SHIFT - 1)) & 1
    v_head = lane >> HEAD_SHIFT
    q_mask = [jnp.where(q_head == h, 1.0, 0.0).astype(BF16) for h in (0, 1)]
    v_mask = [jnp.where(v_head == h, 1.0, 0.0).astype(BF16) for h in (0, 1)]
    row_qh = (lax.broadcasted_iota(jnp.int32, (PAIR, PAIR), 0) >> (HEAD_SHIFT - 1)) & 1
    row_vh = lax.broadcasted_iota(jnp.int32, (PAIR, PAIR), 0) >> HEAD_SHIFT
    col_vh = lax.broadcasted_iota(jnp.int32, (PAIR, PAIR), 1) >> HEAD_SHIFT
    same_head_kv = jnp.where(row_qh == col_vh, 1.0, 0.0)
    head_mean = jnp.where(row_vh == col_vh, 1.0 / HEAD_DIM, 0.0).astype(BF16)

    pairs = range(RET_HEADS // 2)
    mean2 = jnp.concatenate([head_mean, head_mean], axis=0)
    for c in range(RET_TC // RET_CHUNK):
        sl = pl.ds(c * RET_CHUNK, RET_CHUNK)
        cos = cos_ref[sl, :]
        sin = sin_ref[sl, :]
        cols = [slice(p * PAIR, (p + 1) * PAIR) for p in pairs]
        qb, kb, vb, vzb = [], [], [], []
        for p in pairs:
            q = q_ref[0, sl, cols[p]]
            k = k_ref[0, sl, cols[p]]
            v = v_ref[0, sl, cols[p]]
            qb.append((q * cos + pltpu.roll(q, HEAD_DIM, 1) * sin).astype(BF16))
            kb.append(((k * cos + pltpu.roll(k, HEAD_DIM, 1) * sin) * QK_SCALE).astype(BF16))
            vb.append(v.astype(BF16))
            vzb.append((v * zeta_ref[p]).astype(BF16))
        states = [state_ref[p] for p in pairs]
        kk = [jnp.concatenate([kb[p] * q_mask[h] for h in (0, 1)], axis=0) for p in pairs]
        s = [lax.dot_general(qb[p], kk[p], NT_DIMS, preferred_element_type=F32) for p in pairs]
        o_cross = [jnp.dot(qb[p], states[p].astype(BF16), preferred_element_type=F32) for p in pairs]
        kv = [lax.dot_general(kb[p], vzb[p], TN_DIMS, preferred_element_type=F32) for p in pairs]
        sb = [(s[p] * dec_ref[p]).astype(BF16) for p in pairs]
        vv = [jnp.concatenate([vb[p] * v_mask[h] for h in (0, 1)], axis=0) for p in pairs]
        o = [jnp.dot(sb[p], vv[p], preferred_element_type=F32) + o_cross[p] * xi_ref[p] for p in pairs]
        for p in pairs:
            state_ref[p] = states[p] * gch_ref[p] + kv[p] * same_head_kv
        o2 = [o[p] * o[p] for p in pairs]
        o2_hi = [o2[p].astype(BF16) for p in pairs]
        o2_hl = [jnp.concatenate([o2_hi[p], (o2[p] - o2_hi[p].astype(F32)).astype(BF16)], axis=1) for p in pairs]
        ms = [jnp.dot(o2_hl[p], mean2, preferred_element_type=F32) for p in pairs]
        for p in pairs:
            g = g_ref[0, sl, cols[p]]
            y = o[p] * lax.rsqrt(ms[p] + EPS) * nw_ref[p]
            o_ref[0, sl, cols[p]] = (y * (g * _sigmoid(g))).astype(BF16)


def _retention(ret3, cos, sin, dec, zeta, xi, gch, nw, B, T):
    npair = RET_HEADS // 2
    blk = lambda col: pl.BlockSpec((1, RET_TC, D_RET), lambda b, i: (b, i, col))
    tab = pl.BlockSpec((RET_TC, PAIR), lambda b, i: (i, 0))
    whole = lambda a: pl.BlockSpec(a.shape, lambda b, i: (0,) * a.ndim)
    return pl.pallas_call(
        _ret_kernel,
        grid=(B, T // RET_TC),
        in_specs=[blk(0), blk(1), blk(2), blk(3), tab, tab, whole(dec), whole(zeta), whole(xi), whole(gch), whole(nw)],
        out_specs=pl.BlockSpec((1, RET_TC, D_RET), lambda b, i: (b, i, 0)),
        out_shape=jax.ShapeDtypeStruct((B, T, D_RET), BF16),
        scratch_shapes=[pltpu.VMEM((npair, PAIR, PAIR), F32)],
        compiler_params=pltpu.CompilerParams(dimension_semantics=("parallel", "arbitrary"),
                                             vmem_limit_bytes=VMEM_LIMIT),
        name="retention",
    )(ret3, ret3, ret3, ret3, cos, sin, dec, zeta, xi, gch, nw)


def _cmp_kernel(x_ref, pos_ref, w1_ref, w2_ref, w2t_ref, knw_ref, o_ref, ot_ref):
    is_key = pl.program_id(1) == 0
    half = CMP_STRIDE * HEAD_DIM
    for g in range(NSA_KV_HEADS):
        x = x_ref[0, g]
        a = jnp.dot((x + pos_ref[0, 0:1, :]).astype(BF16), w1_ref[0, :half, :], preferred_element_type=F32)
        b = jnp.dot((x + pos_ref[0, 1:2, :]).astype(BF16), w1_ref[0, half:, :], preferred_element_type=F32)
        hid = a + pltpu.roll(b, b.shape[0] - 1, 0)
        hid = (hid * _sigmoid(hid)).astype(BF16)
        out = jnp.dot(hid, w2_ref[0], preferred_element_type=F32)
        ms = jnp.mean(out * out, axis=-1, keepdims=True)
        normed = out * lax.rsqrt(ms + EPS) * knw_ref[...]
        o_ref[0, 0, g] = jnp.where(is_key, normed, out)
        ot_ref[0, 0, g] = lax.dot_general(w2t_ref[0], hid, NT_DIMS, preferred_element_type=F32)


def _compress(xc, pos, w1, w2, w2t, knw, B, ncb):
    return pl.pallas_call(
        _cmp_kernel,
        grid=(B, 2),
        in_specs=[
            pl.BlockSpec((1, NSA_KV_HEADS, ncb, CMP_STRIDE * HEAD_DIM), lambda b, s: (b, s, 0, 0)),
            pl.BlockSpec((1, 2, CMP_STRIDE * HEAD_DIM), lambda b, s: (s, 0, 0)),
            pl.BlockSpec((1, CMP_BLOCK * HEAD_DIM, CMP_HIDDEN), lambda b, s: (s, 0, 0)),
            pl.BlockSpec((1, CMP_HIDDEN, HEAD_DIM), lambda b, s: (s, 0, 0)),
            pl.BlockSpec((1, HEAD_DIM, CMP_HIDDEN), lambda b, s: (s, 0, 0)),
            pl.BlockSpec((1, HEAD_DIM), lambda b, s: (0, 0)),
        ],
        out_specs=[
            pl.BlockSpec((1, 1, NSA_KV_HEADS, ncb, HEAD_DIM), lambda b, s: (b, s, 0, 0, 0)),
            pl.BlockSpec((1, 1, NSA_KV_HEADS, HEAD_DIM, ncb), lambda b, s: (b, s, 0, 0, 0)),
        ],
        out_shape=[
            jax.ShapeDtypeStruct((B, 2, NSA_KV_HEADS, ncb, HEAD_DIM), F32),
            jax.ShapeDtypeStruct((B, 2, NSA_KV_HEADS, HEAD_DIM, ncb), F32),
        ],
        compiler_params=pltpu.CompilerParams(dimension_semantics=("parallel", "parallel"),
                                             vmem_limit_bytes=VMEM_LIMIT),
        name="compress",
    )(xc, pos, w1, w2, w2t, knw)


def _nsa_kernel(qt_ref, qtn_ref, glt_ref, ng_ref, ks_ref, kw_ref, vst_ref, vwt_ref, kc_ref, vct_ref, mt_ref, qnw_ref,
                bg_ref, o_ref, qp_ref, oc_ref, sa_ref, sb_ref, mxa_ref, mxb_ref, sw_ref, m_ref, acc_ref):
    qi = pl.program_id(1)
    t0 = qi * (NSA_QT * NSA_TQ)
    tiles = range(NSA_QT)
    chains = [(qt, g) for qt in tiles for g in range(NSA_KV_HEADS)]
    gq = NSA_GROUP * HEAD_DIM
    gg = 16
    ncb = kc_ref.shape[3]
    n_slc = mt_ref.shape[0]

    def select_masks(base_t0):
        n_idx = lax.broadcasted_iota(jnp.int32, (ncb, NSA_TQ), 0)
        jb = lax.broadcasted_iota(jnp.int32, (n_slc, NSA_TQ), 0)
        out = []
        for qt in tiles:
            tok = base_t0 + qt * NSA_TQ
            tok_c = tok + lax.broadcasted_iota(jnp.int32, (ncb, NSA_TQ), 1)
            cbias = _tile4(jnp.where((n_idx * CMP_STRIDE + (CMP_BLOCK - 1)) <= tok_c, 0.0, NEG))
            tok_row = tok + lax.broadcasted_iota(jnp.int32, (1, NSA_TQ), 1)
            has_block = _tile4(jnp.where(tok_row >= CMP_BLOCK - 1, 1.0, 0.0))
            tok_s = tok + lax.broadcasted_iota(jnp.int32, (n_slc, NSA_TQ), 1)
            valid_s = jb * SLC_BLOCK <= tok_s
            force = (jb == (tok_s >> SLC_SHIFT)) | (jb == 0)
            out.append((cbias, has_block, valid_s, force))
        return out

    def select_scores(src_ref, slot, masks, c, rows=ncb):
        qt, g = chains[c]
        cols = []
        for r in range(NSA_GROUP):
            q = src_ref[0, qt, g * gq + r * HEAD_DIM:g * gq + (r + 1) * HEAD_DIM, :]
            ms = jnp.mean(q * q, axis=0, keepdims=True)
            cols.append(q * lax.rsqrt(ms + EPS) * qnw_ref[...] * (QK_SCALE * LOG2E))
        qs = jnp.concatenate(cols, axis=1).astype(BF16)
        qp_ref[slot, c, 0:HEAD_DIM, :] = qs
        return (jnp.dot(kc_ref[0, 0, g, 0:rows, :].astype(BF16), qs, preferred_element_type=F32)
                + masks[qt][0][0:rows, :])

    def select_probs(sc, slot, masks, c, want_scores=True):
        qt, g = chains[c]
        _, has_block, valid_s, force = masks[qt]
        rows = sc.shape[0]
        pbs, ps = [], None
        for r in range(NSA_GROUP):
            sl = slice(r * NSA_TQ, (r + 1) * NSA_TQ)
            sc_h = sc[:, sl]
            ec = jnp.exp2(sc_h - jnp.max(sc_h, axis=0, keepdims=True))
            p = ec * (has_block[:, sl] / jnp.sum(ec, axis=0, keepdims=True))
            pbs.append(p.astype(BF16))
            ps = p if ps is None else ps + p
        oc_ref[slot, c] = jnp.dot(vct_ref[0, 0, g, :, 0:rows].astype(BF16), jnp.concatenate(pbs, axis=1),
                                  preferred_element_type=F32)
        if not want_scores:
            return None
        ps_hi = ps.astype(BF16)
        ps_lo = (ps - ps_hi.astype(F32)).astype(BF16)
        imp = (jnp.dot(mt_ref[:, 0:rows], ps_hi, preferred_element_type=F32)
               + jnp.dot(mt_ref[:, 0:rows], ps_lo, preferred_element_type=F32))
        return jnp.where(valid_s, jnp.where(force, imp + FORCE_BONUS, imp), NEG)

    def select_rank(scores, valid_s, slot, n_live):
        if n_live <= SLC_TOPK:
            for c, (qt, g) in enumerate(chains):
                qp_ref[slot, c, HEAD_DIM:2 * HEAD_DIM, :] = _tile4(jnp.where(valid_s[qt], 0.0, NEG).astype(BF16))
            return
        sub = lax.broadcasted_iota(jnp.int32, (SUBLANES, NSA_TQ), 0)
        for c, (qt, g) in enumerate(chains):
            score = scores[c]
            blocks = [score[v * SUBLANES:(v + 1) * SUBLANES, :] for v in range(n_live // SUBLANES)]
            ranks = [jnp.zeros((SUBLANES, NSA_TQ), F32) for _ in blocks]
            for i in range(n_live):
                row = score[i:i + 1, :]
                for v, blk in enumerate(blocks):
                    if v * SUBLANES > i:
                        beats = row >= blk
                    elif (v + 1) * SUBLANES <= i:
                        beats = row > blk
                    else:
                        beats = (row > blk) | ((row >= blk) & (sub > i - v * SUBLANES))
                    ranks[v] = ranks[v] + jnp.where(beats, 1.0, 0.0)
            dead = [jnp.full((SUBLANES, NSA_TQ), float(n_slc), F32)] * ((n_slc - n_live) // SUBLANES)
            rank = jnp.concatenate(ranks + dead, axis=0)
            sel = (rank < float(SLC_TOPK)) & valid_s[qt]
            qp_ref[slot, c, HEAD_DIM:2 * HEAD_DIM, :] = _tile4(jnp.where(sel, 0.0, NEG).astype(BF16))

    step_tokens = NSA_QT * NSA_TQ

    def live_blocks(step):
        return ((step + 1) * step_tokens - 1) // SLC_BLOCK + 1

    all_chains = list(range(len(chains)))

    @pl.when(qi == 0)
    def _():
        masks0 = select_masks(0)
        rows0 = min(ncb, -(-((step_tokens - CMP_BLOCK) // CMP_STRIDE + 1) // 16) * 16)
        assert live_blocks(0) <= SLC_TOPK
        for c in all_chains:
            select_probs(select_scores(qt_ref, 0, masks0, c, rows0), 0, masks0, c, want_scores=False)
        select_rank(None, [m[2] for m in masks0], 0, live_blocks(0))

    cur = qi % 2
    nxt = 1 - cur
    t_next = t0 + NSA_QT * NSA_TQ

    kt0, ks0, wbias = [], [], []
    c_minus_r = (lax.broadcasted_iota(jnp.int32, (WIN_KEYS, NSA_TQ), 1)
                 - lax.broadcasted_iota(jnp.int32, (WIN_KEYS, NSA_TQ), 0))
    for qt in tiles:
        kt0.append(jnp.maximum(qi * NSA_QT + qt - WIN_SIZE // NSA_TQ, 0))
        ks0.append(pl.multiple_of(kt0[qt] * NSA_TQ, NSA_TQ))
        delta = (t0 + qt * NSA_TQ - ks0[qt]) + c_minus_r
        in_window = lax.bitcast_convert_type(delta, jnp.uint32) < WIN_SIZE
        wbias.append(_tile4(jnp.where(in_window, 0.0, NEG)))

    def win_scores(c):
        qt, g = chains[c]
        sw = jnp.dot(kw_ref[0, g, pl.ds(ks0[qt], WIN_KEYS), 0:HEAD_DIM], qp_ref[cur, c, 0:HEAD_DIM, :],
                     preferred_element_type=F32) + wbias[qt]
        sw_ref[c] = sw
        return jnp.max(sw, axis=0, keepdims=True)

    def win_attend(c, mw):
        qt, g = chains[c]
        ewb = jnp.exp2(sw_ref[c] - mw).astype(BF16)
        vwt = jnp.concatenate([vwt_ref[0, kt0[qt] + j, g * V_ROWS:(g + 1) * V_ROWS, :]
                               for j in range(WIN_KEYS // LANES)], axis=1)
        ow_aug = jnp.dot(vwt, ewb, preferred_element_type=F32)
        return ow_aug[0:HEAD_DIM, :] * (1.0 / ow_aug[HEAD_DIM:HEAD_DIM + 1, :])

    vt_per_tile = SLC_TK // LANES

    def slc_scores(j, dst_ref, mx_ref, which=all_chains):
        kst = pl.multiple_of(j * SLC_TK, SLC_TK)
        for c in which:
            qt, g = chains[c]
            s = jnp.dot(ks_ref[0, g, pl.ds(kst, SLC_TK), :], qp_ref[cur, c], preferred_element_type=F32)
            dst_ref[c] = s
            mx_ref[c] = jnp.max(s, axis=0, keepdims=True)

    def slc_update(j, src_ref, mx_ref, causal, which=all_chains, visible=SLC_TK):
        if causal:
            band = _tile4(jnp.where(lax.broadcasted_iota(jnp.int32, (NSA_TQ, NSA_TQ), 0)
                                    <= lax.broadcasted_iota(jnp.int32, (NSA_TQ, NSA_TQ), 1), 0.0, NEG))
        for c in which:
            qt, g = chains[c]
            if causal:
                r0 = visible - step_tokens + qt * NSA_TQ
                rows = r0 + NSA_TQ
                s_band = src_ref[c, r0:rows, :] + band
                s = jnp.concatenate([src_ref[c, 0:r0, :], s_band], axis=0) if r0 > 0 else s_band
                tile_max = jnp.max(s, axis=0, keepdims=True)
            else:
                rows = SLC_TK
                s = src_ref[c]
                tile_max = mx_ref[c]
            m_old = m_ref[c]
            m_new = jnp.maximum(m_old, tile_max)
            alpha = jnp.exp2(m_old - m_new)
            eb = jnp.exp2(s - m_new).astype(BF16)
            vt = jnp.concatenate([vst_ref[0, j * vt_per_tile + jj, g * V_ROWS:(g + 1) * V_ROWS, :]
                                  for jj in range(rows // LANES)], axis=1)
            acc_ref[c] = alpha * acc_ref[c] + jnp.dot(vt, eb, preferred_element_type=F32)
            m_ref[c] = m_new

    masks = select_masks(t_next)

    def matmul_stage(c):
        mw = win_scores(c)
        sc = select_scores(qtn_ref, nxt, masks, c)
        slc_scores(0, sa_ref, mxa_ref, [c])
        return sc, mw

    sel_scores, ow_t = [], []
    staged = matmul_stage(0)
    for c in all_chains:
        staged_next = matmul_stage(c + 1) if c + 1 < len(chains) else None
        sel_scores.append(select_probs(staged[0], nxt, masks, c))
        ow_t.append(win_attend(c, staged[1]))
        staged = staged_next
    sel_valid = [m[2] for m in masks]

    need = live_blocks(qi + 1)
    bounds = list(range(SLC_TOPK, n_slc + 1, SUBLANES))
    for lo, hi in zip([0] + bounds[:-1], bounds):
        in_range = (need > lo) if hi == bounds[-1] else ((need > lo) & (need <= hi))
        pl.when(in_range)(functools.partial(select_rank, sel_scores, sel_valid, nxt, hi))

    m_ref[...] = jnp.full(m_ref.shape, NEG, F32)
    acc_ref[...] = jnp.zeros(acc_ref.shape, F32)
    n_full = t0 // SLC_TK

    def pair(jj, carry):
        j = 2 * jj
        for c in all_chains:
            slc_scores(j + 1, sb_ref, mxb_ref, [c])
            slc_update(j, sa_ref, mxa_ref, False, [c])
        for c in all_chains:
            slc_scores(j + 2, sa_ref, mxa_ref, [c])
            slc_update(j + 1, sb_ref, mxb_ref, False, [c])
        return carry

    lax.fori_loop(0, n_full // 2, pair, 0)

    visible = t0 - n_full * SLC_TK + step_tokens

    def tail(odd, rows):
        if odd:
            for c in all_chains:
                slc_scores(n_full, sb_ref, mxb_ref, [c])
                slc_update(n_full - 1, sa_ref, mxa_ref, False, [c])
            slc_update(n_full, sb_ref, mxb_ref, True, visible=rows)
        else:
            slc_update(n_full, sa_ref, mxa_ref, True, visible=rows)

    for rows in range(step_tokens, SLC_TK + 1, step_tokens):
        for odd in (False, True):
            parity = (n_full % 2 == 1) if odd else (n_full % 2 == 0)
            pl.when((visible == rows) & parity)(functools.partial(tail, odd, rows))

    for qt in tiles:
        gates = _sigmoid(glt_ref[0, qt] + bg_ref[...])
        outs = []
        for g in range(NSA_KV_HEADS):
            c = qt * NSA_KV_HEADS + g
            os_t = acc_ref[c, 0:HEAD_DIM, :] * (1.0 / acc_ref[c, HEAD_DIM:HEAD_DIM + 1, :])
            oc_t = oc_ref[cur, c]
            for r in range(NSA_GROUP):
                sl = slice(r * NSA_TQ, (r + 1) * NSA_TQ)
                row = g * gg + r
                outs.append(gates[row:row + 1, :] * oc_t[:, sl]
                            + gates[row + NSA_GROUP:row + NSA_GROUP + 1, :] * os_t[:, sl]
                            + gates[row + 2 * NSA_GROUP:row + 2 * NSA_GROUP + 1, :] * ow_t[c][:, sl])
        o_tok = jnp.concatenate(outs, axis=0).T
        rows = pl.ds(qt * NSA_TQ, NSA_TQ)
        ng = ng_ref[0, rows, :]
        o_ref[0, rows, :] = (o_tok * (ng * _sigmoid(ng))).astype(BF16)


def _nsa(nqt, glt, ng3, ks, kw, vt, kc, vct, mt, qnw, bg, B, T):
    nt = T // LANES
    ncb = kc.shape[3]
    G = NSA_KV_HEADS
    NC = NSA_QT * G
    ncols = NSA_GROUP * NSA_TQ
    steps = T // (NSA_QT * NSA_TQ)
    return pl.pallas_call(
        _nsa_kernel,
        grid=(B, steps),
        in_specs=[
            pl.BlockSpec((1, NSA_QT, D_NSA, LANES), lambda b, i: (b, 0, 0, 0)),
            pl.BlockSpec((1, NSA_QT, D_NSA, LANES), lambda b, i: (b, jnp.minimum(i + 1, steps - 1), 0, 0)),
            pl.BlockSpec((1, NSA_QT, 16 * G, LANES), lambda b, i: (b, i, 0, 0)),
            pl.BlockSpec((1, NSA_QT * NSA_TQ, D_NSA), lambda b, i: (b, i, 0)),
            pl.BlockSpec((1, G, T, PAIR), lambda b, i: (b, 0, 0, 0)),
            pl.BlockSpec((1, G, T, PAIR), lambda b, i: (b, 0, 0, 0)),
            pl.BlockSpec((1, nt, G * V_ROWS, LANES), lambda b, i: (b, 0, 0, 0)),
            pl.BlockSpec((1, nt, G * V_ROWS, LANES), lambda b, i: (b, 0, 1, 0)),
            pl.BlockSpec((1, 1, G, ncb, HEAD_DIM), lambda b, i: (b, 0, 0, 0, 0)),
            pl.BlockSpec((1, 1, G, HEAD_DIM, ncb), lambda b, i: (b, 1, 0, 0, 0)),
            pl.BlockSpec(mt.shape, lambda b, i: (0, 0)),
            pl.BlockSpec((HEAD_DIM, 1), lambda b, i: (0, 0)),
            pl.BlockSpec((16 * G, 1), lambda b, i: (0, 0)),
        ],
        out_specs=pl.BlockSpec((1, NSA_QT * NSA_TQ, D_NSA), lambda b, i: (b, i, 0)),
        out_shape=jax.ShapeDtypeStruct((B, T, D_NSA), BF16),
        scratch_shapes=[
            pltpu.VMEM((2, NC, 2 * HEAD_DIM, ncols), BF16),
            pltpu.VMEM((2, NC, HEAD_DIM, ncols), F32),
            pltpu.VMEM((NC, SLC_TK, ncols), F32),
            pltpu.VMEM((NC, SLC_TK, ncols), F32),
            pltpu.VMEM((NC, 1, ncols), F32),
            pltpu.VMEM((NC, 1, ncols), F32),
            pltpu.VMEM((NC, WIN_KEYS, ncols), F32),
            pltpu.VMEM((NC, 1, ncols), F32),
            pltpu.VMEM((NC, V_ROWS, ncols), F32),
        ],
        compiler_params=pltpu.CompilerParams(dimension_semantics=("parallel", "arbitrary"),
                                             vmem_limit_bytes=VMEM_LIMIT),
        name="nsa",
    )(nqt, nqt, glt, ng3, ks, kw, vt, vt, kc, vct, mt, qnw, bg)


def _out_kernel(x_ref, yr_ref, yn_ref, wr_ref, wn_ref, o_ref):
    o_ref[...] = (x_ref[...]
                  + jnp.dot(yr_ref[...], wr_ref[...], preferred_element_type=F32)
                  + jnp.dot(yn_ref[...], wn_ref[...], preferred_element_type=F32))


def _outproj(x2, yr, yn, wr, wn):
    N = x2.shape[0]
    row = lambda i: (i, 0)
    const = lambda i: (0, 0)
    return pl.pallas_call(
        _out_kernel,
        grid=(N // OUT_TM,),
        in_specs=[pl.BlockSpec((OUT_TM, D_MODEL), row), pl.BlockSpec((OUT_TM, D_RET), row),
                  pl.BlockSpec((OUT_TM, D_NSA), row), pl.BlockSpec(wr.shape, const), pl.BlockSpec(wn.shape, const)],
        out_specs=pl.BlockSpec((OUT_TM, D_MODEL), row),
        out_shape=jax.ShapeDtypeStruct((N, D_MODEL), F32),
        compiler_params=pltpu.CompilerParams(dimension_semantics=("parallel",), vmem_limit_bytes=VMEM_LIMIT),
        name="outproj",
    )(x2, yr, yn, wr, wn)


@functools.lru_cache(maxsize=None)
def _tables(T):
    half = HEAD_DIM // 2
    inv = ROPE_THETA ** (-np.arange(half, dtype=np.float64) / half)
    ang = np.arange(T, dtype=np.float64)[:, None] * inv[None, :]
    cos = np.concatenate([np.cos(ang)] * 4, axis=1).astype(np.float32)
    sin = np.concatenate([-np.sin(ang), -np.sin(ang), np.sin(ang), np.sin(ang)], axis=1).astype(np.float32)

    C = RET_CHUNK
    log_g = np.log1p(-np.exp2(-5.0 - np.arange(RET_HEADS, dtype=np.float64)))
    pos = np.arange(C, dtype=np.float64)
    diff = pos[:, None] - pos[None, :]
    decay = np.where(diff >= 0, np.exp(log_g[:, None, None] * np.maximum(diff, 0.0)), 0.0)
    zeta = np.exp(log_g[:, None] * (C - 1.0 - pos))
    xi = np.exp(log_g[:, None] * (pos + 1.0))
    g_chunk = np.exp(log_g * C)
    npair = RET_HEADS // 2

    def pair_lanes(a):
        return np.repeat(a.reshape(npair, 2, C).transpose(0, 2, 1), HEAD_DIM, axis=2).astype(np.float32)

    dec = decay.reshape(npair, 2, C, C).transpose(0, 2, 1, 3).reshape(npair, C, 2 * C).astype(np.float32)
    gch = np.repeat(g_chunk.reshape(npair, 1, 2), HEAD_DIM, axis=2).astype(np.float32)

    n_cmp = (T - CMP_BLOCK) // CMP_STRIDE + 1
    ncb = T // CMP_STRIDE
    p = np.arange(n_cmp)[:, None] * CMP_STRIDE + np.arange(CMP_BLOCK)[None, :]
    blk = p // SLC_BLOCK
    M = (blk[:, :, None] == np.arange(T // SLC_BLOCK)[None, None, :]).mean(axis=1)
    mt = np.zeros((T // SLC_BLOCK, ncb), np.float32)
    mt[:, :n_cmp] = M.T
    return cos, sin, dec, pair_lanes(zeta), pair_lanes(xi), gch, mt


def kernel(x, norm_w, w_in, ret_norm_w, q_norm_w, k_norm_cmp, k_norm_slc, k_norm_win, cmp_pos_k, cmp_w1_k, cmp_w2_k,
           cmp_pos_v, cmp_w1_v, cmp_w2_v, b_gate, w_out):
    B, T, D = x.shape
    depth = norm_w.shape[0]
    cos, sin, dec, zeta, xi, gch, mt = _tables(T)
    ncb = T // CMP_STRIDE
    half = CMP_STRIDE * HEAD_DIM
    gate_src = np.zeros((NSA_KV_HEADS, 16), np.int32)
    gate_ok = np.zeros((NSA_KV_HEADS, 16), bool)
    for g in range(NSA_KV_HEADS):
        for br in range(N_BRANCH):
            for r in range(NSA_GROUP):
                gate_src[g, br * NSA_GROUP + r] = br * NSA_HEADS + g * NSA_GROUP + r
                gate_ok[g, br * NSA_GROUP + r] = True
    gate_src = gate_src.reshape(-1)
    gate_ok = gate_ok.reshape(-1)

    x2 = x.reshape(B * T, D)
    for layer in range(depth):
        w = w_in[layer].astype(BF16)
        o_ng = 4 * D_RET + D_NSA
        o_kv = o_ng + D_NSA
        quarter = HEAD_DIM // 2
        pair_perm = np.concatenate([np.arange(quarter), HEAD_DIM + np.arange(quarter),
                                    quarter + np.arange(quarter), HEAD_DIM + quarter + np.arange(quarter)])
        qk_perm = np.concatenate([p * PAIR + pair_perm for p in range(RET_HEADS // 2)])
        w_ret = jnp.concatenate([w[:, :D_RET][:, qk_perm], w[:, D_RET:2 * D_RET][:, qk_perm],
                                 w[:, 2 * D_RET:4 * D_RET]], axis=1)
        wt_q = w[:, 4 * D_RET:o_ng].T
        w_ng = w[:, o_ng:o_kv]
        w_ckv = w[:, o_kv:o_kv + 2 * D_KV]
        w_skwk = jnp.concatenate([w[:, o_kv + 2 * D_KV:o_kv + 3 * D_KV], w[:, o_kv + 4 * D_KV:o_kv + 5 * D_KV]], axis=1)
        wt_v = jnp.concatenate([w[:, o_kv + 3 * D_KV:o_kv + 4 * D_KV], w[:, o_kv + 5 * D_KV:o_kv + 6 * D_KV]], axis=1).T
        w_gl = w[:, o_kv + 6 * D_KV:]
        wt_g = jnp.where(gate_ok[:, None], w_gl.T[gate_src], jnp.zeros((), BF16))
        bg = jnp.where(gate_ok, b_gate[layer][gate_src], 0.0).reshape(-1, 1)

        knw = jnp.stack([jnp.tile(k_norm_slc[layer], 2), jnp.tile(k_norm_win[layer], 2)])
        ret, ng, xc, ks, kw, nqt, vt, glt = _proj(x2, norm_w[layer].reshape(1, D), w_ret, w_ng, w_ckv, w_skwk,
                                                  wt_q, wt_v, wt_g, knw, B, T)

        nw_pair = ret_norm_w[layer].reshape(RET_HEADS // 2, 1, PAIR)
        y_ret = _retention(ret.reshape(B, T, 4 * D_RET), jnp.asarray(cos), jnp.asarray(sin), jnp.asarray(dec),
                           jnp.asarray(zeta), jnp.asarray(xi), jnp.asarray(gch), nw_pair, B, T)

        pos =jnp.stack([cmp_pos_k[layer], cmp_pos_v[layer]]).reshape(2, 2, half)
        w1 = jnp.stack([cmp_w1_k[layer], cmp_w1_v[layer]]).astype(BF16)
        w2 = jnp.stack([cmp_w2_k[layer], cmp_w2_v[layer]]).astype(BF16)
        w2t = jnp.swapaxes(w2, 1, 2)
        kc, vct = _compress(xc, pos, w1, w2, w2t, k_norm_cmp[layer].reshape(1, HEAD_DIM), B, ncb)

        y_nsa = _nsa(nqt, glt, ng.reshape(B, T, D_NSA), ks, kw, vt, kc, vct, jnp.asarray(mt).astype(BF16),
                     q_norm_w[layer].reshape(HEAD_DIM, 1), bg, B, T)

        wo = w_out[layer].astype(BF16)
        x2 = _outproj(x2, y_ret.reshape(B * T, D_RET), y_nsa.reshape(B * T, D_NSA), wo[:D_RET], wo[D_RET:])
    return x2.reshape(B, T, D)
```

```python
import functools

import numpy as np
import jax
import jax.numpy as jnp
from jax import lax
from jax.experimental import pallas as pl
from jax.experimental.pallas import tpu as pltpu

F32 = jnp.float32
BF16 = jnp.bfloat16

D_MODEL = 1024
HEAD_DIM = 64
HEAD_SHIFT = 6
RET_HEADS = 8
NSA_HEADS = 8
NSA_KV_HEADS = 2
NSA_GROUP = NSA_HEADS // NSA_KV_HEADS
D_RET = RET_HEADS * HEAD_DIM
D_NSA = NSA_HEADS * HEAD_DIM
D_KV = NSA_KV_HEADS * HEAD_DIM
N_BRANCH = 3
RET_CHUNK = 128
ROPE_THETA = 10000.0
CMP_BLOCK = 32
CMP_STRIDE = 16
CMP_HIDDEN = 256
SLC_BLOCK = 64
SLC_SHIFT = 6
SLC_TOPK = 16
WIN_SIZE = 512
EPS = 1e-6
NEG = -1e30
FORCE_BONUS = 1e4
QK_SCALE = HEAD_DIM ** -0.5
LOG2E = 1.4426950408889634
V_ROWS = HEAD_DIM + 16

LANES = 128
SUBLANES = 8
PAIR = 2 * HEAD_DIM
VMEM_LIMIT = 48 * 1024 * 1024

PROJ_TM = 512
RET_TC = 1024
NSA_TQ = 128
NSA_QT = 2
SLC_TK = 512
WIN_KEYS = WIN_SIZE + NSA_TQ
OUT_TM = 1024

NT_DIMS = (((1,), (1,)), ((), ()))
TN_DIMS = (((0,), (0,)), ((), ()))


def _sigmoid(x):
    return 1.0 / (1.0 + jnp.exp(-x))


def _tile4(a):
    return jnp.concatenate([a, a, a, a], axis=1)


def _proj_kernel(steps_per_batch, x_ref, nw_ref, wm_ref, wt_ref, knw_ref,
                 ret_ref, ng_ref, xc_ref, ks_ref, kw_ref, nqt_ref, vt_ref, glt_ref, ckv_scr):
    o_ng, o_ckv, o_skwk = 4 * D_RET, 4 * D_RET + D_NSA, 4 * D_RET + D_NSA + 2 * D_KV
    r_v, r_g = D_NSA, D_NSA + 2 * D_KV
    x = x_ref[...]
    ms = jnp.mean(x * x, axis=-1, keepdims=True)
    h = (x * lax.rsqrt(ms + EPS) * nw_ref[...]).astype(BF16)
    ret_ref[...] = jnp.dot(h, wm_ref[:, 0:o_ng], preferred_element_type=F32)
    ng_ref[...] = jnp.dot(h, wm_ref[:, o_ng:o_ckv], preferred_element_type=F32)

    ckv = jnp.dot(h, wm_ref[:, o_ckv:o_skwk], preferred_element_type=F32)
    for half in range(2 * D_KV // LANES):
        ckv_scr[half] = ckv[:, half * LANES:(half + 1) * LANES]
    for l in range(CMP_STRIDE):
        for half in range(2 * D_KV // LANES):
            rows = ckv_scr[half, pl.ds(l, PROJ_TM // CMP_STRIDE, stride=CMP_STRIDE), :]
            for s in range(LANES // HEAD_DIM):
                xc_ref[0, half * (LANES // HEAD_DIM) + s, :, l * HEAD_DIM:(l + 1) * HEAD_DIM] = (
                    rows[:, s * HEAD_DIM:(s + 1) * HEAD_DIM])

    skwk = jnp.dot(h, wm_ref[:, o_skwk:], preferred_element_type=F32)
    lane = lax.broadcasted_iota(jnp.int32, (1, PAIR), 1)
    head0 = lane < HEAD_DIM
    m0 = jnp.where(head0, 1.0, 0.0)
    m1 = 1.0 - m0
    t_start = (pl.program_id(0) % steps_per_batch) * PROJ_TM
    tok = t_start + lax.broadcasted_iota(jnp.int32, (PROJ_TM, PAIR), 0)
    col = lax.broadcasted_iota(jnp.int32, (PROJ_TM, PAIR), 1)
    indicator = jnp.where((tok >> SLC_SHIFT) == col - HEAD_DIM, 1.0, 0.0)

    def pair_normed(t, w):
        t2 = t * t
        ms0 = jnp.sum(t2 * m0, axis=-1, keepdims=True)
        ms1 = jnp.sum(t2 * m1, axis=-1, keepdims=True)
        return t * lax.rsqrt(jnp.where(head0, ms0, ms1) * (1.0 / HEAD_DIM) + EPS) * w

    ns = pair_normed(skwk[:, :PAIR], knw_ref[0:1, :])
    nwin = pair_normed(skwk[:, PAIR:], knw_ref[1:2, :])
    for g in range(NSA_KV_HEADS):
        s_g = ns if g == 0 else pltpu.roll(ns, HEAD_DIM, 1)
        w_g = nwin if g == 0 else pltpu.roll(nwin, HEAD_DIM, 1)
        ks_ref[0, g] = jnp.where(head0, s_g, indicator).astype(BF16)
        kw_ref[0, g] = jnp.where(head0, w_g, 0.0).astype(BF16)

    qt = lax.dot_general(wt_ref[0:r_v, :], h, NT_DIMS, preferred_element_type=F32)
    vt = lax.dot_general(wt_ref[r_v:r_g, :], h, NT_DIMS, preferred_element_type=F32)
    gt = lax.dot_general(wt_ref[r_g:, :], h, NT_DIMS, preferred_element_type=F32)
    pad_row = lax.broadcasted_iota(jnp.int32, (V_ROWS - HEAD_DIM, LANES), 0)
    ones_pad = jnp.where(pad_row == 0, 1.0, 0.0).astype(BF16)
    for j in range(PROJ_TM // LANES):
        sl = slice(j * LANES, (j + 1) * LANES)
        nqt_ref[0, j] = qt[:, sl]
        for blk in range(2 * NSA_KV_HEADS):
            vt_ref[0, j, blk * V_ROWS:blk * V_ROWS + HEAD_DIM, :] = (
                vt[blk * HEAD_DIM:(blk + 1) * HEAD_DIM, sl].astype(BF16))
            vt_ref[0, j, blk * V_ROWS + HEAD_DIM:(blk + 1) * V_ROWS, :] = ones_pad
        glt_ref[0, j] = gt[:, sl]


def _proj(x2, nw, w_main, wt_all, knw, B, T):
    N = B * T
    tpb = T // PROJ_TM
    sub = PROJ_TM // LANES
    nt = T // LANES
    const = lambda i: (0, 0)
    row = lambda i: (i, 0)
    trn = lambda i: (i // tpb, i % tpb, 0, 0)
    tokm = lambda i: (i // tpb, 0, i % tpb, 0)
    return pl.pallas_call(
        functools.partial(_proj_kernel, tpb),
        grid=(N // PROJ_TM,),
        in_specs=[
            pl.BlockSpec((PROJ_TM, D_MODEL), row),
            pl.BlockSpec((1, D_MODEL), const),
            pl.BlockSpec(w_main.shape, const),
            pl.BlockSpec(wt_all.shape, const),
            pl.BlockSpec(knw.shape, const),
        ],
        out_specs=[
            pl.BlockSpec((PROJ_TM, 4 * D_RET), row),
            pl.BlockSpec((PROJ_TM, D_NSA), row),
            pl.BlockSpec((1, 2 * NSA_KV_HEADS, PROJ_TM // CMP_STRIDE, CMP_STRIDE * HEAD_DIM), tokm),
            pl.BlockSpec((1, NSA_KV_HEADS, PROJ_TM, PAIR), tokm),
            pl.BlockSpec((1, NSA_KV_HEADS, PROJ_TM, PAIR), tokm),
            pl.BlockSpec((1, sub, D_NSA, LANES), trn),
            pl.BlockSpec((1, sub, 2 * NSA_KV_HEADS * V_ROWS, LANES), trn),
            pl.BlockSpec((1, sub, 32, LANES), trn),
        ],
        out_shape=[
            jax.ShapeDtypeStruct((N, 4 * D_RET), F32),
            jax.ShapeDtypeStruct((N, D_NSA), F32),
            jax.ShapeDtypeStruct((B, 2 * NSA_KV_HEADS, T // CMP_STRIDE, CMP_STRIDE * HEAD_DIM), F32),
            jax.ShapeDtypeStruct((B, NSA_KV_HEADS, T, PAIR), BF16),
            jax.ShapeDtypeStruct((B, NSA_KV_HEADS, T, PAIR), BF16),
            jax.ShapeDtypeStruct((B, nt, D_NSA, LANES), F32),
            jax.ShapeDtypeStruct((B, nt, 2 * NSA_KV_HEADS * V_ROWS, LANES), BF16),
            jax.ShapeDtypeStruct((B, nt, 32, LANES), F32),
        ],
        scratch_shapes=[pltpu.VMEM((2 * D_KV // LANES, PROJ_TM, LANES), F32)],
        compiler_params=pltpu.CompilerParams(dimension_semantics=("parallel",), vmem_limit_bytes=VMEM_LIMIT),
        name="proj",
    )(x2, nw, w_main, wt_all, knw)


def _ret_kernel(q_ref, k_ref, v_ref, g_ref, cos_ref, sin_ref, dec_ref, zeta_ref, xi_ref, gch_ref, nw_ref,
                o_ref, state_ref):
    @pl.when(pl.program_id(1) == 0)
    def _():
        state_ref[...] = jnp.zeros_like(state_ref)

    lane = lax.broadcasted_iota(jnp.int32, (1, PAIR), 1)
    q_head = (lane >> (HEAD_SHIFT - 1)) & 1
    v_head = lane >> HEAD_SHIFT
    q_mask = [jnp.where(q_head == h, 1.0, 0.0).astype(BF16) for h in (0, 1)]
    v_mask = [jnp.where(v_head == h, 1.0, 0.0).astype(BF16) for h in (0, 1)]
    row_qh = (lax.broadcasted_iota(jnp.int32, (PAIR, PAIR), 0) >> (HEAD_SHIFT - 1)) & 1
    row_vh = lax.broadcasted_iota(jnp.int32, (PAIR, PAIR), 0) >> HEAD_SHIFT
    col_vh = lax.broadcasted_iota(jnp.int32, (PAIR, PAIR), 1) >> HEAD_SHIFT
    same_head_kv = jnp.where(row_qh == col_vh, 1.0, 0.0)
    head_mean = jnp.where(row_vh == col_vh, 1.0 / HEAD_DIM, 0.0).astype(BF16)

    pairs = range(RET_HEADS // 2)
    mean2 = jnp.concatenate([head_mean, head_mean], axis=0)
    for c in range(RET_TC // RET_CHUNK):
        sl = pl.ds(c * RET_CHUNK, RET_CHUNK)
        cos = cos_ref[sl, :]
        sin = sin_ref[sl, :]
        cols = [slice(p * PAIR, (p + 1) * PAIR) for p in pairs]
        qb, kb, vb, vzb = [], [], [], []
        for p in pairs:
            q = q_ref[0, sl, cols[p]]
            k = k_ref[0, sl, cols[p]]
            v = v_ref[0, sl, cols[p]]
            qb.append((q * cos + pltpu.roll(q, HEAD_DIM, 1) * sin).astype(BF16))
            kb.append(((k * cos + pltpu.roll(k, HEAD_DIM, 1) * sin) * QK_SCALE).astype(BF16))
            vb.append(v.astype(BF16))
            vzb.append((v * zeta_ref[p]).astype(BF16))
        states = [state_ref[p] for p in pairs]
        kk = [jnp.concatenate([kb[p] * q_mask[h] for h in (0, 1)], axis=0) for p in pairs]
        s = [lax.dot_general(qb[p], kk[p], NT_DIMS, preferred_element_type=F32) for p in pairs]
        o_cross = [jnp.dot(qb[p], states[p].astype(BF16), preferred_element_type=F32) for p in pairs]
        kv = [lax.dot_general(kb[p], vzb[p], TN_DIMS, preferred_element_type=F32) for p in pairs]
        sb = [(s[p] * dec_ref[p]).astype(BF16) for p in pairs]
        vv = [jnp.concatenate([vb[p] * v_mask[h] for h in (0, 1)], axis=0) for p in pairs]
        o = [jnp.dot(sb[p], vv[p], preferred_element_type=F32) + o_cross[p] * xi_ref[p] for p in pairs]
        for p in pairs:
            state_ref[p] = states[p] * gch_ref[p] + kv[p] * same_head_kv
        o2 = [o[p] * o[p] for p in pairs]
        o2_hi = [o2[p].astype(BF16) for p in pairs]
        o2_hl = [jnp.concatenate([o2_hi[p], (o2[p] - o2_hi[p].astype(F32)).astype(BF16)], axis=1) for p in pairs]
        ms = [jnp.dot(o2_hl[p], mean2, preferred_element_type=F32) for p in pairs]
        for p in pairs:
            g = g_ref[0, sl, cols[p]]
            y = o[p] * lax.rsqrt(ms[p] + EPS) * nw_ref[p]
            o_ref[0, sl, cols[p]] = (y * (g * _sigmoid(g))).astype(BF16)


def _retention(ret3, cos, sin, dec, zeta, xi, gch, nw, B, T):
    npair = RET_HEADS // 2
    blk = lambda col: pl.BlockSpec((1, RET_TC, D_RET), lambda b, i: (b, i, col))
    tab = pl.BlockSpec((RET_TC, PAIR), lambda b, i: (i, 0))
    whole = lambda a: pl.BlockSpec(a.shape, lambda b, i: (0,) * a.ndim)
    return pl.pallas_call(
        _ret_kernel,
        grid=(B, T // RET_TC),
        in_specs=[blk(0), blk(1), blk(2), blk(3), tab, tab, whole(dec), whole(zeta), whole(xi), whole(gch), whole(nw)],
        out_specs=pl.BlockSpec((1, RET_TC, D_RET), lambda b, i: (b, i, 0)),
        out_shape=jax.ShapeDtypeStruct((B, T, D_RET), BF16),
        scratch_shapes=[pltpu.VMEM((npair, PAIR, PAIR), F32)],
        compiler_params=pltpu.CompilerParams(dimension_semantics=("parallel", "arbitrary"),
                                             vmem_limit_bytes=VMEM_LIMIT),
        name="retention",
    )(ret3, ret3, ret3, ret3, cos, sin, dec, zeta, xi, gch, nw)


def _cmp_kernel(x_ref, pos_ref, w1_ref, w2_ref, w2t_ref, knw_ref, o_ref, ot_ref):
    is_key = pl.program_id(1) == 0
    half = CMP_STRIDE * HEAD_DIM
    for g in range(NSA_KV_HEADS):
        x = x_ref[0, g]
        a = jnp.dot((x + pos_ref[0, 0:1, :]).astype(BF16), w1_ref[0, :half, :], preferred_element_type=F32)
        b = jnp.dot((x + pos_ref[0, 1:2, :]).astype(BF16), w1_ref[0, half:, :], preferred_element_type=F32)
        hid = a + pltpu.roll(b, b.shape[0] - 1, 0)
        hid = (hid * _sigmoid(hid)).astype(BF16)
        out = jnp.dot(hid, w2_ref[0], preferred_element_type=F32)
        ms = jnp.mean(out * out, axis=-1, keepdims=True)
        normed = out * lax.rsqrt(ms + EPS) * knw_ref[...]
        o_ref[0, 0, g] = jnp.where(is_key, normed, out)
        ot_ref[0, 0, g] = lax.dot_general(w2t_ref[0], hid, NT_DIMS, preferred_element_type=F32)


def _compress(xc, pos, w1, w2, w2t, knw, B, ncb):
    return pl.pallas_call(
        _cmp_kernel,
        grid=(B, 2),
        in_specs=[
            pl.BlockSpec((1, NSA_KV_HEADS, ncb, CMP_STRIDE * HEAD_DIM), lambda b, s: (b, s, 0, 0)),
            pl.BlockSpec((1, 2, CMP_STRIDE * HEAD_DIM), lambda b, s: (s, 0, 0)),
            pl.BlockSpec((1, CMP_BLOCK * HEAD_DIM, CMP_HIDDEN), lambda b, s: (s, 0, 0)),
            pl.BlockSpec((1, CMP_HIDDEN, HEAD_DIM), lambda b, s: (s, 0, 0)),
            pl.BlockSpec((1, HEAD_DIM, CMP_HIDDEN), lambda b, s: (s, 0, 0)),
            pl.BlockSpec((1, HEAD_DIM), lambda b, s: (0, 0)),
        ],
        out_specs=[
            pl.BlockSpec((1, 1, NSA_KV_HEADS, ncb, HEAD_DIM), lambda b, s: (b, s, 0, 0, 0)),
            pl.BlockSpec((1, 1, NSA_KV_HEADS, HEAD_DIM, ncb), lambda b, s: (b, s, 0, 0, 0)),
        ],
        out_shape=[
            jax.ShapeDtypeStruct((B, 2, NSA_KV_HEADS, ncb, HEAD_DIM), F32),
            jax.ShapeDtypeStruct((B, 2, NSA_KV_HEADS, HEAD_DIM, ncb), F32),
        ],
        compiler_params=pltpu.CompilerParams(dimension_semantics=("parallel", "parallel"),
                                             vmem_limit_bytes=VMEM_LIMIT),
        name="compress",
    )(xc, pos, w1, w2, w2t, knw)


def _nsa_kernel(qt_ref, qtn_ref, glt_ref, ng_ref, ks_ref, kw_ref, vst_ref, vwt_ref, kc_ref, vct_ref, mt_ref, qnw_ref,
                bg_ref, o_ref, qp_ref, oc_ref, sa_ref, sb_ref, mxa_ref, mxb_ref, sw_ref, m_ref, acc_ref):
    qi = pl.program_id(1)
    t0 = qi * (NSA_QT * NSA_TQ)
    tiles = range(NSA_QT)
    chains = [(qt, g) for qt in tiles for g in range(NSA_KV_HEADS)]
    gq = NSA_GROUP * HEAD_DIM
    gg = 16
    ncb = kc_ref.shape[3]
    n_slc = mt_ref.shape[0]

    def select_masks(base_t0):
        n_idx = lax.broadcasted_iota(jnp.int32, (ncb, NSA_TQ), 0)
        jb = lax.broadcasted_iota(jnp.int32, (n_slc, NSA_TQ), 0)
        out = []
        for qt in tiles:
            tok = base_t0 + qt * NSA_TQ
            tok_c = tok + lax.broadcasted_iota(jnp.int32, (ncb, NSA_TQ), 1)
            cbias = _tile4(jnp.where((n_idx * CMP_STRIDE + (CMP_BLOCK - 1)) <= tok_c, 0.0, NEG))
            tok_row = tok + lax.broadcasted_iota(jnp.int32, (1, NSA_TQ), 1)
            has_block = _tile4(jnp.where(tok_row >= CMP_BLOCK - 1, 1.0, 0.0))
            tok_s = tok + lax.broadcasted_iota(jnp.int32, (n_slc, NSA_TQ), 1)
            valid_s = jb * SLC_BLOCK <= tok_s
            force = (jb == (tok_s >> SLC_SHIFT)) | (jb == 0)
            out.append((cbias, has_block, valid_s, force))
        return out

    def select_scores(src_ref, slot, masks, c, rows=ncb):
        qt, g = chains[c]
        cols = []
        for r in range(NSA_GROUP):
            q = src_ref[0, qt, g * gq + r * HEAD_DIM:g * gq + (r + 1) * HEAD_DIM, :]
            ms = jnp.mean(q * q, axis=0, keepdims=True)
            cols.append(q * lax.rsqrt(ms + EPS) * qnw_ref[...] * (QK_SCALE * LOG2E))
        qs = jnp.concatenate(cols, axis=1).astype(BF16)
        qp_ref[slot, c, 0:HEAD_DIM, :] = qs
        return (jnp.dot(kc_ref[0, 0, g, 0:rows, :].astype(BF16), qs, preferred_element_type=F32)
                + masks[qt][0][0:rows, :])

    def select_probs(sc, slot, masks, c, want_scores=True):
        qt, g = chains[c]
        _, has_block, valid_s, force = masks[qt]
        rows = sc.shape[0]
        mc = jnp.max(sc, axis=0, keepdims=True)
        ec = jnp.exp2(sc - mc)
        lc = jnp.sum(ec, axis=0, keepdims=True)
        p = ec * (has_block / lc)
        oc_ref[slot, c] = jnp.dot(vct_ref[0, 0, g, :, 0:rows].astype(BF16), p.astype(BF16),
                                  preferred_element_type=F32)
        if not want_scores:
            return None
        ps = p[:, 0:NSA_TQ]
        for r in range(1, NSA_GROUP):
            ps = ps + p[:, r * NSA_TQ:(r + 1) * NSA_TQ]
        ps_hi = ps.astype(BF16)
        ps_lo = (ps - ps_hi.astype(F32)).astype(BF16)
        imp = (jnp.dot(mt_ref[:, 0:rows], ps_hi, preferred_element_type=F32)
               + jnp.dot(mt_ref[:, 0:rows], ps_lo, preferred_element_type=F32))
        return jnp.where(valid_s, jnp.where(force, imp + FORCE_BONUS, imp), NEG)

    def select_rank(scores, valid_s, slot, n_live):
        if n_live <= SLC_TOPK:
            for c, (qt, g) in enumerate(chains):
                qp_ref[slot, c, HEAD_DIM:2 * HEAD_DIM, :] = _tile4(jnp.where(valid_s[qt], 0.0, NEG).astype(BF16))
            return
        sub = lax.broadcasted_iota(jnp.int32, (SUBLANES, NSA_TQ), 0)
        for c, (qt, g) in enumerate(chains):
            score = scores[c]
            blocks = [score[v * SUBLANES:(v + 1) * SUBLANES, :] for v in range(n_live // SUBLANES)]
            ranks = [jnp.zeros((SUBLANES, NSA_TQ), F32) for _ in blocks]
            for i in range(n_live):
                row = score[i:i + 1, :]
                for v, blk in enumerate(blocks):
                    if v * SUBLANES > i:
                        beats = row >= blk
                    elif (v + 1) * SUBLANES <= i:
                        beats = row > blk
                    else:
                        beats = (row > blk) | ((row >= blk) & (sub > i - v * SUBLANES))
                    ranks[v] = ranks[v] + jnp.where(beats, 1.0, 0.0)
            dead = [jnp.full((SUBLANES, NSA_TQ), float(n_slc), F32)] * ((n_slc - n_live) // SUBLANES)
            rank = jnp.concatenate(ranks + dead, axis=0)
            sel = (rank < float(SLC_TOPK)) & valid_s[qt]
            qp_ref[slot, c, HEAD_DIM:2 * HEAD_DIM, :] = _tile4(jnp.where(sel, 0.0, NEG).astype(BF16))

    step_tokens = NSA_QT * NSA_TQ

    def live_blocks(step):
        return ((step + 1) * step_tokens - 1) // SLC_BLOCK + 1

    all_chains = list(range(len(chains)))

    @pl.when(qi == 0)
    def _():
        masks0 = select_masks(0)
        rows0 = min(ncb, -(-((step_tokens - CMP_BLOCK) // CMP_STRIDE + 1) // 16) * 16)
        assert live_blocks(0) <= SLC_TOPK
        for c in all_chains:
            select_probs(select_scores(qt_ref, 0, masks0, c, rows0), 0, masks0, c, want_scores=False)
        select_rank(None, [m[2] for m in masks0], 0, live_blocks(0))

    cur = qi % 2
    nxt = 1 - cur
    t_next = t0 + NSA_QT * NSA_TQ

    kt0, ks0, wbias = [], [], []
    c_minus_r = (lax.broadcasted_iota(jnp.int32, (WIN_KEYS, NSA_TQ), 1)
                 - lax.broadcasted_iota(jnp.int32, (WIN_KEYS, NSA_TQ), 0))
    for qt in tiles:
        kt0.append(jnp.maximum(qi * NSA_QT + qt - WIN_SIZE // NSA_TQ, 0))
        ks0.append(pl.multiple_of(kt0[qt] * NSA_TQ, NSA_TQ))
        delta = (t0 + qt * NSA_TQ - ks0[qt]) + c_minus_r
        in_window = lax.bitcast_convert_type(delta, jnp.uint32) < WIN_SIZE
        wbias.append(_tile4(jnp.where(in_window, 0.0, NEG)))

    def win_scores(c):
        qt, g = chains[c]
        sw = jnp.dot(kw_ref[0, g, pl.ds(ks0[qt], WIN_KEYS), 0:HEAD_DIM], qp_ref[cur, c, 0:HEAD_DIM, :],
                     preferred_element_type=F32) + wbias[qt]
        sw_ref[c] = sw
        return jnp.max(sw, axis=0, keepdims=True)

    def win_attend(c, mw):
        qt, g = chains[c]
        ewb = jnp.exp2(sw_ref[c] - mw).astype(BF16)
        vwt = jnp.concatenate([vwt_ref[0, kt0[qt] + j, g * V_ROWS:(g + 1) * V_ROWS, :]
                               for j in range(WIN_KEYS // LANES)], axis=1)
        ow_aug = jnp.dot(vwt, ewb, preferred_element_type=F32)
        return ow_aug[0:HEAD_DIM, :] * (1.0 / ow_aug[HEAD_DIM:HEAD_DIM + 1, :])

    vt_per_tile = SLC_TK // LANES

    def slc_scores(j, dst_ref, mx_ref, which=all_chains):
        kst = pl.multiple_of(j * SLC_TK, SLC_TK)
        for c in which:
            qt, g = chains[c]
            s = jnp.dot(ks_ref[0, g, pl.ds(kst, SLC_TK), :], qp_ref[cur, c], preferred_element_type=F32)
            dst_ref[c] = s
            mx_ref[c] = jnp.max(s, axis=0, keepdims=True)

    def slc_update(j, src_ref, mx_ref, causal, which=all_chains, visible=SLC_TK):
        if causal:
            band = _tile4(jnp.where(lax.broadcasted_iota(jnp.int32, (NSA_TQ, NSA_TQ), 0)
                                    <= lax.broadcasted_iota(jnp.int32, (NSA_TQ, NSA_TQ), 1), 0.0, NEG))
        for c in which:
            qt, g = chains[c]
            if causal:
                r0 = visible - step_tokens + qt * NSA_TQ
                rows = r0 + NSA_TQ
                s_band = src_ref[c, r0:rows, :] + band
                s = jnp.concatenate([src_ref[c, 0:r0, :], s_band], axis=0) if r0 > 0 else s_band
                tile_max = jnp.max(s, axis=0, keepdims=True)
            else:
                rows = SLC_TK
                s = src_ref[c]
                tile_max = mx_ref[c]
            m_old = m_ref[c]
            m_new = jnp.maximum(m_old, tile_max)
            alpha = jnp.exp2(m_old - m_new)
            eb = jnp.exp2(s - m_new).astype(BF16)
            vt = jnp.concatenate([vst_ref[0, j * vt_per_tile + jj, g * V_ROWS:(g + 1) * V_ROWS, :]
                                  for jj in range(rows // LANES)], axis=1)
            acc_ref[c] = alpha * acc_ref[c] + jnp.dot(vt, eb, preferred_element_type=F32)
            m_ref[c] = m_new

    masks = select_masks(t_next)

    def matmul_stage(c):
        mw = win_scores(c)
        sc = select_scores(qtn_ref, nxt, masks, c)
        slc_scores(0, sa_ref, mxa_ref, [c])
        return sc, mw

    sel_scores, ow_t = [], []
    staged = matmul_stage(0)
    for c in all_chains:
        staged_next = matmul_stage(c + 1) if c + 1 < len(chains) else None
        sel_scores.append(select_probs(staged[0], nxt, masks, c))
        ow_t.append(win_attend(c, staged[1]))
        staged = staged_next
    sel_valid = [m[2] for m in masks]

    need = live_blocks(qi + 1)
    bounds = list(range(SLC_TOPK, n_slc + 1, SUBLANES))
    for lo, hi in zip([0] + bounds[:-1], bounds):
        in_range = (need > lo) if hi == bounds[-1] else ((need > lo) & (need <= hi))
        pl.when(in_range)(functools.partial(select_rank, sel_scores, sel_valid, nxt, hi))

    m_ref[...] = jnp.full(m_ref.shape, NEG, F32)
    acc_ref[...] = jnp.zeros(acc_ref.shape, F32)
    n_full = t0 // SLC_TK

    def pair(jj, carry):
        j = 2 * jj
        for c in all_chains:
            slc_scores(j + 1, sb_ref, mxb_ref, [c])
            slc_update(j, sa_ref, mxa_ref, False, [c])
        for c in all_chains:
            slc_scores(j + 2, sa_ref, mxa_ref, [c])
            slc_update(j + 1, sb_ref, mxb_ref, False, [c])
        return carry

    lax.fori_loop(0, n_full // 2, pair, 0)

    visible = t0 - n_full * SLC_TK + step_tokens

    def tail(odd, rows):
        if odd:
            for c in all_chains:
                slc_scores(n_full, sb_ref, mxb_ref, [c])
                slc_update(n_full - 1, sa_ref, mxa_ref, False, [c])
            slc_update(n_full, sb_ref, mxb_ref, True, visible=rows)
        else:
            slc_update(n_full, sa_ref, mxa_ref, True, visible=rows)

    for rows in range(step_tokens, SLC_TK + 1, step_tokens):
        for odd in (False, True):
            parity = (n_full % 2 == 1) if odd else (n_full % 2 == 0)
            pl.when((visible == rows) & parity)(functools.partial(tail, odd, rows))

    for qt in tiles:
        gates = _sigmoid(glt_ref[0, qt] + bg_ref[...])
        outs = []
        for g in range(NSA_KV_HEADS):
            c = qt * NSA_KV_HEADS + g
            os_t = acc_ref[c, 0:HEAD_DIM, :] * (1.0 / acc_ref[c, HEAD_DIM:HEAD_DIM + 1, :])
            oc_t = oc_ref[cur, c]
            for r in range(NSA_GROUP):
                sl = slice(r * NSA_TQ, (r + 1) * NSA_TQ)
                row = g * gg + r
                outs.append(gates[row:row + 1, :] * oc_t[:, sl]
                            + gates[row + NSA_GROUP:row + NSA_GROUP + 1, :] * os_t[:, sl]
                            + gates[row + 2 * NSA_GROUP:row + 2 * NSA_GROUP + 1, :] * ow_t[c][:, sl])
        o_tok = jnp.concatenate(outs, axis=0).T
        rows = pl.ds(qt * NSA_TQ, NSA_TQ)
        ng = ng_ref[0, rows, :]
        o_ref[0, rows, :] = (o_tok * (ng * _sigmoid(ng))).astype(BF16)


def _nsa(nqt, glt, ng3, ks, kw, vt, kc, vct, mt, qnw, bg, B, T):
    nt = T // LANES
    ncb = kc.shape[3]
    G = NSA_KV_HEADS
    NC = NSA_QT * G
    ncols = NSA_GROUP * NSA_TQ
    steps = T // (NSA_QT * NSA_TQ)
    return pl.pallas_call(
        _nsa_kernel,
        grid=(B, steps),
        in_specs=[
            pl.BlockSpec((1, NSA_QT, D_NSA, LANES), lambda b, i: (b, 0, 0, 0)),
            pl.BlockSpec((1, NSA_QT, D_NSA, LANES), lambda b, i: (b, jnp.minimum(i + 1, steps - 1), 0, 0)),
            pl.BlockSpec((1, NSA_QT, 16 * G, LANES), lambda b, i: (b, i, 0, 0)),
            pl.BlockSpec((1, NSA_QT * NSA_TQ, D_NSA), lambda b, i: (b, i, 0)),
            pl.BlockSpec((1, G, T, PAIR), lambda b, i: (b, 0, 0, 0)),
            pl.BlockSpec((1, G, T, PAIR), lambda b, i: (b, 0, 0, 0)),
            pl.BlockSpec((1, nt, G * V_ROWS, LANES), lambda b, i: (b, 0, 0, 0)),
            pl.BlockSpec((1, nt, G * V_ROWS, LANES), lambda b, i: (b, 0, 1, 0)),
            pl.BlockSpec((1, 1, G, ncb, HEAD_DIM), lambda b, i: (b, 0, 0, 0, 0)),
            pl.BlockSpec((1, 1, G, HEAD_DIM, ncb), lambda b, i: (b, 1, 0, 0, 0)),
            pl.BlockSpec(mt.shape, lambda b, i: (0, 0)),
            pl.BlockSpec((HEAD_DIM, 1), lambda b, i: (0, 0)),
            pl.BlockSpec((16 * G, 1), lambda b, i: (0, 0)),
        ],
        out_specs=pl.BlockSpec((1, NSA_QT * NSA_TQ, D_NSA), lambda b, i: (b, i, 0)),
        out_shape=jax.ShapeDtypeStruct((B, T, D_NSA), BF16),
        scratch_shapes=[
            pltpu.VMEM((2, NC, 2 * HEAD_DIM, ncols), BF16),
            pltpu.VMEM((2, NC, HEAD_DIM, ncols), F32),
            pltpu.VMEM((NC, SLC_TK, ncols), F32),
            pltpu.VMEM((NC, SLC_TK, ncols), F32),
            pltpu.VMEM((NC, 1, ncols), F32),
            pltpu.VMEM((NC, 1, ncols), F32),
            pltpu.VMEM((NC, WIN_KEYS, ncols), F32),
            pltpu.VMEM((NC, 1, ncols), F32),
            pltpu.VMEM((NC, V_ROWS, ncols), F32),
        ],
        compiler_params=pltpu.CompilerParams(dimension_semantics=("parallel", "arbitrary"),
                                             vmem_limit_bytes=VMEM_LIMIT),
        name="nsa",
    )(nqt, nqt, glt, ng3, ks, kw, vt, vt, kc, vct, mt, qnw, bg)


def _out_kernel(x_ref, yr_ref, yn_ref, wr_ref, wn_ref, o_ref):
    o_ref[...] = (x_ref[...]
                  + jnp.dot(yr_ref[...], wr_ref[...], preferred_element_type=F32)
                  + jnp.dot(yn_ref[...], wn_ref[...], preferred_element_type=F32))


def _outproj(x2, yr, yn, wr, wn):
    N = x2.shape[0]
    row = lambda i: (i, 0)
    const = lambda i: (0, 0)
    return pl.pallas_call(
        _out_kernel,
        grid=(N // OUT_TM,),
        in_specs=[pl.BlockSpec((OUT_TM, D_MODEL), row), pl.BlockSpec((OUT_TM, D_RET), row),
                  pl.BlockSpec((OUT_TM, D_NSA), row), pl.BlockSpec(wr.shape, const), pl.BlockSpec(wn.shape, const)],
        out_specs=pl.BlockSpec((OUT_TM, D_MODEL), row),
        out_shape=jax.ShapeDtypeStruct((N, D_MODEL), F32),
        compiler_params=pltpu.CompilerParams(dimension_semantics=("parallel",), vmem_limit_bytes=VMEM_LIMIT),
        name="outproj",
    )(x2, yr, yn, wr, wn)


@functools.lru_cache(maxsize=None)
def _tables(T):
    half = HEAD_DIM // 2
    inv = ROPE_THETA ** (-np.arange(half, dtype=np.float64) / half)
    ang = np.arange(T, dtype=np.float64)[:, None] * inv[None, :]
    cos = np.concatenate([np.cos(ang)] * 4, axis=1).astype(np.float32)
    sin = np.concatenate([-np.sin(ang), -np.sin(ang), np.sin(ang), np.sin(ang)], axis=1).astype(np.float32)

    C = RET_CHUNK
    log_g = np.log1p(-np.exp2(-5.0 - np.arange(RET_HEADS, dtype=np.float64)))
    pos = np.arange(C, dtype=np.float64)
    diff = pos[:, None] - pos[None, :]
    decay = np.where(diff >= 0, np.exp(log_g[:, None, None] * np.maximum(diff, 0.0)), 0.0)
    zeta = np.exp(log_g[:, None] * (C - 1.0 - pos))
    xi = np.exp(log_g[:, None] * (pos + 1.0))
    g_chunk = np.exp(log_g * C)
    npair = RET_HEADS // 2

    def pair_lanes(a):
        return np.repeat(a.reshape(npair, 2, C).transpose(0, 2, 1), HEAD_DIM, axis=2).astype(np.float32)

    dec = decay.reshape(npair, 2, C, C).transpose(0, 2, 1, 3).reshape(npair, C, 2 * C).astype(np.float32)
    gch = np.repeat(g_chunk.reshape(npair, 1, 2), HEAD_DIM, axis=2).astype(np.float32)

    n_cmp = (T - CMP_BLOCK) // CMP_STRIDE + 1
    ncb = T // CMP_STRIDE
    p = np.arange(n_cmp)[:, None] * CMP_STRIDE + np.arange(CMP_BLOCK)[None, :]
    blk = p // SLC_BLOCK
    M = (blk[:, :, None] == np.arange(T // SLC_BLOCK)[None, None, :]).mean(axis=1)
    mt = np.zeros((T // SLC_BLOCK, ncb), np.float32)
    mt[:, :n_cmp] = M.T
    return cos, sin, dec, pair_lanes(zeta), pair_lanes(xi), gch, mt


def kernel(x, norm_w, w_in, ret_norm_w, q_norm_w, k_norm_cmp, k_norm_slc, k_norm_win, cmp_pos_k, cmp_w1_k, cmp_w2_k,
           cmp_pos_v, cmp_w1_v, cmp_w2_v, b_gate, w_out):
    B, T, D = x.shape
    depth = norm_w.shape[0]
    cos, sin, dec, zeta, xi, gch, mt = _tables(T)
    ncb = T // CMP_STRIDE
    half = CMP_STRIDE * HEAD_DIM
    gate_src = np.zeros((NSA_KV_HEADS, 16), np.int32)
    gate_ok = np.zeros((NSA_KV_HEADS, 16), bool)
    for g in range(NSA_KV_HEADS):
        for br in range(N_BRANCH):
            for r in range(NSA_GROUP):
                gate_src[g, br * NSA_GROUP + r] = br * NSA_HEADS + g * NSA_GROUP + r
                gate_ok[g, br * NSA_GROUP + r] = True
    gate_src = gate_src.reshape(-1)
    gate_ok = gate_ok.reshape(-1)

    x2 = x.reshape(B * T, D)
    for layer in range(depth):
        o_ng = 4 * D_RET + D_NSA
        o_kv = o_ng + D_NSA
        o_gl = o_kv + 6 * D_KV
        quarter = HEAD_DIM // 2
        pair_perm = np.concatenate([np.arange(quarter), HEAD_DIM + np.arange(quarter),
                                    quarter + np.arange(quarter), HEAD_DIM + quarter + np.arange(quarter)])
        qk_perm = np.concatenate([p * PAIR + pair_perm for p in range(RET_HEADS // 2)])
        main_cols = np.concatenate([qk_perm, D_RET + qk_perm, np.arange(2 * D_RET, 4 * D_RET),
                                    np.arange(o_ng, o_kv + 2 * D_KV),
                                    np.arange(o_kv + 2 * D_KV, o_kv + 3 * D_KV),
                                    np.arange(o_kv + 4 * D_KV, o_kv + 5 * D_KV)])
        t_cols = np.concatenate([np.arange(4 * D_RET, o_ng), np.arange(o_kv + 3 * D_KV, o_kv + 4 * D_KV),
                                 np.arange(o_kv + 5 * D_KV, o_kv + 6 * D_KV), o_gl + gate_src])
        t_keep = np.concatenate([np.ones(D_NSA + 2 * D_KV, bool), gate_ok])
        w_main = w_in[layer][:, main_cols].astype(BF16)
        wt_all = jnp.where(t_keep[:, None], w_in[layer][:, t_cols].T, 0.0).astype(BF16)
        bg = jnp.where(gate_ok, b_gate[layer][gate_src], 0.0).reshape(-1, 1)

        knw = jnp.stack([jnp.tile(k_norm_slc[layer], 2), jnp.tile(k_norm_win[layer], 2)])
        ret, ng, xc, ks, kw, nqt, vt, glt = _proj(x2, norm_w[layer].reshape(1, D), w_main, wt_all, knw, B, T)

        nw_pair = ret_norm_w[layer].reshape(RET_HEADS // 2, 1, PAIR)
        y_ret = _retention(ret.reshape(B, T, 4 * D_RET), jnp.asarray(cos), jnp.asarray(sin), jnp.asarray(dec),
                           jnp.asarray(zeta), jnp.asarray(xi), jnp.asarray(gch), nw_pair, B, T)

        pos =jnp.stack([cmp_pos_k[layer], cmp_pos_v[layer]]).reshape(2, 2, half)
        w1 = jnp.stack([cmp_w1_k[layer], cmp_w1_v[layer]]).astype(BF16)
        w2 = jnp.stack([cmp_w2_k[layer], cmp_w2_v[layer]]).astype(BF16)
        w2t = jnp.swapaxes(w2, 1, 2)
        kc, vct = _compress(xc, pos, w1, w2, w2t, k_norm_cmp[layer].reshape(1, HEAD_DIM), B, ncb)

        y_nsa = _nsa(nqt, glt, ng.reshape(B, T, D_NSA), ks, kw, vt, kc, vct, jnp.asarray(mt).astype(BF16),
                     q_norm_w[layer].reshape(HEAD_DIM, 1), bg, B, T)

        wo = w_out[layer].astype(BF16)
        x2 = _outproj(x2, y_ret.reshape(B * T, D_RET), y_nsa.reshape(B * T, D_NSA), wo[:D_RET], wo[D_RET:])
    return x2.reshape(B, T, D)
```

```python
import functools

import numpy as np
import jax
import jax.numpy as jnp
from jax import lax
from jax.experimental import pallas as pl
from jax.experimental.pallas import tpu as pltpu

F32 = jnp.float32
BF16 = jnp.bfloat16

D_MODEL = 1024
HEAD_DIM = 64
HEAD_SHIFT = 6
RET_HEADS = 8
NSA_HEADS = 8
NSA_KV_HEADS = 2
NSA_GROUP = NSA_HEADS // NSA_KV_HEADS
D_RET = RET_HEADS * HEAD_DIM
D_NSA = NSA_HEADS * HEAD_DIM
D_KV = NSA_KV_HEADS * HEAD_DIM
N_BRANCH = 3
RET_CHUNK = 128
ROPE_THETA = 10000.0
CMP_BLOCK = 32
CMP_STRIDE = 16
CMP_HIDDEN = 256
SLC_BLOCK = 64
SLC_SHIFT = 6
SLC_TOPK = 16
WIN_SIZE = 512
EPS = 1e-6
NEG = -1e30
FORCE_BONUS = 1e4
QK_SCALE = HEAD_DIM ** -0.5
LOG2E = 1.4426950408889634
V_ROWS = HEAD_DIM + 16

LANES = 128
SUBLANES = 8
PAIR = 2 * HEAD_DIM
VMEM_LIMIT = 48 * 1024 * 1024

PROJ_TM = 512
RET_TC = 1024
NSA_TQ = 128
NSA_QT = 2
SLC_TK = 512
WIN_KEYS = WIN_SIZE + NSA_TQ
OUT_TM = 1024

NT_DIMS = (((1,), (1,)), ((), ()))
TN_DIMS = (((0,), (0,)), ((), ()))


def _sigmoid(x):
    return 1.0 / (1.0 + jnp.exp(-x))


def _tile4(a):
    return jnp.concatenate([a, a, a, a], axis=1)


def _proj_kernel(steps_per_batch, x_ref, nw_ref, w_ret_ref, w_ng_ref, w_ckv_ref, w_skwk_ref, wt_q_ref, wt_v_ref,
                 wt_g_ref, knw_ref, ret_ref, ng_ref, xc_ref, ks_ref, kw_ref, nqt_ref, vt_ref, glt_ref, ckv_scr):
    x = x_ref[...]
    ms = jnp.mean(x * x, axis=-1, keepdims=True)
    h = (x * lax.rsqrt(ms + EPS) * nw_ref[...]).astype(BF16)
    ret_ref[...] = jnp.dot(h, w_ret_ref[...], preferred_element_type=F32)
    ng_ref[...] = jnp.dot(h, w_ng_ref[...], preferred_element_type=F32)

    ckv = jnp.dot(h, w_ckv_ref[...], preferred_element_type=F32)
    for half in range(2 * D_KV // LANES):
        ckv_scr[half] = ckv[:, half * LANES:(half + 1) * LANES]
    for l in range(CMP_STRIDE):
        for half in range(2 * D_KV // LANES):
            rows = ckv_scr[half, pl.ds(l, PROJ_TM // CMP_STRIDE, stride=CMP_STRIDE), :]
            for s in range(LANES // HEAD_DIM):
                xc_ref[0, half * (LANES // HEAD_DIM) + s, :, l * HEAD_DIM:(l + 1) * HEAD_DIM] = (
                    rows[:, s * HEAD_DIM:(s + 1) * HEAD_DIM])

    skwk = jnp.dot(h, w_skwk_ref[...], preferred_element_type=F32)
    lane = lax.broadcasted_iota(jnp.int32, (1, PAIR), 1)
    head0 = lane < HEAD_DIM
    m0 = jnp.where(head0, 1.0, 0.0)
    m1 = 1.0 - m0
    t_start = (pl.program_id(0) % steps_per_batch) * PROJ_TM
    tok = t_start + lax.broadcasted_iota(jnp.int32, (PROJ_TM, PAIR), 0)
    col = lax.broadcasted_iota(jnp.int32, (PROJ_TM, PAIR), 1)
    indicator = jnp.where((tok >> SLC_SHIFT) == col - HEAD_DIM, 1.0, 0.0)

    def pair_normed(t, w):
        t2 = t * t
        ms0 = jnp.sum(t2 * m0, axis=-1, keepdims=True)
        ms1 = jnp.sum(t2 * m1, axis=-1, keepdims=True)
        return t * lax.rsqrt(jnp.where(head0, ms0, ms1) * (1.0 / HEAD_DIM) + EPS) * w

    ns = pair_normed(skwk[:, :PAIR], knw_ref[0:1, :])
    nwin = pair_normed(skwk[:, PAIR:], knw_ref[1:2, :])
    for g in range(NSA_KV_HEADS):
        s_g = ns if g == 0 else pltpu.roll(ns, HEAD_DIM, 1)
        w_g = nwin if g == 0 else pltpu.roll(nwin, HEAD_DIM, 1)
        ks_ref[0, g] = jnp.where(head0, s_g, indicator).astype(BF16)
        kw_ref[0, g] = jnp.where(head0, w_g, 0.0).astype(BF16)

    qt = lax.dot_general(wt_q_ref[...], h, NT_DIMS, preferred_element_type=F32)
    vt = lax.dot_general(wt_v_ref[...], h, NT_DIMS, preferred_element_type=F32)
    gt = lax.dot_general(wt_g_ref[...], h, NT_DIMS, preferred_element_type=F32)
    pad_row = lax.broadcasted_iota(jnp.int32, (V_ROWS - HEAD_DIM, LANES), 0)
    ones_pad = jnp.where(pad_row == 0, 1.0, 0.0).astype(BF16)
    for j in range(PROJ_TM // LANES):
        sl = slice(j * LANES, (j + 1) * LANES)
        nqt_ref[0, j] = qt[:, sl]
        for blk in range(2 * NSA_KV_HEADS):
            vt_ref[0, j, blk * V_ROWS:blk * V_ROWS + HEAD_DIM, :] = (
                vt[blk * HEAD_DIM:(blk + 1) * HEAD_DIM, sl].astype(BF16))
            vt_ref[0, j, blk * V_ROWS + HEAD_DIM:(blk + 1) * V_ROWS, :] = ones_pad
        glt_ref[0, j] = gt[:, sl]


def _proj(x2, nw, w_ret, w_ng, w_ckv, w_skwk, wt_q, wt_v, wt_g, knw, B, T):
    N = B * T
    tpb = T // PROJ_TM
    sub = PROJ_TM // LANES
    nt = T // LANES
    const = lambda i: (0, 0)
    row = lambda i: (i, 0)
    trn = lambda i: (i // tpb, i % tpb, 0, 0)
    tokm = lambda i: (i // tpb, 0, i % tpb, 0)
    return pl.pallas_call(
        functools.partial(_proj_kernel, tpb),
        grid=(N // PROJ_TM,),
        in_specs=[
            pl.BlockSpec((PROJ_TM, D_MODEL), row),
            pl.BlockSpec((1, D_MODEL), const),
            pl.BlockSpec(w_ret.shape, const),
            pl.BlockSpec(w_ng.shape, const),
            pl.BlockSpec(w_ckv.shape, const),
            pl.BlockSpec(w_skwk.shape, const),
            pl.BlockSpec(wt_q.shape, const),
            pl.BlockSpec(wt_v.shape, const),
            pl.BlockSpec(wt_g.shape, const),
            pl.BlockSpec(knw.shape, const),
        ],
        out_specs=[
            pl.BlockSpec((PROJ_TM, 4 * D_RET), row),
            pl.BlockSpec((PROJ_TM, D_NSA), row),
            pl.BlockSpec((1, 2 * NSA_KV_HEADS, PROJ_TM // CMP_STRIDE, CMP_STRIDE * HEAD_DIM), tokm),
            pl.BlockSpec((1, NSA_KV_HEADS, PROJ_TM, PAIR), tokm),
            pl.BlockSpec((1, NSA_KV_HEADS, PROJ_TM, PAIR), tokm),
            pl.BlockSpec((1, sub, D_NSA, LANES), trn),
            pl.BlockSpec((1, sub, 2 * NSA_KV_HEADS * V_ROWS, LANES), trn),
            pl.BlockSpec((1, sub, 32, LANES), trn),
        ],
        out_shape=[
            jax.ShapeDtypeStruct((N, 4 * D_RET), F32),
            jax.ShapeDtypeStruct((N, D_NSA), F32),
            jax.ShapeDtypeStruct((B, 2 * NSA_KV_HEADS, T // CMP_STRIDE, CMP_STRIDE * HEAD_DIM), F32),
            jax.ShapeDtypeStruct((B, NSA_KV_HEADS, T, PAIR), BF16),
            jax.ShapeDtypeStruct((B, NSA_KV_HEADS, T, PAIR), BF16),
            jax.ShapeDtypeStruct((B, nt, D_NSA, LANES), F32),
            jax.ShapeDtypeStruct((B, nt, 2 * NSA_KV_HEADS * V_ROWS, LANES), BF16),
            jax.ShapeDtypeStruct((B, nt, 32, LANES), F32),
        ],
        scratch_shapes=[pltpu.VMEM((2 * D_KV // LANES, PROJ_TM, LANES), F32)],
        compiler_params=pltpu.CompilerParams(dimension_semantics=("parallel",), vmem_limit_bytes=VMEM_LIMIT),
        name="proj",
    )(x2, nw, w_ret, w_ng, w_ckv, w_skwk, wt_q, wt_v, wt_g, knw)


def _ret_kernel(q_ref, k_ref, v_ref, g_ref, cos_ref, sin_ref, dec_ref, zeta_ref, xi_ref, gch_ref, nw_ref,
                o_ref, state_ref):
    @pl.when(pl.program_id(1) == 0)
    def _():
        state_ref[...] = jnp.zeros_like(state_ref)

    lane = lax.broadcasted_iota(jnp.int32, (1, PAIR), 1)
    q_head = (lane >> (HEAD_SHIFT - 1)) & 1
    v_head = lane >> HEAD_SHIFT
    q_mask = [jnp.where(q_head == h, 1.0, 0.0).astype(BF16) for h in (0, 1)]
    v_mask = [jnp.where(v_head == h, 1.0, 0.0).astype(BF16) for h in (0, 1)]
    row_qh = (lax.broadcasted_iota(jnp.int32, (PAIR, PAIR), 0) >> (HEAD_SHIFT - 1)) & 1
    row_vh = lax.broadcasted_iota(jnp.int32, (PAIR, PAIR), 0) >> HEAD_SHIFT
    col_vh = lax.broadcasted_iota(jnp.int32, (PAIR, PAIR), 1) >> HEAD_SHIFT
    same_head_kv = jnp.where(row_qh == col_vh, 1.0, 0.0)
    head_mean = jnp.where(row_vh == col_vh, 1.0 / HEAD_DIM, 0.0).astype(BF16)

    pairs = range(RET_HEADS // 2)
    mean2 = jnp.concatenate([head_mean, head_mean], axis=0)
    tok0 = pl.program_id(1) * RET_TC
    for c in range(RET_TC // RET_CHUNK):
        sl = pl.ds(c * RET_CHUNK, RET_CHUNK)
        pos = pl.ds(pl.multiple_of(tok0 + c * RET_CHUNK, RET_CHUNK), RET_CHUNK)
        cos = cos_ref[pos, :]
        sin = sin_ref[pos, :]
        cols = [slice(p * PAIR, (p + 1) * PAIR) for p in pairs]
        qb, kb, vb, vzb = [], [], [], []
        for p in pairs:
            q = q_ref[0, sl, cols[p]]
            k = k_ref[0, sl, cols[p]]
            v = v_ref[0, sl, cols[p]]
            qb.append((q * cos + pltpu.roll(q, HEAD_DIM, 1) * sin).astype(BF16))
            kb.append(((k * cos + pltpu.roll(k, HEAD_DIM, 1) * sin) * QK_SCALE).astype(BF16))
            vb.append(v.astype(BF16))
            vzb.append((v * zeta_ref[p]).astype(BF16))
        states = [state_ref[p] for p in pairs]
        kk = [jnp.concatenate([kb[p] * q_mask[h] for h in (0, 1)], axis=0) for p in pairs]
        s = [lax.dot_general(qb[p], kk[p], NT_DIMS, preferred_element_type=F32) for p in pairs]
        o_cross = [jnp.dot(qb[p], states[p].astype(BF16), preferred_element_type=F32) for p in pairs]
        kv = [lax.dot_general(kb[p], vzb[p], TN_DIMS, preferred_element_type=F32) for p in pairs]
        sb = [(s[p] * dec_ref[p]).astype(BF16) for p in pairs]
        vv = [jnp.concatenate([vb[p] * v_mask[h] for h in (0, 1)], axis=0) for p in pairs]
        o = [jnp.dot(sb[p], vv[p], preferred_element_type=F32) + o_cross[p] * xi_ref[p] for p in pairs]
        for p in pairs:
            state_ref[p] = states[p] * gch_ref[p] + kv[p] * same_head_kv
        o2 = [o[p] * o[p] for p in pairs]
        o2_hi = [o2[p].astype(BF16) for p in pairs]
        o2_hl = [jnp.concatenate([o2_hi[p], (o2[p] - o2_hi[p].astype(F32)).astype(BF16)], axis=1) for p in pairs]
        ms = [jnp.dot(o2_hl[p], mean2, preferred_element_type=F32) for p in pairs]
        for p in pairs:
            g = g_ref[0, sl, cols[p]]
            y = o[p] * lax.rsqrt(ms[p] + EPS) * nw_ref[p]
            o_ref[0, sl, cols[p]] = (y * (g * _sigmoid(g))).astype(BF16)


def _retention(ret3, cos, sin, dec, zeta, xi, gch, nw, B, T):
    npair = RET_HEADS // 2
    blk = lambda col: pl.BlockSpec((1, RET_TC, D_RET), lambda b, i: (b, i, col))
    whole = lambda a: pl.BlockSpec(a.shape, lambda b, i: (0,) * a.ndim)
    return pl.pallas_call(
        _ret_kernel,
        grid=(B, T // RET_TC),
        in_specs=[blk(0), blk(1), blk(2), blk(3), whole(cos), whole(sin), whole(dec), whole(zeta), whole(xi),
                  whole(gch), whole(nw)],
        out_specs=pl.BlockSpec((1, RET_TC, D_RET), lambda b, i: (b, i, 0)),
        out_shape=jax.ShapeDtypeStruct((B, T, D_RET), BF16),
        scratch_shapes=[pltpu.VMEM((npair, PAIR, PAIR), F32)],
        compiler_params=pltpu.CompilerParams(dimension_semantics=("parallel", "arbitrary"),
                                             vmem_limit_bytes=VMEM_LIMIT),
        name="retention",
    )(ret3, ret3, ret3, ret3, cos, sin, dec, zeta, xi, gch, nw)


def _cmp_kernel(x_ref, pos_ref, w1_ref, w2_ref, w2t_ref, knw_ref, o_ref, ot_ref):
    is_key = pl.program_id(1) == 0
    half = CMP_STRIDE * HEAD_DIM
    for g in range(NSA_KV_HEADS):
        x = x_ref[0, g]
        a = jnp.dot((x + pos_ref[0, 0:1, :]).astype(BF16), w1_ref[0, :half, :], preferred_element_type=F32)
        b = jnp.dot((x + pos_ref[0, 1:2, :]).astype(BF16), w1_ref[0, half:, :], preferred_element_type=F32)
        hid = a + pltpu.roll(b, b.shape[0] - 1, 0)
        hid = (hid * _sigmoid(hid)).astype(BF16)
        out = jnp.dot(hid, w2_ref[0], preferred_element_type=F32)
        ms = jnp.mean(out * out, axis=-1, keepdims=True)
        normed = out * lax.rsqrt(ms + EPS) * knw_ref[...]
        o_ref[0, 0, g] = jnp.where(is_key, normed, out)
        ot_ref[0, 0, g] = lax.dot_general(w2t_ref[0], hid, NT_DIMS, preferred_element_type=F32)


def _compress(xc, pos, w1, w2, w2t, knw, B, ncb):
    return pl.pallas_call(
        _cmp_kernel,
        grid=(B, 2),
        in_specs=[
            pl.BlockSpec((1, NSA_KV_HEADS, ncb, CMP_STRIDE * HEAD_DIM), lambda b, s: (b, s, 0, 0)),
            pl.BlockSpec((1, 2, CMP_STRIDE * HEAD_DIM), lambda b, s: (s, 0, 0)),
            pl.BlockSpec((1, CMP_BLOCK * HEAD_DIM, CMP_HIDDEN), lambda b, s: (s, 0, 0)),
            pl.BlockSpec((1, CMP_HIDDEN, HEAD_DIM), lambda b, s: (s, 0, 0)),
            pl.BlockSpec((1, HEAD_DIM, CMP_HIDDEN), lambda b, s: (s, 0, 0)),
            pl.BlockSpec((1, HEAD_DIM), lambda b, s: (0, 0)),
        ],
        out_specs=[
            pl.BlockSpec((1, 1, NSA_KV_HEADS, ncb, HEAD_DIM), lambda b, s: (b, s, 0, 0, 0)),
            pl.BlockSpec((1, 1, NSA_KV_HEADS, HEAD_DIM, ncb), lambda b, s: (b, s, 0, 0, 0)),
        ],
        out_shape=[
            jax.ShapeDtypeStruct((B, 2, NSA_KV_HEADS, ncb, HEAD_DIM), F32),
            jax.ShapeDtypeStruct((B, 2, NSA_KV_HEADS, HEAD_DIM, ncb), F32),
        ],
        compiler_params=pltpu.CompilerParams(dimension_semantics=("parallel", "parallel"),
                                             vmem_limit_bytes=VMEM_LIMIT),
        name="compress",
    )(xc, pos, w1, w2, w2t, knw)


def _nsa_kernel(qt_ref, qtn_ref, glt_ref, ng_ref, ks_ref, kw_ref, vst_ref, vwt_ref, kc_ref, vct_ref, mt_ref, qnw_ref,
                bg_ref, o_ref, qp_ref, oc_ref, sa_ref, sb_ref, mxa_ref, mxb_ref, sw_ref, m_ref, acc_ref):
    qi = pl.program_id(1)
    t0 = qi * (NSA_QT * NSA_TQ)
    tiles = range(NSA_QT)
    chains = [(qt, g) for qt in tiles for g in range(NSA_KV_HEADS)]
    gq = NSA_GROUP * HEAD_DIM
    gg = 16
    ncb = kc_ref.shape[3]
    n_slc = mt_ref.shape[0]

    def select_masks(base_t0):
        n_idx = lax.broadcasted_iota(jnp.int32, (ncb, NSA_TQ), 0)
        jb = lax.broadcasted_iota(jnp.int32, (n_slc, NSA_TQ), 0)
        out = []
        for qt in tiles:
            tok = base_t0 + qt * NSA_TQ
            tok_c = tok + lax.broadcasted_iota(jnp.int32, (ncb, NSA_TQ), 1)
            cbias = _tile4(jnp.where((n_idx * CMP_STRIDE + (CMP_BLOCK - 1)) <= tok_c, 0.0, NEG))
            tok_row = tok + lax.broadcasted_iota(jnp.int32, (1, NSA_TQ), 1)
            has_block = _tile4(jnp.where(tok_row >= CMP_BLOCK - 1, 1.0, 0.0))
            tok_s = tok + lax.broadcasted_iota(jnp.int32, (n_slc, NSA_TQ), 1)
            valid_s = jb * SLC_BLOCK <= tok_s
            force = (jb == (tok_s >> SLC_SHIFT)) | (jb == 0)
            out.append((cbias, has_block, valid_s, force))
        return out

    def select_scores(src_ref, slot, masks, c, rows=ncb):
        qt, g = chains[c]
        cols = []
        for r in range(NSA_GROUP):
            q = src_ref[0, qt, g * gq + r * HEAD_DIM:g * gq + (r + 1) * HEAD_DIM, :]
            ms = jnp.mean(q * q, axis=0, keepdims=True)
            cols.append(q * lax.rsqrt(ms + EPS) * qnw_ref[...] * (QK_SCALE * LOG2E))
        qs = jnp.concatenate(cols, axis=1).astype(BF16)
        qp_ref[slot, c, 0:HEAD_DIM, :] = qs
        return (jnp.dot(kc_ref[0, 0, g, 0:rows, :].astype(BF16), qs, preferred_element_type=F32)
                + masks[qt][0][0:rows, :])

    def select_probs(sc, slot, masks, c, want_scores=True):
        qt, g = chains[c]
        _, has_block, valid_s, force = masks[qt]
        rows = sc.shape[0]
        mc = jnp.max(sc, axis=0, keepdims=True)
        ec = jnp.exp2(sc - mc)
        lc = jnp.sum(ec, axis=0, keepdims=True)
        p = ec * (has_block / lc)
        oc_ref[slot, c] = jnp.dot(vct_ref[0, 0, g, :, 0:rows].astype(BF16), p.astype(BF16),
                                  preferred_element_type=F32)
        if not want_scores:
            return None
        ps = p[:, 0:NSA_TQ]
        for r in range(1, NSA_GROUP):
            ps = ps + p[:, r * NSA_TQ:(r + 1) * NSA_TQ]
        ps_hi = ps.astype(BF16)
        ps_lo = (ps - ps_hi.astype(F32)).astype(BF16)
        imp = (jnp.dot(mt_ref[:, 0:rows], ps_hi, preferred_element_type=F32)
               + jnp.dot(mt_ref[:, 0:rows], ps_lo, preferred_element_type=F32))
        return jnp.where(valid_s, jnp.where(force, imp + FORCE_BONUS, imp), NEG)

    def select_rank(scores, valid_s, slot, n_live):
        if n_live <= SLC_TOPK:
            for c, (qt, g) in enumerate(chains):
                qp_ref[slot, c, HEAD_DIM:2 * HEAD_DIM, :] = _tile4(jnp.where(valid_s[qt], 0.0, NEG).astype(BF16))
            return
        sub = lax.broadcasted_iota(jnp.int32, (SUBLANES, NSA_TQ), 0)
        for c, (qt, g) in enumerate(chains):
            score = scores[c]
            blocks = [score[v * SUBLANES:(v + 1) * SUBLANES, :] for v in range(n_live // SUBLANES)]
            ranks = [jnp.zeros((SUBLANES, NSA_TQ), F32) for _ in blocks]
            for i in range(n_live):
                row = score[i:i + 1, :]
                for v, blk in enumerate(blocks):
                    if v * SUBLANES > i:
                        beats = row >= blk
                    elif (v + 1) * SUBLANES <= i:
                        beats = row > blk
                    else:
                        beats = (row > blk) | ((row >= blk) & (sub > i - v * SUBLANES))
                    ranks[v] = ranks[v] + jnp.where(beats, 1.0, 0.0)
            dead = [jnp.full((SUBLANES, NSA_TQ), float(n_slc), F32)] * ((n_slc - n_live) // SUBLANES)
            rank = jnp.concatenate(ranks + dead, axis=0)
            sel = (rank < float(SLC_TOPK)) & valid_s[qt]
            qp_ref[slot, c, HEAD_DIM:2 * HEAD_DIM, :] = _tile4(jnp.where(sel, 0.0, NEG).astype(BF16))

    step_tokens = NSA_QT * NSA_TQ

    def live_blocks(step):
        return ((step + 1) * step_tokens - 1) // SLC_BLOCK + 1

    all_chains = list(range(len(chains)))

    @pl.when(qi == 0)
    def _():
        masks0 = select_masks(0)
        rows0 = min(ncb, -(-((step_tokens - CMP_BLOCK) // CMP_STRIDE + 1) // 16) * 16)
        assert live_blocks(0) <= SLC_TOPK
        for c in all_chains:
            select_probs(select_scores(qt_ref, 0, masks0, c, rows0), 0, masks0, c, want_scores=False)
        select_rank(None, [m[2] for m in masks0], 0, live_blocks(0))

    cur = qi % 2
    nxt = 1 - cur
    t_next = t0 + NSA_QT * NSA_TQ

    kt0, ks0, wbias = [], [], []
    c_minus_r = (lax.broadcasted_iota(jnp.int32, (WIN_KEYS, NSA_TQ), 1)
                 - lax.broadcasted_iota(jnp.int32, (WIN_KEYS, NSA_TQ), 0))
    for qt in tiles:
        kt0.append(jnp.maximum(qi * NSA_QT + qt - WIN_SIZE // NSA_TQ, 0))
        ks0.append(pl.multiple_of(kt0[qt] * NSA_TQ, NSA_TQ))
        delta = (t0 + qt * NSA_TQ - ks0[qt]) + c_minus_r
        in_window = lax.bitcast_convert_type(delta, jnp.uint32) < WIN_SIZE
        wbias.append(_tile4(jnp.where(in_window, 0.0, NEG)))

    def win_scores(c):
        qt, g = chains[c]
        sw = jnp.dot(kw_ref[0, g, pl.ds(ks0[qt], WIN_KEYS), 0:HEAD_DIM], qp_ref[cur, c, 0:HEAD_DIM, :],
                     preferred_element_type=F32) + wbias[qt]
        sw_ref[c] = sw
        return jnp.max(sw, axis=0, keepdims=True)

    def win_attend(c, mw):
        qt, g = chains[c]
        ewb = jnp.exp2(sw_ref[c] - mw).astype(BF16)
        vwt = jnp.concatenate([vwt_ref[0, kt0[qt] + j, g * V_ROWS:(g + 1) * V_ROWS, :]
                               for j in range(WIN_KEYS // LANES)], axis=1)
        ow_aug = jnp.dot(vwt, ewb, preferred_element_type=F32)
        return ow_aug[0:HEAD_DIM, :] * (1.0 / ow_aug[HEAD_DIM:HEAD_DIM + 1, :])

    vt_per_tile = SLC_TK // LANES

    def slc_scores(j, dst_ref, mx_ref, which=all_chains):
        kst = pl.multiple_of(j * SLC_TK, SLC_TK)
        for c in which:
            qt, g = chains[c]
            s = jnp.dot(ks_ref[0, g, pl.ds(kst, SLC_TK), :], qp_ref[cur, c], preferred_element_type=F32)
            dst_ref[c] = s
            mx_ref[c] = jnp.max(s, axis=0, keepdims=True)

    def slc_update(j, src_ref, mx_ref, causal, which=all_chains, visible=SLC_TK):
        if causal:
            band = _tile4(jnp.where(lax.broadcasted_iota(jnp.int32, (NSA_TQ, NSA_TQ), 0)
                                    <= lax.broadcasted_iota(jnp.int32, (NSA_TQ, NSA_TQ), 1), 0.0, NEG))
        for c in which:
            qt, g = chains[c]
            if causal:
                r0 = visible - step_tokens + qt * NSA_TQ
                rows = r0 + NSA_TQ
                s_band = src_ref[c, r0:rows, :] + band
                s = jnp.concatenate([src_ref[c, 0:r0, :], s_band], axis=0) if r0 > 0 else s_band
                tile_max = jnp.max(s, axis=0, keepdims=True)
            else:
                rows = SLC_TK
                s = src_ref[c]
                tile_max = mx_ref[c]
            m_old = m_ref[c]
            m_new = jnp.maximum(m_old, tile_max)
            alpha = jnp.exp2(m_old - m_new)
            eb = jnp.exp2(s - m_new).astype(BF16)
            vt = jnp.concatenate([vst_ref[0, j * vt_per_tile + jj, g * V_ROWS:(g + 1) * V_ROWS, :]
                                  for jj in range(rows // LANES)], axis=1)
            acc_ref[c] = alpha * acc_ref[c] + jnp.dot(vt, eb, preferred_element_type=F32)
            m_ref[c] = m_new

    masks = select_masks(t_next)

    def matmul_stage(c):
        mw = win_scores(c)
        sc = select_scores(qtn_ref, nxt, masks, c)
        slc_scores(0, sa_ref, mxa_ref, [c])
        return sc, mw

    sel_scores, ow_t = [], []
    staged = matmul_stage(0)
    for c in all_chains:
        staged_next = matmul_stage(c + 1) if c + 1 < len(chains) else None
        sel_scores.append(select_probs(staged[0], nxt, masks, c))
        ow_t.append(win_attend(c, staged[1]))
        staged = staged_next
    sel_valid = [m[2] for m in masks]

    need = live_blocks(qi + 1)
    bounds = list(range(SLC_TOPK, n_slc + 1, SUBLANES))
    for lo, hi in zip([0] + bounds[:-1], bounds):
        in_range = (need > lo) if hi == bounds[-1] else ((need > lo) & (need <= hi))
        pl.when(in_range)(functools.partial(select_rank, sel_scores, sel_valid, nxt, hi))

    m_ref[...] = jnp.full(m_ref.shape, NEG, F32)
    acc_ref[...] = jnp.zeros(acc_ref.shape, F32)
    n_full = t0 // SLC_TK

    def pair(jj, carry):
        j = 2 * jj
        for c in all_chains:
            slc_scores(j + 1, sb_ref, mxb_ref, [c])
            slc_update(j, sa_ref, mxa_ref, False, [c])
        for c in all_chains:
            slc_scores(j + 2, sa_ref, mxa_ref, [c])
            slc_update(j + 1, sb_ref, mxb_ref, False, [c])
        return carry

    lax.fori_loop(0, n_full // 2, pair, 0)

    visible = t0 - n_full * SLC_TK + step_tokens

    def tail(odd, rows):
        if odd:
            for c in all_chains:
                slc_scores(n_full, sb_ref, mxb_ref, [c])
                slc_update(n_full - 1, sa_ref, mxa_ref, False, [c])
            slc_update(n_full, sb_ref, mxb_ref, True, visible=rows)
        else:
            slc_update(n_full, sa_ref, mxa_ref, True, visible=rows)

    for rows in range(step_tokens, SLC_TK + 1, step_tokens):
        for odd in (False, True):
            parity = (n_full % 2 == 1) if odd else (n_full % 2 == 0)
            pl.when((visible == rows) & parity)(functools.partial(tail, odd, rows))

    for qt in tiles:
        gates = _sigmoid(glt_ref[0, qt] + bg_ref[...])
        outs = []
        for g in range(NSA_KV_HEADS):
            c = qt * NSA_KV_HEADS + g
            os_t = acc_ref[c, 0:HEAD_DIM, :] * (1.0 / acc_ref[c, HEAD_DIM:HEAD_DIM + 1, :])
            oc_t = oc_ref[cur, c]
            for r in range(NSA_GROUP):
                sl = slice(r * NSA_TQ, (r + 1) * NSA_TQ)
                row = g * gg + r
                outs.append(gates[row:row + 1, :] * oc_t[:, sl]
                            + gates[row + NSA_GROUP:row + NSA_GROUP + 1, :] * os_t[:, sl]
                            + gates[row + 2 * NSA_GROUP:row + 2 * NSA_GROUP + 1, :] * ow_t[c][:, sl])
        o_tok = jnp.concatenate(outs, axis=0).T
        rows = pl.ds(qt * NSA_TQ, NSA_TQ)
        ng = ng_ref[0, rows, :]
        o_ref[0, rows, :] = (o_tok * (ng * _sigmoid(ng))).astype(BF16)


def _nsa(nqt, glt, ng3, ks, kw, vt, kc, vct, mt, qnw, bg, B, T):
    nt = T // LANES
    ncb = kc.shape[3]
    G = NSA_KV_HEADS
    NC = NSA_QT * G
    ncols = NSA_GROUP * NSA_TQ
    steps = T // (NSA_QT * NSA_TQ)
    return pl.pallas_call(
        _nsa_kernel,
        grid=(B, steps),
        in_specs=[
            pl.BlockSpec((1, NSA_QT, D_NSA, LANES), lambda b, i: (b, 0, 0, 0)),
            pl.BlockSpec((1, NSA_QT, D_NSA, LANES), lambda b, i: (b, jnp.minimum(i + 1, steps - 1), 0, 0)),
            pl.BlockSpec((1, NSA_QT, 16 * G, LANES), lambda b, i: (b, i, 0, 0)),
            pl.BlockSpec((1, NSA_QT * NSA_TQ, D_NSA), lambda b, i: (b, i, 0)),
            pl.BlockSpec((1, G, T, PAIR), lambda b, i: (b, 0, 0, 0)),
            pl.BlockSpec((1, G, T, PAIR), lambda b, i: (b, 0, 0, 0)),
            pl.BlockSpec((1, nt, G * V_ROWS, LANES), lambda b, i: (b, 0, 0, 0)),
            pl.BlockSpec((1, nt, G * V_ROWS, LANES), lambda b, i: (b, 0, 1, 0)),
            pl.BlockSpec((1, 1, G, ncb, HEAD_DIM), lambda b, i: (b, 0, 0, 0, 0)),
            pl.BlockSpec((1, 1, G, HEAD_DIM, ncb), lambda b, i: (b, 1, 0, 0, 0)),
            pl.BlockSpec(mt.shape, lambda b, i: (0, 0)),
            pl.BlockSpec((HEAD_DIM, 1), lambda b, i: (0, 0)),
            pl.BlockSpec((16 * G, 1), lambda b, i: (0, 0)),
        ],
        out_specs=pl.BlockSpec((1, NSA_QT * NSA_TQ, D_NSA), lambda b, i: (b, i, 0)),
        out_shape=jax.ShapeDtypeStruct((B, T, D_NSA), BF16),
        scratch_shapes=[
            pltpu.VMEM((2, NC, 2 * HEAD_DIM, ncols), BF16),
            pltpu.VMEM((2, NC, HEAD_DIM, ncols), F32),
            pltpu.VMEM((NC, SLC_TK, ncols), F32),
            pltpu.VMEM((NC, SLC_TK, ncols), F32),
            pltpu.VMEM((NC, 1, ncols), F32),
            pltpu.VMEM((NC, 1, ncols), F32),
            pltpu.VMEM((NC, WIN_KEYS, ncols), F32),
            pltpu.VMEM((NC, 1, ncols), F32),
            pltpu.VMEM((NC, V_ROWS, ncols), F32),
        ],
        compiler_params=pltpu.CompilerParams(dimension_semantics=("parallel", "arbitrary"),
                                             vmem_limit_bytes=VMEM_LIMIT),
        name="nsa",
    )(nqt, nqt, glt, ng3, ks, kw, vt, vt, kc, vct, mt, qnw, bg)


def _out_kernel(x_ref, yr_ref, yn_ref, wr_ref, wn_ref, o_ref):
    o_ref[...] = (x_ref[...]
                  + jnp.dot(yr_ref[...], wr_ref[...], preferred_element_type=F32)
                  + jnp.dot(yn_ref[...], wn_ref[...], preferred_element_type=F32))


def _outproj(x2, yr, yn, wr, wn):
    N = x2.shape[0]
    row = lambda i: (i, 0)
    const = lambda i: (0, 0)
    return pl.pallas_call(
        _out_kernel,
        grid=(N // OUT_TM,),
        in_specs=[pl.BlockSpec((OUT_TM, D_MODEL), row), pl.BlockSpec((OUT_TM, D_RET), row),
                  pl.BlockSpec((OUT_TM, D_NSA), row), pl.BlockSpec(wr.shape, const), pl.BlockSpec(wn.shape, const)],
        out_specs=pl.BlockSpec((OUT_TM, D_MODEL), row),
        out_shape=jax.ShapeDtypeStruct((N, D_MODEL), F32),
        compiler_params=pltpu.CompilerParams(dimension_semantics=("parallel",), vmem_limit_bytes=VMEM_LIMIT),
        name="outproj",
    )(x2, yr, yn, wr, wn)


@functools.lru_cache(maxsize=None)
def _tables(T):
    half = HEAD_DIM // 2
    inv = ROPE_THETA ** (-np.arange(half, dtype=np.float64) / half)
    ang = np.arange(T, dtype=np.float64)[:, None] * inv[None, :]
    cos = np.concatenate([np.cos(ang)] * 4, axis=1).astype(np.float32)
    sin = np.concatenate([-np.sin(ang), -np.sin(ang), np.sin(ang), np.sin(ang)], axis=1).astype(np.float32)

    C = RET_CHUNK
    log_g = np.log1p(-np.exp2(-5.0 - np.arange(RET_HEADS, dtype=np.float64)))
    pos = np.arange(C, dtype=np.float64)
    diff = pos[:, None] - pos[None, :]
    decay = np.where(diff >= 0, np.exp(log_g[:, None, None] * np.maximum(diff, 0.0)), 0.0)
    zeta = np.exp(log_g[:, None] * (C - 1.0 - pos))
    xi = np.exp(log_g[:, None] * (pos + 1.0))
    g_chunk = np.exp(log_g * C)
    npair = RET_HEADS // 2

    def pair_lanes(a):
        return np.repeat(a.reshape(npair, 2, C).transpose(0, 2, 1), HEAD_DIM, axis=2).astype(np.float32)

    dec = decay.reshape(npair, 2, C, C).transpose(0, 2, 1, 3).reshape(npair, C, 2 * C).astype(np.float32)
    gch = np.repeat(g_chunk.reshape(npair, 1, 2), HEAD_DIM, axis=2).astype(np.float32)

    n_cmp = (T - CMP_BLOCK) // CMP_STRIDE + 1
    ncb = T // CMP_STRIDE
    p = np.arange(n_cmp)[:, None] * CMP_STRIDE + np.arange(CMP_BLOCK)[None, :]
    blk = p // SLC_BLOCK
    M = (blk[:, :, None] == np.arange(T // SLC_BLOCK)[None, None, :]).mean(axis=1)
    mt = np.zeros((T // SLC_BLOCK, ncb), np.float32)
    mt[:, :n_cmp] = M.T
    return cos, sin, dec, pair_lanes(zeta), pair_lanes(xi), gch, mt


def kernel(x, norm_w, w_in, ret_norm_w, q_norm_w, k_norm_cmp, k_norm_slc, k_norm_win, cmp_pos_k, cmp_w1_k, cmp_w2_k,
           cmp_pos_v, cmp_w1_v, cmp_w2_v, b_gate, w_out):
    B, T, D = x.shape
    depth = norm_w.shape[0]
    cos, sin, dec, zeta, xi, gch, mt = _tables(T)
    ncb = T // CMP_STRIDE
    half = CMP_STRIDE * HEAD_DIM
    gate_src = np.zeros((NSA_KV_HEADS, 16), np.int32)
    gate_ok = np.zeros((NSA_KV_HEADS, 16), bool)
    for g in range(NSA_KV_HEADS):
        for br in range(N_BRANCH):
            for r in range(NSA_GROUP):
                gate_src[g, br * NSA_GROUP + r] = br * NSA_HEADS + g * NSA_GROUP + r
                gate_ok[g, br * NSA_GROUP + r] = True
    gate_src = gate_src.reshape(-1)
    gate_ok = gate_ok.reshape(-1)

    x2 = x.reshape(B * T, D)
    for layer in range(depth):
        w = w_in[layer].astype(BF16)
        o_ng = 4 * D_RET + D_NSA
        o_kv = o_ng + D_NSA
        quarter = HEAD_DIM // 2
        pair_perm = np.concatenate([np.arange(quarter), HEAD_DIM + np.arange(quarter),
                                    quarter + np.arange(quarter), HEAD_DIM + quarter + np.arange(quarter)])
        qk_perm = np.concatenate([p * PAIR + pair_perm for p in range(RET_HEADS // 2)])
        w_ret = jnp.concatenate([w[:, :D_RET][:, qk_perm], w[:, D_RET:2 * D_RET][:, qk_perm],
                                 w[:, 2 * D_RET:4 * D_RET]], axis=1)
        wt_q = w[:, 4 * D_RET:o_ng].T
        w_ng = w[:, o_ng:o_kv]
        w_ckv = w[:, o_kv:o_kv + 2 * D_KV]
        w_skwk = jnp.concatenate([w[:, o_kv + 2 * D_KV:o_kv + 3 * D_KV], w[:, o_kv + 4 * D_KV:o_kv + 5 * D_KV]], axis=1)
        wt_v = jnp.concatenate([w[:, o_kv + 3 * D_KV:o_kv + 4 * D_KV], w[:, o_kv + 5 * D_KV:o_kv + 6 * D_KV]], axis=1).T
        w_gl = w[:, o_kv + 6 * D_KV:]
        wt_g = jnp.where(gate_ok[:, None], w_gl.T[gate_src], jnp.zeros((), BF16))
        bg = jnp.where(gate_ok, b_gate[layer][gate_src], 0.0).reshape(-1, 1)

        knw = jnp.stack([jnp.tile(k_norm_slc[layer], 2), jnp.tile(k_norm_win[layer], 2)])
        ret, ng, xc, ks, kw, nqt, vt, glt = _proj(x2, norm_w[layer].reshape(1, D), w_ret, w_ng, w_ckv, w_skwk,
                                                  wt_q, wt_v, wt_g, knw, B, T)

        nw_pair = ret_norm_w[layer].reshape(RET_HEADS // 2, 1, PAIR)
        y_ret = _retention(ret.reshape(B, T, 4 * D_RET), jnp.asarray(cos), jnp.asarray(sin), jnp.asarray(dec),
                           jnp.asarray(zeta), jnp.asarray(xi), jnp.asarray(gch), nw_pair, B, T)

        pos =jnp.stack([cmp_pos_k[layer], cmp_pos_v[layer]]).reshape(2, 2, half)
        w1 = jnp.stack([cmp_w1_k[layer], cmp_w1_v[layer]]).astype(BF16)
        w2 = jnp.stack([cmp_w2_k[layer], cmp_w2_v[layer]]).astype(BF16)
        w2t = jnp.swapaxes(w2, 1, 2)
        kc, vct = _compress(xc, pos, w1, w2, w2t, k_norm_cmp[layer].reshape(1, HEAD_DIM), B, ncb)

        y_nsa = _nsa(nqt, glt, ng.reshape(B, T, D_NSA), ks, kw, vt, kc, vct, jnp.asarray(mt).astype(BF16),
                     q_norm_w[layer].reshape(HEAD_DIM, 1), bg, B, T)

        wo = w_out[layer].astype(BF16)
        x2 = _outproj(x2, y_ret.reshape(B * T, D_RET), y_nsa.reshape(B * T, D_NSA), wo[:D_RET], wo[D_RET:])
    return x2.reshape(B, T, D)
```

```python
import functools

import numpy as np
import jax
import jax.numpy as jnp
from jax import lax
from jax.experimental import pallas as pl
from jax.experimental.pallas import tpu as pltpu

F32 = jnp.float32
BF16 = jnp.bfloat16

D_MODEL = 1024
HEAD_DIM = 64
HEAD_SHIFT = 6
RET_HEADS = 8
NSA_HEADS = 8
NSA_KV_HEADS = 2
NSA_GROUP = NSA_HEADS // NSA_KV_HEADS
D_RET = RET_HEADS * HEAD_DIM
D_NSA = NSA_HEADS * HEAD_DIM
D_KV = NSA_KV_HEADS * HEAD_DIM
N_BRANCH = 3
RET_CHUNK = 128
ROPE_THETA = 10000.0
CMP_BLOCK = 32
CMP_STRIDE = 16
CMP_HIDDEN = 256
SLC_BLOCK = 64
SLC_SHIFT = 6
SLC_TOPK = 16
WIN_SIZE = 512
EPS = 1e-6
NEG = -1e30
FORCE_BONUS = 1e4
QK_SCALE = HEAD_DIM ** -0.5
LOG2E = 1.4426950408889634
V_ROWS = HEAD_DIM + 16

LANES = 128
SUBLANES = 8
PAIR = 2 * HEAD_DIM
VMEM_LIMIT = 48 * 1024 * 1024

PROJ_TM = 512
RET_TC = 1024
NSA_TQ = 128
NSA_QT = 2
SLC_TK = 512
WIN_KEYS = WIN_SIZE + NSA_TQ
OUT_TM = 1024

NT_DIMS = (((1,), (1,)), ((), ()))
TN_DIMS = (((0,), (0,)), ((), ()))


def _sigmoid(x):
    return 1.0 / (1.0 + jnp.exp(-x))


def _tile4(a):
    return jnp.concatenate([a, a, a, a], axis=1)


def _proj_kernel(steps_per_batch, x_ref, nw_ref, w_ret_ref, w_ng_ref, w_ckv_ref, w_skwk_ref, wt_q_ref, wt_v_ref,
                 wt_g_ref, knw_ref, ret_ref, ng_ref, xc_ref, ks_ref, kw_ref, nqt_ref, vt_ref, glt_ref, ckv_scr):
    x = x_ref[...]
    ms = jnp.mean(x * x, axis=-1, keepdims=True)
    h = (x * lax.rsqrt(ms + EPS) * nw_ref[...]).astype(BF16)
    ret_ref[...] = jnp.dot(h, w_ret_ref[...], preferred_element_type=F32)
    ng_ref[...] = jnp.dot(h, w_ng_ref[...], preferred_element_type=F32)

    ckv = jnp.dot(h, w_ckv_ref[...], preferred_element_type=F32)
    for half in range(2 * D_KV // LANES):
        ckv_scr[half] = ckv[:, half * LANES:(half + 1) * LANES]
    for l in range(CMP_STRIDE):
        for half in range(2 * D_KV // LANES):
            rows = ckv_scr[half, pl.ds(l, PROJ_TM // CMP_STRIDE, stride=CMP_STRIDE), :]
            for s in range(LANES // HEAD_DIM):
                xc_ref[0, half * (LANES // HEAD_DIM) + s, :, l * HEAD_DIM:(l + 1) * HEAD_DIM] = (
                    rows[:, s * HEAD_DIM:(s + 1) * HEAD_DIM])

    skwk = jnp.dot(h, w_skwk_ref[...], preferred_element_type=F32)
    lane = lax.broadcasted_iota(jnp.int32, (1, PAIR), 1)
    head0 = lane < HEAD_DIM
    m0 = jnp.where(head0, 1.0, 0.0)
    m1 = 1.0 - m0
    t_start = (pl.program_id(0) % steps_per_batch) * PROJ_TM
    tok = t_start + lax.broadcasted_iota(jnp.int32, (PROJ_TM, PAIR), 0)
    col = lax.broadcasted_iota(jnp.int32, (PROJ_TM, PAIR), 1)
    indicator = jnp.where((tok >> SLC_SHIFT) == col - HEAD_DIM, 1.0, 0.0)

    def pair_normed(t, w):
        t2 = t * t
        ms0 = jnp.sum(t2 * m0, axis=-1, keepdims=True)
        ms1 = jnp.sum(t2 * m1, axis=-1, keepdims=True)
        return t * lax.rsqrt(jnp.where(head0, ms0, ms1) * (1.0 / HEAD_DIM) + EPS) * w

    ns = pair_normed(skwk[:, :PAIR], knw_ref[0:1, :])
    nwin = pair_normed(skwk[:, PAIR:], knw_ref[1:2, :])
    for g in range(NSA_KV_HEADS):
        s_g = ns if g == 0 else pltpu.roll(ns, HEAD_DIM, 1)
        w_g = nwin if g == 0 else pltpu.roll(nwin, HEAD_DIM, 1)
        ks_ref[0, g] = jnp.where(head0, s_g, indicator).astype(BF16)
        kw_ref[0, g] = jnp.where(head0, w_g, 0.0).astype(BF16)

    qt = lax.dot_general(wt_q_ref[...], h, NT_DIMS, preferred_element_type=F32)
    vt = lax.dot_general(wt_v_ref[...], h, NT_DIMS, preferred_element_type=F32)
    gt = lax.dot_general(wt_g_ref[...], h, NT_DIMS, preferred_element_type=F32)
    pad_row = lax.broadcasted_iota(jnp.int32, (V_ROWS - HEAD_DIM, LANES), 0)
    ones_pad = jnp.where(pad_row == 0, 1.0, 0.0).astype(BF16)
    for j in range(PROJ_TM // LANES):
        sl = slice(j * LANES, (j + 1) * LANES)
        nqt_ref[0, j] = qt[:, sl]
        for blk in range(2 * NSA_KV_HEADS):
            vt_ref[0, j, blk * V_ROWS:blk * V_ROWS + HEAD_DIM, :] = (
                vt[blk * HEAD_DIM:(blk + 1) * HEAD_DIM, sl].astype(BF16))
            vt_ref[0, j, blk * V_ROWS + HEAD_DIM:(blk + 1) * V_ROWS, :] = ones_pad
        glt_ref[0, j] = gt[:, sl]


def _proj(x2, nw, w_ret, w_ng, w_ckv, w_skwk, wt_q, wt_v, wt_g, knw, B, T):
    N = B * T
    tpb = T // PROJ_TM
    sub = PROJ_TM // LANES
    nt = T // LANES
    const = lambda i: (0, 0)
    row = lambda i: (i, 0)
    trn = lambda i: (i // tpb, i % tpb, 0, 0)
    tokm = lambda i: (i // tpb, 0, i % tpb, 0)
    return pl.pallas_call(
        functools.partial(_proj_kernel, tpb),
        grid=(N // PROJ_TM,),
        in_specs=[
            pl.BlockSpec((PROJ_TM, D_MODEL), row),
            pl.BlockSpec((1, D_MODEL), const),
            pl.BlockSpec(w_ret.shape, const),
            pl.BlockSpec(w_ng.shape, const),
            pl.BlockSpec(w_ckv.shape, const),
            pl.BlockSpec(w_skwk.shape, const),
            pl.BlockSpec(wt_q.shape, const),
            pl.BlockSpec(wt_v.shape, const),
            pl.BlockSpec(wt_g.shape, const),
            pl.BlockSpec(knw.shape, const),
        ],
        out_specs=[
            pl.BlockSpec((PROJ_TM, 4 * D_RET), row),
            pl.BlockSpec((PROJ_TM, D_NSA), row),
            pl.BlockSpec((1, 2 * NSA_KV_HEADS, PROJ_TM // CMP_STRIDE, CMP_STRIDE * HEAD_DIM), tokm),
            pl.BlockSpec((1, NSA_KV_HEADS, PROJ_TM, PAIR), tokm),
            pl.BlockSpec((1, NSA_KV_HEADS, PROJ_TM, PAIR), tokm),
            pl.BlockSpec((1, sub, D_NSA, LANES), trn),
            pl.BlockSpec((1, sub, 2 * NSA_KV_HEADS * V_ROWS, LANES), trn),
            pl.BlockSpec((1, sub, 32, LANES), trn),
        ],
        out_shape=[
            jax.ShapeDtypeStruct((N, 4 * D_RET), F32),
            jax.ShapeDtypeStruct((N, D_NSA), F32),
            jax.ShapeDtypeStruct((B, 2 * NSA_KV_HEADS, T // CMP_STRIDE, CMP_STRIDE * HEAD_DIM), F32),
            jax.ShapeDtypeStruct((B, NSA_KV_HEADS, T, PAIR), BF16),
            jax.ShapeDtypeStruct((B, NSA_KV_HEADS, T, PAIR), BF16),
            jax.ShapeDtypeStruct((B, nt, D_NSA, LANES), F32),
            jax.ShapeDtypeStruct((B, nt, 2 * NSA_KV_HEADS * V_ROWS, LANES), BF16),
            jax.ShapeDtypeStruct((B, nt, 32, LANES), F32),
        ],
        scratch_shapes=[pltpu.VMEM((2 * D_KV // LANES, PROJ_TM, LANES), F32)],
        compiler_params=pltpu.CompilerParams(dimension_semantics=("parallel",), vmem_limit_bytes=VMEM_LIMIT),
        name="proj",
    )(x2, nw, w_ret, w_ng, w_ckv, w_skwk, wt_q, wt_v, wt_g, knw)


def _ret_kernel(qkvg_ref, cos_ref, sin_ref, dec_ref, zeta_ref, xi_ref, gch_ref, nw_ref, o_ref, state_ref):
    @pl.when(pl.program_id(1) == 0)
    def _():
        state_ref[...] = jnp.zeros_like(state_ref)

    lane = lax.broadcasted_iota(jnp.int32, (1, PAIR), 1)
    q_head = (lane >> (HEAD_SHIFT - 1)) & 1
    v_head = lane >> HEAD_SHIFT
    q_mask = [jnp.where(q_head == h, 1.0, 0.0).astype(BF16) for h in (0, 1)]
    v_mask = [jnp.where(v_head == h, 1.0, 0.0).astype(BF16) for h in (0, 1)]
    row_qh = (lax.broadcasted_iota(jnp.int32, (PAIR, PAIR), 0) >> (HEAD_SHIFT - 1)) & 1
    row_vh = lax.broadcasted_iota(jnp.int32, (PAIR, PAIR), 0) >> HEAD_SHIFT
    col_vh = lax.broadcasted_iota(jnp.int32, (PAIR, PAIR), 1) >> HEAD_SHIFT
    same_head_kv = jnp.where(row_qh == col_vh, 1.0, 0.0)
    head_mean = jnp.where(row_vh == col_vh, 1.0 / HEAD_DIM, 0.0).astype(BF16)

    pairs = range(RET_HEADS // 2)
    mean2 = jnp.concatenate([head_mean, head_mean], axis=0)
    tok0 = pl.program_id(1) * RET_TC
    for c in range(RET_TC // RET_CHUNK):
        sl = pl.ds(c * RET_CHUNK, RET_CHUNK)
        pos = pl.ds(pl.multiple_of(tok0 + c * RET_CHUNK, RET_CHUNK), RET_CHUNK)
        cos = cos_ref[pos, :]
        sin = sin_ref[pos, :]
        cols = [slice(p * PAIR, (p + 1) * PAIR) for p in pairs]
        qb, kb, vb, vzb = [], [], [], []
        for p in pairs:
            q = qkvg_ref[0, sl, pl.ds(p * PAIR, PAIR)]
            k = qkvg_ref[0, sl, pl.ds(D_RET + p * PAIR, PAIR)]
            v = qkvg_ref[0, sl, pl.ds(2 * D_RET + p * PAIR, PAIR)]
            qb.append((q * cos + pltpu.roll(q, HEAD_DIM, 1) * sin).astype(BF16))
            kb.append(((k * cos + pltpu.roll(k, HEAD_DIM, 1) * sin) * QK_SCALE).astype(BF16))
            vb.append(v.astype(BF16))
            vzb.append((v * zeta_ref[p]).astype(BF16))
        states = [state_ref[p] for p in pairs]
        kk = [jnp.concatenate([kb[p] * q_mask[h] for h in (0, 1)], axis=0) for p in pairs]
        s = [lax.dot_general(qb[p], kk[p], NT_DIMS, preferred_element_type=F32) for p in pairs]
        o_cross = [jnp.dot(qb[p], states[p].astype(BF16), preferred_element_type=F32) for p in pairs]
        kv = [lax.dot_general(kb[p], vzb[p], TN_DIMS, preferred_element_type=F32) for p in pairs]
        sb = [(s[p] * dec_ref[p]).astype(BF16) for p in pairs]
        vv = [jnp.concatenate([vb[p] * v_mask[h] for h in (0, 1)], axis=0) for p in pairs]
        o = [jnp.dot(sb[p], vv[p], preferred_element_type=F32) + o_cross[p] * xi_ref[p] for p in pairs]
        for p in pairs:
            state_ref[p] = states[p] * gch_ref[p] + kv[p] * same_head_kv
        o2 = [o[p] * o[p] for p in pairs]
        o2_hi = [o2[p].astype(BF16) for p in pairs]
        o2_hl = [jnp.concatenate([o2_hi[p], (o2[p] - o2_hi[p].astype(F32)).astype(BF16)], axis=1) for p in pairs]
        ms = [jnp.dot(o2_hl[p], mean2, preferred_element_type=F32) for p in pairs]
        for p in pairs:
            g = qkvg_ref[0, sl, pl.ds(3 * D_RET + p * PAIR, PAIR)]
            y = o[p] * lax.rsqrt(ms[p] + EPS) * nw_ref[p]
            o_ref[0, sl, cols[p]] = (y * (g * _sigmoid(g))).astype(BF16)


def _retention(ret3, cos, sin, dec, zeta, xi, gch, nw, B, T):
    npair = RET_HEADS // 2
    whole = lambda a: pl.BlockSpec(a.shape, lambda b, i: (0,) * a.ndim)
    return pl.pallas_call(
        _ret_kernel,
        grid=(B, T // RET_TC),
        in_specs=[pl.BlockSpec((1, RET_TC, 4 * D_RET), lambda b, i: (b, i, 0)), whole(cos), whole(sin), whole(dec),
                  whole(zeta), whole(xi), whole(gch), whole(nw)],
        out_specs=pl.BlockSpec((1, RET_TC, D_RET), lambda b, i: (b, i, 0)),
        out_shape=jax.ShapeDtypeStruct((B, T, D_RET), BF16),
        scratch_shapes=[pltpu.VMEM((npair, PAIR, PAIR), F32)],
        compiler_params=pltpu.CompilerParams(dimension_semantics=("parallel", "arbitrary"),
                                             vmem_limit_bytes=VMEM_LIMIT),
        name="retention",
    )(ret3, cos, sin, dec, zeta, xi, gch, nw)


def _cmp_kernel(x_ref, pos_ref, w1_ref, w2_ref, w2t_ref, knw_ref, o_ref, ot_ref):
    is_key = pl.program_id(1) == 0
    half = CMP_STRIDE * HEAD_DIM
    for g in range(NSA_KV_HEADS):
        x = x_ref[0, g]
        a = jnp.dot((x + pos_ref[0, 0:1, :]).astype(BF16), w1_ref[0, :half, :], preferred_element_type=F32)
        b = jnp.dot((x + pos_ref[0, 1:2, :]).astype(BF16), w1_ref[0, half:, :], preferred_element_type=F32)
        hid = a + pltpu.roll(b, b.shape[0] - 1, 0)
        hid = (hid * _sigmoid(hid)).astype(BF16)
        out = jnp.dot(hid, w2_ref[0], preferred_element_type=F32)
        ms = jnp.mean(out * out, axis=-1, keepdims=True)
        normed = out * lax.rsqrt(ms + EPS) * knw_ref[...]
        o_ref[0, 0, g] = jnp.where(is_key, normed, out)
        ot_ref[0, 0, g] = lax.dot_general(w2t_ref[0], hid, NT_DIMS, preferred_element_type=F32)


def _compress(xc, pos, w1, w2, w2t, knw, B, ncb):
    return pl.pallas_call(
        _cmp_kernel,
        grid=(B, 2),
        in_specs=[
            pl.BlockSpec((1, NSA_KV_HEADS, ncb, CMP_STRIDE * HEAD_DIM), lambda b, s: (b, s, 0, 0)),
            pl.BlockSpec((1, 2, CMP_STRIDE * HEAD_DIM), lambda b, s: (s, 0, 0)),
            pl.BlockSpec((1, CMP_BLOCK * HEAD_DIM, CMP_HIDDEN), lambda b, s: (s, 0, 0)),
            pl.BlockSpec((1, CMP_HIDDEN, HEAD_DIM), lambda b, s: (s, 0, 0)),
            pl.BlockSpec((1, HEAD_DIM, CMP_HIDDEN), lambda b, s: (s, 0, 0)),
            pl.BlockSpec((1, HEAD_DIM), lambda b, s: (0, 0)),
        ],
        out_specs=[
            pl.BlockSpec((1, 1, NSA_KV_HEADS, ncb, HEAD_DIM), lambda b, s: (b, s, 0, 0, 0)),
            pl.BlockSpec((1, 1, NSA_KV_HEADS, HEAD_DIM, ncb), lambda b, s: (b, s, 0, 0, 0)),
        ],
        out_shape=[
            jax.ShapeDtypeStruct((B, 2, NSA_KV_HEADS, ncb, HEAD_DIM), F32),
            jax.ShapeDtypeStruct((B, 2, NSA_KV_HEADS, HEAD_DIM, ncb), F32),
        ],
        compiler_params=pltpu.CompilerParams(dimension_semantics=("parallel", "parallel"),
                                             vmem_limit_bytes=VMEM_LIMIT),
        name="compress",
    )(xc, pos, w1, w2, w2t, knw)


def _nsa_kernel(qt_ref, qtn_ref, glt_ref, ng_ref, ks_ref, kw_ref, vst_ref, vwt_ref, kc_ref, vct_ref, mt_ref, qnw_ref,
                bg_ref, o_ref, qp_ref, oc_ref, sa_ref, sb_ref, mxa_ref, mxb_ref, sw_ref, m_ref, acc_ref):
    qi = pl.program_id(1)
    t0 = qi * (NSA_QT * NSA_TQ)
    tiles = range(NSA_QT)
    chains = [(qt, g) for qt in tiles for g in range(NSA_KV_HEADS)]
    gq = NSA_GROUP * HEAD_DIM
    gg = 16
    ncb = kc_ref.shape[3]
    n_slc = mt_ref.shape[0]

    def select_masks(base_t0):
        n_idx = lax.broadcasted_iota(jnp.int32, (ncb, NSA_TQ), 0)
        jb = lax.broadcasted_iota(jnp.int32, (n_slc, NSA_TQ), 0)
        out = []
        for qt in tiles:
            tok = base_t0 + qt * NSA_TQ
            tok_c = tok + lax.broadcasted_iota(jnp.int32, (ncb, NSA_TQ), 1)
            cbias = _tile4(jnp.where((n_idx * CMP_STRIDE + (CMP_BLOCK - 1)) <= tok_c, 0.0, NEG))
            tok_row = tok + lax.broadcasted_iota(jnp.int32, (1, NSA_TQ), 1)
            has_block = _tile4(jnp.where(tok_row >= CMP_BLOCK - 1, 1.0, 0.0))
            tok_s = tok + lax.broadcasted_iota(jnp.int32, (n_slc, NSA_TQ), 1)
            valid_s = jb * SLC_BLOCK <= tok_s
            force = (jb == (tok_s >> SLC_SHIFT)) | (jb == 0)
            out.append((cbias, has_block, valid_s, force))
        return out

    def select_scores(src_ref, slot, masks, c, rows=ncb):
        qt, g = chains[c]
        cols = []
        for r in range(NSA_GROUP):
            q = src_ref[0, qt, g * gq + r * HEAD_DIM:g * gq + (r + 1) * HEAD_DIM, :]
            ms = jnp.mean(q * q, axis=0, keepdims=True)
            cols.append(q * lax.rsqrt(ms + EPS) * qnw_ref[...] * (QK_SCALE * LOG2E))
        qs = jnp.concatenate(cols, axis=1).astype(BF16)
        qp_ref[slot, c, 0:HEAD_DIM, :] = qs
        return (jnp.dot(kc_ref[0, 0, g, 0:rows, :].astype(BF16), qs, preferred_element_type=F32)
                + masks[qt][0][0:rows, :])

    def select_probs(sc, slot, masks, c, want_scores=True):
        qt, g = chains[c]
        _, has_block, valid_s, force = masks[qt]
        rows = sc.shape[0]
        mc = jnp.max(sc, axis=0, keepdims=True)
        ec = jnp.exp2(sc - mc)
        lc = jnp.sum(ec, axis=0, keepdims=True)
        p = ec * (has_block / lc)
        oc_ref[slot, c] = jnp.dot(vct_ref[0, 0, g, :, 0:rows].astype(BF16), p.astype(BF16),
                                  preferred_element_type=F32)
        if not want_scores:
            return None
        ps = p[:, 0:NSA_TQ]
        for r in range(1, NSA_GROUP):
            ps = ps + p[:, r * NSA_TQ:(r + 1) * NSA_TQ]
        ps_hi = ps.astype(BF16)
        ps_lo = (ps - ps_hi.astype(F32)).astype(BF16)
        imp = (jnp.dot(mt_ref[:, 0:rows], ps_hi, preferred_element_type=F32)
               + jnp.dot(mt_ref[:, 0:rows], ps_lo, preferred_element_type=F32))
        return jnp.where(valid_s, jnp.where(force, imp + FORCE_BONUS, imp), NEG)

    def select_rank(scores, valid_s, slot, n_live):
        if n_live <= SLC_TOPK:
            for c, (qt, g) in enumerate(chains):
                qp_ref[slot, c, HEAD_DIM:2 * HEAD_DIM, :] = _tile4(jnp.where(valid_s[qt], 0.0, NEG).astype(BF16))
            return
        sub = lax.broadcasted_iota(jnp.int32, (SUBLANES, NSA_TQ), 0)
        for c, (qt, g) in enumerate(chains):
            score = scores[c]
            blocks = [score[v * SUBLANES:(v + 1) * SUBLANES, :] for v in range(n_live // SUBLANES)]
            ranks = [jnp.zeros((SUBLANES, NSA_TQ), F32) for _ in blocks]
            for i in range(n_live):
                row = score[i:i + 1, :]
                for v, blk in enumerate(blocks):
                    if v * SUBLANES > i:
                        beats = row >= blk
                    elif (v + 1) * SUBLANES <= i:
                        beats = row > blk
                    else:
                        beats = (row > blk) | ((row >= blk) & (sub > i - v * SUBLANES))
                    ranks[v] = ranks[v] + jnp.where(beats, 1.0, 0.0)
            dead = [jnp.full((SUBLANES, NSA_TQ), float(n_slc), F32)] * ((n_slc - n_live) // SUBLANES)
            rank = jnp.concatenate(ranks + dead, axis=0)
            sel = (rank < float(SLC_TOPK)) & valid_s[qt]
            qp_ref[slot, c, HEAD_DIM:2 * HEAD_DIM, :] = _tile4(jnp.where(sel, 0.0, NEG).astype(BF16))

    step_tokens = NSA_QT * NSA_TQ

    def live_blocks(step):
        return ((step + 1) * step_tokens - 1) // SLC_BLOCK + 1

    all_chains = list(range(len(chains)))

    @pl.when(qi == 0)
    def _():
        masks0 = select_masks(0)
        rows0 = min(ncb, -(-((step_tokens - CMP_BLOCK) // CMP_STRIDE + 1) // 16) * 16)
        assert live_blocks(0) <= SLC_TOPK
        for c in all_chains:
            select_probs(select_scores(qt_ref, 0, masks0, c, rows0), 0, masks0, c, want_scores=False)
        select_rank(None, [m[2] for m in masks0], 0, live_blocks(0))

    cur = qi % 2
    nxt = 1 - cur
    t_next = t0 + NSA_QT * NSA_TQ

    kt0, ks0, wbias = [], [], []
    c_minus_r = (lax.broadcasted_iota(jnp.int32, (WIN_KEYS, NSA_TQ), 1)
                 - lax.broadcasted_iota(jnp.int32, (WIN_KEYS, NSA_TQ), 0))
    for qt in tiles:
        kt0.append(jnp.maximum(qi * NSA_QT + qt - WIN_SIZE // NSA_TQ, 0))
        ks0.append(pl.multiple_of(kt0[qt] * NSA_TQ, NSA_TQ))
        delta = (t0 + qt * NSA_TQ - ks0[qt]) + c_minus_r
        in_window = lax.bitcast_convert_type(delta, jnp.uint32) < WIN_SIZE
        wbias.append(_tile4(jnp.where(in_window, 0.0, NEG)))

    def win_scores(c):
        qt, g = chains[c]
        sw = jnp.dot(kw_ref[0, g, pl.ds(ks0[qt], WIN_KEYS), 0:HEAD_DIM], qp_ref[cur, c, 0:HEAD_DIM, :],
                     preferred_element_type=F32) + wbias[qt]
        sw_ref[c] = sw
        return jnp.max(sw, axis=0, keepdims=True)

    def win_attend(c, mw):
        qt, g = chains[c]
        ewb = jnp.exp2(sw_ref[c] - mw).astype(BF16)
        vwt = jnp.concatenate([vwt_ref[0, kt0[qt] + j, g * V_ROWS:(g + 1) * V_ROWS, :]
                               for j in range(WIN_KEYS // LANES)], axis=1)
        ow_aug = jnp.dot(vwt, ewb, preferred_element_type=F32)
        return ow_aug[0:HEAD_DIM, :] * (1.0 / ow_aug[HEAD_DIM:HEAD_DIM + 1, :])

    vt_per_tile = SLC_TK // LANES

    def slc_scores(j, dst_ref, mx_ref, which=all_chains):
        kst = pl.multiple_of(j * SLC_TK, SLC_TK)
        for c in which:
            qt, g = chains[c]
            s = jnp.dot(ks_ref[0, g, pl.ds(kst, SLC_TK), :], qp_ref[cur, c], preferred_element_type=F32)
            dst_ref[c] = s
            mx_ref[c] = jnp.max(s, axis=0, keepdims=True)

    def slc_update(j, src_ref, mx_ref, causal, which=all_chains, visible=SLC_TK):
        if causal:
            band = _tile4(jnp.where(lax.broadcasted_iota(jnp.int32, (NSA_TQ, NSA_TQ), 0)
                                    <= lax.broadcasted_iota(jnp.int32, (NSA_TQ, NSA_TQ), 1), 0.0, NEG))
        for c in which:
            qt, g = chains[c]
            if causal:
                r0 = visible - step_tokens + qt * NSA_TQ
                rows = r0 + NSA_TQ
                s_band = src_ref[c, r0:rows, :] + band
                s = jnp.concatenate([src_ref[c, 0:r0, :], s_band], axis=0) if r0 > 0 else s_band
                tile_max = jnp.max(s, axis=0, keepdims=True)
            else:
                rows = SLC_TK
                s = src_ref[c]
                tile_max = mx_ref[c]
            m_old = m_ref[c]
            m_new = jnp.maximum(m_old, tile_max)
            alpha = jnp.exp2(m_old - m_new)
            eb = jnp.exp2(s - m_new).astype(BF16)
            vt = jnp.concatenate([vst_ref[0, j * vt_per_tile + jj, g * V_ROWS:(g + 1) * V_ROWS, :]
                                  for jj in range(rows // LANES)], axis=1)
            acc_ref[c] = alpha * acc_ref[c] + jnp.dot(vt, eb, preferred_element_type=F32)
            m_ref[c] = m_new

    masks = select_masks(t_next)

    def matmul_stage(c):
        mw = win_scores(c)
        sc = select_scores(qtn_ref, nxt, masks, c)
        slc_scores(0, sa_ref, mxa_ref, [c])
        return sc, mw

    sel_scores, ow_t = [], []
    staged = matmul_stage(0)
    for c in all_chains:
        staged_next = matmul_stage(c + 1) if c + 1 < len(chains) else None
        sel_scores.append(select_probs(staged[0], nxt, masks, c))
        ow_t.append(win_attend(c, staged[1]))
        staged = staged_next
    sel_valid = [m[2] for m in masks]

    need = live_blocks(qi + 1)
    bounds = list(range(SLC_TOPK, n_slc + 1, SUBLANES))
    for lo, hi in zip([0] + bounds[:-1], bounds):
        in_range = (need > lo) if hi == bounds[-1] else ((need > lo) & (need <= hi))
        pl.when(in_range)(functools.partial(select_rank, sel_scores, sel_valid, nxt, hi))

    m_ref[...] = jnp.full(m_ref.shape, NEG, F32)
    acc_ref[...] = jnp.zeros(acc_ref.shape, F32)
    n_full = t0 // SLC_TK

    def pair(jj, carry):
        j = 2 * jj
        for c in all_chains:
            slc_scores(j + 1, sb_ref, mxb_ref, [c])
            slc_update(j, sa_ref, mxa_ref, False, [c])
        for c in all_chains:
            slc_scores(j + 2, sa_ref, mxa_ref, [c])
            slc_update(j + 1, sb_ref, mxb_ref, False, [c])
        return carry

    lax.fori_loop(0, n_full // 2, pair, 0)

    visible = t0 - n_full * SLC_TK + step_tokens

    def tail(odd, rows):
        if odd:
            for c in all_chains:
                slc_scores(n_full, sb_ref, mxb_ref, [c])
                slc_update(n_full - 1, sa_ref, mxa_ref, False, [c])
            slc_update(n_full, sb_ref, mxb_ref, True, visible=rows)
        else:
            slc_update(n_full, sa_ref, mxa_ref, True, visible=rows)

    for rows in range(step_tokens, SLC_TK + 1, step_tokens):
        for odd in (False, True):
            parity = (n_full % 2 == 1) if odd else (n_full % 2 == 0)
            pl.when((visible == rows) & parity)(functools.partial(tail, odd, rows))

    for qt in tiles:
        gates = _sigmoid(glt_ref[0, qt] + bg_ref[...])
        outs = []
        for g in range(NSA_KV_HEADS):
            c = qt * NSA_KV_HEADS + g
            os_t = acc_ref[c, 0:HEAD_DIM, :] * (1.0 / acc_ref[c, HEAD_DIM:HEAD_DIM + 1, :])
            oc_t = oc_ref[cur, c]
            for r in range(NSA_GROUP):
                sl = slice(r * NSA_TQ, (r + 1) * NSA_TQ)
                row = g * gg + r
                outs.append(gates[row:row + 1, :] * oc_t[:, sl]
                            + gates[row + NSA_GROUP:row + NSA_GROUP + 1, :] * os_t[:, sl]
                            + gates[row + 2 * NSA_GROUP:row + 2 * NSA_GROUP + 1, :] * ow_t[c][:, sl])
        o_tok = jnp.concatenate(outs, axis=0).T
        rows = pl.ds(qt * NSA_TQ, NSA_TQ)
        ng = ng_ref[0, rows, :]
        o_ref[0, rows, :] = (o_tok * (ng * _sigmoid(ng))).astype(BF16)


def _nsa(nqt, glt, ng3, ks, kw, vt, kc, vct, mt, qnw, bg, B, T):
    nt = T // LANES
    ncb = kc.shape[3]
    G = NSA_KV_HEADS
    NC = NSA_QT * G
    ncols = NSA_GROUP * NSA_TQ
    steps = T // (NSA_QT * NSA_TQ)
    return pl.pallas_call(
        _nsa_kernel,
        grid=(B, steps),
        in_specs=[
            pl.BlockSpec((1, NSA_QT, D_NSA, LANES), lambda b, i: (b, 0, 0, 0)),
            pl.BlockSpec((1, NSA_QT, D_NSA, LANES), lambda b, i: (b, jnp.minimum(i + 1, steps - 1), 0, 0)),
            pl.BlockSpec((1, NSA_QT, 16 * G, LANES), lambda b, i: (b, i, 0, 0)),
            pl.BlockSpec((1, NSA_QT * NSA_TQ, D_NSA), lambda b, i: (b, i, 0)),
            pl.BlockSpec((1, G, T, PAIR), lambda b, i: (b, 0, 0, 0)),
            pl.BlockSpec((1, G, T, PAIR), lambda b, i: (b, 0, 0, 0)),
            pl.BlockSpec((1, nt, G * V_ROWS, LANES), lambda b, i: (b, 0, 0, 0)),
            pl.BlockSpec((1, nt, G * V_ROWS, LANES), lambda b, i: (b, 0, 1, 0)),
            pl.BlockSpec((1, 1, G, ncb, HEAD_DIM), lambda b, i: (b, 0, 0, 0, 0)),
            pl.BlockSpec((1, 1, G, HEAD_DIM, ncb), lambda b, i: (b, 1, 0, 0, 0)),
            pl.BlockSpec(mt.shape, lambda b, i: (0, 0)),
            pl.BlockSpec((HEAD_DIM, 1), lambda b, i: (0, 0)),
            pl.BlockSpec((16 * G, 1), lambda b, i: (0, 0)),
        ],
        out_specs=pl.BlockSpec((1, NSA_QT * NSA_TQ, D_NSA), lambda b, i: (b, i, 0)),
        out_shape=jax.ShapeDtypeStruct((B, T, D_NSA), BF16),
        scratch_shapes=[
            pltpu.VMEM((2, NC, 2 * HEAD_DIM, ncols), BF16),
            pltpu.VMEM((2, NC, HEAD_DIM, ncols), F32),
            pltpu.VMEM((NC, SLC_TK, ncols), F32),
            pltpu.VMEM((NC, SLC_TK, ncols), F32),
            pltpu.VMEM((NC, 1, ncols), F32),
            pltpu.VMEM((NC, 1, ncols), F32),
            pltpu.VMEM((NC, WIN_KEYS, ncols), F32),
            pltpu.VMEM((NC, 1, ncols), F32),
            pltpu.VMEM((NC, V_ROWS, ncols), F32),
        ],
        compiler_params=pltpu.CompilerParams(dimension_semantics=("parallel", "arbitrary"),
                                             vmem_limit_bytes=VMEM_LIMIT),
        name="nsa",
    )(nqt, nqt, glt, ng3, ks, kw, vt, vt, kc, vct, mt, qnw, bg)


def _out_kernel(x_ref, yr_ref, yn_ref, wr_ref, wn_ref, o_ref):
    o_ref[...] = (x_ref[...]
                  + jnp.dot(yr_ref[...], wr_ref[...], preferred_element_type=F32)
                  + jnp.dot(yn_ref[...], wn_ref[...], preferred_element_type=F32))


def _outproj(x2, yr, yn, wr, wn):
    N = x2.shape[0]
    row = lambda i: (i, 0)
    const = lambda i: (0, 0)
    return pl.pallas_call(
        _out_kernel,
        grid=(N // OUT_TM,),
        in_specs=[pl.BlockSpec((OUT_TM, D_MODEL), row), pl.BlockSpec((OUT_TM, D_RET), row),
                  pl.BlockSpec((OUT_TM, D_NSA), row), pl.BlockSpec(wr.shape, const), pl.BlockSpec(wn.shape, const)],
        out_specs=pl.BlockSpec((OUT_TM, D_MODEL), row),
        out_shape=jax.ShapeDtypeStruct((N, D_MODEL), F32),
        compiler_params=pltpu.CompilerParams(dimension_semantics=("parallel",), vmem_limit_bytes=VMEM_LIMIT),
        name="outproj",
    )(x2, yr, yn, wr, wn)


@functools.lru_cache(maxsize=None)
def _tables(T):
    half = HEAD_DIM // 2
    inv = ROPE_THETA ** (-np.arange(half, dtype=np.float64) / half)
    ang = np.arange(T, dtype=np.float64)[:, None] * inv[None, :]
    cos = np.concatenate([np.cos(ang)] * 4, axis=1).astype(np.float32)
    sin = np.concatenate([-np.sin(ang), -np.sin(ang), np.sin(ang), np.sin(ang)], axis=1).astype(np.float32)

    C = RET_CHUNK
    log_g = np.log1p(-np.exp2(-5.0 - np.arange(RET_HEADS, dtype=np.float64)))
    pos = np.arange(C, dtype=np.float64)
    diff = pos[:, None] - pos[None, :]
    decay = np.where(diff >= 0, np.exp(log_g[:, None, None] * np.maximum(diff, 0.0)), 0.0)
    zeta = np.exp(log_g[:, None] * (C - 1.0 - pos))
    xi = np.exp(log_g[:, None] * (pos + 1.0))
    g_chunk = np.exp(log_g * C)
    npair = RET_HEADS // 2

    def pair_lanes(a):
        return np.repeat(a.reshape(npair, 2, C).transpose(0, 2, 1), HEAD_DIM, axis=2).astype(np.float32)

    dec = decay.reshape(npair, 2, C, C).transpose(0, 2, 1, 3).reshape(npair, C, 2 * C).astype(np.float32)
    gch = np.repeat(g_chunk.reshape(npair, 1, 2), HEAD_DIM, axis=2).astype(np.float32)

    n_cmp = (T - CMP_BLOCK) // CMP_STRIDE + 1
    ncb = T // CMP_STRIDE
    p = np.arange(n_cmp)[:, None] * CMP_STRIDE + np.arange(CMP_BLOCK)[None, :]
    blk = p // SLC_BLOCK
    M = (blk[:, :, None] == np.arange(T // SLC_BLOCK)[None, None, :]).mean(axis=1)
    mt = np.zeros((T // SLC_BLOCK, ncb), np.float32)
    mt[:, :n_cmp] = M.T
    return cos, sin, dec, pair_lanes(zeta), pair_lanes(xi), gch, mt


def kernel(x, norm_w, w_in, ret_norm_w, q_norm_w, k_norm_cmp, k_norm_slc, k_norm_win, cmp_pos_k, cmp_w1_k, cmp_w2_k,
           cmp_pos_v, cmp_w1_v, cmp_w2_v, b_gate, w_out):
    B, T, D = x.shape
    depth = norm_w.shape[0]
    cos, sin, dec, zeta, xi, gch, mt = _tables(T)
    ncb = T // CMP_STRIDE
    half = CMP_STRIDE * HEAD_DIM
    gate_src = np.zeros((NSA_KV_HEADS, 16), np.int32)
    gate_ok = np.zeros((NSA_KV_HEADS, 16), bool)
    for g in range(NSA_KV_HEADS):
        for br in range(N_BRANCH):
            for r in range(NSA_GROUP):
                gate_src[g, br * NSA_GROUP + r] = br * NSA_HEADS + g * NSA_GROUP + r
                gate_ok[g, br * NSA_GROUP + r] = True
    gate_src = gate_src.reshape(-1)
    gate_ok = gate_ok.reshape(-1)

    x2 = x.reshape(B * T, D)
    for layer in range(depth):
        w = w_in[layer].astype(BF16)
        o_ng = 4 * D_RET + D_NSA
        o_kv = o_ng + D_NSA
        quarter = HEAD_DIM // 2
        pair_perm = np.concatenate([np.arange(quarter), HEAD_DIM + np.arange(quarter),
                                    quarter + np.arange(quarter), HEAD_DIM + quarter + np.arange(quarter)])
        qk_perm = np.concatenate([p * PAIR + pair_perm for p in range(RET_HEADS // 2)])
        w_ret = jnp.concatenate([w[:, :D_RET][:, qk_perm], w[:, D_RET:2 * D_RET][:, qk_perm],
                                 w[:, 2 * D_RET:4 * D_RET]], axis=1)
        wt_q = w[:, 4 * D_RET:o_ng].T
        w_ng = w[:, o_ng:o_kv]
        w_ckv = w[:, o_kv:o_kv + 2 * D_KV]
        w_skwk = jnp.concatenate([w[:, o_kv + 2 * D_KV:o_kv + 3 * D_KV], w[:, o_kv + 4 * D_KV:o_kv + 5 * D_KV]], axis=1)
        wt_v = jnp.concatenate([w[:, o_kv + 3 * D_KV:o_kv + 4 * D_KV], w[:, o_kv + 5 * D_KV:o_kv + 6 * D_KV]], axis=1).T
        w_gl = w[:, o_kv + 6 * D_KV:]
        wt_g = jnp.where(gate_ok[:, None], w_gl.T[gate_src], jnp.zeros((), BF16))
        bg = jnp.where(gate_ok, b_gate[layer][gate_src], 0.0).reshape(-1, 1)

        knw = jnp.stack([jnp.tile(k_norm_slc[layer], 2), jnp.tile(k_norm_win[layer], 2)])
        ret, ng, xc, ks, kw, nqt, vt, glt = _proj(x2, norm_w[layer].reshape(1, D), w_ret, w_ng, w_ckv, w_skwk,
                                                  wt_q, wt_v, wt_g, knw, B, T)

        nw_pair = ret_norm_w[layer].reshape(RET_HEADS // 2, 1, PAIR)
        y_ret = _retention(ret.reshape(B, T, 4 * D_RET), jnp.asarray(cos), jnp.asarray(sin), jnp.asarray(dec),
                           jnp.asarray(zeta), jnp.asarray(xi), jnp.asarray(gch), nw_pair, B, T)

        pos =jnp.stack([cmp_pos_k[layer], cmp_pos_v[layer]]).reshape(2, 2, half)
        w1 = jnp.stack([cmp_w1_k[layer], cmp_w1_v[layer]]).astype(BF16)
        w2 = jnp.stack([cmp_w2_k[layer], cmp_w2_v[layer]]).astype(BF16)
        w2t = jnp.swapaxes(w2, 1, 2)
        kc, vct = _compress(xc, pos, w1, w2, w2t, k_norm_cmp[layer].reshape(1, HEAD_DIM), B, ncb)

        y_nsa = _nsa(nqt, glt, ng.reshape(B, T, D_NSA), ks, kw, vt, kc, vct, jnp.asarray(mt).astype(BF16),
                     q_norm_w[layer].reshape(HEAD_DIM, 1), bg, B, T)

        wo = w_out[layer].astype(BF16)
        x2 = _outproj(x2, y_ret.reshape(B * T, D_RET), y_nsa.reshape(B * T, D_NSA), wo[:D_RET], wo[D_RET:])
    return x2.reshape(B, T, D)
```

```python
import functools

import numpy as np
import jax
import jax.numpy as jnp
from jax import lax
from jax.experimental import pallas as pl
from jax.experimental.pallas import tpu as pltpu

F32 = jnp.float32
BF16 = jnp.bfloat16

D_MODEL = 1024
HEAD_DIM = 64
HEAD_SHIFT = 6
RET_HEADS = 8
NSA_HEADS = 8
NSA_KV_HEADS = 2
NSA_GROUP = NSA_HEADS // NSA_KV_HEADS
D_RET = RET_HEADS * HEAD_DIM
D_NSA = NSA_HEADS * HEAD_DIM
D_KV = NSA_KV_HEADS * HEAD_DIM
N_BRANCH = 3
RET_CHUNK = 128
ROPE_THETA = 10000.0
CMP_BLOCK = 32
CMP_STRIDE = 16
CMP_HIDDEN = 256
SLC_BLOCK = 64
SLC_SHIFT = 6
SLC_TOPK = 16
WIN_SIZE = 512
EPS = 1e-6
NEG = -1e30
FORCE_BONUS = 1e4
QK_SCALE = HEAD_DIM ** -0.5
LOG2E = 1.4426950408889634
V_ROWS = HEAD_DIM + 16

LANES = 128
SUBLANES = 8
PAIR = 2 * HEAD_DIM
VMEM_LIMIT = 48 * 1024 * 1024

PROJ_TM = 512
RET_TC = 1024
NSA_TQ = 128
NSA_QT = 2
SLC_TK = 512
WIN_KEYS = WIN_SIZE + NSA_TQ
OUT_TM = 1024

NT_DIMS = (((1,), (1,)), ((), ()))
TN_DIMS = (((0,), (0,)), ((), ()))


def _sigmoid(x):
    return 1.0 / (1.0 + jnp.exp(-x))


def _tile4(a):
    return jnp.concatenate([a, a, a, a], axis=1)


def _proj_kernel(steps_per_batch, x_ref, nw_ref, w_ret_ref, w_ng_ref, w_ckv_ref, w_skwk_ref, wt_q_ref, wt_v_ref,
                 wt_g_ref, knw_ref, ret_ref, ng_ref, xc_ref, ks_ref, kw_ref, nqt_ref, vt_ref, glt_ref, ckv_scr):
    x = x_ref[...]
    ms = jnp.mean(x * x, axis=-1, keepdims=True)
    h = (x * lax.rsqrt(ms + EPS) * nw_ref[...]).astype(BF16)
    ret_ref[...] = jnp.dot(h, w_ret_ref[...], preferred_element_type=F32)
    ng_ref[...] = jnp.dot(h, w_ng_ref[...], preferred_element_type=F32)

    ckv = jnp.dot(h, w_ckv_ref[...], preferred_element_type=F32)
    for half in range(2 * D_KV // LANES):
        ckv_scr[half] = ckv[:, half * LANES:(half + 1) * LANES]
    for l in range(CMP_STRIDE):
        for half in range(2 * D_KV // LANES):
            rows = ckv_scr[half, pl.ds(l, PROJ_TM // CMP_STRIDE, stride=CMP_STRIDE), :]
            for s in range(LANES // HEAD_DIM):
                xc_ref[0, half * (LANES // HEAD_DIM) + s, :, l * HEAD_DIM:(l + 1) * HEAD_DIM] = (
                    rows[:, s * HEAD_DIM:(s + 1) * HEAD_DIM])

    skwk = jnp.dot(h, w_skwk_ref[...], preferred_element_type=F32)
    lane = lax.broadcasted_iota(jnp.int32, (1, PAIR), 1)
    head0 = lane < HEAD_DIM
    m0 = jnp.where(head0, 1.0, 0.0)
    m1 = 1.0 - m0
    t_start = (pl.program_id(0) % steps_per_batch) * PROJ_TM
    tok = t_start + lax.broadcasted_iota(jnp.int32, (PROJ_TM, PAIR), 0)
    col = lax.broadcasted_iota(jnp.int32, (PROJ_TM, PAIR), 1)
    indicator = jnp.where((tok >> SLC_SHIFT) == col - HEAD_DIM, 1.0, 0.0)

    def pair_normed(t, w):
        t2 = t * t
        ms0 = jnp.sum(t2 * m0, axis=-1, keepdims=True)
        ms1 = jnp.sum(t2 * m1, axis=-1, keepdims=True)
        return t * lax.rsqrt(jnp.where(head0, ms0, ms1) * (1.0 / HEAD_DIM) + EPS) * w

    ns = pair_normed(skwk[:, :PAIR], knw_ref[0:1, :])
    nwin = pair_normed(skwk[:, PAIR:], knw_ref[1:2, :])
    for g in range(NSA_KV_HEADS):
        s_g = ns if g == 0 else pltpu.roll(ns, HEAD_DIM, 1)
        w_g = nwin if g == 0 else pltpu.roll(nwin, HEAD_DIM, 1)
        ks_ref[0, g] = jnp.where(head0, s_g, indicator).astype(BF16)
        kw_ref[0, g] = jnp.where(head0, w_g, 0.0).astype(BF16)

    qt = lax.dot_general(wt_q_ref[...], h, NT_DIMS, preferred_element_type=F32)
    vt = lax.dot_general(wt_v_ref[...], h, NT_DIMS, preferred_element_type=F32)
    gt = lax.dot_general(wt_g_ref[...], h, NT_DIMS, preferred_element_type=F32)
    pad_row = lax.broadcasted_iota(jnp.int32, (V_ROWS - HEAD_DIM, LANES), 0)
    ones_pad = jnp.where(pad_row == 0, 1.0, 0.0).astype(BF16)
    for j in range(PROJ_TM // LANES):
        sl = slice(j * LANES, (j + 1) * LANES)
        nqt_ref[0, j] = qt[:, sl]
        for blk in range(2 * NSA_KV_HEADS):
            vt_ref[0, j, blk * V_ROWS:blk * V_ROWS + HEAD_DIM, :] = (
                vt[blk * HEAD_DIM:(blk + 1) * HEAD_DIM, sl].astype(BF16))
            vt_ref[0, j, blk * V_ROWS + HEAD_DIM:(blk + 1) * V_ROWS, :] = ones_pad
        glt_ref[0, j] = gt[:, sl]


def _proj(x2, nw, w_ret, w_ng, w_ckv, w_skwk, wt_q, wt_v, wt_g, knw, B, T):
    N = B * T
    tpb = T // PROJ_TM
    sub = PROJ_TM // LANES
    nt = T // LANES
    const = lambda i: (0, 0)
    row = lambda i: (i, 0)
    trn = lambda i: (i // tpb, i % tpb, 0, 0)
    tokm = lambda i: (i // tpb, 0, i % tpb, 0)
    return pl.pallas_call(
        functools.partial(_proj_kernel, tpb),
        grid=(N // PROJ_TM,),
        in_specs=[
            pl.BlockSpec((PROJ_TM, D_MODEL), row),
            pl.BlockSpec((1, D_MODEL), const),
            pl.BlockSpec(w_ret.shape, const),
            pl.BlockSpec(w_ng.shape, const),
            pl.BlockSpec(w_ckv.shape, const),
            pl.BlockSpec(w_skwk.shape, const),
            pl.BlockSpec(wt_q.shape, const),
            pl.BlockSpec(wt_v.shape, const),
            pl.BlockSpec(wt_g.shape, const),
            pl.BlockSpec(knw.shape, const),
        ],
        out_specs=[
            pl.BlockSpec((PROJ_TM, 4 * D_RET), row),
            pl.BlockSpec((PROJ_TM, D_NSA), row),
            pl.BlockSpec((1, 2 * NSA_KV_HEADS, PROJ_TM // CMP_STRIDE, CMP_STRIDE * HEAD_DIM), tokm),
            pl.BlockSpec((1, NSA_KV_HEADS, PROJ_TM, PAIR), tokm),
            pl.BlockSpec((1, NSA_KV_HEADS, PROJ_TM, PAIR), tokm),
            pl.BlockSpec((1, sub, D_NSA, LANES), trn),
            pl.BlockSpec((1, sub, 2 * NSA_KV_HEADS * V_ROWS, LANES), trn),
            pl.BlockSpec((1, sub, 32, LANES), trn),
        ],
        out_shape=[
            jax.ShapeDtypeStruct((N, 4 * D_RET), F32),
            jax.ShapeDtypeStruct((N, D_NSA), F32),
            jax.ShapeDtypeStruct((B, 2 * NSA_KV_HEADS, T // CMP_STRIDE, CMP_STRIDE * HEAD_DIM), F32),
            jax.ShapeDtypeStruct((B, NSA_KV_HEADS, T, PAIR), BF16),
            jax.ShapeDtypeStruct((B, NSA_KV_HEADS, T, PAIR), BF16),
            jax.ShapeDtypeStruct((B, nt, D_NSA, LANES), F32),
            jax.ShapeDtypeStruct((B, nt, 2 * NSA_KV_HEADS * V_ROWS, LANES), BF16),
            jax.ShapeDtypeStruct((B, nt, 32, LANES), F32),
        ],
        scratch_shapes=[pltpu.VMEM((2 * D_KV // LANES, PROJ_TM, LANES), F32)],
        compiler_params=pltpu.CompilerParams(dimension_semantics=("parallel",), vmem_limit_bytes=VMEM_LIMIT),
        name="proj",
    )(x2, nw, w_ret, w_ng, w_ckv, w_skwk, wt_q, wt_v, wt_g, knw)


def _ret_kernel(qkvg_ref, cos_ref, sin_ref, dec_ref, zeta_ref, xi_ref, gch_ref, nw_ref, o_ref, state_ref):
    @pl.when(pl.program_id(1) == 0)
    def _():
        state_ref[...] = jnp.zeros_like(state_ref)

    lane = lax.broadcasted_iota(jnp.int32, (1, PAIR), 1)
    q_head = (lane >> (HEAD_SHIFT - 1)) & 1
    v_head = lane >> HEAD_SHIFT
    q_mask = [jnp.where(q_head == h, 1.0, 0.0).astype(BF16) for h in (0, 1)]
    v_mask = [jnp.where(v_head == h, 1.0, 0.0).astype(BF16) for h in (0, 1)]
    row_qh = (lax.broadcasted_iota(jnp.int32, (PAIR, PAIR), 0) >> (HEAD_SHIFT - 1)) & 1
    row_vh = lax.broadcasted_iota(jnp.int32, (PAIR, PAIR), 0) >> HEAD_SHIFT
    col_vh = lax.broadcasted_iota(jnp.int32, (PAIR, PAIR), 1) >> HEAD_SHIFT
    same_head_kv = jnp.where(row_qh == col_vh, 1.0, 0.0)
    head_mean = jnp.where(row_vh == col_vh, 1.0 / HEAD_DIM, 0.0).astype(BF16)

    pairs = range(RET_HEADS // 2)
    mean2 = jnp.concatenate([head_mean, head_mean], axis=0)
    tok0 = pl.program_id(1) * RET_TC
    for c in range(RET_TC // RET_CHUNK):
        sl = pl.ds(c * RET_CHUNK, RET_CHUNK)
        pos = pl.ds(pl.multiple_of(tok0 + c * RET_CHUNK, RET_CHUNK), RET_CHUNK)
        cos = cos_ref[pos, :]
        sin = sin_ref[pos, :]
        cols = [slice(p * PAIR, (p + 1) * PAIR) for p in pairs]
        qb, kb, vb, vzb = [], [], [], []
        for p in pairs:
            q = qkvg_ref[0, sl, pl.ds(p * PAIR, PAIR)]
            k = qkvg_ref[0, sl, pl.ds(D_RET + p * PAIR, PAIR)]
            v = qkvg_ref[0, sl, pl.ds(2 * D_RET + p * PAIR, PAIR)]
            qb.append((q * cos + pltpu.roll(q, HEAD_DIM, 1) * sin).astype(BF16))
            kb.append(((k * cos + pltpu.roll(k, HEAD_DIM, 1) * sin) * QK_SCALE).astype(BF16))
            vb.append(v.astype(BF16))
            vzb.append((v * zeta_ref[p]).astype(BF16))
        states = [state_ref[p] for p in pairs]
        kk = [jnp.concatenate([kb[p] * q_mask[h] for h in (0, 1)], axis=0) for p in pairs]
        s = [lax.dot_general(qb[p], kk[p], NT_DIMS, preferred_element_type=F32) for p in pairs]
        o_cross = [jnp.dot(qb[p], states[p].astype(BF16), preferred_element_type=F32) for p in pairs]
        kv = [lax.dot_general(kb[p], vzb[p], TN_DIMS, preferred_element_type=F32) for p in pairs]
        sb = [(s[p] * dec_ref[p]).astype(BF16) for p in pairs]
        vv = [jnp.concatenate([vb[p] * v_mask[h] for h in (0, 1)], axis=0) for p in pairs]
        o = [jnp.dot(sb[p], vv[p], preferred_element_type=F32) + o_cross[p] * xi_ref[p] for p in pairs]
        for p in pairs:
            state_ref[p] = states[p] * gch_ref[p] + kv[p] * same_head_kv
        o2 = [o[p] * o[p] for p in pairs]
        o2_hi = [o2[p].astype(BF16) for p in pairs]
        o2_hl = [jnp.concatenate([o2_hi[p], (o2[p] - o2_hi[p].astype(F32)).astype(BF16)], axis=1) for p in pairs]
        ms = [jnp.dot(o2_hl[p], mean2, preferred_element_type=F32) for p in pairs]
        for p in pairs:
            g = qkvg_ref[0, sl, pl.ds(3 * D_RET + p * PAIR, PAIR)]
            y = o[p] * lax.rsqrt(ms[p] + EPS) * nw_ref[p]
            o_ref[0, sl, cols[p]] = (y * (g * _sigmoid(g))).astype(BF16)


def _retention(ret3, cos, sin, dec, zeta, xi, gch, nw, B, T):
    npair = RET_HEADS // 2
    whole = lambda a: pl.BlockSpec(a.shape, lambda b, i: (0,) * a.ndim)
    return pl.pallas_call(
        _ret_kernel,
        grid=(B, T // RET_TC),
        in_specs=[pl.BlockSpec((1, RET_TC, 4 * D_RET), lambda b, i: (b, i, 0)), whole(cos), whole(sin), whole(dec),
                  whole(zeta), whole(xi), whole(gch), whole(nw)],
        out_specs=pl.BlockSpec((1, RET_TC, D_RET), lambda b, i: (b, i, 0)),
        out_shape=jax.ShapeDtypeStruct((B, T, D_RET), BF16),
        scratch_shapes=[pltpu.VMEM((npair, PAIR, PAIR), F32)],
        compiler_params=pltpu.CompilerParams(dimension_semantics=("parallel", "arbitrary"),
                                             vmem_limit_bytes=VMEM_LIMIT),
        name="retention",
    )(ret3, cos, sin, dec, zeta, xi, gch, nw)


def _cmp_kernel(x_ref, pos_ref, w1_ref, w2_ref, w2t_ref, knw_ref, o_ref, ot_ref):
    is_key = pl.program_id(1) == 0
    half = CMP_STRIDE * HEAD_DIM
    for g in range(NSA_KV_HEADS):
        x = x_ref[0, g]
        a = jnp.dot((x + pos_ref[0, 0:1, :]).astype(BF16), w1_ref[0, :half, :], preferred_element_type=F32)
        b = jnp.dot((x + pos_ref[0, 1:2, :]).astype(BF16), w1_ref[0, half:, :], preferred_element_type=F32)
        hid = a + pltpu.roll(b, b.shape[0] - 1, 0)
        hid = (hid * _sigmoid(hid)).astype(BF16)
        out = jnp.dot(hid, w2_ref[0], preferred_element_type=F32)
        ms = jnp.mean(out * out, axis=-1, keepdims=True)
        normed = out * lax.rsqrt(ms + EPS) * knw_ref[...]
        o_ref[0, 0, g] = jnp.where(is_key, normed, out)
        ot_ref[0, 0, g] = lax.dot_general(w2t_ref[0], hid, NT_DIMS, preferred_element_type=F32)


def _compress(xc, pos, w1, w2, w2t, knw, B, ncb):
    return pl.pallas_call(
        _cmp_kernel,
        grid=(B, 2),
        in_specs=[
            pl.BlockSpec((1, NSA_KV_HEADS, ncb, CMP_STRIDE * HEAD_DIM), lambda b, s: (b, s, 0, 0)),
            pl.BlockSpec((1, 2, CMP_STRIDE * HEAD_DIM), lambda b, s: (s, 0, 0)),
            pl.BlockSpec((1, CMP_BLOCK * HEAD_DIM, CMP_HIDDEN), lambda b, s: (s, 0, 0)),
            pl.BlockSpec((1, CMP_HIDDEN, HEAD_DIM), lambda b, s: (s, 0, 0)),
            pl.BlockSpec((1, HEAD_DIM, CMP_HIDDEN), lambda b, s: (s, 0, 0)),
            pl.BlockSpec((1, HEAD_DIM), lambda b, s: (0, 0)),
        ],
        out_specs=[
            pl.BlockSpec((1, 1, NSA_KV_HEADS, ncb, HEAD_DIM), lambda b, s: (b, s, 0, 0, 0)),
            pl.BlockSpec((1, 1, NSA_KV_HEADS, HEAD_DIM, ncb), lambda b, s: (b, s, 0, 0, 0)),
        ],
        out_shape=[
            jax.ShapeDtypeStruct((B, 2, NSA_KV_HEADS, ncb, HEAD_DIM), F32),
            jax.ShapeDtypeStruct((B, 2, NSA_KV_HEADS, HEAD_DIM, ncb), F32),
        ],
        compiler_params=pltpu.CompilerParams(dimension_semantics=("parallel", "parallel"),
                                             vmem_limit_bytes=VMEM_LIMIT),
        name="compress",
    )(xc, pos, w1, w2, w2t, knw)


def _nsa_kernel(qt_ref, qtn_ref, glt_ref, ng_ref, ks_ref, kw_ref, vst_ref, vwt_ref, kc_ref, vct_ref, mt_ref, qnw_ref,
                bg_ref, o_ref, qp_ref, oc_ref, sa_ref, sb_ref, mxa_ref, mxb_ref, sw_ref, m_ref, acc_ref):
    qi = pl.program_id(1)
    t0 = qi * (NSA_QT * NSA_TQ)
    tiles = range(NSA_QT)
    chains = [(qt, g) for qt in tiles for g in range(NSA_KV_HEADS)]
    gq = NSA_GROUP * HEAD_DIM
    gg = 16
    ncb = kc_ref.shape[3]
    n_slc = mt_ref.shape[0]

    def select_masks(base_t0):
        n_idx = lax.broadcasted_iota(jnp.int32, (ncb, NSA_TQ), 0)
        jb = lax.broadcasted_iota(jnp.int32, (n_slc, NSA_TQ), 0)
        out = []
        for qt in tiles:
            tok = base_t0 + qt * NSA_TQ
            tok_c = tok + lax.broadcasted_iota(jnp.int32, (ncb, NSA_TQ), 1)
            cbias = _tile4(jnp.where((n_idx * CMP_STRIDE + (CMP_BLOCK - 1)) <= tok_c, 0.0, NEG))
            tok_row = tok + lax.broadcasted_iota(jnp.int32, (1, NSA_TQ), 1)
            has_block = _tile4(jnp.where(tok_row >= CMP_BLOCK - 1, 1.0, 0.0))
            tok_s = tok + lax.broadcasted_iota(jnp.int32, (n_slc, NSA_TQ), 1)
            valid_s = jb * SLC_BLOCK <= tok_s
            force = (jb == (tok_s >> SLC_SHIFT)) | (jb == 0)
            out.append((cbias, has_block, valid_s, force))
        return out

    def select_scores(src_ref, slot, masks, c, rows=ncb):
        qt, g = chains[c]
        cols = []
        for r in range(NSA_GROUP):
            q = src_ref[0, qt, g * gq + r * HEAD_DIM:g * gq + (r + 1) * HEAD_DIM, :]
            ms = jnp.mean(q * q, axis=0, keepdims=True)
            cols.append(q * lax.rsqrt(ms + EPS) * qnw_ref[...] * (QK_SCALE * LOG2E))
        qs = jnp.concatenate(cols, axis=1).astype(BF16)
        qp_ref[slot, c, 0:HEAD_DIM, :] = qs
        return (jnp.dot(kc_ref[0, 0, g, 0:rows, :].astype(BF16), qs, preferred_element_type=F32)
                + masks[qt][0][0:rows, :])

    def select_probs(sc, slot, masks, c, want_scores=True):
        qt, g = chains[c]
        _, has_block, valid_s, force = masks[qt]
        rows = sc.shape[0]
        mc = jnp.max(sc, axis=0, keepdims=True)
        ec = jnp.exp2(sc - mc)
        lc = jnp.sum(ec, axis=0, keepdims=True)
        p = ec * (has_block / lc)
        oc_ref[slot, c] = jnp.dot(vct_ref[0, 0, g, :, 0:rows].astype(BF16), p.astype(BF16),
                                  preferred_element_type=F32)
        if not want_scores:
            return None
        ps = p[:, 0:NSA_TQ]
        for r in range(1, NSA_GROUP):
            ps = ps + p[:, r * NSA_TQ:(r + 1) * NSA_TQ]
        ps_hi = ps.astype(BF16)
        ps_lo = (ps - ps_hi.astype(F32)).astype(BF16)
        imp = (jnp.dot(mt_ref[:, 0:rows], ps_hi, preferred_element_type=F32)
               + jnp.dot(mt_ref[:, 0:rows], ps_lo, preferred_element_type=F32))
        return jnp.where(valid_s, jnp.where(force, imp + FORCE_BONUS, imp), NEG)

    def select_rank(scores, valid_s, slot, n_live):
        if n_live <= SLC_TOPK:
            for c, (qt, g) in enumerate(chains):
                qp_ref[slot, c, HEAD_DIM:2 * HEAD_DIM, :] = _tile4(jnp.where(valid_s[qt], 0.0, NEG).astype(BF16))
            return
        sub = lax.broadcasted_iota(jnp.int32, (SUBLANES, NSA_TQ), 0)
        for c, (qt, g) in enumerate(chains):
            score = scores[c]
            blocks = [score[v * SUBLANES:(v + 1) * SUBLANES, :] for v in range(n_live // SUBLANES)]
            ranks = [jnp.zeros((SUBLANES, NSA_TQ), F32) for _ in blocks]
            for i in range(n_live):
                row = score[i:i + 1, :]
                for v, blk in enumerate(blocks):
                    if v * SUBLANES > i:
                        beats = row >= blk
                    elif (v + 1) * SUBLANES <= i:
                        beats = row > blk
                    else:
                        beats = (row > blk) | ((row >= blk) & (sub > i - v * SUBLANES))
                    ranks[v] = ranks[v] + jnp.where(beats, 1.0, 0.0)
            dead = [jnp.full((SUBLANES, NSA_TQ), float(n_slc), F32)] * ((n_slc - n_live) // SUBLANES)
            rank = jnp.concatenate(ranks + dead, axis=0)
            sel = (rank < float(SLC_TOPK)) & valid_s[qt]
            qp_ref[slot, c, HEAD_DIM:2 * HEAD_DIM, :] = _tile4(jnp.where(sel, 0.0, NEG).astype(BF16))

    step_tokens = NSA_QT * NSA_TQ

    def live_blocks(step):
        return ((step + 1) * step_tokens - 1) // SLC_BLOCK + 1

    all_chains = list(range(len(chains)))

    @pl.when(qi == 0)
    def _():
        masks0 = select_masks(0)
        rows0 = min(ncb, -(-((step_tokens - CMP_BLOCK) // CMP_STRIDE + 1) // 16) * 16)
        assert live_blocks(0) <= SLC_TOPK
        for c in all_chains:
            select_probs(select_scores(qt_ref, 0, masks0, c, rows0), 0, masks0, c, want_scores=False)
        select_rank(None, [m[2] for m in masks0], 0, live_blocks(0))

    cur = qi % 2
    nxt = 1 - cur
    t_next = t0 + NSA_QT * NSA_TQ

    kt0, ks0, wbias = [], [], []
    c_minus_r = (lax.broadcasted_iota(jnp.int32, (WIN_KEYS, NSA_TQ), 1)
                 - lax.broadcasted_iota(jnp.int32, (WIN_KEYS, NSA_TQ), 0))
    for qt in tiles:
        kt0.append(jnp.maximum(qi * NSA_QT + qt - WIN_SIZE // NSA_TQ, 0))
        ks0.append(pl.multiple_of(kt0[qt] * NSA_TQ, NSA_TQ))
        delta = (t0 + qt * NSA_TQ - ks0[qt]) + c_minus_r
        in_window = lax.bitcast_convert_type(delta, jnp.uint32) < WIN_SIZE
        wbias.append(_tile4(jnp.where(in_window, 0.0, NEG)))

    def win_scores(c):
        qt, g = chains[c]
        sw = jnp.dot(kw_ref[0, g, pl.ds(ks0[qt], WIN_KEYS), 0:HEAD_DIM], qp_ref[cur, c, 0:HEAD_DIM, :],
                     preferred_element_type=F32) + wbias[qt]
        sw_ref[c] = sw
        return jnp.max(sw, axis=0, keepdims=True)

    def win_attend(c, mw):
        qt, g = chains[c]
        ewb = jnp.exp2(sw_ref[c] - mw).astype(BF16)
        vwt = jnp.concatenate([vwt_ref[0, kt0[qt] + j, g * V_ROWS:(g + 1) * V_ROWS, :]
                               for j in range(WIN_KEYS // LANES)], axis=1)
        ow_aug = jnp.dot(vwt, ewb, preferred_element_type=F32)
        return ow_aug[0:HEAD_DIM, :] * (1.0 / ow_aug[HEAD_DIM:HEAD_DIM + 1, :])

    vt_per_tile = SLC_TK // LANES

    def slc_scores(j, dst_ref, mx_ref, which=all_chains):
        kst = pl.multiple_of(j * SLC_TK, SLC_TK)
        for c in which:
            qt, g = chains[c]
            s = jnp.dot(ks_ref[0, g, pl.ds(kst, SLC_TK), :], qp_ref[cur, c], preferred_element_type=F32)
            dst_ref[c] = s
            mx_ref[c] = jnp.max(s, axis=0, keepdims=True)

    def slc_update(j, src_ref, mx_ref, causal, which=all_chains, visible=SLC_TK):
        if causal:
            band = _tile4(jnp.where(lax.broadcasted_iota(jnp.int32, (NSA_TQ, NSA_TQ), 0)
                                    <= lax.broadcasted_iota(jnp.int32, (NSA_TQ, NSA_TQ), 1), 0.0, NEG))
        for c in which:
            qt, g = chains[c]
            if causal:
                r0 = visible - step_tokens + qt * NSA_TQ
                rows = r0 + NSA_TQ
                s_band = src_ref[c, r0:rows, :] + band
                s = jnp.concatenate([src_ref[c, 0:r0, :], s_band], axis=0) if r0 > 0 else s_band
                tile_max = jnp.max(s, axis=0, keepdims=True)
            else:
                rows = SLC_TK
                s = src_ref[c]
                tile_max = mx_ref[c]
            m_old = m_ref[c]
            m_new = jnp.maximum(m_old, tile_max)
            alpha = jnp.exp2(m_old - m_new)
            eb = jnp.exp2(s - m_new).astype(BF16)
            vt = jnp.concatenate([vst_ref[0, j * vt_per_tile + jj, g * V_ROWS:(g + 1) * V_ROWS, :]
                                  for jj in range(rows // LANES)], axis=1)
            acc_ref[c] = alpha * acc_ref[c] + jnp.dot(vt, eb, preferred_element_type=F32)
            m_ref[c] = m_new

    masks = select_masks(t_next)

    def matmul_stage(c):
        mw = win_scores(c)
        sc = select_scores(qtn_ref, nxt, masks, c)
        slc_scores(0, sa_ref, mxa_ref, [c])
        return sc, mw

    sel_scores, ow_t = [], []
    staged = matmul_stage(0)
    for c in all_chains:
        staged_next = matmul_stage(c + 1) if c + 1 < len(chains) else None
        sel_scores.append(select_probs(staged[0], nxt, masks, c))
        ow_t.append(win_attend(c, staged[1]))
        staged = staged_next
    sel_valid = [m[2] for m in masks]

    n_steps = ks_ref.shape[2] // step_tokens
    need = jnp.where(qi + 1 < n_steps, live_blocks(qi + 1), 0)
    bounds = list(range(SLC_TOPK, n_slc + 1, SUBLANES))
    for lo, hi in zip([0] + bounds[:-1], bounds):
        in_range = (need > lo) if hi == bounds[-1] else ((need > lo) & (need <= hi))
        pl.when(in_range)(functools.partial(select_rank, sel_scores, sel_valid, nxt, hi))

    m_ref[...] = jnp.full(m_ref.shape, NEG, F32)
    acc_ref[...] = jnp.zeros(acc_ref.shape, F32)
    n_full = t0 // SLC_TK

    def pair(jj, carry):
        j = 2 * jj
        for c in all_chains:
            slc_scores(j + 1, sb_ref, mxb_ref, [c])
            slc_update(j, sa_ref, mxa_ref, False, [c])
        for c in all_chains:
            slc_scores(j + 2, sa_ref, mxa_ref, [c])
            slc_update(j + 1, sb_ref, mxb_ref, False, [c])
        return carry

    lax.fori_loop(0, n_full // 2, pair, 0)

    visible = t0 - n_full * SLC_TK + step_tokens

    def tail(odd, rows):
        if odd:
            for c in all_chains:
                slc_scores(n_full, sb_ref, mxb_ref, [c])
                slc_update(n_full - 1, sa_ref, mxa_ref, False, [c])
            slc_update(n_full, sb_ref, mxb_ref, True, visible=rows)
        else:
            slc_update(n_full, sa_ref, mxa_ref, True, visible=rows)

    for rows in range(step_tokens, SLC_TK + 1, step_tokens):
        for odd in (False, True):
            parity = (n_full % 2 == 1) if odd else (n_full % 2 == 0)
            pl.when((visible == rows) & parity)(functools.partial(tail, odd, rows))

    for qt in tiles:
        gates = _sigmoid(glt_ref[0, qt] + bg_ref[...])
        outs = []
        for g in range(NSA_KV_HEADS):
            c = qt * NSA_KV_HEADS + g
            os_t = acc_ref[c, 0:HEAD_DIM, :] * (1.0 / acc_ref[c, HEAD_DIM:HEAD_DIM + 1, :])
            oc_t = oc_ref[cur, c]
            for r in range(NSA_GROUP):
                sl = slice(r * NSA_TQ, (r + 1) * NSA_TQ)
                row = g * gg + r
                outs.append(gates[row:row + 1, :] * oc_t[:, sl]
                            + gates[row + NSA_GROUP:row + NSA_GROUP + 1, :] * os_t[:, sl]
                            + gates[row + 2 * NSA_GROUP:row + 2 * NSA_GROUP + 1, :] * ow_t[c][:, sl])
        o_tok = jnp.concatenate(outs, axis=0).T
        rows = pl.ds(qt * NSA_TQ, NSA_TQ)
        ng = ng_ref[0, rows, :]
        o_ref[0, rows, :] = (o_tok * (ng * _sigmoid(ng))).astype(BF16)


def _nsa(nqt, glt, ng3, ks, kw, vt, kc, vct, mt, qnw, bg, B, T):
    nt = T // LANES
    ncb = kc.shape[3]
    G = NSA_KV_HEADS
    NC = NSA_QT * G
    ncols = NSA_GROUP * NSA_TQ
    steps = T // (NSA_QT * NSA_TQ)
    return pl.pallas_call(
        _nsa_kernel,
        grid=(B, steps),
        in_specs=[
            pl.BlockSpec((1, NSA_QT, D_NSA, LANES), lambda b, i: (b, 0, 0, 0)),
            pl.BlockSpec((1, NSA_QT, D_NSA, LANES), lambda b, i: (b, jnp.minimum(i + 1, steps - 1), 0, 0)),
            pl.BlockSpec((1, NSA_QT, 16 * G, LANES), lambda b, i: (b, i, 0, 0)),
            pl.BlockSpec((1, NSA_QT * NSA_TQ, D_NSA), lambda b, i: (b, i, 0)),
            pl.BlockSpec((1, G, T, PAIR), lambda b, i: (b, 0, 0, 0)),
            pl.BlockSpec((1, G, T, PAIR), lambda b, i: (b, 0, 0, 0)),
            pl.BlockSpec((1, nt, G * V_ROWS, LANES), lambda b, i: (b, 0, 0, 0)),
            pl.BlockSpec((1, nt, G * V_ROWS, LANES), lambda b, i: (b, 0, 1, 0)),
            pl.BlockSpec((1, 1, G, ncb, HEAD_DIM), lambda b, i: (b, 0, 0, 0, 0)),
            pl.BlockSpec((1, 1, G, HEAD_DIM, ncb), lambda b, i: (b, 1, 0, 0, 0)),
            pl.BlockSpec(mt.shape, lambda b, i: (0, 0)),
            pl.BlockSpec((HEAD_DIM, 1), lambda b, i: (0, 0)),
            pl.BlockSpec((16 * G, 1), lambda b, i: (0, 0)),
        ],
        out_specs=pl.BlockSpec((1, NSA_QT * NSA_TQ, D_NSA), lambda b, i: (b, i, 0)),
        out_shape=jax.ShapeDtypeStruct((B, T, D_NSA), BF16),
        scratch_shapes=[
            pltpu.VMEM((2, NC, 2 * HEAD_DIM, ncols), BF16),
            pltpu.VMEM((2, NC, HEAD_DIM, ncols), F32),
            pltpu.VMEM((NC, SLC_TK, ncols), F32),
            pltpu.VMEM((NC, SLC_TK, ncols), F32),
            pltpu.VMEM((NC, 1, ncols), F32),
            pltpu.VMEM((NC, 1, ncols), F32),
            pltpu.VMEM((NC, WIN_KEYS, ncols), F32),
            pltpu.VMEM((NC, 1, ncols), F32),
            pltpu.VMEM((NC, V_ROWS, ncols), F32),
        ],
        compiler_params=pltpu.CompilerParams(dimension_semantics=("parallel", "arbitrary"),
                                             vmem_limit_bytes=VMEM_LIMIT),
        name="nsa",
    )(nqt, nqt, glt, ng3, ks, kw, vt, vt, kc, vct, mt, qnw, bg)


def _out_kernel(x_ref, yr_ref, yn_ref, wr_ref, wn_ref, o_ref):
    o_ref[...] = (x_ref[...]
                  + jnp.dot(yr_ref[...], wr_ref[...], preferred_element_type=F32)
                  + jnp.dot(yn_ref[...], wn_ref[...], preferred_element_type=F32))


def _outproj(x2, yr, yn, wr, wn):
    N = x2.shape[0]
    row = lambda i: (i, 0)
    const = lambda i: (0, 0)
    return pl.pallas_call(
        _out_kernel,
        grid=(N // OUT_TM,),
        in_specs=[pl.BlockSpec((OUT_TM, D_MODEL), row), pl.BlockSpec((OUT_TM, D_RET), row),
                  pl.BlockSpec((OUT_TM, D_NSA), row), pl.BlockSpec(wr.shape, const), pl.BlockSpec(wn.shape, const)],
        out_specs=pl.BlockSpec((OUT_TM, D_MODEL), row),
        out_shape=jax.ShapeDtypeStruct((N, D_MODEL), F32),
        compiler_params=pltpu.CompilerParams(dimension_semantics=("parallel",), vmem_limit_bytes=VMEM_LIMIT),
        name="outproj",
    )(x2, yr, yn, wr, wn)


@functools.lru_cache(maxsize=None)
def _tables(T):
    half = HEAD_DIM // 2
    inv = ROPE_THETA ** (-np.arange(half, dtype=np.float64) / half)
    ang = np.arange(T, dtype=np.float64)[:, None] * inv[None, :]
    cos = np.concatenate([np.cos(ang)] * 4, axis=1).astype(np.float32)
    sin = np.concatenate([-np.sin(ang), -np.sin(ang), np.sin(ang), np.sin(ang)], axis=1).astype(np.float32)

    C = RET_CHUNK
    log_g = np.log1p(-np.exp2(-5.0 - np.arange(RET_HEADS, dtype=np.float64)))
    pos = np.arange(C, dtype=np.float64)
    diff = pos[:, None] - pos[None, :]
    decay = np.where(diff >= 0, np.exp(log_g[:, None, None] * np.maximum(diff, 0.0)), 0.0)
    zeta = np.exp(log_g[:, None] * (C - 1.0 - pos))
    xi = np.exp(log_g[:, None] * (pos + 1.0))
    g_chunk = np.exp(log_g * C)
    npair = RET_HEADS // 2

    def pair_lanes(a):
        return np.repeat(a.reshape(npair, 2, C).transpose(0, 2, 1), HEAD_DIM, axis=2).astype(np.float32)

    dec = decay.reshape(npair, 2, C, C).transpose(0, 2, 1, 3).reshape(npair, C, 2 * C).astype(np.float32)
    gch = np.repeat(g_chunk.reshape(npair, 1, 2), HEAD_DIM, axis=2).astype(np.float32)

    n_cmp = (T - CMP_BLOCK) // CMP_STRIDE + 1
    ncb = T // CMP_STRIDE
    p = np.arange(n_cmp)[:, None] * CMP_STRIDE + np.arange(CMP_BLOCK)[None, :]
    blk = p // SLC_BLOCK
    M = (blk[:, :, None] == np.arange(T // SLC_BLOCK)[None, None, :]).mean(axis=1)
    mt = np.zeros((T // SLC_BLOCK, ncb), np.float32)
    mt[:, :n_cmp] = M.T
    return cos, sin, dec, pair_lanes(zeta), pair_lanes(xi), gch, mt


def kernel(x, norm_w, w_in, ret_norm_w, q_norm_w, k_norm_cmp, k_norm_slc, k_norm_win, cmp_pos_k, cmp_w1_k, cmp_w2_k,
           cmp_pos_v, cmp_w1_v, cmp_w2_v, b_gate, w_out):
    B, T, D = x.shape
    depth = norm_w.shape[0]
    cos, sin, dec, zeta, xi, gch, mt = _tables(T)
    ncb = T // CMP_STRIDE
    half = CMP_STRIDE * HEAD_DIM
    gate_src = np.zeros((NSA_KV_HEADS, 16), np.int32)
    gate_ok = np.zeros((NSA_KV_HEADS, 16), bool)
    for g in range(NSA_KV_HEADS):
        for br in range(N_BRANCH):
            for r in range(NSA_GROUP):
                gate_src[g, br * NSA_GROUP + r] = br * NSA_HEADS + g * NSA_GROUP + r
                gate_ok[g, br * NSA_GROUP + r] = True
    gate_src = gate_src.reshape(-1)
    gate_ok = gate_ok.reshape(-1)

    x2 = x.reshape(B * T, D)
    for layer in range(depth):
        w = w_in[layer].astype(BF16)
        o_ng = 4 * D_RET + D_NSA
        o_kv = o_ng + D_NSA
        quarter = HEAD_DIM // 2
        pair_perm = np.concatenate([np.arange(quarter), HEAD_DIM + np.arange(quarter),
                                    quarter + np.arange(quarter), HEAD_DIM + quarter + np.arange(quarter)])
        qk_perm = np.concatenate([p * PAIR + pair_perm for p in range(RET_HEADS // 2)])
        w_ret = jnp.concatenate([w[:, :D_RET][:, qk_perm], w[:, D_RET:2 * D_RET][:, qk_perm],
                                 w[:, 2 * D_RET:4 * D_RET]], axis=1)
        wt_q = w[:, 4 * D_RET:o_ng].T
        w_ng = w[:, o_ng:o_kv]
        w_ckv = w[:, o_kv:o_kv + 2 * D_KV]
        w_skwk = jnp.concatenate([w[:, o_kv + 2 * D_KV:o_kv + 3 * D_KV], w[:, o_kv + 4 * D_KV:o_kv + 5 * D_KV]], axis=1)
        wt_v = jnp.concatenate([w[:, o_kv + 3 * D_KV:o_kv + 4 * D_KV], w[:, o_kv + 5 * D_KV:o_kv + 6 * D_KV]], axis=1).T
        w_gl = w[:, o_kv + 6 * D_KV:]
        wt_g = jnp.where(gate_ok[:, None], w_gl.T[gate_src], jnp.zeros((), BF16))
        bg = jnp.where(gate_ok, b_gate[layer][gate_src], 0.0).reshape(-1, 1)

        knw = jnp.stack([jnp.tile(k_norm_slc[layer], 2), jnp.tile(k_norm_win[layer], 2)])
        ret, ng, xc, ks, kw, nqt, vt, glt = _proj(x2, norm_w[layer].reshape(1, D), w_ret, w_ng, w_ckv, w_skwk,
                                                  wt_q, wt_v, wt_g, knw, B, T)

        nw_pair = ret_norm_w[layer].reshape(RET_HEADS // 2, 1, PAIR)
        y_ret = _retention(ret.reshape(B, T, 4 * D_RET), jnp.asarray(cos), jnp.asarray(sin), jnp.asarray(dec),
                           jnp.asarray(zeta), jnp.asarray(xi), jnp.asarray(gch), nw_pair, B, T)

        pos =jnp.stack([cmp_pos_k[layer], cmp_pos_v[layer]]).reshape(2, 2, half)
        w1 = jnp.stack([cmp_w1_k[layer], cmp_w1_v[layer]]).astype(BF16)
        w2 = jnp.stack([cmp_w2_k[layer], cmp_w2_v[layer]]).astype(BF16)
        w2t = jnp.swapaxes(w2, 1, 2)
        kc, vct = _compress(xc, pos, w1, w2, w2t, k_norm_cmp[layer].reshape(1, HEAD_DIM), B, ncb)

        y_nsa = _nsa(nqt, glt, ng.reshape(B, T, D_NSA), ks, kw, vt, kc, vct, jnp.asarray(mt).astype(BF16),
                     q_norm_w[layer].reshape(HEAD_DIM, 1), bg, B, T)

        wo = w_out[layer].astype(BF16)
        x2 = _outproj(x2, y_ret.reshape(B * T, D_RET), y_nsa.reshape(B * T, D_NSA), wo[:D_RET], wo[D_RET:])
    return x2.reshape(B, T, D)
```

```python
import functools

import numpy as np
import jax
import jax.numpy as jnp
from jax import lax
from jax.experimental import pallas as pl
from jax.experimental.pallas import tpu as pltpu

F32 = jnp.float32
BF16 = jnp.bfloat16

D_MODEL = 1024
HEAD_DIM = 64
HEAD_SHIFT = 6
RET_HEADS = 8
NSA_HEADS = 8
NSA_KV_HEADS = 2
NSA_GROUP = NSA_HEADS // NSA_KV_HEADS
D_RET = RET_HEADS * HEAD_DIM
D_NSA = NSA_HEADS * HEAD_DIM
D_KV = NSA_KV_HEADS * HEAD_DIM
N_BRANCH = 3
RET_CHUNK = 128
ROPE_THETA = 10000.0
CMP_BLOCK = 32
CMP_STRIDE = 16
CMP_HIDDEN = 256
SLC_BLOCK = 64
SLC_SHIFT = 6
SLC_TOPK = 16
WIN_SIZE = 512
EPS = 1e-6
NEG = -1e30
FORCE_BONUS = 1e4
QK_SCALE = HEAD_DIM ** -0.5
LOG2E = 1.4426950408889634
V_ROWS = HEAD_DIM + 16

LANES = 128
SUBLANES = 8
PAIR = 2 * HEAD_DIM
VMEM_LIMIT = 48 * 1024 * 1024

PROJ_TM = 512
RET_TC = 1024
NSA_TQ = 128
NSA_QT = 2
SLC_TK = 512
WIN_KEYS = WIN_SIZE + NSA_TQ
OUT_TM = 1024

NT_DIMS = (((1,), (1,)), ((), ()))
TN_DIMS = (((0,), (0,)), ((), ()))


def _sigmoid(x):
    return 1.0 / (1.0 + jnp.exp(-x))


def _tile4(a):
    return jnp.concatenate([a, a, a, a], axis=1)


def _proj_kernel(steps_per_batch, x_ref, nw_ref, w_ret_ref, w_ng_ref, w_ckv_ref, w_skwk_ref, wt_q_ref, wt_v_ref,
                 wt_g_ref, knw_ref, ret_ref, ng_ref, xc_ref, ks_ref, kw_ref, nqt_ref, vt_ref, glt_ref, ckv_scr):
    x = x_ref[...]
    ms = jnp.mean(x * x, axis=-1, keepdims=True)
    h = (x * lax.rsqrt(ms + EPS) * nw_ref[...]).astype(BF16)
    ret_ref[...] = jnp.dot(h, w_ret_ref[...], preferred_element_type=F32)
    ng_ref[...] = jnp.dot(h, w_ng_ref[...], preferred_element_type=F32)

    ckv = jnp.dot(h, w_ckv_ref[...], preferred_element_type=F32)
    for half in range(2 * D_KV // LANES):
        ckv_scr[half] = ckv[:, half * LANES:(half + 1) * LANES]
    for l in range(CMP_STRIDE):
        for half in range(2 * D_KV // LANES):
            rows = ckv_scr[half, pl.ds(l, PROJ_TM // CMP_STRIDE, stride=CMP_STRIDE), :]
            for s in range(LANES // HEAD_DIM):
                xc_ref[0, half * (LANES // HEAD_DIM) + s, :, l * HEAD_DIM:(l + 1) * HEAD_DIM] = (
                    rows[:, s * HEAD_DIM:(s + 1) * HEAD_DIM])

    skwk = jnp.dot(h, w_skwk_ref[...], preferred_element_type=F32)
    lane = lax.broadcasted_iota(jnp.int32, (1, PAIR), 1)
    head0 = lane < HEAD_DIM
    m0 = jnp.where(head0, 1.0, 0.0)
    m1 = 1.0 - m0
    t_start = (pl.program_id(0) % steps_per_batch) * PROJ_TM
    tok = t_start + lax.broadcasted_iota(jnp.int32, (PROJ_TM, PAIR), 0)
    col = lax.broadcasted_iota(jnp.int32, (PROJ_TM, PAIR), 1)
    indicator = jnp.where((tok >> SLC_SHIFT) == col - HEAD_DIM, 1.0, 0.0)

    def pair_normed(t, w):
        t2 = t * t
        ms0 = jnp.sum(t2 * m0, axis=-1, keepdims=True)
        ms1 = jnp.sum(t2 * m1, axis=-1, keepdims=True)
        return t * lax.rsqrt(jnp.where(head0, ms0, ms1) * (1.0 / HEAD_DIM) + EPS) * w

    ns = pair_normed(skwk[:, :PAIR], knw_ref[0:1, :])
    nwin = pair_normed(skwk[:, PAIR:], knw_ref[1:2, :])
    for g in range(NSA_KV_HEADS):
        s_g = ns if g == 0 else pltpu.roll(ns, HEAD_DIM, 1)
        w_g = nwin if g == 0 else pltpu.roll(nwin, HEAD_DIM, 1)
        ks_ref[0, g] = jnp.where(head0, s_g, indicator).astype(BF16)
        kw_ref[0, g] = jnp.where(head0, w_g, 0.0).astype(BF16)

    qt = lax.dot_general(wt_q_ref[...], h, NT_DIMS, preferred_element_type=F32)
    vt = lax.dot_general(wt_v_ref[...], h, NT_DIMS, preferred_element_type=F32)
    gt = lax.dot_general(wt_g_ref[...], h, NT_DIMS, preferred_element_type=F32)
    pad_row = lax.broadcasted_iota(jnp.int32, (V_ROWS - HEAD_DIM, LANES), 0)
    ones_pad = jnp.where(pad_row == 0, 1.0, 0.0).astype(BF16)
    for j in range(PROJ_TM // LANES):
        sl = slice(j * LANES, (j + 1) * LANES)
        nqt_ref[0, j] = qt[:, sl]
        for blk in range(2 * NSA_KV_HEADS):
            vt_ref[0, j, blk * V_ROWS:blk * V_ROWS + HEAD_DIM, :] = (
                vt[blk * HEAD_DIM:(blk + 1) * HEAD_DIM, sl].astype(BF16))
            vt_ref[0, j, blk * V_ROWS + HEAD_DIM:(blk + 1) * V_ROWS, :] = ones_pad
        glt_ref[0, j] = gt[:, sl]


def _proj(x2, nw, w_ret, w_ng, w_ckv, w_skwk, wt_q, wt_v, wt_g, knw, B, T):
    N = B * T
    tpb = T // PROJ_TM
    sub = PROJ_TM // LANES
    nt = T // LANES
    const = lambda i: (0, 0)
    row = lambda i: (i, 0)
    trn = lambda i: (i // tpb, i % tpb, 0, 0)
    tokm = lambda i: (i // tpb, 0, i % tpb, 0)
    return pl.pallas_call(
        functools.partial(_proj_kernel, tpb),
        grid=(N // PROJ_TM,),
        in_specs=[
            pl.BlockSpec((PROJ_TM, D_MODEL), row),
            pl.BlockSpec((1, D_MODEL), const),
            pl.BlockSpec(w_ret.shape, const),
            pl.BlockSpec(w_ng.shape, const),
            pl.BlockSpec(w_ckv.shape, const),
            pl.BlockSpec(w_skwk.shape, const),
            pl.BlockSpec(wt_q.shape, const),
            pl.BlockSpec(wt_v.shape, const),
            pl.BlockSpec(wt_g.shape, const),
            pl.BlockSpec(knw.shape, const),
        ],
        out_specs=[
            pl.BlockSpec((PROJ_TM, 4 * D_RET), row),
            pl.BlockSpec((PROJ_TM, D_NSA), row),
            pl.BlockSpec((1, 2 * NSA_KV_HEADS, PROJ_TM // CMP_STRIDE, CMP_STRIDE * HEAD_DIM), tokm),
            pl.BlockSpec((1, NSA_KV_HEADS, PROJ_TM, PAIR), tokm),
            pl.BlockSpec((1, NSA_KV_HEADS, PROJ_TM, PAIR), tokm),
            pl.BlockSpec((1, sub, D_NSA, LANES), trn),
            pl.BlockSpec((1, sub, 2 * NSA_KV_HEADS * V_ROWS, LANES), trn),
            pl.BlockSpec((1, sub, 32, LANES), trn),
        ],
        out_shape=[
            jax.ShapeDtypeStruct((N, 4 * D_RET), F32),
            jax.ShapeDtypeStruct((N, D_NSA), F32),
            jax.ShapeDtypeStruct((B, 2 * NSA_KV_HEADS, T // CMP_STRIDE, CMP_STRIDE * HEAD_DIM), F32),
            jax.ShapeDtypeStruct((B, NSA_KV_HEADS, T, PAIR), BF16),
            jax.ShapeDtypeStruct((B, NSA_KV_HEADS, T, PAIR), BF16),
            jax.ShapeDtypeStruct((B, nt, D_NSA, LANES), F32),
            jax.ShapeDtypeStruct((B, nt, 2 * NSA_KV_HEADS * V_ROWS, LANES), BF16),
            jax.ShapeDtypeStruct((B, nt, 32, LANES), F32),
        ],
        scratch_shapes=[pltpu.VMEM((2 * D_KV // LANES, PROJ_TM, LANES), F32)],
        compiler_params=pltpu.CompilerParams(dimension_semantics=("parallel",), vmem_limit_bytes=VMEM_LIMIT),
        name="proj",
    )(x2, nw, w_ret, w_ng, w_ckv, w_skwk, wt_q, wt_v, wt_g, knw)


def _ret_kernel(qkvg_ref, cos_ref, sin_ref, dec_ref, zeta_ref, xi_ref, gch_ref, nw_ref, o_ref, state_ref):
    @pl.when(pl.program_id(1) == 0)
    def _():
        state_ref[...] = jnp.zeros_like(state_ref)

    lane = lax.broadcasted_iota(jnp.int32, (1, PAIR), 1)
    q_head = (lane >> (HEAD_SHIFT - 1)) & 1
    v_head = lane >> HEAD_SHIFT
    q_mask = [jnp.where(q_head == h, 1.0, 0.0).astype(BF16) for h in (0, 1)]
    v_mask = [jnp.where(v_head == h, 1.0, 0.0).astype(BF16) for h in (0, 1)]
    row_qh = (lax.broadcasted_iota(jnp.int32, (PAIR, PAIR), 0) >> (HEAD_SHIFT - 1)) & 1
    row_vh = lax.broadcasted_iota(jnp.int32, (PAIR, PAIR), 0) >> HEAD_SHIFT
    col_vh = lax.broadcasted_iota(jnp.int32, (PAIR, PAIR), 1) >> HEAD_SHIFT
    same_head_kv = jnp.where(row_qh == col_vh, 1.0, 0.0)
    head_mean = jnp.where(row_vh == col_vh, 1.0 / HEAD_DIM, 0.0).astype(BF16)

    pairs = range(RET_HEADS // 2)
    mean2 = jnp.concatenate([head_mean, head_mean], axis=0)
    tok0 = pl.program_id(1) * RET_TC
    for c in range(RET_TC // RET_CHUNK):
        sl = pl.ds(c * RET_CHUNK, RET_CHUNK)
        pos = pl.ds(pl.multiple_of(tok0 + c * RET_CHUNK, RET_CHUNK), RET_CHUNK)
        cos = cos_ref[pos, :]
        sin = sin_ref[pos, :]
        cols = [slice(p * PAIR, (p + 1) * PAIR) for p in pairs]
        qb, kb, vb, vzb = [], [], [], []
        for p in pairs:
            q = qkvg_ref[0, sl, pl.ds(p * PAIR, PAIR)]
            k = qkvg_ref[0, sl, pl.ds(D_RET + p * PAIR, PAIR)]
            v = qkvg_ref[0, sl, pl.ds(2 * D_RET + p * PAIR, PAIR)]
            qb.append((q * cos + pltpu.roll(q, HEAD_DIM, 1) * sin).astype(BF16))
            kb.append(((k * cos + pltpu.roll(k, HEAD_DIM, 1) * sin) * QK_SCALE).astype(BF16))
            vb.append(v.astype(BF16))
            vzb.append((v * zeta_ref[p]).astype(BF16))
        states = [state_ref[p] for p in pairs]
        kk = [jnp.concatenate([kb[p] * q_mask[h] for h in (0, 1)], axis=0) for p in pairs]
        s = [lax.dot_general(qb[p], kk[p], NT_DIMS, preferred_element_type=F32) for p in pairs]
        o_cross = [jnp.dot(qb[p], states[p].astype(BF16), preferred_element_type=F32) for p in pairs]
        kv = [lax.dot_general(kb[p], vzb[p], TN_DIMS, preferred_element_type=F32) for p in pairs]
        sb = [(s[p] * dec_ref[p]).astype(BF16) for p in pairs]
        vv = [jnp.concatenate([vb[p] * v_mask[h] for h in (0, 1)], axis=0) for p in pairs]
        o = [jnp.dot(sb[p], vv[p], preferred_element_type=F32) + o_cross[p] * xi_ref[p] for p in pairs]
        for p in pairs:
            state_ref[p] = states[p] * gch_ref[p] + kv[p] * same_head_kv
        o2 = [o[p] * o[p] for p in pairs]
        o2_hi = [o2[p].astype(BF16) for p in pairs]
        o2_hl = [jnp.concatenate([o2_hi[p], (o2[p] - o2_hi[p].astype(F32)).astype(BF16)], axis=1) for p in pairs]
        ms = [jnp.dot(o2_hl[p], mean2, preferred_element_type=F32) for p in pairs]
        for p in pairs:
            g = qkvg_ref[0, sl, pl.ds(3 * D_RET + p * PAIR, PAIR)]
            y = o[p] * lax.rsqrt(ms[p] + EPS) * nw_ref[p]
            o_ref[0, sl, cols[p]] = (y * (g * _sigmoid(g))).astype(BF16)


def _retention(ret3, cos, sin, dec, zeta, xi, gch, nw, B, T):
    npair = RET_HEADS // 2
    whole = lambda a: pl.BlockSpec(a.shape, lambda b, i: (0,) * a.ndim)
    return pl.pallas_call(
        _ret_kernel,
        grid=(B, T // RET_TC),
        in_specs=[pl.BlockSpec((1, RET_TC, 4 * D_RET), lambda b, i: (b, i, 0)), whole(cos), whole(sin), whole(dec),
                  whole(zeta), whole(xi), whole(gch), whole(nw)],
        out_specs=pl.BlockSpec((1, RET_TC, D_RET), lambda b, i: (b, i, 0)),
        out_shape=jax.ShapeDtypeStruct((B, T, D_RET), BF16),
        scratch_shapes=[pltpu.VMEM((npair, PAIR, PAIR), F32)],
        compiler_params=pltpu.CompilerParams(dimension_semantics=("parallel", "arbitrary"),
                                             vmem_limit_bytes=VMEM_LIMIT),
        name="retention",
    )(ret3, cos, sin, dec, zeta, xi, gch, nw)


def _cmp_kernel(x_ref, pos_ref, w1_ref, w2_ref, w2t_ref, knw_ref, o_ref, ot_ref):
    is_key = pl.program_id(1) == 0
    half = CMP_STRIDE * HEAD_DIM
    for g in range(NSA_KV_HEADS):
        x = x_ref[0, g]
        a = jnp.dot((x + pos_ref[0, 0:1, :]).astype(BF16), w1_ref[0, :half, :], preferred_element_type=F32)
        b = jnp.dot((x + pos_ref[0, 1:2, :]).astype(BF16), w1_ref[0, half:, :], preferred_element_type=F32)
        hid = a + pltpu.roll(b, b.shape[0] - 1, 0)
        hid = (hid * _sigmoid(hid)).astype(BF16)
        out = jnp.dot(hid, w2_ref[0], preferred_element_type=F32)
        ms = jnp.mean(out * out, axis=-1, keepdims=True)
        normed = out * lax.rsqrt(ms + EPS) * knw_ref[...]
        o_ref[0, 0, g] = jnp.where(is_key, normed, out)
        ot_ref[0, 0, g] = lax.dot_general(w2t_ref[0], hid, NT_DIMS, preferred_element_type=F32)


def _compress(xc, pos, w1, w2, w2t, knw, B, ncb):
    return pl.pallas_call(
        _cmp_kernel,
        grid=(B, 2),
        in_specs=[
            pl.BlockSpec((1, NSA_KV_HEADS, ncb, CMP_STRIDE * HEAD_DIM), lambda b, s: (b, s, 0, 0)),
            pl.BlockSpec((1, 2, CMP_STRIDE * HEAD_DIM), lambda b, s: (s, 0, 0)),
            pl.BlockSpec((1, CMP_BLOCK * HEAD_DIM, CMP_HIDDEN), lambda b, s: (s, 0, 0)),
            pl.BlockSpec((1, CMP_HIDDEN, HEAD_DIM), lambda b, s: (s, 0, 0)),
            pl.BlockSpec((1, HEAD_DIM, CMP_HIDDEN), lambda b, s: (s, 0, 0)),
            pl.BlockSpec((1, HEAD_DIM), lambda b, s: (0, 0)),
        ],
        out_specs=[
            pl.BlockSpec((1, 1, NSA_KV_HEADS, ncb, HEAD_DIM), lambda b, s: (b, s, 0, 0, 0)),
            pl.BlockSpec((1, 1, NSA_KV_HEADS, HEAD_DIM, ncb), lambda b, s: (b, s, 0, 0, 0)),
        ],
        out_shape=[
            jax.ShapeDtypeStruct((B, 2, NSA_KV_HEADS, ncb, HEAD_DIM), F32),
            jax.ShapeDtypeStruct((B, 2, NSA_KV_HEADS, HEAD_DIM, ncb), F32),
        ],
        compiler_params=pltpu.CompilerParams(dimension_semantics=("parallel", "parallel"),
                                             vmem_limit_bytes=VMEM_LIMIT),
        name="compress",
    )(xc, pos, w1, w2, w2t, knw)


def _nsa_kernel(qt_ref, qtn_ref, glt_ref, ng_ref, ks_ref, kw_ref, vst_ref, vwt_ref, kc_ref, vct_ref, mt_ref, qnw_ref,
                bg_ref, o_ref, qp_ref, oc_ref, sa_ref, sb_ref, mxa_ref, mxb_ref, sw_ref, ow_ref, m_ref, acc_ref):
    qi = pl.program_id(1)
    t0 = qi * (NSA_QT * NSA_TQ)
    tiles = range(NSA_QT)
    chains = [(qt, g) for qt in tiles for g in range(NSA_KV_HEADS)]
    gq = NSA_GROUP * HEAD_DIM
    gg = 16
    ncb = kc_ref.shape[3]
    n_slc = mt_ref.shape[0]

    def select_masks(base_t0):
        n_idx = lax.broadcasted_iota(jnp.int32, (ncb, NSA_TQ), 0)
        jb = lax.broadcasted_iota(jnp.int32, (n_slc, NSA_TQ), 0)
        out = []
        for qt in tiles:
            tok = base_t0 + qt * NSA_TQ
            tok_c = tok + lax.broadcasted_iota(jnp.int32, (ncb, NSA_TQ), 1)
            cbias = _tile4(jnp.where((n_idx * CMP_STRIDE + (CMP_BLOCK - 1)) <= tok_c, 0.0, NEG))
            tok_row = tok + lax.broadcasted_iota(jnp.int32, (1, NSA_TQ), 1)
            has_block = _tile4(jnp.where(tok_row >= CMP_BLOCK - 1, 1.0, 0.0))
            tok_s = tok + lax.broadcasted_iota(jnp.int32, (n_slc, NSA_TQ), 1)
            valid_s = jb * SLC_BLOCK <= tok_s
            force = (jb == (tok_s >> SLC_SHIFT)) | (jb == 0)
            out.append((cbias, has_block, valid_s, force))
        return out

    def select_scores(src_ref, slot, masks, c, rows=ncb):
        qt, g = chains[c]
        cols = []
        for r in range(NSA_GROUP):
            q = src_ref[0, qt, g * gq + r * HEAD_DIM:g * gq + (r + 1) * HEAD_DIM, :]
            ms = jnp.mean(q * q, axis=0, keepdims=True)
            cols.append(q * lax.rsqrt(ms + EPS) * qnw_ref[...] * (QK_SCALE * LOG2E))
        qs = jnp.concatenate(cols, axis=1).astype(BF16)
        qp_ref[slot, c, 0:HEAD_DIM, :] = qs
        return (jnp.dot(kc_ref[0, 0, g, 0:rows, :].astype(BF16), qs, preferred_element_type=F32)
                + masks[qt][0][0:rows, :])

    def select_probs(sc, slot, masks, c, want_scores=True):
        qt, g = chains[c]
        _, has_block, valid_s, force = masks[qt]
        rows = sc.shape[0]
        mc = jnp.max(sc, axis=0, keepdims=True)
        ec = jnp.exp2(sc - mc)
        lc = jnp.sum(ec, axis=0, keepdims=True)
        p = ec * (has_block / lc)
        oc_ref[slot, c] = jnp.dot(vct_ref[0, 0, g, :, 0:rows].astype(BF16), p.astype(BF16),
                                  preferred_element_type=F32)
        if not want_scores:
            return None
        ps = p[:, 0:NSA_TQ]
        for r in range(1, NSA_GROUP):
            ps = ps + p[:, r * NSA_TQ:(r + 1) * NSA_TQ]
        ps_hi = ps.astype(BF16)
        ps_lo = (ps - ps_hi.astype(F32)).astype(BF16)
        imp = (jnp.dot(mt_ref[:, 0:rows], ps_hi, preferred_element_type=F32)
               + jnp.dot(mt_ref[:, 0:rows], ps_lo, preferred_element_type=F32))
        return jnp.where(valid_s, jnp.where(force, imp + FORCE_BONUS, imp), NEG)

    def select_rank(scores, valid_s, slot, n_live):
        if n_live <= SLC_TOPK:
            for c, (qt, g) in enumerate(chains):
                qp_ref[slot, c, HEAD_DIM:2 * HEAD_DIM, :] = _tile4(jnp.where(valid_s[qt], 0.0, NEG).astype(BF16))
            return
        sub = lax.broadcasted_iota(jnp.int32, (SUBLANES, NSA_TQ), 0)
        for c, (qt, g) in enumerate(chains):
            score = scores[c]
            blocks = [score[v * SUBLANES:(v + 1) * SUBLANES, :] for v in range(n_live // SUBLANES)]
            ranks = [jnp.zeros((SUBLANES, NSA_TQ), F32) for _ in blocks]
            for i in range(n_live):
                row = score[i:i + 1, :]
                for v, blk in enumerate(blocks):
                    if v * SUBLANES > i:
                        beats = row >= blk
                    elif (v + 1) * SUBLANES <= i:
                        beats = row > blk
                    else:
                        beats = (row > blk) | ((row >= blk) & (sub > i - v * SUBLANES))
                    ranks[v] = ranks[v] + jnp.where(beats, 1.0, 0.0)
            dead = [jnp.full((SUBLANES, NSA_TQ), float(n_slc), F32)] * ((n_slc - n_live) // SUBLANES)
            rank = jnp.concatenate(ranks + dead, axis=0)
            sel = (rank < float(SLC_TOPK)) & valid_s[qt]
            qp_ref[slot, c, HEAD_DIM:2 * HEAD_DIM, :] = _tile4(jnp.where(sel, 0.0, NEG).astype(BF16))

    step_tokens = NSA_QT * NSA_TQ

    def live_blocks(step):
        return ((step + 1) * step_tokens - 1) // SLC_BLOCK + 1

    all_chains = list(range(len(chains)))

    @pl.when(qi == 0)
    def _():
        masks0 = select_masks(0)
        rows0 = min(ncb, -(-((step_tokens - CMP_BLOCK) // CMP_STRIDE + 1) // 16) * 16)
        assert live_blocks(0) <= SLC_TOPK
        for c in all_chains:
            select_probs(select_scores(qt_ref, 0, masks0, c, rows0), 0, masks0, c, want_scores=False)
        select_rank(None, [m[2] for m in masks0], 0, live_blocks(0))

    cur = qi % 2
    nxt = 1 - cur
    t_next = t0 + NSA_QT * NSA_TQ

    kt0, ks0, wbias = [], [], []
    c_minus_r = (lax.broadcasted_iota(jnp.int32, (WIN_KEYS, NSA_TQ), 1)
                 - lax.broadcasted_iota(jnp.int32, (WIN_KEYS, NSA_TQ), 0))
    for qt in tiles:
        kt0.append(jnp.maximum(qi * NSA_QT + qt - WIN_SIZE // NSA_TQ, 0))
        ks0.append(pl.multiple_of(kt0[qt] * NSA_TQ, NSA_TQ))
        delta = (t0 + qt * NSA_TQ - ks0[qt]) + c_minus_r
        in_window = lax.bitcast_convert_type(delta, jnp.uint32) < WIN_SIZE
        wbias.append(_tile4(jnp.where(in_window, 0.0, NEG)))

    def win_scores(c):
        qt, g = chains[c]
        sw = jnp.dot(kw_ref[0, g, pl.ds(ks0[qt], WIN_KEYS), 0:HEAD_DIM], qp_ref[cur, c, 0:HEAD_DIM, :],
                     preferred_element_type=F32) + wbias[qt]
        sw_ref[c] = sw
        return jnp.max(sw, axis=0, keepdims=True)

    def win_attend(c, mw):
        qt, g = chains[c]
        ewb = jnp.exp2(sw_ref[c] - mw).astype(BF16)
        vwt = jnp.concatenate([vwt_ref[0, kt0[qt] + j, g * V_ROWS:(g + 1) * V_ROWS, :]
                               for j in range(WIN_KEYS // LANES)], axis=1)
        ow_aug = jnp.dot(vwt, ewb, preferred_element_type=F32)
        return ow_aug[0:HEAD_DIM, :] * (1.0 / ow_aug[HEAD_DIM:HEAD_DIM + 1, :])

    vt_per_tile = SLC_TK // LANES

    def slc_scores(j, dst_ref, mx_ref, which=all_chains):
        kst = pl.multiple_of(j * SLC_TK, SLC_TK)
        for c in which:
            qt, g = chains[c]
            s = jnp.dot(ks_ref[0, g, pl.ds(kst, SLC_TK), :], qp_ref[cur, c], preferred_element_type=F32)
            dst_ref[c] = s
            mx_ref[c] = jnp.max(s, axis=0, keepdims=True)

    def slc_update(j, src_ref, mx_ref, causal, which=all_chains, visible=SLC_TK):
        if causal:
            band = _tile4(jnp.where(lax.broadcasted_iota(jnp.int32, (NSA_TQ, NSA_TQ), 0)
                                    <= lax.broadcasted_iota(jnp.int32, (NSA_TQ, NSA_TQ), 1), 0.0, NEG))
        for c in which:
            qt, g = chains[c]
            if causal:
                r0 = visible - step_tokens + qt * NSA_TQ
                rows = r0 + NSA_TQ
                s_band = src_ref[c, r0:rows, :] + band
                s = jnp.concatenate([src_ref[c, 0:r0, :], s_band], axis=0) if r0 > 0 else s_band
                tile_max = jnp.max(s, axis=0, keepdims=True)
            else:
                rows = SLC_TK
                s = src_ref[c]
                tile_max = mx_ref[c]
            m_old = m_ref[c]
            m_new = jnp.maximum(m_old, tile_max)
            alpha = jnp.exp2(m_old - m_new)
            eb = jnp.exp2(s - m_new).astype(BF16)
            vt = jnp.concatenate([vst_ref[0, j * vt_per_tile + jj, g * V_ROWS:(g + 1) * V_ROWS, :]
                                  for jj in range(rows // LANES)], axis=1)
            acc_ref[c] = alpha * acc_ref[c] + jnp.dot(vt, eb, preferred_element_type=F32)
            m_ref[c] = m_new

    def before_loop(select_next):
        masks = select_masks(t_next) if select_next else None

        def matmul_stage(c):
            mw = win_scores(c)
            sc = select_scores(qtn_ref, nxt, masks, c) if select_next else None
            slc_scores(0, sa_ref, mxa_ref, [c])
            return sc, mw

        sel_scores = []
        staged = matmul_stage(0)
        for c in all_chains:
            staged_next = matmul_stage(c + 1) if c + 1 < len(chains) else None
            if select_next:
                sel_scores.append(select_probs(staged[0], nxt, masks, c))
            ow_ref[c] = win_attend(c, staged[1])
            staged = staged_next
        if not select_next:
            return
        sel_valid = [m[2] for m in masks]
        need = live_blocks(qi + 1)
        bounds = list(range(SLC_TOPK, n_slc + 1, SUBLANES))
        for lo, hi in zip([0] + bounds[:-1], bounds):
            in_range = (need > lo) if hi == bounds[-1] else ((need > lo) & (need <= hi))
            pl.when(in_range)(functools.partial(select_rank, sel_scores, sel_valid, nxt, hi))

    n_steps = ks_ref.shape[2] // step_tokens
    pl.when(qi + 1 < n_steps)(functools.partial(before_loop, True))
    pl.when(qi + 1 >= n_steps)(functools.partial(before_loop, False))

    m_ref[...] = jnp.full(m_ref.shape, NEG, F32)
    acc_ref[...] = jnp.zeros(acc_ref.shape, F32)
    n_full = t0 // SLC_TK

    def pair(jj, carry):
        j = 2 * jj
        for c in all_chains:
            slc_scores(j + 1, sb_ref, mxb_ref, [c])
            slc_update(j, sa_ref, mxa_ref, False, [c])
        for c in all_chains:
            slc_scores(j + 2, sa_ref, mxa_ref, [c])
            slc_update(j + 1, sb_ref, mxb_ref, False, [c])
        return carry

    lax.fori_loop(0, n_full // 2, pair, 0)

    visible = t0 - n_full * SLC_TK + step_tokens

    def tail(odd, rows):
        if odd:
            for c in all_chains:
                slc_scores(n_full, sb_ref, mxb_ref, [c])
                slc_update(n_full - 1, sa_ref, mxa_ref, False, [c])
            slc_update(n_full, sb_ref, mxb_ref, True, visible=rows)
        else:
            slc_update(n_full, sa_ref, mxa_ref, True, visible=rows)

    for rows in range(step_tokens, SLC_TK + 1, step_tokens):
        for odd in (False, True):
            parity = (n_full % 2 == 1) if odd else (n_full % 2 == 0)
            pl.when((visible == rows) & parity)(functools.partial(tail, odd, rows))

    for qt in tiles:
        gates = _sigmoid(glt_ref[0, qt] + bg_ref[...])
        outs = []
        for g in range(NSA_KV_HEADS):
            c = qt * NSA_KV_HEADS + g
            os_t = acc_ref[c, 0:HEAD_DIM, :] * (1.0 / acc_ref[c, HEAD_DIM:HEAD_DIM + 1, :])
            oc_t = oc_ref[cur, c]
            ow_t = ow_ref[c]
            for r in range(NSA_GROUP):
                sl = slice(r * NSA_TQ, (r + 1) * NSA_TQ)
                row = g * gg + r
                outs.append(gates[row:row + 1, :] * oc_t[:, sl]
                            + gates[row + NSA_GROUP:row + NSA_GROUP + 1, :] * os_t[:, sl]
                            + gates[row + 2 * NSA_GROUP:row + 2 * NSA_GROUP + 1, :] * ow_t[:, sl])
        o_tok = jnp.concatenate(outs, axis=0).T
        rows = pl.ds(qt * NSA_TQ, NSA_TQ)
        ng = ng_ref[0, rows, :]
        o_ref[0, rows, :] = (o_tok * (ng * _sigmoid(ng))).astype(BF16)


def _nsa(nqt, glt, ng3, ks, kw, vt, kc, vct, mt, qnw, bg, B, T):
    nt = T // LANES
    ncb = kc.shape[3]
    G = NSA_KV_HEADS
    NC = NSA_QT * G
    ncols = NSA_GROUP * NSA_TQ
    steps = T // (NSA_QT * NSA_TQ)
    return pl.pallas_call(
        _nsa_kernel,
        grid=(B, steps),
        in_specs=[
            pl.BlockSpec((1, NSA_QT, D_NSA, LANES), lambda b, i: (b, 0, 0, 0)),
            pl.BlockSpec((1, NSA_QT, D_NSA, LANES), lambda b, i: (b, jnp.minimum(i + 1, steps - 1), 0, 0)),
            pl.BlockSpec((1, NSA_QT, 16 * G, LANES), lambda b, i: (b, i, 0, 0)),
            pl.BlockSpec((1, NSA_QT * NSA_TQ, D_NSA), lambda b, i: (b, i, 0)),
            pl.BlockSpec((1, G, T, PAIR), lambda b, i: (b, 0, 0, 0)),
            pl.BlockSpec((1, G, T, PAIR), lambda b, i: (b, 0, 0, 0)),
            pl.BlockSpec((1, nt, G * V_ROWS, LANES), lambda b, i: (b, 0, 0, 0)),
            pl.BlockSpec((1, nt, G * V_ROWS, LANES), lambda b, i: (b, 0, 1, 0)),
            pl.BlockSpec((1, 1, G, ncb, HEAD_DIM), lambda b, i: (b, 0, 0, 0, 0)),
            pl.BlockSpec((1, 1, G, HEAD_DIM, ncb), lambda b, i: (b, 1, 0, 0, 0)),
            pl.BlockSpec(mt.shape, lambda b, i: (0, 0)),
            pl.BlockSpec((HEAD_DIM, 1), lambda b, i: (0, 0)),
            pl.BlockSpec((16 * G, 1), lambda b, i: (0, 0)),
        ],
        out_specs=pl.BlockSpec((1, NSA_QT * NSA_TQ, D_NSA), lambda b, i: (b, i, 0)),
        out_shape=jax.ShapeDtypeStruct((B, T, D_NSA), BF16),
        scratch_shapes=[
            pltpu.VMEM((2, NC, 2 * HEAD_DIM, ncols), BF16),
            pltpu.VMEM((2, NC, HEAD_DIM, ncols), F32),
            pltpu.VMEM((NC, SLC_TK, ncols), F32),
            pltpu.VMEM((NC, SLC_TK, ncols), F32),
            pltpu.VMEM((NC, 1, ncols), F32),
            pltpu.VMEM((NC, 1, ncols), F32),
            pltpu.VMEM((NC, WIN_KEYS, ncols), F32),
            pltpu.VMEM((NC, HEAD_DIM, ncols), F32),
            pltpu.VMEM((NC, 1, ncols), F32),
            pltpu.VMEM((NC, V_ROWS, ncols), F32),
        ],
        compiler_params=pltpu.CompilerParams(dimension_semantics=("parallel", "arbitrary"),
                                             vmem_limit_bytes=VMEM_LIMIT),
        name="nsa",
    )(nqt, nqt, glt, ng3, ks, kw, vt, vt, kc, vct, mt, qnw, bg)


def _out_kernel(x_ref, yr_ref, yn_ref, wr_ref, wn_ref, o_ref):
    o_ref[...] = (x_ref[...]
                  + jnp.dot(yr_ref[...], wr_ref[...], preferred_element_type=F32)
                  + jnp.dot(yn_ref[...], wn_ref[...], preferred_element_type=F32))


def _outproj(x2, yr, yn, wr, wn):
    N = x2.shape[0]
    row = lambda i: (i, 0)
    const = lambda i: (0, 0)
    return pl.pallas_call(
        _out_kernel,
        grid=(N // OUT_TM,),
        in_specs=[pl.BlockSpec((OUT_TM, D_MODEL), row), pl.BlockSpec((OUT_TM, D_RET), row),
                  pl.BlockSpec((OUT_TM, D_NSA), row), pl.BlockSpec(wr.shape, const), pl.BlockSpec(wn.shape, const)],
        out_specs=pl.BlockSpec((OUT_TM, D_MODEL), row),
        out_shape=jax.ShapeDtypeStruct((N, D_MODEL), F32),
        compiler_params=pltpu.CompilerParams(dimension_semantics=("parallel",), vmem_limit_bytes=VMEM_LIMIT),
        name="outproj",
    )(x2, yr, yn, wr, wn)


@functools.lru_cache(maxsize=None)
def _tables(T):
    half = HEAD_DIM // 2
    inv = ROPE_THETA ** (-np.arange(half, dtype=np.float64) / half)
    ang = np.arange(T, dtype=np.float64)[:, None] * inv[None, :]
    cos = np.concatenate([np.cos(ang)] * 4, axis=1).astype(np.float32)
    sin = np.concatenate([-np.sin(ang), -np.sin(ang), np.sin(ang), np.sin(ang)], axis=1).astype(np.float32)

    C = RET_CHUNK
    log_g = np.log1p(-np.exp2(-5.0 - np.arange(RET_HEADS, dtype=np.float64)))
    pos = np.arange(C, dtype=np.float64)
    diff = pos[:, None] - pos[None, :]
    decay = np.where(diff >= 0, np.exp(log_g[:, None, None] * np.maximum(diff, 0.0)), 0.0)
    zeta = np.exp(log_g[:, None] * (C - 1.0 - pos))
    xi = np.exp(log_g[:, None] * (pos + 1.0))
    g_chunk = np.exp(log_g * C)
    npair = RET_HEADS // 2

    def pair_lanes(a):
        return np.repeat(a.reshape(npair, 2, C).transpose(0, 2, 1), HEAD_DIM, axis=2).astype(np.float32)

    dec = decay.reshape(npair, 2, C, C).transpose(0, 2, 1, 3).reshape(npair, C, 2 * C).astype(np.float32)
    gch = np.repeat(g_chunk.reshape(npair, 1, 2), HEAD_DIM, axis=2).astype(np.float32)

    n_cmp = (T - CMP_BLOCK) // CMP_STRIDE + 1
    ncb = T // CMP_STRIDE
    p = np.arange(n_cmp)[:, None] * CMP_STRIDE + np.arange(CMP_BLOCK)[None, :]
    blk = p // SLC_BLOCK
    M = (blk[:, :, None] == np.arange(T // SLC_BLOCK)[None, None, :]).mean(axis=1)
    mt = np.zeros((T // SLC_BLOCK, ncb), np.float32)
    mt[:, :n_cmp] = M.T
    return cos, sin, dec, pair_lanes(zeta), pair_lanes(xi), gch, mt


def kernel(x, norm_w, w_in, ret_norm_w, q_norm_w, k_norm_cmp, k_norm_slc, k_norm_win, cmp_pos_k, cmp_w1_k, cmp_w2_k,
           cmp_pos_v, cmp_w1_v, cmp_w2_v, b_gate, w_out):
    B, T, D = x.shape
    depth = norm_w.shape[0]
    cos, sin, dec, zeta, xi, gch, mt = _tables(T)
    ncb = T // CMP_STRIDE
    half = CMP_STRIDE * HEAD_DIM
    gate_src = np.zeros((NSA_KV_HEADS, 16), np.int32)
    gate_ok = np.zeros((NSA_KV_HEADS, 16), bool)
    for g in range(NSA_KV_HEADS):
        for br in range(N_BRANCH):
            for r in range(NSA_GROUP):
                gate_src[g, br * NSA_GROUP + r] = br * NSA_HEADS + g * NSA_GROUP + r
                gate_ok[g, br * NSA_GROUP + r] = True
    gate_src = gate_src.reshape(-1)
    gate_ok = gate_ok.reshape(-1)

    x2 = x.reshape(B * T, D)
    for layer in range(depth):
        w = w_in[layer].astype(BF16)
        o_ng = 4 * D_RET + D_NSA
        o_kv = o_ng + D_NSA
        quarter = HEAD_DIM // 2
        pair_perm = np.concatenate([np.arange(quarter), HEAD_DIM + np.arange(quarter),
                                    quarter + np.arange(quarter), HEAD_DIM + quarter + np.arange(quarter)])
        qk_perm = np.concatenate([p * PAIR + pair_perm for p in range(RET_HEADS // 2)])
        w_ret = jnp.concatenate([w[:, :D_RET][:, qk_perm], w[:, D_RET:2 * D_RET][:, qk_perm],
                                 w[:, 2 * D_RET:4 * D_RET]], axis=1)
        wt_q = w[:, 4 * D_RET:o_ng].T
        w_ng = w[:, o_ng:o_kv]
        w_ckv = w[:, o_kv:o_kv + 2 * D_KV]
        w_skwk = jnp.concatenate([w[:, o_kv + 2 * D_KV:o_kv + 3 * D_KV], w[:, o_kv + 4 * D_KV:o_kv + 5 * D_KV]], axis=1)
        wt_v = jnp.concatenate([w[:, o_kv + 3 * D_KV:o_kv + 4 * D_KV], w[:, o_kv + 5 * D_KV:o_kv + 6 * D_KV]], axis=1).T
        w_gl = w[:, o_kv + 6 * D_KV:]
        wt_g = jnp.where(gate_ok[:, None], w_gl.T[gate_src], jnp.zeros((), BF16))
        bg = jnp.where(gate_ok, b_gate[layer][gate_src], 0.0).reshape(-1, 1)

        knw = jnp.stack([jnp.tile(k_norm_slc[layer], 2), jnp.tile(k_norm_win[layer], 2)])
        ret, ng, xc, ks, kw, nqt, vt, glt = _proj(x2, norm_w[layer].reshape(1, D), w_ret, w_ng, w_ckv, w_skwk,
                                                  wt_q, wt_v, wt_g, knw, B, T)

        nw_pair = ret_norm_w[layer].reshape(RET_HEADS // 2, 1, PAIR)
        y_ret = _retention(ret.reshape(B, T, 4 * D_RET), jnp.asarray(cos), jnp.asarray(sin), jnp.asarray(dec),
                           jnp.asarray(zeta), jnp.asarray(xi), jnp.asarray(gch), nw_pair, B, T)

        pos =jnp.stack([cmp_pos_k[layer], cmp_pos_v[layer]]).reshape(2, 2, half)
        w1 = jnp.stack([cmp_w1_k[layer], cmp_w1_v[layer]]).astype(BF16)
        w2 = jnp.stack([cmp_w2_k[layer], cmp_w2_v[layer]]).astype(BF16)
        w2t = jnp.swapaxes(w2, 1, 2)
        kc, vct = _compress(xc, pos, w1, w2, w2t, k_norm_cmp[layer].reshape(1, HEAD_DIM), B, ncb)

        y_nsa = _nsa(nqt, glt, ng.reshape(B, T, D_NSA), ks, kw, vt, kc, vct, jnp.asarray(mt).astype(BF16),
                     q_norm_w[layer].reshape(HEAD_DIM, 1), bg, B, T)

        wo = w_out[layer].astype(BF16)
        x2 = _outproj(x2, y_ret.reshape(B * T, D_RET), y_nsa.reshape(B * T, D_NSA), wo[:D_RET], wo[D_RET:])
    return x2.reshape(B, T, D)
```

```python
import functools

import numpy as np
import jax
import jax.numpy as jnp
from jax import lax
from jax.experimental import pallas as pl
from jax.experimental.pallas import tpu as pltpu

F32 = jnp.float32
BF16 = jnp.bfloat16

D_MODEL = 1024
HEAD_DIM = 64
HEAD_SHIFT = 6
RET_HEADS = 8
NSA_HEADS = 8
NSA_KV_HEADS = 2
NSA_GROUP = NSA_HEADS // NSA_KV_HEADS
D_RET = RET_HEADS * HEAD_DIM
D_NSA = NSA_HEADS * HEAD_DIM
D_KV = NSA_KV_HEADS * HEAD_DIM
N_BRANCH = 3
RET_CHUNK = 128
ROPE_THETA = 10000.0
CMP_BLOCK = 32
CMP_STRIDE = 16
CMP_HIDDEN = 256
SLC_BLOCK = 64
SLC_SHIFT = 6
SLC_TOPK = 16
WIN_SIZE = 512
EPS = 1e-6
NEG = -1e30
FORCE_BONUS = 1e4
QK_SCALE = HEAD_DIM ** -0.5
LOG2E = 1.4426950408889634
V_ROWS = HEAD_DIM + 16

LANES = 128
SUBLANES = 8
PAIR = 2 * HEAD_DIM
VMEM_LIMIT = 48 * 1024 * 1024

PROJ_TM = 512
RET_TC = 1024
NSA_TQ = 128
NSA_QT = 2
SLC_TK = 512
WIN_KEYS = WIN_SIZE + NSA_TQ
OUT_TM = 1024

NT_DIMS = (((1,), (1,)), ((), ()))
TN_DIMS = (((0,), (0,)), ((), ()))


def _sigmoid(x):
    return 1.0 / (1.0 + jnp.exp(-x))


def _tile4(a):
    return jnp.concatenate([a, a, a, a], axis=1)


def _proj_kernel(steps_per_batch, x_ref, nw_ref, w_ret_ref, w_ng_ref, w_ckv_ref, w_skwk_ref, wt_q_ref, wt_v_ref,
                 wt_g_ref, knw_ref, ret_ref, ng_ref, xc_ref, ks_ref, kw_ref, nqt_ref, vt_ref, glt_ref, ckv_scr):
    x = x_ref[...]
    ms = jnp.mean(x * x, axis=-1, keepdims=True)
    h = (x * lax.rsqrt(ms + EPS) * nw_ref[...]).astype(BF16)
    ret_ref[...] = jnp.dot(h, w_ret_ref[...], preferred_element_type=F32)
    ng_ref[...] = jnp.dot(h, w_ng_ref[...], preferred_element_type=F32)

    ckv = jnp.dot(h, w_ckv_ref[...], preferred_element_type=F32)
    for half in range(2 * D_KV // LANES):
        ckv_scr[half] = ckv[:, half * LANES:(half + 1) * LANES]
    for l in range(CMP_STRIDE):
        for half in range(2 * D_KV // LANES):
            rows = ckv_scr[half, pl.ds(l, PROJ_TM // CMP_STRIDE, stride=CMP_STRIDE), :]
            for s in range(LANES // HEAD_DIM):
                xc_ref[0, half * (LANES // HEAD_DIM) + s, :, l * HEAD_DIM:(l + 1) * HEAD_DIM] = (
                    rows[:, s * HEAD_DIM:(s + 1) * HEAD_DIM])

    skwk = jnp.dot(h, w_skwk_ref[...], preferred_element_type=F32)
    lane = lax.broadcasted_iota(jnp.int32, (1, PAIR), 1)
    head0 = lane < HEAD_DIM
    m0 = jnp.where(head0, 1.0, 0.0)
    m1 = 1.0 - m0
    t_start = (pl.program_id(0) % steps_per_batch) * PROJ_TM
    tok = t_start + lax.broadcasted_iota(jnp.int32, (PROJ_TM, PAIR), 0)
    col = lax.broadcasted_iota(jnp.int32, (PROJ_TM, PAIR), 1)
    indicator = jnp.where((tok >> SLC_SHIFT) == col - HEAD_DIM, 1.0, 0.0)

    def pair_normed(t, w):
        t2 = t * t
        ms0 = jnp.sum(t2 * m0, axis=-1, keepdims=True)
        ms1 = jnp.sum(t2 * m1, axis=-1, keepdims=True)
        return t * lax.rsqrt(jnp.where(head0, ms0, ms1) * (1.0 / HEAD_DIM) + EPS) * w

    ns = pair_normed(skwk[:, :PAIR], knw_ref[0:1, :])
    nwin = pair_normed(skwk[:, PAIR:], knw_ref[1:2, :])
    for g in range(NSA_KV_HEADS):
        s_g = ns if g == 0 else pltpu.roll(ns, HEAD_DIM, 1)
        w_g = nwin if g == 0 else pltpu.roll(nwin, HEAD_DIM, 1)
        ks_ref[0, g] = jnp.where(head0, s_g, indicator).astype(BF16)
        kw_ref[0, g] = jnp.where(head0, w_g, 0.0).astype(BF16)

    qt = lax.dot_general(wt_q_ref[...], h, NT_DIMS, preferred_element_type=F32)
    vt = lax.dot_general(wt_v_ref[...], h, NT_DIMS, preferred_element_type=F32)
    gt = lax.dot_general(wt_g_ref[...], h, NT_DIMS, preferred_element_type=F32)
    pad_row = lax.broadcasted_iota(jnp.int32, (V_ROWS - HEAD_DIM, LANES), 0)
    ones_pad = jnp.where(pad_row == 0, 1.0, 0.0).astype(BF16)
    for j in range(PROJ_TM // LANES):
        sl = slice(j * LANES, (j + 1) * LANES)
        nqt_ref[0, j] = qt[:, sl]
        for blk in range(2 * NSA_KV_HEADS):
            vt_ref[0, j, blk * V_ROWS:blk * V_ROWS + HEAD_DIM, :] = (
                vt[blk * HEAD_DIM:(blk + 1) * HEAD_DIM, sl].astype(BF16))
            vt_ref[0, j, blk * V_ROWS + HEAD_DIM:(blk + 1) * V_ROWS, :] = ones_pad
        glt_ref[0, j] = gt[:, sl]


def _proj(x2, nw, w_ret, w_ng, w_ckv, w_skwk, wt_q, wt_v, wt_g, knw, B, T):
    N = B * T
    tpb = T // PROJ_TM
    sub = PROJ_TM // LANES
    nt = T // LANES
    const = lambda i: (0, 0)
    row = lambda i: (i, 0)
    trn = lambda i: (i // tpb, i % tpb, 0, 0)
    tokm = lambda i: (i // tpb, 0, i % tpb, 0)
    return pl.pallas_call(
        functools.partial(_proj_kernel, tpb),
        grid=(N // PROJ_TM,),
        in_specs=[
            pl.BlockSpec((PROJ_TM, D_MODEL), row),
            pl.BlockSpec((1, D_MODEL), const),
            pl.BlockSpec(w_ret.shape, const),
            pl.BlockSpec(w_ng.shape, const),
            pl.BlockSpec(w_ckv.shape, const),
            pl.BlockSpec(w_skwk.shape, const),
            pl.BlockSpec(wt_q.shape, const),
            pl.BlockSpec(wt_v.shape, const),
            pl.BlockSpec(wt_g.shape, const),
            pl.BlockSpec(knw.shape, const),
        ],
        out_specs=[
            pl.BlockSpec((PROJ_TM, 4 * D_RET), row),
            pl.BlockSpec((PROJ_TM, D_NSA), row),
            pl.BlockSpec((1, 2 * NSA_KV_HEADS, PROJ_TM // CMP_STRIDE, CMP_STRIDE * HEAD_DIM), tokm),
            pl.BlockSpec((1, NSA_KV_HEADS, PROJ_TM, PAIR), tokm),
            pl.BlockSpec((1, NSA_KV_HEADS, PROJ_TM, PAIR), tokm),
            pl.BlockSpec((1, sub, D_NSA, LANES), trn),
            pl.BlockSpec((1, sub, 2 * NSA_KV_HEADS * V_ROWS, LANES), trn),
            pl.BlockSpec((1, sub, 32, LANES), trn),
        ],
        out_shape=[
            jax.ShapeDtypeStruct((N, 4 * D_RET), F32),
            jax.ShapeDtypeStruct((N, D_NSA), F32),
            jax.ShapeDtypeStruct((B, 2 * NSA_KV_HEADS, T // CMP_STRIDE, CMP_STRIDE * HEAD_DIM), F32),
            jax.ShapeDtypeStruct((B, NSA_KV_HEADS, T, PAIR), BF16),
            jax.ShapeDtypeStruct((B, NSA_KV_HEADS, T, PAIR), BF16),
            jax.ShapeDtypeStruct((B, nt, D_NSA, LANES), F32),
            jax.ShapeDtypeStruct((B, nt, 2 * NSA_KV_HEADS * V_ROWS, LANES), BF16),
            jax.ShapeDtypeStruct((B, nt, 32, LANES), F32),
        ],
        scratch_shapes=[pltpu.VMEM((2 * D_KV // LANES, PROJ_TM, LANES), F32)],
        compiler_params=pltpu.CompilerParams(dimension_semantics=("parallel",), vmem_limit_bytes=VMEM_LIMIT),
        name="proj",
    )(x2, nw, w_ret, w_ng, w_ckv, w_skwk, wt_q, wt_v, wt_g, knw)


def _ret_kernel(qkvg_ref, cos_ref, sin_ref, dec_ref, zeta_ref, xi_ref, gch_ref, nw_ref, o_ref, state_ref):
    @pl.when(pl.program_id(1) == 0)
    def _():
        state_ref[...] = jnp.zeros_like(state_ref)

    lane = lax.broadcasted_iota(jnp.int32, (1, PAIR), 1)
    q_head = (lane >> (HEAD_SHIFT - 1)) & 1
    v_head = lane >> HEAD_SHIFT
    q_mask = [jnp.where(q_head == h, 1.0, 0.0).astype(BF16) for h in (0, 1)]
    v_mask = [jnp.where(v_head == h, 1.0, 0.0).astype(BF16) for h in (0, 1)]
    row_qh = (lax.broadcasted_iota(jnp.int32, (PAIR, PAIR), 0) >> (HEAD_SHIFT - 1)) & 1
    row_vh = lax.broadcasted_iota(jnp.int32, (PAIR, PAIR), 0) >> HEAD_SHIFT
    col_vh = lax.broadcasted_iota(jnp.int32, (PAIR, PAIR), 1) >> HEAD_SHIFT
    same_head_kv = jnp.where(row_qh == col_vh, 1.0, 0.0)
    head_mean = jnp.where(row_vh == col_vh, 1.0 / HEAD_DIM, 0.0).astype(BF16)

    pairs = range(RET_HEADS // 2)
    mean2 = jnp.concatenate([head_mean, head_mean], axis=0)
    tok0 = pl.program_id(1) * RET_TC
    for c in range(RET_TC // RET_CHUNK):
        sl = pl.ds(c * RET_CHUNK, RET_CHUNK)
        pos = pl.ds(pl.multiple_of(tok0 + c * RET_CHUNK, RET_CHUNK), RET_CHUNK)
        cos = cos_ref[pos, :]
        sin = sin_ref[pos, :]
        cols = [slice(p * PAIR, (p + 1) * PAIR) for p in pairs]
        qb, kb, vb, vzb = [], [], [], []
        for p in pairs:
            q = qkvg_ref[0, sl, pl.ds(p * PAIR, PAIR)]
            k = qkvg_ref[0, sl, pl.ds(D_RET + p * PAIR, PAIR)]
            v = qkvg_ref[0, sl, pl.ds(2 * D_RET + p * PAIR, PAIR)]
            qb.append((q * cos + pltpu.roll(q, HEAD_DIM, 1) * sin).astype(BF16))
            kb.append(((k * cos + pltpu.roll(k, HEAD_DIM, 1) * sin) * QK_SCALE).astype(BF16))
            vb.append(v.astype(BF16))
            vzb.append((v * zeta_ref[p]).astype(BF16))
        states = [state_ref[p] for p in pairs]
        kk = [jnp.concatenate([kb[p] * q_mask[h] for h in (0, 1)], axis=0) for p in pairs]
        s = [lax.dot_general(qb[p], kk[p], NT_DIMS, preferred_element_type=F32) for p in pairs]
        o_cross = [jnp.dot(qb[p], states[p].astype(BF16), preferred_element_type=F32) for p in pairs]
        kv = [lax.dot_general(kb[p], vzb[p], TN_DIMS, preferred_element_type=F32) for p in pairs]
        sb = [(s[p] * dec_ref[p]).astype(BF16) for p in pairs]
        vv = [jnp.concatenate([vb[p] * v_mask[h] for h in (0, 1)], axis=0) for p in pairs]
        o = [jnp.dot(sb[p], vv[p], preferred_element_type=F32) + o_cross[p] * xi_ref[p] for p in pairs]
        for p in pairs:
            state_ref[p] = states[p] * gch_ref[p] + kv[p] * same_head_kv
        o2 = [o[p] * o[p] for p in pairs]
        o2_hi = [o2[p].astype(BF16) for p in pairs]
        o2_hl = [jnp.concatenate([o2_hi[p], (o2[p] - o2_hi[p].astype(F32)).astype(BF16)], axis=1) for p in pairs]
        ms = [jnp.dot(o2_hl[p], mean2, preferred_element_type=F32) for p in pairs]
        for p in pairs:
            g = qkvg_ref[0, sl, pl.ds(3 * D_RET + p * PAIR, PAIR)]
            y = o[p] * lax.rsqrt(ms[p] + EPS) * nw_ref[p]
            o_ref[0, sl, cols[p]] = (y * (g * _sigmoid(g))).astype(BF16)


def _retention(ret3, cos, sin, dec, zeta, xi, gch, nw, B, T):
    npair = RET_HEADS // 2
    whole = lambda a: pl.BlockSpec(a.shape, lambda b, i: (0,) * a.ndim)
    return pl.pallas_call(
        _ret_kernel,
        grid=(B, T // RET_TC),
        in_specs=[pl.BlockSpec((1, RET_TC, 4 * D_RET), lambda b, i: (b, i, 0)), whole(cos), whole(sin), whole(dec),
                  whole(zeta), whole(xi), whole(gch), whole(nw)],
        out_specs=pl.BlockSpec((1, RET_TC, D_RET), lambda b, i: (b, i, 0)),
        out_shape=jax.ShapeDtypeStruct((B, T, D_RET), BF16),
        scratch_shapes=[pltpu.VMEM((npair, PAIR, PAIR), F32)],
        compiler_params=pltpu.CompilerParams(dimension_semantics=("parallel", "arbitrary"),
                                             vmem_limit_bytes=VMEM_LIMIT),
        name="retention",
    )(ret3, cos, sin, dec, zeta, xi, gch, nw)


def _cmp_kernel(x_ref, pos_ref, w1_ref, w2_ref, w2t_ref, knw_ref, o_ref, ot_ref):
    is_key = pl.program_id(1) == 0
    half = CMP_STRIDE * HEAD_DIM
    for g in range(NSA_KV_HEADS):
        x = x_ref[0, g]
        a = jnp.dot((x + pos_ref[0, 0:1, :]).astype(BF16), w1_ref[0, :half, :], preferred_element_type=F32)
        b = jnp.dot((x + pos_ref[0, 1:2, :]).astype(BF16), w1_ref[0, half:, :], preferred_element_type=F32)
        hid = a + pltpu.roll(b, b.shape[0] - 1, 0)
        hid = (hid * _sigmoid(hid)).astype(BF16)
        out = jnp.dot(hid, w2_ref[0], preferred_element_type=F32)
        ms = jnp.mean(out * out, axis=-1, keepdims=True)
        normed = out * lax.rsqrt(ms + EPS) * knw_ref[...]
        o_ref[0, 0, g] = jnp.where(is_key, normed, out)
        ot_ref[0, 0, g] = lax.dot_general(w2t_ref[0], hid, NT_DIMS, preferred_element_type=F32)


def _compress(xc, pos, w1, w2, w2t, knw, B, ncb):
    return pl.pallas_call(
        _cmp_kernel,
        grid=(B, 2),
        in_specs=[
            pl.BlockSpec((1, NSA_KV_HEADS, ncb, CMP_STRIDE * HEAD_DIM), lambda b, s: (b, s, 0, 0)),
            pl.BlockSpec((1, 2, CMP_STRIDE * HEAD_DIM), lambda b, s: (s, 0, 0)),
            pl.BlockSpec((1, CMP_BLOCK * HEAD_DIM, CMP_HIDDEN), lambda b, s: (s, 0, 0)),
            pl.BlockSpec((1, CMP_HIDDEN, HEAD_DIM), lambda b, s: (s, 0, 0)),
            pl.BlockSpec((1, HEAD_DIM, CMP_HIDDEN), lambda b, s: (s, 0, 0)),
            pl.BlockSpec((1, HEAD_DIM), lambda b, s: (0, 0)),
        ],
        out_specs=[
            pl.BlockSpec((1, 1, NSA_KV_HEADS, ncb, HEAD_DIM), lambda b, s: (b, s, 0, 0, 0)),
            pl.BlockSpec((1, 1, NSA_KV_HEADS, HEAD_DIM, ncb), lambda b, s: (b, s, 0, 0, 0)),
        ],
        out_shape=[
            jax.ShapeDtypeStruct((B, 2, NSA_KV_HEADS, ncb, HEAD_DIM), F32),
            jax.ShapeDtypeStruct((B, 2, NSA_KV_HEADS, HEAD_DIM, ncb), F32),
        ],
        compiler_params=pltpu.CompilerParams(dimension_semantics=("parallel", "parallel"),
                                             vmem_limit_bytes=VMEM_LIMIT),
        name="compress",
    )(xc, pos, w1, w2, w2t, knw)


def _nsa_kernel(qt_ref, qtn_ref, glt_ref, ng_ref, ks_ref, kw_ref, vst_ref, vwt_ref, kc_ref, vct_ref, mt_ref, qnw_ref,
                bg_ref, o_ref, qp_ref, oc_ref, sa_ref, sb_ref, mxa_ref, mxb_ref, sw_ref, ow_ref, m_ref, acc_ref):
    qi = pl.program_id(1)
    t0 = qi * (NSA_QT * NSA_TQ)
    tiles = range(NSA_QT)
    chains = [(qt, g) for qt in tiles for g in range(NSA_KV_HEADS)]
    gq = NSA_GROUP * HEAD_DIM
    gg = 16
    ncb = kc_ref.shape[3]
    n_slc = mt_ref.shape[0]

    def select_masks(base_t0):
        n_idx = lax.broadcasted_iota(jnp.int32, (ncb, NSA_TQ), 0)
        jb = lax.broadcasted_iota(jnp.int32, (n_slc, NSA_TQ), 0)
        out = []
        for qt in tiles:
            tok = base_t0 + qt * NSA_TQ
            tok_c = tok + lax.broadcasted_iota(jnp.int32, (ncb, NSA_TQ), 1)
            cbias = _tile4(jnp.where((n_idx * CMP_STRIDE + (CMP_BLOCK - 1)) <= tok_c, 0.0, NEG))
            tok_row = tok + lax.broadcasted_iota(jnp.int32, (1, NSA_TQ), 1)
            has_block = _tile4(jnp.where(tok_row >= CMP_BLOCK - 1, 1.0, 0.0))
            tok_s = tok + lax.broadcasted_iota(jnp.int32, (n_slc, NSA_TQ), 1)
            valid_s = jb * SLC_BLOCK <= tok_s
            force = (jb == (tok_s >> SLC_SHIFT)) | (jb == 0)
            out.append((cbias, has_block, valid_s, force))
        return out

    def select_scores(src_ref, slot, masks, c, rows=ncb):
        qt, g = chains[c]
        cols = []
        for r in range(NSA_GROUP):
            q = src_ref[0, qt, g * gq + r * HEAD_DIM:g * gq + (r + 1) * HEAD_DIM, :]
            ms = jnp.mean(q * q, axis=0, keepdims=True)
            cols.append(q * lax.rsqrt(ms + EPS) * qnw_ref[...] * (QK_SCALE * LOG2E))
        qs = jnp.concatenate(cols, axis=1).astype(BF16)
        qp_ref[slot, c, 0:HEAD_DIM, :] = qs
        return (jnp.dot(kc_ref[0, 0, g, 0:rows, :].astype(BF16), qs, preferred_element_type=F32)
                + masks[qt][0][0:rows, :])

    def select_probs(sc, slot, masks, c, want_scores=True):
        qt, g = chains[c]
        _, has_block, valid_s, force = masks[qt]
        rows = sc.shape[0]
        mc = jnp.max(sc, axis=0, keepdims=True)
        ec = jnp.exp2(sc - mc)
        lc = jnp.sum(ec, axis=0, keepdims=True)
        p = ec * (has_block / lc)
        oc_ref[slot, c] = jnp.dot(vct_ref[0, 0, g, :, 0:rows].astype(BF16), p.astype(BF16),
                                  preferred_element_type=F32)
        if not want_scores:
            return None
        ps = p[:, 0:NSA_TQ]
        for r in range(1, NSA_GROUP):
            ps = ps + p[:, r * NSA_TQ:(r + 1) * NSA_TQ]
        ps_hi = ps.astype(BF16)
        ps_lo = (ps - ps_hi.astype(F32)).astype(BF16)
        imp = (jnp.dot(mt_ref[:, 0:rows], ps_hi, preferred_element_type=F32)
               + jnp.dot(mt_ref[:, 0:rows], ps_lo, preferred_element_type=F32))
        return jnp.where(valid_s, jnp.where(force, imp + FORCE_BONUS, imp), NEG)

    def select_rank(scores, valid_s, slot, n_live):
        if n_live <= SLC_TOPK:
            for c, (qt, g) in enumerate(chains):
                qp_ref[slot, c, HEAD_DIM:2 * HEAD_DIM, :] = _tile4(jnp.where(valid_s[qt], 0.0, NEG).astype(BF16))
            return
        sub = lax.broadcasted_iota(jnp.int32, (SUBLANES, NSA_TQ), 0)
        for c, (qt, g) in enumerate(chains):
            score = scores[c]
            blocks = [score[v * SUBLANES:(v + 1) * SUBLANES, :] for v in range(n_live // SUBLANES)]
            ranks = [jnp.zeros((SUBLANES, NSA_TQ), F32) for _ in blocks]
            for i in range(n_live):
                row = score[i:i + 1, :]
                for v, blk in enumerate(blocks):
                    if v * SUBLANES > i:
                        beats = row >= blk
                    elif (v + 1) * SUBLANES <= i:
                        beats = row > blk
                    else:
                        beats = (row > blk) | ((row >= blk) & (sub > i - v * SUBLANES))
                    ranks[v] = ranks[v] + jnp.where(beats, 1.0, 0.0)
            dead = [jnp.full((SUBLANES, NSA_TQ), float(n_slc), F32)] * ((n_slc - n_live) // SUBLANES)
            rank = jnp.concatenate(ranks + dead, axis=0)
            sel = (rank < float(SLC_TOPK)) & valid_s[qt]
            qp_ref[slot, c, HEAD_DIM:2 * HEAD_DIM, :] = _tile4(jnp.where(sel, 0.0, NEG).astype(BF16))

    step_tokens = NSA_QT * NSA_TQ

    def live_blocks(step):
        return ((step + 1) * step_tokens - 1) // SLC_BLOCK + 1

    all_chains = list(range(len(chains)))

    @pl.when(qi == 0)
    def _():
        masks0 = select_masks(0)
        rows0 = min(ncb, -(-((step_tokens - CMP_BLOCK) // CMP_STRIDE + 1) // 16) * 16)
        assert live_blocks(0) <= SLC_TOPK
        for c in all_chains:
            select_probs(select_scores(qt_ref, 0, masks0, c, rows0), 0, masks0, c, want_scores=False)
        select_rank(None, [m[2] for m in masks0], 0, live_blocks(0))

    cur = qi % 2
    nxt = 1 - cur
    t_next = t0 + NSA_QT * NSA_TQ

    kt0, ks0, wbias = [], [], []
    c_minus_r = (lax.broadcasted_iota(jnp.int32, (WIN_KEYS, NSA_TQ), 1)
                 - lax.broadcasted_iota(jnp.int32, (WIN_KEYS, NSA_TQ), 0))
    for qt in tiles:
        kt0.append(jnp.maximum(qi * NSA_QT + qt - WIN_SIZE // NSA_TQ, 0))
        ks0.append(pl.multiple_of(kt0[qt] * NSA_TQ, NSA_TQ))
        delta = (t0 + qt * NSA_TQ - ks0[qt]) + c_minus_r
        in_window = lax.bitcast_convert_type(delta, jnp.uint32) < WIN_SIZE
        wbias.append(_tile4(jnp.where(in_window, 0.0, NEG)))

    def win_scores(c):
        qt, g = chains[c]
        sw = jnp.dot(kw_ref[0, g, pl.ds(ks0[qt], WIN_KEYS), 0:HEAD_DIM], qp_ref[cur, c, 0:HEAD_DIM, :],
                     preferred_element_type=F32) + wbias[qt]
        sw_ref[c] = sw
        return jnp.max(sw, axis=0, keepdims=True)

    def win_attend(c, mw):
        qt, g = chains[c]
        ewb = jnp.exp2(sw_ref[c] - mw).astype(BF16)
        vwt = jnp.concatenate([vwt_ref[0, kt0[qt] + j, g * V_ROWS:(g + 1) * V_ROWS, :]
                               for j in range(WIN_KEYS // LANES)], axis=1)
        ow_aug = jnp.dot(vwt, ewb, preferred_element_type=F32)
        return ow_aug[0:HEAD_DIM, :] * (1.0 / ow_aug[HEAD_DIM:HEAD_DIM + 1, :])

    vt_per_tile = SLC_TK // LANES

    def slc_scores(j, dst_ref, mx_ref, which=all_chains):
        kst = pl.multiple_of(j * SLC_TK, SLC_TK)
        for c in which:
            qt, g = chains[c]
            s = jnp.dot(ks_ref[0, g, pl.ds(kst, SLC_TK), :], qp_ref[cur, c], preferred_element_type=F32)
            dst_ref[c] = s
            mx_ref[c] = jnp.max(s, axis=0, keepdims=True)

    def slc_update(j, src_ref, mx_ref, causal, which=all_chains, visible=SLC_TK):
        if causal:
            band = _tile4(jnp.where(lax.broadcasted_iota(jnp.int32, (NSA_TQ, NSA_TQ), 0)
                                    <= lax.broadcasted_iota(jnp.int32, (NSA_TQ, NSA_TQ), 1), 0.0, NEG))
        for c in which:
            qt, g = chains[c]
            if causal:
                r0 = visible - step_tokens + qt * NSA_TQ
                rows = r0 + NSA_TQ
                s_band = src_ref[c, r0:rows, :] + band
                s = jnp.concatenate([src_ref[c, 0:r0, :], s_band], axis=0) if r0 > 0 else s_band
                tile_max = jnp.max(s, axis=0, keepdims=True)
            else:
                rows = SLC_TK
                s = src_ref[c]
                tile_max = mx_ref[c]
            m_old = m_ref[c]
            m_new = jnp.maximum(m_old, tile_max)
            alpha = jnp.exp2(m_old - m_new)
            eb = jnp.exp2(s - m_new).astype(BF16)
            vt = jnp.concatenate([vst_ref[0, j * vt_per_tile + jj, g * V_ROWS:(g + 1) * V_ROWS, :]
                                  for jj in range(rows // LANES)], axis=1)
            acc_ref[c] = alpha * acc_ref[c] + jnp.dot(vt, eb, preferred_element_type=F32)
            m_ref[c] = m_new

    rank_bounds = list(range(SLC_TOPK, n_slc + 1, SUBLANES))

    def before_loop(select_next, rows, steps):
        masks = select_masks(t_next) if select_next else None

        def matmul_stage(c):
            mw = win_scores(c)
            sc = select_scores(qtn_ref, nxt, masks, c, rows) if select_next else None
            slc_scores(0, sa_ref, mxa_ref, [c])
            return sc, mw

        sel_scores = []
        staged = matmul_stage(0)
        for c in all_chains:
            staged_next = matmul_stage(c + 1) if c + 1 < len(chains) else None
            if select_next:
                sel_scores.append(select_probs(staged[0], nxt, masks, c))
            ow_ref[c] = win_attend(c, staged[1])
            staged = staged_next
        if not select_next:
            return
        sel_valid = [m[2] for m in masks]
        need = live_blocks(qi + 1)
        used = {min(b for b in rank_bounds if b >= min(live_blocks(q + 1), n_slc)) for q in steps}
        for lo, hi in zip([0] + rank_bounds[:-1], rank_bounds):
            if hi in used:
                in_range = (need > lo) if hi == rank_bounds[-1] else ((need > lo) & (need <= hi))
                pl.when(in_range)(functools.partial(select_rank, sel_scores, sel_valid, nxt, hi))

    n_steps = ks_ref.shape[2] // step_tokens
    row_options = list(range(SLC_BLOCK, ncb + 1, SLC_BLOCK))
    by_rows = {}
    for q in range(n_steps - 1):
        visible_blocks = ((q + 2) * step_tokens - CMP_BLOCK) // CMP_STRIDE + 1
        by_rows.setdefault(min(r for r in row_options if r >= min(visible_blocks, ncb)), []).append(q)
    for rows, steps in by_rows.items():
        pl.when((qi >= steps[0]) & (qi <= steps[-1]))(functools.partial(before_loop, True, rows, steps))
    pl.when(qi + 1 >= n_steps)(functools.partial(before_loop, False, ncb, []))

    m_ref[...] = jnp.full(m_ref.shape, NEG, F32)
    acc_ref[...] = jnp.zeros(acc_ref.shape, F32)
    n_full = t0 // SLC_TK

    def pair(jj, carry):
        j = 2 * jj
        for c in all_chains:
            slc_scores(j + 1, sb_ref, mxb_ref, [c])
            slc_update(j, sa_ref, mxa_ref, False, [c])
        for c in all_chains:
            slc_scores(j + 2, sa_ref, mxa_ref, [c])
            slc_update(j + 1, sb_ref, mxb_ref, False, [c])
        return carry

    lax.fori_loop(0, n_full // 2, pair, 0)

    visible = t0 - n_full * SLC_TK + step_tokens

    def tail(odd, rows):
        if odd:
            for c in all_chains:
                slc_scores(n_full, sb_ref, mxb_ref, [c])
                slc_update(n_full - 1, sa_ref, mxa_ref, False, [c])
            slc_update(n_full, sb_ref, mxb_ref, True, visible=rows)
        else:
            slc_update(n_full, sa_ref, mxa_ref, True, visible=rows)

    for rows in range(step_tokens, SLC_TK + 1, step_tokens):
        for odd in (False, True):
            parity = (n_full % 2 == 1) if odd else (n_full % 2 == 0)
            pl.when((visible == rows) & parity)(functools.partial(tail, odd, rows))

    for qt in tiles:
        gates = _sigmoid(glt_ref[0, qt] + bg_ref[...])
        outs = []
        for g in range(NSA_KV_HEADS):
            c = qt * NSA_KV_HEADS + g
            os_t = acc_ref[c, 0:HEAD_DIM, :] * (1.0 / acc_ref[c, HEAD_DIM:HEAD_DIM + 1, :])
            oc_t = oc_ref[cur, c]
            ow_t = ow_ref[c]
            for r in range(NSA_GROUP):
                sl = slice(r * NSA_TQ, (r + 1) * NSA_TQ)
                row = g * gg + r
                outs.append(gates[row:row + 1, :] * oc_t[:, sl]
                            + gates[row + NSA_GROUP:row + NSA_GROUP + 1, :] * os_t[:, sl]
                            + gates[row + 2 * NSA_GROUP:row + 2 * NSA_GROUP + 1, :] * ow_t[:, sl])
        o_tok = jnp.concatenate(outs, axis=0).T
        rows = pl.ds(qt * NSA_TQ, NSA_TQ)
        ng = ng_ref[0, rows, :]
        o_ref[0, rows, :] = (o_tok * (ng * _sigmoid(ng))).astype(BF16)


def _nsa(nqt, glt, ng3, ks, kw, vt, kc, vct, mt, qnw, bg, B, T):
    nt = T // LANES
    ncb = kc.shape[3]
    G = NSA_KV_HEADS
    NC = NSA_QT * G
    ncols = NSA_GROUP * NSA_TQ
    steps = T // (NSA_QT * NSA_TQ)
    return pl.pallas_call(
        _nsa_kernel,
        grid=(B, steps),
        in_specs=[
            pl.BlockSpec((1, NSA_QT, D_NSA, LANES), lambda b, i: (b, 0, 0, 0)),
            pl.BlockSpec((1, NSA_QT, D_NSA, LANES), lambda b, i: (b, jnp.minimum(i + 1, steps - 1), 0, 0)),
            pl.BlockSpec((1, NSA_QT, 16 * G, LANES), lambda b, i: (b, i, 0, 0)),
            pl.BlockSpec((1, NSA_QT * NSA_TQ, D_NSA), lambda b, i: (b, i, 0)),
            pl.BlockSpec((1, G, T, PAIR), lambda b, i: (b, 0, 0, 0)),
            pl.BlockSpec((1, G, T, PAIR), lambda b, i: (b, 0, 0, 0)),
            pl.BlockSpec((1, nt, G * V_ROWS, LANES), lambda b, i: (b, 0, 0, 0)),
            pl.BlockSpec((1, nt, G * V_ROWS, LANES), lambda b, i: (b, 0, 1, 0)),
            pl.BlockSpec((1, 1, G, ncb, HEAD_DIM), lambda b, i: (b, 0, 0, 0, 0)),
            pl.BlockSpec((1, 1, G, HEAD_DIM, ncb), lambda b, i: (b, 1, 0, 0, 0)),
            pl.BlockSpec(mt.shape, lambda b, i: (0, 0)),
            pl.BlockSpec((HEAD_DIM, 1), lambda b, i: (0, 0)),
            pl.BlockSpec((16 * G, 1), lambda b, i: (0, 0)),
        ],
        out_specs=pl.BlockSpec((1, NSA_QT * NSA_TQ, D_NSA), lambda b, i: (b, i, 0)),
        out_shape=jax.ShapeDtypeStruct((B, T, D_NSA), BF16),
        scratch_shapes=[
            pltpu.VMEM((2, NC, 2 * HEAD_DIM, ncols), BF16),
            pltpu.VMEM((2, NC, HEAD_DIM, ncols), F32),
            pltpu.VMEM((NC, SLC_TK, ncols), F32),
            pltpu.VMEM((NC, SLC_TK, ncols), F32),
            pltpu.VMEM((NC, 1, ncols), F32),
            pltpu.VMEM((NC, 1, ncols), F32),
            pltpu.VMEM((NC, WIN_KEYS, ncols), F32),
            pltpu.VMEM((NC, HEAD_DIM, ncols), F32),
            pltpu.VMEM((NC, 1, ncols), F32),
            pltpu.VMEM((NC, V_ROWS, ncols), F32),
        ],
        compiler_params=pltpu.CompilerParams(dimension_semantics=("parallel", "arbitrary"),
                                             vmem_limit_bytes=VMEM_LIMIT),
        name="nsa",
    )(nqt, nqt, glt, ng3, ks, kw, vt, vt, kc, vct, mt, qnw, bg)


def _out_kernel(x_ref, yr_ref, yn_ref, wr_ref, wn_ref, o_ref):
    o_ref[...] = (x_ref[...]
                  + jnp.dot(yr_ref[...], wr_ref[...], preferred_element_type=F32)
                  + jnp.dot(yn_ref[...], wn_ref[...], preferred_element_type=F32))


def _outproj(x2, yr, yn, wr, wn):
    N = x2.shape[0]
    row = lambda i: (i, 0)
    const = lambda i: (0, 0)
    return pl.pallas_call(
        _out_kernel,
        grid=(N // OUT_TM,),
        in_specs=[pl.BlockSpec((OUT_TM, D_MODEL), row), pl.BlockSpec((OUT_TM, D_RET), row),
                  pl.BlockSpec((OUT_TM, D_NSA), row), pl.BlockSpec(wr.shape, const), pl.BlockSpec(wn.shape, const)],
        out_specs=pl.BlockSpec((OUT_TM, D_MODEL), row),
        out_shape=jax.ShapeDtypeStruct((N, D_MODEL), F32),
        compiler_params=pltpu.CompilerParams(dimension_semantics=("parallel",), vmem_limit_bytes=VMEM_LIMIT),
        name="outproj",
    )(x2, yr, yn, wr, wn)


@functools.lru_cache(maxsize=None)
def _tables(T):
    half = HEAD_DIM // 2
    inv = ROPE_THETA ** (-np.arange(half, dtype=np.float64) / half)
    ang = np.arange(T, dtype=np.float64)[:, None] * inv[None, :]
    cos = np.concatenate([np.cos(ang)] * 4, axis=1).astype(np.float32)
    sin = np.concatenate([-np.sin(ang), -np.sin(ang), np.sin(ang), np.sin(ang)], axis=1).astype(np.float32)

    C = RET_CHUNK
    log_g = np.log1p(-np.exp2(-5.0 - np.arange(RET_HEADS, dtype=np.float64)))
    pos = np.arange(C, dtype=np.float64)
    diff = pos[:, None] - pos[None, :]
    decay = np.where(diff >= 0, np.exp(log_g[:, None, None] * np.maximum(diff, 0.0)), 0.0)
    zeta = np.exp(log_g[:, None] * (C - 1.0 - pos))
    xi = np.exp(log_g[:, None] * (pos + 1.0))
    g_chunk = np.exp(log_g * C)
    npair = RET_HEADS // 2

    def pair_lanes(a):
        return np.repeat(a.reshape(npair, 2, C).transpose(0, 2, 1), HEAD_DIM, axis=2).astype(np.float32)

    dec = decay.reshape(npair, 2, C, C).transpose(0, 2, 1, 3).reshape(npair, C, 2 * C).astype(np.float32)
    gch = np.repeat(g_chunk.reshape(npair, 1, 2), HEAD_DIM, axis=2).astype(np.float32)

    n_cmp = (T - CMP_BLOCK) // CMP_STRIDE + 1
    ncb = T // CMP_STRIDE
    p = np.arange(n_cmp)[:, None] * CMP_STRIDE + np.arange(CMP_BLOCK)[None, :]
    blk = p // SLC_BLOCK
    M = (blk[:, :, None] == np.arange(T // SLC_BLOCK)[None, None, :]).mean(axis=1)
    mt = np.zeros((T // SLC_BLOCK, ncb), np.float32)
    mt[:, :n_cmp] = M.T
    return cos, sin, dec, pair_lanes(zeta), pair_lanes(xi), gch, mt


def kernel(x, norm_w, w_in, ret_norm_w, q_norm_w, k_norm_cmp, k_norm_slc, k_norm_win, cmp_pos_k, cmp_w1_k, cmp_w2_k,
           cmp_pos_v, cmp_w1_v, cmp_w2_v, b_gate, w_out):
    B, T, D = x.shape
    depth = norm_w.shape[0]
    cos, sin, dec, zeta, xi, gch, mt = _tables(T)
    ncb = T // CMP_STRIDE
    half = CMP_STRIDE * HEAD_DIM
    gate_src = np.zeros((NSA_KV_HEADS, 16), np.int32)
    gate_ok = np.zeros((NSA_KV_HEADS, 16), bool)
    for g in range(NSA_KV_HEADS):
        for br in range(N_BRANCH):
            for r in range(NSA_GROUP):
                gate_src[g, br * NSA_GROUP + r] = br * NSA_HEADS + g * NSA_GROUP + r
                gate_ok[g, br * NSA_GROUP + r] = True
    gate_src = gate_src.reshape(-1)
    gate_ok = gate_ok.reshape(-1)

    x2 = x.reshape(B * T, D)
    for layer in range(depth):
        w = w_in[layer].astype(BF16)
        o_ng = 4 * D_RET + D_NSA
        o_kv = o_ng + D_NSA
        quarter = HEAD_DIM // 2
        pair_perm = np.concatenate([np.arange(quarter), HEAD_DIM + np.arange(quarter),
                                    quarter + np.arange(quarter), HEAD_DIM + quarter + np.arange(quarter)])
        qk_perm = np.concatenate([p * PAIR + pair_perm for p in range(RET_HEADS // 2)])
        w_ret = jnp.concatenate([w[:, :D_RET][:, qk_perm], w[:, D_RET:2 * D_RET][:, qk_perm],
                                 w[:, 2 * D_RET:4 * D_RET]], axis=1)
        wt_q = w[:, 4 * D_RET:o_ng].T
        w_ng = w[:, o_ng:o_kv]
        w_ckv = w[:, o_kv:o_kv + 2 * D_KV]
        w_skwk = jnp.concatenate([w[:, o_kv + 2 * D_KV:o_kv + 3 * D_KV], w[:, o_kv + 4 * D_KV:o_kv + 5 * D_KV]], axis=1)
        wt_v = jnp.concatenate([w[:, o_kv + 3 * D_KV:o_kv + 4 * D_KV], w[:, o_kv + 5 * D_KV:o_kv + 6 * D_KV]], axis=1).T
        w_gl = w[:, o_kv + 6 * D_KV:]
        wt_g = jnp.where(gate_ok[:, None], w_gl.T[gate_src], jnp.zeros((), BF16))
        bg = jnp.where(gate_ok, b_gate[layer][gate_src], 0.0).reshape(-1, 1)

        knw = jnp.stack([jnp.tile(k_norm_slc[layer], 2), jnp.tile(k_norm_win[layer], 2)])
        ret, ng, xc, ks, kw, nqt, vt, glt = _proj(x2, norm_w[layer].reshape(1, D), w_ret, w_ng, w_ckv, w_skwk,
                                                  wt_q, wt_v, wt_g, knw, B, T)

        nw_pair = ret_norm_w[layer].reshape(RET_HEADS // 2, 1, PAIR)
        y_ret = _retention(ret.reshape(B, T, 4 * D_RET), jnp.asarray(cos), jnp.asarray(sin), jnp.asarray(dec),
                           jnp.asarray(zeta), jnp.asarray(xi), jnp.asarray(gch), nw_pair, B, T)

        pos =jnp.stack([cmp_pos_k[layer], cmp_pos_v[layer]]).reshape(2, 2, half)
        w1 = jnp.stack([cmp_w1_k[layer], cmp_w1_v[layer]]).astype(BF16)
        w2 = jnp.stack([cmp_w2_k[layer], cmp_w2_v[layer]]).astype(BF16)
        w2t = jnp.swapaxes(w2, 1, 2)
        kc, vct = _compress(xc, pos, w1, w2, w2t, k_norm_cmp[layer].reshape(1, HEAD_DIM), B, ncb)

        y_nsa = _nsa(nqt, glt, ng.reshape(B, T, D_NSA), ks, kw, vt, kc, vct, jnp.asarray(mt).astype(BF16),
                     q_norm_w[layer].reshape(HEAD_DIM, 1), bg, B, T)

        wo = w_out[layer].astype(BF16)
        x2 = _outproj(x2, y_ret.reshape(B * T, D_RET), y_nsa.reshape(B * T, D_NSA), wo[:D_RET], wo[D_RET:])
    return x2.reshape(B, T, D)
```

```python
import functools

import numpy as np
import jax
import jax.numpy as jnp
from jax import lax
from jax.experimental import pallas as pl
from jax.experimental.pallas import tpu as pltpu

F32 = jnp.float32
BF16 = jnp.bfloat16

D_MODEL = 1024
HEAD_DIM = 64
HEAD_SHIFT = 6
RET_HEADS = 8
NSA_HEADS = 8
NSA_KV_HEADS = 2
NSA_GROUP = NSA_HEADS // NSA_KV_HEADS
D_RET = RET_HEADS * HEAD_DIM
D_NSA = NSA_HEADS * HEAD_DIM
D_KV = NSA_KV_HEADS * HEAD_DIM
N_BRANCH = 3
RET_CHUNK = 128
ROPE_THETA = 10000.0
CMP_BLOCK = 32
CMP_STRIDE = 16
CMP_HIDDEN = 256
SLC_BLOCK = 64
SLC_SHIFT = 6
SLC_TOPK = 16
WIN_SIZE = 512
EPS = 1e-6
NEG = -1e30
FORCE_BONUS = 1e4
QK_SCALE = HEAD_DIM ** -0.5
LOG2E = 1.4426950408889634
V_ROWS = HEAD_DIM + 16

LANES = 128
SUBLANES = 8
PAIR = 2 * HEAD_DIM
VMEM_LIMIT = 48 * 1024 * 1024

PROJ_TM = 512
RET_TC = 1024
NSA_TQ = 128
NSA_QT = 2
SLC_TK = 512
WIN_KEYS = WIN_SIZE + NSA_TQ
OUT_TM = 1024

NT_DIMS = (((1,), (1,)), ((), ()))
TN_DIMS = (((0,), (0,)), ((), ()))


def _sigmoid(x):
    return 1.0 / (1.0 + jnp.exp(-x))


def _tile4(a):
    return jnp.concatenate([a, a, a, a], axis=1)


def _proj_kernel(steps_per_batch, x_ref, nw_ref, w_ret_ref, w_ng_ref, w_ckv_ref, w_skwk_ref, wt_q_ref, wt_v_ref,
                 wt_g_ref, knw_ref, ret_ref, ng_ref, xc_ref, ks_ref, kw_ref, nqt_ref, vt_ref, glt_ref, ckv_scr):
    x = x_ref[...]
    ms = jnp.mean(x * x, axis=-1, keepdims=True)
    h = (x * lax.rsqrt(ms + EPS) * nw_ref[...]).astype(BF16)
    ret_ref[...] = jnp.dot(h, w_ret_ref[...], preferred_element_type=F32)
    ng_ref[...] = jnp.dot(h, w_ng_ref[...], preferred_element_type=F32)

    ckv = jnp.dot(h, w_ckv_ref[...], preferred_element_type=F32)
    for half in range(2 * D_KV // LANES):
        ckv_scr[half] = ckv[:, half * LANES:(half + 1) * LANES]
    for l in range(CMP_STRIDE):
        for half in range(2 * D_KV // LANES):
            rows = ckv_scr[half, pl.ds(l, PROJ_TM // CMP_STRIDE, stride=CMP_STRIDE), :]
            for s in range(LANES // HEAD_DIM):
                xc_ref[0, half * (LANES // HEAD_DIM) + s, :, l * HEAD_DIM:(l + 1) * HEAD_DIM] = (
                    rows[:, s * HEAD_DIM:(s + 1) * HEAD_DIM])

    skwk = jnp.dot(h, w_skwk_ref[...], preferred_element_type=F32)
    lane = lax.broadcasted_iota(jnp.int32, (1, PAIR), 1)
    head0 = lane < HEAD_DIM
    m0 = jnp.where(head0, 1.0, 0.0)
    m1 = 1.0 - m0
    t_start = (pl.program_id(0) % steps_per_batch) * PROJ_TM
    tok = t_start + lax.broadcasted_iota(jnp.int32, (PROJ_TM, PAIR), 0)
    col = lax.broadcasted_iota(jnp.int32, (PROJ_TM, PAIR), 1)
    indicator = jnp.where((tok >> SLC_SHIFT) == col - HEAD_DIM, 1.0, 0.0)

    def pair_normed(t, w):
        t2 = t * t
        ms0 = jnp.sum(t2 * m0, axis=-1, keepdims=True)
        ms1 = jnp.sum(t2 * m1, axis=-1, keepdims=True)
        return t * lax.rsqrt(jnp.where(head0, ms0, ms1) * (1.0 / HEAD_DIM) + EPS) * w

    ns = pair_normed(skwk[:, :PAIR], knw_ref[0:1, :])
    nwin = pair_normed(skwk[:, PAIR:], knw_ref[1:2, :])
    for g in range(NSA_KV_HEADS):
        s_g = ns if g == 0 else pltpu.roll(ns, HEAD_DIM, 1)
        w_g = nwin if g == 0 else pltpu.roll(nwin, HEAD_DIM, 1)
        ks_ref[0, g] = jnp.where(head0, s_g, indicator).astype(BF16)
        kw_ref[0, g] = jnp.where(head0, w_g, 0.0).astype(BF16)

    qt = lax.dot_general(wt_q_ref[...], h, NT_DIMS, preferred_element_type=F32)
    vt = lax.dot_general(wt_v_ref[...], h, NT_DIMS, preferred_element_type=F32)
    gt = lax.dot_general(wt_g_ref[...], h, NT_DIMS, preferred_element_type=F32)
    pad_row = lax.broadcasted_iota(jnp.int32, (V_ROWS - HEAD_DIM, LANES), 0)
    ones_pad = jnp.where(pad_row == 0, 1.0, 0.0).astype(BF16)
    for j in range(PROJ_TM // LANES):
        sl = slice(j * LANES, (j + 1) * LANES)
        nqt_ref[0, j] = qt[:, sl]
        for blk in range(2 * NSA_KV_HEADS):
            vt_ref[0, j, blk * V_ROWS:blk * V_ROWS + HEAD_DIM, :] = (
                vt[blk * HEAD_DIM:(blk + 1) * HEAD_DIM, sl].astype(BF16))
            vt_ref[0, j, blk * V_ROWS + HEAD_DIM:(blk + 1) * V_ROWS, :] = ones_pad
        glt_ref[0, j] = gt[:, sl]


def _proj(x2, nw, w_ret, w_ng, w_ckv, w_skwk, wt_q, wt_v, wt_g, knw, B, T):
    N = B * T
    tpb = T // PROJ_TM
    sub = PROJ_TM // LANES
    nt = T // LANES
    const = lambda i: (0, 0)
    row = lambda i: (i, 0)
    trn = lambda i: (i // tpb, i % tpb, 0, 0)
    tokm = lambda i: (i // tpb, 0, i % tpb, 0)
    return pl.pallas_call(
        functools.partial(_proj_kernel, tpb),
        grid=(N // PROJ_TM,),
        in_specs=[
            pl.BlockSpec((PROJ_TM, D_MODEL), row),
            pl.BlockSpec((1, D_MODEL), const),
            pl.BlockSpec(w_ret.shape, const),
            pl.BlockSpec(w_ng.shape, const),
            pl.BlockSpec(w_ckv.shape, const),
            pl.BlockSpec(w_skwk.shape, const),
            pl.BlockSpec(wt_q.shape, const),
            pl.BlockSpec(wt_v.shape, const),
            pl.BlockSpec(wt_g.shape, const),
            pl.BlockSpec(knw.shape, const),
        ],
        out_specs=[
            pl.BlockSpec((PROJ_TM, 4 * D_RET), row),
            pl.BlockSpec((PROJ_TM, D_NSA), row),
            pl.BlockSpec((1, 2 * NSA_KV_HEADS, PROJ_TM // CMP_STRIDE, CMP_STRIDE * HEAD_DIM), tokm),
            pl.BlockSpec((1, NSA_KV_HEADS, PROJ_TM, PAIR), tokm),
            pl.BlockSpec((1, NSA_KV_HEADS, PROJ_TM, PAIR), tokm),
            pl.BlockSpec((1, sub, D_NSA, LANES), trn),
            pl.BlockSpec((1, sub, 2 * NSA_KV_HEADS * V_ROWS, LANES), trn),
            pl.BlockSpec((1, sub, 32, LANES), trn),
        ],
        out_shape=[
            jax.ShapeDtypeStruct((N, 4 * D_RET), F32),
            jax.ShapeDtypeStruct((N, D_NSA), F32),
            jax.ShapeDtypeStruct((B, 2 * NSA_KV_HEADS, T // CMP_STRIDE, CMP_STRIDE * HEAD_DIM), F32),
            jax.ShapeDtypeStruct((B, NSA_KV_HEADS, T, PAIR), BF16),
            jax.ShapeDtypeStruct((B, NSA_KV_HEADS, T, PAIR), BF16),
            jax.ShapeDtypeStruct((B, nt, D_NSA, LANES), F32),
            jax.ShapeDtypeStruct((B, nt, 2 * NSA_KV_HEADS * V_ROWS, LANES), BF16),
            jax.ShapeDtypeStruct((B, nt, 32, LANES), F32),
        ],
        scratch_shapes=[pltpu.VMEM((2 * D_KV // LANES, PROJ_TM, LANES), F32)],
        compiler_params=pltpu.CompilerParams(dimension_semantics=("parallel",), vmem_limit_bytes=VMEM_LIMIT),
        name="proj",
    )(x2, nw, w_ret, w_ng, w_ckv, w_skwk, wt_q, wt_v, wt_g, knw)


def _ret_kernel(qkvg_ref, cos_ref, sin_ref, dec_ref, zeta_ref, xi_ref, gch_ref, nw_ref, o_ref, state_ref):
    @pl.when(pl.program_id(1) == 0)
    def _():
        state_ref[...] = jnp.zeros_like(state_ref)

    lane = lax.broadcasted_iota(jnp.int32, (1, PAIR), 1)
    q_head = (lane >> (HEAD_SHIFT - 1)) & 1
    v_head = lane >> HEAD_SHIFT
    q_mask = [jnp.where(q_head == h, 1.0, 0.0).astype(BF16) for h in (0, 1)]
    v_mask = [jnp.where(v_head == h, 1.0, 0.0).astype(BF16) for h in (0, 1)]
    row_qh = (lax.broadcasted_iota(jnp.int32, (PAIR, PAIR), 0) >> (HEAD_SHIFT - 1)) & 1
    row_vh = lax.broadcasted_iota(jnp.int32, (PAIR, PAIR), 0) >> HEAD_SHIFT
    col_vh = lax.broadcasted_iota(jnp.int32, (PAIR, PAIR), 1) >> HEAD_SHIFT
    same_head_kv = jnp.where(row_qh == col_vh, 1.0, 0.0)
    head_mean = jnp.where(row_vh == col_vh, 1.0 / HEAD_DIM, 0.0).astype(BF16)

    pairs = range(RET_HEADS // 2)
    mean2 = jnp.concatenate([head_mean, head_mean], axis=0)
    tok0 = pl.program_id(1) * RET_TC
    for c in range(RET_TC // RET_CHUNK):
        sl = pl.ds(c * RET_CHUNK, RET_CHUNK)
        pos = pl.ds(pl.multiple_of(tok0 + c * RET_CHUNK, RET_CHUNK), RET_CHUNK)
        cos = cos_ref[pos, :]
        sin = sin_ref[pos, :]
        cols = [slice(p * PAIR, (p + 1) * PAIR) for p in pairs]
        qb, kb, vb, vzb = [], [], [], []
        for p in pairs:
            q = qkvg_ref[0, sl, pl.ds(p * PAIR, PAIR)]
            k = qkvg_ref[0, sl, pl.ds(D_RET + p * PAIR, PAIR)]
            v = qkvg_ref[0, sl, pl.ds(2 * D_RET + p * PAIR, PAIR)]
            qb.append((q * cos + pltpu.roll(q, HEAD_DIM, 1) * sin).astype(BF16))
            kb.append(((k * cos + pltpu.roll(k, HEAD_DIM, 1) * sin) * QK_SCALE).astype(BF16))
            vb.append(v.astype(BF16))
            vzb.append((v * zeta_ref[p]).astype(BF16))
        states = [state_ref[p] for p in pairs]
        kk = [jnp.concatenate([kb[p] * q_mask[h] for h in (0, 1)], axis=0) for p in pairs]
        s = [lax.dot_general(qb[p], kk[p], NT_DIMS, preferred_element_type=F32) for p in pairs]
        o_cross = [jnp.dot(qb[p], states[p].astype(BF16), preferred_element_type=F32) for p in pairs]
        kv = [lax.dot_general(kb[p], vzb[p], TN_DIMS, preferred_element_type=F32) for p in pairs]
        sb = [(s[p] * dec_ref[p]).astype(BF16) for p in pairs]
        vv = [jnp.concatenate([vb[p] * v_mask[h] for h in (0, 1)], axis=0) for p in pairs]
        o = [jnp.dot(sb[p], vv[p], preferred_element_type=F32) + o_cross[p] * xi_ref[p] for p in pairs]
        for p in pairs:
            state_ref[p] = states[p] * gch_ref[p] + kv[p] * same_head_kv
        o2 = [o[p] * o[p] for p in pairs]
        o2_hi = [o2[p].astype(BF16) for p in pairs]
        o2_hl = [jnp.concatenate([o2_hi[p], (o2[p] - o2_hi[p].astype(F32)).astype(BF16)], axis=1) for p in pairs]
        ms = [jnp.dot(o2_hl[p], mean2, preferred_element_type=F32) for p in pairs]
        for p in pairs:
            g = qkvg_ref[0, sl, pl.ds(3 * D_RET + p * PAIR, PAIR)]
            y = o[p] * lax.rsqrt(ms[p] + EPS) * nw_ref[p]
            o_ref[0, sl, cols[p]] = (y * (g * _sigmoid(g))).astype(BF16)


def _retention(ret3, cos, sin, dec, zeta, xi, gch, nw, B, T):
    npair = RET_HEADS // 2
    whole = lambda a: pl.BlockSpec(a.shape, lambda b, i: (0,) * a.ndim)
    return pl.pallas_call(
        _ret_kernel,
        grid=(B, T // RET_TC),
        in_specs=[pl.BlockSpec((1, RET_TC, 4 * D_RET), lambda b, i: (b, i, 0)), whole(cos), whole(sin), whole(dec),
                  whole(zeta), whole(xi), whole(gch), whole(nw)],
        out_specs=pl.BlockSpec((1, RET_TC, D_RET), lambda b, i: (b, i, 0)),
        out_shape=jax.ShapeDtypeStruct((B, T, D_RET), BF16),
        scratch_shapes=[pltpu.VMEM((npair, PAIR, PAIR), F32)],
        compiler_params=pltpu.CompilerParams(dimension_semantics=("parallel", "arbitrary"),
                                             vmem_limit_bytes=VMEM_LIMIT),
        name="retention",
    )(ret3, cos, sin, dec, zeta, xi, gch, nw)


def _cmp_kernel(x_ref, pos_ref, w1_ref, w2_ref, w2t_ref, knw_ref, o_ref, ot_ref):
    is_key = pl.program_id(1) == 0
    half = CMP_STRIDE * HEAD_DIM
    for g in range(NSA_KV_HEADS):
        x = x_ref[0, g]
        a = jnp.dot((x + pos_ref[0, 0:1, :]).astype(BF16), w1_ref[0, :half, :], preferred_element_type=F32)
        b = jnp.dot((x + pos_ref[0, 1:2, :]).astype(BF16), w1_ref[0, half:, :], preferred_element_type=F32)
        hid = a + pltpu.roll(b, b.shape[0] - 1, 0)
        hid = (hid * _sigmoid(hid)).astype(BF16)
        out = jnp.dot(hid, w2_ref[0], preferred_element_type=F32)
        ms = jnp.mean(out * out, axis=-1, keepdims=True)
        normed = out * lax.rsqrt(ms + EPS) * knw_ref[...]
        o_ref[0, 0, g] = jnp.where(is_key, normed, out)
        ot_ref[0, 0, g] = lax.dot_general(w2t_ref[0], hid, NT_DIMS, preferred_element_type=F32)


def _compress(xc, pos, w1, w2, w2t, knw, B, ncb):
    return pl.pallas_call(
        _cmp_kernel,
        grid=(B, 2),
        in_specs=[
            pl.BlockSpec((1, NSA_KV_HEADS, ncb, CMP_STRIDE * HEAD_DIM), lambda b, s: (b, s, 0, 0)),
            pl.BlockSpec((1, 2, CMP_STRIDE * HEAD_DIM), lambda b, s: (s, 0, 0)),
            pl.BlockSpec((1, CMP_BLOCK * HEAD_DIM, CMP_HIDDEN), lambda b, s: (s, 0, 0)),
            pl.BlockSpec((1, CMP_HIDDEN, HEAD_DIM), lambda b, s: (s, 0, 0)),
            pl.BlockSpec((1, HEAD_DIM, CMP_HIDDEN), lambda b, s: (s, 0, 0)),
            pl.BlockSpec((1, HEAD_DIM), lambda b, s: (0, 0)),
        ],
        out_specs=[
            pl.BlockSpec((1, 1, NSA_KV_HEADS, ncb, HEAD_DIM), lambda b, s: (b, s, 0, 0, 0)),
            pl.BlockSpec((1, 1, NSA_KV_HEADS, HEAD_DIM, ncb), lambda b, s: (b, s, 0, 0, 0)),
        ],
        out_shape=[
            jax.ShapeDtypeStruct((B, 2, NSA_KV_HEADS, ncb, HEAD_DIM), F32),
            jax.ShapeDtypeStruct((B, 2, NSA_KV_HEADS, HEAD_DIM, ncb), F32),
        ],
        compiler_params=pltpu.CompilerParams(dimension_semantics=("parallel", "parallel"),
                                             vmem_limit_bytes=VMEM_LIMIT),
        name="compress",
    )(xc, pos, w1, w2, w2t, knw)


def _nsa_kernel(qt_ref, qtn_ref, glt_ref, ng_ref, ks_ref, kw_ref, vst_ref, vwt_ref, kc_ref, vct_ref, mt_ref, qnw_ref,
                bg_ref, o_ref, qp_ref, oc_ref, sa_ref, sb_ref, mxa_ref, mxb_ref, sw_ref, ow_ref, m_ref, acc_ref):
    qi = pl.program_id(1)
    t0 = qi * (NSA_QT * NSA_TQ)
    tiles = range(NSA_QT)
    chains = [(qt, g) for qt in tiles for g in range(NSA_KV_HEADS)]
    gq = NSA_GROUP * HEAD_DIM
    gg = 16
    ncb = kc_ref.shape[3]
    n_slc = mt_ref.shape[0]

    def select_masks(base_t0):
        n_idx = lax.broadcasted_iota(jnp.int32, (ncb, NSA_TQ), 0)
        jb = lax.broadcasted_iota(jnp.int32, (n_slc, NSA_TQ), 0)
        out = []
        for qt in tiles:
            tok = base_t0 + qt * NSA_TQ
            tok_c = tok + lax.broadcasted_iota(jnp.int32, (ncb, NSA_TQ), 1)
            cbias = _tile4(jnp.where((n_idx * CMP_STRIDE + (CMP_BLOCK - 1)) <= tok_c, 0.0, NEG))
            tok_row = tok + lax.broadcasted_iota(jnp.int32, (1, NSA_TQ), 1)
            has_block = _tile4(jnp.where(tok_row >= CMP_BLOCK - 1, 1.0, 0.0))
            tok_s = tok + lax.broadcasted_iota(jnp.int32, (n_slc, NSA_TQ), 1)
            valid_s = jb * SLC_BLOCK <= tok_s
            force = (jb == (tok_s >> SLC_SHIFT)) | (jb == 0)
            out.append((cbias, has_block, valid_s, force))
        return out

    def select_scores(src_ref, slot, masks, c, rows=ncb):
        qt, g = chains[c]
        cols = []
        for r in range(NSA_GROUP):
            q = src_ref[0, qt, g * gq + r * HEAD_DIM:g * gq + (r + 1) * HEAD_DIM, :]
            ms = jnp.mean(q * q, axis=0, keepdims=True)
            cols.append(q * lax.rsqrt(ms + EPS) * qnw_ref[...] * (QK_SCALE * LOG2E))
        qs = jnp.concatenate(cols, axis=1).astype(BF16)
        qp_ref[slot, c, 0:HEAD_DIM, :] = qs
        return (jnp.dot(kc_ref[0, 0, g, 0:rows, :].astype(BF16), qs, preferred_element_type=F32)
                + masks[qt][0][0:rows, :])

    def select_probs(sc, slot, masks, c, want_scores=True):
        qt, g = chains[c]
        _, has_block, valid_s, force = masks[qt]
        rows = sc.shape[0]
        mc = jnp.max(sc, axis=0, keepdims=True)
        ec = jnp.exp2(sc - mc)
        lc = jnp.sum(ec, axis=0, keepdims=True)
        p = ec * (has_block / lc)
        oc_ref[slot, c] = jnp.dot(vct_ref[0, 0, g, :, 0:rows].astype(BF16), p.astype(BF16),
                                  preferred_element_type=F32)
        if not want_scores:
            return None
        ps = p[:, 0:NSA_TQ]
        for r in range(1, NSA_GROUP):
            ps = ps + p[:, r * NSA_TQ:(r + 1) * NSA_TQ]
        ps_hi = ps.astype(BF16)
        ps_lo = (ps - ps_hi.astype(F32)).astype(BF16)
        imp = (jnp.dot(mt_ref[:, 0:rows], ps_hi, preferred_element_type=F32)
               + jnp.dot(mt_ref[:, 0:rows], ps_lo, preferred_element_type=F32))
        return jnp.where(valid_s, jnp.where(force, imp + FORCE_BONUS, imp), NEG)

    def select_rank(scores, valid_s, slot, n_live):
        if n_live <= SLC_TOPK:
            for c, (qt, g) in enumerate(chains):
                qp_ref[slot, c, HEAD_DIM:2 * HEAD_DIM, :] = _tile4(jnp.where(valid_s[qt], 0.0, NEG).astype(BF16))
            return
        sub = lax.broadcasted_iota(jnp.int32, (SUBLANES, NSA_TQ), 0)
        for c, (qt, g) in enumerate(chains):
            score = scores[c]
            blocks = [score[v * SUBLANES:(v + 1) * SUBLANES, :] for v in range(n_live // SUBLANES)]
            ranks = [jnp.zeros((SUBLANES, NSA_TQ), F32) for _ in blocks]
            for i in range(n_live):
                row = score[i:i + 1, :]
                for v, blk in enumerate(blocks):
                    if v * SUBLANES > i:
                        beats = row >= blk
                    elif (v + 1) * SUBLANES <= i:
                        beats = row > blk
                    else:
                        beats = (row > blk) | ((row >= blk) & (sub > i - v * SUBLANES))
                    ranks[v] = ranks[v] + jnp.where(beats, 1.0, 0.0)
            dead = [jnp.full((SUBLANES, NSA_TQ), float(n_slc), F32)] * ((n_slc - n_live) // SUBLANES)
            rank = jnp.concatenate(ranks + dead, axis=0)
            sel = (rank < float(SLC_TOPK)) & valid_s[qt]
            qp_ref[slot, c, HEAD_DIM:2 * HEAD_DIM, :] = _tile4(jnp.where(sel, 0.0, NEG).astype(BF16))

    step_tokens = NSA_QT * NSA_TQ

    def live_blocks(step):
        return ((step + 1) * step_tokens - 1) // SLC_BLOCK + 1

    all_chains = list(range(len(chains)))

    @pl.when(qi == 0)
    def _():
        masks0 = select_masks(0)
        rows0 = min(ncb, -(-((step_tokens - CMP_BLOCK) // CMP_STRIDE + 1) // 16) * 16)
        assert live_blocks(0) <= SLC_TOPK
        for c in all_chains:
            select_probs(select_scores(qt_ref, 0, masks0, c, rows0), 0, masks0, c, want_scores=False)
        select_rank(None, [m[2] for m in masks0], 0, live_blocks(0))

    cur = qi % 2
    nxt = 1 - cur
    t_next = t0 + NSA_QT * NSA_TQ

    kt0, ks0 = [], []
    for qt in tiles:
        kt0.append(jnp.maximum(qi * NSA_QT + qt - WIN_SIZE // NSA_TQ, 0))
        ks0.append(pl.multiple_of(kt0[qt] * NSA_TQ, NSA_TQ))

    def window_masks():
        c_minus_r = (lax.broadcasted_iota(jnp.int32, (WIN_KEYS, NSA_TQ), 1)
                     - lax.broadcasted_iota(jnp.int32, (WIN_KEYS, NSA_TQ), 0))
        out = []
        for qt in tiles:
            delta = (t0 + qt * NSA_TQ - ks0[qt]) + c_minus_r
            in_window = lax.bitcast_convert_type(delta, jnp.uint32) < WIN_SIZE
            out.append(_tile4(jnp.where(in_window, 0.0, NEG)))
        return out

    def win_scores(c, wbias):
        qt, g = chains[c]
        sw = jnp.dot(kw_ref[0, g, pl.ds(ks0[qt], WIN_KEYS), 0:HEAD_DIM], qp_ref[cur, c, 0:HEAD_DIM, :],
                     preferred_element_type=F32)
        if wbias is None:
            r_b = lax.broadcasted_iota(jnp.int32, (NSA_TQ, NSA_TQ), 0)
            c_b = lax.broadcasted_iota(jnp.int32, (NSA_TQ, NSA_TQ), 1)
            sw = jnp.concatenate([sw[0:NSA_TQ] + _tile4(jnp.where(r_b > c_b, 0.0, NEG)),
                                  sw[NSA_TQ:WIN_SIZE],
                                  sw[WIN_SIZE:] + _tile4(jnp.where(r_b <= c_b, 0.0, NEG))], axis=0)
        else:
            sw = sw + wbias[qt]
        sw_ref[c] = sw
        return jnp.max(sw, axis=0, keepdims=True)

    def win_attend(c, mw):
        qt, g = chains[c]
        ewb = jnp.exp2(sw_ref[c] - mw).astype(BF16)
        vwt = jnp.concatenate([vwt_ref[0, kt0[qt] + j, g * V_ROWS:(g + 1) * V_ROWS, :]
                               for j in range(WIN_KEYS // LANES)], axis=1)
        ow_aug = jnp.dot(vwt, ewb, preferred_element_type=F32)
        return ow_aug[0:HEAD_DIM, :] * (1.0 / ow_aug[HEAD_DIM:HEAD_DIM + 1, :])

    vt_per_tile = SLC_TK // LANES

    def slc_scores(j, dst_ref, mx_ref, which=all_chains):
        kst = pl.multiple_of(j * SLC_TK, SLC_TK)
        for c in which:
            qt, g = chains[c]
            s = jnp.dot(ks_ref[0, g, pl.ds(kst, SLC_TK), :], qp_ref[cur, c], preferred_element_type=F32)
            dst_ref[c] = s
            mx_ref[c] = jnp.max(s, axis=0, keepdims=True)

    def slc_update(j, src_ref, mx_ref, causal, which=all_chains, visible=SLC_TK):
        if causal:
            band = _tile4(jnp.where(lax.broadcasted_iota(jnp.int32, (NSA_TQ, NSA_TQ), 0)
                                    <= lax.broadcasted_iota(jnp.int32, (NSA_TQ, NSA_TQ), 1), 0.0, NEG))
        for c in which:
            qt, g = chains[c]
            if causal:
                r0 = visible - step_tokens + qt * NSA_TQ
                rows = r0 + NSA_TQ
                s_band = src_ref[c, r0:rows, :] + band
                s = jnp.concatenate([src_ref[c, 0:r0, :], s_band], axis=0) if r0 > 0 else s_band
                tile_max = jnp.max(s, axis=0, keepdims=True)
            else:
                rows = SLC_TK
                s = src_ref[c]
                tile_max = mx_ref[c]
            m_old = m_ref[c]
            m_new = jnp.maximum(m_old, tile_max)
            alpha = jnp.exp2(m_old - m_new)
            eb = jnp.exp2(s - m_new).astype(BF16)
            vt = jnp.concatenate([vst_ref[0, j * vt_per_tile + jj, g * V_ROWS:(g + 1) * V_ROWS, :]
                                  for jj in range(rows // LANES)], axis=1)
            acc_ref[c] = alpha * acc_ref[c] + jnp.dot(vt, eb, preferred_element_type=F32)
            m_ref[c] = m_new

    rank_bounds = list(range(SLC_TOPK, n_slc + 1, SUBLANES))

    def before_loop(select_next, rows, steps):
        masks = select_masks(t_next) if select_next else None
        past_start = all(q * step_tokens >= WIN_SIZE for q in steps)
        wbias = None if past_start else window_masks()

        def matmul_stage(c):
            mw = win_scores(c, wbias)
            sc = select_scores(qtn_ref, nxt, masks, c, rows) if select_next else None
            slc_scores(0, sa_ref, mxa_ref, [c])
            return sc, mw

        sel_scores = []
        staged = matmul_stage(0)
        for c in all_chains:
            staged_next = matmul_stage(c + 1) if c + 1 < len(chains) else None
            if select_next:
                sel_scores.append(select_probs(staged[0], nxt, masks, c))
            ow_ref[c] = win_attend(c, staged[1])
            staged = staged_next
        if not select_next:
            return
        sel_valid = [m[2] for m in masks]
        need = live_blocks(qi + 1)
        used = {min(b for b in rank_bounds if b >= min(live_blocks(q + 1), n_slc)) for q in steps}
        for lo, hi in zip([0] + rank_bounds[:-1], rank_bounds):
            if hi in used:
                in_range = (need > lo) if hi == rank_bounds[-1] else ((need > lo) & (need <= hi))
                pl.when(in_range)(functools.partial(select_rank, sel_scores, sel_valid, nxt, hi))

    n_steps = ks_ref.shape[2] // step_tokens
    row_options = list(range(SLC_BLOCK, ncb + 1, SLC_BLOCK))
    by_rows = {}
    for q in range(n_steps - 1):
        visible_blocks = ((q + 2) * step_tokens - CMP_BLOCK) // CMP_STRIDE + 1
        by_rows.setdefault(min(r for r in row_options if r >= min(visible_blocks, ncb)), []).append(q)
    for rows, steps in by_rows.items():
        pl.when((qi >= steps[0]) & (qi <= steps[-1]))(functools.partial(before_loop, True, rows, steps))
    pl.when(qi + 1 >= n_steps)(functools.partial(before_loop, False, ncb, [n_steps - 1]))

    m_ref[...] = jnp.full(m_ref.shape, NEG, F32)
    acc_ref[...] = jnp.zeros(acc_ref.shape, F32)
    n_full = t0 // SLC_TK

    def pair(jj, carry):
        j = 2 * jj
        for c in all_chains:
            slc_scores(j + 1, sb_ref, mxb_ref, [c])
            slc_update(j, sa_ref, mxa_ref, False, [c])
        for c in all_chains:
            slc_scores(j + 2, sa_ref, mxa_ref, [c])
            slc_update(j + 1, sb_ref, mxb_ref, False, [c])
        return carry

    lax.fori_loop(0, n_full // 2, pair, 0)

    visible = t0 - n_full * SLC_TK + step_tokens

    def tail(odd, rows):
        if odd:
            for c in all_chains:
                slc_scores(n_full, sb_ref, mxb_ref, [c])
                slc_update(n_full - 1, sa_ref, mxa_ref, False, [c])
            slc_update(n_full, sb_ref, mxb_ref, True, visible=rows)
        else:
            slc_update(n_full, sa_ref, mxa_ref, True, visible=rows)

    for rows in range(step_tokens, SLC_TK + 1, step_tokens):
        for odd in (False, True):
            parity = (n_full % 2 == 1) if odd else (n_full % 2 == 0)
            pl.when((visible == rows) & parity)(functools.partial(tail, odd, rows))

    for qt in tiles:
        gates = _sigmoid(glt_ref[0, qt] + bg_ref[...])
        outs = []
        for g in range(NSA_KV_HEADS):
            c = qt * NSA_KV_HEADS + g
            os_t = acc_ref[c, 0:HEAD_DIM, :] * (1.0 / acc_ref[c, HEAD_DIM:HEAD_DIM + 1, :])
            oc_t = oc_ref[cur, c]
            ow_t = ow_ref[c]
            for r in range(NSA_GROUP):
                sl = slice(r * NSA_TQ, (r + 1) * NSA_TQ)
                row = g * gg + r
                outs.append(gates[row:row + 1, :] * oc_t[:, sl]
                            + gates[row + NSA_GROUP:row + NSA_GROUP + 1, :] * os_t[:, sl]
                            + gates[row + 2 * NSA_GROUP:row + 2 * NSA_GROUP + 1, :] * ow_t[:, sl])
        o_tok = jnp.concatenate(outs, axis=0).T
        rows = pl.ds(qt * NSA_TQ, NSA_TQ)
        ng = ng_ref[0, rows, :]
        o_ref[0, rows, :] = (o_tok * (ng * _sigmoid(ng))).astype(BF16)


def _nsa(nqt, glt, ng3, ks, kw, vt, kc, vct, mt, qnw, bg, B, T):
    nt = T // LANES
    ncb = kc.shape[3]
    G = NSA_KV_HEADS
    NC = NSA_QT * G
    ncols = NSA_GROUP * NSA_TQ
    steps = T // (NSA_QT * NSA_TQ)
    return pl.pallas_call(
        _nsa_kernel,
        grid=(B, steps),
        in_specs=[
            pl.BlockSpec((1, NSA_QT, D_NSA, LANES), lambda b, i: (b, 0, 0, 0)),
            pl.BlockSpec((1, NSA_QT, D_NSA, LANES), lambda b, i: (b, jnp.minimum(i + 1, steps - 1), 0, 0)),
            pl.BlockSpec((1, NSA_QT, 16 * G, LANES), lambda b, i: (b, i, 0, 0)),
            pl.BlockSpec((1, NSA_QT * NSA_TQ, D_NSA), lambda b, i: (b, i, 0)),
            pl.BlockSpec((1, G, T, PAIR), lambda b, i: (b, 0, 0, 0)),
            pl.BlockSpec((1, G, T, PAIR), lambda b, i: (b, 0, 0, 0)),
            pl.BlockSpec((1, nt, G * V_ROWS, LANES), lambda b, i: (b, 0, 0, 0)),
            pl.BlockSpec((1, nt, G * V_ROWS, LANES), lambda b, i: (b, 0, 1, 0)),
            pl.BlockSpec((1, 1, G, ncb, HEAD_DIM), lambda b, i: (b, 0, 0, 0, 0)),
            pl.BlockSpec((1, 1, G, HEAD_DIM, ncb), lambda b, i: (b, 1, 0, 0, 0)),
            pl.BlockSpec(mt.shape, lambda b, i: (0, 0)),
            pl.BlockSpec((HEAD_DIM, 1), lambda b, i: (0, 0)),
            pl.BlockSpec((16 * G, 1), lambda b, i: (0, 0)),
        ],
        out_specs=pl.BlockSpec((1, NSA_QT * NSA_TQ, D_NSA), lambda b, i: (b, i, 0)),
        out_shape=jax.ShapeDtypeStruct((B, T, D_NSA), BF16),
        scratch_shapes=[
            pltpu.VMEM((2, NC, 2 * HEAD_DIM, ncols), BF16),
            pltpu.VMEM((2, NC, HEAD_DIM, ncols), F32),
            pltpu.VMEM((NC, SLC_TK, ncols), F32),
            pltpu.VMEM((NC, SLC_TK, ncols), F32),
            pltpu.VMEM((NC, 1, ncols), F32),
            pltpu.VMEM((NC, 1, ncols), F32),
            pltpu.VMEM((NC, WIN_KEYS, ncols), F32),
            pltpu.VMEM((NC, HEAD_DIM, ncols), F32),
            pltpu.VMEM((NC, 1, ncols), F32),
            pltpu.VMEM((NC, V_ROWS, ncols), F32),
        ],
        compiler_params=pltpu.CompilerParams(dimension_semantics=("parallel", "arbitrary"),
                                             vmem_limit_bytes=VMEM_LIMIT),
        name="nsa",
    )(nqt, nqt, glt, ng3, ks, kw, vt, vt, kc, vct, mt, qnw, bg)


def _out_kernel(x_ref, yr_ref, yn_ref, wr_ref, wn_ref, o_ref):
    o_ref[...] = (x_ref[...]
                  + jnp.dot(yr_ref[...], wr_ref[...], preferred_element_type=F32)
                  + jnp.dot(yn_ref[...], wn_ref[...], preferred_element_type=F32))


def _outproj(x2, yr, yn, wr, wn):
    N = x2.shape[0]
    row = lambda i: (i, 0)
    const = lambda i: (0, 0)
    return pl.pallas_call(
        _out_kernel,
        grid=(N // OUT_TM,),
        in_specs=[pl.BlockSpec((OUT_TM, D_MODEL), row), pl.BlockSpec((OUT_TM, D_RET), row),
                  pl.BlockSpec((OUT_TM, D_NSA), row), pl.BlockSpec(wr.shape, const), pl.BlockSpec(wn.shape, const)],
        out_specs=pl.BlockSpec((OUT_TM, D_MODEL), row),
        out_shape=jax.ShapeDtypeStruct((N, D_MODEL), F32),
        compiler_params=pltpu.CompilerParams(dimension_semantics=("parallel",), vmem_limit_bytes=VMEM_LIMIT),
        name="outproj",
    )(x2, yr, yn, wr, wn)


@functools.lru_cache(maxsize=None)
def _tables(T):
    half = HEAD_DIM // 2
    inv = ROPE_THETA ** (-np.arange(half, dtype=np.float64) / half)
    ang = np.arange(T, dtype=np.float64)[:, None] * inv[None, :]
    cos = np.concatenate([np.cos(ang)] * 4, axis=1).astype(np.float32)
    sin = np.concatenate([-np.sin(ang), -np.sin(ang), np.sin(ang), np.sin(ang)], axis=1).astype(np.float32)

    C = RET_CHUNK
    log_g = np.log1p(-np.exp2(-5.0 - np.arange(RET_HEADS, dtype=np.float64)))
    pos = np.arange(C, dtype=np.float64)
    diff = pos[:, None] - pos[None, :]
    decay = np.where(diff >= 0, np.exp(log_g[:, None, None] * np.maximum(diff, 0.0)), 0.0)
    zeta = np.exp(log_g[:, None] * (C - 1.0 - pos))
    xi = np.exp(log_g[:, None] * (pos + 1.0))
    g_chunk = np.exp(log_g * C)
    npair = RET_HEADS // 2

    def pair_lanes(a):
        return np.repeat(a.reshape(npair, 2, C).transpose(0, 2, 1), HEAD_DIM, axis=2).astype(np.float32)

    dec = decay.reshape(npair, 2, C, C).transpose(0, 2, 1, 3).reshape(npair, C, 2 * C).astype(np.float32)
    gch = np.repeat(g_chunk.reshape(npair, 1, 2), HEAD_DIM, axis=2).astype(np.float32)

    n_cmp = (T - CMP_BLOCK) // CMP_STRIDE + 1
    ncb = T // CMP_STRIDE
    p = np.arange(n_cmp)[:, None] * CMP_STRIDE + np.arange(CMP_BLOCK)[None, :]
    blk = p // SLC_BLOCK
    M = (blk[:, :, None] == np.arange(T // SLC_BLOCK)[None, None, :]).mean(axis=1)
    mt = np.zeros((T // SLC_BLOCK, ncb), np.float32)
    mt[:, :n_cmp] = M.T
    return cos, sin, dec, pair_lanes(zeta), pair_lanes(xi), gch, mt


def kernel(x, norm_w, w_in, ret_norm_w, q_norm_w, k_norm_cmp, k_norm_slc, k_norm_win, cmp_pos_k, cmp_w1_k, cmp_w2_k,
           cmp_pos_v, cmp_w1_v, cmp_w2_v, b_gate, w_out):
    B, T, D = x.shape
    depth = norm_w.shape[0]
    cos, sin, dec, zeta, xi, gch, mt = _tables(T)
    ncb = T // CMP_STRIDE
    half = CMP_STRIDE * HEAD_DIM
    gate_src = np.zeros((NSA_KV_HEADS, 16), np.int32)
    gate_ok = np.zeros((NSA_KV_HEADS, 16), bool)
    for g in range(NSA_KV_HEADS):
        for br in range(N_BRANCH):
            for r in range(NSA_GROUP):
                gate_src[g, br * NSA_GROUP + r] = br * NSA_HEADS + g * NSA_GROUP + r
                gate_ok[g, br * NSA_GROUP + r] = True
    gate_src = gate_src.reshape(-1)
    gate_ok = gate_ok.reshape(-1)

    x2 = x.reshape(B * T, D)
    for layer in range(depth):
        w = w_in[layer].astype(BF16)
        o_ng = 4 * D_RET + D_NSA
        o_kv = o_ng + D_NSA
        quarter = HEAD_DIM // 2
        pair_perm = np.concatenate([np.arange(quarter), HEAD_DIM + np.arange(quarter),
                                    quarter + np.arange(quarter), HEAD_DIM + quarter + np.arange(quarter)])
        qk_perm = np.concatenate([p * PAIR + pair_perm for p in range(RET_HEADS // 2)])
        w_ret = jnp.concatenate([w[:, :D_RET][:, qk_perm], w[:, D_RET:2 * D_RET][:, qk_perm],
                                 w[:, 2 * D_RET:4 * D_RET]], axis=1)
        wt_q = w[:, 4 * D_RET:o_ng].T
        w_ng = w[:, o_ng:o_kv]
        w_ckv = w[:, o_kv:o_kv + 2 * D_KV]
        w_skwk = jnp.concatenate([w[:, o_kv + 2 * D_KV:o_kv + 3 * D_KV], w[:, o_kv + 4 * D_KV:o_kv + 5 * D_KV]], axis=1)
        wt_v = jnp.concatenate([w[:, o_kv + 3 * D_KV:o_kv + 4 * D_KV], w[:, o_kv + 5 * D_KV:o_kv + 6 * D_KV]], axis=1).T
        w_gl = w[:, o_kv + 6 * D_KV:]
        wt_g = jnp.where(gate_ok[:, None], w_gl.T[gate_src], jnp.zeros((), BF16))
        bg = jnp.where(gate_ok, b_gate[layer][gate_src], 0.0).reshape(-1, 1)

        knw = jnp.stack([jnp.tile(k_norm_slc[layer], 2), jnp.tile(k_norm_win[layer], 2)])
        ret, ng, xc, ks, kw, nqt, vt, glt = _proj(x2, norm_w[layer].reshape(1, D), w_ret, w_ng, w_ckv, w_skwk,
                                                  wt_q, wt_v, wt_g, knw, B, T)

        nw_pair = ret_norm_w[layer].reshape(RET_HEADS // 2, 1, PAIR)
        y_ret = _retention(ret.reshape(B, T, 4 * D_RET), jnp.asarray(cos), jnp.asarray(sin), jnp.asarray(dec),
                           jnp.asarray(zeta), jnp.asarray(xi), jnp.asarray(gch), nw_pair, B, T)

        pos =jnp.stack([cmp_pos_k[layer], cmp_pos_v[layer]]).reshape(2, 2, half)
        w1 = jnp.stack([cmp_w1_k[layer], cmp_w1_v[layer]]).astype(BF16)
        w2 = jnp.stack([cmp_w2_k[layer], cmp_w2_v[layer]]).astype(BF16)
        w2t = jnp.swapaxes(w2, 1, 2)
        kc, vct = _compress(xc, pos, w1, w2, w2t, k_norm_cmp[layer].reshape(1, HEAD_DIM), B, ncb)

        y_nsa = _nsa(nqt, glt, ng.reshape(B, T, D_NSA), ks, kw, vt, kc, vct, jnp.asarray(mt).astype(BF16),
                     q_norm_w[layer].reshape(HEAD_DIM, 1), bg, B, T)

        wo = w_out[layer].astype(BF16)
        x2 = _outproj(x2, y_ret.reshape(B * T, D_RET), y_nsa.reshape(B * T, D_NSA), wo[:D_RET], wo[D_RET:])
    return x2.reshape(B, T, D)
```

```python
import functools

import numpy as np
import jax
import jax.numpy as jnp
from jax import lax
from jax.experimental import pallas as pl
from jax.experimental.pallas import tpu as pltpu

F32 = jnp.float32
BF16 = jnp.bfloat16

D_MODEL = 1024
HEAD_DIM = 64
HEAD_SHIFT = 6
RET_HEADS = 8
NSA_HEADS = 8
NSA_KV_HEADS = 2
NSA_GROUP = NSA_HEADS // NSA_KV_HEADS
D_RET = RET_HEADS * HEAD_DIM
D_NSA = NSA_HEADS * HEAD_DIM
D_KV = NSA_KV_HEADS * HEAD_DIM
N_BRANCH = 3
RET_CHUNK = 128
ROPE_THETA = 10000.0
CMP_BLOCK = 32
CMP_STRIDE = 16
CMP_HIDDEN = 256
SLC_BLOCK = 64
SLC_SHIFT = 6
SLC_TOPK = 16
WIN_SIZE = 512
EPS = 1e-6
NEG = -1e30
FORCE_BONUS = 1e4
QK_SCALE = HEAD_DIM ** -0.5
LOG2E = 1.4426950408889634
V_ROWS = HEAD_DIM + 16

LANES = 128
SUBLANES = 8
PAIR = 2 * HEAD_DIM
VMEM_LIMIT = 48 * 1024 * 1024

PROJ_TM = 512
RET_TC = 1024
NSA_TQ = 128
NSA_QT = 2
SLC_TK = 512
WIN_KEYS = WIN_SIZE + NSA_TQ
OUT_TM = 2048

NT_DIMS = (((1,), (1,)), ((), ()))
TN_DIMS = (((0,), (0,)), ((), ()))


def _sigmoid(x):
    return 1.0 / (1.0 + jnp.exp(-x))


def _tile4(a):
    return jnp.concatenate([a, a, a, a], axis=1)


def _proj_kernel(steps_per_batch, x_ref, nw_ref, w_ret_ref, w_ng_ref, w_ckv_ref, w_skwk_ref, wt_q_ref, wt_v_ref,
                 wt_g_ref, knw_ref, ret_ref, ng_ref, xc_ref, ks_ref, kw_ref, nqt_ref, vt_ref, glt_ref, ckv_scr):
    x = x_ref[...]
    ms = jnp.mean(x * x, axis=-1, keepdims=True)
    h = (x * lax.rsqrt(ms + EPS) * nw_ref[...]).astype(BF16)
    ret_ref[...] = jnp.dot(h, w_ret_ref[...], preferred_element_type=F32)
    ng_ref[...] = jnp.dot(h, w_ng_ref[...], preferred_element_type=F32)

    ckv = jnp.dot(h, w_ckv_ref[...], preferred_element_type=F32)
    for half in range(2 * D_KV // LANES):
        ckv_scr[half] = ckv[:, half * LANES:(half + 1) * LANES]
    for l in range(CMP_STRIDE):
        for half in range(2 * D_KV // LANES):
            rows = ckv_scr[half, pl.ds(l, PROJ_TM // CMP_STRIDE, stride=CMP_STRIDE), :]
            for s in range(LANES // HEAD_DIM):
                xc_ref[0, half * (LANES // HEAD_DIM) + s, :, l * HEAD_DIM:(l + 1) * HEAD_DIM] = (
                    rows[:, s * HEAD_DIM:(s + 1) * HEAD_DIM])

    skwk = jnp.dot(h, w_skwk_ref[...], preferred_element_type=F32)
    lane = lax.broadcasted_iota(jnp.int32, (1, PAIR), 1)
    head0 = lane < HEAD_DIM
    m0 = jnp.where(head0, 1.0, 0.0)
    m1 = 1.0 - m0
    t_start = (pl.program_id(0) % steps_per_batch) * PROJ_TM
    tok = t_start + lax.broadcasted_iota(jnp.int32, (PROJ_TM, PAIR), 0)
    col = lax.broadcasted_iota(jnp.int32, (PROJ_TM, PAIR), 1)
    indicator = jnp.where((tok >> SLC_SHIFT) == col - HEAD_DIM, 1.0, 0.0)

    def pair_normed(t, w):
        t2 = t * t
        ms0 = jnp.sum(t2 * m0, axis=-1, keepdims=True)
        ms1 = jnp.sum(t2 * m1, axis=-1, keepdims=True)
        return t * lax.rsqrt(jnp.where(head0, ms0, ms1) * (1.0 / HEAD_DIM) + EPS) * w

    ns = pair_normed(skwk[:, :PAIR], knw_ref[0:1, :])
    nwin = pair_normed(skwk[:, PAIR:], knw_ref[1:2, :])
    for g in range(NSA_KV_HEADS):
        s_g = ns if g == 0 else pltpu.roll(ns, HEAD_DIM, 1)
        w_g = nwin if g == 0 else pltpu.roll(nwin, HEAD_DIM, 1)
        ks_ref[0, g] = jnp.where(head0, s_g, indicator).astype(BF16)
        kw_ref[0, g] = jnp.where(head0, w_g, 0.0).astype(BF16)

    qt = lax.dot_general(wt_q_ref[...], h, NT_DIMS, preferred_element_type=F32)
    vt = lax.dot_general(wt_v_ref[...], h, NT_DIMS, preferred_element_type=F32)
    gt = lax.dot_general(wt_g_ref[...], h, NT_DIMS, preferred_element_type=F32)
    pad_row = lax.broadcasted_iota(jnp.int32, (V_ROWS - HEAD_DIM, LANES), 0)
    ones_pad = jnp.where(pad_row == 0, 1.0, 0.0).astype(BF16)
    for j in range(PROJ_TM // LANES):
        sl = slice(j * LANES, (j + 1) * LANES)
        nqt_ref[0, j] = qt[:, sl]
        for blk in range(2 * NSA_KV_HEADS):
            vt_ref[0, j, blk * V_ROWS:blk * V_ROWS + HEAD_DIM, :] = (
                vt[blk * HEAD_DIM:(blk + 1) * HEAD_DIM, sl].astype(BF16))
            vt_ref[0, j, blk * V_ROWS + HEAD_DIM:(blk + 1) * V_ROWS, :] = ones_pad
        glt_ref[0, j] = gt[:, sl]


def _proj(x2, nw, w_ret, w_ng, w_ckv, w_skwk, wt_q, wt_v, wt_g, knw, B, T):
    N = B * T
    tpb = T // PROJ_TM
    sub = PROJ_TM // LANES
    nt = T // LANES
    const = lambda i: (0, 0)
    row = lambda i: (i, 0)
    trn = lambda i: (i // tpb, i % tpb, 0, 0)
    tokm = lambda i: (i // tpb, 0, i % tpb, 0)
    return pl.pallas_call(
        functools.partial(_proj_kernel, tpb),
        grid=(N // PROJ_TM,),
        in_specs=[
            pl.BlockSpec((PROJ_TM, D_MODEL), row),
            pl.BlockSpec((1, D_MODEL), const),
            pl.BlockSpec(w_ret.shape, const),
            pl.BlockSpec(w_ng.shape, const),
            pl.BlockSpec(w_ckv.shape, const),
            pl.BlockSpec(w_skwk.shape, const),
            pl.BlockSpec(wt_q.shape, const),
            pl.BlockSpec(wt_v.shape, const),
            pl.BlockSpec(wt_g.shape, const),
            pl.BlockSpec(knw.shape, const),
        ],
        out_specs=[
            pl.BlockSpec((PROJ_TM, 4 * D_RET), row),
            pl.BlockSpec((PROJ_TM, D_NSA), row),
            pl.BlockSpec((1, 2 * NSA_KV_HEADS, PROJ_TM // CMP_STRIDE, CMP_STRIDE * HEAD_DIM), tokm),
            pl.BlockSpec((1, NSA_KV_HEADS, PROJ_TM, PAIR), tokm),
            pl.BlockSpec((1, NSA_KV_HEADS, PROJ_TM, PAIR), tokm),
            pl.BlockSpec((1, sub, D_NSA, LANES), trn),
            pl.BlockSpec((1, sub, 2 * NSA_KV_HEADS * V_ROWS, LANES), trn),
            pl.BlockSpec((1, sub, 32, LANES), trn),
        ],
        out_shape=[
            jax.ShapeDtypeStruct((N, 4 * D_RET), F32),
            jax.ShapeDtypeStruct((N, D_NSA), F32),
            jax.ShapeDtypeStruct((B, 2 * NSA_KV_HEADS, T // CMP_STRIDE, CMP_STRIDE * HEAD_DIM), F32),
            jax.ShapeDtypeStruct((B, NSA_KV_HEADS, T, PAIR), BF16),
            jax.ShapeDtypeStruct((B, NSA_KV_HEADS, T, PAIR), BF16),
            jax.ShapeDtypeStruct((B, nt, D_NSA, LANES), F32),
            jax.ShapeDtypeStruct((B, nt, 2 * NSA_KV_HEADS * V_ROWS, LANES), BF16),
            jax.ShapeDtypeStruct((B, nt, 32, LANES), F32),
        ],
        scratch_shapes=[pltpu.VMEM((2 * D_KV // LANES, PROJ_TM, LANES), F32)],
        compiler_params=pltpu.CompilerParams(dimension_semantics=("parallel",), vmem_limit_bytes=VMEM_LIMIT),
        name="proj",
    )(x2, nw, w_ret, w_ng, w_ckv, w_skwk, wt_q, wt_v, wt_g, knw)


def _ret_kernel(qkvg_ref, cos_ref, sin_ref, dec_ref, zeta_ref, xi_ref, gch_ref, nw_ref, o_ref, state_ref):
    @pl.when(pl.program_id(1) == 0)
    def _():
        state_ref[...] = jnp.zeros_like(state_ref)

    lane = lax.broadcasted_iota(jnp.int32, (1, PAIR), 1)
    q_head = (lane >> (HEAD_SHIFT - 1)) & 1
    v_head = lane >> HEAD_SHIFT
    q_mask = [jnp.where(q_head == h, 1.0, 0.0).astype(BF16) for h in (0, 1)]
    v_mask = [jnp.where(v_head == h, 1.0, 0.0).astype(BF16) for h in (0, 1)]
    row_qh = (lax.broadcasted_iota(jnp.int32, (PAIR, PAIR), 0) >> (HEAD_SHIFT - 1)) & 1
    row_vh = lax.broadcasted_iota(jnp.int32, (PAIR, PAIR), 0) >> HEAD_SHIFT
    col_vh = lax.broadcasted_iota(jnp.int32, (PAIR, PAIR), 1) >> HEAD_SHIFT
    same_head_kv = jnp.where(row_qh == col_vh, 1.0, 0.0)
    head_mean = jnp.where(row_vh == col_vh, 1.0 / HEAD_DIM, 0.0).astype(BF16)

    pairs = range(RET_HEADS // 2)
    mean2 = jnp.concatenate([head_mean, head_mean], axis=0)
    tok0 = pl.program_id(1) * RET_TC
    for c in range(RET_TC // RET_CHUNK):
        sl = pl.ds(c * RET_CHUNK, RET_CHUNK)
        pos = pl.ds(pl.multiple_of(tok0 + c * RET_CHUNK, RET_CHUNK), RET_CHUNK)
        cos = cos_ref[pos, :]
        sin = sin_ref[pos, :]
        cols = [slice(p * PAIR, (p + 1) * PAIR) for p in pairs]
        qb, kb, vb, vzb = [], [], [], []
        for p in pairs:
            q = qkvg_ref[0, sl, pl.ds(p * PAIR, PAIR)]
            k = qkvg_ref[0, sl, pl.ds(D_RET + p * PAIR, PAIR)]
            v = qkvg_ref[0, sl, pl.ds(2 * D_RET + p * PAIR, PAIR)]
            qb.append((q * cos + pltpu.roll(q, HEAD_DIM, 1) * sin).astype(BF16))
            kb.append(((k * cos + pltpu.roll(k, HEAD_DIM, 1) * sin) * QK_SCALE).astype(BF16))
            vb.append(v.astype(BF16))
            vzb.append((v * zeta_ref[p]).astype(BF16))
        states = [state_ref[p] for p in pairs]
        kk = [jnp.concatenate([kb[p] * q_mask[h] for h in (0, 1)], axis=0) for p in pairs]
        s = [lax.dot_general(qb[p], kk[p], NT_DIMS, preferred_element_type=F32) for p in pairs]
        o_cross = [jnp.dot(qb[p], states[p].astype(BF16), preferred_element_type=F32) for p in pairs]
        kv = [lax.dot_general(kb[p], vzb[p], TN_DIMS, preferred_element_type=F32) for p in pairs]
        sb = [(s[p] * dec_ref[p]).astype(BF16) for p in pairs]
        vv = [jnp.concatenate([vb[p] * v_mask[h] for h in (0, 1)], axis=0) for p in pairs]
        o = [jnp.dot(sb[p], vv[p], preferred_element_type=F32) + o_cross[p] * xi_ref[p] for p in pairs]
        for p in pairs:
            state_ref[p] = states[p] * gch_ref[p] + kv[p] * same_head_kv
        o2 = [o[p] * o[p] for p in pairs]
        o2_hi = [o2[p].astype(BF16) for p in pairs]
        o2_hl = [jnp.concatenate([o2_hi[p], (o2[p] - o2_hi[p].astype(F32)).astype(BF16)], axis=1) for p in pairs]
        ms = [jnp.dot(o2_hl[p], mean2, preferred_element_type=F32) for p in pairs]
        for p in pairs:
            g = qkvg_ref[0, sl, pl.ds(3 * D_RET + p * PAIR, PAIR)]
            y = o[p] * lax.rsqrt(ms[p] + EPS) * nw_ref[p]
            o_ref[0, sl, cols[p]] = (y * (g * _sigmoid(g))).astype(BF16)


def _retention(ret3, cos, sin, dec, zeta, xi, gch, nw, B, T):
    npair = RET_HEADS // 2
    whole = lambda a: pl.BlockSpec(a.shape, lambda b, i: (0,) * a.ndim)
    return pl.pallas_call(
        _ret_kernel,
        grid=(B, T // RET_TC),
        in_specs=[pl.BlockSpec((1, RET_TC, 4 * D_RET), lambda b, i: (b, i, 0)), whole(cos), whole(sin), whole(dec),
                  whole(zeta), whole(xi), whole(gch), whole(nw)],
        out_specs=pl.BlockSpec((1, RET_TC, D_RET), lambda b, i: (b, i, 0)),
        out_shape=jax.ShapeDtypeStruct((B, T, D_RET), BF16),
        scratch_shapes=[pltpu.VMEM((npair, PAIR, PAIR), F32)],
        compiler_params=pltpu.CompilerParams(dimension_semantics=("parallel", "arbitrary"),
                                             vmem_limit_bytes=VMEM_LIMIT),
        name="retention",
    )(ret3, cos, sin, dec, zeta, xi, gch, nw)


def _cmp_kernel(x_ref, pos_ref, w1_ref, w2_ref, w2t_ref, knw_ref, o_ref, ot_ref):
    is_key = pl.program_id(1) == 0
    half = CMP_STRIDE * HEAD_DIM
    for g in range(NSA_KV_HEADS):
        x = x_ref[0, g]
        a = jnp.dot((x + pos_ref[0, 0:1, :]).astype(BF16), w1_ref[0, :half, :], preferred_element_type=F32)
        b = jnp.dot((x + pos_ref[0, 1:2, :]).astype(BF16), w1_ref[0, half:, :], preferred_element_type=F32)
        hid = a + pltpu.roll(b, b.shape[0] - 1, 0)
        hid = (hid * _sigmoid(hid)).astype(BF16)
        out = jnp.dot(hid, w2_ref[0], preferred_element_type=F32)
        ms = jnp.mean(out * out, axis=-1, keepdims=True)
        normed = out * lax.rsqrt(ms + EPS) * knw_ref[...]
        o_ref[0, 0, g] = jnp.where(is_key, normed, out)
        ot_ref[0, 0, g] = lax.dot_general(w2t_ref[0], hid, NT_DIMS, preferred_element_type=F32)


def _compress(xc, pos, w1, w2, w2t, knw, B, ncb):
    return pl.pallas_call(
        _cmp_kernel,
        grid=(B, 2),
        in_specs=[
            pl.BlockSpec((1, NSA_KV_HEADS, ncb, CMP_STRIDE * HEAD_DIM), lambda b, s: (b, s, 0, 0)),
            pl.BlockSpec((1, 2, CMP_STRIDE * HEAD_DIM), lambda b, s: (s, 0, 0)),
            pl.BlockSpec((1, CMP_BLOCK * HEAD_DIM, CMP_HIDDEN), lambda b, s: (s, 0, 0)),
            pl.BlockSpec((1, CMP_HIDDEN, HEAD_DIM), lambda b, s: (s, 0, 0)),
            pl.BlockSpec((1, HEAD_DIM, CMP_HIDDEN), lambda b, s: (s, 0, 0)),
            pl.BlockSpec((1, HEAD_DIM), lambda b, s: (0, 0)),
        ],
        out_specs=[
            pl.BlockSpec((1, 1, NSA_KV_HEADS, ncb, HEAD_DIM), lambda b, s: (b, s, 0, 0, 0)),
            pl.BlockSpec((1, 1, NSA_KV_HEADS, HEAD_DIM, ncb), lambda b, s: (b, s, 0, 0, 0)),
        ],
        out_shape=[
            jax.ShapeDtypeStruct((B, 2, NSA_KV_HEADS, ncb, HEAD_DIM), F32),
            jax.ShapeDtypeStruct((B, 2, NSA_KV_HEADS, HEAD_DIM, ncb), F32),
        ],
        compiler_params=pltpu.CompilerParams(dimension_semantics=("parallel", "parallel"),
                                             vmem_limit_bytes=VMEM_LIMIT),
        name="compress",
    )(xc, pos, w1, w2, w2t, knw)


def _nsa_kernel(qt_ref, qtn_ref, glt_ref, ng_ref, ks_ref, kw_ref, vst_ref, vwt_ref, kc_ref, vct_ref, mt_ref, qnw_ref,
                bg_ref, o_ref, qp_ref, oc_ref, sa_ref, sb_ref, mxa_ref, mxb_ref, sw_ref, ow_ref, m_ref, acc_ref):
    qi = pl.program_id(1)
    t0 = qi * (NSA_QT * NSA_TQ)
    tiles = range(NSA_QT)
    chains = [(qt, g) for qt in tiles for g in range(NSA_KV_HEADS)]
    gq = NSA_GROUP * HEAD_DIM
    gg = 16
    ncb = kc_ref.shape[3]
    n_slc = mt_ref.shape[0]

    def select_masks(base_t0):
        n_idx = lax.broadcasted_iota(jnp.int32, (ncb, NSA_TQ), 0)
        jb = lax.broadcasted_iota(jnp.int32, (n_slc, NSA_TQ), 0)
        out = []
        for qt in tiles:
            tok = base_t0 + qt * NSA_TQ
            tok_c = tok + lax.broadcasted_iota(jnp.int32, (ncb, NSA_TQ), 1)
            cbias = _tile4(jnp.where((n_idx * CMP_STRIDE + (CMP_BLOCK - 1)) <= tok_c, 0.0, NEG))
            tok_row = tok + lax.broadcasted_iota(jnp.int32, (1, NSA_TQ), 1)
            has_block = _tile4(jnp.where(tok_row >= CMP_BLOCK - 1, 1.0, 0.0))
            tok_s = tok + lax.broadcasted_iota(jnp.int32, (n_slc, NSA_TQ), 1)
            valid_s = jb * SLC_BLOCK <= tok_s
            force = (jb == (tok_s >> SLC_SHIFT)) | (jb == 0)
            out.append((cbias, has_block, valid_s, force))
        return out

    def select_scores(src_ref, slot, masks, c, rows=ncb):
        qt, g = chains[c]
        cols = []
        for r in range(NSA_GROUP):
            q = src_ref[0, qt, g * gq + r * HEAD_DIM:g * gq + (r + 1) * HEAD_DIM, :]
            ms = jnp.mean(q * q, axis=0, keepdims=True)
            cols.append(q * lax.rsqrt(ms + EPS) * qnw_ref[...] * (QK_SCALE * LOG2E))
        qs = jnp.concatenate(cols, axis=1).astype(BF16)
        qp_ref[slot, c, 0:HEAD_DIM, :] = qs
        return (jnp.dot(kc_ref[0, 0, g, 0:rows, :].astype(BF16), qs, preferred_element_type=F32)
                + masks[qt][0][0:rows, :])

    def select_probs(sc, slot, masks, c, want_scores=True):
        qt, g = chains[c]
        _, has_block, valid_s, force = masks[qt]
        rows = sc.shape[0]
        mc = jnp.max(sc, axis=0, keepdims=True)
        ec = jnp.exp2(sc - mc)
        lc = jnp.sum(ec, axis=0, keepdims=True)
        p = ec * (has_block / lc)
        oc_ref[slot, c] = jnp.dot(vct_ref[0, 0, g, :, 0:rows].astype(BF16), p.astype(BF16),
                                  preferred_element_type=F32)
        if not want_scores:
            return None
        ps = p[:, 0:NSA_TQ]
        for r in range(1, NSA_GROUP):
            ps = ps + p[:, r * NSA_TQ:(r + 1) * NSA_TQ]
        ps_hi = ps.astype(BF16)
        ps_lo = (ps - ps_hi.astype(F32)).astype(BF16)
        imp = (jnp.dot(mt_ref[:, 0:rows], ps_hi, preferred_element_type=F32)
               + jnp.dot(mt_ref[:, 0:rows], ps_lo, preferred_element_type=F32))
        return jnp.where(valid_s, jnp.where(force, imp + FORCE_BONUS, imp), NEG)

    def select_rank(scores, valid_s, slot, n_live):
        if n_live <= SLC_TOPK:
            for c, (qt, g) in enumerate(chains):
                qp_ref[slot, c, HEAD_DIM:2 * HEAD_DIM, :] = _tile4(jnp.where(valid_s[qt], 0.0, NEG).astype(BF16))
            return
        sub = lax.broadcasted_iota(jnp.int32, (SUBLANES, NSA_TQ), 0)
        for c, (qt, g) in enumerate(chains):
            score = scores[c]
            blocks = [score[v * SUBLANES:(v + 1) * SUBLANES, :] for v in range(n_live // SUBLANES)]
            ranks = [jnp.zeros((SUBLANES, NSA_TQ), F32) for _ in blocks]
            for i in range(n_live):
                row = score[i:i + 1, :]
                for v, blk in enumerate(blocks):
                    if v * SUBLANES > i:
                        beats = row >= blk
                    elif (v + 1) * SUBLANES <= i:
                        beats = row > blk
                    else:
                        beats = (row > blk) | ((row >= blk) & (sub > i - v * SUBLANES))
                    ranks[v] = ranks[v] + jnp.where(beats, 1.0, 0.0)
            dead = [jnp.full((SUBLANES, NSA_TQ), float(n_slc), F32)] * ((n_slc - n_live) // SUBLANES)
            rank = jnp.concatenate(ranks + dead, axis=0)
            sel = (rank < float(SLC_TOPK)) & valid_s[qt]
            qp_ref[slot, c, HEAD_DIM:2 * HEAD_DIM, :] = _tile4(jnp.where(sel, 0.0, NEG).astype(BF16))

    step_tokens = NSA_QT * NSA_TQ

    def live_blocks(step):
        return ((step + 1) * step_tokens - 1) // SLC_BLOCK + 1

    all_chains = list(range(len(chains)))

    @pl.when(qi == 0)
    def _():
        masks0 = select_masks(0)
        rows0 = min(ncb, -(-((step_tokens - CMP_BLOCK) // CMP_STRIDE + 1) // 16) * 16)
        assert live_blocks(0) <= SLC_TOPK
        for c in all_chains:
            select_probs(select_scores(qt_ref, 0, masks0, c, rows0), 0, masks0, c, want_scores=False)
        select_rank(None, [m[2] for m in masks0], 0, live_blocks(0))

    cur = qi % 2
    nxt = 1 - cur
    t_next = t0 + NSA_QT * NSA_TQ

    kt0, ks0 = [], []
    for qt in tiles:
        kt0.append(jnp.maximum(qi * NSA_QT + qt - WIN_SIZE // NSA_TQ, 0))
        ks0.append(pl.multiple_of(kt0[qt] * NSA_TQ, NSA_TQ))

    def window_masks():
        c_minus_r = (lax.broadcasted_iota(jnp.int32, (WIN_KEYS, NSA_TQ), 1)
                     - lax.broadcasted_iota(jnp.int32, (WIN_KEYS, NSA_TQ), 0))
        out = []
        for qt in tiles:
            delta = (t0 + qt * NSA_TQ - ks0[qt]) + c_minus_r
            in_window = lax.bitcast_convert_type(delta, jnp.uint32) < WIN_SIZE
            out.append(_tile4(jnp.where(in_window, 0.0, NEG)))
        return out

    def win_scores(c, wbias):
        qt, g = chains[c]
        sw = jnp.dot(kw_ref[0, g, pl.ds(ks0[qt], WIN_KEYS), 0:HEAD_DIM], qp_ref[cur, c, 0:HEAD_DIM, :],
                     preferred_element_type=F32)
        if wbias is None:
            r_b = lax.broadcasted_iota(jnp.int32, (NSA_TQ, NSA_TQ), 0)
            c_b = lax.broadcasted_iota(jnp.int32, (NSA_TQ, NSA_TQ), 1)
            sw = jnp.concatenate([sw[0:NSA_TQ] + _tile4(jnp.where(r_b > c_b, 0.0, NEG)),
                                  sw[NSA_TQ:WIN_SIZE],
                                  sw[WIN_SIZE:] + _tile4(jnp.where(r_b <= c_b, 0.0, NEG))], axis=0)
        else:
            sw = sw + wbias[qt]
        sw_ref[c] = sw
        return jnp.max(sw, axis=0, keepdims=True)

    def win_attend(c, mw):
        qt, g = chains[c]
        ewb = jnp.exp2(sw_ref[c] - mw).astype(BF16)
        vwt = jnp.concatenate([vwt_ref[0, kt0[qt] + j, g * V_ROWS:(g + 1) * V_ROWS, :]
                               for j in range(WIN_KEYS // LANES)], axis=1)
        ow_aug = jnp.dot(vwt, ewb, preferred_element_type=F32)
        return ow_aug[0:HEAD_DIM, :] * (1.0 / ow_aug[HEAD_DIM:HEAD_DIM + 1, :])

    vt_per_tile = SLC_TK // LANES

    def slc_scores(j, dst_ref, mx_ref, which=all_chains):
        kst = pl.multiple_of(j * SLC_TK, SLC_TK)
        for c in which:
            qt, g = chains[c]
            s = jnp.dot(ks_ref[0, g, pl.ds(kst, SLC_TK), :], qp_ref[cur, c], preferred_element_type=F32)
            dst_ref[c] = s
            mx_ref[c] = jnp.max(s, axis=0, keepdims=True)

    def slc_update(j, src_ref, mx_ref, causal, which=all_chains, visible=SLC_TK):
        if causal:
            band = _tile4(jnp.where(lax.broadcasted_iota(jnp.int32, (NSA_TQ, NSA_TQ), 0)
                                    <= lax.broadcasted_iota(jnp.int32, (NSA_TQ, NSA_TQ), 1), 0.0, NEG))
        for c in which:
            qt, g = chains[c]
            if causal:
                r0 = visible - step_tokens + qt * NSA_TQ
                rows = r0 + NSA_TQ
                s_band = src_ref[c, r0:rows, :] + band
                s = jnp.concatenate([src_ref[c, 0:r0, :], s_band], axis=0) if r0 > 0 else s_band
                tile_max = jnp.max(s, axis=0, keepdims=True)
            else:
                rows = SLC_TK
                s = src_ref[c]
                tile_max = mx_ref[c]
            m_old = m_ref[c]
            m_new = jnp.maximum(m_old, tile_max)
            alpha = jnp.exp2(m_old - m_new)
            eb = jnp.exp2(s - m_new).astype(BF16)
            vt = jnp.concatenate([vst_ref[0, j * vt_per_tile + jj, g * V_ROWS:(g + 1) * V_ROWS, :]
                                  for jj in range(rows // LANES)], axis=1)
            acc_ref[c] = alpha * acc_ref[c] + jnp.dot(vt, eb, preferred_element_type=F32)
            m_ref[c] = m_new

    rank_bounds = list(range(SLC_TOPK, n_slc + 1, SUBLANES))

    def before_loop(select_next, rows, steps):
        masks = select_masks(t_next) if select_next else None
        past_start = all(q * step_tokens >= WIN_SIZE for q in steps)
        wbias = None if past_start else window_masks()

        def matmul_stage(c):
            mw = win_scores(c, wbias)
            sc = select_scores(qtn_ref, nxt, masks, c, rows) if select_next else None
            slc_scores(0, sa_ref, mxa_ref, [c])
            return sc, mw

        sel_scores = []
        staged = matmul_stage(0)
        for c in all_chains:
            staged_next = matmul_stage(c + 1) if c + 1 < len(chains) else None
            if select_next:
                sel_scores.append(select_probs(staged[0], nxt, masks, c))
            ow_ref[c] = win_attend(c, staged[1])
            staged = staged_next
        if not select_next:
            return
        sel_valid = [m[2] for m in masks]
        need = live_blocks(qi + 1)
        used = {min(b for b in rank_bounds if b >= min(live_blocks(q + 1), n_slc)) for q in steps}
        for lo, hi in zip([0] + rank_bounds[:-1], rank_bounds):
            if hi in used:
                in_range = (need > lo) if hi == rank_bounds[-1] else ((need > lo) & (need <= hi))
                pl.when(in_range)(functools.partial(select_rank, sel_scores, sel_valid, nxt, hi))

    n_steps = ks_ref.shape[2] // step_tokens
    row_options = list(range(SLC_BLOCK, ncb + 1, SLC_BLOCK))
    by_rows = {}
    for q in range(n_steps - 1):
        visible_blocks = ((q + 2) * step_tokens - CMP_BLOCK) // CMP_STRIDE + 1
        by_rows.setdefault(min(r for r in row_options if r >= min(visible_blocks, ncb)), []).append(q)
    for rows, steps in by_rows.items():
        pl.when((qi >= steps[0]) & (qi <= steps[-1]))(functools.partial(before_loop, True, rows, steps))
    pl.when(qi + 1 >= n_steps)(functools.partial(before_loop, False, ncb, [n_steps - 1]))

    m_ref[...] = jnp.full(m_ref.shape, NEG, F32)
    acc_ref[...] = jnp.zeros(acc_ref.shape, F32)
    n_full = t0 // SLC_TK

    def pair(jj, carry):
        j = 2 * jj
        for c in all_chains:
            slc_scores(j + 1, sb_ref, mxb_ref, [c])
            slc_update(j, sa_ref, mxa_ref, False, [c])
        for c in all_chains:
            slc_scores(j + 2, sa_ref, mxa_ref, [c])
            slc_update(j + 1, sb_ref, mxb_ref, False, [c])
        return carry

    lax.fori_loop(0, n_full // 2, pair, 0)

    visible = t0 - n_full * SLC_TK + step_tokens

    def tail(odd, rows):
        if odd:
            for c in all_chains:
                slc_scores(n_full, sb_ref, mxb_ref, [c])
                slc_update(n_full - 1, sa_ref, mxa_ref, False, [c])
            slc_update(n_full, sb_ref, mxb_ref, True, visible=rows)
        else:
            slc_update(n_full, sa_ref, mxa_ref, True, visible=rows)

    for rows in range(step_tokens, SLC_TK + 1, step_tokens):
        for odd in (False, True):
            parity = (n_full % 2 == 1) if odd else (n_full % 2 == 0)
            pl.when((visible == rows) & parity)(functools.partial(tail, odd, rows))

    for qt in tiles:
        gates = _sigmoid(glt_ref[0, qt] + bg_ref[...])
        outs = []
        for g in range(NSA_KV_HEADS):
            c = qt * NSA_KV_HEADS + g
            os_t = acc_ref[c, 0:HEAD_DIM, :] * (1.0 / acc_ref[c, HEAD_DIM:HEAD_DIM + 1, :])
            oc_t = oc_ref[cur, c]
            ow_t = ow_ref[c]
            for r in range(NSA_GROUP):
                sl = slice(r * NSA_TQ, (r + 1) * NSA_TQ)
                row = g * gg + r
                outs.append(gates[row:row + 1, :] * oc_t[:, sl]
                            + gates[row + NSA_GROUP:row + NSA_GROUP + 1, :] * os_t[:, sl]
                            + gates[row + 2 * NSA_GROUP:row + 2 * NSA_GROUP + 1, :] * ow_t[:, sl])
        o_tok = jnp.concatenate(outs, axis=0).T
        rows = pl.ds(qt * NSA_TQ, NSA_TQ)
        ng = ng_ref[0, rows, :]
        o_ref[0, rows, :] = (o_tok * (ng * _sigmoid(ng))).astype(BF16)


def _nsa(nqt, glt, ng3, ks, kw, vt, kc, vct, mt, qnw, bg, B, T):
    nt = T // LANES
    ncb = kc.shape[3]
    G = NSA_KV_HEADS
    NC = NSA_QT * G
    ncols = NSA_GROUP * NSA_TQ
    steps = T // (NSA_QT * NSA_TQ)
    return pl.pallas_call(
        _nsa_kernel,
        grid=(B, steps),
        in_specs=[
            pl.BlockSpec((1, NSA_QT, D_NSA, LANES), lambda b, i: (b, 0, 0, 0)),
            pl.BlockSpec((1, NSA_QT, D_NSA, LANES), lambda b, i: (b, jnp.minimum(i + 1, steps - 1), 0, 0)),
            pl.BlockSpec((1, NSA_QT, 16 * G, LANES), lambda b, i: (b, i, 0, 0)),
            pl.BlockSpec((1, NSA_QT * NSA_TQ, D_NSA), lambda b, i: (b, i, 0)),
            pl.BlockSpec((1, G, T, PAIR), lambda b, i: (b, 0, 0, 0)),
            pl.BlockSpec((1, G, T, PAIR), lambda b, i: (b, 0, 0, 0)),
            pl.BlockSpec((1, nt, G * V_ROWS, LANES), lambda b, i: (b, 0, 0, 0)),
            pl.BlockSpec((1, nt, G * V_ROWS, LANES), lambda b, i: (b, 0, 1, 0)),
            pl.BlockSpec((1, 1, G, ncb, HEAD_DIM), lambda b, i: (b, 0, 0, 0, 0)),
            pl.BlockSpec((1, 1, G, HEAD_DIM, ncb), lambda b, i: (b, 1, 0, 0, 0)),
            pl.BlockSpec(mt.shape, lambda b, i: (0, 0)),
            pl.BlockSpec((HEAD_DIM, 1), lambda b, i: (0, 0)),
            pl.BlockSpec((16 * G, 1), lambda b, i: (0, 0)),
        ],
        out_specs=pl.BlockSpec((1, NSA_QT * NSA_TQ, D_NSA), lambda b, i: (b, i, 0)),
        out_shape=jax.ShapeDtypeStruct((B, T, D_NSA), BF16),
        scratch_shapes=[
            pltpu.VMEM((2, NC, 2 * HEAD_DIM, ncols), BF16),
            pltpu.VMEM((2, NC, HEAD_DIM, ncols), F32),
            pltpu.VMEM((NC, SLC_TK, ncols), F32),
            pltpu.VMEM((NC, SLC_TK, ncols), F32),
            pltpu.VMEM((NC, 1, ncols), F32),
            pltpu.VMEM((NC, 1, ncols), F32),
            pltpu.VMEM((NC, WIN_KEYS, ncols), F32),
            pltpu.VMEM((NC, HEAD_DIM, ncols), F32),
            pltpu.VMEM((NC, 1, ncols), F32),
            pltpu.VMEM((NC, V_ROWS, ncols), F32),
        ],
        compiler_params=pltpu.CompilerParams(dimension_semantics=("parallel", "arbitrary"),
                                             vmem_limit_bytes=VMEM_LIMIT),
        name="nsa",
    )(nqt, nqt, glt, ng3, ks, kw, vt, vt, kc, vct, mt, qnw, bg)


def _out_kernel(x_ref, yr_ref, yn_ref, wr_ref, wn_ref, o_ref):
    o_ref[...] = (x_ref[...]
                  + jnp.dot(yr_ref[...], wr_ref[...], preferred_element_type=F32)
                  + jnp.dot(yn_ref[...], wn_ref[...], preferred_element_type=F32))


def _outproj(x2, yr, yn, wr, wn):
    N = x2.shape[0]
    row = lambda i: (i, 0)
    const = lambda i: (0, 0)
    return pl.pallas_call(
        _out_kernel,
        grid=(N // OUT_TM,),
        in_specs=[pl.BlockSpec((OUT_TM, D_MODEL), row), pl.BlockSpec((OUT_TM, D_RET), row),
                  pl.BlockSpec((OUT_TM, D_NSA), row), pl.BlockSpec(wr.shape, const), pl.BlockSpec(wn.shape, const)],
        out_specs=pl.BlockSpec((OUT_TM, D_MODEL), row),
        out_shape=jax.ShapeDtypeStruct((N, D_MODEL), F32),
        compiler_params=pltpu.CompilerParams(dimension_semantics=("parallel",), vmem_limit_bytes=VMEM_LIMIT),
        name="outproj",
    )(x2, yr, yn, wr, wn)


@functools.lru_cache(maxsize=None)
def _tables(T):
    half = HEAD_DIM // 2
    inv = ROPE_THETA ** (-np.arange(half, dtype=np.float64) / half)
    ang = np.arange(T, dtype=np.float64)[:, None] * inv[None, :]
    cos = np.concatenate([np.cos(ang)] * 4, axis=1).astype(np.float32)
    sin = np.concatenate([-np.sin(ang), -np.sin(ang), np.sin(ang), np.sin(ang)], axis=1).astype(np.float32)

    C = RET_CHUNK
    log_g = np.log1p(-np.exp2(-5.0 - np.arange(RET_HEADS, dtype=np.float64)))
    pos = np.arange(C, dtype=np.float64)
    diff = pos[:, None] - pos[None, :]
    decay = np.where(diff >= 0, np.exp(log_g[:, None, None] * np.maximum(diff, 0.0)), 0.0)
    zeta = np.exp(log_g[:, None] * (C - 1.0 - pos))
    xi = np.exp(log_g[:, None] * (pos + 1.0))
    g_chunk = np.exp(log_g * C)
    npair = RET_HEADS // 2

    def pair_lanes(a):
        return np.repeat(a.reshape(npair, 2, C).transpose(0, 2, 1), HEAD_DIM, axis=2).astype(np.float32)

    dec = decay.reshape(npair, 2, C, C).transpose(0, 2, 1, 3).reshape(npair, C, 2 * C).astype(np.float32)
    gch = np.repeat(g_chunk.reshape(npair, 1, 2), HEAD_DIM, axis=2).astype(np.float32)

    n_cmp = (T - CMP_BLOCK) // CMP_STRIDE + 1
    ncb = T // CMP_STRIDE
    p = np.arange(n_cmp)[:, None] * CMP_STRIDE + np.arange(CMP_BLOCK)[None, :]
    blk = p // SLC_BLOCK
    M = (blk[:, :, None] == np.arange(T // SLC_BLOCK)[None, None, :]).mean(axis=1)
    mt = np.zeros((T // SLC_BLOCK, ncb), np.float32)
    mt[:, :n_cmp] = M.T
    return cos, sin, dec, pair_lanes(zeta), pair_lanes(xi), gch, mt


def kernel(x, norm_w, w_in, ret_norm_w, q_norm_w, k_norm_cmp, k_norm_slc, k_norm_win, cmp_pos_k, cmp_w1_k, cmp_w2_k,
           cmp_pos_v, cmp_w1_v, cmp_w2_v, b_gate, w_out):
    B, T, D = x.shape
    depth = norm_w.shape[0]
    cos, sin, dec, zeta, xi, gch, mt = _tables(T)
    ncb = T // CMP_STRIDE
    half = CMP_STRIDE * HEAD_DIM
    gate_src = np.zeros((NSA_KV_HEADS, 16), np.int32)
    gate_ok = np.zeros((NSA_KV_HEADS, 16), bool)
    for g in range(NSA_KV_HEADS):
        for br in range(N_BRANCH):
            for r in range(NSA_GROUP):
                gate_src[g, br * NSA_GROUP + r] = br * NSA_HEADS + g * NSA_GROUP + r
                gate_ok[g, br * NSA_GROUP + r] = True
    gate_src = gate_src.reshape(-1)
    gate_ok = gate_ok.reshape(-1)

    x2 = x.reshape(B * T, D)
    for layer in range(depth):
        w = w_in[layer].astype(BF16)
        o_ng = 4 * D_RET + D_NSA
        o_kv = o_ng + D_NSA
        quarter = HEAD_DIM // 2
        pair_perm = np.concatenate([np.arange(quarter), HEAD_DIM + np.arange(quarter),
                                    quarter + np.arange(quarter), HEAD_DIM + quarter + np.arange(quarter)])
        qk_perm = np.concatenate([p * PAIR + pair_perm for p in range(RET_HEADS // 2)])
        w_ret = jnp.concatenate([w[:, :D_RET][:, qk_perm], w[:, D_RET:2 * D_RET][:, qk_perm],
                                 w[:, 2 * D_RET:4 * D_RET]], axis=1)
        wt_q = w[:, 4 * D_RET:o_ng].T
        w_ng = w[:, o_ng:o_kv]
        w_ckv = w[:, o_kv:o_kv + 2 * D_KV]
        w_skwk = jnp.concatenate([w[:, o_kv + 2 * D_KV:o_kv + 3 * D_KV], w[:, o_kv + 4 * D_KV:o_kv + 5 * D_KV]], axis=1)
        wt_v = jnp.concatenate([w[:, o_kv + 3 * D_KV:o_kv + 4 * D_KV], w[:, o_kv + 5 * D_KV:o_kv + 6 * D_KV]], axis=1).T
        w_gl = w[:, o_kv + 6 * D_KV:]
        wt_g = jnp.where(gate_ok[:, None], w_gl.T[gate_src], jnp.zeros((), BF16))
        bg = jnp.where(gate_ok, b_gate[layer][gate_src], 0.0).reshape(-1, 1)

        knw = jnp.stack([jnp.tile(k_norm_slc[layer], 2), jnp.tile(k_norm_win[layer], 2)])
        ret, ng, xc, ks, kw, nqt, vt, glt = _proj(x2, norm_w[layer].reshape(1, D), w_ret, w_ng, w_ckv, w_skwk,
                                                  wt_q, wt_v, wt_g, knw, B, T)

        nw_pair = ret_norm_w[layer].reshape(RET_HEADS // 2, 1, PAIR)
        y_ret = _retention(ret.reshape(B, T, 4 * D_RET), jnp.asarray(cos), jnp.asarray(sin), jnp.asarray(dec),
                           jnp.asarray(zeta), jnp.asarray(xi), jnp.asarray(gch), nw_pair, B, T)

        pos =jnp.stack([cmp_pos_k[layer], cmp_pos_v[layer]]).reshape(2, 2, half)
        w1 = jnp.stack([cmp_w1_k[layer], cmp_w1_v[layer]]).astype(BF16)
        w2 = jnp.stack([cmp_w2_k[layer], cmp_w2_v[layer]]).astype(BF16)
        w2t = jnp.swapaxes(w2, 1, 2)
        kc, vct = _compress(xc, pos, w1, w2, w2t, k_norm_cmp[layer].reshape(1, HEAD_DIM), B, ncb)

        y_nsa = _nsa(nqt, glt, ng.reshape(B, T, D_NSA), ks, kw, vt, kc, vct, jnp.asarray(mt).astype(BF16),
                     q_norm_w[layer].reshape(HEAD_DIM, 1), bg, B, T)

        wo = w_out[layer].astype(BF16)
        x2 = _outproj(x2, y_ret.reshape(B * T, D_RET), y_nsa.reshape(B * T, D_NSA), wo[:D_RET], wo[D_RET:])
    return x2.reshape(B, T, D)
```
